```python
import jax, jax.numpy as jnp
from jax import lax
import numpy as np

D_MODEL = 1024
BATCH = 8
SEQ = 2048
DEPTH = 4

N_BRANCH = 4
MIX_WIDTH = D_MODEL // 2
CONF_KERNEL = 31
POOL_WINDOWS = (2, 4, 8, 16)
POOL_GROUPS = len(POOL_WINDOWS)
POOL_GROUP_WIDTH = MIX_WIDTH // POOL_GROUPS
SC_KERNEL = 3
GMLP_CHUNK = 128
GMLP_GROUPS = 4
GMLP_GROUP_WIDTH = MIX_WIDTH // GMLP_GROUPS
D_FF = 4 * D_MODEL
PLE_DIM = 256
EPS = 1e-6

COLS_A = 2 * MIX_WIDTH
COLS_B = MIX_WIDTH
COLS_C = 3 * MIX_WIDTH
COLS_D = 2 * MIX_WIDTH
COLS_G = N_BRANCH * D_MODEL
COLS_IN = COLS_A + COLS_B + COLS_C + COLS_D + COLS_G
SPLITS = (COLS_A, COLS_A + COLS_B, COLS_A + COLS_B + COLS_C, COLS_A + COLS_B + COLS_C + COLS_D)

kernel_name = "hybrid_conv_pool_shortconv_gmlp_block"


def rms_norm(x, g):
    xf = x.astype(jnp.float32)
    y = xf * lax.rsqrt(jnp.mean(xf * xf, axis=-1, keepdims=True) + EPS)
    return (y * g.astype(jnp.float32)).astype(x.dtype)


def layer_norm(x, g, b):
    xf = x.astype(jnp.float32)
    mu = jnp.mean(xf, axis=-1, keepdims=True)
    var = jnp.mean(jnp.square(xf - mu), axis=-1, keepdims=True)
    y = (xf - mu) * lax.rsqrt(var + EPS)
    return (y * g.astype(jnp.float32) + b.astype(jnp.float32)).astype(x.dtype)


def causal_depthwise_conv(x, w):
    K, C = w.shape
    return lax.conv_general_dilated(
        x, w[:, None, :].astype(x.dtype), window_strides=(1,), padding=[(K - 1, 0)],
        dimension_numbers=('NWC', 'WIO', 'NWC'), feature_group_count=C)


def multiscale_pool(u, pool_w, pool_scale):
    S = u.shape[1]
    c = jnp.cumsum(u.astype(jnp.float32), axis=1)
    pos = jnp.arange(S, dtype=jnp.float32)[:, None] + 1.0
    outs = []
    for gi, w in enumerate(POOL_WINDOWS):
        sl = slice(gi * POOL_GROUP_WIDTH, (gi + 1) * POOL_GROUP_WIDTH)
        cg = c[:, :, sl]
        c_shift = jnp.pad(cg, ((0, 0), (w, 0), (0, 0)))[:, :S]
        mean = (cg - c_shift) / jnp.minimum(pos, float(w))
        outs.append(mean.astype(u.dtype) - u[:, :, sl])
    pooled = jnp.stack(outs, axis=2)
    mixed = jnp.einsum('bsgc,gcd->bsgd', pooled, pool_w)
    return mixed.reshape(u.shape) * pool_scale


def spatial_gating(v, ws, bs):
    B, S, _ = v.shape
    n = S // GMLP_CHUNK
    mask = jnp.tril(jnp.ones((GMLP_CHUNK, GMLP_CHUNK), dtype=bool))
    ws_m = jnp.where(mask[None], ws, jnp.zeros_like(ws))
    vc = v.reshape(B, n, GMLP_CHUNK, GMLP_GROUPS, GMLP_GROUP_WIDTH)
    out = jnp.einsum('gts,bnsgc->bntgc', ws_m, vc) + bs.T[:, :, None]
    return out.reshape(B, S, MIX_WIDTH)


def mixer_block(h, w_in, conf_dw, conf_dw_b, conf_ln_g, conf_ln_b, pool_w, pool_scale,
                sc_conv, gmlp_ln_g, gmlp_ln_b, gmlp_ws, gmlp_bs, w_branch, w_out):
    B, S, _ = h.shape
    proj = h @ w_in
    a_in, pool_in, sc_in, g_in, gate_in = jnp.split(proj, SPLITS, axis=-1)
    a, a_gate = jnp.split(a_in, 2, axis=-1)
    ya = causal_depthwise_conv(a * jax.nn.sigmoid(a_gate), conf_dw) + conf_dw_b
    ya = jax.nn.silu(layer_norm(ya, conf_ln_g, conf_ln_b))
    yb = multiscale_pool(pool_in, pool_w, pool_scale)
    bg, cg, hx = jnp.split(sc_in, 3, axis=-1)
    yc = bg * causal_depthwise_conv(cg * hx, sc_conv)
    u, v = jnp.split(g_in, 2, axis=-1)
    yd = u * spatial_gating(layer_norm(v, gmlp_ln_g, gmlp_ln_b), gmlp_ws, gmlp_bs)
    branches = jnp.stack([ya, yb, yc, yd], axis=2)
    z = jnp.einsum('bskw,kwd->bskd', branches, w_branch)
    gates = jax.nn.sigmoid(gate_in.reshape(B, S, N_BRANCH, D_MODEL))
    merged = jnp.sum(gates * z, axis=2)
    return merged @ w_out


def _fwd_setup_inputs(seed: int = 0) -> dict:
    key = jax.random.key(seed)
    ks = jax.random.split(key, 26)
    f = jnp.float32
    nrm = lambda k, shape, fan: jax.random.normal(k, shape, f) * (fan ** -0.5)
    gain = lambda k, shape: 1.0 + 0.02 * jax.random.normal(k, shape, f)
    small = lambda k, shape: 0.02 * jax.random.normal(k, shape, f)
    return {
        "x": jax.random.normal(ks[0], (BATCH, SEQ, D_MODEL), f),
        "p": jax.random.normal(ks[1], (DEPTH, BATCH, SEQ, PLE_DIM), f),
        "norm_mix": gain(ks[2], (DEPTH, D_MODEL)),
        "w_in": nrm(ks[3], (DEPTH, D_MODEL, COLS_IN), D_MODEL),
        "conf_dw": nrm(ks[4], (DEPTH, CONF_KERNEL, MIX_WIDTH), CONF_KERNEL),
        "conf_dw_b": small(ks[5], (DEPTH, MIX_WIDTH)),
        "conf_ln_g": gain(ks[6], (DEPTH, MIX_WIDTH)),
        "conf_ln_b": small(ks[7], (DEPTH, MIX_WIDTH)),
        "pool_w": nrm(ks[8], (DEPTH, POOL_GROUPS, POOL_GROUP_WIDTH, POOL_GROUP_WIDTH), POOL_GROUP_WIDTH),
        "pool_scale": gain(ks[9], (DEPTH, MIX_WIDTH)),
        "sc_conv": nrm(ks[10], (DEPTH, SC_KERNEL, MIX_WIDTH), SC_KERNEL),
        "gmlp_ln_g": gain(ks[11], (DEPTH, MIX_WIDTH)),
        "gmlp_ln_b": small(ks[12], (DEPTH, MIX_WIDTH)),
        "gmlp_ws": nrm(ks[13], (DEPTH, GMLP_GROUPS, GMLP_CHUNK, GMLP_CHUNK), GMLP_CHUNK),
        "gmlp_bs": gain(ks[14], (DEPTH, GMLP_GROUPS, GMLP_CHUNK)),
        "w_branch": nrm(ks[15], (DEPTH, N_BRANCH, MIX_WIDTH, D_MODEL), MIX_WIDTH),
        "w_out": nrm(ks[16], (DEPTH, D_MODEL, D_MODEL), D_MODEL),
        "norm_mlp": gain(ks[17], (DEPTH, D_MODEL)),
        "w_up": nrm(ks[18], (DEPTH, D_MODEL, D_FF), D_MODEL),
        "w_down": nrm(ks[19], (DEPTH, D_FF, D_MODEL), D_FF),
        "norm_ple": gain(ks[20], (DEPTH, D_MODEL)),
        "w_ple": nrm(ks[21], (DEPTH, PLE_DIM, D_MODEL), PLE_DIM),
        "w_ple_gate": nrm(ks[22], (DEPTH, D_MODEL, D_MODEL), D_MODEL),
        "norm_final": gain(ks[23], (D_MODEL,)),
    }


def _fwd_reference(x, p, norm_mix, w_in, conf_dw, conf_dw_b, conf_ln_g, conf_ln_b, pool_w, pool_scale,
              sc_conv, gmlp_ln_g, gmlp_ln_b, gmlp_ws, gmlp_bs, w_branch, w_out,
              norm_mlp, w_up, w_down, norm_ple, w_ple, w_ple_gate, norm_final):
    for i in range(DEPTH):
        h = rms_norm(x, norm_mix[i])
        x = x + mixer_block(h, w_in[i], conf_dw[i], conf_dw_b[i], conf_ln_g[i], conf_ln_b[i],
                            pool_w[i], pool_scale[i], sc_conv[i], gmlp_ln_g[i], gmlp_ln_b[i],
                            gmlp_ws[i], gmlp_bs[i], w_branch[i], w_out[i])
        h = rms_norm(x, norm_mlp[i])
        x = x + jnp.square(jax.nn.relu(h @ w_up[i])) @ w_down[i]
        h = rms_norm(x, norm_ple[i])
        x = x + (p[i] @ w_ple[i]) * jax.nn.sigmoid(h @ w_ple_gate[i])
    return rms_norm(x, norm_final)


import jax as _jax
import jax.numpy as _jnp

TWIN_FORMAT = 'train_step'
FWD_PARAMS = ['x', 'p', 'norm_mix', 'w_in', 'conf_dw', 'conf_dw_b', 'conf_ln_g', 'conf_ln_b', 'pool_w', 'pool_scale', 'sc_conv', 'gmlp_ln_g', 'gmlp_ln_b', 'gmlp_ws', 'gmlp_bs', 'w_branch', 'w_out', 'norm_mlp', 'w_up', 'w_down', 'norm_ple', 'w_ple', 'w_ple_gate', 'norm_final']
TWIN_WEIGHTS = ['norm_mix', 'w_in', 'conf_dw', 'conf_dw_b', 'conf_ln_g', 'conf_ln_b', 'pool_w', 'pool_scale', 'sc_conv', 'gmlp_ln_g', 'gmlp_ln_b', 'gmlp_ws', 'gmlp_bs', 'w_branch', 'w_out', 'norm_mlp', 'w_up', 'w_down', 'norm_ple', 'w_ple', 'w_ple_gate', 'norm_final']
TWIN_DIFF_INPUT = 'x'
TWIN_INPUTS = ['x', 'p', 'norm_mix', 'w_in', 'conf_dw', 'conf_dw_b', 'conf_ln_g', 'conf_ln_b', 'pool_w', 'pool_scale', 'sc_conv', 'gmlp_ln_g', 'gmlp_ln_b', 'gmlp_ws', 'gmlp_bs', 'w_branch', 'w_out', 'norm_mlp', 'w_up', 'w_down', 'norm_ple', 'w_ple', 'w_ple_gate', 'norm_final', 'loss_target', 'm_norm_mix', 'm_w_in', 'm_conf_dw', 'm_conf_dw_b', 'm_conf_ln_g', 'm_conf_ln_b', 'm_pool_w', 'm_pool_scale', 'm_sc_conv', 'm_gmlp_ln_g', 'm_gmlp_ln_b', 'm_gmlp_ws', 'm_gmlp_bs', 'm_w_branch', 'm_w_out', 'm_norm_mlp', 'm_w_up', 'm_w_down', 'm_norm_ple', 'm_w_ple', 'm_w_ple_gate', 'm_norm_final', 'v_norm_mix', 'v_w_in', 'v_conf_dw', 'v_conf_dw_b', 'v_conf_ln_g', 'v_conf_ln_b', 'v_pool_w', 'v_pool_scale', 'v_sc_conv', 'v_gmlp_ln_g', 'v_gmlp_ln_b', 'v_gmlp_ws', 'v_gmlp_bs', 'v_w_branch', 'v_w_out', 'v_norm_mlp', 'v_w_up', 'v_w_down', 'v_norm_ple', 'v_w_ple', 'v_w_ple_gate', 'v_norm_final']
TWIN_OUTPUTS = ['loss', 'grad_x', 'grad_norm_mix', 'grad_w_in', 'grad_conf_dw', 'grad_conf_dw_b', 'grad_conf_ln_g', 'grad_conf_ln_b', 'grad_pool_w', 'grad_pool_scale', 'grad_sc_conv', 'grad_gmlp_ln_g', 'grad_gmlp_ln_b', 'grad_gmlp_ws', 'grad_gmlp_bs', 'grad_w_branch', 'grad_w_out', 'grad_norm_mlp', 'grad_w_up', 'grad_w_down', 'grad_norm_ple', 'grad_w_ple', 'grad_w_ple_gate', 'grad_norm_final', 'delta_norm_mix', 'delta_w_in', 'delta_conf_dw', 'delta_conf_dw_b', 'delta_conf_ln_g', 'delta_conf_ln_b', 'delta_pool_w', 'delta_pool_scale', 'delta_sc_conv', 'delta_gmlp_ln_g', 'delta_gmlp_ln_b', 'delta_gmlp_ws', 'delta_gmlp_bs', 'delta_w_branch', 'delta_w_out', 'delta_norm_mlp', 'delta_w_up', 'delta_w_down', 'delta_norm_ple', 'delta_w_ple', 'delta_w_ple_gate', 'delta_norm_final', 'new_m_norm_mix', 'new_m_w_in', 'new_m_conf_dw', 'new_m_conf_dw_b', 'new_m_conf_ln_g', 'new_m_conf_ln_b', 'new_m_pool_w', 'new_m_pool_scale', 'new_m_sc_conv', 'new_m_gmlp_ln_g', 'new_m_gmlp_ln_b', 'new_m_gmlp_ws', 'new_m_gmlp_bs', 'new_m_w_branch', 'new_m_w_out', 'new_m_norm_mlp', 'new_m_w_up', 'new_m_w_down', 'new_m_norm_ple', 'new_m_w_ple', 'new_m_w_ple_gate', 'new_m_norm_final', 'new_v_norm_mix', 'new_v_w_in', 'new_v_conf_dw', 'new_v_conf_dw_b', 'new_v_conf_ln_g', 'new_v_conf_ln_b', 'new_v_pool_w', 'new_v_pool_scale', 'new_v_sc_conv', 'new_v_gmlp_ln_g', 'new_v_gmlp_ln_b', 'new_v_gmlp_ws', 'new_v_gmlp_bs', 'new_v_w_branch', 'new_v_w_out', 'new_v_norm_mlp', 'new_v_w_up', 'new_v_w_down', 'new_v_norm_ple', 'new_v_w_ple', 'new_v_w_ple_gate', 'new_v_norm_final']
TWIN_LEAF_KINDS = {'loss': 'loss', 'grad_x': 'grad_x', 'grad_norm_mix': 'grad_w', 'grad_w_in': 'grad_w', 'grad_conf_dw': 'grad_w', 'grad_conf_dw_b': 'grad_w', 'grad_conf_ln_g': 'grad_w', 'grad_conf_ln_b': 'grad_w', 'grad_pool_w': 'grad_w', 'grad_pool_scale': 'grad_w', 'grad_sc_conv': 'grad_w', 'grad_gmlp_ln_g': 'grad_w', 'grad_gmlp_ln_b': 'grad_w', 'grad_gmlp_ws': 'grad_w', 'grad_gmlp_bs': 'grad_w', 'grad_w_branch': 'grad_w', 'grad_w_out': 'grad_w', 'grad_norm_mlp': 'grad_w', 'grad_w_up': 'grad_w', 'grad_w_down': 'grad_w', 'grad_norm_ple': 'grad_w', 'grad_w_ple': 'grad_w', 'grad_w_ple_gate': 'grad_w', 'grad_norm_final': 'grad_w', 'delta_norm_mix': 'delta_w', 'delta_w_in': 'delta_w', 'delta_conf_dw': 'delta_w', 'delta_conf_dw_b': 'delta_w', 'delta_conf_ln_g': 'delta_w', 'delta_conf_ln_b': 'delta_w', 'delta_pool_w': 'delta_w', 'delta_pool_scale': 'delta_w', 'delta_sc_conv': 'delta_w', 'delta_gmlp_ln_g': 'delta_w', 'delta_gmlp_ln_b': 'delta_w', 'delta_gmlp_ws': 'delta_w', 'delta_gmlp_bs': 'delta_w', 'delta_w_branch': 'delta_w', 'delta_w_out': 'delta_w', 'delta_norm_mlp': 'delta_w', 'delta_w_up': 'delta_w', 'delta_w_down': 'delta_w', 'delta_norm_ple': 'delta_w', 'delta_w_ple': 'delta_w', 'delta_w_ple_gate': 'delta_w', 'delta_norm_final': 'delta_w', 'new_m_norm_mix': 'new_m', 'new_m_w_in': 'new_m', 'new_m_conf_dw': 'new_m', 'new_m_conf_dw_b': 'new_m', 'new_m_conf_ln_g': 'new_m', 'new_m_conf_ln_b': 'new_m', 'new_m_pool_w': 'new_m', 'new_m_pool_scale': 'new_m', 'new_m_sc_conv': 'new_m', 'new_m_gmlp_ln_g': 'new_m', 'new_m_gmlp_ln_b': 'new_m', 'new_m_gmlp_ws': 'new_m', 'new_m_gmlp_bs': 'new_m', 'new_m_w_branch': 'new_m', 'new_m_w_out': 'new_m', 'new_m_norm_mlp': 'new_m', 'new_m_w_up': 'new_m', 'new_m_w_down': 'new_m', 'new_m_norm_ple': 'new_m', 'new_m_w_ple': 'new_m', 'new_m_w_ple_gate': 'new_m', 'new_m_norm_final': 'new_m', 'new_v_norm_mix': 'new_v', 'new_v_w_in': 'new_v', 'new_v_conf_dw': 'new_v', 'new_v_conf_dw_b': 'new_v', 'new_v_conf_ln_g': 'new_v', 'new_v_conf_ln_b': 'new_v', 'new_v_pool_w': 'new_v', 'new_v_pool_scale': 'new_v', 'new_v_sc_conv': 'new_v', 'new_v_gmlp_ln_g': 'new_v', 'new_v_gmlp_ln_b': 'new_v', 'new_v_gmlp_ws': 'new_v', 'new_v_gmlp_bs': 'new_v', 'new_v_w_branch': 'new_v', 'new_v_w_out': 'new_v', 'new_v_norm_mlp': 'new_v', 'new_v_w_up': 'new_v', 'new_v_w_down': 'new_v', 'new_v_norm_ple': 'new_v', 'new_v_w_ple': 'new_v', 'new_v_w_ple_gate': 'new_v', 'new_v_norm_final': 'new_v'}


def _forward(args):
    return _fwd_reference(*[args[k] for k in FWD_PARAMS])


def _output_shape():
    out = _jax.eval_shape(lambda: _forward(_fwd_setup_inputs(0)))
    return out.shape, out.dtype

N_MICROBATCH = 1
ADAM_LR = 0.001
ADAM_B1 = 0.9
ADAM_B2 = 0.999
ADAM_EPS = 1e-08
ADAM_WD = 0.01
ADAM_STEP = 10
PER_EXAMPLE_BATCH_AXIS = {'x': 0, 'p': 1, 'loss_target': 0}
SHARED_INPUTS = []
_WEIGHT_DTYPES = {'norm_mix': _jnp.float32, 'w_in': _jnp.float32, 'conf_dw': _jnp.float32, 'conf_dw_b': _jnp.float32, 'conf_ln_g': _jnp.float32, 'conf_ln_b': _jnp.float32, 'pool_w': _jnp.float32, 'pool_scale': _jnp.float32, 'sc_conv': _jnp.float32, 'gmlp_ln_g': _jnp.float32, 'gmlp_ln_b': _jnp.float32, 'gmlp_ws': _jnp.float32, 'gmlp_bs': _jnp.float32, 'w_branch': _jnp.float32, 'w_out': _jnp.float32, 'norm_mlp': _jnp.float32, 'w_up': _jnp.float32, 'w_down': _jnp.float32, 'norm_ple': _jnp.float32, 'w_ple': _jnp.float32, 'w_ple_gate': _jnp.float32, 'norm_final': _jnp.float32}
MOMENT_SCALE = {'norm_mix': 1.084626e-01, 'w_in': 3.581734e-02, 'conf_dw': 3.443030e-02, 'conf_dw_b': 8.007924e-02, 'conf_ln_g': 4.170766e-02, 'conf_ln_b': 3.701244e-02, 'pool_w': 4.758440e-02, 'pool_scale': 4.905726e-02, 'sc_conv': 5.674383e-02, 'gmlp_ln_g': 4.084883e-02, 'gmlp_ln_b': 3.743995e-02, 'gmlp_ws': 3.861576e-02, 'gmlp_bs': 5.417157e-02, 'w_branch': 3.701070e-02, 'w_out': 7.420863e-02, 'norm_mlp': 7.973894e-02, 'w_up': 3.987913e-02, 'w_down': 7.170515e-02, 'norm_ple': 1.159465e-02, 'w_ple': 3.025030e-02, 'w_ple_gate': 1.188191e-02, 'norm_final': 1.618125e+01}


def _to_microbatches(a, axis):
    t = _jnp.moveaxis(a, axis, 0)
    t = t.reshape((N_MICROBATCH, t.shape[0] // N_MICROBATCH) + t.shape[1:])
    return _jnp.moveaxis(t, 1, axis + 1)


def setup_inputs(seed: int = 0) -> dict:
    inp = _fwd_setup_inputs(seed)
    key = _jax.random.fold_in(_jax.random.key(seed), 7919)
    shape, _ = _output_shape()
    out = dict(inp)
    out["loss_target"] = _jax.random.normal(_jax.random.fold_in(key, 0), shape, _jnp.float32)
    for i, name in enumerate(TWIN_WEIGHTS):
        w = inp[name].astype(_jnp.float32)
        if MOMENT_SCALE is None:
            s = _jnp.sqrt(_jnp.mean(_jnp.square(w)) + 1e-30)
        else:
            s = MOMENT_SCALE[name]
        km, kv = _jax.random.split(_jax.random.fold_in(key, i + 1))
        out[name] = w
        out["m_" + name] = s * _jax.random.normal(km, w.shape, _jnp.float32)
        out["v_" + name] = (s * s) * _jax.random.uniform(kv, w.shape, _jnp.float32, 0.5, 1.5)
    if N_MICROBATCH > 1:
        for name, axis in PER_EXAMPLE_BATCH_AXIS.items():
            out[name] = _to_microbatches(out[name], axis)
    return {'x': out['x'], 'p': out['p'], 'norm_mix': out['norm_mix'], 'w_in': out['w_in'], 'conf_dw': out['conf_dw'], 'conf_dw_b': out['conf_dw_b'], 'conf_ln_g': out['conf_ln_g'], 'conf_ln_b': out['conf_ln_b'], 'pool_w': out['pool_w'], 'pool_scale': out['pool_scale'], 'sc_conv': out['sc_conv'], 'gmlp_ln_g': out['gmlp_ln_g'], 'gmlp_ln_b': out['gmlp_ln_b'], 'gmlp_ws': out['gmlp_ws'], 'gmlp_bs': out['gmlp_bs'], 'w_branch': out['w_branch'], 'w_out': out['w_out'], 'norm_mlp': out['norm_mlp'], 'w_up': out['w_up'], 'w_down': out['w_down'], 'norm_ple': out['norm_ple'], 'w_ple': out['w_ple'], 'w_ple_gate': out['w_ple_gate'], 'norm_final': out['norm_final'], 'loss_target': out['loss_target'], 'm_norm_mix': out['m_norm_mix'], 'm_w_in': out['m_w_in'], 'm_conf_dw': out['m_conf_dw'], 'm_conf_dw_b': out['m_conf_dw_b'], 'm_conf_ln_g': out['m_conf_ln_g'], 'm_conf_ln_b': out['m_conf_ln_b'], 'm_pool_w': out['m_pool_w'], 'm_pool_scale': out['m_pool_scale'], 'm_sc_conv': out['m_sc_conv'], 'm_gmlp_ln_g': out['m_gmlp_ln_g'], 'm_gmlp_ln_b': out['m_gmlp_ln_b'], 'm_gmlp_ws': out['m_gmlp_ws'], 'm_gmlp_bs': out['m_gmlp_bs'], 'm_w_branch': out['m_w_branch'], 'm_w_out': out['m_w_out'], 'm_norm_mlp': out['m_norm_mlp'], 'm_w_up': out['m_w_up'], 'm_w_down': out['m_w_down'], 'm_norm_ple': out['m_norm_ple'], 'm_w_ple': out['m_w_ple'], 'm_w_ple_gate': out['m_w_ple_gate'], 'm_norm_final': out['m_norm_final'], 'v_norm_mix': out['v_norm_mix'], 'v_w_in': out['v_w_in'], 'v_conf_dw': out['v_conf_dw'], 'v_conf_dw_b': out['v_conf_dw_b'], 'v_conf_ln_g': out['v_conf_ln_g'], 'v_conf_ln_b': out['v_conf_ln_b'], 'v_pool_w': out['v_pool_w'], 'v_pool_scale': out['v_pool_scale'], 'v_sc_conv': out['v_sc_conv'], 'v_gmlp_ln_g': out['v_gmlp_ln_g'], 'v_gmlp_ln_b': out['v_gmlp_ln_b'], 'v_gmlp_ws': out['v_gmlp_ws'], 'v_gmlp_bs': out['v_gmlp_bs'], 'v_w_branch': out['v_w_branch'], 'v_w_out': out['v_w_out'], 'v_norm_mlp': out['v_norm_mlp'], 'v_w_up': out['v_w_up'], 'v_w_down': out['v_w_down'], 'v_norm_ple': out['v_norm_ple'], 'v_w_ple': out['v_w_ple'], 'v_w_ple_gate': out['v_w_ple_gate'], 'v_norm_final': out['v_norm_final']}


def _loss(weights, diff, rest, loss_target):
    with _jax.named_scope("forward"):
        args = {**rest, TWIN_DIFF_INPUT: diff, **{k: w.astype(_WEIGHT_DTYPES[k]) for k, w in weights.items()}}
        y = _forward(args)
    with _jax.named_scope("loss_head"):
        err = _jnp.square(y.astype(_jnp.float32) - loss_target)
        return 0.5 * _jnp.sum(_jnp.mean(err, axis=-1)) if err.ndim else 0.5 * err


def _adamw(w, g, m, v):
    m = ADAM_B1 * m + (1.0 - ADAM_B1) * g
    v = ADAM_B2 * v + (1.0 - ADAM_B2) * _jnp.square(g)
    m_hat = m / (1.0 - ADAM_B1 ** ADAM_STEP)
    v_hat = v / (1.0 - ADAM_B2 ** ADAM_STEP)
    delta = -ADAM_LR * (m_hat / (_jnp.sqrt(v_hat) + ADAM_EPS) + ADAM_WD * w)
    return delta, m, v


def reference(x, p, norm_mix, w_in, conf_dw, conf_dw_b, conf_ln_g, conf_ln_b, pool_w, pool_scale, sc_conv, gmlp_ln_g, gmlp_ln_b, gmlp_ws, gmlp_bs, w_branch, w_out, norm_mlp, w_up, w_down, norm_ple, w_ple, w_ple_gate, norm_final, loss_target, m_norm_mix, m_w_in, m_conf_dw, m_conf_dw_b, m_conf_ln_g, m_conf_ln_b, m_pool_w, m_pool_scale, m_sc_conv, m_gmlp_ln_g, m_gmlp_ln_b, m_gmlp_ws, m_gmlp_bs, m_w_branch, m_w_out, m_norm_mlp, m_w_up, m_w_down, m_norm_ple, m_w_ple, m_w_ple_gate, m_norm_final, v_norm_mix, v_w_in, v_conf_dw, v_conf_dw_b, v_conf_ln_g, v_conf_ln_b, v_pool_w, v_pool_scale, v_sc_conv, v_gmlp_ln_g, v_gmlp_ln_b, v_gmlp_ws, v_gmlp_bs, v_w_branch, v_w_out, v_norm_mlp, v_w_up, v_w_down, v_norm_ple, v_w_ple, v_w_ple_gate, v_norm_final):
    given = dict(x=x, p=p, norm_mix=norm_mix, w_in=w_in, conf_dw=conf_dw, conf_dw_b=conf_dw_b, conf_ln_g=conf_ln_g, conf_ln_b=conf_ln_b, pool_w=pool_w, pool_scale=pool_scale, sc_conv=sc_conv, gmlp_ln_g=gmlp_ln_g, gmlp_ln_b=gmlp_ln_b, gmlp_ws=gmlp_ws, gmlp_bs=gmlp_bs, w_branch=w_branch, w_out=w_out, norm_mlp=norm_mlp, w_up=w_up, w_down=w_down, norm_ple=norm_ple, w_ple=w_ple, w_ple_gate=w_ple_gate, norm_final=norm_final, loss_target=loss_target, m_norm_mix=m_norm_mix, m_w_in=m_w_in, m_conf_dw=m_conf_dw, m_conf_dw_b=m_conf_dw_b, m_conf_ln_g=m_conf_ln_g, m_conf_ln_b=m_conf_ln_b, m_pool_w=m_pool_w, m_pool_scale=m_pool_scale, m_sc_conv=m_sc_conv, m_gmlp_ln_g=m_gmlp_ln_g, m_gmlp_ln_b=m_gmlp_ln_b, m_gmlp_ws=m_gmlp_ws, m_gmlp_bs=m_gmlp_bs, m_w_branch=m_w_branch, m_w_out=m_w_out, m_norm_mlp=m_norm_mlp, m_w_up=m_w_up, m_w_down=m_w_down, m_norm_ple=m_norm_ple, m_w_ple=m_w_ple, m_w_ple_gate=m_w_ple_gate, m_norm_final=m_norm_final, v_norm_mix=v_norm_mix, v_w_in=v_w_in, v_conf_dw=v_conf_dw, v_conf_dw_b=v_conf_dw_b, v_conf_ln_g=v_conf_ln_g, v_conf_ln_b=v_conf_ln_b, v_pool_w=v_pool_w, v_pool_scale=v_pool_scale, v_sc_conv=v_sc_conv, v_gmlp_ln_g=v_gmlp_ln_g, v_gmlp_ln_b=v_gmlp_ln_b, v_gmlp_ws=v_gmlp_ws, v_gmlp_bs=v_gmlp_bs, v_w_branch=v_w_branch, v_w_out=v_w_out, v_norm_mlp=v_norm_mlp, v_w_up=v_w_up, v_w_down=v_w_down, v_norm_ple=v_norm_ple, v_w_ple=v_w_ple, v_w_ple_gate=v_w_ple_gate, v_norm_final=v_norm_final)
    weights = {n: given[n] for n in TWIN_WEIGHTS}
    shared = {n: given[n] for n in SHARED_INPUTS}
    per_example = {n: given[n] for n in ['x', 'p']}
    grad_fn = _jax.value_and_grad(_loss, argnums=(0, 1))

    def one_microbatch(ex, loss_target):
        ex = dict(ex)
        diff = ex.pop(TWIN_DIFF_INPUT)
        return grad_fn(weights, diff, {**shared, **ex}, loss_target)

    if N_MICROBATCH == 1:
        loss, (grad_w, grad_x) = one_microbatch(per_example, given["loss_target"])
    else:
        def body(carry, xs):
            loss_sum, grad_sum = carry
            l_k, (gw_k, gx_k) = one_microbatch(xs[0], xs[1])
            with _jax.named_scope("update"):
                return (loss_sum + l_k, _jax.tree.map(_jnp.add, grad_sum, gw_k)), gx_k

        init = (_jnp.zeros((), _jnp.float32), _jax.tree.map(_jnp.zeros_like, weights))
        (loss, grad_w), grad_x = _jax.lax.scan(body, init, (per_example, given["loss_target"]))
    with _jax.named_scope("update"):
        delta_w, new_m, new_v = {}, {}, {}
        for n in TWIN_WEIGHTS:
            delta_w[n], new_m[n], new_v[n] = _adamw(weights[n], grad_w[n], given["m_" + n], given["v_" + n])
    return (loss, grad_x, *[grad_w[n] for n in TWIN_WEIGHTS], *[delta_w[n] for n in TWIN_WEIGHTS],
            *[new_m[n] for n in TWIN_WEIGHTS], *[new_v[n] for n in TWIN_WEIGHTS])
```

```python
import functools

import jax
import jax.numpy as jnp
from jax import lax
from jax.experimental import pallas as pl
from jax.experimental.pallas import tpu as pltpu

F32, BF16 = jnp.float32, jnp.bfloat16
EPS = 1e-6
ADAM_LR, ADAM_B1, ADAM_B2, ADAM_EPS, ADAM_WD, ADAM_STEP = 0.001, 0.9, 0.999, 1e-08, 0.01, 10
CONF_KERNEL, SC_KERNEL = 31, 3
POOL_WINDOWS = (2, 4, 8, 16)
MIX = 512
GROUP = 128
HALO = 32
SEQ_TILE = 256
VMEM_LIMIT_BYTES = 48 * 1024 * 1024
MESH_AXES = ("x", "y", "c")
ANY = pl.BlockSpec(memory_space=pl.ANY)

BIG = (("w_in", 2, 1), ("w_branch", 3, 1), ("w_out", 1, 2), ("w_up", 2, 1), ("w_down", 1, 2),
       ("w_ple", 2, 1), ("w_ple_gate", 1, 2))
SMALL_REP = ("norm_mix", "conf_dw_b", "conf_ln_g", "conf_ln_b", "pool_w", "pool_scale", "gmlp_ln_g",
             "gmlp_ln_b", "gmlp_ws", "gmlp_bs", "norm_mlp", "norm_ple", "norm_final")
SMALL_SHARDED = ("conf_dw", "sc_conv")
PACK_LANES = 128
PACK_ROW_MULTIPLE = 256


def _params(*semantics):
    return pltpu.CompilerParams(dimension_semantics=semantics, vmem_limit_bytes=VMEM_LIMIT_BYTES)


def _accumulate(ref, value, first):
    @pl.when(first)
    def _():
        ref[...] = value.astype(ref.dtype)

    @pl.when(jnp.logical_not(first))
    def _():
        ref[...] += value.astype(ref.dtype)


def _rowwise(name, fn, rows, params, outs, accs=(), tm=256):
    n_rows = rows[0][0].shape[-2]
    tm = min(tm, n_rows)
    n_in, n_out = len(rows) + len(params), len(outs)

    def body(*refs):
        res = fn(*[r[...] for r in refs[:n_in]])
        res = res if isinstance(res, (tuple, list)) else (res,)
        out_refs = refs[n_in:]
        for r, v in zip(out_refs[:n_out], res[:n_out]):
            r[...] = v.astype(r.dtype)
        first = pl.program_id(0) == 0
        for r, v in zip(out_refs[n_out:], res[n_out:]):
            _accumulate(r, v, first)

    in_specs = []
    for row in rows:
        arr, w, cb = row[:3]
        if len(row) == 4:
            in_specs.append(pl.BlockSpec((None, tm, w), lambda i, cb=cb, g=row[3]: (g, i, cb)))
        else:
            in_specs.append(pl.BlockSpec((tm, w), lambda i, cb=cb: (i, cb)))
    for p in params:
        in_specs.append(pl.BlockSpec(p.shape, lambda i, nd=p.ndim: (0,) * nd))
    out_specs = [pl.BlockSpec((tm, w), lambda i: (i, 0)) for w, _ in outs]
    out_specs += [pl.BlockSpec(s, lambda i, nd=len(s): (0,) * nd) for s in accs]
    out_shape = [jax.ShapeDtypeStruct((n_rows, w), d) for w, d in outs]
    out_shape += [jax.ShapeDtypeStruct(s, F32) for s in accs]
    return pl.pallas_call(
        body, name=name, grid=(n_rows // tm,), in_specs=in_specs, out_specs=out_specs, out_shape=out_shape,
        compiler_params=_params("arbitrary"),
    )(*[r[0] for r in rows], *params)


def _mm(name, a, b, *, ta=False, tb=False, res=(), epi=None, out_dtypes=(F32,), tm=512, tn=1024, tk=1024):
    g_, m_, k_ = (a.shape[0], a.shape[2], a.shape[1]) if ta else a.shape
    n_ = b.shape[1] if tb else b.shape[2]
    tm, tn, tk = min(tm, m_), min(tn, n_), min(tk, k_)
    nk, n_res, n_out = k_ // tk, len(res), len(out_dtypes)
    dims = (((0 if ta else 1,), (1 if tb else 0,)), ((), ()))
    if epi is None:
        epi = lambda acc: (acc,)

    def body(*refs):
        a_ref, b_ref = refs[:2]
        res_refs, out_refs, acc_ref = refs[2:2 + n_res], refs[2 + n_res:2 + n_res + n_out], refs[-1]
        k = pl.program_id(3)

        @pl.when(k == 0)
        def _():
            acc_ref[...] = jnp.zeros_like(acc_ref)

        acc_ref[...] += lax.dot_general(a_ref[...].astype(BF16), b_ref[...].astype(BF16), dims,
                                        preferred_element_type=F32)

        @pl.when(k == nk - 1)
        def _():
            vals = epi(acc_ref[...], *[r[...] for r in res_refs])
            for o, v in zip(out_refs, vals):
                o[...] = v.astype(o.dtype)

    a_spec = (pl.BlockSpec((None, tk, tm), lambda g, i, j, k: (g, k, i)) if ta
              else pl.BlockSpec((None, tm, tk), lambda g, i, j, k: (g, i, k)))
    b_spec = (pl.BlockSpec((None, tn, tk), lambda g, i, j, k: (g, j, k)) if tb
              else pl.BlockSpec((None, tk, tn), lambda g, i, j, k: (g, k, j)))
    res_specs = [pl.BlockSpec((None, tm, tn), lambda g, i, j, k, off=off: (g, i, j + off)) for _, off in res]
    out_spec = pl.BlockSpec((None, tm, tn), lambda g, i, j, k: (g, i, j))
    outs = pl.pallas_call(
        body, name=name, grid=(g_, m_ // tm, n_ // tn, nk),
        in_specs=[a_spec, b_spec, *res_specs], out_specs=[out_spec] * n_out,
        out_shape=[jax.ShapeDtypeStruct((g_, m_, n_), d) for d in out_dtypes],
        scratch_shapes=[pltpu.VMEM((tm, tn), F32)],
        compiler_params=_params("parallel", "parallel", "parallel", "arbitrary"),
    )(a, b, *[r for r, _ in res])
    return outs


def _mm2(name, a, b, *, res=(), **kw):
    outs = _mm(name, a[None], b[None], res=[(r[None], off) for r, off in res], **kw)
    return [o[0] for o in outs]


@jax.custom_vjp
def _bdot(a, b):
    return jnp.dot(a.astype(BF16), b.astype(BF16), preferred_element_type=F32)


def _bdot_fwd(a, b):
    return _bdot(a, b), (a, b)


def _bdot_bwd(saved, g):
    a, b = saved
    gb = g.astype(BF16)
    da = lax.dot_general(gb, b.astype(BF16), (((1,), (1,)), ((), ())), preferred_element_type=F32)
    db = lax.dot_general(a.astype(BF16), gb, (((0,), (0,)), ((), ())), preferred_element_type=F32)
    return da.astype(a.dtype), db.astype(b.dtype)


_bdot.defvjp(_bdot_fwd, _bdot_bwd)


def _rms(x, g):
    return x * lax.rsqrt(jnp.mean(x * x, axis=-1, keepdims=True) + EPS) * g


def _layer_norm(x, g, b):
    mu = jnp.mean(x, axis=-1, keepdims=True)
    var = jnp.mean(jnp.square(x - mu), axis=-1, keepdims=True)
    return (x - mu) * lax.rsqrt(var + EPS) * g + b


def _glu_and_gate_fn(a, a_gate, cg, hx):
    return a * jax.nn.sigmoid(a_gate), cg * hx


def _branches_fn(ca, pooled, cc, bg, u, v, conf_b, conf_g, conf_beta, pool_scale, gm_g, gm_b, *mats):
    pool_w, ws, bs = mats[0:4], mats[4:8], mats[8:12]
    ya = jax.nn.silu(_layer_norm(ca + conf_b, conf_g, conf_beta))
    yb = jnp.concatenate([_bdot(pooled[:, g * GROUP:(g + 1) * GROUP], pool_w[g]) for g in range(4)], axis=1) * pool_scale
    yc = bg * cc
    vln = _layer_norm(v, gm_g, gm_b)
    causal = lax.broadcasted_iota(jnp.int32, (GROUP, GROUP), 0) >= lax.broadcasted_iota(jnp.int32, (GROUP, GROUP), 1)
    sg = jnp.concatenate(
        [_bdot(jnp.where(causal, ws[g], 0.0), vln[:, g * GROUP:(g + 1) * GROUP]) + bs[g] for g in range(4)], axis=1)
    yd = u * sg
    return ya, yb, yc, yd


def _branches_bwd_fn(*args):
    ins, dys = args[:6] + args[10:], args[6:10]
    _, vjp = jax.vjp(_branches_fn, *ins)
    grads = vjp(tuple(dys))
    return grads


def _glu_bwd_fn(a, a_gate, cg, hx, dga, dchx):
    _, vjp = jax.vjp(_glu_and_gate_fn, a, a_gate, cg, hx)
    return vjp((dga, dchx))


def _rms_bwd_fn(x, dh, dres, g):
    _, vjp = jax.vjp(_rms, x, g)
    dx, dg = vjp(dh)
    return dx + dres, dg


def _loss_fn(x, target, g):
    def f(x, g):
        err = jnp.square(_rms(x, g) - target)
        return 0.5 * jnp.sum(jnp.mean(err, axis=-1, keepdims=True), axis=0, keepdims=True)

    loss, vjp = jax.vjp(f, x, g)
    dx, dg = vjp(jnp.ones((1, 1), F32))
    return dx, dg, jnp.broadcast_to(loss, (1, PACK_LANES))


def _adamw_fn(w, g, m, v):
    m = ADAM_B1 * m + (1.0 - ADAM_B1) * g
    v = ADAM_B2 * v + (1.0 - ADAM_B2) * jnp.square(g)
    m_hat = m / (1.0 - ADAM_B1 ** ADAM_STEP)
    v_hat = v / (1.0 - ADAM_B2 ** ADAM_STEP)
    delta = -ADAM_LR * (m_hat / (jnp.sqrt(v_hat) + ADAM_EPS) + ADAM_WD * w)
    return delta, m, v


def _group_pick(j):
    return [(j == g).astype(F32) for g in range(len(POOL_WINDOWS))]


def _accumulate_row(ref, row, value, first):
    @pl.when(first)
    def _():
        ref[pl.ds(row, 1), :] = value

    @pl.when(jnp.logical_not(first))
    def _():
        ref[pl.ds(row, 1), :] += value


def _rows_iota(n):
    return lax.broadcasted_iota(jnp.int32, (n, 1), 0)


def _seq_specs(n_time, causal):
    per = SEQ_TILE // HALO
    tile = pl.BlockSpec((SEQ_TILE, GROUP), lambda j, i: (i, j))
    if causal:
        halo = pl.BlockSpec((HALO, GROUP), lambda j, i: (jnp.maximum(i * per - 1, 0), j))
    else:
        halo = pl.BlockSpec((HALO, GROUP), lambda j, i: (jnp.minimum((i + 1) * per, n_time * per - 1), j))
    return tile, halo


def _seq_fwd(ga, proj, chx, conf_dw, sc_conv):
    s_len = ga.shape[0]
    n_time = s_len // SEQ_TILE
    ext = HALO + SEQ_TILE

    def body(ga_ref, ga_h, pin_ref, pin_h, chx_ref, chx_h, wc_ref, ws_ref, ca_ref, po_ref, cc_ref):
        j, i = pl.program_id(0), pl.program_id(1)
        keep = (i > 0).astype(F32)

        def extended(cur, halo):
            return jnp.concatenate([halo[...] * keep, cur[...]], axis=0)

        def conv(x, w_ref, taps):
            acc = x * w_ref[pl.ds(taps - 1, 1), :]
            for d in range(1, taps):
                acc = acc + pltpu.roll(x, d, 0) * w_ref[pl.ds(taps - 1 - d, 1), :]
            return acc[HALO:, :]

        ca_ref[...] = conv(extended(ga_ref, ga_h), wc_ref, CONF_KERNEL)
        cc_ref[...] = conv(extended(chx_ref, chx_h), ws_ref, SC_KERNEL)
        u = extended(pin_ref, pin_h)
        s1 = u + pltpu.roll(u, 1, 0)
        s2 = s1 + pltpu.roll(s1, 2, 0)
        s3 = s2 + pltpu.roll(s2, 4, 0)
        s4 = s3 + pltpu.roll(s3, 8, 0)
        pick = _group_pick(j)
        window_sum = s1 * pick[0] + s2 * pick[1] + s3 * pick[2] + s4 * pick[3]
        window = sum(float(w) * f for w, f in zip(POOL_WINDOWS, pick))
        pos = (_rows_iota(ext) + (i * SEQ_TILE - HALO + 1)).astype(F32)
        count = jnp.maximum(jnp.minimum(pos, window), 1.0)
        po_ref[...] = (window_sum / count - u)[HALO:, :]

    tile, halo = _seq_specs(n_time, causal=True)
    pin_tile = pl.BlockSpec((SEQ_TILE, GROUP), lambda j, i: (i, 2 * (MIX // GROUP) + j))
    pin_halo = pl.BlockSpec((HALO, GROUP), lambda j, i: (jnp.maximum(i * (SEQ_TILE // HALO) - 1, 0), 2 * (MIX // GROUP) + j))
    w_spec = lambda taps: pl.BlockSpec((taps, GROUP), lambda j, i: (0, j))
    return pl.pallas_call(
        body, name="seq_fwd", grid=(MIX // GROUP, n_time),
        in_specs=[tile, halo, pin_tile, pin_halo, tile, halo, w_spec(CONF_KERNEL), w_spec(SC_KERNEL)],
        out_specs=[tile] * 3, out_shape=[jax.ShapeDtypeStruct((s_len, MIX), F32)] * 3,
        compiler_params=_params("parallel", "arbitrary"),
    )(ga, ga, proj, proj, chx, chx, conf_dw, sc_conv)


def _seq_bwd(dca, dpooled, dcc, ga, chx, conf_dw, sc_conv):
    s_len = ga.shape[0]
    n_time = s_len // SEQ_TILE
    ext = HALO + SEQ_TILE

    def body(dca_ref, dca_h, dpo_ref, dpo_h, dcc_ref, dcc_h, ga_ref, ga_h, chx_ref, chx_h, wc_ref, ws_ref,
             dga_ref, dpin_ref, dchx_ref, dwc_ref, dws_ref):
        j, i = pl.program_id(0), pl.program_id(1)
        keep_prev = (i > 0).astype(F32)
        keep_next = (i < n_time - 1).astype(F32)

        def with_next(cur, halo):
            return jnp.concatenate([cur[...], halo[...] * keep_next], axis=0)

        def with_prev(cur, halo):
            return jnp.concatenate([halo[...] * keep_prev, cur[...]], axis=0)

        def conv_t(dy, w_ref, taps):
            acc = dy * w_ref[pl.ds(taps - 1, 1), :]
            for d in range(1, taps):
                acc = acc + pltpu.roll(dy, ext - d, 0) * w_ref[pl.ds(taps - 1 - d, 1), :]
            return acc[:SEQ_TILE, :]

        def tap_grads(dy_cur, x, dw_ref, taps):
            for d in range(taps):
                shifted = x if d == 0 else pltpu.roll(x, d, 0)
                part = jnp.sum(dy_cur * shifted[HALO:, :], axis=0, keepdims=True)
                _accumulate_row(dw_ref, taps - 1 - d, part, i == 0)

        dga_ref[...] = conv_t(with_next(dca_ref, dca_h), wc_ref, CONF_KERNEL)
        dchx_ref[...] = conv_t(with_next(dcc_ref, dcc_h), ws_ref, SC_KERNEL)
        tap_grads(dca_ref[...], with_prev(ga_ref, ga_h), dwc_ref, CONF_KERNEL)
        tap_grads(dcc_ref[...], with_prev(chx_ref, chx_h), dws_ref, SC_KERNEL)

        dpo = with_next(dpo_ref, dpo_h)
        pick = _group_pick(j)
        window = sum(float(w) * f for w, f in zip(POOL_WINDOWS, pick))
        pos = (_rows_iota(ext) + (i * SEQ_TILE + 1)).astype(F32)
        r = dpo / jnp.minimum(pos, window)
        q1 = r + pltpu.roll(r, ext - 1, 0)
        q2 = q1 + pltpu.roll(q1, ext - 2, 0)
        q3 = q2 + pltpu.roll(q2, ext - 4, 0)
        q4 = q3 + pltpu.roll(q3, ext - 8, 0)
        q = q1 * pick[0] + q2 * pick[1] + q3 * pick[2] + q4 * pick[3]
        dpin_ref[...] = (q - dpo)[:SEQ_TILE, :].astype(dpin_ref.dtype)

    tile, nxt = _seq_specs(n_time, causal=False)
    _, prv = _seq_specs(n_time, causal=True)
    w_spec = lambda taps: pl.BlockSpec((taps, GROUP), lambda j, i: (0, j))
    return pl.pallas_call(
        body, name="seq_bwd", grid=(MIX // GROUP, n_time),
        in_specs=[tile, nxt, tile, nxt, tile, nxt, tile, prv, tile, prv, w_spec(CONF_KERNEL), w_spec(SC_KERNEL)],
        out_specs=[tile, tile, tile, w_spec(CONF_KERNEL), w_spec(SC_KERNEL)],
        out_shape=[jax.ShapeDtypeStruct((s_len, MIX), F32), jax.ShapeDtypeStruct((s_len, MIX), BF16),
                   jax.ShapeDtypeStruct((s_len, MIX), F32), jax.ShapeDtypeStruct(conf_dw.shape, F32),
                   jax.ShapeDtypeStruct(sc_conv.shape, F32)],
        compiler_params=_params("parallel", "arbitrary"),
    )(dca, dca, dpooled, dpooled, dcc, dcc, ga, ga, chx, chx, conf_dw, sc_conv)


MERGE_TILE = 512


def _merge_fwd(y, w_branch, proj):
    s_len, d_model = y.shape[1], w_branch.shape[2]
    t = min(MERGE_TILE, s_len)
    gate_block0 = (proj.shape[1] - 4 * d_model) // MERGE_TILE
    per_branch = d_model // MERGE_TILE

    def body(y_ref, w_ref, gate_ref, out_ref, acc_ref):
        k = pl.program_id(2)

        @pl.when(k == 0)
        def _():
            acc_ref[...] = jnp.zeros_like(acc_ref)

        z = jnp.dot(y_ref[...], w_ref[...], preferred_element_type=F32)
        acc_ref[...] += jax.nn.sigmoid(gate_ref[...]) * z

        @pl.when(k == 3)
        def _():
            out_ref[...] = acc_ref[...].astype(out_ref.dtype)

    return pl.pallas_call(
        body, name="merge_fwd", grid=(s_len // t, per_branch, 4),
        in_specs=[pl.BlockSpec((None, t, MIX), lambda i, j, k: (k, i, 0)),
                  pl.BlockSpec((None, MIX, MERGE_TILE), lambda i, j, k: (k, 0, j)),
                  pl.BlockSpec((t, MERGE_TILE), lambda i, j, k: (i, gate_block0 + k * per_branch + j))],
        out_specs=pl.BlockSpec((t, MERGE_TILE), lambda i, j, k: (i, j)),
        out_shape=jax.ShapeDtypeStruct((s_len, d_model), BF16),
        scratch_shapes=[pltpu.VMEM((t, MERGE_TILE), F32)],
        compiler_params=_params("parallel", "parallel", "arbitrary"),
    )(y, w_branch, proj)


def _merge_bwd(y, w_branch, proj, dmerged):
    s_len, d_model = y.shape[1], w_branch.shape[2]
    t = min(MERGE_TILE, s_len)
    gate_block0 = (proj.shape[1] - 4 * d_model) // MERGE_TILE
    per_branch = d_model // MERGE_TILE

    def body(y_ref, w_ref, gate_ref, dm_ref, dz_ref, dgate_ref):
        z = jnp.dot(y_ref[...], w_ref[...], preferred_element_type=F32)
        gate = jax.nn.sigmoid(gate_ref[...])
        dm = dm_ref[...]
        dz_ref[...] = (dm * gate).astype(dz_ref.dtype)
        dgate_ref[...] = (dm * z * gate * (1.0 - gate)).astype(dgate_ref.dtype)

    return pl.pallas_call(
        body, name="merge_bwd", grid=(s_len // t, per_branch, 4),
        in_specs=[pl.BlockSpec((None, t, MIX), lambda i, j, k: (k, i, 0)),
                  pl.BlockSpec((None, MIX, MERGE_TILE), lambda i, j, k: (k, 0, j)),
                  pl.BlockSpec((t, MERGE_TILE), lambda i, j, k: (i, gate_block0 + k * per_branch + j)),
                  pl.BlockSpec((t, MERGE_TILE), lambda i, j, k: (i, j))],
        out_specs=[pl.BlockSpec((None, t, MERGE_TILE), lambda i, j, k: (k, i, j)),
                   pl.BlockSpec((t, MERGE_TILE), lambda i, j, k: (i, k * per_branch + j))],
        out_shape=[jax.ShapeDtypeStruct((4, s_len, d_model), BF16), jax.ShapeDtypeStruct((s_len, 4 * d_model), BF16)],
        compiler_params=_params("parallel", "parallel", "arbitrary"),
    )(y, w_branch, proj, dmerged)


def _row(v):
    return v.reshape(1, -1)


def _branch_params(w):
    mats = [w["pool_w"][g] for g in range(4)] + [w["gmlp_ws"][g] for g in range(4)]
    mats += [w["gmlp_bs"][g].reshape(GROUP, 1) for g in range(4)]
    return [_row(w["conf_dw_b"]), _row(w["conf_ln_g"]), _row(w["conf_ln_b"]), _row(w["pool_scale"]),
            _row(w["gmlp_ln_g"]), _row(w["gmlp_ln_b"])] + mats


def _layer_fwd(x0, p_l, w):
    d_model = x0.shape[1]
    (h1,) = _rowwise("rms_mix", _rms, [(x0, d_model, 0)], [_row(w["norm_mix"])], [(d_model, BF16)])
    (proj,) = _mm2("proj", h1, w["w_in"])
    ga, chx = _rowwise("glu", _glu_and_gate_fn, [(proj, MIX, 0), (proj, MIX, 1), (proj, MIX, 4), (proj, MIX, 5)], [],
                       [(MIX, F32), (MIX, F32)])
    ca, pooled, cc = _seq_fwd(ga, proj, chx, w["conf_dw"], w["sc_conv"])
    ys = _rowwise("branches", _branches_fn,
                  [(ca, MIX, 0), (pooled, MIX, 0), (cc, MIX, 0), (proj, MIX, 3), (proj, MIX, 6), (proj, MIX, 7)],
                  _branch_params(w), [(MIX, BF16)] * 4, tm=GROUP)
    y = jnp.stack(ys)
    merged = _merge_fwd(y, w["w_branch"], proj)
    (x1,) = _mm2("out_proj", merged, w["w_out"], res=[(x0, 0)], epi=lambda acc, r: (acc + r,))
    (h2,) = _rowwise("rms_mlp", _rms, [(x1, d_model, 0)], [_row(w["norm_mlp"])], [(d_model, BF16)])
    up, act = _mm2("mlp_up", h2, w["w_up"], out_dtypes=(F32, BF16),
                   epi=lambda acc: (acc, jnp.square(jnp.maximum(acc, 0.0))))
    (x2,) = _mm2("mlp_down", act, w["w_down"], res=[(x1, 0)], epi=lambda acc, r: (acc + r,))
    (h3,) = _rowwise("rms_ple", _rms, [(x2, d_model, 0)], [_row(w["norm_ple"])], [(d_model, BF16)])
    (pg,) = _mm2("ple_gate", h3, w["w_ple_gate"])
    (x3,) = _mm2("ple", p_l, w["w_ple"], res=[(pg, 0), (x2, 0)],
                 epi=lambda acc, g, r: (acc * jax.nn.sigmoid(g) + r,))
    saved = dict(x0=x0, h1=h1, proj=proj, ga=ga, chx=chx, ca=ca, pooled=pooled, cc=cc, y=y, merged=merged, x1=x1,
                 h2=h2, up=up, act=act, x2=x2, h3=h3, pg=pg)
    return x3, saved


def _layer_bwd(dx3, p_l, w, s):
    d_model = dx3.shape[1]
    big, small = {}, {}

    def ple_epi(acc, dx, g):
        sig = jax.nn.sigmoid(g)
        return dx * sig, dx * acc * sig * (1.0 - sig)

    dpp, dpg = _mm2("ple_bwd", p_l, w["w_ple"], res=[(dx3, 0), (s["pg"], 0)], epi=ple_epi, out_dtypes=(BF16, BF16))
    (big["w_ple"],) = _mm2("dw_ple", p_l, dpp, ta=True, out_dtypes=(BF16,))
    (big["w_ple_gate"],) = _mm2("dw_ple_gate", s["h3"], dpg, ta=True, out_dtypes=(BF16,))
    (dh3,) = _mm2("dh_ple", dpg, w["w_ple_gate"], tb=True)
    dx2, small["norm_ple"] = _rowwise("rms_ple_bwd", _rms_bwd_fn, [(s["x2"], d_model, 0), (dh3, d_model, 0), (dx3, d_model, 0)],
                                      [_row(w["norm_ple"])], [(d_model, F32)], [(1, d_model)])

    (dup,) = _mm2("mlp_down_bwd", dx2, w["w_down"], tb=True, res=[(s["up"], 0)], out_dtypes=(BF16,),
                  epi=lambda acc, up: (acc * 2.0 * jnp.maximum(up, 0.0),))
    (big["w_down"],) = _mm2("dw_down", s["act"], dx2, ta=True, out_dtypes=(BF16,))
    (big["w_up"],) = _mm2("dw_up", s["h2"], dup, ta=True, out_dtypes=(BF16,))
    (dh2,) = _mm2("mlp_up_bwd", dup, w["w_up"], tb=True)
    dx1, small["norm_mlp"] = _rowwise("rms_mlp_bwd", _rms_bwd_fn, [(s["x1"], d_model, 0), (dh2, d_model, 0), (dx2, d_model, 0)],
                                      [_row(w["norm_mlp"])], [(d_model, F32)], [(1, d_model)])

    (dmerged,) = _mm2("out_proj_bwd", dx1, w["w_out"], tb=True)
    (big["w_out"],) = _mm2("dw_out", s["merged"], dx1, ta=True, out_dtypes=(BF16,))
    dz, dgate = _merge_bwd(s["y"], w["w_branch"], s["proj"], dmerged)
    (dy,) = _mm("branch_bwd", dz, w["w_branch"], tb=True)
    (big["w_branch"],) = _mm("dw_branch", s["y"], dz, ta=True, out_dtypes=(BF16,))

    proj = s["proj"]
    params = _branch_params(w)
    grads = _rowwise(
        "branches_bwd", _branches_bwd_fn,
        [(s["ca"], MIX, 0), (s["pooled"], MIX, 0), (s["cc"], MIX, 0), (proj, MIX, 3), (proj, MIX, 6), (proj, MIX, 7)]
        + [(dy, MIX, 0, g) for g in range(4)], params,
        [(MIX, F32), (MIX, F32), (MIX, F32), (MIX, BF16), (MIX, BF16), (MIX, BF16)], [p.shape for p in params], tm=GROUP)
    dca, dpooled, dcc, dbg, du, dv = grads[:6]
    pg_ = grads[6:]
    small["conf_dw_b"], small["conf_ln_g"], small["conf_ln_b"], small["pool_scale"] = pg_[0], pg_[1], pg_[2], pg_[3]
    small["gmlp_ln_g"], small["gmlp_ln_b"] = pg_[4], pg_[5]
    small["pool_w"] = jnp.stack(pg_[6:10])
    small["gmlp_ws"] = jnp.stack(pg_[10:14])
    small["gmlp_bs"] = jnp.stack([b.reshape(GROUP) for b in pg_[14:18]])

    dga, dpin, dchx, small["conf_dw"], small["sc_conv"] = _seq_bwd(dca, dpooled, dcc, s["ga"], s["chx"], w["conf_dw"], w["sc_conv"])
    da, dag, dcg, dhx = _rowwise("glu_bwd", _glu_bwd_fn,
                                 [(proj, MIX, 0), (proj, MIX, 1), (proj, MIX, 4), (proj, MIX, 5), (dga, MIX, 0), (dchx, MIX, 0)],
                                 [], [(MIX, BF16)] * 4)
    dproj = jnp.concatenate([da, dag, dpin, dbg, dcg, dhx, du, dv, dgate], axis=1)
    (big["w_in"],) = _mm2("dw_in", s["h1"], dproj, ta=True, out_dtypes=(BF16,))
    (dh1,) = _mm2("proj_bwd", dproj, w["w_in"], tb=True)
    dx0, small["norm_mix"] = _rowwise("rms_mix_bwd", _rms_bwd_fn, [(s["x0"], d_model, 0), (dh1, d_model, 0), (dx1, d_model, 0)],
                                      [_row(w["norm_mix"])], [(d_model, F32)], [(1, d_model)])
    return dx0, big, small


def _position():
    return lax.axis_index("x"), lax.axis_index("y"), lax.axis_index("c")


def _other_chips(x, y):
    return [(1 - x, y), (x, 1 - y), (1 - x, 1 - y)]


def _region(ref, cuts):
    return ref.at[tuple(pl.ds(*cuts[a]) if a in cuts else slice(None) for a in range(len(ref.shape)))]


def _allgather_small(name, block):
    m_per, n = block.shape

    def body(x_ref, out_ref, send_sems, recv_sems, local_sem):
        x, y, c = _position()
        me, sibling = (x, y, c), (x, y, 1 - c)
        chips = _other_chips(x, y)

        def rows(px, py, pc):
            return out_ref.at[pl.ds((4 * px + 2 * py + pc) * m_per, m_per), :]

        def copy(k, blk, to, src=None):
            return pltpu.make_async_remote_copy(
                src_ref=rows(*blk) if src is None else src, dst_ref=rows(*blk), send_sem=send_sems.at[k],
                recv_sem=recv_sems.at[k], device_id=to, device_id_type=pl.DeviceIdType.MESH)

        mine = pltpu.make_async_copy(x_ref, rows(*me), local_sem)
        mine.start()
        first = [copy(0, me, sibling, src=x_ref)]
        first += [copy(1 + j, me, (*chip, c), src=x_ref) for j, chip in enumerate(chips)]
        for cp in first:
            cp.start()
        passed = [copy(4 + j, (*chip, c), sibling) for j, chip in enumerate(chips)]
        for j, chip in enumerate(chips):
            copy(1 + j, (*chip, c), me).wait_recv()
            passed[j].start()
        copy(0, sibling, me).wait_recv()
        for j, chip in enumerate(chips):
            copy(4 + j, (*chip, 1 - c), me).wait_recv()
        for cp in first + passed:
            cp.wait_send()
        mine.wait()

    return pl.pallas_call(
        body, name=name, out_shape=jax.ShapeDtypeStruct((8 * m_per, n), block.dtype), in_specs=[ANY], out_specs=ANY,
        scratch_shapes=[pltpu.SemaphoreType.DMA((7,)), pltpu.SemaphoreType.DMA((7,)), pltpu.SemaphoreType.DMA],
    )(block)


def _cuts(arr_shape, shard_axis, half_axis, shard=None, half=None):
    cuts = {}
    if shard is not None:
        n = arr_shape[shard_axis] // 4
        cuts[shard_axis] = (shard * n, n)
    if half is not None:
        n = arr_shape[half_axis] // 2
        cuts[half_axis] = (half * n, n)
    return cuts


def _full_shape(shard_shape, shard_axis):
    return tuple(4 * n if a == shard_axis else n for a, n in enumerate(shard_shape))


def _allgather_weights(shards):
    n_w = len(BIG)
    full_shapes = [_full_shape(sh.shape, sa) for sh, (_, sa, _) in zip(shards, BIG)]

    def body(*refs):
        src, dst = refs[:n_w], refs[n_w:2 * n_w]
        ici_send, ici_recv, d2d_send, d2d_recv, local_sems = refs[2 * n_w:]
        x, y, c = _position()
        chips = _other_chips(x, y)
        my_shard = 2 * x + y
        local, sent, passed = [], [], []
        for wi, (_, sa, ha) in enumerate(BIG):
            shape = full_shapes[wi]
            cp = pltpu.make_async_copy(src[wi], _region(dst[wi], _cuts(shape, sa, ha, shard=my_shard)), local_sems.at[wi])
            cp.start()
            local.append(cp)
            for j, chip in enumerate(chips):
                half_src = _region(src[wi], {ha: (c * (shape[ha] // 2), shape[ha] // 2)})
                cp = pltpu.make_async_remote_copy(
                    src_ref=half_src, dst_ref=_region(dst[wi], _cuts(shape, sa, ha, shard=my_shard, half=c)),
                    send_sem=ici_send.at[3 * wi + j], recv_sem=ici_recv.at[3 * wi + j],
                    device_id=(*chip, c), device_id_type=pl.DeviceIdType.MESH)
                cp.start()
                sent.append(cp)
        for wi, (_, sa, ha) in enumerate(BIG):
            shape = full_shapes[wi]
            for j, (px, py) in enumerate(chips):
                landed = _region(dst[wi], _cuts(shape, sa, ha, shard=2 * px + py, half=c))
                pltpu.make_async_remote_copy(
                    src_ref=landed, dst_ref=landed, send_sem=ici_send.at[3 * wi + j], recv_sem=ici_recv.at[3 * wi + j],
                    device_id=(px, py, c), device_id_type=pl.DeviceIdType.MESH).wait_recv()
                cp = pltpu.make_async_remote_copy(
                    src_ref=landed, dst_ref=landed, send_sem=d2d_send.at[3 * wi + j], recv_sem=d2d_recv.at[3 * wi + j],
                    device_id=(x, y, 1 - c), device_id_type=pl.DeviceIdType.MESH)
                cp.start()
                passed.append(cp)
        for wi, (_, sa, ha) in enumerate(BIG):
            shape = full_shapes[wi]
            for j, (px, py) in enumerate(chips):
                other_half = _region(dst[wi], _cuts(shape, sa, ha, shard=2 * px + py, half=1 - c))
                pltpu.make_async_remote_copy(
                    src_ref=other_half, dst_ref=other_half, send_sem=d2d_send.at[3 * wi + j], recv_sem=d2d_recv.at[3 * wi + j],
                    device_id=(x, y, 1 - c), device_id_type=pl.DeviceIdType.MESH).wait_recv()
        for cp in sent + passed:
            cp.wait_send()
        for cp in local:
            cp.wait()

    return pl.pallas_call(
        body, name="allgather_weights", in_specs=[ANY] * n_w, out_specs=[ANY] * n_w,
        out_shape=[jax.ShapeDtypeStruct(s, BF16) for s in full_shapes],
        scratch_shapes=[pltpu.SemaphoreType.DMA((3 * n_w,))] * 4 + [pltpu.SemaphoreType.DMA((n_w,))],
    )(*shards)


def _exchange(name, inputs, out_shapes, plan):
    n_in = len(inputs)
    remote0, local0 = plan([None] * n_in, [None] * len(out_shapes), None)

    def body(*refs):
        ins, outs = refs[:n_in], refs[n_in:n_in + len(out_shapes)]
        send_sems, recv_sems, local_sems = refs[n_in + len(out_shapes):]
        remote, local = plan(ins, outs, _position())
        local_copies = [pltpu.make_async_copy(s, d, local_sems.at[k]) for k, (s, d) in enumerate(local)]
        for cp in local_copies:
            cp.start()
        copies = [pltpu.make_async_remote_copy(src_ref=s, dst_ref=d, send_sem=send_sems.at[k], recv_sem=recv_sems.at[k],
                                               device_id=peer, device_id_type=pl.DeviceIdType.MESH)
                  for k, (s, d, peer) in enumerate(remote)]
        for cp in copies:
            cp.start()
        for cp in copies:
            cp.wait()
        for cp in local_copies:
            cp.wait()

    return pl.pallas_call(
        body, name=name, in_specs=[ANY] * n_in, out_specs=[ANY] * len(out_shapes), out_shape=out_shapes,
        scratch_shapes=[pltpu.SemaphoreType.DMA((len(remote0),)), pltpu.SemaphoreType.DMA((len(remote0),)),
                        pltpu.SemaphoreType.DMA((max(len(local0), 1),))],
    )(*inputs)


def _half_shape(shape, half_axis):
    return tuple(n // 2 if a == half_axis else n for a, n in enumerate(shape))


def _shard_shape(shape, shard_axis):
    return tuple(n // 4 if a == shard_axis else n for a, n in enumerate(shape))


def _flat2d(a):
    return a.reshape(-1, a.shape[-1])


ELEMENTWISE_TILE_ELEMENTS = 1 << 18


def _row_tile(rows, cols):
    tile = 1
    while rows % (2 * tile) == 0 and 2 * tile * cols <= ELEMENTWISE_TILE_ELEMENTS:
        tile *= 2
    assert tile >= 16 or tile == rows, (rows, cols)
    return tile


def _sum_arrays(name, arrays, out_dtype):
    shape = arrays[0].shape
    flat = [_flat2d(a) for a in arrays]
    rows, cols = flat[0].shape
    tm = _row_tile(rows, cols)

    def fn(*tiles):
        acc = tiles[0].astype(F32)
        for t in tiles[1:]:
            acc = acc + t.astype(F32)
        return acc

    (out,) = _rowwise(name, fn, [(a, cols, 0) for a in flat], [], [(cols, out_dtype)], tm=tm)
    return out.reshape(shape)


def _reduce_scatter_grads(grads):
    x, y, c = _position()
    my_shard = 2 * x + y
    shapes = [g.shape for g in grads]

    def to_sibling(ins, outs, pos):
        remote = []
        for wi, (_, sa, ha) in enumerate(BIG):
            if pos is None:
                remote.append(None)
                continue
            px, py, pc = pos
            remote.append((_region(ins[wi], _cuts(shapes[wi], sa, ha, half=1 - pc)), outs[wi], (px, py, 1 - pc)))
        return remote, []

    landed = _exchange("grads_to_sibling", grads, [jax.ShapeDtypeStruct(_half_shape(s, ha), BF16) for s, (_, _, ha) in zip(shapes, BIG)],
                       to_sibling)
    chip_sums = []
    for wi, (name, sa, ha) in enumerate(BIG):
        n = shapes[wi][ha] // 2
        mine = lax.dynamic_slice_in_dim(grads[wi], c * n, n, axis=ha)
        chip_sums.append(_sum_arrays("chip_sum_" + name, [mine, landed[wi]], BF16))
    half_shapes = [t.shape for t in chip_sums]

    def to_chips(ins, outs, pos):
        remote = []
        for wi, (_, sa, ha) in enumerate(BIG):
            for j in range(3):
                if pos is None:
                    remote.append(None)
                    continue
                px, py, pc = pos
                qx, qy = _other_chips(px, py)[j]
                piece = _region(ins[wi], _cuts(shapes[wi], sa, ha, shard=2 * qx + qy))
                remote.append((piece, outs[wi].at[j], (qx, qy, pc)))
        return remote, []

    piece_shapes = [_shard_shape(s, sa) for s, (_, sa, _) in zip(half_shapes, BIG)]
    pieces = _exchange("grads_to_chips", chip_sums, [jax.ShapeDtypeStruct((3, *s), BF16) for s in piece_shapes], to_chips)
    reduced = []
    for wi, (name, sa, ha) in enumerate(BIG):
        n = shapes[wi][sa] // 4
        mine = lax.dynamic_slice_in_dim(chip_sums[wi], my_shard * n, n, axis=sa)
        reduced.append(_sum_arrays("shard_sum_" + name, [mine, pieces[wi][0], pieces[wi][1], pieces[wi][2]], F32))

    shard_shapes = [_shard_shape(s, sa) for s, (_, sa, _) in zip(shapes, BIG)]

    def halves_to_sibling(ins, outs, pos):
        remote, local = [], []
        for wi, (_, sa, ha) in enumerate(BIG):
            if pos is None:
                remote.append(None)
                local.append(None)
                continue
            px, py, pc = pos
            n = shard_shapes[wi][ha] // 2
            place = _region(outs[wi], {ha: (pc * n, n)})
            remote.append((ins[wi], place, (px, py, 1 - pc)))
            local.append((ins[wi], place))
        return remote, local

    return _exchange("grad_halves_to_sibling", reduced, [jax.ShapeDtypeStruct(s, F32) for s in shard_shapes], halves_to_sibling)


def _pack(arrays):
    flat = jnp.concatenate([a.reshape(-1).astype(F32) for a in arrays])
    rows = -(-flat.shape[0] // PACK_LANES)
    rows = -(-rows // PACK_ROW_MULTIPLE) * PACK_ROW_MULTIPLE
    return jnp.pad(flat, (0, rows * PACK_LANES - flat.shape[0])).reshape(rows, PACK_LANES)


def _unpack(buf, shapes):
    flat, out, off = buf.reshape(-1), [], 0
    for s in shapes:
        n = 1
        for d in s:
            n *= d
        out.append(flat[off:off + n].reshape(s))
        off += n
    return out


def _sum8(name, gathered, rows):
    def body(in_ref, out_ref):
        acc = in_ref[0]
        for d in range(1, 8):
            acc = acc + in_ref[d]
        out_ref[...] = acc

    return pl.pallas_call(
        body, name=name, grid=(rows // PACK_ROW_MULTIPLE,),
        in_specs=[pl.BlockSpec((8, PACK_ROW_MULTIPLE, PACK_LANES), lambda i: (0, i, 0))],
        out_specs=pl.BlockSpec((PACK_ROW_MULTIPLE, PACK_LANES), lambda i: (i, 0)),
        out_shape=jax.ShapeDtypeStruct((rows, PACK_LANES), F32), compiler_params=_params("arbitrary"),
    )(gathered.reshape(8, rows, PACK_LANES))


def _adamw(name, w, g, m, v):
    shape = w.shape
    flat = [_flat2d(a) for a in (w, g, m, v)]
    rows, cols = flat[0].shape
    tm = _row_tile(rows, cols)
    outs = _rowwise(name, _adamw_fn, [(a, cols, 0) for a in flat], [], [(cols, F32)] * 3, tm=tm)
    return [o.reshape(shape) for o in outs]


WEIGHT_ORDER = ("norm_mix", "w_in", "conf_dw", "conf_dw_b", "conf_ln_g", "conf_ln_b", "pool_w", "pool_scale", "sc_conv",
                "gmlp_ln_g", "gmlp_ln_b", "gmlp_ws", "gmlp_bs", "w_branch", "w_out", "norm_mlp", "w_up", "w_down",
                "norm_ple", "w_ple", "w_ple_gate", "norm_final")


def _local_step(x, p, loss_target, weights, full_big, full_small_sharded):
    n_layers = p.shape[0]
    d_model = x.shape[1]
    layer_w = []
    for l in range(n_layers):
        w = {n: weights[n][l] for n in SMALL_REP if n != "norm_final"}
        w.update({n: full_big[n][l] for n, _, _ in BIG})
        w.update({n: full_small_sharded[n][l] for n in SMALL_SHARDED})
        layer_w.append(w)
    saved = []
    h = x
    for l in range(n_layers):
        h, s = _layer_fwd(h, p[l], layer_w[l])
        saved.append(s)
    dx, d_final, loss = _rowwise("loss", _loss_fn, [(h, d_model, 0), (loss_target, d_model, 0)], [_row(weights["norm_final"])],
                                 [(d_model, F32)], [(1, d_model), (1, PACK_LANES)])
    big_grads = {n: [None] * n_layers for n, _, _ in BIG}
    small_grads = {n: [None] * n_layers for n in SMALL_REP + SMALL_SHARDED if n != "norm_final"}
    for l in reversed(range(n_layers)):
        dx, big, small = _layer_bwd(dx, p[l], layer_w[l], saved[l])
        for n in big:
            big_grads[n][l] = big[n]
        for n in small:
            small_grads[n][l] = small[n].reshape(weights[n].shape[1:]) if n in SMALL_REP else small[n]
    big_grads = {n: jnp.stack(v) for n, v in big_grads.items()}
    small_grads = {n: jnp.stack(v) for n, v in small_grads.items()}
    small_grads["norm_final"] = d_final.reshape(-1)
    return loss, dx, big_grads, small_grads


def kernel(x, p, norm_mix, w_in, conf_dw, conf_dw_b, conf_ln_g, conf_ln_b, pool_w, pool_scale, sc_conv, gmlp_ln_g, gmlp_ln_b, gmlp_ws, gmlp_bs, w_branch, w_out, norm_mlp, w_up, w_down, norm_ple, w_ple, w_ple_gate, norm_final, loss_target, m_norm_mix, m_w_in, m_conf_dw, m_conf_dw_b, m_conf_ln_g, m_conf_ln_b, m_pool_w, m_pool_scale, m_sc_conv, m_gmlp_ln_g, m_gmlp_ln_b, m_gmlp_ws, m_gmlp_bs, m_w_branch, m_w_out, m_norm_mlp, m_w_up, m_w_down, m_norm_ple, m_w_ple, m_w_ple_gate, m_norm_final, v_norm_mix, v_w_in, v_conf_dw, v_conf_dw_b, v_conf_ln_g, v_conf_ln_b, v_pool_w, v_pool_scale, v_sc_conv, v_gmlp_ln_g, v_gmlp_ln_b, v_gmlp_ws, v_gmlp_bs, v_w_branch, v_w_out, v_norm_mlp, v_w_up, v_w_down, v_norm_ple, v_w_ple, v_w_ple_gate, v_norm_final):
    given = dict(locals())
    weights = {n: given[n] for n in WEIGHT_ORDER}
    mom_m = {n: given["m_" + n] for n in WEIGHT_ORDER}
    mom_v = {n: given["v_" + n] for n in WEIGHT_ORDER}
    my_shard = 2 * lax.axis_index("x") + lax.axis_index("y")

    full = _allgather_weights([weights[n].astype(BF16) for n, _, _ in BIG])
    full_big = {n: f for (n, _, _), f in zip(BIG, full)}
    shard_shapes = [weights[n].shape for n in SMALL_SHARDED]
    gathered = _allgather_small("allgather_small_weights", _pack([weights[n] for n in SMALL_SHARDED]))
    per_dev = gathered.reshape(8, -1, PACK_LANES)
    full_small_sharded = {}
    chip_parts = [_unpack(per_dev[2 * s], shard_shapes) for s in range(4)]
    for i, n in enumerate(SMALL_SHARDED):
        full_small_sharded[n] = jnp.concatenate([chip_parts[s][i] for s in range(4)], axis=-1)

    loss_row, dx, big_grads, small_grads = _local_step(x[0], p[:, 0], loss_target[0], weights, full_big, full_small_sharded)
    loss = lax.psum(loss_row[0, 0], MESH_AXES)

    big_reduced = _reduce_scatter_grads([big_grads[n] for n, _, _ in BIG])
    grads = {n: g for (n, _, _), g in zip(BIG, big_reduced)}

    small_names = SMALL_REP + SMALL_SHARDED
    packed = _pack([small_grads[n] for n in small_names])
    rows = packed.shape[0]
    summed = _sum8("sum_small_grads", _allgather_small("allgather_small_grads", packed), rows)
    small_full_shapes = [weights[n].shape for n in SMALL_REP] + [full_small_sharded[n].shape for n in SMALL_SHARDED]
    for n, g in zip(small_names, _unpack(summed, small_full_shapes)):
        if n in SMALL_SHARDED:
            width = weights[n].shape[-1]
            g = lax.dynamic_slice_in_dim(g, my_shard * width, width, axis=g.ndim - 1)
        grads[n] = g

    delta, new_m, new_v = {}, {}, {}
    for n, _, _ in BIG:
        delta[n], new_m[n], new_v[n] = _adamw("adamw_" + n, weights[n], grads[n], mom_m[n], mom_v[n])
    for group, names in (("rep", SMALL_REP), ("sharded", SMALL_SHARDED)):
        outs = _adamw("adamw_small_" + group, *[_pack([src[n] for n in names]) for src in (weights, grads, mom_m, mom_v)])
        shapes = [weights[n].shape for n in names]
        for dst, buf in zip((delta, new_m, new_v), outs):
            for n, a in zip(names, _unpack(buf, shapes)):
                dst[n] = a

    return (loss, dx[None], *[grads[n] for n in WEIGHT_ORDER], *[delta[n] for n in WEIGHT_ORDER],
            *[new_m[n] for n in WEIGHT_ORDER], *[new_v[n] for n in WEIGHT_ORDER])
```

```python
import functools

import jax
import jax.numpy as jnp
from jax import lax
from jax.experimental import pallas as pl
from jax.experimental.pallas import tpu as pltpu

F32, BF16 = jnp.float32, jnp.bfloat16
EPS = 1e-6
ADAM_LR, ADAM_B1, ADAM_B2, ADAM_EPS, ADAM_WD, ADAM_STEP = 0.001, 0.9, 0.999, 1e-08, 0.01, 10
CONF_KERNEL, SC_KERNEL = 31, 3
POOL_WINDOWS = (2, 4, 8, 16)
MIX = 512
GROUP = 128
HALO = 32
SEQ_TILE = 256
VMEM_LIMIT_BYTES = 48 * 1024 * 1024
MESH_AXES = ("x", "y", "c")
ANY = pl.BlockSpec(memory_space=pl.ANY)

BIG = (("w_in", 2, 1), ("w_branch", 3, 1), ("w_out", 1, 2), ("w_up", 2, 1), ("w_down", 1, 2),
       ("w_ple", 2, 1), ("w_ple_gate", 1, 2))
SMALL_REP = ("norm_mix", "conf_dw_b", "conf_ln_g", "conf_ln_b", "pool_w", "pool_scale", "gmlp_ln_g",
             "gmlp_ln_b", "gmlp_ws", "gmlp_bs", "norm_mlp", "norm_ple", "norm_final")
SMALL_SHARDED = ("conf_dw", "sc_conv")
PACK_LANES = 128
PACK_ROW_MULTIPLE = 256


def _params(*semantics):
    return pltpu.CompilerParams(dimension_semantics=semantics, vmem_limit_bytes=VMEM_LIMIT_BYTES)


def _accumulate(ref, value, first):
    @pl.when(first)
    def _():
        ref[...] = value.astype(ref.dtype)

    @pl.when(jnp.logical_not(first))
    def _():
        ref[...] += value.astype(ref.dtype)


def _rowwise(name, fn, rows, params, outs, accs=(), tm=256):
    n_rows = rows[0][0].shape[-2]
    tm = min(tm, n_rows)
    n_in, n_out = len(rows) + len(params), len(outs)

    def body(*refs):
        res = fn(*[r[...] for r in refs[:n_in]])
        res = res if isinstance(res, (tuple, list)) else (res,)
        out_refs = refs[n_in:]
        for r, v in zip(out_refs[:n_out], res[:n_out]):
            r[...] = v.astype(r.dtype)
        first = pl.program_id(0) == 0
        for r, v in zip(out_refs[n_out:], res[n_out:]):
            _accumulate(r, v, first)

    in_specs = []
    for row in rows:
        arr, w, cb = row[:3]
        if len(row) == 4:
            in_specs.append(pl.BlockSpec((None, tm, w), lambda i, cb=cb, g=row[3]: (g, i, cb)))
        else:
            in_specs.append(pl.BlockSpec((tm, w), lambda i, cb=cb: (i, cb)))
    for p in params:
        in_specs.append(pl.BlockSpec(p.shape, lambda i, nd=p.ndim: (0,) * nd))
    out_specs = [pl.BlockSpec((tm, w), lambda i: (i, 0)) for w, _ in outs]
    out_specs += [pl.BlockSpec(s, lambda i, nd=len(s): (0,) * nd) for s in accs]
    out_shape = [jax.ShapeDtypeStruct((n_rows, w), d) for w, d in outs]
    out_shape += [jax.ShapeDtypeStruct(s, F32) for s in accs]
    return pl.pallas_call(
        body, name=name, grid=(n_rows // tm,), in_specs=in_specs, out_specs=out_specs, out_shape=out_shape,
        compiler_params=_params("arbitrary"),
    )(*[r[0] for r in rows], *params)


def _mm(name, a, b, *, ta=False, tb=False, res=(), epi=None, out_dtypes=(F32,), tm=512, tn=1024, tk=1024):
    g_, m_, k_ = (a.shape[0], a.shape[2], a.shape[1]) if ta else a.shape
    n_ = b.shape[1] if tb else b.shape[2]
    tm, tn, tk = min(tm, m_), min(tn, n_), min(tk, k_)
    nk, n_res, n_out = k_ // tk, len(res), len(out_dtypes)
    dims = (((0 if ta else 1,), (1 if tb else 0,)), ((), ()))
    if epi is None:
        epi = lambda acc: (acc,)

    def body(*refs):
        a_ref, b_ref = refs[:2]
        res_refs, out_refs, acc_ref = refs[2:2 + n_res], refs[2 + n_res:2 + n_res + n_out], refs[-1]
        k = pl.program_id(3)

        @pl.when(k == 0)
        def _():
            acc_ref[...] = jnp.zeros_like(acc_ref)

        acc_ref[...] += lax.dot_general(a_ref[...].astype(BF16), b_ref[...].astype(BF16), dims,
                                        preferred_element_type=F32)

        @pl.when(k == nk - 1)
        def _():
            vals = epi(acc_ref[...], *[r[...] for r in res_refs])
            for o, v in zip(out_refs, vals):
                o[...] = v.astype(o.dtype)

    a_spec = (pl.BlockSpec((None, tk, tm), lambda g, i, j, k: (g, k, i)) if ta
              else pl.BlockSpec((None, tm, tk), lambda g, i, j, k: (g, i, k)))
    b_spec = (pl.BlockSpec((None, tn, tk), lambda g, i, j, k: (g, j, k)) if tb
              else pl.BlockSpec((None, tk, tn), lambda g, i, j, k: (g, k, j)))
    res_specs = [pl.BlockSpec((None, tm, tn), lambda g, i, j, k, off=off: (g, i, j + off)) for _, off in res]
    out_spec = pl.BlockSpec((None, tm, tn), lambda g, i, j, k: (g, i, j))
    outs = pl.pallas_call(
        body, name=name, grid=(g_, m_ // tm, n_ // tn, nk),
        in_specs=[a_spec, b_spec, *res_specs], out_specs=[out_spec] * n_out,
        out_shape=[jax.ShapeDtypeStruct((g_, m_, n_), d) for d in out_dtypes],
        scratch_shapes=[pltpu.VMEM((tm, tn), F32)],
        compiler_params=_params("parallel", "parallel", "parallel", "arbitrary"),
    )(a, b, *[r for r, _ in res])
    return outs


def _mm2(name, a, b, *, res=(), **kw):
    outs = _mm(name, a[None], b[None], res=[(r[None], off) for r, off in res], **kw)
    return [o[0] for o in outs]


@jax.custom_vjp
def _bdot(a, b):
    return jnp.dot(a.astype(BF16), b.astype(BF16), preferred_element_type=F32)


def _bdot_fwd(a, b):
    return _bdot(a, b), (a, b)


def _bdot_bwd(saved, g):
    a, b = saved
    gb = g.astype(BF16)
    da = lax.dot_general(gb, b.astype(BF16), (((1,), (1,)), ((), ())), preferred_element_type=F32)
    db = lax.dot_general(a.astype(BF16), gb, (((0,), (0,)), ((), ())), preferred_element_type=F32)
    return da.astype(a.dtype), db.astype(b.dtype)


_bdot.defvjp(_bdot_fwd, _bdot_bwd)


def _rms(x, g):
    return x * lax.rsqrt(jnp.mean(x * x, axis=-1, keepdims=True) + EPS) * g


def _layer_norm(x, g, b):
    mu = jnp.mean(x, axis=-1, keepdims=True)
    var = jnp.mean(jnp.square(x - mu), axis=-1, keepdims=True)
    return (x - mu) * lax.rsqrt(var + EPS) * g + b


def _glu_and_gate_fn(a, a_gate, cg, hx):
    return a * jax.nn.sigmoid(a_gate), cg * hx


def _branches_fn(ca, pooled, cc, bg, u, v, conf_b, conf_g, conf_beta, pool_scale, gm_g, gm_b, *mats):
    pool_w, ws, bs = mats[0:4], mats[4:8], mats[8:12]
    ya = jax.nn.silu(_layer_norm(ca + conf_b, conf_g, conf_beta))
    yb = jnp.concatenate([_bdot(pooled[:, g * GROUP:(g + 1) * GROUP], pool_w[g]) for g in range(4)], axis=1) * pool_scale
    yc = bg * cc
    vln = _layer_norm(v, gm_g, gm_b)
    causal = lax.broadcasted_iota(jnp.int32, (GROUP, GROUP), 0) >= lax.broadcasted_iota(jnp.int32, (GROUP, GROUP), 1)
    sg = jnp.concatenate(
        [_bdot(jnp.where(causal, ws[g], 0.0), vln[:, g * GROUP:(g + 1) * GROUP]) + bs[g] for g in range(4)], axis=1)
    yd = u * sg
    return ya, yb, yc, yd


def _branches_bwd_fn(*args):
    ins, dys = args[:6] + args[10:], args[6:10]
    _, vjp = jax.vjp(_branches_fn, *ins)
    grads = vjp(tuple(dys))
    return grads


def _glu_bwd_fn(a, a_gate, cg, hx, dga, dchx):
    _, vjp = jax.vjp(_glu_and_gate_fn, a, a_gate, cg, hx)
    return vjp((dga, dchx))


def _rms_bwd_fn(x, dh, dres, g):
    _, vjp = jax.vjp(_rms, x, g)
    dx, dg = vjp(dh)
    return dx + dres, dg


def _loss_fn(x, target, g):
    def f(x, g):
        err = jnp.square(_rms(x, g) - target)
        return 0.5 * jnp.sum(jnp.mean(err, axis=-1, keepdims=True), axis=0, keepdims=True)

    loss, vjp = jax.vjp(f, x, g)
    dx, dg = vjp(jnp.ones((1, 1), F32))
    return dx, dg, jnp.broadcast_to(loss, (1, PACK_LANES))


def _adamw_fn(w, g, m, v):
    m = ADAM_B1 * m + (1.0 - ADAM_B1) * g
    v = ADAM_B2 * v + (1.0 - ADAM_B2) * jnp.square(g)
    m_hat = m / (1.0 - ADAM_B1 ** ADAM_STEP)
    v_hat = v / (1.0 - ADAM_B2 ** ADAM_STEP)
    delta = -ADAM_LR * (m_hat / (jnp.sqrt(v_hat) + ADAM_EPS) + ADAM_WD * w)
    return delta, m, v


def _group_pick(j):
    return [(j == g).astype(F32) for g in range(len(POOL_WINDOWS))]


def _accumulate_row(ref, row, value, first):
    @pl.when(first)
    def _():
        ref[pl.ds(row, 1), :] = value

    @pl.when(jnp.logical_not(first))
    def _():
        ref[pl.ds(row, 1), :] += value


def _rows_iota(n):
    return lax.broadcasted_iota(jnp.int32, (n, 1), 0)


def _seq_specs(n_time, causal):
    per = SEQ_TILE // HALO
    tile = pl.BlockSpec((SEQ_TILE, GROUP), lambda j, i: (i, j))
    if causal:
        halo = pl.BlockSpec((HALO, GROUP), lambda j, i: (jnp.maximum(i * per - 1, 0), j))
    else:
        halo = pl.BlockSpec((HALO, GROUP), lambda j, i: (jnp.minimum((i + 1) * per, n_time * per - 1), j))
    return tile, halo


def _seq_fwd(ga, proj, chx, conf_dw, sc_conv):
    s_len = ga.shape[0]
    n_time = s_len // SEQ_TILE
    ext = HALO + SEQ_TILE

    def body(ga_ref, ga_h, pin_ref, pin_h, chx_ref, chx_h, wc_ref, ws_ref, ca_ref, po_ref, cc_ref):
        j, i = pl.program_id(0), pl.program_id(1)
        keep = (i > 0).astype(F32)

        def extended(cur, halo):
            return jnp.concatenate([halo[...] * keep, cur[...]], axis=0)

        def conv(x, w_ref, taps):
            acc = x * w_ref[pl.ds(taps - 1, 1), :]
            for d in range(1, taps):
                acc = acc + pltpu.roll(x, d, 0) * w_ref[pl.ds(taps - 1 - d, 1), :]
            return acc[HALO:, :]

        ca_ref[...] = conv(extended(ga_ref, ga_h), wc_ref, CONF_KERNEL)
        cc_ref[...] = conv(extended(chx_ref, chx_h), ws_ref, SC_KERNEL)
        u = extended(pin_ref, pin_h)
        s1 = u + pltpu.roll(u, 1, 0)
        s2 = s1 + pltpu.roll(s1, 2, 0)
        s3 = s2 + pltpu.roll(s2, 4, 0)
        s4 = s3 + pltpu.roll(s3, 8, 0)
        pick = _group_pick(j)
        window_sum = s1 * pick[0] + s2 * pick[1] + s3 * pick[2] + s4 * pick[3]
        window = sum(float(w) * f for w, f in zip(POOL_WINDOWS, pick))
        pos = (_rows_iota(ext) + (i * SEQ_TILE - HALO + 1)).astype(F32)
        count = jnp.maximum(jnp.minimum(pos, window), 1.0)
        po_ref[...] = (window_sum / count - u)[HALO:, :]

    tile, halo = _seq_specs(n_time, causal=True)
    pin_tile = pl.BlockSpec((SEQ_TILE, GROUP), lambda j, i: (i, 2 * (MIX // GROUP) + j))
    pin_halo = pl.BlockSpec((HALO, GROUP), lambda j, i: (jnp.maximum(i * (SEQ_TILE // HALO) - 1, 0), 2 * (MIX // GROUP) + j))
    w_spec = lambda taps: pl.BlockSpec((taps, GROUP), lambda j, i: (0, j))
    return pl.pallas_call(
        body, name="seq_fwd", grid=(MIX // GROUP, n_time),
        in_specs=[tile, halo, pin_tile, pin_halo, tile, halo, w_spec(CONF_KERNEL), w_spec(SC_KERNEL)],
        out_specs=[tile] * 3, out_shape=[jax.ShapeDtypeStruct((s_len, MIX), F32)] * 3,
        compiler_params=_params("parallel", "arbitrary"),
    )(ga, ga, proj, proj, chx, chx, conf_dw, sc_conv)


def _seq_bwd(dca, dpooled, dcc, ga, chx, conf_dw, sc_conv):
    s_len = ga.shape[0]
    n_time = s_len // SEQ_TILE
    ext = HALO + SEQ_TILE

    def body(dca_ref, dca_h, dpo_ref, dpo_h, dcc_ref, dcc_h, ga_ref, ga_h, chx_ref, chx_h, wc_ref, ws_ref,
             dga_ref, dpin_ref, dchx_ref, dwc_ref, dws_ref):
        j, i = pl.program_id(0), pl.program_id(1)
        keep_prev = (i > 0).astype(F32)
        keep_next = (i < n_time - 1).astype(F32)

        def with_next(cur, halo):
            return jnp.concatenate([cur[...], halo[...] * keep_next], axis=0)

        def with_prev(cur, halo):
            return jnp.concatenate([halo[...] * keep_prev, cur[...]], axis=0)

        def conv_t(dy, w_ref, taps):
            acc = dy * w_ref[pl.ds(taps - 1, 1), :]
            for d in range(1, taps):
                acc = acc + pltpu.roll(dy, ext - d, 0) * w_ref[pl.ds(taps - 1 - d, 1), :]
            return acc[:SEQ_TILE, :]

        def tap_grads(dy_cur, x, dw_ref, taps):
            for d in range(taps):
                shifted = x if d == 0 else pltpu.roll(x, d, 0)
                part = jnp.sum(dy_cur * shifted[HALO:, :], axis=0, keepdims=True)
                _accumulate_row(dw_ref, taps - 1 - d, part, i == 0)

        dga_ref[...] = conv_t(with_next(dca_ref, dca_h), wc_ref, CONF_KERNEL)
        dchx_ref[...] = conv_t(with_next(dcc_ref, dcc_h), ws_ref, SC_KERNEL)
        tap_grads(dca_ref[...], with_prev(ga_ref, ga_h), dwc_ref, CONF_KERNEL)
        tap_grads(dcc_ref[...], with_prev(chx_ref, chx_h), dws_ref, SC_KERNEL)

        dpo = with_next(dpo_ref, dpo_h)
        pick = _group_pick(j)
        window = sum(float(w) * f for w, f in zip(POOL_WINDOWS, pick))
        pos = (_rows_iota(ext) + (i * SEQ_TILE + 1)).astype(F32)
        r = dpo / jnp.minimum(pos, window)
        q1 = r + pltpu.roll(r, ext - 1, 0)
        q2 = q1 + pltpu.roll(q1, ext - 2, 0)
        q3 = q2 + pltpu.roll(q2, ext - 4, 0)
        q4 = q3 + pltpu.roll(q3, ext - 8, 0)
        q = q1 * pick[0] + q2 * pick[1] + q3 * pick[2] + q4 * pick[3]
        dpin_ref[...] = (q - dpo)[:SEQ_TILE, :].astype(dpin_ref.dtype)

    tile, nxt = _seq_specs(n_time, causal=False)
    _, prv = _seq_specs(n_time, causal=True)
    w_spec = lambda taps: pl.BlockSpec((taps, GROUP), lambda j, i: (0, j))
    return pl.pallas_call(
        body, name="seq_bwd", grid=(MIX // GROUP, n_time),
        in_specs=[tile, nxt, tile, nxt, tile, nxt, tile, prv, tile, prv, w_spec(CONF_KERNEL), w_spec(SC_KERNEL)],
        out_specs=[tile, tile, tile, w_spec(CONF_KERNEL), w_spec(SC_KERNEL)],
        out_shape=[jax.ShapeDtypeStruct((s_len, MIX), F32), jax.ShapeDtypeStruct((s_len, MIX), BF16),
                   jax.ShapeDtypeStruct((s_len, MIX), F32), jax.ShapeDtypeStruct(conf_dw.shape, F32),
                   jax.ShapeDtypeStruct(sc_conv.shape, F32)],
        compiler_params=_params("parallel", "arbitrary"),
    )(dca, dca, dpooled, dpooled, dcc, dcc, ga, ga, chx, chx, conf_dw, sc_conv)


MERGE_TILE = 512


def _merge_fwd(y, w_branch, proj):
    s_len, d_model = y.shape[1], w_branch.shape[2]
    t = min(MERGE_TILE, s_len)
    gate_block0 = (proj.shape[1] - 4 * d_model) // MERGE_TILE
    per_branch = d_model // MERGE_TILE

    def body(y_ref, w_ref, gate_ref, out_ref, acc_ref):
        k = pl.program_id(2)

        @pl.when(k == 0)
        def _():
            acc_ref[...] = jnp.zeros_like(acc_ref)

        z = jnp.dot(y_ref[...], w_ref[...], preferred_element_type=F32)
        acc_ref[...] += jax.nn.sigmoid(gate_ref[...]) * z

        @pl.when(k == 3)
        def _():
            out_ref[...] = acc_ref[...].astype(out_ref.dtype)

    return pl.pallas_call(
        body, name="merge_fwd", grid=(s_len // t, per_branch, 4),
        in_specs=[pl.BlockSpec((None, t, MIX), lambda i, j, k: (k, i, 0)),
                  pl.BlockSpec((None, MIX, MERGE_TILE), lambda i, j, k: (k, 0, j)),
                  pl.BlockSpec((t, MERGE_TILE), lambda i, j, k: (i, gate_block0 + k * per_branch + j))],
        out_specs=pl.BlockSpec((t, MERGE_TILE), lambda i, j, k: (i, j)),
        out_shape=jax.ShapeDtypeStruct((s_len, d_model), BF16),
        scratch_shapes=[pltpu.VMEM((t, MERGE_TILE), F32)],
        compiler_params=_params("parallel", "parallel", "arbitrary"),
    )(y, w_branch, proj)


def _merge_bwd(y, w_branch, proj, dmerged):
    s_len, d_model = y.shape[1], w_branch.shape[2]
    t = min(MERGE_TILE, s_len)
    gate_block0 = (proj.shape[1] - 4 * d_model) // MERGE_TILE
    per_branch = d_model // MERGE_TILE

    def body(y_ref, w_ref, gate_ref, dm_ref, dz_ref, dgate_ref):
        z = jnp.dot(y_ref[...], w_ref[...], preferred_element_type=F32)
        gate = jax.nn.sigmoid(gate_ref[...])
        dm = dm_ref[...]
        dz_ref[...] = (dm * gate).astype(dz_ref.dtype)
        dgate_ref[...] = (dm * z * gate * (1.0 - gate)).astype(dgate_ref.dtype)

    return pl.pallas_call(
        body, name="merge_bwd", grid=(s_len // t, per_branch, 4),
        in_specs=[pl.BlockSpec((None, t, MIX), lambda i, j, k: (k, i, 0)),
                  pl.BlockSpec((None, MIX, MERGE_TILE), lambda i, j, k: (k, 0, j)),
                  pl.BlockSpec((t, MERGE_TILE), lambda i, j, k: (i, gate_block0 + k * per_branch + j)),
                  pl.BlockSpec((t, MERGE_TILE), lambda i, j, k: (i, j))],
        out_specs=[pl.BlockSpec((None, t, MERGE_TILE), lambda i, j, k: (k, i, j)),
                   pl.BlockSpec((t, MERGE_TILE), lambda i, j, k: (i, k * per_branch + j))],
        out_shape=[jax.ShapeDtypeStruct((4, s_len, d_model), BF16), jax.ShapeDtypeStruct((s_len, 4 * d_model), BF16)],
        compiler_params=_params("parallel", "parallel", "arbitrary"),
    )(y, w_branch, proj, dmerged)


def _row(v):
    return v.reshape(1, -1)


def _branch_params(w):
    mats = [w["pool_w"][g] for g in range(4)] + [w["gmlp_ws"][g] for g in range(4)]
    mats += [w["gmlp_bs"][g].reshape(GROUP, 1) for g in range(4)]
    return [_row(w["conf_dw_b"]), _row(w["conf_ln_g"]), _row(w["conf_ln_b"]), _row(w["pool_scale"]),
            _row(w["gmlp_ln_g"]), _row(w["gmlp_ln_b"])] + mats


def _layer_fwd(x0, p_l, w):
    d_model = x0.shape[1]
    (h1,) = _rowwise("rms_mix", _rms, [(x0, d_model, 0)], [_row(w["norm_mix"])], [(d_model, BF16)])
    (proj,) = _mm2("proj", h1, w["w_in"])
    ga, chx = _rowwise("glu", _glu_and_gate_fn, [(proj, MIX, 0), (proj, MIX, 1), (proj, MIX, 4), (proj, MIX, 5)], [],
                       [(MIX, F32), (MIX, F32)])
    ca, pooled, cc = _seq_fwd(ga, proj, chx, w["conf_dw"], w["sc_conv"])
    ys = _rowwise("branches", _branches_fn,
                  [(ca, MIX, 0), (pooled, MIX, 0), (cc, MIX, 0), (proj, MIX, 3), (proj, MIX, 6), (proj, MIX, 7)],
                  _branch_params(w), [(MIX, BF16)] * 4, tm=GROUP)
    y = jnp.stack(ys)
    merged = _merge_fwd(y, w["w_branch"], proj)
    (x1,) = _mm2("out_proj", merged, w["w_out"], res=[(x0, 0)], epi=lambda acc, r: (acc + r,))
    (h2,) = _rowwise("rms_mlp", _rms, [(x1, d_model, 0)], [_row(w["norm_mlp"])], [(d_model, BF16)])
    up, act = _mm2("mlp_up", h2, w["w_up"], out_dtypes=(F32, BF16),
                   epi=lambda acc: (acc, jnp.square(jnp.maximum(acc, 0.0))))
    (x2,) = _mm2("mlp_down", act, w["w_down"], res=[(x1, 0)], epi=lambda acc, r: (acc + r,))
    (h3,) = _rowwise("rms_ple", _rms, [(x2, d_model, 0)], [_row(w["norm_ple"])], [(d_model, BF16)])
    (pg,) = _mm2("ple_gate", h3, w["w_ple_gate"])
    (x3,) = _mm2("ple", p_l, w["w_ple"], res=[(pg, 0), (x2, 0)],
                 epi=lambda acc, g, r: (acc * jax.nn.sigmoid(g) + r,))
    saved = dict(x0=x0, h1=h1, proj=proj, ga=ga, chx=chx, ca=ca, pooled=pooled, cc=cc, y=y, merged=merged, x1=x1,
                 h2=h2, up=up, act=act, x2=x2, h3=h3, pg=pg)
    return x3, saved


def _layer_bwd(dx3, p_l, w, s):
    d_model = dx3.shape[1]
    big, small = {}, {}

    def ple_epi(acc, dx, g):
        sig = jax.nn.sigmoid(g)
        return dx * sig, dx * acc * sig * (1.0 - sig)

    dpp, dpg = _mm2("ple_bwd", p_l, w["w_ple"], res=[(dx3, 0), (s["pg"], 0)], epi=ple_epi, out_dtypes=(BF16, BF16))
    (big["w_ple"],) = _mm2("dw_ple", p_l, dpp, ta=True, out_dtypes=(BF16,))
    (big["w_ple_gate"],) = _mm2("dw_ple_gate", s["h3"], dpg, ta=True, out_dtypes=(BF16,))
    (dh3,) = _mm2("dh_ple", dpg, w["w_ple_gate"], tb=True)
    dx2, small["norm_ple"] = _rowwise("rms_ple_bwd", _rms_bwd_fn, [(s["x2"], d_model, 0), (dh3, d_model, 0), (dx3, d_model, 0)],
                                      [_row(w["norm_ple"])], [(d_model, F32)], [(1, d_model)])

    (dup,) = _mm2("mlp_down_bwd", dx2, w["w_down"], tb=True, res=[(s["up"], 0)], out_dtypes=(BF16,),
                  epi=lambda acc, up: (acc * 2.0 * jnp.maximum(up, 0.0),))
    (big["w_down"],) = _mm2("dw_down", s["act"], dx2, ta=True, out_dtypes=(BF16,))
    (big["w_up"],) = _mm2("dw_up", s["h2"], dup, ta=True, out_dtypes=(BF16,))
    (dh2,) = _mm2("mlp_up_bwd", dup, w["w_up"], tb=True)
    dx1, small["norm_mlp"] = _rowwise("rms_mlp_bwd", _rms_bwd_fn, [(s["x1"], d_model, 0), (dh2, d_model, 0), (dx2, d_model, 0)],
                                      [_row(w["norm_mlp"])], [(d_model, F32)], [(1, d_model)])

    (dmerged,) = _mm2("out_proj_bwd", dx1, w["w_out"], tb=True)
    (big["w_out"],) = _mm2("dw_out", s["merged"], dx1, ta=True, out_dtypes=(BF16,))
    dz, dgate = _merge_bwd(s["y"], w["w_branch"], s["proj"], dmerged)
    (dy,) = _mm("branch_bwd", dz, w["w_branch"], tb=True)
    (big["w_branch"],) = _mm("dw_branch", s["y"], dz, ta=True, out_dtypes=(BF16,))

    proj = s["proj"]
    params = _branch_params(w)
    grads = _rowwise(
        "branches_bwd", _branches_bwd_fn,
        [(s["ca"], MIX, 0), (s["pooled"], MIX, 0), (s["cc"], MIX, 0), (proj, MIX, 3), (proj, MIX, 6), (proj, MIX, 7)]
        + [(dy, MIX, 0, g) for g in range(4)], params,
        [(MIX, F32), (MIX, F32), (MIX, F32), (MIX, BF16), (MIX, BF16), (MIX, BF16)], [p.shape for p in params], tm=GROUP)
    dca, dpooled, dcc, dbg, du, dv = grads[:6]
    pg_ = grads[6:]
    small["conf_dw_b"], small["conf_ln_g"], small["conf_ln_b"], small["pool_scale"] = pg_[0], pg_[1], pg_[2], pg_[3]
    small["gmlp_ln_g"], small["gmlp_ln_b"] = pg_[4], pg_[5]
    small["pool_w"] = jnp.stack(pg_[6:10])
    small["gmlp_ws"] = jnp.stack(pg_[10:14])
    small["gmlp_bs"] = jnp.stack([b.reshape(GROUP) for b in pg_[14:18]])

    dga, dpin, dchx, small["conf_dw"], small["sc_conv"] = _seq_bwd(dca, dpooled, dcc, s["ga"], s["chx"], w["conf_dw"], w["sc_conv"])
    da, dag, dcg, dhx = _rowwise("glu_bwd", _glu_bwd_fn,
                                 [(proj, MIX, 0), (proj, MIX, 1), (proj, MIX, 4), (proj, MIX, 5), (dga, MIX, 0), (dchx, MIX, 0)],
                                 [], [(MIX, BF16)] * 4)
    dproj = jnp.concatenate([da, dag, dpin, dbg, dcg, dhx, du, dv, dgate], axis=1)
    (big["w_in"],) = _mm2("dw_in", s["h1"], dproj, ta=True, out_dtypes=(BF16,))
    (dh1,) = _mm2("proj_bwd", dproj, w["w_in"], tb=True)
    dx0, small["norm_mix"] = _rowwise("rms_mix_bwd", _rms_bwd_fn, [(s["x0"], d_model, 0), (dh1, d_model, 0), (dx1, d_model, 0)],
                                      [_row(w["norm_mix"])], [(d_model, F32)], [(1, d_model)])
    return dx0, big, small


def _position():
    return lax.axis_index("x"), lax.axis_index("y"), lax.axis_index("c")


def _other_chips(x, y):
    return [(1 - x, y), (x, 1 - y), (1 - x, 1 - y)]


def _region(ref, cuts):
    return ref.at[tuple(pl.ds(*cuts[a]) if a in cuts else slice(None) for a in range(len(ref.shape)))]


def _allgather_small(name, block):
    m_per, n = block.shape

    def body(x_ref, out_ref, send_sems, recv_sems, local_sem):
        x, y, c = _position()
        me, sibling = (x, y, c), (x, y, 1 - c)
        chips = _other_chips(x, y)

        def rows(px, py, pc):
            return out_ref.at[pl.ds((4 * px + 2 * py + pc) * m_per, m_per), :]

        def copy(k, blk, to, src=None):
            return pltpu.make_async_remote_copy(
                src_ref=rows(*blk) if src is None else src, dst_ref=rows(*blk), send_sem=send_sems.at[k],
                recv_sem=recv_sems.at[k], device_id=to, device_id_type=pl.DeviceIdType.MESH)

        mine = pltpu.make_async_copy(x_ref, rows(*me), local_sem)
        mine.start()
        first = [copy(0, me, sibling, src=x_ref)]
        first += [copy(1 + j, me, (*chip, c), src=x_ref) for j, chip in enumerate(chips)]
        for cp in first:
            cp.start()
        passed = [copy(4 + j, (*chip, c), sibling) for j, chip in enumerate(chips)]
        for j, chip in enumerate(chips):
            copy(1 + j, (*chip, c), me).wait_recv()
            passed[j].start()
        copy(0, sibling, me).wait_recv()
        for j, chip in enumerate(chips):
            copy(4 + j, (*chip, 1 - c), me).wait_recv()
        for cp in first + passed:
            cp.wait_send()
        mine.wait()

    return pl.pallas_call(
        body, name=name, out_shape=jax.ShapeDtypeStruct((8 * m_per, n), block.dtype), in_specs=[ANY], out_specs=ANY,
        scratch_shapes=[pltpu.SemaphoreType.DMA((7,)), pltpu.SemaphoreType.DMA((7,)), pltpu.SemaphoreType.DMA],
    )(block)


def _cuts(arr_shape, shard_axis, half_axis, shard=None, half=None):
    cuts = {}
    if shard is not None:
        n = arr_shape[shard_axis] // 4
        cuts[shard_axis] = (shard * n, n)
    if half is not None:
        n = arr_shape[half_axis] // 2
        cuts[half_axis] = (half * n, n)
    return cuts


def _full_shape(shard_shape, shard_axis):
    return tuple(4 * n if a == shard_axis else n for a, n in enumerate(shard_shape))


def _allgather_weights(shards):
    n_w = len(BIG)
    full_shapes = [_full_shape(sh.shape, sa) for sh, (_, sa, _) in zip(shards, BIG)]

    def body(*refs):
        src, dst = refs[:n_w], refs[n_w:2 * n_w]
        ici_send, ici_recv, d2d_send, d2d_recv, local_sems = refs[2 * n_w:]
        x, y, c = _position()
        chips = _other_chips(x, y)
        my_shard = 2 * x + y
        local, sent, passed = [], [], []
        for wi, (_, sa, ha) in enumerate(BIG):
            shape = full_shapes[wi]
            cp = pltpu.make_async_copy(src[wi], _region(dst[wi], _cuts(shape, sa, ha, shard=my_shard)), local_sems.at[wi])
            cp.start()
            local.append(cp)
            for j, chip in enumerate(chips):
                half_src = _region(src[wi], {ha: (c * (shape[ha] // 2), shape[ha] // 2)})
                cp = pltpu.make_async_remote_copy(
                    src_ref=half_src, dst_ref=_region(dst[wi], _cuts(shape, sa, ha, shard=my_shard, half=c)),
                    send_sem=ici_send.at[3 * wi + j], recv_sem=ici_recv.at[3 * wi + j],
                    device_id=(*chip, c), device_id_type=pl.DeviceIdType.MESH)
                cp.start()
                sent.append(cp)
        for wi, (_, sa, ha) in enumerate(BIG):
            shape = full_shapes[wi]
            for j, (px, py) in enumerate(chips):
                landed = _region(dst[wi], _cuts(shape, sa, ha, shard=2 * px + py, half=c))
                pltpu.make_async_remote_copy(
                    src_ref=landed, dst_ref=landed, send_sem=ici_send.at[3 * wi + j], recv_sem=ici_recv.at[3 * wi + j],
                    device_id=(px, py, c), device_id_type=pl.DeviceIdType.MESH).wait_recv()
                cp = pltpu.make_async_remote_copy(
                    src_ref=landed, dst_ref=landed, send_sem=d2d_send.at[3 * wi + j], recv_sem=d2d_recv.at[3 * wi + j],
                    device_id=(x, y, 1 - c), device_id_type=pl.DeviceIdType.MESH)
                cp.start()
                passed.append(cp)
        for wi, (_, sa, ha) in enumerate(BIG):
            shape = full_shapes[wi]
            for j, (px, py) in enumerate(chips):
                other_half = _region(dst[wi], _cuts(shape, sa, ha, shard=2 * px + py, half=1 - c))
                pltpu.make_async_remote_copy(
                    src_ref=other_half, dst_ref=other_half, send_sem=d2d_send.at[3 * wi + j], recv_sem=d2d_recv.at[3 * wi + j],
                    device_id=(x, y, 1 - c), device_id_type=pl.DeviceIdType.MESH).wait_recv()
        for cp in sent + passed:
            cp.wait_send()
        for cp in local:
            cp.wait()

    return pl.pallas_call(
        body, name="allgather_weights", in_specs=[ANY] * n_w, out_specs=[ANY] * n_w,
        out_shape=[jax.ShapeDtypeStruct(s, BF16) for s in full_shapes],
        scratch_shapes=[pltpu.SemaphoreType.DMA((3 * n_w,))] * 4 + [pltpu.SemaphoreType.DMA((n_w,))],
    )(*shards)


def _exchange(name, inputs, out_shapes, plan):
    n_in = len(inputs)
    remote0, local0 = plan([None] * n_in, [None] * len(out_shapes), None)

    def body(*refs):
        ins, outs = refs[:n_in], refs[n_in:n_in + len(out_shapes)]
        send_sems, recv_sems, local_sems = refs[n_in + len(out_shapes):]
        remote, local = plan(ins, outs, _position())
        local_copies = [pltpu.make_async_copy(s, d, local_sems.at[k]) for k, (s, d) in enumerate(local)]
        for cp in local_copies:
            cp.start()
        copies = [pltpu.make_async_remote_copy(src_ref=s, dst_ref=d, send_sem=send_sems.at[k], recv_sem=recv_sems.at[k],
                                               device_id=peer, device_id_type=pl.DeviceIdType.MESH)
                  for k, (s, d, peer) in enumerate(remote)]
        for cp in copies:
            cp.start()
        for cp in copies:
            cp.wait()
        for cp in local_copies:
            cp.wait()

    return pl.pallas_call(
        body, name=name, in_specs=[ANY] * n_in, out_specs=[ANY] * len(out_shapes), out_shape=out_shapes,
        scratch_shapes=[pltpu.SemaphoreType.DMA((len(remote0),)), pltpu.SemaphoreType.DMA((len(remote0),)),
                        pltpu.SemaphoreType.DMA((max(len(local0), 1),))],
    )(*inputs)


def _half_shape(shape, half_axis):
    return tuple(n // 2 if a == half_axis else n for a, n in enumerate(shape))


def _shard_shape(shape, shard_axis):
    return tuple(n // 4 if a == shard_axis else n for a, n in enumerate(shape))


def _flat2d(a):
    return a.reshape(-1, a.shape[-1])


ELEMENTWISE_TILE_ELEMENTS = 1 << 18


def _row_tile(rows, cols):
    tile = 1
    while rows % (2 * tile) == 0 and 2 * tile * cols <= ELEMENTWISE_TILE_ELEMENTS:
        tile *= 2
    assert tile >= 16 or tile == rows, (rows, cols)
    return tile


def _sum_arrays(name, arrays, out_dtype):
    shape = arrays[0].shape
    flat = [_flat2d(a) for a in arrays]
    rows, cols = flat[0].shape
    tm = _row_tile(rows, cols)

    def fn(*tiles):
        acc = tiles[0].astype(F32)
        for t in tiles[1:]:
            acc = acc + t.astype(F32)
        return acc

    (out,) = _rowwise(name, fn, [(a, cols, 0) for a in flat], [], [(cols, out_dtype)], tm=tm)
    return out.reshape(shape)


def _reduce_scatter_grads(grads):
    x, y, c = _position()
    my_shard = 2 * x + y
    shapes = [g.shape for g in grads]

    def to_sibling(ins, outs, pos):
        remote = []
        for wi, (_, sa, ha) in enumerate(BIG):
            if pos is None:
                remote.append(None)
                continue
            px, py, pc = pos
            remote.append((_region(ins[wi], _cuts(shapes[wi], sa, ha, half=1 - pc)), outs[wi], (px, py, 1 - pc)))
        return remote, []

    landed = _exchange("grads_to_sibling", grads, [jax.ShapeDtypeStruct(_half_shape(s, ha), BF16) for s, (_, _, ha) in zip(shapes, BIG)],
                       to_sibling)
    chip_sums = []
    for wi, (name, sa, ha) in enumerate(BIG):
        n = shapes[wi][ha] // 2
        mine = lax.dynamic_slice_in_dim(grads[wi], c * n, n, axis=ha)
        chip_sums.append(_sum_arrays("chip_sum_" + name, [mine, landed[wi]], BF16))
    half_shapes = [t.shape for t in chip_sums]

    def to_chips(ins, outs, pos):
        remote = []
        for wi, (_, sa, ha) in enumerate(BIG):
            for j in range(3):
                if pos is None:
                    remote.append(None)
                    continue
                px, py, pc = pos
                qx, qy = _other_chips(px, py)[j]
                piece = _region(ins[wi], _cuts(shapes[wi], sa, ha, shard=2 * qx + qy))
                remote.append((piece, outs[wi].at[j], (qx, qy, pc)))
        return remote, []

    piece_shapes = [_shard_shape(s, sa) for s, (_, sa, _) in zip(half_shapes, BIG)]
    pieces = _exchange("grads_to_chips", chip_sums, [jax.ShapeDtypeStruct((3, *s), BF16) for s in piece_shapes], to_chips)
    reduced = []
    for wi, (name, sa, ha) in enumerate(BIG):
        n = shapes[wi][sa] // 4
        mine = lax.dynamic_slice_in_dim(chip_sums[wi], my_shard * n, n, axis=sa)
        reduced.append(_sum_arrays("shard_sum_" + name, [mine, pieces[wi][0], pieces[wi][1], pieces[wi][2]], F32))

    def halves_to_sibling(ins, outs, pos):
        if pos is None:
            return [None] * len(BIG), []
        px, py, pc = pos
        return [(ins[wi], outs[wi], (px, py, 1 - pc)) for wi in range(len(BIG))], []

    from_sibling = _exchange("grad_halves_to_sibling", reduced, [jax.ShapeDtypeStruct(r.shape, F32) for r in reduced],
                             halves_to_sibling)
    out = []
    for wi, (_, sa, ha) in enumerate(BIG):
        low = jnp.where(c == 0, reduced[wi], from_sibling[wi])
        high = jnp.where(c == 0, from_sibling[wi], reduced[wi])
        out.append(jnp.concatenate([low, high], axis=ha))
    return out


def _pack(arrays):
    flat = jnp.concatenate([a.reshape(-1).astype(F32) for a in arrays])
    rows = -(-flat.shape[0] // PACK_LANES)
    rows = -(-rows // PACK_ROW_MULTIPLE) * PACK_ROW_MULTIPLE
    return jnp.pad(flat, (0, rows * PACK_LANES - flat.shape[0])).reshape(rows, PACK_LANES)


def _unpack(buf, shapes):
    flat, out, off = buf.reshape(-1), [], 0
    for s in shapes:
        n = 1
        for d in s:
            n *= d
        out.append(flat[off:off + n].reshape(s))
        off += n
    return out


def _sum8(name, gathered, rows):
    def body(in_ref, out_ref):
        acc = in_ref[0]
        for d in range(1, 8):
            acc = acc + in_ref[d]
        out_ref[...] = acc

    return pl.pallas_call(
        body, name=name, grid=(rows // PACK_ROW_MULTIPLE,),
        in_specs=[pl.BlockSpec((8, PACK_ROW_MULTIPLE, PACK_LANES), lambda i: (0, i, 0))],
        out_specs=pl.BlockSpec((PACK_ROW_MULTIPLE, PACK_LANES), lambda i: (i, 0)),
        out_shape=jax.ShapeDtypeStruct((rows, PACK_LANES), F32), compiler_params=_params("arbitrary"),
    )(gathered.reshape(8, rows, PACK_LANES))


def _adamw(name, w, g, m, v):
    shape = w.shape
    flat = [_flat2d(a) for a in (w, g, m, v)]
    rows, cols = flat[0].shape
    tm = _row_tile(rows, cols)
    outs = _rowwise(name, _adamw_fn, [(a, cols, 0) for a in flat], [], [(cols, F32)] * 3, tm=tm)
    return [o.reshape(shape) for o in outs]


WEIGHT_ORDER = ("norm_mix", "w_in", "conf_dw", "conf_dw_b", "conf_ln_g", "conf_ln_b", "pool_w", "pool_scale", "sc_conv",
                "gmlp_ln_g", "gmlp_ln_b", "gmlp_ws", "gmlp_bs", "w_branch", "w_out", "norm_mlp", "w_up", "w_down",
                "norm_ple", "w_ple", "w_ple_gate", "norm_final")


def _local_step(x, p, loss_target, weights, full_big, full_small_sharded):
    n_layers = p.shape[0]
    d_model = x.shape[1]
    layer_w = []
    for l in range(n_layers):
        w = {n: weights[n][l] for n in SMALL_REP if n != "norm_final"}
        w.update({n: full_big[n][l] for n, _, _ in BIG})
        w.update({n: full_small_sharded[n][l] for n in SMALL_SHARDED})
        layer_w.append(w)
    saved = []
    h = x
    for l in range(n_layers):
        h, s = _layer_fwd(h, p[l], layer_w[l])
        saved.append(s)
    dx, d_final, loss = _rowwise("loss", _loss_fn, [(h, d_model, 0), (loss_target, d_model, 0)], [_row(weights["norm_final"])],
                                 [(d_model, F32)], [(1, d_model), (1, PACK_LANES)])
    big_grads = {n: [None] * n_layers for n, _, _ in BIG}
    small_grads = {n: [None] * n_layers for n in SMALL_REP + SMALL_SHARDED if n != "norm_final"}
    for l in reversed(range(n_layers)):
        dx, big, small = _layer_bwd(dx, p[l], layer_w[l], saved[l])
        for n in big:
            big_grads[n][l] = big[n]
        for n in small:
            small_grads[n][l] = small[n].reshape(weights[n].shape[1:]) if n in SMALL_REP else small[n]
    big_grads = {n: jnp.stack(v) for n, v in big_grads.items()}
    small_grads = {n: jnp.stack(v) for n, v in small_grads.items()}
    small_grads["norm_final"] = d_final.reshape(-1)
    return loss, dx, big_grads, small_grads


def kernel(x, p, norm_mix, w_in, conf_dw, conf_dw_b, conf_ln_g, conf_ln_b, pool_w, pool_scale, sc_conv, gmlp_ln_g, gmlp_ln_b, gmlp_ws, gmlp_bs, w_branch, w_out, norm_mlp, w_up, w_down, norm_ple, w_ple, w_ple_gate, norm_final, loss_target, m_norm_mix, m_w_in, m_conf_dw, m_conf_dw_b, m_conf_ln_g, m_conf_ln_b, m_pool_w, m_pool_scale, m_sc_conv, m_gmlp_ln_g, m_gmlp_ln_b, m_gmlp_ws, m_gmlp_bs, m_w_branch, m_w_out, m_norm_mlp, m_w_up, m_w_down, m_norm_ple, m_w_ple, m_w_ple_gate, m_norm_final, v_norm_mix, v_w_in, v_conf_dw, v_conf_dw_b, v_conf_ln_g, v_conf_ln_b, v_pool_w, v_pool_scale, v_sc_conv, v_gmlp_ln_g, v_gmlp_ln_b, v_gmlp_ws, v_gmlp_bs, v_w_branch, v_w_out, v_norm_mlp, v_w_up, v_w_down, v_norm_ple, v_w_ple, v_w_ple_gate, v_norm_final):
    given = dict(locals())
    weights = {n: given[n] for n in WEIGHT_ORDER}
    mom_m = {n: given["m_" + n] for n in WEIGHT_ORDER}
    mom_v = {n: given["v_" + n] for n in WEIGHT_ORDER}
    my_shard = 2 * lax.axis_index("x") + lax.axis_index("y")

    full = _allgather_weights([weights[n].astype(BF16) for n, _, _ in BIG])
    full_big = {n: f for (n, _, _), f in zip(BIG, full)}
    shard_shapes = [weights[n].shape for n in SMALL_SHARDED]
    gathered = _allgather_small("allgather_small_weights", _pack([weights[n] for n in SMALL_SHARDED]))
    per_dev = gathered.reshape(8, -1, PACK_LANES)
    full_small_sharded = {}
    chip_parts = [_unpack(per_dev[2 * s], shard_shapes) for s in range(4)]
    for i, n in enumerate(SMALL_SHARDED):
        full_small_sharded[n] = jnp.concatenate([chip_parts[s][i] for s in range(4)], axis=-1)

    loss_row, dx, big_grads, small_grads = _local_step(x[0], p[:, 0], loss_target[0], weights, full_big, full_small_sharded)
    loss = lax.psum(loss_row[0, 0], MESH_AXES)

    big_reduced = _reduce_scatter_grads([big_grads[n] for n, _, _ in BIG])
    grads = {n: g for (n, _, _), g in zip(BIG, big_reduced)}

    small_names = SMALL_REP + SMALL_SHARDED
    packed = _pack([small_grads[n] for n in small_names])
    rows = packed.shape[0]
    summed = _sum8("sum_small_grads", _allgather_small("allgather_small_grads", packed), rows)
    small_full_shapes = [weights[n].shape for n in SMALL_REP] + [full_small_sharded[n].shape for n in SMALL_SHARDED]
    for n, g in zip(small_names, _unpack(summed, small_full_shapes)):
        if n in SMALL_SHARDED:
            width = weights[n].shape[-1]
            g = lax.dynamic_slice_in_dim(g, my_shard * width, width, axis=g.ndim - 1)
        grads[n] = g

    delta, new_m, new_v = {}, {}, {}
    for n, _, _ in BIG:
        delta[n], new_m[n], new_v[n] = _adamw("adamw_" + n, weights[n], grads[n], mom_m[n], mom_v[n])
    for group, names in (("rep", SMALL_REP), ("sharded", SMALL_SHARDED)):
        outs = _adamw("adamw_small_" + group, *[_pack([src[n] for n in names]) for src in (weights, grads, mom_m, mom_v)])
        shapes = [weights[n].shape for n in names]
        for dst, buf in zip((delta, new_m, new_v), outs):
            for n, a in zip(names, _unpack(buf, shapes)):
                dst[n] = a

    return (loss, dx[None], *[grads[n] for n in WEIGHT_ORDER], *[delta[n] for n in WEIGHT_ORDER],
            *[new_m[n] for n in WEIGHT_ORDER], *[new_v[n] for n in WEIGHT_ORDER])
```

```python
import functools

import jax
import jax.numpy as jnp
from jax import lax
from jax.experimental import pallas as pl
from jax.experimental.pallas import tpu as pltpu

F32, BF16 = jnp.float32, jnp.bfloat16
EPS = 1e-6
ADAM_LR, ADAM_B1, ADAM_B2, ADAM_EPS, ADAM_WD, ADAM_STEP = 0.001, 0.9, 0.999, 1e-08, 0.01, 10
CONF_KERNEL, SC_KERNEL = 31, 3
POOL_WINDOWS = (2, 4, 8, 16)
MIX = 512
GROUP = 128
HALO = 32
SEQ_TILE = 256
VMEM_LIMIT_BYTES = 48 * 1024 * 1024
MESH_AXES = ("x", "y", "c")
ANY = pl.BlockSpec(memory_space=pl.ANY)

BIG = (("w_in", 2, 1), ("w_branch", 3, 1), ("w_out", 1, 2), ("w_up", 2, 1), ("w_down", 1, 2),
       ("w_ple", 2, 1), ("w_ple_gate", 1, 2))
SMALL_REP = ("norm_mix", "conf_dw_b", "conf_ln_g", "conf_ln_b", "pool_w", "pool_scale", "gmlp_ln_g",
             "gmlp_ln_b", "gmlp_ws", "gmlp_bs", "norm_mlp", "norm_ple", "norm_final")
SMALL_SHARDED = ("conf_dw", "sc_conv")
PACK_LANES = 128
PACK_ROW_MULTIPLE = 256


def _params(*semantics):
    return pltpu.CompilerParams(dimension_semantics=semantics, vmem_limit_bytes=VMEM_LIMIT_BYTES)


def _accumulate(ref, value, first):
    @pl.when(first)
    def _():
        ref[...] = value.astype(ref.dtype)

    @pl.when(jnp.logical_not(first))
    def _():
        ref[...] += value.astype(ref.dtype)


def _rowwise(name, fn, rows, params, outs, accs=(), tm=256):
    n_rows = rows[0][0].shape[-2]
    tm = min(tm, n_rows)
    n_in, n_out = len(rows) + len(params), len(outs)

    def body(*refs):
        res = fn(*[r[...] for r in refs[:n_in]])
        res = res if isinstance(res, (tuple, list)) else (res,)
        out_refs = refs[n_in:]
        for r, v in zip(out_refs[:n_out], res[:n_out]):
            r[...] = v.astype(r.dtype)
        first = pl.program_id(0) == 0
        for r, v in zip(out_refs[n_out:], res[n_out:]):
            _accumulate(r, v, first)

    in_specs = []
    for row in rows:
        arr, w, cb = row[:3]
        if len(row) == 4:
            in_specs.append(pl.BlockSpec((None, tm, w), lambda i, cb=cb, g=row[3]: (g, i, cb)))
        else:
            in_specs.append(pl.BlockSpec((tm, w), lambda i, cb=cb: (i, cb)))
    for p in params:
        in_specs.append(pl.BlockSpec(p.shape, lambda i, nd=p.ndim: (0,) * nd))
    out_specs = [pl.BlockSpec((tm, w), lambda i: (i, 0)) for w, _ in outs]
    out_specs += [pl.BlockSpec(s, lambda i, nd=len(s): (0,) * nd) for s in accs]
    out_shape = [jax.ShapeDtypeStruct((n_rows, w), d) for w, d in outs]
    out_shape += [jax.ShapeDtypeStruct(s, F32) for s in accs]
    return pl.pallas_call(
        body, name=name, grid=(n_rows // tm,), in_specs=in_specs, out_specs=out_specs, out_shape=out_shape,
        compiler_params=_params("arbitrary"),
    )(*[r[0] for r in rows], *params)


def _mm(name, a, b, *, ta=False, tb=False, res=(), epi=None, out_dtypes=(F32,), tm=512, tn=1024, tk=1024):
    g_, m_, k_ = (a.shape[0], a.shape[2], a.shape[1]) if ta else a.shape
    n_ = b.shape[1] if tb else b.shape[2]
    tm, tn, tk = min(tm, m_), min(tn, n_), min(tk, k_)
    nk, n_res, n_out = k_ // tk, len(res), len(out_dtypes)
    dims = (((0 if ta else 1,), (1 if tb else 0,)), ((), ()))
    if epi is None:
        epi = lambda acc: (acc,)

    def body(*refs):
        a_ref, b_ref = refs[:2]
        res_refs, out_refs, acc_ref = refs[2:2 + n_res], refs[2 + n_res:2 + n_res + n_out], refs[-1]
        k = pl.program_id(3)

        @pl.when(k == 0)
        def _():
            acc_ref[...] = jnp.zeros_like(acc_ref)

        acc_ref[...] += lax.dot_general(a_ref[...].astype(BF16), b_ref[...].astype(BF16), dims,
                                        preferred_element_type=F32)

        @pl.when(k == nk - 1)
        def _():
            vals = epi(acc_ref[...], *[r[...] for r in res_refs])
            for o, v in zip(out_refs, vals):
                o[...] = v.astype(o.dtype)

    a_spec = (pl.BlockSpec((None, tk, tm), lambda g, i, j, k: (g, k, i)) if ta
              else pl.BlockSpec((None, tm, tk), lambda g, i, j, k: (g, i, k)))
    b_spec = (pl.BlockSpec((None, tn, tk), lambda g, i, j, k: (g, j, k)) if tb
              else pl.BlockSpec((None, tk, tn), lambda g, i, j, k: (g, k, j)))
    res_specs = [pl.BlockSpec((None, tm, tn), lambda g, i, j, k, off=off: (g, i, j + off)) for _, off in res]
    out_spec = pl.BlockSpec((None, tm, tn), lambda g, i, j, k: (g, i, j))
    outs = pl.pallas_call(
        body, name=name, grid=(g_, m_ // tm, n_ // tn, nk),
        in_specs=[a_spec, b_spec, *res_specs], out_specs=[out_spec] * n_out,
        out_shape=[jax.ShapeDtypeStruct((g_, m_, n_), d) for d in out_dtypes],
        scratch_shapes=[pltpu.VMEM((tm, tn), F32)],
        compiler_params=_params("parallel", "parallel", "parallel", "arbitrary"),
    )(a, b, *[r for r, _ in res])
    return outs


def _mm2(name, a, b, *, res=(), **kw):
    outs = _mm(name, a[None], b[None], res=[(r[None], off) for r, off in res], **kw)
    return [o[0] for o in outs]


@jax.custom_vjp
def _bdot(a, b):
    return jnp.dot(a.astype(BF16), b.astype(BF16), preferred_element_type=F32)


def _bdot_fwd(a, b):
    return _bdot(a, b), (a, b)


def _bdot_bwd(saved, g):
    a, b = saved
    gb = g.astype(BF16)
    da = lax.dot_general(gb, b.astype(BF16), (((1,), (1,)), ((), ())), preferred_element_type=F32)
    db = lax.dot_general(a.astype(BF16), gb, (((0,), (0,)), ((), ())), preferred_element_type=F32)
    return da.astype(a.dtype), db.astype(b.dtype)


_bdot.defvjp(_bdot_fwd, _bdot_bwd)


def _rms(x, g):
    return x * lax.rsqrt(jnp.mean(x * x, axis=-1, keepdims=True) + EPS) * g


def _layer_norm(x, g, b):
    mu = jnp.mean(x, axis=-1, keepdims=True)
    var = jnp.mean(jnp.square(x - mu), axis=-1, keepdims=True)
    return (x - mu) * lax.rsqrt(var + EPS) * g + b


def _glu_and_gate_fn(a, a_gate, cg, hx):
    return a * jax.nn.sigmoid(a_gate), cg * hx


def _branches_fn(ca, pooled, cc, bg, u, v, conf_b, conf_g, conf_beta, pool_scale, gm_g, gm_b, *mats):
    pool_w, ws, bs = mats[0:4], mats[4:8], mats[8:12]
    ya = jax.nn.silu(_layer_norm(ca + conf_b, conf_g, conf_beta))
    yb = jnp.concatenate([_bdot(pooled[:, g * GROUP:(g + 1) * GROUP], pool_w[g]) for g in range(4)], axis=1) * pool_scale
    yc = bg * cc
    vln = _layer_norm(v, gm_g, gm_b)
    causal = lax.broadcasted_iota(jnp.int32, (GROUP, GROUP), 0) >= lax.broadcasted_iota(jnp.int32, (GROUP, GROUP), 1)
    sg = jnp.concatenate(
        [_bdot(jnp.where(causal, ws[g], 0.0), vln[:, g * GROUP:(g + 1) * GROUP]) + bs[g] for g in range(4)], axis=1)
    yd = u * sg
    return ya, yb, yc, yd


def _branches_bwd_fn(*args):
    ins, dys = args[:6] + args[10:], args[6:10]
    _, vjp = jax.vjp(_branches_fn, *ins)
    grads = vjp(tuple(dys))
    return grads


def _glu_bwd_fn(a, a_gate, cg, hx, dga, dchx):
    _, vjp = jax.vjp(_glu_and_gate_fn, a, a_gate, cg, hx)
    return vjp((dga, dchx))


def _rms_bwd_fn(x, dh, dres, g):
    _, vjp = jax.vjp(_rms, x, g)
    dx, dg = vjp(dh)
    return dx + dres, dg


def _loss_fn(x, target, g):
    def f(x, g):
        err = jnp.square(_rms(x, g) - target)
        return 0.5 * jnp.sum(jnp.mean(err, axis=-1, keepdims=True), axis=0, keepdims=True)

    loss, vjp = jax.vjp(f, x, g)
    dx, dg = vjp(jnp.ones((1, 1), F32))
    return dx, dg, jnp.broadcast_to(loss, (1, PACK_LANES))


def _adamw_fn(w, g, m, v):
    m = ADAM_B1 * m + (1.0 - ADAM_B1) * g
    v = ADAM_B2 * v + (1.0 - ADAM_B2) * jnp.square(g)
    m_hat = m / (1.0 - ADAM_B1 ** ADAM_STEP)
    v_hat = v / (1.0 - ADAM_B2 ** ADAM_STEP)
    delta = -ADAM_LR * (m_hat / (jnp.sqrt(v_hat) + ADAM_EPS) + ADAM_WD * w)
    return delta, m, v


def _group_pick(j):
    return [(j == g).astype(F32) for g in range(len(POOL_WINDOWS))]


def _accumulate_row(ref, row, value, first):
    @pl.when(first)
    def _():
        ref[pl.ds(row, 1), :] = value

    @pl.when(jnp.logical_not(first))
    def _():
        ref[pl.ds(row, 1), :] += value


def _rows_iota(n):
    return lax.broadcasted_iota(jnp.int32, (n, 1), 0)


def _seq_specs(n_time, causal):
    per = SEQ_TILE // HALO
    tile = pl.BlockSpec((SEQ_TILE, GROUP), lambda j, i: (i, j))
    if causal:
        halo = pl.BlockSpec((HALO, GROUP), lambda j, i: (jnp.maximum(i * per - 1, 0), j))
    else:
        halo = pl.BlockSpec((HALO, GROUP), lambda j, i: (jnp.minimum((i + 1) * per, n_time * per - 1), j))
    return tile, halo


def _seq_fwd(ga, proj, chx, conf_dw, sc_conv):
    s_len = ga.shape[0]
    n_time = s_len // SEQ_TILE
    ext = HALO + SEQ_TILE

    def body(ga_ref, ga_h, pin_ref, pin_h, chx_ref, chx_h, wc_ref, ws_ref, ca_ref, po_ref, cc_ref):
        j, i = pl.program_id(0), pl.program_id(1)
        keep = (i > 0).astype(F32)

        def extended(cur, halo):
            return jnp.concatenate([halo[...] * keep, cur[...]], axis=0)

        def conv(x, w_ref, taps):
            acc = x * w_ref[pl.ds(taps - 1, 1), :]
            for d in range(1, taps):
                acc = acc + pltpu.roll(x, d, 0) * w_ref[pl.ds(taps - 1 - d, 1), :]
            return acc[HALO:, :]

        ca_ref[...] = conv(extended(ga_ref, ga_h), wc_ref, CONF_KERNEL)
        cc_ref[...] = conv(extended(chx_ref, chx_h), ws_ref, SC_KERNEL)
        u = extended(pin_ref, pin_h)
        s1 = u + pltpu.roll(u, 1, 0)
        s2 = s1 + pltpu.roll(s1, 2, 0)
        s3 = s2 + pltpu.roll(s2, 4, 0)
        s4 = s3 + pltpu.roll(s3, 8, 0)
        pick = _group_pick(j)
        window_sum = s1 * pick[0] + s2 * pick[1] + s3 * pick[2] + s4 * pick[3]
        window = sum(float(w) * f for w, f in zip(POOL_WINDOWS, pick))
        pos = (_rows_iota(ext) + (i * SEQ_TILE - HALO + 1)).astype(F32)
        count = jnp.maximum(jnp.minimum(pos, window), 1.0)
        po_ref[...] = (window_sum / count - u)[HALO:, :]

    tile, halo = _seq_specs(n_time, causal=True)
    pin_tile = pl.BlockSpec((SEQ_TILE, GROUP), lambda j, i: (i, 2 * (MIX // GROUP) + j))
    pin_halo = pl.BlockSpec((HALO, GROUP), lambda j, i: (jnp.maximum(i * (SEQ_TILE // HALO) - 1, 0), 2 * (MIX // GROUP) + j))
    w_spec = lambda taps: pl.BlockSpec((taps, GROUP), lambda j, i: (0, j))
    return pl.pallas_call(
        body, name="seq_fwd", grid=(MIX // GROUP, n_time),
        in_specs=[tile, halo, pin_tile, pin_halo, tile, halo, w_spec(CONF_KERNEL), w_spec(SC_KERNEL)],
        out_specs=[tile] * 3, out_shape=[jax.ShapeDtypeStruct((s_len, MIX), F32)] * 3,
        compiler_params=_params("parallel", "arbitrary"),
    )(ga, ga, proj, proj, chx, chx, conf_dw, sc_conv)


def _seq_bwd(dca, dpooled, dcc, ga, chx, conf_dw, sc_conv):
    s_len = ga.shape[0]
    n_time = s_len // SEQ_TILE
    ext = HALO + SEQ_TILE

    def body(dca_ref, dca_h, dpo_ref, dpo_h, dcc_ref, dcc_h, ga_ref, ga_h, chx_ref, chx_h, wc_ref, ws_ref,
             dga_ref, dpin_ref, dchx_ref, dwc_ref, dws_ref):
        j, i = pl.program_id(0), pl.program_id(1)
        keep_prev = (i > 0).astype(F32)
        keep_next = (i < n_time - 1).astype(F32)

        def with_next(cur, halo):
            return jnp.concatenate([cur[...], halo[...] * keep_next], axis=0)

        def with_prev(cur, halo):
            return jnp.concatenate([halo[...] * keep_prev, cur[...]], axis=0)

        def conv_t(dy, w_ref, taps):
            acc = dy * w_ref[pl.ds(taps - 1, 1), :]
            for d in range(1, taps):
                acc = acc + pltpu.roll(dy, ext - d, 0) * w_ref[pl.ds(taps - 1 - d, 1), :]
            return acc[:SEQ_TILE, :]

        def tap_grads(dy_cur, x, dw_ref, taps):
            for d in range(taps):
                shifted = x if d == 0 else pltpu.roll(x, d, 0)
                part = jnp.sum(dy_cur * shifted[HALO:, :], axis=0, keepdims=True)
                _accumulate_row(dw_ref, taps - 1 - d, part, i == 0)

        dga_ref[...] = conv_t(with_next(dca_ref, dca_h), wc_ref, CONF_KERNEL)
        dchx_ref[...] = conv_t(with_next(dcc_ref, dcc_h), ws_ref, SC_KERNEL)
        tap_grads(dca_ref[...], with_prev(ga_ref, ga_h), dwc_ref, CONF_KERNEL)
        tap_grads(dcc_ref[...], with_prev(chx_ref, chx_h), dws_ref, SC_KERNEL)

        dpo = with_next(dpo_ref, dpo_h)
        pick = _group_pick(j)
        window = sum(float(w) * f for w, f in zip(POOL_WINDOWS, pick))
        pos = (_rows_iota(ext) + (i * SEQ_TILE + 1)).astype(F32)
        r = dpo / jnp.minimum(pos, window)
        q1 = r + pltpu.roll(r, ext - 1, 0)
        q2 = q1 + pltpu.roll(q1, ext - 2, 0)
        q3 = q2 + pltpu.roll(q2, ext - 4, 0)
        q4 = q3 + pltpu.roll(q3, ext - 8, 0)
        q = q1 * pick[0] + q2 * pick[1] + q3 * pick[2] + q4 * pick[3]
        dpin_ref[...] = (q - dpo)[:SEQ_TILE, :].astype(dpin_ref.dtype)

    tile, nxt = _seq_specs(n_time, causal=False)
    _, prv = _seq_specs(n_time, causal=True)
    w_spec = lambda taps: pl.BlockSpec((taps, GROUP), lambda j, i: (0, j))
    return pl.pallas_call(
        body, name="seq_bwd", grid=(MIX // GROUP, n_time),
        in_specs=[tile, nxt, tile, nxt, tile, nxt, tile, prv, tile, prv, w_spec(CONF_KERNEL), w_spec(SC_KERNEL)],
        out_specs=[tile, tile, tile, w_spec(CONF_KERNEL), w_spec(SC_KERNEL)],
        out_shape=[jax.ShapeDtypeStruct((s_len, MIX), F32), jax.ShapeDtypeStruct((s_len, MIX), BF16),
                   jax.ShapeDtypeStruct((s_len, MIX), F32), jax.ShapeDtypeStruct(conf_dw.shape, F32),
                   jax.ShapeDtypeStruct(sc_conv.shape, F32)],
        compiler_params=_params("parallel", "arbitrary"),
    )(dca, dca, dpooled, dpooled, dcc, dcc, ga, ga, chx, chx, conf_dw, sc_conv)


MERGE_TILE = 512


def _merge_fwd(y, w_branch, proj):
    s_len, d_model = y.shape[1], w_branch.shape[2]
    t = min(MERGE_TILE, s_len)
    gate_block0 = (proj.shape[1] - 4 * d_model) // MERGE_TILE
    per_branch = d_model // MERGE_TILE

    def body(y_ref, w_ref, gate_ref, out_ref, acc_ref):
        k = pl.program_id(2)

        @pl.when(k == 0)
        def _():
            acc_ref[...] = jnp.zeros_like(acc_ref)

        z = jnp.dot(y_ref[...], w_ref[...], preferred_element_type=F32)
        acc_ref[...] += jax.nn.sigmoid(gate_ref[...]) * z

        @pl.when(k == 3)
        def _():
            out_ref[...] = acc_ref[...].astype(out_ref.dtype)

    return pl.pallas_call(
        body, name="merge_fwd", grid=(s_len // t, per_branch, 4),
        in_specs=[pl.BlockSpec((None, t, MIX), lambda i, j, k: (k, i, 0)),
                  pl.BlockSpec((None, MIX, MERGE_TILE), lambda i, j, k: (k, 0, j)),
                  pl.BlockSpec((t, MERGE_TILE), lambda i, j, k: (i, gate_block0 + k * per_branch + j))],
        out_specs=pl.BlockSpec((t, MERGE_TILE), lambda i, j, k: (i, j)),
        out_shape=jax.ShapeDtypeStruct((s_len, d_model), BF16),
        scratch_shapes=[pltpu.VMEM((t, MERGE_TILE), F32)],
        compiler_params=_params("parallel", "parallel", "arbitrary"),
    )(y, w_branch, proj)


def _merge_bwd(y, w_branch, proj, dmerged):
    s_len, d_model = y.shape[1], w_branch.shape[2]
    t = min(MERGE_TILE, s_len)
    gate_block0 = (proj.shape[1] - 4 * d_model) // MERGE_TILE
    per_branch = d_model // MERGE_TILE

    def body(y_ref, w_ref, gate_ref, dm_ref, dz_ref, dgate_ref):
        z = jnp.dot(y_ref[...], w_ref[...], preferred_element_type=F32)
        gate = jax.nn.sigmoid(gate_ref[...])
        dm = dm_ref[...]
        dz_ref[...] = (dm * gate).astype(dz_ref.dtype)
        dgate_ref[...] = (dm * z * gate * (1.0 - gate)).astype(dgate_ref.dtype)

    return pl.pallas_call(
        body, name="merge_bwd", grid=(s_len // t, per_branch, 4),
        in_specs=[pl.BlockSpec((None, t, MIX), lambda i, j, k: (k, i, 0)),
                  pl.BlockSpec((None, MIX, MERGE_TILE), lambda i, j, k: (k, 0, j)),
                  pl.BlockSpec((t, MERGE_TILE), lambda i, j, k: (i, gate_block0 + k * per_branch + j)),
                  pl.BlockSpec((t, MERGE_TILE), lambda i, j, k: (i, j))],
        out_specs=[pl.BlockSpec((None, t, MERGE_TILE), lambda i, j, k: (k, i, j)),
                   pl.BlockSpec((t, MERGE_TILE), lambda i, j, k: (i, k * per_branch + j))],
        out_shape=[jax.ShapeDtypeStruct((4, s_len, d_model), BF16), jax.ShapeDtypeStruct((s_len, 4 * d_model), BF16)],
        compiler_params=_params("parallel", "parallel", "arbitrary"),
    )(y, w_branch, proj, dmerged)


def _row(v):
    return v.reshape(1, -1)


def _branch_params(w):
    mats = [w["pool_w"][g] for g in range(4)] + [w["gmlp_ws"][g] for g in range(4)]
    mats += [w["gmlp_bs"][g].reshape(GROUP, 1) for g in range(4)]
    return [_row(w["conf_dw_b"]), _row(w["conf_ln_g"]), _row(w["conf_ln_b"]), _row(w["pool_scale"]),
            _row(w["gmlp_ln_g"]), _row(w["gmlp_ln_b"])] + mats


def _layer_fwd(x0, p_l, w):
    d_model = x0.shape[1]
    (h1,) = _rowwise("rms_mix", _rms, [(x0, d_model, 0)], [_row(w["norm_mix"])], [(d_model, BF16)])
    (proj,) = _mm2("proj", h1, w["w_in"])
    ga, chx = _rowwise("glu", _glu_and_gate_fn, [(proj, MIX, 0), (proj, MIX, 1), (proj, MIX, 4), (proj, MIX, 5)], [],
                       [(MIX, F32), (MIX, F32)])
    ca, pooled, cc = _seq_fwd(ga, proj, chx, w["conf_dw"], w["sc_conv"])
    ys = _rowwise("branches", _branches_fn,
                  [(ca, MIX, 0), (pooled, MIX, 0), (cc, MIX, 0), (proj, MIX, 3), (proj, MIX, 6), (proj, MIX, 7)],
                  _branch_params(w), [(MIX, BF16)] * 4, tm=GROUP)
    y = jnp.stack(ys)
    merged = _merge_fwd(y, w["w_branch"], proj)
    (x1,) = _mm2("out_proj", merged, w["w_out"], res=[(x0, 0)], epi=lambda acc, r: (acc + r,))
    (h2,) = _rowwise("rms_mlp", _rms, [(x1, d_model, 0)], [_row(w["norm_mlp"])], [(d_model, BF16)])
    up, act = _mm2("mlp_up", h2, w["w_up"], out_dtypes=(F32, BF16),
                   epi=lambda acc: (acc, jnp.square(jnp.maximum(acc, 0.0))))
    (x2,) = _mm2("mlp_down", act, w["w_down"], res=[(x1, 0)], epi=lambda acc, r: (acc + r,))
    (h3,) = _rowwise("rms_ple", _rms, [(x2, d_model, 0)], [_row(w["norm_ple"])], [(d_model, BF16)])
    (pg,) = _mm2("ple_gate", h3, w["w_ple_gate"])
    (x3,) = _mm2("ple", p_l, w["w_ple"], res=[(pg, 0), (x2, 0)],
                 epi=lambda acc, g, r: (acc * jax.nn.sigmoid(g) + r,))
    saved = dict(x0=x0, h1=h1, proj=proj, ga=ga, chx=chx, ca=ca, pooled=pooled, cc=cc, y=y, merged=merged, x1=x1,
                 h2=h2, up=up, act=act, x2=x2, h3=h3, pg=pg)
    return x3, saved


def _layer_bwd(dx3, p_l, w, s):
    d_model = dx3.shape[1]
    big, small = {}, {}

    def ple_epi(acc, dx, g):
        sig = jax.nn.sigmoid(g)
        return dx * sig, dx * acc * sig * (1.0 - sig)

    dpp, dpg = _mm2("ple_bwd", p_l, w["w_ple"], res=[(dx3, 0), (s["pg"], 0)], epi=ple_epi, out_dtypes=(BF16, BF16))
    (big["w_ple"],) = _mm2("dw_ple", p_l, dpp, ta=True, out_dtypes=(BF16,))
    (big["w_ple_gate"],) = _mm2("dw_ple_gate", s["h3"], dpg, ta=True, out_dtypes=(BF16,))
    (dh3,) = _mm2("dh_ple", dpg, w["w_ple_gate"], tb=True)
    dx2, small["norm_ple"] = _rowwise("rms_ple_bwd", _rms_bwd_fn, [(s["x2"], d_model, 0), (dh3, d_model, 0), (dx3, d_model, 0)],
                                      [_row(w["norm_ple"])], [(d_model, F32)], [(1, d_model)])

    (dup,) = _mm2("mlp_down_bwd", dx2, w["w_down"], tb=True, res=[(s["up"], 0)], out_dtypes=(BF16,),
                  epi=lambda acc, up: (acc * 2.0 * jnp.maximum(up, 0.0),))
    (big["w_down"],) = _mm2("dw_down", s["act"], dx2, ta=True, out_dtypes=(BF16,))
    (big["w_up"],) = _mm2("dw_up", s["h2"], dup, ta=True, out_dtypes=(BF16,))
    (dh2,) = _mm2("mlp_up_bwd", dup, w["w_up"], tb=True)
    dx1, small["norm_mlp"] = _rowwise("rms_mlp_bwd", _rms_bwd_fn, [(s["x1"], d_model, 0), (dh2, d_model, 0), (dx2, d_model, 0)],
                                      [_row(w["norm_mlp"])], [(d_model, F32)], [(1, d_model)])

    (dmerged,) = _mm2("out_proj_bwd", dx1, w["w_out"], tb=True)
    (big["w_out"],) = _mm2("dw_out", s["merged"], dx1, ta=True, out_dtypes=(BF16,))
    dz, dgate = _merge_bwd(s["y"], w["w_branch"], s["proj"], dmerged)
    (dy,) = _mm("branch_bwd", dz, w["w_branch"], tb=True)
    (big["w_branch"],) = _mm("dw_branch", s["y"], dz, ta=True, out_dtypes=(BF16,))

    proj = s["proj"]
    params = _branch_params(w)
    grads = _rowwise(
        "branches_bwd", _branches_bwd_fn,
        [(s["ca"], MIX, 0), (s["pooled"], MIX, 0), (s["cc"], MIX, 0), (proj, MIX, 3), (proj, MIX, 6), (proj, MIX, 7)]
        + [(dy, MIX, 0, g) for g in range(4)], params,
        [(MIX, F32), (MIX, F32), (MIX, F32), (MIX, BF16), (MIX, BF16), (MIX, BF16)], [p.shape for p in params], tm=GROUP)
    dca, dpooled, dcc, dbg, du, dv = grads[:6]
    pg_ = grads[6:]
    small["conf_dw_b"], small["conf_ln_g"], small["conf_ln_b"], small["pool_scale"] = pg_[0], pg_[1], pg_[2], pg_[3]
    small["gmlp_ln_g"], small["gmlp_ln_b"] = pg_[4], pg_[5]
    small["pool_w"] = jnp.stack(pg_[6:10])
    small["gmlp_ws"] = jnp.stack(pg_[10:14])
    small["gmlp_bs"] = jnp.stack([b.reshape(GROUP) for b in pg_[14:18]])

    dga, dpin, dchx, small["conf_dw"], small["sc_conv"] = _seq_bwd(dca, dpooled, dcc, s["ga"], s["chx"], w["conf_dw"], w["sc_conv"])
    da, dag, dcg, dhx = _rowwise("glu_bwd", _glu_bwd_fn,
                                 [(proj, MIX, 0), (proj, MIX, 1), (proj, MIX, 4), (proj, MIX, 5), (dga, MIX, 0), (dchx, MIX, 0)],
                                 [], [(MIX, BF16)] * 4)
    dproj = jnp.concatenate([da, dag, dpin, dbg, dcg, dhx, du, dv, dgate], axis=1)
    (big["w_in"],) = _mm2("dw_in", s["h1"], dproj, ta=True, out_dtypes=(BF16,))
    (dh1,) = _mm2("proj_bwd", dproj, w["w_in"], tb=True)
    dx0, small["norm_mix"] = _rowwise("rms_mix_bwd", _rms_bwd_fn, [(s["x0"], d_model, 0), (dh1, d_model, 0), (dx1, d_model, 0)],
                                      [_row(w["norm_mix"])], [(d_model, F32)], [(1, d_model)])
    return dx0, big, small


def _position():
    return lax.axis_index("x"), lax.axis_index("y"), lax.axis_index("c")


def _other_chips(x, y):
    return [(1 - x, y), (x, 1 - y), (1 - x, 1 - y)]


def _region(ref, cuts):
    return ref.at[tuple(pl.ds(*cuts[a]) if a in cuts else slice(None) for a in range(len(ref.shape)))]


def _allgather_small(name, block):
    m_per, n = block.shape

    def body(x_ref, out_ref, send_sems, recv_sems, local_sem):
        x, y, c = _position()
        me, sibling = (x, y, c), (x, y, 1 - c)
        chips = _other_chips(x, y)

        def rows(px, py, pc):
            return out_ref.at[pl.ds((4 * px + 2 * py + pc) * m_per, m_per), :]

        def copy(k, blk, to, src=None):
            return pltpu.make_async_remote_copy(
                src_ref=rows(*blk) if src is None else src, dst_ref=rows(*blk), send_sem=send_sems.at[k],
                recv_sem=recv_sems.at[k], device_id=to, device_id_type=pl.DeviceIdType.MESH)

        mine = pltpu.make_async_copy(x_ref, rows(*me), local_sem)
        mine.start()
        first = [copy(0, me, sibling, src=x_ref)]
        first += [copy(1 + j, me, (*chip, c), src=x_ref) for j, chip in enumerate(chips)]
        for cp in first:
            cp.start()
        passed = [copy(4 + j, (*chip, c), sibling) for j, chip in enumerate(chips)]
        for j, chip in enumerate(chips):
            copy(1 + j, (*chip, c), me).wait_recv()
            passed[j].start()
        copy(0, sibling, me).wait_recv()
        for j, chip in enumerate(chips):
            copy(4 + j, (*chip, 1 - c), me).wait_recv()
        for cp in first + passed:
            cp.wait_send()
        mine.wait()

    return pl.pallas_call(
        body, name=name, out_shape=jax.ShapeDtypeStruct((8 * m_per, n), block.dtype), in_specs=[ANY], out_specs=ANY,
        scratch_shapes=[pltpu.SemaphoreType.DMA((7,)), pltpu.SemaphoreType.DMA((7,)), pltpu.SemaphoreType.DMA],
    )(block)


def _cuts(arr_shape, shard_axis, half_axis, shard=None, half=None):
    cuts = {}
    if shard is not None:
        n = arr_shape[shard_axis] // 4
        cuts[shard_axis] = (shard * n, n)
    if half is not None:
        n = arr_shape[half_axis] // 2
        cuts[half_axis] = (half * n, n)
    return cuts


def _full_shape(shard_shape, shard_axis):
    return tuple(4 * n if a == shard_axis else n for a, n in enumerate(shard_shape))


HBM = pl.BlockSpec(memory_space=pltpu.HBM)
SEM = pl.BlockSpec(memory_space=pltpu.SEMAPHORE)
DATAFLOW_EFFECT = pltpu.SideEffectType.DATAFLOW_SIDE_EFFECTING
TOKEN_SHAPE = (8, 128)


def _remote(src, dst, send_sems, recv_sems, k, peer):
    return pltpu.make_async_remote_copy(src_ref=src, dst_ref=dst, send_sem=send_sems.at[k], recv_sem=recv_sems.at[k],
                                        device_id=peer, device_id_type=pl.DeviceIdType.MESH)


def _split_start(name, sources, landings, plan):
    n_s, n_l = len(sources), len(landings)
    n_copies = len(plan([None] * n_s, [None] * n_l, None))

    def body(*refs):
        src, land = refs[:n_s], refs[n_s:n_s + n_l]
        send_sems, recv_sems, token = refs[n_s + n_l], refs[n_s + n_l + 1], refs[-1]
        for k, (s, d, peer) in enumerate(plan(src, land, _position())):
            _remote(s, d, send_sems, recv_sems, k, peer).start()
        token[...] = jnp.zeros_like(token)

    arrays = [pltpu.with_memory_space_constraint(a, pltpu.HBM) for a in (*sources, *landings)]
    outs = pl.pallas_call(
        body, name=name,
        out_shape=(pltpu.SemaphoreType.DMA((n_copies,)), pltpu.SemaphoreType.DMA((n_copies,)),
                   *[pltpu.HBM(a.shape, a.dtype) for a in arrays], jax.ShapeDtypeStruct(TOKEN_SHAPE, F32)),
        in_specs=[HBM] * (n_s + n_l),
        out_specs=(SEM, SEM, *[HBM] * (n_s + n_l), pl.BlockSpec(memory_space=pltpu.VMEM)),
        input_output_aliases={i: 2 + i for i in range(n_s + n_l)},
        compiler_params=pltpu.CompilerParams(has_side_effects=DATAFLOW_EFFECT),
    )(*arrays)
    return outs[0], outs[1], list(outs[2:2 + n_s]), list(outs[2 + n_s:2 + n_s + n_l]), outs[-1]


def _split_wait(name, started, after, plan):
    send_sems, recv_sems, sources, landings, _ = started
    n_s, n_l = len(sources), len(landings)

    def body(*refs):
        src, land = refs[:n_s], refs[n_s:n_s + n_l]
        send_sems_ref, recv_sems_ref = refs[n_s + n_l], refs[n_s + n_l + 1]
        for k, (s, d, peer) in enumerate(plan(src, land, _position())):
            cp = _remote(s, d, send_sems_ref, recv_sems_ref, k, peer)
            cp.wait_send()
            cp.wait_recv()

    outs = pl.pallas_call(
        body, name=name, out_shape=tuple(pltpu.HBM(a.shape, a.dtype) for a in (*sources, *landings)),
        in_specs=[*[HBM] * (n_s + n_l), SEM, SEM, ANY], out_specs=tuple([HBM] * (n_s + n_l)),
        input_output_aliases={i: i for i in range(n_s + n_l)},
        compiler_params=pltpu.CompilerParams(has_side_effects=DATAFLOW_EFFECT),
    )(*sources, *landings, send_sems, recv_sems, after)
    return list(outs[:n_s]), list(outs[n_s:])


def _exchange(name, inputs, out_shapes, plan, in_place=False):
    n_in = len(inputs)
    n_out = n_in if in_place else len(out_shapes)
    n_copies = len(plan([None] * n_in, [None] * n_out, None))

    def body(*refs):
        ins, outs = refs[:n_in], refs[n_in:n_in + n_out]
        send_sems, recv_sems = refs[n_in + n_out:]
        copies = [_remote(s, d, send_sems, recv_sems, k, peer)
                  for k, (s, d, peer) in enumerate(plan(ins, ins if in_place else outs, _position()))]
        for cp in copies:
            cp.start()
        for cp in copies:
            cp.wait()

    if in_place:
        out_shapes = [jax.ShapeDtypeStruct(a.shape, a.dtype) for a in inputs]
    return pl.pallas_call(
        body, name=name, in_specs=[ANY] * n_in, out_specs=[ANY] * n_out, out_shape=out_shapes,
        input_output_aliases={i: i for i in range(n_in)} if in_place else {},
        scratch_shapes=[pltpu.SemaphoreType.DMA((n_copies,)), pltpu.SemaphoreType.DMA((n_copies,))],
    )(*inputs)


def _half_rows(ref, half):
    n = ref.shape[0] // 2
    return ref.at[pl.ds(half * n, n)]


def _gather_plan(sources, landings, pos):
    if pos is None:
        return [None] * (3 * len(sources))
    x, y, c = pos
    copies = []
    for src, land in zip(sources, landings):
        for qx, qy in _other_chips(x, y):
            copies.append((_half_rows(src, c), _half_rows(land.at[2 * x + y], c), (qx, qy, c)))
    return copies


def _sibling_plan(landings, _, pos):
    if pos is None:
        return [None] * (3 * len(landings))
    x, y, c = pos
    copies = []
    for land in landings:
        for qx, qy in _other_chips(x, y):
            region = _half_rows(land.at[2 * qx + qy], c)
            copies.append((region, region, (x, y, 1 - c)))
    return copies


def _weights_start(layer, shards):
    landings = [lax.empty((4, *s.shape), BF16) for s in shards]
    return _split_start(f"weights_start_{layer}", shards, landings, _gather_plan)


def _weights_finish(layer, started, after):
    shards, landings = _split_wait(f"weights_wait_{layer}", started, after, _gather_plan)
    landings = _exchange(f"weights_to_sibling_{layer}", landings, None, _sibling_plan, in_place=True)
    my_shard = 2 * lax.axis_index("x") + lax.axis_index("y")
    full = {}
    for (name, sa, _), shard, land in zip(BIG, shards, landings):
        land = lax.dynamic_update_index_in_dim(land, shard, my_shard, axis=0)
        axis = sa - 1
        moved = jnp.moveaxis(land, 0, axis)
        full[name] = moved.reshape(*moved.shape[:axis], 4 * moved.shape[axis + 1], *moved.shape[axis + 2:])
    return full


def _half_shape(shape, half_axis):
    return tuple(n // 2 if a == half_axis else n for a, n in enumerate(shape))


def _shard_shape(shape, shard_axis):
    return tuple(n // 4 if a == shard_axis else n for a, n in enumerate(shape))


def _flat2d(a):
    return a.reshape(-1, a.shape[-1])


ELEMENTWISE_TILE_ELEMENTS = 1 << 18


def _row_tile(rows, cols):
    tile = 1
    while rows % (2 * tile) == 0 and 2 * tile * cols <= ELEMENTWISE_TILE_ELEMENTS:
        tile *= 2
    assert tile >= 16 or tile == rows, (rows, cols)
    return tile


def _sum_arrays(name, arrays, out_dtype):
    shape = arrays[0].shape
    flat = [_flat2d(a) for a in arrays]
    rows, cols = flat[0].shape
    tm = _row_tile(rows, cols)

    def fn(*tiles):
        acc = tiles[0].astype(F32)
        for t in tiles[1:]:
            acc = acc + t.astype(F32)
        return acc

    (out,) = _rowwise(name, fn, [(a, cols, 0) for a in flat], [], [(cols, out_dtype)], tm=tm)
    return out.reshape(shape)


def _grads_start(layer, grads):
    c = lax.axis_index("c")
    shapes = [g.shape for g in grads]
    axes = [(sa - 1, ha - 1) for _, sa, ha in BIG]

    def to_sibling(ins, outs, pos):
        if pos is None:
            return [None] * len(BIG)
        px, py, pc = pos
        return [(_region(ins[wi], _cuts(shapes[wi], sa, ha, half=1 - pc)), outs[wi], (px, py, 1 - pc))
                for wi, (sa, ha) in enumerate(axes)]

    landed = _exchange("grads_to_sibling", grads, [jax.ShapeDtypeStruct(_half_shape(s, ha), BF16) for s, (_, ha) in zip(shapes, axes)],
                       to_sibling)
    chip_sums = []
    for wi, (name, _, _) in enumerate(BIG):
        ha = axes[wi][1]
        n = shapes[wi][ha] // 2
        mine = lax.dynamic_slice_in_dim(grads[wi], c * n, n, axis=ha)
        chip_sums.append(_sum_arrays("chip_sum_" + name, [mine, landed[wi]], BF16))

    def to_chips(sources, landings, pos):
        if pos is None:
            return [None] * (3 * len(BIG))
        px, py, pc = pos
        copies = []
        for wi, (sa, ha) in enumerate(axes):
            for j, (qx, qy) in enumerate(_other_chips(px, py)):
                piece = _region(sources[wi], _cuts(shapes[wi], sa, ha, shard=2 * qx + qy))
                copies.append((piece, landings[wi].at[j], (qx, qy, pc)))
        return copies

    landings = [lax.empty((3, *_shard_shape(t.shape, sa)), BF16) for t, (sa, _) in zip(chip_sums, axes)]
    return _split_start(f"grads_start_{layer}", chip_sums, landings, to_chips), to_chips


def _grads_finish(layer, started, after):
    chip_sums, pieces = _split_wait(f"grads_wait_{layer}", started[0], after, started[1])
    my_shard = 2 * lax.axis_index("x") + lax.axis_index("y")
    reduced = []
    for (name, sa, _), chip_sum, piece in zip(BIG, chip_sums, pieces):
        n = chip_sum.shape[sa - 1] // 4
        mine = lax.dynamic_slice_in_dim(chip_sum, my_shard * n, n, axis=sa - 1)
        reduced.append(_sum_arrays("shard_sum_" + name, [mine, piece[0], piece[1], piece[2]], F32))
    return reduced


def _join_halves(reduced):
    c = lax.axis_index("c")

    def halves_to_sibling(ins, outs, pos):
        if pos is None:
            return [None] * len(BIG)
        px, py, pc = pos
        return [(ins[wi], outs[wi], (px, py, 1 - pc)) for wi in range(len(BIG))]

    from_sibling = _exchange("grad_halves_to_sibling", reduced, [jax.ShapeDtypeStruct(r.shape, F32) for r in reduced],
                             halves_to_sibling)
    out = []
    for wi, (_, _, ha) in enumerate(BIG):
        low = jnp.where(c == 0, reduced[wi], from_sibling[wi])
        high = jnp.where(c == 0, from_sibling[wi], reduced[wi])
        out.append(jnp.concatenate([low, high], axis=ha))
    return out


def _pack(arrays):
    flat = jnp.concatenate([a.reshape(-1).astype(F32) for a in arrays])
    rows = -(-flat.shape[0] // PACK_LANES)
    rows = -(-rows // PACK_ROW_MULTIPLE) * PACK_ROW_MULTIPLE
    return jnp.pad(flat, (0, rows * PACK_LANES - flat.shape[0])).reshape(rows, PACK_LANES)


def _unpack(buf, shapes):
    flat, out, off = buf.reshape(-1), [], 0
    for s in shapes:
        n = 1
        for d in s:
            n *= d
        out.append(flat[off:off + n].reshape(s))
        off += n
    return out


def _sum8(name, gathered, rows):
    def body(in_ref, out_ref):
        acc = in_ref[0]
        for d in range(1, 8):
            acc = acc + in_ref[d]
        out_ref[...] = acc

    return pl.pallas_call(
        body, name=name, grid=(rows // PACK_ROW_MULTIPLE,),
        in_specs=[pl.BlockSpec((8, PACK_ROW_MULTIPLE, PACK_LANES), lambda i: (0, i, 0))],
        out_specs=pl.BlockSpec((PACK_ROW_MULTIPLE, PACK_LANES), lambda i: (i, 0)),
        out_shape=jax.ShapeDtypeStruct((rows, PACK_LANES), F32), compiler_params=_params("arbitrary"),
    )(gathered.reshape(8, rows, PACK_LANES))


def _adamw(name, w, g, m, v):
    shape = w.shape
    flat = [_flat2d(a) for a in (w, g, m, v)]
    rows, cols = flat[0].shape
    tm = _row_tile(rows, cols)
    outs = _rowwise(name, _adamw_fn, [(a, cols, 0) for a in flat], [], [(cols, F32)] * 3, tm=tm)
    return [o.reshape(shape) for o in outs]


WEIGHT_ORDER = ("norm_mix", "w_in", "conf_dw", "conf_dw_b", "conf_ln_g", "conf_ln_b", "pool_w", "pool_scale", "sc_conv",
                "gmlp_ln_g", "gmlp_ln_b", "gmlp_ws", "gmlp_bs", "w_branch", "w_out", "norm_mlp", "w_up", "w_down",
                "norm_ple", "w_ple", "w_ple_gate", "norm_final")


def _local_step(x, p, loss_target, weights, layer_big, full_small_sharded, on_big_grads):
    n_layers = p.shape[0]
    d_model = x.shape[1]
    layer_w, saved = [], []
    h = x
    for l in range(n_layers):
        w = {n: weights[n][l] for n in SMALL_REP if n != "norm_final"}
        w.update(layer_big(l, h))
        w.update({n: full_small_sharded[n][l] for n in SMALL_SHARDED})
        layer_w.append(w)
        h, s = _layer_fwd(h, p[l], w)
        saved.append(s)
    dx, d_final, loss = _rowwise("loss", _loss_fn, [(h, d_model, 0), (loss_target, d_model, 0)], [_row(weights["norm_final"])],
                                 [(d_model, F32)], [(1, d_model), (1, PACK_LANES)])
    small_grads = {n: [None] * n_layers for n in SMALL_REP + SMALL_SHARDED if n != "norm_final"}
    for l in reversed(range(n_layers)):
        dx, big, small = _layer_bwd(dx, p[l], layer_w[l], saved[l])
        dx = on_big_grads(l, [big[n] for n, _, _ in BIG], dx)
        for n in small:
            small_grads[n][l] = small[n].reshape(weights[n].shape[1:]) if n in SMALL_REP else small[n]
    small_grads = {n: jnp.stack(v) for n, v in small_grads.items()}
    small_grads["norm_final"] = d_final.reshape(-1)
    return loss, dx, small_grads


def kernel(x, p, norm_mix, w_in, conf_dw, conf_dw_b, conf_ln_g, conf_ln_b, pool_w, pool_scale, sc_conv, gmlp_ln_g, gmlp_ln_b, gmlp_ws, gmlp_bs, w_branch, w_out, norm_mlp, w_up, w_down, norm_ple, w_ple, w_ple_gate, norm_final, loss_target, m_norm_mix, m_w_in, m_conf_dw, m_conf_dw_b, m_conf_ln_g, m_conf_ln_b, m_pool_w, m_pool_scale, m_sc_conv, m_gmlp_ln_g, m_gmlp_ln_b, m_gmlp_ws, m_gmlp_bs, m_w_branch, m_w_out, m_norm_mlp, m_w_up, m_w_down, m_norm_ple, m_w_ple, m_w_ple_gate, m_norm_final, v_norm_mix, v_w_in, v_conf_dw, v_conf_dw_b, v_conf_ln_g, v_conf_ln_b, v_pool_w, v_pool_scale, v_sc_conv, v_gmlp_ln_g, v_gmlp_ln_b, v_gmlp_ws, v_gmlp_bs, v_w_branch, v_w_out, v_norm_mlp, v_w_up, v_w_down, v_norm_ple, v_w_ple, v_w_ple_gate, v_norm_final):
    given = dict(locals())
    weights = {n: given[n] for n in WEIGHT_ORDER}
    mom_m = {n: given["m_" + n] for n in WEIGHT_ORDER}
    mom_v = {n: given["v_" + n] for n in WEIGHT_ORDER}
    my_shard = 2 * lax.axis_index("x") + lax.axis_index("y")

    n_layers = p.shape[0]
    weight_copies = [_weights_start(l, [weights[n][l].astype(BF16) for n, _, _ in BIG]) for l in range(n_layers)]
    started = sum(cp[4][0, 0] for cp in weight_copies)
    shard_shapes = [weights[n].shape for n in SMALL_SHARDED]
    gathered = _allgather_small("allgather_small_weights", _pack([weights[n] for n in SMALL_SHARDED]))
    per_dev = gathered.reshape(8, -1, PACK_LANES)
    full_small_sharded = {}
    chip_parts = [_unpack(per_dev[2 * s], shard_shapes) for s in range(4)]
    for i, n in enumerate(SMALL_SHARDED):
        full_small_sharded[n] = jnp.concatenate([chip_parts[s][i] for s in range(4)], axis=-1)

    grad_copies = [None] * n_layers

    def layer_big(l, h):
        return _weights_finish(l, weight_copies[l], h)

    def on_big_grads(l, layer_grads, dx):
        grad_copies[l] = _grads_start(l, layer_grads)
        return dx + grad_copies[l][0][4][0, 0]

    loss_row, dx, small_grads = _local_step(x[0] + started, p[:, 0], loss_target[0], weights, layer_big, full_small_sharded,
                                            on_big_grads)
    loss = lax.psum(loss_row[0, 0], MESH_AXES)

    halves = [_grads_finish(l, grad_copies[l], dx) for l in reversed(range(n_layers))][::-1]
    big_reduced = _join_halves([jnp.stack([halves[l][wi] for l in range(n_layers)]) for wi in range(len(BIG))])
    grads = {n: g for (n, _, _), g in zip(BIG, big_reduced)}

    small_names = SMALL_REP + SMALL_SHARDED
    packed = _pack([small_grads[n] for n in small_names])
    rows = packed.shape[0]
    summed = _sum8("sum_small_grads", _allgather_small("allgather_small_grads", packed), rows)
    small_full_shapes = [weights[n].shape for n in SMALL_REP] + [full_small_sharded[n].shape for n in SMALL_SHARDED]
    for n, g in zip(small_names, _unpack(summed, small_full_shapes)):
        if n in SMALL_SHARDED:
            width = weights[n].shape[-1]
            g = lax.dynamic_slice_in_dim(g, my_shard * width, width, axis=g.ndim - 1)
        grads[n] = g

    delta, new_m, new_v = {}, {}, {}
    for n, _, _ in BIG:
        delta[n], new_m[n], new_v[n] = _adamw("adamw_" + n, weights[n], grads[n], mom_m[n], mom_v[n])
    for group, names in (("rep", SMALL_REP), ("sharded", SMALL_SHARDED)):
        outs = _adamw("adamw_small_" + group, *[_pack([src[n] for n in names]) for src in (weights, grads, mom_m, mom_v)])
        shapes = [weights[n].shape for n in names]
        for dst, buf in zip((delta, new_m, new_v), outs):
            for n, a in zip(names, _unpack(buf, shapes)):
                dst[n] = a

    return (loss, dx[None], *[grads[n] for n in WEIGHT_ORDER], *[delta[n] for n in WEIGHT_ORDER],
            *[new_m[n] for n in WEIGHT_ORDER], *[new_v[n] for n in WEIGHT_ORDER])
```

```python
import functools

import jax
import jax.numpy as jnp
from jax import lax
from jax.experimental import pallas as pl
from jax.experimental.pallas import tpu as pltpu

F32, BF16 = jnp.float32, jnp.bfloat16
EPS = 1e-6
ADAM_LR, ADAM_B1, ADAM_B2, ADAM_EPS, ADAM_WD, ADAM_STEP = 0.001, 0.9, 0.999, 1e-08, 0.01, 10
CONF_KERNEL, SC_KERNEL = 31, 3
POOL_WINDOWS = (2, 4, 8, 16)
MIX = 512
GROUP = 128
HALO = 32
SEQ_TILE = 256
VMEM_LIMIT_BYTES = 48 * 1024 * 1024
MESH_AXES = ("x", "y", "c")
ANY = pl.BlockSpec(memory_space=pl.ANY)

BIG = (("w_in", 2, 1), ("w_branch", 3, 1), ("w_out", 1, 2), ("w_up", 2, 1), ("w_down", 1, 2),
       ("w_ple", 2, 1), ("w_ple_gate", 1, 2))
SMALL_REP = ("norm_mix", "conf_dw_b", "conf_ln_g", "conf_ln_b", "pool_w", "pool_scale", "gmlp_ln_g",
             "gmlp_ln_b", "gmlp_ws", "gmlp_bs", "norm_mlp", "norm_ple", "norm_final")
SMALL_SHARDED = ("conf_dw", "sc_conv")
BY_CHIP = ("w_in", "w_up")
PACK_LANES = 128
PACK_ROW_MULTIPLE = 256


def _params(*semantics):
    return pltpu.CompilerParams(dimension_semantics=semantics, vmem_limit_bytes=VMEM_LIMIT_BYTES)


def _accumulate(ref, value, first):
    @pl.when(first)
    def _():
        ref[...] = value.astype(ref.dtype)

    @pl.when(jnp.logical_not(first))
    def _():
        ref[...] += value.astype(ref.dtype)


def _rowwise(name, fn, rows, params, outs, accs=(), tm=256):
    n_rows = rows[0][0].shape[-2]
    tm = min(tm, n_rows)
    n_in, n_out = len(rows) + len(params), len(outs)

    def body(*refs):
        res = fn(*[r[...] for r in refs[:n_in]])
        res = res if isinstance(res, (tuple, list)) else (res,)
        out_refs = refs[n_in:]
        for r, v in zip(out_refs[:n_out], res[:n_out]):
            r[...] = v.astype(r.dtype)
        first = pl.program_id(0) == 0
        for r, v in zip(out_refs[n_out:], res[n_out:]):
            _accumulate(r, v, first)

    in_specs = []
    for row in rows:
        arr, w, cb = row[:3]
        if len(row) == 4:
            in_specs.append(pl.BlockSpec((None, tm, w), lambda i, cb=cb, g=row[3]: (g, i, cb)))
        else:
            in_specs.append(pl.BlockSpec((tm, w), lambda i, cb=cb: (i, cb)))
    for p in params:
        in_specs.append(pl.BlockSpec(p.shape, lambda i, nd=p.ndim: (0,) * nd))
    out_specs = [pl.BlockSpec((tm, w), lambda i: (i, 0)) for w, _ in outs]
    out_specs += [pl.BlockSpec(s, lambda i, nd=len(s): (0,) * nd) for s in accs]
    out_shape = [jax.ShapeDtypeStruct((n_rows, w), d) for w, d in outs]
    out_shape += [jax.ShapeDtypeStruct(s, F32) for s in accs]
    return pl.pallas_call(
        body, name=name, grid=(n_rows // tm,), in_specs=in_specs, out_specs=out_specs, out_shape=out_shape,
        compiler_params=_params("arbitrary"),
    )(*[r[0] for r in rows], *params)


def _mm(name, a, b, *, ta=False, tb=False, b_by_chip=False, res=(), epi=None, out_dtypes=(F32,), tm=1024, tn=1024, tk=1024):
    g_, m_, k_ = (a.shape[0], a.shape[2], a.shape[1]) if ta else a.shape
    if b_by_chip:
        n_ = b.shape[2] if tb else 4 * b.shape[3]
    else:
        n_ = b.shape[1] if tb else b.shape[2]
    tm, tn, tk = min(tm, m_), min(tn, n_), min(tk, k_)
    if b_by_chip:
        tn, tk = (tn, min(tk, b.shape[3])) if tb else (min(tn, b.shape[3]), tk)
        per_chip = b.shape[3] // (tk if tb else tn)
    nk, n_res, n_out = k_ // tk, len(res), len(out_dtypes)
    dims = (((0 if ta else 1,), (1 if tb else 0,)), ((), ()))
    if epi is None:
        epi = lambda acc: (acc,)

    def body(*refs):
        a_ref, b_ref = refs[:2]
        res_refs, out_refs, acc_ref = refs[2:2 + n_res], refs[2 + n_res:2 + n_res + n_out], refs[-1]
        k = pl.program_id(3)

        @pl.when(k == 0)
        def _():
            acc_ref[...] = jnp.zeros_like(acc_ref)

        acc_ref[...] += lax.dot_general(a_ref[...].astype(BF16), b_ref[...].astype(BF16), dims,
                                        preferred_element_type=F32)

        @pl.when(k == nk - 1)
        def _():
            vals = epi(acc_ref[...], *[r[...] for r in res_refs])
            for o, v in zip(out_refs, vals):
                o[...] = v.astype(o.dtype)

    a_spec = (pl.BlockSpec((None, tk, tm), lambda g, i, j, k: (g, k, i)) if ta
              else pl.BlockSpec((None, tm, tk), lambda g, i, j, k: (g, i, k)))
    if b_by_chip and tb:
        b_spec = pl.BlockSpec((None, None, tn, tk), lambda g, i, j, k: (g, k // per_chip, j, k % per_chip))
    elif b_by_chip:
        b_spec = pl.BlockSpec((None, None, tk, tn), lambda g, i, j, k: (g, j // per_chip, k, j % per_chip))
    elif tb:
        b_spec = pl.BlockSpec((None, tn, tk), lambda g, i, j, k: (g, j, k))
    else:
        b_spec = pl.BlockSpec((None, tk, tn), lambda g, i, j, k: (g, k, j))
    res_specs = [pl.BlockSpec((None, tm, tn), lambda g, i, j, k, off=off: (g, i, j + off)) for _, off in res]
    out_spec = pl.BlockSpec((None, tm, tn), lambda g, i, j, k: (g, i, j))
    outs = pl.pallas_call(
        body, name=name, grid=(g_, m_ // tm, n_ // tn, nk),
        in_specs=[a_spec, b_spec, *res_specs], out_specs=[out_spec] * n_out,
        out_shape=[jax.ShapeDtypeStruct((g_, m_, n_), d) for d in out_dtypes],
        scratch_shapes=[pltpu.VMEM((tm, tn), F32)],
        compiler_params=_params("parallel", "parallel", "parallel", "arbitrary"),
    )(a, b, *[r for r, _ in res])
    return outs


def _mm2(name, a, b, *, res=(), **kw):
    outs = _mm(name, a[None], b[None], res=[(r[None], off) for r, off in res], **kw)
    return [o[0] for o in outs]


@jax.custom_vjp
def _bdot(a, b):
    return jnp.dot(a.astype(BF16), b.astype(BF16), preferred_element_type=F32)


def _bdot_fwd(a, b):
    return _bdot(a, b), (a, b)


def _bdot_bwd(saved, g):
    a, b = saved
    gb = g.astype(BF16)
    da = lax.dot_general(gb, b.astype(BF16), (((1,), (1,)), ((), ())), preferred_element_type=F32)
    db = lax.dot_general(a.astype(BF16), gb, (((0,), (0,)), ((), ())), preferred_element_type=F32)
    return da.astype(a.dtype), db.astype(b.dtype)


_bdot.defvjp(_bdot_fwd, _bdot_bwd)


def _rms(x, g):
    return x * lax.rsqrt(jnp.mean(x * x, axis=-1, keepdims=True) + EPS) * g


def _layer_norm(x, g, b):
    mu = jnp.mean(x, axis=-1, keepdims=True)
    var = jnp.mean(jnp.square(x - mu), axis=-1, keepdims=True)
    return (x - mu) * lax.rsqrt(var + EPS) * g + b


def _glu_and_gate_fn(a, a_gate, cg, hx):
    return a * jax.nn.sigmoid(a_gate), cg * hx


def _branches_fn(ca, pooled, cc, bg, u, v, conf_b, conf_g, conf_beta, pool_scale, gm_g, gm_b, *mats):
    pool_w, ws, bs = mats[0:4], mats[4:8], mats[8:12]
    ya = jax.nn.silu(_layer_norm(ca + conf_b, conf_g, conf_beta))
    yb = jnp.concatenate([_bdot(pooled[:, g * GROUP:(g + 1) * GROUP], pool_w[g]) for g in range(4)], axis=1) * pool_scale
    yc = bg * cc
    vln = _layer_norm(v, gm_g, gm_b)
    causal = lax.broadcasted_iota(jnp.int32, (GROUP, GROUP), 0) >= lax.broadcasted_iota(jnp.int32, (GROUP, GROUP), 1)
    sg = jnp.concatenate(
        [_bdot(jnp.where(causal, ws[g], 0.0), vln[:, g * GROUP:(g + 1) * GROUP]) + bs[g] for g in range(4)], axis=1)
    yd = u * sg
    return ya, yb, yc, yd


def _branches_bwd_fn(*args):
    ins, dys = args[:6] + args[10:], args[6:10]
    _, vjp = jax.vjp(_branches_fn, *ins)
    grads = vjp(tuple(dys))
    return grads


def _glu_bwd_fn(a, a_gate, cg, hx, dga, dchx):
    _, vjp = jax.vjp(_glu_and_gate_fn, a, a_gate, cg, hx)
    return vjp((dga, dchx))


def _rms_bwd_fn(x, dh, dres, g):
    _, vjp = jax.vjp(_rms, x, g)
    dx, dg = vjp(dh)
    return dx + dres, dg


def _loss_fn(x, target, g):
    def f(x, g):
        err = jnp.square(_rms(x, g) - target)
        return 0.5 * jnp.sum(jnp.mean(err, axis=-1, keepdims=True), axis=0, keepdims=True)

    loss, vjp = jax.vjp(f, x, g)
    dx, dg = vjp(jnp.ones((1, 1), F32))
    return dx, dg, jnp.broadcast_to(loss, (1, PACK_LANES))


def _adamw_fn(w, g, m, v):
    m = ADAM_B1 * m + (1.0 - ADAM_B1) * g
    v = ADAM_B2 * v + (1.0 - ADAM_B2) * jnp.square(g)
    m_hat = m / (1.0 - ADAM_B1 ** ADAM_STEP)
    v_hat = v / (1.0 - ADAM_B2 ** ADAM_STEP)
    delta = -ADAM_LR * (m_hat / (jnp.sqrt(v_hat) + ADAM_EPS) + ADAM_WD * w)
    return delta, m, v


def _group_pick(j):
    return [(j == g).astype(F32) for g in range(len(POOL_WINDOWS))]


def _accumulate_row(ref, row, value, first):
    @pl.when(first)
    def _():
        ref[pl.ds(row, 1), :] = value

    @pl.when(jnp.logical_not(first))
    def _():
        ref[pl.ds(row, 1), :] += value


def _rows_iota(n):
    return lax.broadcasted_iota(jnp.int32, (n, 1), 0)


def _seq_specs(n_time, causal):
    per = SEQ_TILE // HALO
    tile = pl.BlockSpec((SEQ_TILE, GROUP), lambda j, i: (i, j))
    if causal:
        halo = pl.BlockSpec((HALO, GROUP), lambda j, i: (jnp.maximum(i * per - 1, 0), j))
    else:
        halo = pl.BlockSpec((HALO, GROUP), lambda j, i: (jnp.minimum((i + 1) * per, n_time * per - 1), j))
    return tile, halo


def _seq_fwd(ga, proj, chx, conf_dw, sc_conv):
    s_len = ga.shape[0]
    n_time = s_len // SEQ_TILE
    ext = HALO + SEQ_TILE

    def body(ga_ref, ga_h, pin_ref, pin_h, chx_ref, chx_h, wc_ref, ws_ref, ca_ref, po_ref, cc_ref):
        j, i = pl.program_id(0), pl.program_id(1)
        keep = (i > 0).astype(F32)

        def extended(cur, halo):
            return jnp.concatenate([halo[...] * keep, cur[...]], axis=0)

        def conv(x, w_ref, taps):
            acc = x * w_ref[pl.ds(taps - 1, 1), :]
            for d in range(1, taps):
                acc = acc + pltpu.roll(x, d, 0) * w_ref[pl.ds(taps - 1 - d, 1), :]
            return acc[HALO:, :]

        ca_ref[...] = conv(extended(ga_ref, ga_h), wc_ref, CONF_KERNEL)
        cc_ref[...] = conv(extended(chx_ref, chx_h), ws_ref, SC_KERNEL)
        u = extended(pin_ref, pin_h)
        s1 = u + pltpu.roll(u, 1, 0)
        s2 = s1 + pltpu.roll(s1, 2, 0)
        s3 = s2 + pltpu.roll(s2, 4, 0)
        s4 = s3 + pltpu.roll(s3, 8, 0)
        pick = _group_pick(j)
        window_sum = s1 * pick[0] + s2 * pick[1] + s3 * pick[2] + s4 * pick[3]
        window = sum(float(w) * f for w, f in zip(POOL_WINDOWS, pick))
        pos = (_rows_iota(ext) + (i * SEQ_TILE - HALO + 1)).astype(F32)
        count = jnp.maximum(jnp.minimum(pos, window), 1.0)
        po_ref[...] = (window_sum / count - u)[HALO:, :]

    tile, halo = _seq_specs(n_time, causal=True)
    pin_tile = pl.BlockSpec((SEQ_TILE, GROUP), lambda j, i: (i, 2 * (MIX // GROUP) + j))
    pin_halo = pl.BlockSpec((HALO, GROUP), lambda j, i: (jnp.maximum(i * (SEQ_TILE // HALO) - 1, 0), 2 * (MIX // GROUP) + j))
    w_spec = lambda taps: pl.BlockSpec((taps, GROUP), lambda j, i: (0, j))
    return pl.pallas_call(
        body, name="seq_fwd", grid=(MIX // GROUP, n_time),
        in_specs=[tile, halo, pin_tile, pin_halo, tile, halo, w_spec(CONF_KERNEL), w_spec(SC_KERNEL)],
        out_specs=[tile] * 3, out_shape=[jax.ShapeDtypeStruct((s_len, MIX), F32)] * 3,
        compiler_params=_params("parallel", "arbitrary"),
    )(ga, ga, proj, proj, chx, chx, conf_dw, sc_conv)


def _seq_bwd(dca, dpooled, dcc, ga, chx, conf_dw, sc_conv):
    s_len = ga.shape[0]
    n_time = s_len // SEQ_TILE
    ext = HALO + SEQ_TILE

    def body(dca_ref, dca_h, dpo_ref, dpo_h, dcc_ref, dcc_h, ga_ref, ga_h, chx_ref, chx_h, wc_ref, ws_ref,
             dga_ref, dpin_ref, dchx_ref, dwc_ref, dws_ref):
        j, i = pl.program_id(0), pl.program_id(1)
        keep_prev = (i > 0).astype(F32)
        keep_next = (i < n_time - 1).astype(F32)

        def with_next(cur, halo):
            return jnp.concatenate([cur[...], halo[...] * keep_next], axis=0)

        def with_prev(cur, halo):
            return jnp.concatenate([halo[...] * keep_prev, cur[...]], axis=0)

        def conv_t(dy, w_ref, taps):
            acc = dy * w_ref[pl.ds(taps - 1, 1), :]
            for d in range(1, taps):
                acc = acc + pltpu.roll(dy, ext - d, 0) * w_ref[pl.ds(taps - 1 - d, 1), :]
            return acc[:SEQ_TILE, :]

        def tap_grads(dy_cur, x, dw_ref, taps):
            for d in range(taps):
                shifted = x if d == 0 else pltpu.roll(x, d, 0)
                part = jnp.sum(dy_cur * shifted[HALO:, :], axis=0, keepdims=True)
                _accumulate_row(dw_ref, taps - 1 - d, part, i == 0)

        dga_ref[...] = conv_t(with_next(dca_ref, dca_h), wc_ref, CONF_KERNEL)
        dchx_ref[...] = conv_t(with_next(dcc_ref, dcc_h), ws_ref, SC_KERNEL)
        tap_grads(dca_ref[...], with_prev(ga_ref, ga_h), dwc_ref, CONF_KERNEL)
        tap_grads(dcc_ref[...], with_prev(chx_ref, chx_h), dws_ref, SC_KERNEL)

        dpo = with_next(dpo_ref, dpo_h)
        pick = _group_pick(j)
        window = sum(float(w) * f for w, f in zip(POOL_WINDOWS, pick))
        pos = (_rows_iota(ext) + (i * SEQ_TILE + 1)).astype(F32)
        r = dpo / jnp.minimum(pos, window)
        q1 = r + pltpu.roll(r, ext - 1, 0)
        q2 = q1 + pltpu.roll(q1, ext - 2, 0)
        q3 = q2 + pltpu.roll(q2, ext - 4, 0)
        q4 = q3 + pltpu.roll(q3, ext - 8, 0)
        q = q1 * pick[0] + q2 * pick[1] + q3 * pick[2] + q4 * pick[3]
        dpin_ref[...] = (q - dpo)[:SEQ_TILE, :].astype(dpin_ref.dtype)

    tile, nxt = _seq_specs(n_time, causal=False)
    _, prv = _seq_specs(n_time, causal=True)
    w_spec = lambda taps: pl.BlockSpec((taps, GROUP), lambda j, i: (0, j))
    return pl.pallas_call(
        body, name="seq_bwd", grid=(MIX // GROUP, n_time),
        in_specs=[tile, nxt, tile, nxt, tile, nxt, tile, prv, tile, prv, w_spec(CONF_KERNEL), w_spec(SC_KERNEL)],
        out_specs=[tile, tile, tile, w_spec(CONF_KERNEL), w_spec(SC_KERNEL)],
        out_shape=[jax.ShapeDtypeStruct((s_len, MIX), F32), jax.ShapeDtypeStruct((s_len, MIX), BF16),
                   jax.ShapeDtypeStruct((s_len, MIX), F32), jax.ShapeDtypeStruct(conf_dw.shape, F32),
                   jax.ShapeDtypeStruct(sc_conv.shape, F32)],
        compiler_params=_params("parallel", "arbitrary"),
    )(dca, dca, dpooled, dpooled, dcc, dcc, ga, ga, chx, chx, conf_dw, sc_conv)


MERGE_TILE = 512


def _merge_fwd(y, w_branch, proj):
    s_len, d_model = y.shape[1], w_branch.shape[2]
    t = min(MERGE_TILE, s_len)
    gate_block0 = (proj.shape[1] - 4 * d_model) // MERGE_TILE
    per_branch = d_model // MERGE_TILE

    def body(y_ref, w_ref, gate_ref, out_ref, acc_ref):
        k = pl.program_id(2)

        @pl.when(k == 0)
        def _():
            acc_ref[...] = jnp.zeros_like(acc_ref)

        z = jnp.dot(y_ref[...], w_ref[...], preferred_element_type=F32)
        acc_ref[...] += jax.nn.sigmoid(gate_ref[...]) * z

        @pl.when(k == 3)
        def _():
            out_ref[...] = acc_ref[...].astype(out_ref.dtype)

    return pl.pallas_call(
        body, name="merge_fwd", grid=(s_len // t, per_branch, 4),
        in_specs=[pl.BlockSpec((None, t, MIX), lambda i, j, k: (k, i, 0)),
                  pl.BlockSpec((None, MIX, MERGE_TILE), lambda i, j, k: (k, 0, j)),
                  pl.BlockSpec((t, MERGE_TILE), lambda i, j, k: (i, gate_block0 + k * per_branch + j))],
        out_specs=pl.BlockSpec((t, MERGE_TILE), lambda i, j, k: (i, j)),
        out_shape=jax.ShapeDtypeStruct((s_len, d_model), BF16),
        scratch_shapes=[pltpu.VMEM((t, MERGE_TILE), F32)],
        compiler_params=_params("parallel", "parallel", "arbitrary"),
    )(y, w_branch, proj)


def _merge_bwd(y, w_branch, proj, dmerged):
    s_len, d_model = y.shape[1], w_branch.shape[2]
    t = min(MERGE_TILE, s_len)
    gate_block0 = (proj.shape[1] - 4 * d_model) // MERGE_TILE
    per_branch = d_model // MERGE_TILE

    def body(y_ref, w_ref, gate_ref, dm_ref, dz_ref, dgate_ref):
        z = jnp.dot(y_ref[...], w_ref[...], preferred_element_type=F32)
        gate = jax.nn.sigmoid(gate_ref[...])
        dm = dm_ref[...]
        dz_ref[...] = (dm * gate).astype(dz_ref.dtype)
        dgate_ref[...] = (dm * z * gate * (1.0 - gate)).astype(dgate_ref.dtype)

    return pl.pallas_call(
        body, name="merge_bwd", grid=(s_len // t, per_branch, 4),
        in_specs=[pl.BlockSpec((None, t, MIX), lambda i, j, k: (k, i, 0)),
                  pl.BlockSpec((None, MIX, MERGE_TILE), lambda i, j, k: (k, 0, j)),
                  pl.BlockSpec((t, MERGE_TILE), lambda i, j, k: (i, gate_block0 + k * per_branch + j)),
                  pl.BlockSpec((t, MERGE_TILE), lambda i, j, k: (i, j))],
        out_specs=[pl.BlockSpec((None, t, MERGE_TILE), lambda i, j, k: (k, i, j)),
                   pl.BlockSpec((t, MERGE_TILE), lambda i, j, k: (i, k * per_branch + j))],
        out_shape=[jax.ShapeDtypeStruct((4, s_len, d_model), BF16), jax.ShapeDtypeStruct((s_len, 4 * d_model), BF16)],
        compiler_params=_params("parallel", "parallel", "arbitrary"),
    )(y, w_branch, proj, dmerged)


def _row(v):
    return v.reshape(1, -1)


def _branch_params(w):
    mats = [w["pool_w"][g] for g in range(4)] + [w["gmlp_ws"][g] for g in range(4)]
    mats += [w["gmlp_bs"][g].reshape(GROUP, 1) for g in range(4)]
    return [_row(w["conf_dw_b"]), _row(w["conf_ln_g"]), _row(w["conf_ln_b"]), _row(w["pool_scale"]),
            _row(w["gmlp_ln_g"]), _row(w["gmlp_ln_b"])] + mats


def _layer_fwd(x0, p_l, w):
    d_model = x0.shape[1]
    (h1,) = _rowwise("rms_mix", _rms, [(x0, d_model, 0)], [_row(w["norm_mix"])], [(d_model, BF16)])
    (proj,) = _mm2("proj", h1, w["w_in"], b_by_chip=True)
    ga, chx = _rowwise("glu", _glu_and_gate_fn, [(proj, MIX, 0), (proj, MIX, 1), (proj, MIX, 4), (proj, MIX, 5)], [],
                       [(MIX, F32), (MIX, F32)])
    ca, pooled, cc = _seq_fwd(ga, proj, chx, w["conf_dw"], w["sc_conv"])
    ys = _rowwise("branches", _branches_fn,
                  [(ca, MIX, 0), (pooled, MIX, 0), (cc, MIX, 0), (proj, MIX, 3), (proj, MIX, 6), (proj, MIX, 7)],
                  _branch_params(w), [(MIX, BF16)] * 4, tm=GROUP)
    y = jnp.stack(ys)
    merged = _merge_fwd(y, w["w_branch"], proj)
    (x1,) = _mm2("out_proj", merged, w["w_out"], res=[(x0, 0)], epi=lambda acc, r: (acc + r,))
    (h2,) = _rowwise("rms_mlp", _rms, [(x1, d_model, 0)], [_row(w["norm_mlp"])], [(d_model, BF16)])
    up, act = _mm2("mlp_up", h2, w["w_up"], b_by_chip=True, out_dtypes=(F32, BF16),
                   epi=lambda acc: (acc, jnp.square(jnp.maximum(acc, 0.0))))
    (x2,) = _mm2("mlp_down", act, w["w_down"], res=[(x1, 0)], epi=lambda acc, r: (acc + r,))
    (h3,) = _rowwise("rms_ple", _rms, [(x2, d_model, 0)], [_row(w["norm_ple"])], [(d_model, BF16)])
    (pg,) = _mm2("ple_gate", h3, w["w_ple_gate"])
    (x3,) = _mm2("ple", p_l, w["w_ple"], res=[(pg, 0), (x2, 0)],
                 epi=lambda acc, g, r: (acc * jax.nn.sigmoid(g) + r,))
    saved = dict(x0=x0, h1=h1, proj=proj, ga=ga, chx=chx, ca=ca, pooled=pooled, cc=cc, y=y, merged=merged, x1=x1,
                 h2=h2, up=up, act=act, x2=x2, h3=h3, pg=pg)
    return x3, saved


def _layer_bwd(dx3, p_l, w, s):
    d_model = dx3.shape[1]
    big, small = {}, {}

    def ple_epi(acc, dx, g):
        sig = jax.nn.sigmoid(g)
        return dx * sig, dx * acc * sig * (1.0 - sig)

    dpp, dpg = _mm2("ple_bwd", p_l, w["w_ple"], res=[(dx3, 0), (s["pg"], 0)], epi=ple_epi, out_dtypes=(BF16, BF16))
    (big["w_ple"],) = _mm2("dw_ple", p_l, dpp, ta=True, out_dtypes=(BF16,))
    (big["w_ple_gate"],) = _mm2("dw_ple_gate", s["h3"], dpg, ta=True, out_dtypes=(BF16,))
    (dh3,) = _mm2("dh_ple", dpg, w["w_ple_gate"], tb=True)
    dx2, small["norm_ple"] = _rowwise("rms_ple_bwd", _rms_bwd_fn, [(s["x2"], d_model, 0), (dh3, d_model, 0), (dx3, d_model, 0)],
                                      [_row(w["norm_ple"])], [(d_model, F32)], [(1, d_model)])

    (dup,) = _mm2("mlp_down_bwd", dx2, w["w_down"], tb=True, res=[(s["up"], 0)], out_dtypes=(BF16,),
                  epi=lambda acc, up: (acc * 2.0 * jnp.maximum(up, 0.0),))
    (big["w_down"],) = _mm2("dw_down", s["act"], dx2, ta=True, out_dtypes=(BF16,))
    (big["w_up"],) = _mm2("dw_up", s["h2"], dup, ta=True, out_dtypes=(BF16,))
    (dh2,) = _mm2("mlp_up_bwd", dup, w["w_up"], tb=True, b_by_chip=True)
    dx1, small["norm_mlp"] = _rowwise("rms_mlp_bwd", _rms_bwd_fn, [(s["x1"], d_model, 0), (dh2, d_model, 0), (dx2, d_model, 0)],
                                      [_row(w["norm_mlp"])], [(d_model, F32)], [(1, d_model)])

    (dmerged,) = _mm2("out_proj_bwd", dx1, w["w_out"], tb=True)
    (big["w_out"],) = _mm2("dw_out", s["merged"], dx1, ta=True, out_dtypes=(BF16,))
    dz, dgate = _merge_bwd(s["y"], w["w_branch"], s["proj"], dmerged)
    (dy,) = _mm("branch_bwd", dz, w["w_branch"], tb=True)
    (big["w_branch"],) = _mm("dw_branch", s["y"], dz, ta=True, out_dtypes=(BF16,))

    proj = s["proj"]
    params = _branch_params(w)
    grads = _rowwise(
        "branches_bwd", _branches_bwd_fn,
        [(s["ca"], MIX, 0), (s["pooled"], MIX, 0), (s["cc"], MIX, 0), (proj, MIX, 3), (proj, MIX, 6), (proj, MIX, 7)]
        + [(dy, MIX, 0, g) for g in range(4)], params,
        [(MIX, F32), (MIX, F32), (MIX, F32), (MIX, BF16), (MIX, BF16), (MIX, BF16)], [p.shape for p in params], tm=GROUP)
    dca, dpooled, dcc, dbg, du, dv = grads[:6]
    pg_ = grads[6:]
    small["conf_dw_b"], small["conf_ln_g"], small["conf_ln_b"], small["pool_scale"] = pg_[0], pg_[1], pg_[2], pg_[3]
    small["gmlp_ln_g"], small["gmlp_ln_b"] = pg_[4], pg_[5]
    small["pool_w"] = jnp.stack(pg_[6:10])
    small["gmlp_ws"] = jnp.stack(pg_[10:14])
    small["gmlp_bs"] = jnp.stack([b.reshape(GROUP) for b in pg_[14:18]])

    dga, dpin, dchx, small["conf_dw"], small["sc_conv"] = _seq_bwd(dca, dpooled, dcc, s["ga"], s["chx"], w["conf_dw"], w["sc_conv"])
    da, dag, dcg, dhx = _rowwise("glu_bwd", _glu_bwd_fn,
                                 [(proj, MIX, 0), (proj, MIX, 1), (proj, MIX, 4), (proj, MIX, 5), (dga, MIX, 0), (dchx, MIX, 0)],
                                 [], [(MIX, BF16)] * 4)
    dproj = jnp.concatenate([da, dag, dpin, dbg, dcg, dhx, du, dv, dgate], axis=1)
    (big["w_in"],) = _mm2("dw_in", s["h1"], dproj, ta=True, out_dtypes=(BF16,))
    (dh1,) = _mm2("proj_bwd", dproj, w["w_in"], tb=True, b_by_chip=True)
    dx0, small["norm_mix"] = _rowwise("rms_mix_bwd", _rms_bwd_fn, [(s["x0"], d_model, 0), (dh1, d_model, 0), (dx1, d_model, 0)],
                                      [_row(w["norm_mix"])], [(d_model, F32)], [(1, d_model)])
    return dx0, big, small


def _position():
    return lax.axis_index("x"), lax.axis_index("y"), lax.axis_index("c")


def _other_chips(x, y):
    return [(1 - x, y), (x, 1 - y), (1 - x, 1 - y)]


def _region(ref, cuts):
    return ref.at[tuple(pl.ds(*cuts[a]) if a in cuts else slice(None) for a in range(len(ref.shape)))]


def _allgather_small(name, block):
    m_per, n = block.shape

    def body(x_ref, out_ref, send_sems, recv_sems, local_sem):
        x, y, c = _position()
        me, sibling = (x, y, c), (x, y, 1 - c)
        chips = _other_chips(x, y)

        def rows(px, py, pc):
            return out_ref.at[pl.ds((4 * px + 2 * py + pc) * m_per, m_per), :]

        def copy(k, blk, to, src=None):
            return pltpu.make_async_remote_copy(
                src_ref=rows(*blk) if src is None else src, dst_ref=rows(*blk), send_sem=send_sems.at[k],
                recv_sem=recv_sems.at[k], device_id=to, device_id_type=pl.DeviceIdType.MESH)

        mine = pltpu.make_async_copy(x_ref, rows(*me), local_sem)
        mine.start()
        first = [copy(0, me, sibling, src=x_ref)]
        first += [copy(1 + j, me, (*chip, c), src=x_ref) for j, chip in enumerate(chips)]
        for cp in first:
            cp.start()
        passed = [copy(4 + j, (*chip, c), sibling) for j, chip in enumerate(chips)]
        for j, chip in enumerate(chips):
            copy(1 + j, (*chip, c), me).wait_recv()
            passed[j].start()
        copy(0, sibling, me).wait_recv()
        for j, chip in enumerate(chips):
            copy(4 + j, (*chip, 1 - c), me).wait_recv()
        for cp in first + passed:
            cp.wait_send()
        mine.wait()

    return pl.pallas_call(
        body, name=name, out_shape=jax.ShapeDtypeStruct((8 * m_per, n), block.dtype), in_specs=[ANY], out_specs=ANY,
        scratch_shapes=[pltpu.SemaphoreType.DMA((7,)), pltpu.SemaphoreType.DMA((7,)), pltpu.SemaphoreType.DMA],
    )(block)


def _cuts(arr_shape, shard_axis, half_axis, shard=None, half=None):
    cuts = {}
    if shard is not None:
        n = arr_shape[shard_axis] // 4
        cuts[shard_axis] = (shard * n, n)
    if half is not None:
        n = arr_shape[half_axis] // 2
        cuts[half_axis] = (half * n, n)
    return cuts


def _full_shape(shard_shape, shard_axis):
    return tuple(4 * n if a == shard_axis else n for a, n in enumerate(shard_shape))


HBM = pl.BlockSpec(memory_space=pltpu.HBM)
SEM = pl.BlockSpec(memory_space=pltpu.SEMAPHORE)
DATAFLOW_EFFECT = pltpu.SideEffectType.DATAFLOW_SIDE_EFFECTING
TOKEN_SHAPE = (8, 128)


def _remote(src, dst, send_sems, recv_sems, k, peer):
    return pltpu.make_async_remote_copy(src_ref=src, dst_ref=dst, send_sem=send_sems.at[k], recv_sem=recv_sems.at[k],
                                        device_id=peer, device_id_type=pl.DeviceIdType.MESH)


def _split_start(name, sources, landings, plan):
    n_s, n_l = len(sources), len(landings)
    n_copies = len(plan([None] * n_s, [None] * n_l, None))

    def body(*refs):
        src, land = refs[:n_s], refs[n_s:n_s + n_l]
        send_sems, recv_sems, token = refs[n_s + n_l], refs[n_s + n_l + 1], refs[-1]
        for k, (s, d, peer) in enumerate(plan(src, land, _position())):
            _remote(s, d, send_sems, recv_sems, k, peer).start()
        token[...] = jnp.zeros_like(token)

    arrays = [pltpu.with_memory_space_constraint(a, pltpu.HBM) for a in (*sources, *landings)]
    outs = pl.pallas_call(
        body, name=name,
        out_shape=(pltpu.SemaphoreType.DMA((n_copies,)), pltpu.SemaphoreType.DMA((n_copies,)),
                   *[pltpu.HBM(a.shape, a.dtype) for a in arrays], jax.ShapeDtypeStruct(TOKEN_SHAPE, F32)),
        in_specs=[HBM] * (n_s + n_l),
        out_specs=(SEM, SEM, *[HBM] * (n_s + n_l), pl.BlockSpec(memory_space=pltpu.VMEM)),
        input_output_aliases={i: 2 + i for i in range(n_s + n_l)},
        compiler_params=pltpu.CompilerParams(has_side_effects=DATAFLOW_EFFECT),
    )(*arrays)
    return outs[0], outs[1], list(outs[2:2 + n_s]), list(outs[2 + n_s:2 + n_s + n_l]), outs[-1]


def _split_wait(name, started, after, plan):
    send_sems, recv_sems, sources, landings, _ = started
    n_s, n_l = len(sources), len(landings)

    def body(*refs):
        src, land = refs[:n_s], refs[n_s:n_s + n_l]
        send_sems_ref, recv_sems_ref = refs[n_s + n_l], refs[n_s + n_l + 1]
        for k, (s, d, peer) in enumerate(plan(src, land, _position())):
            cp = _remote(s, d, send_sems_ref, recv_sems_ref, k, peer)
            cp.wait_send()
            cp.wait_recv()

    outs = pl.pallas_call(
        body, name=name, out_shape=tuple(pltpu.HBM(a.shape, a.dtype) for a in (*sources, *landings)),
        in_specs=[*[HBM] * (n_s + n_l), SEM, SEM, ANY], out_specs=tuple([HBM] * (n_s + n_l)),
        input_output_aliases={i: i for i in range(n_s + n_l)},
        compiler_params=pltpu.CompilerParams(has_side_effects=DATAFLOW_EFFECT),
    )(*sources, *landings, send_sems, recv_sems, after)
    return list(outs[:n_s]), list(outs[n_s:])


def _exchange(name, inputs, out_shapes, plan, in_place=False):
    n_in = len(inputs)
    n_out = n_in if in_place else len(out_shapes)
    n_copies = len(plan([None] * n_in, [None] * n_out, None))

    def body(*refs):
        ins, outs = refs[:n_in], refs[n_in:n_in + n_out]
        send_sems, recv_sems = refs[n_in + n_out:]
        copies = [_remote(s, d, send_sems, recv_sems, k, peer)
                  for k, (s, d, peer) in enumerate(plan(ins, ins if in_place else outs, _position()))]
        for cp in copies:
            cp.start()
        for cp in copies:
            cp.wait()

    if in_place:
        out_shapes = [jax.ShapeDtypeStruct(a.shape, a.dtype) for a in inputs]
    return pl.pallas_call(
        body, name=name, in_specs=[ANY] * n_in, out_specs=[ANY] * n_out, out_shape=out_shapes,
        input_output_aliases={i: i for i in range(n_in)} if in_place else {},
        scratch_shapes=[pltpu.SemaphoreType.DMA((n_copies,)), pltpu.SemaphoreType.DMA((n_copies,))],
    )(*inputs)


def _half_rows(ref, half):
    n = ref.shape[0] // 2
    return ref.at[pl.ds(half * n, n)]


def _gather_plan(sources, landings, pos):
    if pos is None:
        return [None] * (3 * len(sources))
    x, y, c = pos
    copies = []
    for src, land in zip(sources, landings):
        for qx, qy in _other_chips(x, y):
            copies.append((_half_rows(src, c), _half_rows(land.at[2 * x + y], c), (qx, qy, c)))
    return copies


def _sibling_plan(landings, _, pos):
    if pos is None:
        return [None] * (3 * len(landings))
    x, y, c = pos
    copies = []
    for land in landings:
        for qx, qy in _other_chips(x, y):
            region = _half_rows(land.at[2 * qx + qy], c)
            copies.append((region, region, (x, y, 1 - c)))
    return copies


def _weights_start(layer, shards):
    landings = [lax.empty((4, *s.shape), BF16) for s in shards]
    return _split_start(f"weights_start_{layer}", shards, landings, _gather_plan)


def _weights_finish(layer, started, after):
    shards, landings = _split_wait(f"weights_wait_{layer}", started, after, _gather_plan)
    landings = _exchange(f"weights_to_sibling_{layer}", landings, None, _sibling_plan, in_place=True)
    my_shard = 2 * lax.axis_index("x") + lax.axis_index("y")
    full = {}
    for (name, sa, _), shard, land in zip(BIG, shards, landings):
        land = lax.dynamic_update_index_in_dim(land, shard, my_shard, axis=0)
        if name in BY_CHIP:
            full[name] = land
            continue
        axis = sa - 1
        moved = jnp.moveaxis(land, 0, axis)
        full[name] = moved.reshape(*moved.shape[:axis], 4 * moved.shape[axis + 1], *moved.shape[axis + 2:])
    return full


def _half_shape(shape, half_axis):
    return tuple(n // 2 if a == half_axis else n for a, n in enumerate(shape))


def _shard_shape(shape, shard_axis):
    return tuple(n // 4 if a == shard_axis else n for a, n in enumerate(shape))


def _flat2d(a):
    return a.reshape(-1, a.shape[-1])


ELEMENTWISE_TILE_ELEMENTS = 1 << 18


def _row_tile(rows, cols):
    tile = 1
    while rows % (2 * tile) == 0 and 2 * tile * cols <= ELEMENTWISE_TILE_ELEMENTS:
        tile *= 2
    assert tile >= 16 or tile == rows, (rows, cols)
    return tile


def _sum_arrays(name, arrays, out_dtype):
    shape = arrays[0].shape
    flat = [_flat2d(a) for a in arrays]
    rows, cols = flat[0].shape
    tm = _row_tile(rows, cols)

    def fn(*tiles):
        acc = tiles[0].astype(F32)
        for t in tiles[1:]:
            acc = acc + t.astype(F32)
        return acc

    (out,) = _rowwise(name, fn, [(a, cols, 0) for a in flat], [], [(cols, out_dtype)], tm=tm)
    return out.reshape(shape)


def _grads_start(layer, grads):
    c = lax.axis_index("c")
    shapes = [g.shape for g in grads]
    axes = [(sa - 1, ha - 1) for _, sa, ha in BIG]

    def to_sibling(ins, outs, pos):
        if pos is None:
            return [None] * len(BIG)
        px, py, pc = pos
        return [(_region(ins[wi], _cuts(shapes[wi], sa, ha, half=1 - pc)), outs[wi], (px, py, 1 - pc))
                for wi, (sa, ha) in enumerate(axes)]

    landed = _exchange("grads_to_sibling", grads, [jax.ShapeDtypeStruct(_half_shape(s, ha), BF16) for s, (_, ha) in zip(shapes, axes)],
                       to_sibling)
    chip_sums = []
    for wi, (name, _, _) in enumerate(BIG):
        ha = axes[wi][1]
        n = shapes[wi][ha] // 2
        mine = lax.dynamic_slice_in_dim(grads[wi], c * n, n, axis=ha)
        chip_sums.append(_sum_arrays("chip_sum_" + name, [mine, landed[wi]], BF16))

    def to_chips(sources, landings, pos):
        if pos is None:
            return [None] * (3 * len(BIG))
        px, py, pc = pos
        copies = []
        for wi, (sa, ha) in enumerate(axes):
            for j, (qx, qy) in enumerate(_other_chips(px, py)):
                piece = _region(sources[wi], _cuts(shapes[wi], sa, ha, shard=2 * qx + qy))
                copies.append((piece, landings[wi].at[j], (qx, qy, pc)))
        return copies

    landings = [lax.empty((3, *_shard_shape(t.shape, sa)), BF16) for t, (sa, _) in zip(chip_sums, axes)]
    return _split_start(f"grads_start_{layer}", chip_sums, landings, to_chips), to_chips


def _grads_finish(layer, started, after):
    chip_sums, pieces = _split_wait(f"grads_wait_{layer}", started[0], after, started[1])
    my_shard = 2 * lax.axis_index("x") + lax.axis_index("y")
    reduced = []
    for (name, sa, _), chip_sum, piece in zip(BIG, chip_sums, pieces):
        n = chip_sum.shape[sa - 1] // 4
        mine = lax.dynamic_slice_in_dim(chip_sum, my_shard * n, n, axis=sa - 1)
        reduced.append(_sum_arrays("shard_sum_" + name, [mine, piece[0], piece[1], piece[2]], F32))
    return reduced


def _join_halves(reduced):
    c = lax.axis_index("c")

    def halves_to_sibling(ins, outs, pos):
        if pos is None:
            return [None] * len(BIG)
        px, py, pc = pos
        return [(ins[wi], outs[wi], (px, py, 1 - pc)) for wi in range(len(BIG))]

    from_sibling = _exchange("grad_halves_to_sibling", reduced, [jax.ShapeDtypeStruct(r.shape, F32) for r in reduced],
                             halves_to_sibling)
    out = []
    for wi, (_, _, ha) in enumerate(BIG):
        low = jnp.where(c == 0, reduced[wi], from_sibling[wi])
        high = jnp.where(c == 0, from_sibling[wi], reduced[wi])
        out.append(jnp.concatenate([low, high], axis=ha))
    return out


def _pack(arrays):
    flat = jnp.concatenate([a.reshape(-1).astype(F32) for a in arrays])
    rows = -(-flat.shape[0] // PACK_LANES)
    rows = -(-rows // PACK_ROW_MULTIPLE) * PACK_ROW_MULTIPLE
    return jnp.pad(flat, (0, rows * PACK_LANES - flat.shape[0])).reshape(rows, PACK_LANES)


def _unpack(buf, shapes):
    flat, out, off = buf.reshape(-1), [], 0
    for s in shapes:
        n = 1
        for d in s:
            n *= d
        out.append(flat[off:off + n].reshape(s))
        off += n
    return out


def _sum8(name, gathered, rows):
    def body(in_ref, out_ref):
        acc = in_ref[0]
        for d in range(1, 8):
            acc = acc + in_ref[d]
        out_ref[...] = acc

    return pl.pallas_call(
        body, name=name, grid=(rows // PACK_ROW_MULTIPLE,),
        in_specs=[pl.BlockSpec((8, PACK_ROW_MULTIPLE, PACK_LANES), lambda i: (0, i, 0))],
        out_specs=pl.BlockSpec((PACK_ROW_MULTIPLE, PACK_LANES), lambda i: (i, 0)),
        out_shape=jax.ShapeDtypeStruct((rows, PACK_LANES), F32), compiler_params=_params("arbitrary"),
    )(gathered.reshape(8, rows, PACK_LANES))


def _adamw(name, w, g, m, v):
    shape = w.shape
    flat = [_flat2d(a) for a in (w, g, m, v)]
    rows, cols = flat[0].shape
    tm = _row_tile(rows, cols)
    outs = _rowwise(name, _adamw_fn, [(a, cols, 0) for a in flat], [], [(cols, F32)] * 3, tm=tm)
    return [o.reshape(shape) for o in outs]


WEIGHT_ORDER = ("norm_mix", "w_in", "conf_dw", "conf_dw_b", "conf_ln_g", "conf_ln_b", "pool_w", "pool_scale", "sc_conv",
                "gmlp_ln_g", "gmlp_ln_b", "gmlp_ws", "gmlp_bs", "w_branch", "w_out", "norm_mlp", "w_up", "w_down",
                "norm_ple", "w_ple", "w_ple_gate", "norm_final")


def _local_step(x, p, loss_target, weights, layer_big, full_small_sharded, on_big_grads):
    n_layers = p.shape[0]
    d_model = x.shape[1]
    layer_w, saved = [], []
    h = x
    for l in range(n_layers):
        w = {n: weights[n][l] for n in SMALL_REP if n != "norm_final"}
        w.update(layer_big(l, h))
        w.update({n: full_small_sharded[n][l] for n in SMALL_SHARDED})
        layer_w.append(w)
        h, s = _layer_fwd(h, p[l], w)
        saved.append(s)
    dx, d_final, loss = _rowwise("loss", _loss_fn, [(h, d_model, 0), (loss_target, d_model, 0)], [_row(weights["norm_final"])],
                                 [(d_model, F32)], [(1, d_model), (1, PACK_LANES)])
    small_grads = {n: [None] * n_layers for n in SMALL_REP + SMALL_SHARDED if n != "norm_final"}
    for l in reversed(range(n_layers)):
        dx, big, small = _layer_bwd(dx, p[l], layer_w[l], saved[l])
        dx = on_big_grads(l, [big[n] for n, _, _ in BIG], dx)
        for n in small:
            small_grads[n][l] = small[n].reshape(weights[n].shape[1:]) if n in SMALL_REP else small[n]
    small_grads = {n: jnp.stack(v) for n, v in small_grads.items()}
    small_grads["norm_final"] = d_final.reshape(-1)
    return loss, dx, small_grads


def kernel(x, p, norm_mix, w_in, conf_dw, conf_dw_b, conf_ln_g, conf_ln_b, pool_w, pool_scale, sc_conv, gmlp_ln_g, gmlp_ln_b, gmlp_ws, gmlp_bs, w_branch, w_out, norm_mlp, w_up, w_down, norm_ple, w_ple, w_ple_gate, norm_final, loss_target, m_norm_mix, m_w_in, m_conf_dw, m_conf_dw_b, m_conf_ln_g, m_conf_ln_b, m_pool_w, m_pool_scale, m_sc_conv, m_gmlp_ln_g, m_gmlp_ln_b, m_gmlp_ws, m_gmlp_bs, m_w_branch, m_w_out, m_norm_mlp, m_w_up, m_w_down, m_norm_ple, m_w_ple, m_w_ple_gate, m_norm_final, v_norm_mix, v_w_in, v_conf_dw, v_conf_dw_b, v_conf_ln_g, v_conf_ln_b, v_pool_w, v_pool_scale, v_sc_conv, v_gmlp_ln_g, v_gmlp_ln_b, v_gmlp_ws, v_gmlp_bs, v_w_branch, v_w_out, v_norm_mlp, v_w_up, v_w_down, v_norm_ple, v_w_ple, v_w_ple_gate, v_norm_final):
    given = dict(locals())
    weights = {n: given[n] for n in WEIGHT_ORDER}
    mom_m = {n: given["m_" + n] for n in WEIGHT_ORDER}
    mom_v = {n: given["v_" + n] for n in WEIGHT_ORDER}
    my_shard = 2 * lax.axis_index("x") + lax.axis_index("y")

    n_layers = p.shape[0]
    shard_shapes = [weights[n].shape for n in SMALL_SHARDED]
    gathered = _allgather_small("allgather_small_weights", _pack([weights[n] for n in SMALL_SHARDED]))
    after_small = gathered[0, 0] * 0.0
    after_small = jnp.where(after_small == 0.0, after_small, 0.0)

    def bf16_shards(l):
        return [(weights[n][l] + after_small if n == "w_ple" else weights[n][l]).astype(BF16) for n, _, _ in BIG]

    weight_copies = [_weights_start(l, bf16_shards(l)) for l in range(n_layers)]
    started = sum(cp[4][0, 0] for cp in weight_copies)
    per_dev = gathered.reshape(8, -1, PACK_LANES)
    full_small_sharded = {}
    chip_parts = [_unpack(per_dev[2 * s], shard_shapes) for s in range(4)]
    for i, n in enumerate(SMALL_SHARDED):
        full_small_sharded[n] = jnp.concatenate([chip_parts[s][i] for s in range(4)], axis=-1)

    grad_copies = [None] * n_layers

    def layer_big(l, h):
        return _weights_finish(l, weight_copies[l], h)

    def on_big_grads(l, layer_grads, dx):
        grad_copies[l] = _grads_start(l, layer_grads)
        return dx + grad_copies[l][0][4][0, 0]

    loss_row, dx, small_grads = _local_step(x[0] + started, p[:, 0], loss_target[0], weights, layer_big, full_small_sharded,
                                            on_big_grads)
    loss = lax.psum(loss_row[0, 0], MESH_AXES)

    halves = [_grads_finish(l, grad_copies[l], dx) for l in reversed(range(n_layers))][::-1]
    big_reduced = _join_halves([jnp.stack([halves[l][wi] for l in range(n_layers)]) for wi in range(len(BIG))])
    grads = {n: g for (n, _, _), g in zip(BIG, big_reduced)}

    small_names = SMALL_REP + SMALL_SHARDED
    packed = _pack([small_grads[n] for n in small_names])
    rows = packed.shape[0]
    summed = _sum8("sum_small_grads", _allgather_small("allgather_small_grads", packed), rows)
    small_full_shapes = [weights[n].shape for n in SMALL_REP] + [full_small_sharded[n].shape for n in SMALL_SHARDED]
    for n, g in zip(small_names, _unpack(summed, small_full_shapes)):
        if n in SMALL_SHARDED:
            width = weights[n].shape[-1]
            g = lax.dynamic_slice_in_dim(g, my_shard * width, width, axis=g.ndim - 1)
        grads[n] = g

    delta, new_m, new_v = {}, {}, {}
    for n, _, _ in BIG:
        delta[n], new_m[n], new_v[n] = _adamw("adamw_" + n, weights[n], grads[n], mom_m[n], mom_v[n])
    for group, names in (("rep", SMALL_REP), ("sharded", SMALL_SHARDED)):
        outs = _adamw("adamw_small_" + group, *[_pack([src[n] for n in names]) for src in (weights, grads, mom_m, mom_v)])
        shapes = [weights[n].shape for n in names]
        for dst, buf in zip((delta, new_m, new_v), outs):
            for n, a in zip(names, _unpack(buf, shapes)):
                dst[n] = a

    return (loss, dx[None], *[grads[n] for n in WEIGHT_ORDER], *[delta[n] for n in WEIGHT_ORDER],
            *[new_m[n] for n in WEIGHT_ORDER], *[new_v[n] for n in WEIGHT_ORDER])
```

```python
import functools

import jax
import jax.numpy as jnp
from jax import lax
from jax.experimental import pallas as pl
from jax.experimental.pallas import tpu as pltpu

F32, BF16 = jnp.float32, jnp.bfloat16
EPS = 1e-6
ADAM_LR, ADAM_B1, ADAM_B2, ADAM_EPS, ADAM_WD, ADAM_STEP = 0.001, 0.9, 0.999, 1e-08, 0.01, 10
CONF_KERNEL, SC_KERNEL = 31, 3
POOL_WINDOWS = (2, 4, 8, 16)
MIX = 512
GROUP = 128
HALO = 32
SEQ_TILE = 256
VMEM_LIMIT_BYTES = 48 * 1024 * 1024
MESH_AXES = ("x", "y", "c")
ANY = pl.BlockSpec(memory_space=pl.ANY)

BIG = (("w_in", 2, 1), ("w_branch", 3, 1), ("w_out", 1, 2), ("w_up", 2, 1), ("w_down", 1, 2),
       ("w_ple", 2, 1), ("w_ple_gate", 1, 2))
SMALL_REP = ("norm_mix", "conf_dw_b", "conf_ln_g", "conf_ln_b", "pool_w", "pool_scale", "gmlp_ln_g",
             "gmlp_ln_b", "gmlp_ws", "gmlp_bs", "norm_mlp", "norm_ple", "norm_final")
SMALL_SHARDED = ("conf_dw", "sc_conv")
BY_CHIP = ("w_in", "w_up")
BIG_AXES = {name: (shard_axis, half_axis) for name, shard_axis, half_axis in BIG}
WEIGHT_GROUPS = (("w_in",), ("w_branch", "w_out", "w_up", "w_down", "w_ple_gate", "w_ple"))
GRAD_GROUPS = (("w_ple", "w_ple_gate", "w_down", "w_up"), ("w_out", "w_branch", "w_in"))
PACK_LANES = 128
PACK_ROW_MULTIPLE = 256


def _params(*semantics):
    return pltpu.CompilerParams(dimension_semantics=semantics, vmem_limit_bytes=VMEM_LIMIT_BYTES)


def _accumulate(ref, value, first):
    @pl.when(first)
    def _():
        ref[...] = value.astype(ref.dtype)

    @pl.when(jnp.logical_not(first))
    def _():
        ref[...] += value.astype(ref.dtype)


def _rowwise(name, fn, rows, params, outs, accs=(), tm=256):
    n_rows = rows[0][0].shape[-2]
    tm = min(tm, n_rows)
    n_in, n_out = len(rows) + len(params), len(outs)

    def body(*refs):
        res = fn(*[r[...] for r in refs[:n_in]])
        res = res if isinstance(res, (tuple, list)) else (res,)
        out_refs = refs[n_in:]
        for r, v in zip(out_refs[:n_out], res[:n_out]):
            r[...] = v.astype(r.dtype)
        first = pl.program_id(0) == 0
        for r, v in zip(out_refs[n_out:], res[n_out:]):
            _accumulate(r, v, first)

    in_specs = []
    for row in rows:
        arr, w, cb = row[:3]
        if len(row) == 4:
            in_specs.append(pl.BlockSpec((None, tm, w), lambda i, cb=cb, g=row[3]: (g, i, cb)))
        else:
            in_specs.append(pl.BlockSpec((tm, w), lambda i, cb=cb: (i, cb)))
    for p in params:
        in_specs.append(pl.BlockSpec(p.shape, lambda i, nd=p.ndim: (0,) * nd))
    out_specs = [pl.BlockSpec((tm, w), lambda i: (i, 0)) for w, _ in outs]
    out_specs += [pl.BlockSpec(s, lambda i, nd=len(s): (0,) * nd) for s in accs]
    out_shape = [jax.ShapeDtypeStruct((n_rows, w), d) for w, d in outs]
    out_shape += [jax.ShapeDtypeStruct(s, F32) for s in accs]
    return pl.pallas_call(
        body, name=name, grid=(n_rows // tm,), in_specs=in_specs, out_specs=out_specs, out_shape=out_shape,
        compiler_params=_params("arbitrary"),
    )(*[r[0] for r in rows], *params)


def _mm(name, a, b, *, ta=False, tb=False, b_by_chip=False, res=(), epi=None, out_dtypes=(F32,), tm=1024, tn=1024, tk=1024):
    g_, m_, k_ = (a.shape[0], a.shape[2], a.shape[1]) if ta else a.shape
    if b_by_chip:
        n_ = b.shape[2] if tb else 4 * b.shape[3]
    else:
        n_ = b.shape[1] if tb else b.shape[2]
    tm, tn, tk = min(tm, m_), min(tn, n_), min(tk, k_)
    if b_by_chip:
        tn, tk = (tn, min(tk, b.shape[3])) if tb else (min(tn, b.shape[3]), tk)
        per_chip = b.shape[3] // (tk if tb else tn)
    nk, n_res, n_out = k_ // tk, len(res), len(out_dtypes)
    dims = (((0 if ta else 1,), (1 if tb else 0,)), ((), ()))
    if epi is None:
        epi = lambda acc: (acc,)

    def body(*refs):
        a_ref, b_ref = refs[:2]
        res_refs, out_refs, acc_ref = refs[2:2 + n_res], refs[2 + n_res:2 + n_res + n_out], refs[-1]
        k = pl.program_id(3)

        @pl.when(k == 0)
        def _():
            acc_ref[...] = jnp.zeros_like(acc_ref)

        acc_ref[...] += lax.dot_general(a_ref[...].astype(BF16), b_ref[...].astype(BF16), dims,
                                        preferred_element_type=F32)

        @pl.when(k == nk - 1)
        def _():
            vals = epi(acc_ref[...], *[r[...] for r in res_refs])
            for o, v in zip(out_refs, vals):
                o[...] = v.astype(o.dtype)

    a_spec = (pl.BlockSpec((None, tk, tm), lambda g, i, j, k: (g, k, i)) if ta
              else pl.BlockSpec((None, tm, tk), lambda g, i, j, k: (g, i, k)))
    if b_by_chip and tb:
        b_spec = pl.BlockSpec((None, None, tn, tk), lambda g, i, j, k: (g, k // per_chip, j, k % per_chip))
    elif b_by_chip:
        b_spec = pl.BlockSpec((None, None, tk, tn), lambda g, i, j, k: (g, j // per_chip, k, j % per_chip))
    elif tb:
        b_spec = pl.BlockSpec((None, tn, tk), lambda g, i, j, k: (g, j, k))
    else:
        b_spec = pl.BlockSpec((None, tk, tn), lambda g, i, j, k: (g, k, j))
    res_specs = [pl.BlockSpec((None, tm, tn), lambda g, i, j, k, off=off: (g, i, j + off)) for _, off in res]
    out_spec = pl.BlockSpec((None, tm, tn), lambda g, i, j, k: (g, i, j))
    outs = pl.pallas_call(
        body, name=name, grid=(g_, m_ // tm, n_ // tn, nk),
        in_specs=[a_spec, b_spec, *res_specs], out_specs=[out_spec] * n_out,
        out_shape=[jax.ShapeDtypeStruct((g_, m_, n_), d) for d in out_dtypes],
        scratch_shapes=[pltpu.VMEM((tm, tn), F32)],
        compiler_params=_params("parallel", "parallel", "parallel", "arbitrary"),
    )(a, b, *[r for r, _ in res])
    return outs


def _mm2(name, a, b, *, res=(), **kw):
    outs = _mm(name, a[None], b[None], res=[(r[None], off) for r, off in res], **kw)
    return [o[0] for o in outs]


@jax.custom_vjp
def _bdot(a, b):
    return jnp.dot(a.astype(BF16), b.astype(BF16), preferred_element_type=F32)


def _bdot_fwd(a, b):
    return _bdot(a, b), (a, b)


def _bdot_bwd(saved, g):
    a, b = saved
    gb = g.astype(BF16)
    da = lax.dot_general(gb, b.astype(BF16), (((1,), (1,)), ((), ())), preferred_element_type=F32)
    db = lax.dot_general(a.astype(BF16), gb, (((0,), (0,)), ((), ())), preferred_element_type=F32)
    return da.astype(a.dtype), db.astype(b.dtype)


_bdot.defvjp(_bdot_fwd, _bdot_bwd)


def _rms(x, g):
    return x * lax.rsqrt(jnp.mean(x * x, axis=-1, keepdims=True) + EPS) * g


def _layer_norm(x, g, b):
    mu = jnp.mean(x, axis=-1, keepdims=True)
    var = jnp.mean(jnp.square(x - mu), axis=-1, keepdims=True)
    return (x - mu) * lax.rsqrt(var + EPS) * g + b


def _glu_and_gate_fn(a, a_gate, cg, hx):
    return a * jax.nn.sigmoid(a_gate), cg * hx


def _branches_fn(ca, pooled, cc, bg, u, v, conf_b, conf_g, conf_beta, pool_scale, gm_g, gm_b, *mats):
    pool_w, ws, bs = mats[0:4], mats[4:8], mats[8:12]
    ya = jax.nn.silu(_layer_norm(ca + conf_b, conf_g, conf_beta))
    yb = jnp.concatenate([_bdot(pooled[:, g * GROUP:(g + 1) * GROUP], pool_w[g]) for g in range(4)], axis=1) * pool_scale
    yc = bg * cc
    vln = _layer_norm(v, gm_g, gm_b)
    causal = lax.broadcasted_iota(jnp.int32, (GROUP, GROUP), 0) >= lax.broadcasted_iota(jnp.int32, (GROUP, GROUP), 1)
    sg = jnp.concatenate(
        [_bdot(jnp.where(causal, ws[g], 0.0), vln[:, g * GROUP:(g + 1) * GROUP]) + bs[g] for g in range(4)], axis=1)
    yd = u * sg
    return ya, yb, yc, yd


def _branches_bwd_fn(*args):
    ins, dys = args[:6] + args[10:], args[6:10]
    _, vjp = jax.vjp(_branches_fn, *ins)
    grads = vjp(tuple(dys))
    return grads


def _glu_bwd_fn(a, a_gate, cg, hx, dga, dchx):
    _, vjp = jax.vjp(_glu_and_gate_fn, a, a_gate, cg, hx)
    return vjp((dga, dchx))


def _rms_bwd_fn(x, dh, dres, g):
    _, vjp = jax.vjp(_rms, x, g)
    dx, dg = vjp(dh)
    return dx + dres, dg


def _loss_fn(x, target, g):
    def f(x, g):
        err = jnp.square(_rms(x, g) - target)
        return 0.5 * jnp.sum(jnp.mean(err, axis=-1, keepdims=True), axis=0, keepdims=True)

    loss, vjp = jax.vjp(f, x, g)
    dx, dg = vjp(jnp.ones((1, 1), F32))
    return dx, dg, jnp.broadcast_to(loss, (1, PACK_LANES))


def _adamw_fn(w, g, m, v):
    m = ADAM_B1 * m + (1.0 - ADAM_B1) * g
    v = ADAM_B2 * v + (1.0 - ADAM_B2) * jnp.square(g)
    m_hat = m / (1.0 - ADAM_B1 ** ADAM_STEP)
    v_hat = v / (1.0 - ADAM_B2 ** ADAM_STEP)
    delta = -ADAM_LR * (m_hat / (jnp.sqrt(v_hat) + ADAM_EPS) + ADAM_WD * w)
    return delta, m, v


def _group_pick(j):
    return [(j == g).astype(F32) for g in range(len(POOL_WINDOWS))]


def _accumulate_row(ref, row, value, first):
    @pl.when(first)
    def _():
        ref[pl.ds(row, 1), :] = value

    @pl.when(jnp.logical_not(first))
    def _():
        ref[pl.ds(row, 1), :] += value


SUBLANES = 8


def _causal_taps(x, taps):
    for r in range(min(SUBLANES, taps)):
        behind = x if r == 0 else pltpu.roll(x, r, 0)
        for q in range((taps - 1 - r) // SUBLANES + 1):
            start = HALO - SUBLANES * q
            yield SUBLANES * q + r, behind[start:start + SEQ_TILE, :]


def _rows_iota(n):
    return lax.broadcasted_iota(jnp.int32, (n, 1), 0)


def _seq_specs(n_time, causal):
    per = SEQ_TILE // HALO
    tile = pl.BlockSpec((SEQ_TILE, GROUP), lambda j, i: (i, j))
    if causal:
        halo = pl.BlockSpec((HALO, GROUP), lambda j, i: (jnp.maximum(i * per - 1, 0), j))
    else:
        halo = pl.BlockSpec((HALO, GROUP), lambda j, i: (jnp.minimum((i + 1) * per, n_time * per - 1), j))
    return tile, halo


def _seq_fwd(ga, proj, chx, conf_dw, sc_conv):
    s_len = ga.shape[0]
    n_time = s_len // SEQ_TILE
    ext = HALO + SEQ_TILE

    def body(ga_ref, ga_h, pin_ref, pin_h, chx_ref, chx_h, wc_ref, ws_ref, ca_ref, po_ref, cc_ref):
        j, i = pl.program_id(0), pl.program_id(1)
        keep = (i > 0).astype(F32)

        def extended(cur, halo):
            return jnp.concatenate([halo[...] * keep, cur[...]], axis=0)

        def conv(x, w_ref, taps):
            acc = None
            for d, rows in _causal_taps(x, taps):
                term = rows * w_ref[pl.ds(taps - 1 - d, 1), :]
                acc = term if acc is None else acc + term
            return acc

        ca_ref[...] = conv(extended(ga_ref, ga_h), wc_ref, CONF_KERNEL)
        cc_ref[...] = conv(extended(chx_ref, chx_h), ws_ref, SC_KERNEL)
        u = extended(pin_ref, pin_h)
        s1 = u + pltpu.roll(u, 1, 0)
        s2 = s1 + pltpu.roll(s1, 2, 0)
        s3 = s2 + pltpu.roll(s2, 4, 0)
        s4 = s3 + pltpu.roll(s3, 8, 0)
        pick = _group_pick(j)
        window_sum = s1 * pick[0] + s2 * pick[1] + s3 * pick[2] + s4 * pick[3]
        window = sum(float(w) * f for w, f in zip(POOL_WINDOWS, pick))
        pos = (_rows_iota(ext) + (i * SEQ_TILE - HALO + 1)).astype(F32)
        count = jnp.maximum(jnp.minimum(pos, window), 1.0)
        po_ref[...] = (window_sum / count - u)[HALO:, :]

    tile, halo = _seq_specs(n_time, causal=True)
    pin_tile = pl.BlockSpec((SEQ_TILE, GROUP), lambda j, i: (i, 2 * (MIX // GROUP) + j))
    pin_halo = pl.BlockSpec((HALO, GROUP), lambda j, i: (jnp.maximum(i * (SEQ_TILE // HALO) - 1, 0), 2 * (MIX // GROUP) + j))
    w_spec = lambda taps: pl.BlockSpec((taps, GROUP), lambda j, i: (0, j))
    return pl.pallas_call(
        body, name="seq_fwd", grid=(MIX // GROUP, n_time),
        in_specs=[tile, halo, pin_tile, pin_halo, tile, halo, w_spec(CONF_KERNEL), w_spec(SC_KERNEL)],
        out_specs=[tile] * 3, out_shape=[jax.ShapeDtypeStruct((s_len, MIX), F32)] * 3,
        compiler_params=_params("parallel", "arbitrary"),
    )(ga, ga, proj, proj, chx, chx, conf_dw, sc_conv)


def _seq_bwd(dca, dpooled, dcc, ga, chx, conf_dw, sc_conv):
    s_len = ga.shape[0]
    n_time = s_len // SEQ_TILE
    ext = HALO + SEQ_TILE

    def body(dca_ref, dca_h, dpo_ref, dpo_h, dcc_ref, dcc_h, ga_ref, ga_h, chx_ref, chx_h, wc_ref, ws_ref,
             dga_ref, dpin_ref, dchx_ref, dwc_ref, dws_ref):
        j, i = pl.program_id(0), pl.program_id(1)
        keep_prev = (i > 0).astype(F32)
        keep_next = (i < n_time - 1).astype(F32)

        def with_next(cur, halo):
            return jnp.concatenate([cur[...], halo[...] * keep_next], axis=0)

        def with_prev(cur, halo):
            return jnp.concatenate([halo[...] * keep_prev, cur[...]], axis=0)

        def conv_t(dy, w_ref, taps):
            acc = None
            for r in range(min(SUBLANES, taps)):
                ahead = dy if r == 0 else pltpu.roll(dy, ext - r, 0)
                for q in range((taps - 1 - r) // SUBLANES + 1):
                    term = ahead[SUBLANES * q:SUBLANES * q + SEQ_TILE, :] * w_ref[pl.ds(taps - 1 - (SUBLANES * q + r), 1), :]
                    acc = term if acc is None else acc + term
            return acc

        def tap_grads(dy_cur, x, dw_ref, taps):
            for d, rows in _causal_taps(x, taps):
                part = jnp.sum(dy_cur * rows, axis=0, keepdims=True)
                _accumulate_row(dw_ref, taps - 1 - d, part, i == 0)

        dga_ref[...] = conv_t(with_next(dca_ref, dca_h), wc_ref, CONF_KERNEL)
        dchx_ref[...] = conv_t(with_next(dcc_ref, dcc_h), ws_ref, SC_KERNEL)
        tap_grads(dca_ref[...], with_prev(ga_ref, ga_h), dwc_ref, CONF_KERNEL)
        tap_grads(dcc_ref[...], with_prev(chx_ref, chx_h), dws_ref, SC_KERNEL)

        dpo = with_next(dpo_ref, dpo_h)
        pick = _group_pick(j)
        window = sum(float(w) * f for w, f in zip(POOL_WINDOWS, pick))
        pos = (_rows_iota(ext) + (i * SEQ_TILE + 1)).astype(F32)
        r = dpo / jnp.minimum(pos, window)
        q1 = r + pltpu.roll(r, ext - 1, 0)
        q2 = q1 + pltpu.roll(q1, ext - 2, 0)
        q3 = q2 + pltpu.roll(q2, ext - 4, 0)
        q4 = q3 + pltpu.roll(q3, ext - 8, 0)
        q = q1 * pick[0] + q2 * pick[1] + q3 * pick[2] + q4 * pick[3]
        dpin_ref[...] = (q - dpo)[:SEQ_TILE, :].astype(dpin_ref.dtype)

    tile, nxt = _seq_specs(n_time, causal=False)
    _, prv = _seq_specs(n_time, causal=True)
    w_spec = lambda taps: pl.BlockSpec((taps, GROUP), lambda j, i: (0, j))
    return pl.pallas_call(
        body, name="seq_bwd", grid=(MIX // GROUP, n_time),
        in_specs=[tile, nxt, tile, nxt, tile, nxt, tile, prv, tile, prv, w_spec(CONF_KERNEL), w_spec(SC_KERNEL)],
        out_specs=[tile, tile, tile, w_spec(CONF_KERNEL), w_spec(SC_KERNEL)],
        out_shape=[jax.ShapeDtypeStruct((s_len, MIX), F32), jax.ShapeDtypeStruct((s_len, MIX), BF16),
                   jax.ShapeDtypeStruct((s_len, MIX), F32), jax.ShapeDtypeStruct(conf_dw.shape, F32),
                   jax.ShapeDtypeStruct(sc_conv.shape, F32)],
        compiler_params=_params("parallel", "arbitrary"),
    )(dca, dca, dpooled, dpooled, dcc, dcc, ga, ga, chx, chx, conf_dw, sc_conv)


MERGE_TILE = 512


def _merge_fwd(y, w_branch, proj):
    s_len, d_model = y.shape[1], w_branch.shape[2]
    t = min(MERGE_TILE, s_len)
    gate_block0 = (proj.shape[1] - 4 * d_model) // MERGE_TILE
    per_branch = d_model // MERGE_TILE

    def body(y_ref, w_ref, gate_ref, out_ref, acc_ref):
        k = pl.program_id(2)

        @pl.when(k == 0)
        def _():
            acc_ref[...] = jnp.zeros_like(acc_ref)

        z = jnp.dot(y_ref[...], w_ref[...], preferred_element_type=F32)
        acc_ref[...] += jax.nn.sigmoid(gate_ref[...]) * z

        @pl.when(k == 3)
        def _():
            out_ref[...] = acc_ref[...].astype(out_ref.dtype)

    return pl.pallas_call(
        body, name="merge_fwd", grid=(s_len // t, per_branch, 4),
        in_specs=[pl.BlockSpec((None, t, MIX), lambda i, j, k: (k, i, 0)),
                  pl.BlockSpec((None, MIX, MERGE_TILE), lambda i, j, k: (k, 0, j)),
                  pl.BlockSpec((t, MERGE_TILE), lambda i, j, k: (i, gate_block0 + k * per_branch + j))],
        out_specs=pl.BlockSpec((t, MERGE_TILE), lambda i, j, k: (i, j)),
        out_shape=jax.ShapeDtypeStruct((s_len, d_model), BF16),
        scratch_shapes=[pltpu.VMEM((t, MERGE_TILE), F32)],
        compiler_params=_params("parallel", "parallel", "arbitrary"),
    )(y, w_branch, proj)


def _merge_bwd(y, w_branch, proj, dmerged):
    s_len, d_model = y.shape[1], w_branch.shape[2]
    t = min(MERGE_TILE, s_len)
    gate_block0 = (proj.shape[1] - 4 * d_model) // MERGE_TILE
    per_branch = d_model // MERGE_TILE

    def body(y_ref, w_ref, gate_ref, dm_ref, dz_ref, dgate_ref):
        z = jnp.dot(y_ref[...], w_ref[...], preferred_element_type=F32)
        gate = jax.nn.sigmoid(gate_ref[...])
        dm = dm_ref[...]
        dz_ref[...] = (dm * gate).astype(dz_ref.dtype)
        dgate_ref[...] = (dm * z * gate * (1.0 - gate)).astype(dgate_ref.dtype)

    return pl.pallas_call(
        body, name="merge_bwd", grid=(s_len // t, per_branch, 4),
        in_specs=[pl.BlockSpec((None, t, MIX), lambda i, j, k: (k, i, 0)),
                  pl.BlockSpec((None, MIX, MERGE_TILE), lambda i, j, k: (k, 0, j)),
                  pl.BlockSpec((t, MERGE_TILE), lambda i, j, k: (i, gate_block0 + k * per_branch + j)),
                  pl.BlockSpec((t, MERGE_TILE), lambda i, j, k: (i, j))],
        out_specs=[pl.BlockSpec((None, t, MERGE_TILE), lambda i, j, k: (k, i, j)),
                   pl.BlockSpec((t, MERGE_TILE), lambda i, j, k: (i, k * per_branch + j))],
        out_shape=[jax.ShapeDtypeStruct((4, s_len, d_model), BF16), jax.ShapeDtypeStruct((s_len, 4 * d_model), BF16)],
        compiler_params=_params("parallel", "parallel", "arbitrary"),
    )(y, w_branch, proj, dmerged)


def _row(v):
    return v.reshape(1, -1)


def _branch_params(w):
    mats = [w["pool_w"][g] for g in range(4)] + [w["gmlp_ws"][g] for g in range(4)]
    mats += [w["gmlp_bs"][g].reshape(GROUP, 1) for g in range(4)]
    return [_row(w["conf_dw_b"]), _row(w["conf_ln_g"]), _row(w["conf_ln_b"]), _row(w["pool_scale"]),
            _row(w["gmlp_ln_g"]), _row(w["gmlp_ln_b"])] + mats


def _layer_fwd(x0, p_l, w, fetch):
    d_model = x0.shape[1]
    (h1,) = _rowwise("rms_mix", _rms, [(x0, d_model, 0)], [_row(w["norm_mix"])], [(d_model, BF16)])
    w.update(fetch(0, h1))
    (proj,) = _mm2("proj", h1, w["w_in"], b_by_chip=True)
    w.update(fetch(1, proj))
    ga, chx = _rowwise("glu", _glu_and_gate_fn, [(proj, MIX, 0), (proj, MIX, 1), (proj, MIX, 4), (proj, MIX, 5)], [],
                       [(MIX, F32), (MIX, F32)])
    ca, pooled, cc = _seq_fwd(ga, proj, chx, w["conf_dw"], w["sc_conv"])
    ys = _rowwise("branches", _branches_fn,
                  [(ca, MIX, 0), (pooled, MIX, 0), (cc, MIX, 0), (proj, MIX, 3), (proj, MIX, 6), (proj, MIX, 7)],
                  _branch_params(w), [(MIX, BF16)] * 4, tm=GROUP)
    y = jnp.stack(ys)
    merged = _merge_fwd(y, w["w_branch"], proj)
    (x1,) = _mm2("out_proj", merged, w["w_out"], res=[(x0, 0)], epi=lambda acc, r: (acc + r,))
    (h2,) = _rowwise("rms_mlp", _rms, [(x1, d_model, 0)], [_row(w["norm_mlp"])], [(d_model, BF16)])
    up, act = _mm2("mlp_up", h2, w["w_up"], b_by_chip=True, out_dtypes=(F32, BF16),
                   epi=lambda acc: (acc, jnp.square(jnp.maximum(acc, 0.0))))
    (x2,) = _mm2("mlp_down", act, w["w_down"], res=[(x1, 0)], epi=lambda acc, r: (acc + r,))
    (h3,) = _rowwise("rms_ple", _rms, [(x2, d_model, 0)], [_row(w["norm_ple"])], [(d_model, BF16)])
    (pg,) = _mm2("ple_gate", h3, w["w_ple_gate"])
    (x3,) = _mm2("ple", p_l, w["w_ple"], res=[(pg, 0), (x2, 0)],
                 epi=lambda acc, g, r: (acc * jax.nn.sigmoid(g) + r,))
    saved = dict(x0=x0, h1=h1, proj=proj, ga=ga, chx=chx, ca=ca, pooled=pooled, cc=cc, y=y, merged=merged, x1=x1,
                 h2=h2, up=up, act=act, x2=x2, h3=h3, pg=pg)
    return x3, saved


def _layer_bwd(dx3, p_l, w, s, send):
    d_model = dx3.shape[1]
    big, small = {}, {}

    def ple_epi(acc, dx, g):
        sig = jax.nn.sigmoid(g)
        return dx * sig, dx * acc * sig * (1.0 - sig)

    dpp, dpg = _mm2("ple_bwd", p_l, w["w_ple"], res=[(dx3, 0), (s["pg"], 0)], epi=ple_epi, out_dtypes=(BF16, BF16))
    (big["w_ple"],) = _mm2("dw_ple", p_l, dpp, ta=True, out_dtypes=(BF16,))
    (big["w_ple_gate"],) = _mm2("dw_ple_gate", s["h3"], dpg, ta=True, out_dtypes=(BF16,))
    (dh3,) = _mm2("dh_ple", dpg, w["w_ple_gate"], tb=True)
    dx2, small["norm_ple"] = _rowwise("rms_ple_bwd", _rms_bwd_fn, [(s["x2"], d_model, 0), (dh3, d_model, 0), (dx3, d_model, 0)],
                                      [_row(w["norm_ple"])], [(d_model, F32)], [(1, d_model)])

    (dup,) = _mm2("mlp_down_bwd", dx2, w["w_down"], tb=True, res=[(s["up"], 0)], out_dtypes=(BF16,),
                  epi=lambda acc, up: (acc * 2.0 * jnp.maximum(up, 0.0),))
    (big["w_down"],) = _mm2("dw_down", s["act"], dx2, ta=True, out_dtypes=(BF16,))
    (big["w_up"],) = _mm2("dw_up", s["h2"], dup, ta=True, out_dtypes=(BF16,))
    (dh2,) = _mm2("mlp_up_bwd", dup, w["w_up"], tb=True, b_by_chip=True)
    sent = send(0, [big[n] for n in GRAD_GROUPS[0]])
    dx1, small["norm_mlp"] = _rowwise("rms_mlp_bwd", _rms_bwd_fn, [(s["x1"], d_model, 0), (dh2, d_model, 0), (dx2, d_model, 0)],
                                      [_row(w["norm_mlp"]) + sent], [(d_model, F32)], [(1, d_model)])

    (dmerged,) = _mm2("out_proj_bwd", dx1, w["w_out"], tb=True)
    (big["w_out"],) = _mm2("dw_out", s["merged"], dx1, ta=True, out_dtypes=(BF16,))
    dz, dgate = _merge_bwd(s["y"], w["w_branch"], s["proj"], dmerged)
    (dy,) = _mm("branch_bwd", dz, w["w_branch"], tb=True)
    (big["w_branch"],) = _mm("dw_branch", s["y"], dz, ta=True, out_dtypes=(BF16,))

    proj = s["proj"]
    params = _branch_params(w)
    grads = _rowwise(
        "branches_bwd", _branches_bwd_fn,
        [(s["ca"], MIX, 0), (s["pooled"], MIX, 0), (s["cc"], MIX, 0), (proj, MIX, 3), (proj, MIX, 6), (proj, MIX, 7)]
        + [(dy, MIX, 0, g) for g in range(4)], params,
        [(MIX, F32), (MIX, F32), (MIX, F32), (MIX, BF16), (MIX, BF16), (MIX, BF16)], [p.shape for p in params], tm=GROUP)
    dca, dpooled, dcc, dbg, du, dv = grads[:6]
    pg_ = grads[6:]
    small["conf_dw_b"], small["conf_ln_g"], small["conf_ln_b"], small["pool_scale"] = pg_[0], pg_[1], pg_[2], pg_[3]
    small["gmlp_ln_g"], small["gmlp_ln_b"] = pg_[4], pg_[5]
    small["pool_w"] = jnp.stack(pg_[6:10])
    small["gmlp_ws"] = jnp.stack(pg_[10:14])
    small["gmlp_bs"] = jnp.stack([b.reshape(GROUP) for b in pg_[14:18]])

    dga, dpin, dchx, small["conf_dw"], small["sc_conv"] = _seq_bwd(dca, dpooled, dcc, s["ga"], s["chx"], w["conf_dw"], w["sc_conv"])
    da, dag, dcg, dhx = _rowwise("glu_bwd", _glu_bwd_fn,
                                 [(proj, MIX, 0), (proj, MIX, 1), (proj, MIX, 4), (proj, MIX, 5), (dga, MIX, 0), (dchx, MIX, 0)],
                                 [], [(MIX, BF16)] * 4)
    dproj = jnp.concatenate([da, dag, dpin, dbg, dcg, dhx, du, dv, dgate], axis=1)
    (big["w_in"],) = _mm2("dw_in", s["h1"], dproj, ta=True, out_dtypes=(BF16,))
    (dh1,) = _mm2("proj_bwd", dproj, w["w_in"], tb=True, b_by_chip=True)
    sent = send(1, [big[n] for n in GRAD_GROUPS[1]])
    dx0, small["norm_mix"] = _rowwise("rms_mix_bwd", _rms_bwd_fn, [(s["x0"], d_model, 0), (dh1, d_model, 0), (dx1, d_model, 0)],
                                      [_row(w["norm_mix"]) + sent], [(d_model, F32)], [(1, d_model)])
    return dx0, small


def _position():
    return lax.axis_index("x"), lax.axis_index("y"), lax.axis_index("c")


def _other_chips(x, y):
    return [(1 - x, y), (x, 1 - y), (1 - x, 1 - y)]


def _region(ref, cuts):
    return ref.at[tuple(pl.ds(*cuts[a]) if a in cuts else slice(None) for a in range(len(ref.shape)))]


def _allgather_small(name, block):
    m_per, n = block.shape

    def body(x_ref, out_ref, send_sems, recv_sems, local_sem):
        x, y, c = _position()
        me, sibling = (x, y, c), (x, y, 1 - c)
        chips = _other_chips(x, y)

        def rows(px, py, pc):
            return out_ref.at[pl.ds((4 * px + 2 * py + pc) * m_per, m_per), :]

        def copy(k, blk, to, src=None):
            return pltpu.make_async_remote_copy(
                src_ref=rows(*blk) if src is None else src, dst_ref=rows(*blk), send_sem=send_sems.at[k],
                recv_sem=recv_sems.at[k], device_id=to, device_id_type=pl.DeviceIdType.MESH)

        mine = pltpu.make_async_copy(x_ref, rows(*me), local_sem)
        mine.start()
        first = [copy(0, me, sibling, src=x_ref)]
        first += [copy(1 + j, me, (*chip, c), src=x_ref) for j, chip in enumerate(chips)]
        for cp in first:
            cp.start()
        passed = [copy(4 + j, (*chip, c), sibling) for j, chip in enumerate(chips)]
        for j, chip in enumerate(chips):
            copy(1 + j, (*chip, c), me).wait_recv()
            passed[j].start()
        copy(0, sibling, me).wait_recv()
        for j, chip in enumerate(chips):
            copy(4 + j, (*chip, 1 - c), me).wait_recv()
        for cp in first + passed:
            cp.wait_send()
        mine.wait()

    return pl.pallas_call(
        body, name=name, out_shape=jax.ShapeDtypeStruct((8 * m_per, n), block.dtype), in_specs=[ANY], out_specs=ANY,
        scratch_shapes=[pltpu.SemaphoreType.DMA((7,)), pltpu.SemaphoreType.DMA((7,)), pltpu.SemaphoreType.DMA],
    )(block)


def _cuts(arr_shape, shard_axis, half_axis, shard=None, half=None):
    cuts = {}
    if shard is not None:
        n = arr_shape[shard_axis] // 4
        cuts[shard_axis] = (shard * n, n)
    if half is not None:
        n = arr_shape[half_axis] // 2
        cuts[half_axis] = (half * n, n)
    return cuts


def _full_shape(shard_shape, shard_axis):
    return tuple(4 * n if a == shard_axis else n for a, n in enumerate(shard_shape))


HBM = pl.BlockSpec(memory_space=pltpu.HBM)
SEM = pl.BlockSpec(memory_space=pltpu.SEMAPHORE)
DATAFLOW_EFFECT = pltpu.SideEffectType.DATAFLOW_SIDE_EFFECTING
TOKEN_SHAPE = (8, 128)


def _remote(src, dst, send_sems, recv_sems, k, peer):
    return pltpu.make_async_remote_copy(src_ref=src, dst_ref=dst, send_sem=send_sems.at[k], recv_sem=recv_sems.at[k],
                                        device_id=peer, device_id_type=pl.DeviceIdType.MESH)


def _split_start(name, sources, landings, plan):
    n_s, n_l = len(sources), len(landings)
    n_copies = len(plan([None] * n_s, [None] * n_l, None))

    def body(*refs):
        src, land = refs[:n_s], refs[n_s:n_s + n_l]
        send_sems, recv_sems, token = refs[n_s + n_l], refs[n_s + n_l + 1], refs[-1]
        for k, (s, d, peer) in enumerate(plan(src, land, _position())):
            _remote(s, d, send_sems, recv_sems, k, peer).start()
        token[...] = jnp.zeros_like(token)

    arrays = [pltpu.with_memory_space_constraint(a, pltpu.HBM) for a in (*sources, *landings)]
    outs = pl.pallas_call(
        body, name=name,
        out_shape=(pltpu.SemaphoreType.DMA((n_copies,)), pltpu.SemaphoreType.DMA((n_copies,)),
                   *[pltpu.HBM(a.shape, a.dtype) for a in arrays], jax.ShapeDtypeStruct(TOKEN_SHAPE, F32)),
        in_specs=[HBM] * (n_s + n_l),
        out_specs=(SEM, SEM, *[HBM] * (n_s + n_l), pl.BlockSpec(memory_space=pltpu.VMEM)),
        input_output_aliases={i: 2 + i for i in range(n_s + n_l)},
        compiler_params=pltpu.CompilerParams(has_side_effects=DATAFLOW_EFFECT),
    )(*arrays)
    return outs[0], outs[1], list(outs[2:2 + n_s]), list(outs[2 + n_s:2 + n_s + n_l]), outs[-1]


def _split_wait(name, started, after, plan):
    send_sems, recv_sems, sources, landings, _ = started
    n_s, n_l = len(sources), len(landings)

    def body(*refs):
        src, land = refs[:n_s], refs[n_s:n_s + n_l]
        send_sems_ref, recv_sems_ref = refs[n_s + n_l], refs[n_s + n_l + 1]
        for k, (s, d, peer) in enumerate(plan(src, land, _position())):
            cp = _remote(s, d, send_sems_ref, recv_sems_ref, k, peer)
            cp.wait_send()
            cp.wait_recv()

    outs = pl.pallas_call(
        body, name=name, out_shape=tuple(pltpu.HBM(a.shape, a.dtype) for a in (*sources, *landings)),
        in_specs=[*[HBM] * (n_s + n_l), SEM, SEM, ANY], out_specs=tuple([HBM] * (n_s + n_l)),
        input_output_aliases={i: i for i in range(n_s + n_l)},
        compiler_params=pltpu.CompilerParams(has_side_effects=DATAFLOW_EFFECT),
    )(*sources, *landings, send_sems, recv_sems, after)
    return list(outs[:n_s]), list(outs[n_s:])


def _exchange(name, inputs, out_shapes, plan, in_place=False):
    n_in = len(inputs)
    n_out = n_in if in_place else len(out_shapes)
    n_copies = len(plan([None] * n_in, [None] * n_out, None))

    def body(*refs):
        ins, outs = refs[:n_in], refs[n_in:n_in + n_out]
        send_sems, recv_sems = refs[n_in + n_out:]
        copies = [_remote(s, d, send_sems, recv_sems, k, peer)
                  for k, (s, d, peer) in enumerate(plan(ins, ins if in_place else outs, _position()))]
        for cp in copies:
            cp.start()
        for cp in copies:
            cp.wait()

    if in_place:
        out_shapes = [jax.ShapeDtypeStruct(a.shape, a.dtype) for a in inputs]
    return pl.pallas_call(
        body, name=name, in_specs=[ANY] * n_in, out_specs=[ANY] * n_out, out_shape=out_shapes,
        input_output_aliases={i: i for i in range(n_in)} if in_place else {},
        scratch_shapes=[pltpu.SemaphoreType.DMA((n_copies,)), pltpu.SemaphoreType.DMA((n_copies,))],
    )(*inputs)


def _half_rows(ref, half):
    n = ref.shape[0] // 2
    return ref.at[pl.ds(half * n, n)]


def _gather_plan(sources, landings, pos):
    if pos is None:
        return [None] * (3 * len(sources))
    x, y, c = pos
    copies = []
    for src, land in zip(sources, landings):
        for qx, qy in _other_chips(x, y):
            copies.append((_half_rows(src, c), _half_rows(land.at[2 * x + y], c), (qx, qy, c)))
    return copies


def _sibling_plan(landings, _, pos):
    if pos is None:
        return [None] * (3 * len(landings))
    x, y, c = pos
    copies = []
    for land in landings:
        for qx, qy in _other_chips(x, y):
            region = _half_rows(land.at[2 * qx + qy], c)
            copies.append((region, region, (x, y, 1 - c)))
    return copies


def _weights_start(tag, shards):
    landings = [lax.empty((4, *s.shape), BF16) for s in shards]
    return _split_start(f"weights_start_{tag}", shards, landings, _gather_plan)


def _weights_finish(tag, names, started, after):
    shards, landings = _split_wait(f"weights_wait_{tag}", started, after, _gather_plan)
    landings = _exchange("weights_to_sibling", landings, None, _sibling_plan, in_place=True)
    my_shard = 2 * lax.axis_index("x") + lax.axis_index("y")
    full = {}
    for name, shard, land in zip(names, shards, landings):
        sa = BIG_AXES[name][0]
        land = lax.dynamic_update_index_in_dim(land, shard, my_shard, axis=0)
        if name in BY_CHIP:
            full[name] = land
            continue
        axis = sa - 1
        moved = jnp.moveaxis(land, 0, axis)
        full[name] = moved.reshape(*moved.shape[:axis], 4 * moved.shape[axis + 1], *moved.shape[axis + 2:])
    return full


def _half_shape(shape, half_axis):
    return tuple(n // 2 if a == half_axis else n for a, n in enumerate(shape))


def _shard_shape(shape, shard_axis):
    return tuple(n // 4 if a == shard_axis else n for a, n in enumerate(shape))


def _flat2d(a):
    return a.reshape(-1, a.shape[-1])


ELEMENTWISE_TILE_ELEMENTS = 1 << 18


def _row_tile(rows, cols):
    tile = 1
    while rows % (2 * tile) == 0 and 2 * tile * cols <= ELEMENTWISE_TILE_ELEMENTS:
        tile *= 2
    assert tile >= 16 or tile == rows, (rows, cols)
    return tile


def _sum_arrays(name, arrays, out_dtype):
    shape = arrays[0].shape
    flat = [_flat2d(a) for a in arrays]
    rows, cols = flat[0].shape
    tm = _row_tile(rows, cols)

    def fn(*tiles):
        acc = tiles[0].astype(F32)
        for t in tiles[1:]:
            acc = acc + t.astype(F32)
        return acc

    (out,) = _rowwise(name, fn, [(a, cols, 0) for a in flat], [], [(cols, out_dtype)], tm=tm)
    return out.reshape(shape)


def _grads_start(tag, names, grads):
    c = lax.axis_index("c")
    shapes = [g.shape for g in grads]
    axes = [(BIG_AXES[n][0] - 1, BIG_AXES[n][1] - 1) for n in names]

    def to_sibling(ins, outs, pos):
        if pos is None:
            return [None] * len(names)
        px, py, pc = pos
        return [(_region(ins[wi], _cuts(shapes[wi], sa, ha, half=1 - pc)), outs[wi], (px, py, 1 - pc))
                for wi, (sa, ha) in enumerate(axes)]

    landed = _exchange("grads_to_sibling", grads, [jax.ShapeDtypeStruct(_half_shape(s, ha), BF16) for s, (_, ha) in zip(shapes, axes)],
                       to_sibling)
    chip_sums = []
    for wi, name in enumerate(names):
        ha = axes[wi][1]
        n = shapes[wi][ha] // 2
        mine = lax.dynamic_slice_in_dim(grads[wi], c * n, n, axis=ha)
        chip_sums.append(_sum_arrays("chip_sum_" + name, [mine, landed[wi]], BF16))

    def to_chips(sources, landings, pos):
        if pos is None:
            return [None] * (3 * len(names))
        px, py, pc = pos
        copies = []
        for wi, (sa, ha) in enumerate(axes):
            for j, (qx, qy) in enumerate(_other_chips(px, py)):
                piece = _region(sources[wi], _cuts(shapes[wi], sa, ha, shard=2 * qx + qy))
                copies.append((piece, landings[wi].at[j], (qx, qy, pc)))
        return copies

    landings = [lax.empty((3, *_shard_shape(t.shape, sa)), BF16) for t, (sa, _) in zip(chip_sums, axes)]
    return _split_start(f"grads_start_{tag}", chip_sums, landings, to_chips), to_chips


def _grads_finish(tag, names, started, after):
    chip_sums, pieces = _split_wait(f"grads_wait_{tag}", started[0], after, started[1])
    my_shard = 2 * lax.axis_index("x") + lax.axis_index("y")
    reduced = {}
    for name, chip_sum, piece in zip(names, chip_sums, pieces):
        sa = BIG_AXES[name][0] - 1
        n = chip_sum.shape[sa] // 4
        mine = lax.dynamic_slice_in_dim(chip_sum, my_shard * n, n, axis=sa)
        reduced[name] = _sum_arrays("shard_sum_" + name, [mine, piece[0], piece[1], piece[2]], F32)
    return reduced


def _join_halves(reduced):
    c = lax.axis_index("c")

    def halves_to_sibling(ins, outs, pos):
        if pos is None:
            return [None] * len(BIG)
        px, py, pc = pos
        return [(ins[wi], outs[wi], (px, py, 1 - pc)) for wi in range(len(BIG))]

    from_sibling = _exchange("grad_halves_to_sibling", reduced, [jax.ShapeDtypeStruct(r.shape, F32) for r in reduced],
                             halves_to_sibling)
    out = []
    for wi, (_, _, ha) in enumerate(BIG):
        low = jnp.where(c == 0, reduced[wi], from_sibling[wi])
        high = jnp.where(c == 0, from_sibling[wi], reduced[wi])
        out.append(jnp.concatenate([low, high], axis=ha))
    return out


def _pack(arrays):
    flat = jnp.concatenate([a.reshape(-1).astype(F32) for a in arrays])
    rows = -(-flat.shape[0] // PACK_LANES)
    rows = -(-rows // PACK_ROW_MULTIPLE) * PACK_ROW_MULTIPLE
    return jnp.pad(flat, (0, rows * PACK_LANES - flat.shape[0])).reshape(rows, PACK_LANES)


def _unpack(buf, shapes):
    flat, out, off = buf.reshape(-1), [], 0
    for s in shapes:
        n = 1
        for d in s:
            n *= d
        out.append(flat[off:off + n].reshape(s))
        off += n
    return out


def _sum8(name, gathered, rows):
    def body(in_ref, out_ref):
        acc = in_ref[0]
        for d in range(1, 8):
            acc = acc + in_ref[d]
        out_ref[...] = acc

    return pl.pallas_call(
        body, name=name, grid=(rows // PACK_ROW_MULTIPLE,),
        in_specs=[pl.BlockSpec((8, PACK_ROW_MULTIPLE, PACK_LANES), lambda i: (0, i, 0))],
        out_specs=pl.BlockSpec((PACK_ROW_MULTIPLE, PACK_LANES), lambda i: (i, 0)),
        out_shape=jax.ShapeDtypeStruct((rows, PACK_LANES), F32), compiler_params=_params("arbitrary"),
    )(gathered.reshape(8, rows, PACK_LANES))


def _adamw(name, w, g, m, v):
    shape = w.shape
    flat = [_flat2d(a) for a in (w, g, m, v)]
    rows, cols = flat[0].shape
    tm = _row_tile(rows, cols)
    outs = _rowwise(name, _adamw_fn, [(a, cols, 0) for a in flat], [], [(cols, F32)] * 3, tm=tm)
    return [o.reshape(shape) for o in outs]


WEIGHT_ORDER = ("norm_mix", "w_in", "conf_dw", "conf_dw_b", "conf_ln_g", "conf_ln_b", "pool_w", "pool_scale", "sc_conv",
                "gmlp_ln_g", "gmlp_ln_b", "gmlp_ws", "gmlp_bs", "w_branch", "w_out", "norm_mlp", "w_up", "w_down",
                "norm_ple", "w_ple", "w_ple_gate", "norm_final")


def _local_step(x, p, loss_target, weights, layer_big, full_small_sharded, on_big_grads):
    n_layers = p.shape[0]
    d_model = x.shape[1]
    layer_w, saved = [], []
    h = x
    for l in range(n_layers):
        w = {n: weights[n][l] for n in SMALL_REP if n != "norm_final"}
        w.update({n: full_small_sharded[n][l] for n in SMALL_SHARDED})
        layer_w.append(w)
        h, s = _layer_fwd(h, p[l], w, layer_big(l))
        saved.append(s)
    dx, d_final, loss = _rowwise("loss", _loss_fn, [(h, d_model, 0), (loss_target, d_model, 0)], [_row(weights["norm_final"])],
                                 [(d_model, F32)], [(1, d_model), (1, PACK_LANES)])
    small_grads = {n: [None] * n_layers for n in SMALL_REP + SMALL_SHARDED if n != "norm_final"}
    for l in reversed(range(n_layers)):
        dx, small = _layer_bwd(dx, p[l], layer_w[l], saved[l], on_big_grads(l))
        for n in small:
            small_grads[n][l] = small[n].reshape(weights[n].shape[1:]) if n in SMALL_REP else small[n]
    small_grads = {n: jnp.stack(v) for n, v in small_grads.items()}
    small_grads["norm_final"] = d_final.reshape(-1)
    return loss, dx, small_grads


def kernel(x, p, norm_mix, w_in, conf_dw, conf_dw_b, conf_ln_g, conf_ln_b, pool_w, pool_scale, sc_conv, gmlp_ln_g, gmlp_ln_b, gmlp_ws, gmlp_bs, w_branch, w_out, norm_mlp, w_up, w_down, norm_ple, w_ple, w_ple_gate, norm_final, loss_target, m_norm_mix, m_w_in, m_conf_dw, m_conf_dw_b, m_conf_ln_g, m_conf_ln_b, m_pool_w, m_pool_scale, m_sc_conv, m_gmlp_ln_g, m_gmlp_ln_b, m_gmlp_ws, m_gmlp_bs, m_w_branch, m_w_out, m_norm_mlp, m_w_up, m_w_down, m_norm_ple, m_w_ple, m_w_ple_gate, m_norm_final, v_norm_mix, v_w_in, v_conf_dw, v_conf_dw_b, v_conf_ln_g, v_conf_ln_b, v_pool_w, v_pool_scale, v_sc_conv, v_gmlp_ln_g, v_gmlp_ln_b, v_gmlp_ws, v_gmlp_bs, v_w_branch, v_w_out, v_norm_mlp, v_w_up, v_w_down, v_norm_ple, v_w_ple, v_w_ple_gate, v_norm_final):
    given = dict(locals())
    weights = {n: given[n] for n in WEIGHT_ORDER}
    mom_m = {n: given["m_" + n] for n in WEIGHT_ORDER}
    mom_v = {n: given["v_" + n] for n in WEIGHT_ORDER}
    my_shard = 2 * lax.axis_index("x") + lax.axis_index("y")

    n_layers = p.shape[0]
    shard_shapes = [weights[n].shape for n in SMALL_SHARDED]
    gathered = _allgather_small("allgather_small_weights", _pack([weights[n] for n in SMALL_SHARDED]))
    after_small = gathered[0, 0] * 0.0
    after_small = jnp.where(after_small == 0.0, after_small, 0.0)

    weight_copies = [[None] * len(WEIGHT_GROUPS) for _ in range(n_layers)]
    started = after_small
    for l in range(n_layers):
        for g, names in enumerate(WEIGHT_GROUPS):
            shards = [(weights[n][l] + started if i == 0 else weights[n][l]).astype(BF16) for i, n in enumerate(names)]
            weight_copies[l][g] = _weights_start(f"{l}_{g}", shards)
            started = weight_copies[l][g][4][0, 0]
    per_dev = gathered.reshape(8, -1, PACK_LANES)
    full_small_sharded = {}
    chip_parts = [_unpack(per_dev[2 * s], shard_shapes) for s in range(4)]
    for i, n in enumerate(SMALL_SHARDED):
        full_small_sharded[n] = jnp.concatenate([chip_parts[s][i] for s in range(4)], axis=-1)

    grad_copies = [[None] * len(GRAD_GROUPS) for _ in range(n_layers)]

    def layer_big(l):
        return lambda g, after: _weights_finish(f"{l}_{g}", WEIGHT_GROUPS[g], weight_copies[l][g], after)

    def on_big_grads(l):
        def send(g, group_grads):
            grad_copies[l][g] = _grads_start(f"{l}_{g}", GRAD_GROUPS[g], group_grads)
            return grad_copies[l][g][0][4][0, 0]
        return send

    loss_row, dx, small_grads = _local_step(x[0] + started, p[:, 0], loss_target[0], weights, layer_big, full_small_sharded,
                                            on_big_grads)
    loss = lax.psum(loss_row[0, 0], MESH_AXES)

    halves = [{} for _ in range(n_layers)]
    for l in reversed(range(n_layers)):
        for g, names in enumerate(GRAD_GROUPS):
            halves[l].update(_grads_finish(f"{l}_{g}", names, grad_copies[l][g], dx))
    big_reduced = _join_halves([jnp.stack([halves[l][n] for l in range(n_layers)]) for n, _, _ in BIG])
    grads = {n: g for (n, _, _), g in zip(BIG, big_reduced)}

    small_names = SMALL_REP + SMALL_SHARDED
    packed = _pack([small_grads[n] for n in small_names])
    rows = packed.shape[0]
    summed = _sum8("sum_small_grads", _allgather_small("allgather_small_grads", packed), rows)
    small_full_shapes = [weights[n].shape for n in SMALL_REP] + [full_small_sharded[n].shape for n in SMALL_SHARDED]
    for n, g in zip(small_names, _unpack(summed, small_full_shapes)):
        if n in SMALL_SHARDED:
            width = weights[n].shape[-1]
            g = lax.dynamic_slice_in_dim(g, my_shard * width, width, axis=g.ndim - 1)
        grads[n] = g

    delta, new_m, new_v = {}, {}, {}
    for n, _, _ in BIG:
        delta[n], new_m[n], new_v[n] = _adamw("adamw_" + n, weights[n], grads[n], mom_m[n], mom_v[n])
    for group, names in (("rep", SMALL_REP), ("sharded", SMALL_SHARDED)):
        outs = _adamw("adamw_small_" + group, *[_pack([src[n] for n in names]) for src in (weights, grads, mom_m, mom_v)])
        shapes = [weights[n].shape for n in names]
        for dst, buf in zip((delta, new_m, new_v), outs):
            for n, a in zip(names, _unpack(buf, shapes)):
                dst[n] = a

    return (loss, dx[None], *[grads[n] for n in WEIGHT_ORDER], *[delta[n] for n in WEIGHT_ORDER],
            *[new_m[n] for n in WEIGHT_ORDER], *[new_v[n] for n in WEIGHT_ORDER])
```

```python
import functools

import jax
import jax.numpy as jnp
from jax import lax
from jax.experimental import pallas as pl
from jax.experimental.pallas import tpu as pltpu

F32, BF16 = jnp.float32, jnp.bfloat16
EPS = 1e-6
ADAM_LR, ADAM_B1, ADAM_B2, ADAM_EPS, ADAM_WD, ADAM_STEP = 0.001, 0.9, 0.999, 1e-08, 0.01, 10
CONF_KERNEL, SC_KERNEL = 31, 3
POOL_WINDOWS = (2, 4, 8, 16)
MIX = 512
GROUP = 128
HALO = 32
SEQ_TILE = 256
VMEM_LIMIT_BYTES = 48 * 1024 * 1024
MESH_AXES = ("x", "y", "c")
ANY = pl.BlockSpec(memory_space=pl.ANY)

BIG = (("w_in", 2, 1), ("w_branch", 3, 1), ("w_out", 1, 2), ("w_up", 2, 1), ("w_down", 1, 2),
       ("w_ple", 2, 1), ("w_ple_gate", 1, 2))
SMALL_REP = ("norm_mix", "conf_dw_b", "conf_ln_g", "conf_ln_b", "pool_w", "pool_scale", "gmlp_ln_g",
             "gmlp_ln_b", "gmlp_ws", "gmlp_bs", "norm_mlp", "norm_ple", "norm_final")
SMALL_SHARDED = ("conf_dw", "sc_conv")
BY_CHIP = ("w_in", "w_up")
BIG_AXES = {name: (shard_axis, half_axis) for name, shard_axis, half_axis in BIG}
WEIGHT_GROUPS = (("w_in",), ("w_branch", "w_out", "w_up", "w_down", "w_ple_gate", "w_ple"))
GRAD_GROUPS = (("w_ple", "w_ple_gate", "w_down", "w_up"), ("w_out", "w_branch", "w_in"))
PACK_LANES = 128
PACK_ROW_MULTIPLE = 256


def _params(*semantics):
    return pltpu.CompilerParams(dimension_semantics=semantics, vmem_limit_bytes=VMEM_LIMIT_BYTES)


def _accumulate(ref, value, first):
    @pl.when(first)
    def _():
        ref[...] = value.astype(ref.dtype)

    @pl.when(jnp.logical_not(first))
    def _():
        ref[...] += value.astype(ref.dtype)


def _rowwise(name, fn, rows, params, outs, accs=(), tm=256):
    n_rows = rows[0][0].shape[-2]
    tm = min(tm, n_rows)
    n_in, n_out = len(rows) + len(params), len(outs)

    def body(*refs):
        res = fn(*[r[...] for r in refs[:n_in]])
        res = res if isinstance(res, (tuple, list)) else (res,)
        out_refs = refs[n_in:]
        for r, v in zip(out_refs[:n_out], res[:n_out]):
            r[...] = v.astype(r.dtype)
        first = pl.program_id(0) == 0
        for r, v in zip(out_refs[n_out:], res[n_out:]):
            _accumulate(r, v, first)

    in_specs = []
    for row in rows:
        arr, w, cb = row[:3]
        if len(row) == 4:
            in_specs.append(pl.BlockSpec((None, tm, w), lambda i, cb=cb, g=row[3]: (g, i, cb)))
        else:
            in_specs.append(pl.BlockSpec((tm, w), lambda i, cb=cb: (i, cb)))
    for p in params:
        in_specs.append(pl.BlockSpec(p.shape, lambda i, nd=p.ndim: (0,) * nd))
    out_specs = [pl.BlockSpec((tm, w), lambda i: (i, 0)) for w, _ in outs]
    out_specs += [pl.BlockSpec(s, lambda i, nd=len(s): (0,) * nd) for s in accs]
    out_shape = [jax.ShapeDtypeStruct((n_rows, w), d) for w, d in outs]
    out_shape += [jax.ShapeDtypeStruct(s, F32) for s in accs]
    return pl.pallas_call(
        body, name=name, grid=(n_rows // tm,), in_specs=in_specs, out_specs=out_specs, out_shape=out_shape,
        compiler_params=_params("arbitrary"),
    )(*[r[0] for r in rows], *params)


ROW_OUT_SUBLANES = 8


def _mm(name, a, b, *, ta=False, tb=False, b_by_chip=False, chips=(0, 4), res=(), rows=(), epi=None, out_dtypes=(F32,), n_row_outs=0,
        tm=1024, tn=1024, tk=1024):
    g_, m_, k_ = (a.shape[0], a.shape[2], a.shape[1]) if ta else a.shape
    if b_by_chip:
        n_ = b.shape[2] if tb else chips[1] * b.shape[3]
    else:
        n_ = b.shape[1] if tb else b.shape[2]
    tm, tn, tk = min(tm, m_), min(tn, n_), min(tk, k_)
    if b_by_chip:
        tn, tk = (tn, min(tk, b.shape[3])) if tb else (min(tn, b.shape[3]), tk)
        per_chip = b.shape[3] // (tk if tb else tn)
    nk, n_res, n_rows, n_out = k_ // tk, len(res), len(rows), len(out_dtypes)
    n_in = 2 + n_res + n_rows
    dims = (((0 if ta else 1,), (1 if tb else 0,)), ((), ()))
    if epi is None:
        epi = lambda acc: (acc,)

    def body(*refs):
        a_ref, b_ref = refs[:2]
        res_refs, out_refs, acc_ref = refs[2:n_in], refs[n_in:n_in + n_out], refs[-1]
        row_out_refs = refs[n_in + n_out:-1]
        k = pl.program_id(3)

        @pl.when(k == 0)
        def _():
            acc_ref[...] = jnp.zeros_like(acc_ref)

        acc_ref[...] += lax.dot_general(a_ref[...].astype(BF16), b_ref[...].astype(BF16), dims,
                                        preferred_element_type=F32)

        @pl.when(k == nk - 1)
        def _():
            vals = epi(acc_ref[...], *[r[...] for r in res_refs])
            for o, v in zip(out_refs, vals):
                o[...] = v.astype(o.dtype)
            for o, v in zip(row_out_refs, vals[n_out:]):
                o[...] = jnp.broadcast_to(v, o.shape)

    a_spec = (pl.BlockSpec((None, tk, tm), lambda g, i, j, k: (g, k, i)) if ta
              else pl.BlockSpec((None, tm, tk), lambda g, i, j, k: (g, i, k)))
    if b_by_chip and tb:
        b_spec = pl.BlockSpec((None, None, tn, tk), lambda g, i, j, k: (g, k // per_chip, j, k % per_chip))
    elif b_by_chip:
        b_spec = pl.BlockSpec((None, None, tk, tn), lambda g, i, j, k: (g, chips[0] + j // per_chip, k, j % per_chip))
    elif tb:
        b_spec = pl.BlockSpec((None, tn, tk), lambda g, i, j, k: (g, j, k))
    else:
        b_spec = pl.BlockSpec((None, tk, tn), lambda g, i, j, k: (g, k, j))
    res_specs = [pl.BlockSpec((None, tm, tn), lambda g, i, j, k, off=off: (g, i, j + off)) for _, off in res]
    row_specs = [pl.BlockSpec((None, 1, tn), lambda g, i, j, k: (g, 0, j))] * n_rows
    out_spec = pl.BlockSpec((None, tm, tn), lambda g, i, j, k: (g, i, j))
    row_out_spec = pl.BlockSpec((None, ROW_OUT_SUBLANES, tn), lambda g, i, j, k: (g, i, j))
    outs = pl.pallas_call(
        body, name=name, grid=(g_, m_ // tm, n_ // tn, nk),
        in_specs=[a_spec, b_spec, *res_specs, *row_specs], out_specs=[out_spec] * n_out + [row_out_spec] * n_row_outs,
        out_shape=[jax.ShapeDtypeStruct((g_, m_, n_), d) for d in out_dtypes]
        + [jax.ShapeDtypeStruct((g_, ROW_OUT_SUBLANES * (m_ // tm), n_), F32)] * n_row_outs,
        scratch_shapes=[pltpu.VMEM((tm, tn), F32)],
        compiler_params=_params("parallel", "parallel", "parallel", "arbitrary"),
    )(a, b, *[r for r, _ in res], *rows)
    return outs


def _mm2(name, a, b, *, res=(), rows=(), **kw):
    outs = _mm(name, a[None], b[None], res=[(r[None], off) for r, off in res], rows=[r[None] for r in rows], **kw)
    return [o[0] for o in outs]


def _sum_row_out(row_out):
    return jnp.sum(row_out[::ROW_OUT_SUBLANES], axis=0, keepdims=True)


@jax.custom_vjp
def _bdot(a, b):
    return jnp.dot(a.astype(BF16), b.astype(BF16), preferred_element_type=F32)


def _bdot_fwd(a, b):
    return _bdot(a, b), (a, b)


def _bdot_bwd(saved, g):
    a, b = saved
    gb = g.astype(BF16)
    da = lax.dot_general(gb, b.astype(BF16), (((1,), (1,)), ((), ())), preferred_element_type=F32)
    db = lax.dot_general(a.astype(BF16), gb, (((0,), (0,)), ((), ())), preferred_element_type=F32)
    return da.astype(a.dtype), db.astype(b.dtype)


_bdot.defvjp(_bdot_fwd, _bdot_bwd)


def _rms(x, g):
    return x * lax.rsqrt(jnp.mean(x * x, axis=-1, keepdims=True) + EPS) * g


def _layer_norm(x, g, b):
    mu = jnp.mean(x, axis=-1, keepdims=True)
    var = jnp.mean(jnp.square(x - mu), axis=-1, keepdims=True)
    return (x - mu) * lax.rsqrt(var + EPS) * g + b


def _glu_and_gate_fn(a, a_gate, cg, hx):
    return a * jax.nn.sigmoid(a_gate), cg * hx


def _branches_fn(ca, pooled, cc, bg, u, v, conf_b, conf_g, conf_beta, pool_scale, gm_g, gm_b, *mats):
    pool_w, ws, bs = mats[0:4], mats[4:8], mats[8:12]
    ya = jax.nn.silu(_layer_norm(ca + conf_b, conf_g, conf_beta))
    yb = jnp.concatenate([_bdot(pooled[:, g * GROUP:(g + 1) * GROUP], pool_w[g]) for g in range(4)], axis=1) * pool_scale
    yc = bg * cc
    vln = _layer_norm(v, gm_g, gm_b)
    causal = lax.broadcasted_iota(jnp.int32, (GROUP, GROUP), 0) >= lax.broadcasted_iota(jnp.int32, (GROUP, GROUP), 1)
    sg = jnp.concatenate(
        [_bdot(jnp.where(causal, ws[g], 0.0), vln[:, g * GROUP:(g + 1) * GROUP]) + bs[g] for g in range(4)], axis=1)
    yd = u * sg
    return ya, yb, yc, yd


def _branches_bwd_fn(*args):
    ins, dys = args[:6] + args[10:], args[6:10]
    _, vjp = jax.vjp(_branches_fn, *ins)
    grads = vjp(tuple(dys))
    return grads


def _glu_bwd_fn(a, a_gate, cg, hx, dga, dchx):
    _, vjp = jax.vjp(_glu_and_gate_fn, a, a_gate, cg, hx)
    return vjp((dga, dchx))


def _rms_bwd_epilogue(dh, x, dres, g):
    _, vjp = jax.vjp(_rms, x, g)
    dx, dg = vjp(dh)
    return dx + dres, dg


def _residual_then_rms(acc, residual, g):
    x = acc + residual
    return x, _rms(x, g)


def _loss_fn(x, target, g):
    def f(x, g):
        err = jnp.square(_rms(x, g) - target)
        return 0.5 * jnp.sum(jnp.mean(err, axis=-1, keepdims=True), axis=0, keepdims=True)

    loss, vjp = jax.vjp(f, x, g)
    dx, dg = vjp(jnp.ones((1, 1), F32))
    return dx, dg, jnp.broadcast_to(loss, (1, PACK_LANES))


def _adamw_fn(w, g, m, v):
    m = ADAM_B1 * m + (1.0 - ADAM_B1) * g
    v = ADAM_B2 * v + (1.0 - ADAM_B2) * jnp.square(g)
    m_hat = m / (1.0 - ADAM_B1 ** ADAM_STEP)
    v_hat = v / (1.0 - ADAM_B2 ** ADAM_STEP)
    delta = -ADAM_LR * (m_hat / (jnp.sqrt(v_hat) + ADAM_EPS) + ADAM_WD * w)
    return delta, m, v


def _group_pick(j):
    return [(j == g).astype(F32) for g in range(len(POOL_WINDOWS))]


def _accumulate_row(ref, row, value, first):
    @pl.when(first)
    def _():
        ref[pl.ds(row, 1), :] = value

    @pl.when(jnp.logical_not(first))
    def _():
        ref[pl.ds(row, 1), :] += value


SUBLANES = 8


def _causal_taps(x, taps):
    for r in range(min(SUBLANES, taps)):
        behind = x if r == 0 else pltpu.roll(x, r, 0)
        for q in range((taps - 1 - r) // SUBLANES + 1):
            start = HALO - SUBLANES * q
            yield SUBLANES * q + r, behind[start:start + SEQ_TILE, :]


def _rows_iota(n):
    return lax.broadcasted_iota(jnp.int32, (n, 1), 0)


def _seq_specs(n_time, causal):
    per = SEQ_TILE // HALO
    tile = pl.BlockSpec((SEQ_TILE, GROUP), lambda j, i: (i, j))
    if causal:
        halo = pl.BlockSpec((HALO, GROUP), lambda j, i: (jnp.maximum(i * per - 1, 0), j))
    else:
        halo = pl.BlockSpec((HALO, GROUP), lambda j, i: (jnp.minimum((i + 1) * per, n_time * per - 1), j))
    return tile, halo


def _seq_fwd(ga, proj, chx, conf_dw, sc_conv):
    s_len = ga.shape[0]
    n_time = s_len // SEQ_TILE
    ext = HALO + SEQ_TILE

    def body(ga_ref, ga_h, pin_ref, pin_h, chx_ref, chx_h, wc_ref, ws_ref, ca_ref, po_ref, cc_ref):
        j, i = pl.program_id(0), pl.program_id(1)
        keep = (i > 0).astype(F32)

        def extended(cur, halo):
            return jnp.concatenate([halo[...] * keep, cur[...]], axis=0)

        def conv(x, w_ref, taps):
            acc = None
            for d, rows in _causal_taps(x, taps):
                term = rows * w_ref[pl.ds(taps - 1 - d, 1), :]
                acc = term if acc is None else acc + term
            return acc

        ca_ref[...] = conv(extended(ga_ref, ga_h), wc_ref, CONF_KERNEL)
        cc_ref[...] = conv(extended(chx_ref, chx_h), ws_ref, SC_KERNEL)
        u = extended(pin_ref, pin_h)
        s1 = u + pltpu.roll(u, 1, 0)
        s2 = s1 + pltpu.roll(s1, 2, 0)
        s3 = s2 + pltpu.roll(s2, 4, 0)
        s4 = s3 + pltpu.roll(s3, 8, 0)
        pick = _group_pick(j)
        window_sum = s1 * pick[0] + s2 * pick[1] + s3 * pick[2] + s4 * pick[3]
        window = sum(float(w) * f for w, f in zip(POOL_WINDOWS, pick))
        pos = (_rows_iota(ext) + (i * SEQ_TILE - HALO + 1)).astype(F32)
        count = jnp.maximum(jnp.minimum(pos, window), 1.0)
        po_ref[...] = (window_sum / count - u)[HALO:, :]

    tile, halo = _seq_specs(n_time, causal=True)
    pin_tile = pl.BlockSpec((SEQ_TILE, GROUP), lambda j, i: (i, 2 * (MIX // GROUP) + j))
    pin_halo = pl.BlockSpec((HALO, GROUP), lambda j, i: (jnp.maximum(i * (SEQ_TILE // HALO) - 1, 0), 2 * (MIX // GROUP) + j))
    w_spec = lambda taps: pl.BlockSpec((taps, GROUP), lambda j, i: (0, j))
    return pl.pallas_call(
        body, name="seq_fwd", grid=(MIX // GROUP, n_time),
        in_specs=[tile, halo, pin_tile, pin_halo, tile, halo, w_spec(CONF_KERNEL), w_spec(SC_KERNEL)],
        out_specs=[tile] * 3, out_shape=[jax.ShapeDtypeStruct((s_len, MIX), F32)] * 3,
        compiler_params=_params("parallel", "arbitrary"),
    )(ga, ga, proj, proj, chx, chx, conf_dw, sc_conv)


def _seq_bwd(dca, dpooled, dcc, ga, chx, conf_dw, sc_conv):
    s_len = ga.shape[0]
    n_time = s_len // SEQ_TILE
    ext = HALO + SEQ_TILE

    def body(dca_ref, dca_h, dpo_ref, dpo_h, dcc_ref, dcc_h, ga_ref, ga_h, chx_ref, chx_h, wc_ref, ws_ref,
             dga_ref, dpin_ref, dchx_ref, dwc_ref, dws_ref):
        j, i = pl.program_id(0), pl.program_id(1)
        keep_prev = (i > 0).astype(F32)
        keep_next = (i < n_time - 1).astype(F32)

        def with_next(cur, halo):
            return jnp.concatenate([cur[...], halo[...] * keep_next], axis=0)

        def with_prev(cur, halo):
            return jnp.concatenate([halo[...] * keep_prev, cur[...]], axis=0)

        def conv_t(dy, w_ref, taps):
            acc = None
            for r in range(min(SUBLANES, taps)):
                ahead = dy if r == 0 else pltpu.roll(dy, ext - r, 0)
                for q in range((taps - 1 - r) // SUBLANES + 1):
                    term = ahead[SUBLANES * q:SUBLANES * q + SEQ_TILE, :] * w_ref[pl.ds(taps - 1 - (SUBLANES * q + r), 1), :]
                    acc = term if acc is None else acc + term
            return acc

        def tap_grads(dy_cur, x, dw_ref, taps):
            for d, rows in _causal_taps(x, taps):
                part = jnp.sum(dy_cur * rows, axis=0, keepdims=True)
                _accumulate_row(dw_ref, taps - 1 - d, part, i == 0)

        dga_ref[...] = conv_t(with_next(dca_ref, dca_h), wc_ref, CONF_KERNEL)
        dchx_ref[...] = conv_t(with_next(dcc_ref, dcc_h), ws_ref, SC_KERNEL)
        tap_grads(dca_ref[...], with_prev(ga_ref, ga_h), dwc_ref, CONF_KERNEL)
        tap_grads(dcc_ref[...], with_prev(chx_ref, chx_h), dws_ref, SC_KERNEL)

        dpo = with_next(dpo_ref, dpo_h)
        pick = _group_pick(j)
        window = sum(float(w) * f for w, f in zip(POOL_WINDOWS, pick))
        pos = (_rows_iota(ext) + (i * SEQ_TILE + 1)).astype(F32)
        r = dpo / jnp.minimum(pos, window)
        q1 = r + pltpu.roll(r, ext - 1, 0)
        q2 = q1 + pltpu.roll(q1, ext - 2, 0)
        q3 = q2 + pltpu.roll(q2, ext - 4, 0)
        q4 = q3 + pltpu.roll(q3, ext - 8, 0)
        q = q1 * pick[0] + q2 * pick[1] + q3 * pick[2] + q4 * pick[3]
        dpin_ref[...] = (q - dpo)[:SEQ_TILE, :].astype(dpin_ref.dtype)

    tile, nxt = _seq_specs(n_time, causal=False)
    _, prv = _seq_specs(n_time, causal=True)
    w_spec = lambda taps: pl.BlockSpec((taps, GROUP), lambda j, i: (0, j))
    return pl.pallas_call(
        body, name="seq_bwd", grid=(MIX // GROUP, n_time),
        in_specs=[tile, nxt, tile, nxt, tile, nxt, tile, prv, tile, prv, w_spec(CONF_KERNEL), w_spec(SC_KERNEL)],
        out_specs=[tile, tile, tile, w_spec(CONF_KERNEL), w_spec(SC_KERNEL)],
        out_shape=[jax.ShapeDtypeStruct((s_len, MIX), F32), jax.ShapeDtypeStruct((s_len, MIX), BF16),
                   jax.ShapeDtypeStruct((s_len, MIX), F32), jax.ShapeDtypeStruct(conf_dw.shape, F32),
                   jax.ShapeDtypeStruct(sc_conv.shape, F32)],
        compiler_params=_params("parallel", "arbitrary"),
    )(dca, dca, dpooled, dpooled, dcc, dcc, ga, ga, chx, chx, conf_dw, sc_conv)


MERGE_TILE = 512


def _merge_fwd(y, w_branch, gates):
    s_len, d_model = y.shape[1], w_branch.shape[2]
    t = min(MERGE_TILE, s_len)
    per_branch = d_model // MERGE_TILE

    def body(y_ref, w_ref, gate_ref, out_ref, acc_ref):
        k = pl.program_id(2)

        @pl.when(k == 0)
        def _():
            acc_ref[...] = jnp.zeros_like(acc_ref)

        z = jnp.dot(y_ref[...], w_ref[...], preferred_element_type=F32)
        acc_ref[...] += gate_ref[...].astype(F32) * z

        @pl.when(k == 3)
        def _():
            out_ref[...] = acc_ref[...].astype(out_ref.dtype)

    return pl.pallas_call(
        body, name="merge_fwd", grid=(s_len // t, per_branch, 4),
        in_specs=[pl.BlockSpec((None, t, MIX), lambda i, j, k: (k, i, 0)),
                  pl.BlockSpec((None, MIX, MERGE_TILE), lambda i, j, k: (k, 0, j)),
                  pl.BlockSpec((t, MERGE_TILE), lambda i, j, k: (i, k * per_branch + j))],
        out_specs=pl.BlockSpec((t, MERGE_TILE), lambda i, j, k: (i, j)),
        out_shape=jax.ShapeDtypeStruct((s_len, d_model), BF16),
        scratch_shapes=[pltpu.VMEM((t, MERGE_TILE), F32)],
        compiler_params=_params("parallel", "parallel", "arbitrary"),
    )(y, w_branch, gates)


def _merge_bwd(y, w_branch, gates, dmerged):
    s_len, d_model = y.shape[1], w_branch.shape[2]
    t = min(MERGE_TILE, s_len)
    per_branch = d_model // MERGE_TILE

    def body(y_ref, w_ref, gate_ref, dm_ref, dz_ref, dgate_ref):
        z = jnp.dot(y_ref[...], w_ref[...], preferred_element_type=F32)
        gate = gate_ref[...].astype(F32)
        dm = dm_ref[...]
        dz_ref[...] = (dm * gate).astype(dz_ref.dtype)
        dgate_ref[...] = (dm * z * gate * (1.0 - gate)).astype(dgate_ref.dtype)

    return pl.pallas_call(
        body, name="merge_bwd", grid=(s_len // t, per_branch, 4),
        in_specs=[pl.BlockSpec((None, t, MIX), lambda i, j, k: (k, i, 0)),
                  pl.BlockSpec((None, MIX, MERGE_TILE), lambda i, j, k: (k, 0, j)),
                  pl.BlockSpec((t, MERGE_TILE), lambda i, j, k: (i, k * per_branch + j)),
                  pl.BlockSpec((t, MERGE_TILE), lambda i, j, k: (i, j))],
        out_specs=[pl.BlockSpec((None, t, MERGE_TILE), lambda i, j, k: (k, i, j)),
                   pl.BlockSpec((t, MERGE_TILE), lambda i, j, k: (i, k * per_branch + j))],
        out_shape=[jax.ShapeDtypeStruct((4, s_len, d_model), BF16), jax.ShapeDtypeStruct((s_len, 4 * d_model), BF16)],
        compiler_params=_params("parallel", "parallel", "arbitrary"),
    )(y, w_branch, gates, dmerged)


def _row(v):
    return v.reshape(1, -1)


def _branch_params(w):
    mats = [w["pool_w"][g] for g in range(4)] + [w["gmlp_ws"][g] for g in range(4)]
    mats += [w["gmlp_bs"][g].reshape(GROUP, 1) for g in range(4)]
    return [_row(w["conf_dw_b"]), _row(w["conf_ln_g"]), _row(w["conf_ln_b"]), _row(w["pool_scale"]),
            _row(w["gmlp_ln_g"]), _row(w["gmlp_ln_b"])] + mats


def _layer_fwd(x0, p_l, w, fetch):
    d_model = x0.shape[1]
    (h1,) = _rowwise("rms_mix", _rms, [(x0, d_model, 0)], [_row(w["norm_mix"])], [(d_model, BF16)])
    w.update(fetch(0, h1))
    (proj,) = _mm2("proj", h1, w["w_in"], b_by_chip=True, chips=(0, 2))
    w.update(fetch(1, proj))
    (gates,) = _mm2("proj_gates", h1, w["w_in"], b_by_chip=True, chips=(2, 2), out_dtypes=(BF16,),
                    epi=lambda acc: (jax.nn.sigmoid(acc),))
    ga, chx = _rowwise("glu", _glu_and_gate_fn, [(proj, MIX, 0), (proj, MIX, 1), (proj, MIX, 4), (proj, MIX, 5)], [],
                       [(MIX, F32), (MIX, F32)])
    ca, pooled, cc = _seq_fwd(ga, proj, chx, w["conf_dw"], w["sc_conv"])
    ys = _rowwise("branches", _branches_fn,
                  [(ca, MIX, 0), (pooled, MIX, 0), (cc, MIX, 0), (proj, MIX, 3), (proj, MIX, 6), (proj, MIX, 7)],
                  _branch_params(w), [(MIX, BF16)] * 4, tm=GROUP)
    y = jnp.stack(ys)
    merged = _merge_fwd(y, w["w_branch"], gates)
    x1, h2 = _mm2("out_proj", merged, w["w_out"], res=[(x0, 0)], rows=[_row(w["norm_mlp"])], out_dtypes=(F32, BF16),
                  epi=_residual_then_rms)
    up, act = _mm2("mlp_up", h2, w["w_up"], b_by_chip=True, out_dtypes=(F32, BF16),
                   epi=lambda acc: (acc, jnp.square(jnp.maximum(acc, 0.0))))
    x2, h3 = _mm2("mlp_down", act, w["w_down"], res=[(x1, 0)], rows=[_row(w["norm_ple"])], out_dtypes=(F32, BF16),
                  epi=_residual_then_rms)
    (pg,) = _mm2("ple_gate", h3, w["w_ple_gate"])
    (x3,) = _mm2("ple", p_l, w["w_ple"], res=[(pg, 0), (x2, 0)],
                 epi=lambda acc, g, r: (acc * jax.nn.sigmoid(g) + r,))
    saved = dict(x0=x0, h1=h1, proj=proj, gates=gates, ga=ga, chx=chx, ca=ca, pooled=pooled, cc=cc, y=y, merged=merged, x1=x1,
                 h2=h2, up=up, act=act, x2=x2, h3=h3, pg=pg)
    return x3, saved


def _layer_bwd(dx3, p_l, w, s, send):
    d_model = dx3.shape[1]
    big, small = {}, {}

    def ple_epi(acc, dx, g):
        sig = jax.nn.sigmoid(g)
        return dx * sig, dx * acc * sig * (1.0 - sig)

    dpp, dpg = _mm2("ple_bwd", p_l, w["w_ple"], res=[(dx3, 0), (s["pg"], 0)], epi=ple_epi, out_dtypes=(BF16, BF16))
    (big["w_ple"],) = _mm2("dw_ple", p_l, dpp, ta=True, out_dtypes=(BF16,))
    (big["w_ple_gate"],) = _mm2("dw_ple_gate", s["h3"], dpg, ta=True, out_dtypes=(BF16,))
    dx2, dg = _mm2("dh_ple", dpg, w["w_ple_gate"], tb=True, res=[(s["x2"], 0), (dx3, 0)], rows=[_row(w["norm_ple"])],
                   epi=_rms_bwd_epilogue, n_row_outs=1)
    small["norm_ple"] = _sum_row_out(dg)

    (dup,) = _mm2("mlp_down_bwd", dx2, w["w_down"], tb=True, res=[(s["up"], 0)], out_dtypes=(BF16,),
                  epi=lambda acc, up: (acc * 2.0 * jnp.maximum(up, 0.0),))
    (big["w_down"],) = _mm2("dw_down", s["act"], dx2, ta=True, out_dtypes=(BF16,))
    (big["w_up"],) = _mm2("dw_up", s["h2"], dup, ta=True, out_dtypes=(BF16,))
    sent = send(0, [big[n] for n in GRAD_GROUPS[0]])
    dx1, dg = _mm2("mlp_up_bwd", dup, w["w_up"], tb=True, b_by_chip=True, res=[(s["x1"], 0), (dx2, 0)],
                   rows=[_row(w["norm_mlp"]) + sent], epi=_rms_bwd_epilogue, n_row_outs=1)
    small["norm_mlp"] = _sum_row_out(dg)

    (dmerged,) = _mm2("out_proj_bwd", dx1, w["w_out"], tb=True)
    (big["w_out"],) = _mm2("dw_out", s["merged"], dx1, ta=True, out_dtypes=(BF16,))
    dz, dgate = _merge_bwd(s["y"], w["w_branch"], s["gates"], dmerged)
    (dy,) = _mm("branch_bwd", dz, w["w_branch"], tb=True)
    (big["w_branch"],) = _mm("dw_branch", s["y"], dz, ta=True, out_dtypes=(BF16,))

    proj = s["proj"]
    params = _branch_params(w)
    grads = _rowwise(
        "branches_bwd", _branches_bwd_fn,
        [(s["ca"], MIX, 0), (s["pooled"], MIX, 0), (s["cc"], MIX, 0), (proj, MIX, 3), (proj, MIX, 6), (proj, MIX, 7)]
        + [(dy, MIX, 0, g) for g in range(4)], params,
        [(MIX, F32), (MIX, F32), (MIX, F32), (MIX, BF16), (MIX, BF16), (MIX, BF16)], [p.shape for p in params], tm=GROUP)
    dca, dpooled, dcc, dbg, du, dv = grads[:6]
    pg_ = grads[6:]
    small["conf_dw_b"], small["conf_ln_g"], small["conf_ln_b"], small["pool_scale"] = pg_[0], pg_[1], pg_[2], pg_[3]
    small["gmlp_ln_g"], small["gmlp_ln_b"] = pg_[4], pg_[5]
    small["pool_w"] = jnp.stack(pg_[6:10])
    small["gmlp_ws"] = jnp.stack(pg_[10:14])
    small["gmlp_bs"] = jnp.stack([b.reshape(GROUP) for b in pg_[14:18]])

    dga, dpin, dchx, small["conf_dw"], small["sc_conv"] = _seq_bwd(dca, dpooled, dcc, s["ga"], s["chx"], w["conf_dw"], w["sc_conv"])
    da, dag, dcg, dhx = _rowwise("glu_bwd", _glu_bwd_fn,
                                 [(proj, MIX, 0), (proj, MIX, 1), (proj, MIX, 4), (proj, MIX, 5), (dga, MIX, 0), (dchx, MIX, 0)],
                                 [], [(MIX, BF16)] * 4)
    dproj = jnp.concatenate([da, dag, dpin, dbg, dcg, dhx, du, dv, dgate], axis=1)
    (big["w_in"],) = _mm2("dw_in", s["h1"], dproj, ta=True, out_dtypes=(BF16,))
    sent = send(1, [big[n] for n in GRAD_GROUPS[1]])
    dx0, dg = _mm2("proj_bwd", dproj, w["w_in"], tb=True, b_by_chip=True, res=[(s["x0"], 0), (dx1, 0)],
                   rows=[_row(w["norm_mix"]) + sent], epi=_rms_bwd_epilogue, n_row_outs=1)
    small["norm_mix"] = _sum_row_out(dg)
    return dx0, small


def _position():
    return lax.axis_index("x"), lax.axis_index("y"), lax.axis_index("c")


def _other_chips(x, y):
    return [(1 - x, y), (x, 1 - y), (1 - x, 1 - y)]


def _region(ref, cuts):
    return ref.at[tuple(pl.ds(*cuts[a]) if a in cuts else slice(None) for a in range(len(ref.shape)))]


def _allgather_small(name, block):
    m_per, n = block.shape

    def body(x_ref, out_ref, send_sems, recv_sems, local_sem):
        x, y, c = _position()
        me, sibling = (x, y, c), (x, y, 1 - c)
        chips = _other_chips(x, y)

        def rows(px, py, pc):
            return out_ref.at[pl.ds((4 * px + 2 * py + pc) * m_per, m_per), :]

        def copy(k, blk, to, src=None):
            return pltpu.make_async_remote_copy(
                src_ref=rows(*blk) if src is None else src, dst_ref=rows(*blk), send_sem=send_sems.at[k],
                recv_sem=recv_sems.at[k], device_id=to, device_id_type=pl.DeviceIdType.MESH)

        mine = pltpu.make_async_copy(x_ref, rows(*me), local_sem)
        mine.start()
        first = [copy(0, me, sibling, src=x_ref)]
        first += [copy(1 + j, me, (*chip, c), src=x_ref) for j, chip in enumerate(chips)]
        for cp in first:
            cp.start()
        passed = [copy(4 + j, (*chip, c), sibling) for j, chip in enumerate(chips)]
        for j, chip in enumerate(chips):
            copy(1 + j, (*chip, c), me).wait_recv()
            passed[j].start()
        copy(0, sibling, me).wait_recv()
        for j, chip in enumerate(chips):
            copy(4 + j, (*chip, 1 - c), me).wait_recv()
        for cp in first + passed:
            cp.wait_send()
        mine.wait()

    return pl.pallas_call(
        body, name=name, out_shape=jax.ShapeDtypeStruct((8 * m_per, n), block.dtype), in_specs=[ANY], out_specs=ANY,
        scratch_shapes=[pltpu.SemaphoreType.DMA((7,)), pltpu.SemaphoreType.DMA((7,)), pltpu.SemaphoreType.DMA],
    )(block)


def _cuts(arr_shape, shard_axis, half_axis, shard=None, half=None):
    cuts = {}
    if shard is not None:
        n = arr_shape[shard_axis] // 4
        cuts[shard_axis] = (shard * n, n)
    if half is not None:
        n = arr_shape[half_axis] // 2
        cuts[half_axis] = (half * n, n)
    return cuts


def _full_shape(shard_shape, shard_axis):
    return tuple(4 * n if a == shard_axis else n for a, n in enumerate(shard_shape))


HBM = pl.BlockSpec(memory_space=pltpu.HBM)
SEM = pl.BlockSpec(memory_space=pltpu.SEMAPHORE)
DATAFLOW_EFFECT = pltpu.SideEffectType.DATAFLOW_SIDE_EFFECTING
TOKEN_SHAPE = (8, 128)


def _remote(src, dst, send_sems, recv_sems, k, peer):
    return pltpu.make_async_remote_copy(src_ref=src, dst_ref=dst, send_sem=send_sems.at[k], recv_sem=recv_sems.at[k],
                                        device_id=peer, device_id_type=pl.DeviceIdType.MESH)


def _split_start(name, sources, landings, plan):
    n_s, n_l = len(sources), len(landings)
    n_copies = len(plan([None] * n_s, [None] * n_l, None))

    def body(*refs):
        src, land = refs[:n_s], refs[n_s:n_s + n_l]
        send_sems, recv_sems, token = refs[n_s + n_l], refs[n_s + n_l + 1], refs[-1]
        for k, (s, d, peer) in enumerate(plan(src, land, _position())):
            _remote(s, d, send_sems, recv_sems, k, peer).start()
        token[...] = jnp.zeros_like(token)

    arrays = [pltpu.with_memory_space_constraint(a, pltpu.HBM) for a in (*sources, *landings)]
    outs = pl.pallas_call(
        body, name=name,
        out_shape=(pltpu.SemaphoreType.DMA((n_copies,)), pltpu.SemaphoreType.DMA((n_copies,)),
                   *[pltpu.HBM(a.shape, a.dtype) for a in arrays], jax.ShapeDtypeStruct(TOKEN_SHAPE, F32)),
        in_specs=[HBM] * (n_s + n_l),
        out_specs=(SEM, SEM, *[HBM] * (n_s + n_l), pl.BlockSpec(memory_space=pltpu.VMEM)),
        input_output_aliases={i: 2 + i for i in range(n_s + n_l)},
        compiler_params=pltpu.CompilerParams(has_side_effects=DATAFLOW_EFFECT),
    )(*arrays)
    return outs[0], outs[1], list(outs[2:2 + n_s]), list(outs[2 + n_s:2 + n_s + n_l]), outs[-1]


def _split_wait(name, started, after, plan):
    send_sems, recv_sems, sources, landings, _ = started
    n_s, n_l = len(sources), len(landings)

    def body(*refs):
        src, land = refs[:n_s], refs[n_s:n_s + n_l]
        send_sems_ref, recv_sems_ref = refs[n_s + n_l], refs[n_s + n_l + 1]
        for k, (s, d, peer) in enumerate(plan(src, land, _position())):
            cp = _remote(s, d, send_sems_ref, recv_sems_ref, k, peer)
            cp.wait_send()
            cp.wait_recv()

    outs = pl.pallas_call(
        body, name=name, out_shape=tuple(pltpu.HBM(a.shape, a.dtype) for a in (*sources, *landings)),
        in_specs=[*[HBM] * (n_s + n_l), SEM, SEM, ANY], out_specs=tuple([HBM] * (n_s + n_l)),
        input_output_aliases={i: i for i in range(n_s + n_l)},
        compiler_params=pltpu.CompilerParams(has_side_effects=DATAFLOW_EFFECT),
    )(*sources, *landings, send_sems, recv_sems, after)
    return list(outs[:n_s]), list(outs[n_s:])


def _exchange(name, inputs, out_shapes, plan, in_place=False):
    n_in = len(inputs)
    n_out = n_in if in_place else len(out_shapes)
    n_copies = len(plan([None] * n_in, [None] * n_out, None))

    def body(*refs):
        ins, outs = refs[:n_in], refs[n_in:n_in + n_out]
        send_sems, recv_sems = refs[n_in + n_out:]
        copies = [_remote(s, d, send_sems, recv_sems, k, peer)
                  for k, (s, d, peer) in enumerate(plan(ins, ins if in_place else outs, _position()))]
        for cp in copies:
            cp.start()
        for cp in copies:
            cp.wait()

    if in_place:
        out_shapes = [jax.ShapeDtypeStruct(a.shape, a.dtype) for a in inputs]
    return pl.pallas_call(
        body, name=name, in_specs=[ANY] * n_in, out_specs=[ANY] * n_out, out_shape=out_shapes,
        input_output_aliases={i: i for i in range(n_in)} if in_place else {},
        scratch_shapes=[pltpu.SemaphoreType.DMA((n_copies,)), pltpu.SemaphoreType.DMA((n_copies,))],
    )(*inputs)


def _half_rows(ref, half):
    n = ref.shape[0] // 2
    return ref.at[pl.ds(half * n, n)]


def _gather_plan(sources, landings, pos):
    if pos is None:
        return [None] * (3 * len(sources))
    x, y, c = pos
    copies = []
    for src, land in zip(sources, landings):
        for qx, qy in _other_chips(x, y):
            copies.append((_half_rows(src, c), _half_rows(land.at[2 * x + y], c), (qx, qy, c)))
    return copies


def _sibling_plan(landings, _, pos):
    if pos is None:
        return [None] * (3 * len(landings))
    x, y, c = pos
    copies = []
    for land in landings:
        for qx, qy in _other_chips(x, y):
            region = _half_rows(land.at[2 * qx + qy], c)
            copies.append((region, region, (x, y, 1 - c)))
    return copies


def _weights_start(tag, shards):
    landings = [lax.empty((4, *s.shape), BF16) for s in shards]
    return _split_start(f"weights_start_{tag}", shards, landings, _gather_plan)


def _weights_finish(tag, names, started, after):
    shards, landings = _split_wait(f"weights_wait_{tag}", started, after, _gather_plan)
    landings = _exchange("weights_to_sibling", landings, None, _sibling_plan, in_place=True)
    my_shard = 2 * lax.axis_index("x") + lax.axis_index("y")
    full = {}
    for name, shard, land in zip(names, shards, landings):
        sa = BIG_AXES[name][0]
        land = lax.dynamic_update_index_in_dim(land, shard, my_shard, axis=0)
        if name in BY_CHIP:
            full[name] = land
            continue
        axis = sa - 1
        moved = jnp.moveaxis(land, 0, axis)
        full[name] = moved.reshape(*moved.shape[:axis], 4 * moved.shape[axis + 1], *moved.shape[axis + 2:])
    return full


def _half_shape(shape, half_axis):
    return tuple(n // 2 if a == half_axis else n for a, n in enumerate(shape))


def _shard_shape(shape, shard_axis):
    return tuple(n // 4 if a == shard_axis else n for a, n in enumerate(shape))


def _flat2d(a):
    return a.reshape(-1, a.shape[-1])


ELEMENTWISE_TILE_ELEMENTS = 1 << 18


def _row_tile(rows, cols):
    tile = 1
    while rows % (2 * tile) == 0 and 2 * tile * cols <= ELEMENTWISE_TILE_ELEMENTS:
        tile *= 2
    assert tile >= 16 or tile == rows, (rows, cols)
    return tile


def _sum_arrays(name, arrays, out_dtype):
    shape = arrays[0].shape
    flat = [_flat2d(a) for a in arrays]
    rows, cols = flat[0].shape
    tm = _row_tile(rows, cols)

    def fn(*tiles):
        acc = tiles[0].astype(F32)
        for t in tiles[1:]:
            acc = acc + t.astype(F32)
        return acc

    (out,) = _rowwise(name, fn, [(a, cols, 0) for a in flat], [], [(cols, out_dtype)], tm=tm)
    return out.reshape(shape)


def _grads_start(tag, names, grads):
    c = lax.axis_index("c")
    shapes = [g.shape for g in grads]
    axes = [(BIG_AXES[n][0] - 1, BIG_AXES[n][1] - 1) for n in names]

    def to_sibling(ins, outs, pos):
        if pos is None:
            return [None] * len(names)
        px, py, pc = pos
        return [(_region(ins[wi], _cuts(shapes[wi], sa, ha, half=1 - pc)), outs[wi], (px, py, 1 - pc))
                for wi, (sa, ha) in enumerate(axes)]

    landed = _exchange("grads_to_sibling", grads, [jax.ShapeDtypeStruct(_half_shape(s, ha), BF16) for s, (_, ha) in zip(shapes, axes)],
                       to_sibling)
    chip_sums = []
    for wi, name in enumerate(names):
        ha = axes[wi][1]
        n = shapes[wi][ha] // 2
        mine = lax.dynamic_slice_in_dim(grads[wi], c * n, n, axis=ha)
        chip_sums.append(_sum_arrays("chip_sum_" + name, [mine, landed[wi]], BF16))

    def to_chips(sources, landings, pos):
        if pos is None:
            return [None] * (3 * len(names))
        px, py, pc = pos
        copies = []
        for wi, (sa, ha) in enumerate(axes):
            for j, (qx, qy) in enumerate(_other_chips(px, py)):
                piece = _region(sources[wi], _cuts(shapes[wi], sa, ha, shard=2 * qx + qy))
                copies.append((piece, landings[wi].at[j], (qx, qy, pc)))
        return copies

    landings = [lax.empty((3, *_shard_shape(t.shape, sa)), BF16) for t, (sa, _) in zip(chip_sums, axes)]
    return _split_start(f"grads_start_{tag}", chip_sums, landings, to_chips), to_chips


def _grads_finish(tag, names, started, after):
    chip_sums, pieces = _split_wait(f"grads_wait_{tag}", started[0], after, started[1])
    my_shard = 2 * lax.axis_index("x") + lax.axis_index("y")
    reduced = {}
    for name, chip_sum, piece in zip(names, chip_sums, pieces):
        sa = BIG_AXES[name][0] - 1
        n = chip_sum.shape[sa] // 4
        mine = lax.dynamic_slice_in_dim(chip_sum, my_shard * n, n, axis=sa)
        reduced[name] = _sum_arrays("shard_sum_" + name, [mine, piece[0], piece[1], piece[2]], F32)
    return reduced


def _join_halves(reduced):
    c = lax.axis_index("c")

    def halves_to_sibling(ins, outs, pos):
        if pos is None:
            return [None] * len(BIG)
        px, py, pc = pos
        return [(ins[wi], outs[wi], (px, py, 1 - pc)) for wi in range(len(BIG))]

    from_sibling = _exchange("grad_halves_to_sibling", reduced, [jax.ShapeDtypeStruct(r.shape, F32) for r in reduced],
                             halves_to_sibling)
    out = []
    for wi, (_, _, ha) in enumerate(BIG):
        low = jnp.where(c == 0, reduced[wi], from_sibling[wi])
        high = jnp.where(c == 0, from_sibling[wi], reduced[wi])
        out.append(jnp.concatenate([low, high], axis=ha))
    return out


def _pack(arrays):
    flat = jnp.concatenate([a.reshape(-1).astype(F32) for a in arrays])
    rows = -(-flat.shape[0] // PACK_LANES)
    rows = -(-rows // PACK_ROW_MULTIPLE) * PACK_ROW_MULTIPLE
    return jnp.pad(flat, (0, rows * PACK_LANES - flat.shape[0])).reshape(rows, PACK_LANES)


def _unpack(buf, shapes):
    flat, out, off = buf.reshape(-1), [], 0
    for s in shapes:
        n = 1
        for d in s:
            n *= d
        out.append(flat[off:off + n].reshape(s))
        off += n
    return out


def _sum8(name, gathered, rows):
    def body(in_ref, out_ref):
        acc = in_ref[0].astype(F32)
        for d in range(1, 8):
            acc = acc + in_ref[d].astype(F32)
        out_ref[...] = acc

    return pl.pallas_call(
        body, name=name, grid=(rows // PACK_ROW_MULTIPLE,),
        in_specs=[pl.BlockSpec((8, PACK_ROW_MULTIPLE, PACK_LANES), lambda i: (0, i, 0))],
        out_specs=pl.BlockSpec((PACK_ROW_MULTIPLE, PACK_LANES), lambda i: (i, 0)),
        out_shape=jax.ShapeDtypeStruct((rows, PACK_LANES), F32), compiler_params=_params("arbitrary"),
    )(gathered.reshape(8, rows, PACK_LANES))


def _adamw(name, w, g, m, v):
    shape = w.shape
    flat = [_flat2d(a) for a in (w, g, m, v)]
    rows, cols = flat[0].shape
    tm = _row_tile(rows, cols)
    outs = _rowwise(name, _adamw_fn, [(a, cols, 0) for a in flat], [], [(cols, F32)] * 3, tm=tm)
    return [o.reshape(shape) for o in outs]


WEIGHT_ORDER = ("norm_mix", "w_in", "conf_dw", "conf_dw_b", "conf_ln_g", "conf_ln_b", "pool_w", "pool_scale", "sc_conv",
                "gmlp_ln_g", "gmlp_ln_b", "gmlp_ws", "gmlp_bs", "w_branch", "w_out", "norm_mlp", "w_up", "w_down",
                "norm_ple", "w_ple", "w_ple_gate", "norm_final")


def _local_step(x, p, loss_target, weights, layer_big, full_small_sharded, on_big_grads):
    n_layers = p.shape[0]
    d_model = x.shape[1]
    layer_w, saved = [], []
    h = x
    for l in range(n_layers):
        w = {n: weights[n][l] for n in SMALL_REP if n != "norm_final"}
        w.update({n: full_small_sharded[n][l] for n in SMALL_SHARDED})
        layer_w.append(w)
        h, s = _layer_fwd(h, p[l], w, layer_big(l))
        saved.append(s)
    dx, d_final, loss = _rowwise("loss", _loss_fn, [(h, d_model, 0), (loss_target, d_model, 0)], [_row(weights["norm_final"])],
                                 [(d_model, F32)], [(1, d_model), (1, PACK_LANES)])
    small_grads = {n: [None] * n_layers for n in SMALL_REP + SMALL_SHARDED if n != "norm_final"}
    for l in reversed(range(n_layers)):
        dx, small = _layer_bwd(dx, p[l], layer_w[l], saved[l], on_big_grads(l))
        for n in small:
            small_grads[n][l] = small[n].reshape(weights[n].shape[1:]) if n in SMALL_REP else small[n]
    small_grads = {n: jnp.stack(v) for n, v in small_grads.items()}
    small_grads["norm_final"] = d_final.reshape(-1)
    return loss, dx, small_grads


def kernel(x, p, norm_mix, w_in, conf_dw, conf_dw_b, conf_ln_g, conf_ln_b, pool_w, pool_scale, sc_conv, gmlp_ln_g, gmlp_ln_b, gmlp_ws, gmlp_bs, w_branch, w_out, norm_mlp, w_up, w_down, norm_ple, w_ple, w_ple_gate, norm_final, loss_target, m_norm_mix, m_w_in, m_conf_dw, m_conf_dw_b, m_conf_ln_g, m_conf_ln_b, m_pool_w, m_pool_scale, m_sc_conv, m_gmlp_ln_g, m_gmlp_ln_b, m_gmlp_ws, m_gmlp_bs, m_w_branch, m_w_out, m_norm_mlp, m_w_up, m_w_down, m_norm_ple, m_w_ple, m_w_ple_gate, m_norm_final, v_norm_mix, v_w_in, v_conf_dw, v_conf_dw_b, v_conf_ln_g, v_conf_ln_b, v_pool_w, v_pool_scale, v_sc_conv, v_gmlp_ln_g, v_gmlp_ln_b, v_gmlp_ws, v_gmlp_bs, v_w_branch, v_w_out, v_norm_mlp, v_w_up, v_w_down, v_norm_ple, v_w_ple, v_w_ple_gate, v_norm_final):
    given = dict(locals())
    weights = {n: given[n] for n in WEIGHT_ORDER}
    mom_m = {n: given["m_" + n] for n in WEIGHT_ORDER}
    mom_v = {n: given["v_" + n] for n in WEIGHT_ORDER}
    my_shard = 2 * lax.axis_index("x") + lax.axis_index("y")

    n_layers = p.shape[0]
    shard_shapes = [weights[n].shape for n in SMALL_SHARDED]
    gathered = _allgather_small("allgather_small_weights", _pack([weights[n] for n in SMALL_SHARDED]))
    after_small = gathered[0, 0] * 0.0
    after_small = jnp.where(after_small == 0.0, after_small, 0.0)

    weight_copies = [[None] * len(WEIGHT_GROUPS) for _ in range(n_layers)]
    started = after_small
    for l in range(n_layers):
        for g, names in enumerate(WEIGHT_GROUPS):
            shards = [(weights[n][l] + started if i == 0 else weights[n][l]).astype(BF16) for i, n in enumerate(names)]
            weight_copies[l][g] = _weights_start(f"{l}_{g}", shards)
            started = weight_copies[l][g][4][0, 0]
    per_dev = gathered.reshape(8, -1, PACK_LANES)
    full_small_sharded = {}
    chip_parts = [_unpack(per_dev[2 * s], shard_shapes) for s in range(4)]
    for i, n in enumerate(SMALL_SHARDED):
        full_small_sharded[n] = jnp.concatenate([chip_parts[s][i] for s in range(4)], axis=-1)

    grad_copies = [[None] * len(GRAD_GROUPS) for _ in range(n_layers)]

    def layer_big(l):
        return lambda g, after: _weights_finish(f"{l}_{g}", WEIGHT_GROUPS[g], weight_copies[l][g], after)

    def on_big_grads(l):
        def send(g, group_grads):
            grad_copies[l][g] = _grads_start(f"{l}_{g}", GRAD_GROUPS[g], group_grads)
            return grad_copies[l][g][0][4][0, 0]
        return send

    loss_row, dx, small_grads = _local_step(x[0] + started, p[:, 0], loss_target[0], weights, layer_big, full_small_sharded,
                                            on_big_grads)
    loss = lax.psum(loss_row[0, 0], MESH_AXES)

    halves = [{} for _ in range(n_layers)]
    for l in reversed(range(n_layers)):
        for g, names in enumerate(GRAD_GROUPS):
            halves[l].update(_grads_finish(f"{l}_{g}", names, grad_copies[l][g], dx))
    big_reduced = _join_halves([jnp.stack([halves[l][n] for l in range(n_layers)]) for n, _, _ in BIG])
    grads = {n: g for (n, _, _), g in zip(BIG, big_reduced)}

    small_names = SMALL_REP + SMALL_SHARDED
    packed = _pack([small_grads[n] for n in small_names])
    rows = packed.shape[0]
    summed = _sum8("sum_small_grads", _allgather_small("allgather_small_grads", packed.astype(BF16)), rows)
    small_full_shapes = [weights[n].shape for n in SMALL_REP] + [full_small_sharded[n].shape for n in SMALL_SHARDED]
    for n, g in zip(small_names, _unpack(summed, small_full_shapes)):
        if n in SMALL_SHARDED:
            width = weights[n].shape[-1]
            g = lax.dynamic_slice_in_dim(g, my_shard * width, width, axis=g.ndim - 1)
        grads[n] = g

    delta, new_m, new_v = {}, {}, {}
    for n, _, _ in BIG:
        delta[n], new_m[n], new_v[n] = _adamw("adamw_" + n, weights[n], grads[n], mom_m[n], mom_v[n])
    for group, names in (("rep", SMALL_REP), ("sharded", SMALL_SHARDED)):
        outs = _adamw("adamw_small_" + group, *[_pack([src[n] for n in names]) for src in (weights, grads, mom_m, mom_v)])
        shapes = [weights[n].shape for n in names]
        for dst, buf in zip((delta, new_m, new_v), outs):
            for n, a in zip(names, _unpack(buf, shapes)):
                dst[n] = a

    return (loss, dx[None], *[grads[n] for n in WEIGHT_ORDER], *[delta[n] for n in WEIGHT_ORDER],
            *[new_m[n] for n in WEIGHT_ORDER], *[new_v[n] for n in WEIGHT_ORDER])
```

```python
import functools

import jax
import jax.numpy as jnp
from jax import lax
from jax.experimental import pallas as pl
from jax.experimental.pallas import tpu as pltpu

F32, BF16 = jnp.float32, jnp.bfloat16
EPS = 1e-6
ADAM_LR, ADAM_B1, ADAM_B2, ADAM_EPS, ADAM_WD, ADAM_STEP = 0.001, 0.9, 0.999, 1e-08, 0.01, 10
CONF_KERNEL, SC_KERNEL = 31, 3
POOL_WINDOWS = (2, 4, 8, 16)
MIX = 512
GROUP = 128
HALO = 32
SEQ_TILE = 256
VMEM_LIMIT_BYTES = 48 * 1024 * 1024
MESH_AXES = ("x", "y", "c")
ANY = pl.BlockSpec(memory_space=pl.ANY)

BIG = (("w_in", 2, 1), ("w_branch", 3, 1), ("w_out", 1, 2), ("w_up", 2, 1), ("w_down", 1, 2),
       ("w_ple", 2, 1), ("w_ple_gate", 1, 2))
SMALL_REP = ("norm_mix", "conf_dw_b", "conf_ln_g", "conf_ln_b", "pool_w", "pool_scale", "gmlp_ln_g",
             "gmlp_ln_b", "gmlp_ws", "gmlp_bs", "norm_mlp", "norm_ple", "norm_final")
SMALL_SHARDED = ("conf_dw", "sc_conv")
BY_CHIP = ("w_in", "w_up")
BIG_AXES = {name: (shard_axis, half_axis) for name, shard_axis, half_axis in BIG}
WEIGHT_GROUPS = (("w_in",), ("w_branch", "w_out", "w_up", "w_down", "w_ple_gate", "w_ple"))
GRAD_GROUPS = (("w_ple", "w_ple_gate", "w_down", "w_up"), ("w_out", "w_branch", "w_in"))
PACK_LANES = 128
PACK_ROW_MULTIPLE = 256


def _params(*semantics):
    return pltpu.CompilerParams(dimension_semantics=semantics, vmem_limit_bytes=VMEM_LIMIT_BYTES)


def _accumulate(ref, value, first):
    @pl.when(first)
    def _():
        ref[...] = value.astype(ref.dtype)

    @pl.when(jnp.logical_not(first))
    def _():
        ref[...] += value.astype(ref.dtype)


def _rowwise(name, fn, rows, params, outs, accs=(), tm=256):
    n_rows = rows[0][0].shape[-2]
    tm = min(tm, n_rows)
    n_in, n_out = len(rows) + len(params), len(outs)

    def body(*refs):
        res = fn(*[r[...] for r in refs[:n_in]])
        res = res if isinstance(res, (tuple, list)) else (res,)
        out_refs = refs[n_in:]
        for r, v in zip(out_refs[:n_out], res[:n_out]):
            r[...] = v.astype(r.dtype)
        first = pl.program_id(0) == 0
        for r, v in zip(out_refs[n_out:], res[n_out:]):
            _accumulate(r, v, first)

    in_specs = []
    for row in rows:
        arr, w, cb = row[:3]
        if len(row) == 4:
            in_specs.append(pl.BlockSpec((None, tm, w), lambda i, cb=cb, g=row[3]: (g, i, cb)))
        else:
            in_specs.append(pl.BlockSpec((tm, w), lambda i, cb=cb: (i, cb)))
    for p in params:
        in_specs.append(pl.BlockSpec(p.shape, lambda i, nd=p.ndim: (0,) * nd))
    out_specs = [pl.BlockSpec((tm, w), lambda i: (i, 0)) for w, _ in outs]
    out_specs += [pl.BlockSpec(s, lambda i, nd=len(s): (0,) * nd) for s in accs]
    out_shape = [jax.ShapeDtypeStruct((n_rows, w), d) for w, d in outs]
    out_shape += [jax.ShapeDtypeStruct(s, F32) for s in accs]
    return pl.pallas_call(
        body, name=name, grid=(n_rows // tm,), in_specs=in_specs, out_specs=out_specs, out_shape=out_shape,
        compiler_params=_params("arbitrary"),
    )(*[r[0] for r in rows], *params)


ROW_OUT_SUBLANES = 8


def _mm(name, a, b, *, ta=False, tb=False, b_by_chip=False, chips=(0, 4), res=(), rows=(), epi=None, out_dtypes=(F32,), n_row_outs=0,
        tm=1024, tn=1024, tk=1024):
    g_, m_, k_ = (a.shape[0], a.shape[2], a.shape[1]) if ta else a.shape
    if b_by_chip:
        n_ = b.shape[2] if tb else chips[1] * b.shape[3]
    else:
        n_ = b.shape[1] if tb else b.shape[2]
    tm, tn, tk = min(tm, m_), min(tn, n_), min(tk, k_)
    if b_by_chip:
        tn, tk = (tn, min(tk, b.shape[3])) if tb else (min(tn, b.shape[3]), tk)
        per_chip = b.shape[3] // (tk if tb else tn)
    nk, n_res, n_rows, n_out = k_ // tk, len(res), len(rows), len(out_dtypes)
    n_in = 2 + n_res + n_rows
    dims = (((0 if ta else 1,), (1 if tb else 0,)), ((), ()))
    if epi is None:
        epi = lambda acc: (acc,)

    def body(*refs):
        a_ref, b_ref = refs[:2]
        res_refs, out_refs, acc_ref = refs[2:n_in], refs[n_in:n_in + n_out], refs[-1]
        row_out_refs = refs[n_in + n_out:-1]
        k = pl.program_id(3)

        @pl.when(k == 0)
        def _():
            acc_ref[...] = jnp.zeros_like(acc_ref)

        acc_ref[...] += lax.dot_general(a_ref[...].astype(BF16), b_ref[...].astype(BF16), dims,
                                        preferred_element_type=F32)

        @pl.when(k == nk - 1)
        def _():
            vals = epi(acc_ref[...], *[r[...] for r in res_refs])
            for o, v in zip(out_refs, vals):
                o[...] = v.astype(o.dtype)
            for o, v in zip(row_out_refs, vals[n_out:]):
                o[...] = jnp.broadcast_to(v, o.shape)

    a_spec = (pl.BlockSpec((None, tk, tm), lambda g, i, j, k: (g, k, i)) if ta
              else pl.BlockSpec((None, tm, tk), lambda g, i, j, k: (g, i, k)))
    if b_by_chip and tb:
        b_spec = pl.BlockSpec((None, None, tn, tk), lambda g, i, j, k: (g, k // per_chip, j, k % per_chip))
    elif b_by_chip:
        b_spec = pl.BlockSpec((None, None, tk, tn), lambda g, i, j, k: (g, chips[0] + j // per_chip, k, j % per_chip))
    elif tb:
        b_spec = pl.BlockSpec((None, tn, tk), lambda g, i, j, k: (g, j, k))
    else:
        b_spec = pl.BlockSpec((None, tk, tn), lambda g, i, j, k: (g, k, j))
    res_specs = [pl.BlockSpec((None, tm, tn), lambda g, i, j, k, off=off: (g, i, j + off)) for _, off in res]
    row_specs = [pl.BlockSpec((None, 1, tn), lambda g, i, j, k: (g, 0, j))] * n_rows
    out_spec = pl.BlockSpec((None, tm, tn), lambda g, i, j, k: (g, i, j))
    row_out_spec = pl.BlockSpec((None, ROW_OUT_SUBLANES, tn), lambda g, i, j, k: (g, i, j))
    outs = pl.pallas_call(
        body, name=name, grid=(g_, m_ // tm, n_ // tn, nk),
        in_specs=[a_spec, b_spec, *res_specs, *row_specs], out_specs=[out_spec] * n_out + [row_out_spec] * n_row_outs,
        out_shape=[jax.ShapeDtypeStruct((g_, m_, n_), d) for d in out_dtypes]
        + [jax.ShapeDtypeStruct((g_, ROW_OUT_SUBLANES * (m_ // tm), n_), F32)] * n_row_outs,
        scratch_shapes=[pltpu.VMEM((tm, tn), F32)],
        compiler_params=_params("parallel", "parallel", "parallel", "arbitrary"),
    )(a, b, *[r for r, _ in res], *rows)
    return outs


def _mm2(name, a, b, *, res=(), rows=(), **kw):
    outs = _mm(name, a[None], b[None], res=[(r[None], off) for r, off in res], rows=[r[None] for r in rows], **kw)
    return [o[0] for o in outs]


def _sum_row_out(row_out):
    return jnp.sum(row_out[::ROW_OUT_SUBLANES], axis=0, keepdims=True)


@jax.custom_vjp
def _bdot(a, b):
    return jnp.dot(a.astype(BF16), b.astype(BF16), preferred_element_type=F32)


def _bdot_fwd(a, b):
    return _bdot(a, b), (a, b)


def _bdot_bwd(saved, g):
    a, b = saved
    gb = g.astype(BF16)
    da = lax.dot_general(gb, b.astype(BF16), (((1,), (1,)), ((), ())), preferred_element_type=F32)
    db = lax.dot_general(a.astype(BF16), gb, (((0,), (0,)), ((), ())), preferred_element_type=F32)
    return da.astype(a.dtype), db.astype(b.dtype)


_bdot.defvjp(_bdot_fwd, _bdot_bwd)


def _rms(x, g):
    return x * lax.rsqrt(jnp.mean(x * x, axis=-1, keepdims=True) + EPS) * g


def _layer_norm(x, g, b):
    mu = jnp.mean(x, axis=-1, keepdims=True)
    var = jnp.mean(jnp.square(x - mu), axis=-1, keepdims=True)
    return (x - mu) * lax.rsqrt(var + EPS) * g + b


def _glu_and_gate_fn(a, a_gate, cg, hx):
    return a * jax.nn.sigmoid(a_gate), cg * hx


def _branches_fn(ca, pooled, cc, bg, u, v, conf_b, conf_g, conf_beta, pool_scale, gm_g, gm_b, *mats):
    pool_w, ws, bs = mats[0:4], mats[4:8], mats[8:12]
    ya = jax.nn.silu(_layer_norm(ca + conf_b, conf_g, conf_beta))
    yb = jnp.concatenate([_bdot(pooled[:, g * GROUP:(g + 1) * GROUP], pool_w[g]) for g in range(4)], axis=1) * pool_scale
    yc = bg * cc
    vln = _layer_norm(v, gm_g, gm_b)
    causal = lax.broadcasted_iota(jnp.int32, (GROUP, GROUP), 0) >= lax.broadcasted_iota(jnp.int32, (GROUP, GROUP), 1)
    sg = jnp.concatenate(
        [_bdot(jnp.where(causal, ws[g], 0.0), vln[:, g * GROUP:(g + 1) * GROUP]) + bs[g] for g in range(4)], axis=1)
    yd = u * sg
    return ya, yb, yc, yd


def _branches_bwd_fn(*args):
    ins, dys = args[:6] + args[10:], args[6:10]
    _, vjp = jax.vjp(_branches_fn, *ins)
    grads = vjp(tuple(dys))
    return grads


def _glu_bwd_fn(a, a_gate, cg, hx, dga, dchx):
    _, vjp = jax.vjp(_glu_and_gate_fn, a, a_gate, cg, hx)
    return vjp((dga, dchx))


def _rms_bwd_epilogue(dh, x, dres, g):
    _, vjp = jax.vjp(_rms, x, g)
    dx, dg = vjp(dh)
    return dx + dres, dg


def _residual_then_rms(acc, residual, g):
    x = acc + residual
    return x, _rms(x, g)


def _loss_fn(x, target, g):
    def f(x, g):
        err = jnp.square(_rms(x, g) - target)
        return 0.5 * jnp.sum(jnp.mean(err, axis=-1, keepdims=True), axis=0, keepdims=True)

    loss, vjp = jax.vjp(f, x, g)
    dx, dg = vjp(jnp.ones((1, 1), F32))
    return dx, dg, jnp.broadcast_to(loss, (1, PACK_LANES))


def _adamw_fn(w, g, m, v):
    m = ADAM_B1 * m + (1.0 - ADAM_B1) * g
    v = ADAM_B2 * v + (1.0 - ADAM_B2) * jnp.square(g)
    m_hat = m / (1.0 - ADAM_B1 ** ADAM_STEP)
    v_hat = v / (1.0 - ADAM_B2 ** ADAM_STEP)
    delta = -ADAM_LR * (m_hat / (jnp.sqrt(v_hat) + ADAM_EPS) + ADAM_WD * w)
    return delta, m, v


def _group_pick(j):
    return [(j == g).astype(F32) for g in range(len(POOL_WINDOWS))]


def _accumulate_row(ref, row, value, first):
    @pl.when(first)
    def _():
        ref[pl.ds(row, 1), :] = value

    @pl.when(jnp.logical_not(first))
    def _():
        ref[pl.ds(row, 1), :] += value


SUBLANES = 8


def _causal_taps(x, taps):
    for r in range(min(SUBLANES, taps)):
        behind = x if r == 0 else pltpu.roll(x, r, 0)
        for q in range((taps - 1 - r) // SUBLANES + 1):
            start = HALO - SUBLANES * q
            yield SUBLANES * q + r, behind[start:start + SEQ_TILE, :]


def _rows_iota(n):
    return lax.broadcasted_iota(jnp.int32, (n, 1), 0)


def _seq_specs(n_time, causal):
    per = SEQ_TILE // HALO
    tile = pl.BlockSpec((SEQ_TILE, GROUP), lambda j, i: (i, j))
    if causal:
        halo = pl.BlockSpec((HALO, GROUP), lambda j, i: (jnp.maximum(i * per - 1, 0), j))
    else:
        halo = pl.BlockSpec((HALO, GROUP), lambda j, i: (jnp.minimum((i + 1) * per, n_time * per - 1), j))
    return tile, halo


def _seq_fwd(ga, proj, chx, conf_dw, sc_conv):
    s_len = ga.shape[0]
    n_time = s_len // SEQ_TILE
    ext = HALO + SEQ_TILE

    def body(ga_ref, ga_h, pin_ref, pin_h, chx_ref, chx_h, wc_ref, ws_ref, ca_ref, po_ref, cc_ref):
        j, i = pl.program_id(0), pl.program_id(1)
        keep = (i > 0).astype(F32)

        def extended(cur, halo):
            return jnp.concatenate([halo[...] * keep, cur[...]], axis=0)

        def conv(x, w_ref, taps):
            acc = None
            for d, rows in _causal_taps(x, taps):
                term = rows * w_ref[pl.ds(taps - 1 - d, 1), :]
                acc = term if acc is None else acc + term
            return acc

        ca_ref[...] = conv(extended(ga_ref, ga_h), wc_ref, CONF_KERNEL)
        cc_ref[...] = conv(extended(chx_ref, chx_h), ws_ref, SC_KERNEL)
        u = extended(pin_ref, pin_h)
        s1 = u + pltpu.roll(u, 1, 0)
        s2 = s1 + pltpu.roll(s1, 2, 0)
        s3 = s2 + pltpu.roll(s2, 4, 0)
        s4 = s3 + pltpu.roll(s3, 8, 0)
        pick = _group_pick(j)
        window_sum = s1 * pick[0] + s2 * pick[1] + s3 * pick[2] + s4 * pick[3]
        window = sum(float(w) * f for w, f in zip(POOL_WINDOWS, pick))
        pos = (_rows_iota(ext) + (i * SEQ_TILE - HALO + 1)).astype(F32)
        count = jnp.maximum(jnp.minimum(pos, window), 1.0)
        po_ref[...] = (window_sum / count - u)[HALO:, :]

    tile, halo = _seq_specs(n_time, causal=True)
    pin_tile = pl.BlockSpec((SEQ_TILE, GROUP), lambda j, i: (i, 2 * (MIX // GROUP) + j))
    pin_halo = pl.BlockSpec((HALO, GROUP), lambda j, i: (jnp.maximum(i * (SEQ_TILE // HALO) - 1, 0), 2 * (MIX // GROUP) + j))
    w_spec = lambda taps: pl.BlockSpec((taps, GROUP), lambda j, i: (0, j))
    return pl.pallas_call(
        body, name="seq_fwd", grid=(MIX // GROUP, n_time),
        in_specs=[tile, halo, pin_tile, pin_halo, tile, halo, w_spec(CONF_KERNEL), w_spec(SC_KERNEL)],
        out_specs=[tile] * 3, out_shape=[jax.ShapeDtypeStruct((s_len, MIX), F32)] * 3,
        compiler_params=_params("parallel", "arbitrary"),
    )(ga, ga, proj, proj, chx, chx, conf_dw, sc_conv)


def _seq_bwd(dca, dpooled, dcc, ga, chx, conf_dw, sc_conv):
    s_len = ga.shape[0]
    n_time = s_len // SEQ_TILE
    ext = HALO + SEQ_TILE

    def body(dca_ref, dca_h, dpo_ref, dpo_h, dcc_ref, dcc_h, ga_ref, ga_h, chx_ref, chx_h, wc_ref, ws_ref,
             dga_ref, dpin_ref, dchx_ref, dwc_ref, dws_ref):
        j, i = pl.program_id(0), pl.program_id(1)
        keep_prev = (i > 0).astype(F32)
        keep_next = (i < n_time - 1).astype(F32)

        def with_next(cur, halo):
            return jnp.concatenate([cur[...], halo[...] * keep_next], axis=0)

        def with_prev(cur, halo):
            return jnp.concatenate([halo[...] * keep_prev, cur[...]], axis=0)

        def conv_t(dy, w_ref, taps):
            acc = None
            for r in range(min(SUBLANES, taps)):
                ahead = dy if r == 0 else pltpu.roll(dy, ext - r, 0)
                for q in range((taps - 1 - r) // SUBLANES + 1):
                    term = ahead[SUBLANES * q:SUBLANES * q + SEQ_TILE, :] * w_ref[pl.ds(taps - 1 - (SUBLANES * q + r), 1), :]
                    acc = term if acc is None else acc + term
            return acc

        def tap_grads(dy_cur, x, dw_ref, taps):
            for d, rows in _causal_taps(x, taps):
                part = jnp.sum(dy_cur * rows, axis=0, keepdims=True)
                _accumulate_row(dw_ref, taps - 1 - d, part, i == 0)

        dga_ref[...] = conv_t(with_next(dca_ref, dca_h), wc_ref, CONF_KERNEL)
        dchx_ref[...] = conv_t(with_next(dcc_ref, dcc_h), ws_ref, SC_KERNEL)
        tap_grads(dca_ref[...], with_prev(ga_ref, ga_h), dwc_ref, CONF_KERNEL)
        tap_grads(dcc_ref[...], with_prev(chx_ref, chx_h), dws_ref, SC_KERNEL)

        dpo = with_next(dpo_ref, dpo_h)
        pick = _group_pick(j)
        window = sum(float(w) * f for w, f in zip(POOL_WINDOWS, pick))
        pos = (_rows_iota(ext) + (i * SEQ_TILE + 1)).astype(F32)
        r = dpo / jnp.minimum(pos, window)
        q1 = r + pltpu.roll(r, ext - 1, 0)
        q2 = q1 + pltpu.roll(q1, ext - 2, 0)
        q3 = q2 + pltpu.roll(q2, ext - 4, 0)
        q4 = q3 + pltpu.roll(q3, ext - 8, 0)
        q = q1 * pick[0] + q2 * pick[1] + q3 * pick[2] + q4 * pick[3]
        dpin_ref[...] = (q - dpo)[:SEQ_TILE, :].astype(dpin_ref.dtype)

    tile, nxt = _seq_specs(n_time, causal=False)
    _, prv = _seq_specs(n_time, causal=True)
    w_spec = lambda taps: pl.BlockSpec((taps, GROUP), lambda j, i: (0, j))
    return pl.pallas_call(
        body, name="seq_bwd", grid=(MIX // GROUP, n_time),
        in_specs=[tile, nxt, tile, nxt, tile, nxt, tile, prv, tile, prv, w_spec(CONF_KERNEL), w_spec(SC_KERNEL)],
        out_specs=[tile, tile, tile, w_spec(CONF_KERNEL), w_spec(SC_KERNEL)],
        out_shape=[jax.ShapeDtypeStruct((s_len, MIX), F32), jax.ShapeDtypeStruct((s_len, MIX), BF16),
                   jax.ShapeDtypeStruct((s_len, MIX), F32), jax.ShapeDtypeStruct(conf_dw.shape, F32),
                   jax.ShapeDtypeStruct(sc_conv.shape, F32)],
        compiler_params=_params("parallel", "arbitrary"),
    )(dca, dca, dpooled, dpooled, dcc, dcc, ga, ga, chx, chx, conf_dw, sc_conv)


MERGE_TILE = 512
MERGE_ROWS = 1024


def _merge_fwd(y, w_branch, gates):
    s_len, d_model = y.shape[1], w_branch.shape[2]
    t = min(MERGE_ROWS, s_len)
    per_branch = d_model // MERGE_TILE

    def body(y_ref, w_ref, gate_ref, out_ref, acc_ref):
        k = pl.program_id(2)

        @pl.when(k == 0)
        def _():
            acc_ref[...] = jnp.zeros_like(acc_ref)

        z = jnp.dot(y_ref[...], w_ref[...], preferred_element_type=F32)
        acc_ref[...] += gate_ref[...].astype(F32) * z

        @pl.when(k == 3)
        def _():
            out_ref[...] = acc_ref[...].astype(out_ref.dtype)

    return pl.pallas_call(
        body, name="merge_fwd", grid=(s_len // t, per_branch, 4),
        in_specs=[pl.BlockSpec((None, t, MIX), lambda i, j, k: (k, i, 0)),
                  pl.BlockSpec((None, MIX, MERGE_TILE), lambda i, j, k: (k, 0, j)),
                  pl.BlockSpec((t, MERGE_TILE), lambda i, j, k: (i, k * per_branch + j))],
        out_specs=pl.BlockSpec((t, MERGE_TILE), lambda i, j, k: (i, j)),
        out_shape=jax.ShapeDtypeStruct((s_len, d_model), BF16),
        scratch_shapes=[pltpu.VMEM((t, MERGE_TILE), F32)],
        compiler_params=_params("parallel", "parallel", "arbitrary"),
    )(y, w_branch, gates)


def _merge_bwd(y, w_branch, gates, dmerged):
    s_len, d_model = y.shape[1], w_branch.shape[2]
    t = min(MERGE_ROWS, s_len)
    per_branch = d_model // MERGE_TILE

    def body(y_ref, w_ref, gate_ref, dm_ref, dz_ref, dgate_ref):
        z = jnp.dot(y_ref[...], w_ref[...], preferred_element_type=F32)
        gate = gate_ref[...].astype(F32)
        dm = dm_ref[...]
        dz_ref[...] = (dm * gate).astype(dz_ref.dtype)
        dgate_ref[...] = (dm * z * gate * (1.0 - gate)).astype(dgate_ref.dtype)

    return pl.pallas_call(
        body, name="merge_bwd", grid=(s_len // t, per_branch, 4),
        in_specs=[pl.BlockSpec((None, t, MIX), lambda i, j, k: (k, i, 0)),
                  pl.BlockSpec((None, MIX, MERGE_TILE), lambda i, j, k: (k, 0, j)),
                  pl.BlockSpec((t, MERGE_TILE), lambda i, j, k: (i, k * per_branch + j)),
                  pl.BlockSpec((t, MERGE_TILE), lambda i, j, k: (i, j))],
        out_specs=[pl.BlockSpec((None, t, MERGE_TILE), lambda i, j, k: (k, i, j)),
                   pl.BlockSpec((t, MERGE_TILE), lambda i, j, k: (i, k * per_branch + j))],
        out_shape=[jax.ShapeDtypeStruct((4, s_len, d_model), BF16), jax.ShapeDtypeStruct((s_len, 4 * d_model), BF16)],
        compiler_params=_params("parallel", "parallel", "arbitrary"),
    )(y, w_branch, gates, dmerged)


def _row(v):
    return v.reshape(1, -1)


def _branch_params(w):
    mats = [w["pool_w"][g] for g in range(4)] + [w["gmlp_ws"][g] for g in range(4)]
    mats += [w["gmlp_bs"][g].reshape(GROUP, 1) for g in range(4)]
    return [_row(w["conf_dw_b"]), _row(w["conf_ln_g"]), _row(w["conf_ln_b"]), _row(w["pool_scale"]),
            _row(w["gmlp_ln_g"]), _row(w["gmlp_ln_b"])] + mats


def _layer_fwd(x0, p_l, w, fetch):
    d_model = x0.shape[1]
    (h1,) = _rowwise("rms_mix", _rms, [(x0, d_model, 0)], [_row(w["norm_mix"])], [(d_model, BF16)])
    w.update(fetch(0, h1))
    (proj,) = _mm2("proj", h1, w["w_in"], b_by_chip=True, chips=(0, 2))
    w.update(fetch(1, proj))
    (gates,) = _mm2("proj_gates", h1, w["w_in"], b_by_chip=True, chips=(2, 2), out_dtypes=(BF16,),
                    epi=lambda acc: (jax.nn.sigmoid(acc),))
    ga, chx = _rowwise("glu", _glu_and_gate_fn, [(proj, MIX, 0), (proj, MIX, 1), (proj, MIX, 4), (proj, MIX, 5)], [],
                       [(MIX, F32), (MIX, F32)])
    ca, pooled, cc = _seq_fwd(ga, proj, chx, w["conf_dw"], w["sc_conv"])
    ys = _rowwise("branches", _branches_fn,
                  [(ca, MIX, 0), (pooled, MIX, 0), (cc, MIX, 0), (proj, MIX, 3), (proj, MIX, 6), (proj, MIX, 7)],
                  _branch_params(w), [(MIX, BF16)] * 4, tm=GROUP)
    y = jnp.stack(ys)
    merged = _merge_fwd(y, w["w_branch"], gates)
    x1, h2 = _mm2("out_proj", merged, w["w_out"], res=[(x0, 0)], rows=[_row(w["norm_mlp"])], out_dtypes=(F32, BF16),
                  epi=_residual_then_rms)
    up, act = _mm2("mlp_up", h2, w["w_up"], b_by_chip=True, out_dtypes=(F32, BF16),
                   epi=lambda acc: (acc, jnp.square(jnp.maximum(acc, 0.0))))
    x2, h3 = _mm2("mlp_down", act, w["w_down"], res=[(x1, 0)], rows=[_row(w["norm_ple"])], out_dtypes=(F32, BF16),
                  epi=_residual_then_rms)
    (pg,) = _mm2("ple_gate", h3, w["w_ple_gate"])
    (x3,) = _mm2("ple", p_l, w["w_ple"], res=[(pg, 0), (x2, 0)],
                 epi=lambda acc, g, r: (acc * jax.nn.sigmoid(g) + r,))
    saved = dict(x0=x0, h1=h1, proj=proj, gates=gates, ga=ga, chx=chx, ca=ca, pooled=pooled, cc=cc, y=y, merged=merged, x1=x1,
                 h2=h2, up=up, act=act, x2=x2, h3=h3, pg=pg)
    return x3, saved


def _layer_bwd(dx3, p_l, w, s, send):
    d_model = dx3.shape[1]
    big, small = {}, {}

    def ple_epi(acc, dx, g):
        sig = jax.nn.sigmoid(g)
        return dx * sig, dx * acc * sig * (1.0 - sig)

    dpp, dpg = _mm2("ple_bwd", p_l, w["w_ple"], res=[(dx3, 0), (s["pg"], 0)], epi=ple_epi, out_dtypes=(BF16, BF16))
    (big["w_ple"],) = _mm2("dw_ple", p_l, dpp, ta=True, out_dtypes=(BF16,))
    (big["w_ple_gate"],) = _mm2("dw_ple_gate", s["h3"], dpg, ta=True, out_dtypes=(BF16,))
    dx2, dg = _mm2("dh_ple", dpg, w["w_ple_gate"], tb=True, res=[(s["x2"], 0), (dx3, 0)], rows=[_row(w["norm_ple"])],
                   epi=_rms_bwd_epilogue, n_row_outs=1)
    small["norm_ple"] = _sum_row_out(dg)

    (dup,) = _mm2("mlp_down_bwd", dx2, w["w_down"], tb=True, res=[(s["up"], 0)], out_dtypes=(BF16,),
                  epi=lambda acc, up: (acc * 2.0 * jnp.maximum(up, 0.0),))
    (big["w_down"],) = _mm2("dw_down", s["act"], dx2, ta=True, out_dtypes=(BF16,))
    (big["w_up"],) = _mm2("dw_up", s["h2"], dup, ta=True, out_dtypes=(BF16,))
    sent = send(0, [big[n] for n in GRAD_GROUPS[0]])
    dx1, dg = _mm2("mlp_up_bwd", dup, w["w_up"], tb=True, b_by_chip=True, res=[(s["x1"], 0), (dx2, 0)],
                   rows=[_row(w["norm_mlp"]) + sent], epi=_rms_bwd_epilogue, n_row_outs=1)
    small["norm_mlp"] = _sum_row_out(dg)

    (dmerged,) = _mm2("out_proj_bwd", dx1, w["w_out"], tb=True)
    (big["w_out"],) = _mm2("dw_out", s["merged"], dx1, ta=True, out_dtypes=(BF16,))
    dz, dgate = _merge_bwd(s["y"], w["w_branch"], s["gates"], dmerged)
    (dy,) = _mm("branch_bwd", dz, w["w_branch"], tb=True)
    (big["w_branch"],) = _mm("dw_branch", s["y"], dz, ta=True, out_dtypes=(BF16,))

    proj = s["proj"]
    params = _branch_params(w)
    grads = _rowwise(
        "branches_bwd", _branches_bwd_fn,
        [(s["ca"], MIX, 0), (s["pooled"], MIX, 0), (s["cc"], MIX, 0), (proj, MIX, 3), (proj, MIX, 6), (proj, MIX, 7)]
        + [(dy, MIX, 0, g) for g in range(4)], params,
        [(MIX, F32), (MIX, F32), (MIX, F32), (MIX, BF16), (MIX, BF16), (MIX, BF16)], [p.shape for p in params], tm=GROUP)
    dca, dpooled, dcc, dbg, du, dv = grads[:6]
    pg_ = grads[6:]
    small["conf_dw_b"], small["conf_ln_g"], small["conf_ln_b"], small["pool_scale"] = pg_[0], pg_[1], pg_[2], pg_[3]
    small["gmlp_ln_g"], small["gmlp_ln_b"] = pg_[4], pg_[5]
    small["pool_w"] = jnp.stack(pg_[6:10])
    small["gmlp_ws"] = jnp.stack(pg_[10:14])
    small["gmlp_bs"] = jnp.stack([b.reshape(GROUP) for b in pg_[14:18]])

    dga, dpin, dchx, small["conf_dw"], small["sc_conv"] = _seq_bwd(dca, dpooled, dcc, s["ga"], s["chx"], w["conf_dw"], w["sc_conv"])
    da, dag, dcg, dhx = _rowwise("glu_bwd", _glu_bwd_fn,
                                 [(proj, MIX, 0), (proj, MIX, 1), (proj, MIX, 4), (proj, MIX, 5), (dga, MIX, 0), (dchx, MIX, 0)],
                                 [], [(MIX, BF16)] * 4)
    dproj = jnp.concatenate([da, dag, dpin, dbg, dcg, dhx, du, dv, dgate], axis=1)
    (big["w_in"],) = _mm2("dw_in", s["h1"], dproj, ta=True, out_dtypes=(BF16,))
    sent = send(1, [big[n] for n in GRAD_GROUPS[1]])
    dx0, dg = _mm2("proj_bwd", dproj, w["w_in"], tb=True, b_by_chip=True, res=[(s["x0"], 0), (dx1, 0)],
                   rows=[_row(w["norm_mix"]) + sent], epi=_rms_bwd_epilogue, n_row_outs=1)
    small["norm_mix"] = _sum_row_out(dg)
    return dx0, small


def _position():
    return lax.axis_index("x"), lax.axis_index("y"), lax.axis_index("c")


def _other_chips(x, y):
    return [(1 - x, y), (x, 1 - y), (1 - x, 1 - y)]


def _region(ref, cuts):
    return ref.at[tuple(pl.ds(*cuts[a]) if a in cuts else slice(None) for a in range(len(ref.shape)))]


def _allgather_small(name, block):
    m_per, n = block.shape

    def body(x_ref, out_ref, send_sems, recv_sems, local_sem):
        x, y, c = _position()
        me, sibling = (x, y, c), (x, y, 1 - c)
        chips = _other_chips(x, y)

        def rows(px, py, pc):
            return out_ref.at[pl.ds((4 * px + 2 * py + pc) * m_per, m_per), :]

        def copy(k, blk, to, src=None):
            return pltpu.make_async_remote_copy(
                src_ref=rows(*blk) if src is None else src, dst_ref=rows(*blk), send_sem=send_sems.at[k],
                recv_sem=recv_sems.at[k], device_id=to, device_id_type=pl.DeviceIdType.MESH)

        mine = pltpu.make_async_copy(x_ref, rows(*me), local_sem)
        mine.start()
        first = [copy(0, me, sibling, src=x_ref)]
        first += [copy(1 + j, me, (*chip, c), src=x_ref) for j, chip in enumerate(chips)]
        for cp in first:
            cp.start()
        passed = [copy(4 + j, (*chip, c), sibling) for j, chip in enumerate(chips)]
        for j, chip in enumerate(chips):
            copy(1 + j, (*chip, c), me).wait_recv()
            passed[j].start()
        copy(0, sibling, me).wait_recv()
        for j, chip in enumerate(chips):
            copy(4 + j, (*chip, 1 - c), me).wait_recv()
        for cp in first + passed:
            cp.wait_send()
        mine.wait()

    return pl.pallas_call(
        body, name=name, out_shape=jax.ShapeDtypeStruct((8 * m_per, n), block.dtype), in_specs=[ANY], out_specs=ANY,
        scratch_shapes=[pltpu.SemaphoreType.DMA((7,)), pltpu.SemaphoreType.DMA((7,)), pltpu.SemaphoreType.DMA],
    )(block)


def _cuts(arr_shape, shard_axis, half_axis, shard=None, half=None):
    cuts = {}
    if shard is not None:
        n = arr_shape[shard_axis] // 4
        cuts[shard_axis] = (shard * n, n)
    if half is not None:
        n = arr_shape[half_axis] // 2
        cuts[half_axis] = (half * n, n)
    return cuts


def _full_shape(shard_shape, shard_axis):
    return tuple(4 * n if a == shard_axis else n for a, n in enumerate(shard_shape))


HBM = pl.BlockSpec(memory_space=pltpu.HBM)
SEM = pl.BlockSpec(memory_space=pltpu.SEMAPHORE)
DATAFLOW_EFFECT = pltpu.SideEffectType.DATAFLOW_SIDE_EFFECTING
TOKEN_SHAPE = (8, 128)


def _remote(src, dst, send_sems, recv_sems, k, peer):
    return pltpu.make_async_remote_copy(src_ref=src, dst_ref=dst, send_sem=send_sems.at[k], recv_sem=recv_sems.at[k],
                                        device_id=peer, device_id_type=pl.DeviceIdType.MESH)


def _split_start(name, sources, landings, plan):
    n_s, n_l = len(sources), len(landings)
    n_copies = len(plan([None] * n_s, [None] * n_l, None))

    def body(*refs):
        src, land = refs[:n_s], refs[n_s:n_s + n_l]
        send_sems, recv_sems, token = refs[n_s + n_l], refs[n_s + n_l + 1], refs[-1]
        for k, (s, d, peer) in enumerate(plan(src, land, _position())):
            _remote(s, d, send_sems, recv_sems, k, peer).start()
        token[...] = jnp.zeros_like(token)

    arrays = [pltpu.with_memory_space_constraint(a, pltpu.HBM) for a in (*sources, *landings)]
    outs = pl.pallas_call(
        body, name=name,
        out_shape=(pltpu.SemaphoreType.DMA((n_copies,)), pltpu.SemaphoreType.DMA((n_copies,)),
                   *[pltpu.HBM(a.shape, a.dtype) for a in arrays], jax.ShapeDtypeStruct(TOKEN_SHAPE, F32)),
        in_specs=[HBM] * (n_s + n_l),
        out_specs=(SEM, SEM, *[HBM] * (n_s + n_l), pl.BlockSpec(memory_space=pltpu.VMEM)),
        input_output_aliases={i: 2 + i for i in range(n_s + n_l)},
        compiler_params=pltpu.CompilerParams(has_side_effects=DATAFLOW_EFFECT),
    )(*arrays)
    return outs[0], outs[1], list(outs[2:2 + n_s]), list(outs[2 + n_s:2 + n_s + n_l]), outs[-1]


def _split_wait(name, started, after, plan):
    send_sems, recv_sems, sources, landings, _ = started
    n_s, n_l = len(sources), len(landings)

    def body(*refs):
        src, land = refs[:n_s], refs[n_s:n_s + n_l]
        send_sems_ref, recv_sems_ref = refs[n_s + n_l], refs[n_s + n_l + 1]
        for k, (s, d, peer) in enumerate(plan(src, land, _position())):
            cp = _remote(s, d, send_sems_ref, recv_sems_ref, k, peer)
            cp.wait_send()
            cp.wait_recv()

    outs = pl.pallas_call(
        body, name=name, out_shape=tuple(pltpu.HBM(a.shape, a.dtype) for a in (*sources, *landings)),
        in_specs=[*[HBM] * (n_s + n_l), SEM, SEM, ANY], out_specs=tuple([HBM] * (n_s + n_l)),
        input_output_aliases={i: i for i in range(n_s + n_l)},
        compiler_params=pltpu.CompilerParams(has_side_effects=DATAFLOW_EFFECT),
    )(*sources, *landings, send_sems, recv_sems, after)
    return list(outs[:n_s]), list(outs[n_s:])


def _exchange(name, inputs, out_shapes, plan, in_place=False, local_plan=None):
    n_in = len(inputs)
    n_out = n_in if in_place else len(out_shapes)
    n_copies = len(plan([None] * n_in, [None] * n_out, None))
    n_local = len(local_plan([None] * n_in, [None] * n_out, None)) if local_plan else 0

    def body(*refs):
        ins, outs = refs[:n_in], refs[n_in:n_in + n_out]
        send_sems, recv_sems, local_sems = refs[n_in + n_out:]
        outs = ins if in_place else outs
        local = [pltpu.make_async_copy(s_, d_, local_sems.at[k])
                 for k, (s_, d_) in enumerate(local_plan(ins, outs, _position()) if local_plan else [])]
        for cp in local:
            cp.start()
        copies = [_remote(s_, d_, send_sems, recv_sems, k, peer) for k, (s_, d_, peer) in enumerate(plan(ins, outs, _position()))]
        for cp in copies:
            cp.start()
        for cp in copies:
            cp.wait()
        for cp in local:
            cp.wait()

    if in_place:
        out_shapes = [jax.ShapeDtypeStruct(a.shape, a.dtype) for a in inputs]
    return pl.pallas_call(
        body, name=name, in_specs=[ANY] * n_in, out_specs=[ANY] * n_out, out_shape=out_shapes,
        input_output_aliases={i: i for i in range(n_in)} if in_place else {},
        scratch_shapes=[pltpu.SemaphoreType.DMA((n_copies,)), pltpu.SemaphoreType.DMA((n_copies,)),
                        pltpu.SemaphoreType.DMA((max(n_local, 1),))],
    )(*inputs)


def _half_rows(ref, half):
    n = ref.shape[0] // 2
    return ref.at[pl.ds(half * n, n)]


def _gather_plan(sources, landings, pos):
    if pos is None:
        return [None] * (3 * len(sources))
    x, y, c = pos
    copies = []
    for src, land in zip(sources, landings):
        for qx, qy in _other_chips(x, y):
            copies.append((_half_rows(src, c), _half_rows(land.at[2 * x + y], c), (qx, qy, c)))
    return copies


def _sibling_plan(refs, _, pos):
    landings = refs[:len(refs) // 2]
    if pos is None:
        return [None] * (3 * len(landings))
    x, y, c = pos
    copies = []
    for land in landings:
        for qx, qy in _other_chips(x, y):
            region = _half_rows(land.at[2 * qx + qy], c)
            copies.append((region, region, (x, y, 1 - c)))
    return copies


def _own_shard_plan(refs, _, pos):
    n = len(refs) // 2
    if pos is None:
        return [None] * n
    x, y, _ = pos
    return [(shard, land.at[2 * x + y]) for land, shard in zip(refs[:n], refs[n:])]


def _weights_start(tag, shards):
    landings = [lax.empty((4, *s.shape), BF16) for s in shards]
    return _split_start(f"weights_start_{tag}", shards, landings, _gather_plan)


def _weights_finish(tag, names, started, after):
    shards, landings = _split_wait(f"weights_wait_{tag}", started, after, _gather_plan)
    landings = _exchange("weights_to_sibling", [*landings, *shards], None, _sibling_plan, in_place=True,
                         local_plan=_own_shard_plan)[:len(names)]
    full = {}
    for name, land in zip(names, landings):
        sa = BIG_AXES[name][0]
        if name in BY_CHIP:
            full[name] = land
            continue
        axis = sa - 1
        moved = jnp.moveaxis(land, 0, axis)
        full[name] = moved.reshape(*moved.shape[:axis], 4 * moved.shape[axis + 1], *moved.shape[axis + 2:])
    return full


def _half_shape(shape, half_axis):
    return tuple(n // 2 if a == half_axis else n for a, n in enumerate(shape))


def _shard_shape(shape, shard_axis):
    return tuple(n // 4 if a == shard_axis else n for a, n in enumerate(shape))


def _flat2d(a):
    return a.reshape(-1, a.shape[-1])


ELEMENTWISE_TILE_ELEMENTS = 1 << 18


def _row_tile(rows, cols):
    tile = 1
    while rows % (2 * tile) == 0 and 2 * tile * cols <= ELEMENTWISE_TILE_ELEMENTS:
        tile *= 2
    assert tile >= 16 or tile == rows, (rows, cols)
    return tile


def _window_of(arr, axis, parts, index):
    flat = _flat2d(arr)
    rows, cols = flat.shape
    if axis == arr.ndim - 1:
        return flat, (rows, cols // parts), (0, index)
    assert axis == 0, axis
    return flat, (rows // parts, cols), (index, 0)


def _window_sum(name, source, window, index, others, out_dtype):
    w_rows, w_cols = window
    tm = _row_tile(w_rows, w_cols)
    blocks = w_rows // tm
    where = jnp.stack([jnp.asarray(i, jnp.int32) for i in index])

    def body(where_ref, src_ref, *refs):
        acc = src_ref[...].astype(F32)
        for r in refs[:-1]:
            acc = acc + r[...].astype(F32)
        refs[-1][...] = acc.astype(refs[-1].dtype)

    same = pl.BlockSpec((tm, w_cols), lambda i, where_ref: (i, 0))
    grid_spec = pltpu.PrefetchScalarGridSpec(
        num_scalar_prefetch=1, grid=(blocks,),
        in_specs=[pl.BlockSpec((tm, w_cols), lambda i, where_ref: (where_ref[0] * blocks + i, where_ref[1])), *[same] * len(others)],
        out_specs=same)
    return pl.pallas_call(body, name=name, grid_spec=grid_spec, out_shape=jax.ShapeDtypeStruct(window, out_dtype),
                          compiler_params=_params("arbitrary"))(where, source, *others)


def _sum_arrays(name, arrays, out_dtype):
    shape = arrays[0].shape
    flat = [_flat2d(a) for a in arrays]
    rows, cols = flat[0].shape
    tm = _row_tile(rows, cols)

    def fn(*tiles):
        acc = tiles[0].astype(F32)
        for t in tiles[1:]:
            acc = acc + t.astype(F32)
        return acc

    (out,) = _rowwise(name, fn, [(a, cols, 0) for a in flat], [], [(cols, out_dtype)], tm=tm)
    return out.reshape(shape)


def _grads_start(tag, names, grads):
    c = lax.axis_index("c")
    shapes = [g.shape for g in grads]
    axes = [(BIG_AXES[n][0] - 1, BIG_AXES[n][1] - 1) for n in names]

    def to_sibling(ins, outs, pos):
        if pos is None:
            return [None] * len(names)
        px, py, pc = pos
        return [(_region(ins[wi], _cuts(shapes[wi], sa, ha, half=1 - pc)), outs[wi], (px, py, 1 - pc))
                for wi, (sa, ha) in enumerate(axes)]

    landed = _exchange("grads_to_sibling", grads, [jax.ShapeDtypeStruct(_half_shape(s, ha), BF16) for s, (_, ha) in zip(shapes, axes)],
                       to_sibling)
    chip_sums = []
    for wi, name in enumerate(names):
        source, window, index = _window_of(grads[wi], axes[wi][1], 2, c)
        chip_sum = _window_sum("chip_sum_" + name, source, window, index, [_flat2d(landed[wi])], BF16)
        chip_sums.append(chip_sum.reshape(landed[wi].shape))

    def to_chips(sources, landings, pos):
        if pos is None:
            return [None] * (3 * len(names))
        px, py, pc = pos
        copies = []
        for wi, (sa, ha) in enumerate(axes):
            for j, (qx, qy) in enumerate(_other_chips(px, py)):
                piece = _region(sources[wi], _cuts(shapes[wi], sa, ha, shard=2 * qx + qy))
                copies.append((piece, landings[wi].at[j], (qx, qy, pc)))
        return copies

    landings = [lax.empty((3, *_shard_shape(t.shape, sa)), BF16) for t, (sa, _) in zip(chip_sums, axes)]
    return _split_start(f"grads_start_{tag}", chip_sums, landings, to_chips), to_chips


def _grads_finish(tag, names, started, after):
    chip_sums, pieces = _split_wait(f"grads_wait_{tag}", started[0], after, started[1])
    my_shard = 2 * lax.axis_index("x") + lax.axis_index("y")
    reduced = {}
    for name, chip_sum, piece in zip(names, chip_sums, pieces):
        source, window, index = _window_of(chip_sum, BIG_AXES[name][0] - 1, 4, my_shard)
        total = _window_sum("shard_sum_" + name, source, window, index, [_flat2d(piece[j]) for j in range(3)], F32)
        reduced[name] = total.reshape(piece.shape[1:])
    return reduced


def _join_halves(reduced):
    c = lax.axis_index("c")

    def halves_to_sibling(ins, outs, pos):
        if pos is None:
            return [None] * len(BIG)
        px, py, pc = pos
        return [(ins[wi], outs[wi], (px, py, 1 - pc)) for wi in range(len(BIG))]

    from_sibling = _exchange("grad_halves_to_sibling", reduced, [jax.ShapeDtypeStruct(r.shape, F32) for r in reduced],
                             halves_to_sibling)
    out = []
    for wi, (_, _, ha) in enumerate(BIG):
        low = jnp.where(c == 0, reduced[wi], from_sibling[wi])
        high = jnp.where(c == 0, from_sibling[wi], reduced[wi])
        out.append(jnp.concatenate([low, high], axis=ha))
    return out


def _pack(arrays):
    flat = jnp.concatenate([a.reshape(-1).astype(F32) for a in arrays])
    rows = -(-flat.shape[0] // PACK_LANES)
    rows = -(-rows // PACK_ROW_MULTIPLE) * PACK_ROW_MULTIPLE
    return jnp.pad(flat, (0, rows * PACK_LANES - flat.shape[0])).reshape(rows, PACK_LANES)


def _unpack(buf, shapes):
    flat, out, off = buf.reshape(-1), [], 0
    for s in shapes:
        n = 1
        for d in s:
            n *= d
        out.append(flat[off:off + n].reshape(s))
        off += n
    return out


def _sum8(name, gathered, rows):
    def body(in_ref, out_ref):
        acc = in_ref[0].astype(F32)
        for d in range(1, 8):
            acc = acc + in_ref[d].astype(F32)
        out_ref[...] = acc

    return pl.pallas_call(
        body, name=name, grid=(rows // PACK_ROW_MULTIPLE,),
        in_specs=[pl.BlockSpec((8, PACK_ROW_MULTIPLE, PACK_LANES), lambda i: (0, i, 0))],
        out_specs=pl.BlockSpec((PACK_ROW_MULTIPLE, PACK_LANES), lambda i: (i, 0)),
        out_shape=jax.ShapeDtypeStruct((rows, PACK_LANES), F32), compiler_params=_params("arbitrary"),
    )(gathered.reshape(8, rows, PACK_LANES))


def _adamw(name, w, g, m, v):
    shape = w.shape
    flat = [_flat2d(a) for a in (w, g, m, v)]
    rows, cols = flat[0].shape
    tm = _row_tile(rows, cols)
    outs = _rowwise(name, _adamw_fn, [(a, cols, 0) for a in flat], [], [(cols, F32)] * 3, tm=tm)
    return [o.reshape(shape) for o in outs]


WEIGHT_ORDER = ("norm_mix", "w_in", "conf_dw", "conf_dw_b", "conf_ln_g", "conf_ln_b", "pool_w", "pool_scale", "sc_conv",
                "gmlp_ln_g", "gmlp_ln_b", "gmlp_ws", "gmlp_bs", "w_branch", "w_out", "norm_mlp", "w_up", "w_down",
                "norm_ple", "w_ple", "w_ple_gate", "norm_final")


def _local_step(x, p, loss_target, weights, layer_big, full_small_sharded, on_big_grads):
    n_layers = p.shape[0]
    d_model = x.shape[1]
    layer_w, saved = [], []
    h = x
    for l in range(n_layers):
        w = {n: weights[n][l] for n in SMALL_REP if n != "norm_final"}
        w.update({n: full_small_sharded[n][l] for n in SMALL_SHARDED})
        layer_w.append(w)
        h, s = _layer_fwd(h, p[l], w, layer_big(l))
        saved.append(s)
    dx, d_final, loss = _rowwise("loss", _loss_fn, [(h, d_model, 0), (loss_target, d_model, 0)], [_row(weights["norm_final"])],
                                 [(d_model, F32)], [(1, d_model), (1, PACK_LANES)])
    small_grads = {n: [None] * n_layers for n in SMALL_REP + SMALL_SHARDED if n != "norm_final"}
    for l in reversed(range(n_layers)):
        dx, small = _layer_bwd(dx, p[l], layer_w[l], saved[l], on_big_grads(l))
        for n in small:
            small_grads[n][l] = small[n].reshape(weights[n].shape[1:]) if n in SMALL_REP else small[n]
    small_grads = {n: jnp.stack(v) for n, v in small_grads.items()}
    small_grads["norm_final"] = d_final.reshape(-1)
    return loss, dx, small_grads


def kernel(x, p, norm_mix, w_in, conf_dw, conf_dw_b, conf_ln_g, conf_ln_b, pool_w, pool_scale, sc_conv, gmlp_ln_g, gmlp_ln_b, gmlp_ws, gmlp_bs, w_branch, w_out, norm_mlp, w_up, w_down, norm_ple, w_ple, w_ple_gate, norm_final, loss_target, m_norm_mix, m_w_in, m_conf_dw, m_conf_dw_b, m_conf_ln_g, m_conf_ln_b, m_pool_w, m_pool_scale, m_sc_conv, m_gmlp_ln_g, m_gmlp_ln_b, m_gmlp_ws, m_gmlp_bs, m_w_branch, m_w_out, m_norm_mlp, m_w_up, m_w_down, m_norm_ple, m_w_ple, m_w_ple_gate, m_norm_final, v_norm_mix, v_w_in, v_conf_dw, v_conf_dw_b, v_conf_ln_g, v_conf_ln_b, v_pool_w, v_pool_scale, v_sc_conv, v_gmlp_ln_g, v_gmlp_ln_b, v_gmlp_ws, v_gmlp_bs, v_w_branch, v_w_out, v_norm_mlp, v_w_up, v_w_down, v_norm_ple, v_w_ple, v_w_ple_gate, v_norm_final):
    given = dict(locals())
    weights = {n: given[n] for n in WEIGHT_ORDER}
    mom_m = {n: given["m_" + n] for n in WEIGHT_ORDER}
    mom_v = {n: given["v_" + n] for n in WEIGHT_ORDER}
    my_shard = 2 * lax.axis_index("x") + lax.axis_index("y")

    n_layers = p.shape[0]
    shard_shapes = [weights[n].shape for n in SMALL_SHARDED]
    gathered = _allgather_small("allgather_small_weights", _pack([weights[n] for n in SMALL_SHARDED]))
    after_small = gathered[0, 0] * 0.0
    after_small = jnp.where(after_small == 0.0, after_small, 0.0)

    weight_copies = [[None] * len(WEIGHT_GROUPS) for _ in range(n_layers)]
    started = after_small
    for l in range(n_layers):
        for g, names in enumerate(WEIGHT_GROUPS):
            shards = [(weights[n][l] + started if i == 0 else weights[n][l]).astype(BF16) for i, n in enumerate(names)]
            weight_copies[l][g] = _weights_start(f"{l}_{g}", shards)
            started = weight_copies[l][g][4][0, 0]
    per_dev = gathered.reshape(8, -1, PACK_LANES)
    full_small_sharded = {}
    chip_parts = [_unpack(per_dev[2 * s], shard_shapes) for s in range(4)]
    for i, n in enumerate(SMALL_SHARDED):
        full_small_sharded[n] = jnp.concatenate([chip_parts[s][i] for s in range(4)], axis=-1)

    grad_copies = [[None] * len(GRAD_GROUPS) for _ in range(n_layers)]

    def layer_big(l):
        return lambda g, after: _weights_finish(f"{l}_{g}", WEIGHT_GROUPS[g], weight_copies[l][g], after)

    def on_big_grads(l):
        def send(g, group_grads):
            grad_copies[l][g] = _grads_start(f"{l}_{g}", GRAD_GROUPS[g], group_grads)
            return grad_copies[l][g][0][4][0, 0]
        return send

    loss_row, dx, small_grads = _local_step(x[0] + started, p[:, 0], loss_target[0], weights, layer_big, full_small_sharded,
                                            on_big_grads)
    loss = lax.psum(loss_row[0, 0], MESH_AXES)

    halves = [{} for _ in range(n_layers)]
    for l in reversed(range(n_layers)):
        for g, names in enumerate(GRAD_GROUPS):
            halves[l].update(_grads_finish(f"{l}_{g}", names, grad_copies[l][g], dx))
    big_reduced = _join_halves([jnp.stack([halves[l][n] for l in range(n_layers)]) for n, _, _ in BIG])
    grads = {n: g for (n, _, _), g in zip(BIG, big_reduced)}

    small_names = SMALL_REP + SMALL_SHARDED
    packed = _pack([small_grads[n] for n in small_names])
    rows = packed.shape[0]
    summed = _sum8("sum_small_grads", _allgather_small("allgather_small_grads", packed.astype(BF16)), rows)
    small_full_shapes = [weights[n].shape for n in SMALL_REP] + [full_small_sharded[n].shape for n in SMALL_SHARDED]
    for n, g in zip(small_names, _unpack(summed, small_full_shapes)):
        if n in SMALL_SHARDED:
            width = weights[n].shape[-1]
            g = lax.dynamic_slice_in_dim(g, my_shard * width, width, axis=g.ndim - 1)
        grads[n] = g

    delta, new_m, new_v = {}, {}, {}
    for n, _, _ in BIG:
        delta[n], new_m[n], new_v[n] = _adamw("adamw_" + n, weights[n], grads[n], mom_m[n], mom_v[n])
    for group, names in (("rep", SMALL_REP), ("sharded", SMALL_SHARDED)):
        outs = _adamw("adamw_small_" + group, *[_pack([src[n] for n in names]) for src in (weights, grads, mom_m, mom_v)])
        shapes = [weights[n].shape for n in names]
        for dst, buf in zip((delta, new_m, new_v), outs):
            for n, a in zip(names, _unpack(buf, shapes)):
                dst[n] = a

    return (loss, dx[None], *[grads[n] for n in WEIGHT_ORDER], *[delta[n] for n in WEIGHT_ORDER],
            *[new_m[n] for n in WEIGHT_ORDER], *[new_v[n] for n in WEIGHT_ORDER])
```

```python
import functools

import jax
import jax.numpy as jnp
from jax import lax
from jax.experimental import pallas as pl
from jax.experimental.pallas import tpu as pltpu

F32, BF16 = jnp.float32, jnp.bfloat16
EPS = 1e-6
ADAM_LR, ADAM_B1, ADAM_B2, ADAM_EPS, ADAM_WD, ADAM_STEP = 0.001, 0.9, 0.999, 1e-08, 0.01, 10
CONF_KERNEL, SC_KERNEL = 31, 3
POOL_WINDOWS = (2, 4, 8, 16)
MIX = 512
GROUP = 128
HALO = 32
SEQ_TILE = 256
VMEM_LIMIT_BYTES = 48 * 1024 * 1024
MESH_AXES = ("x", "y", "c")
ANY = pl.BlockSpec(memory_space=pl.ANY)

BIG = (("w_in", 2, 1), ("w_branch", 3, 1), ("w_out", 1, 2), ("w_up", 2, 1), ("w_down", 1, 2),
       ("w_ple", 2, 1), ("w_ple_gate", 1, 2))
SMALL_REP = ("norm_mix", "conf_dw_b", "conf_ln_g", "conf_ln_b", "pool_w", "pool_scale", "gmlp_ln_g",
             "gmlp_ln_b", "gmlp_ws", "gmlp_bs", "norm_mlp", "norm_ple", "norm_final")
SMALL_SHARDED = ("conf_dw", "sc_conv")
BY_CHIP = ("w_in", "w_up")
BIG_AXES = {name: (shard_axis, half_axis) for name, shard_axis, half_axis in BIG}
WEIGHT_GROUPS = (("w_in",), ("w_branch", "w_out", "w_up", "w_down", "w_ple_gate", "w_ple"))
GRAD_GROUPS = (("w_ple", "w_ple_gate", "w_down", "w_up"), ("w_out", "w_branch", "w_in"))
PACK_LANES = 128
PACK_ROW_MULTIPLE = 256


def _params(*semantics):
    return pltpu.CompilerParams(dimension_semantics=semantics, vmem_limit_bytes=VMEM_LIMIT_BYTES)


def _accumulate(ref, value, first):
    @pl.when(first)
    def _():
        ref[...] = value.astype(ref.dtype)

    @pl.when(jnp.logical_not(first))
    def _():
        ref[...] += value.astype(ref.dtype)


def _rowwise(name, fn, rows, params, outs, accs=(), tm=256):
    n_rows = rows[0][0].shape[-2]
    tm = min(tm, n_rows)
    n_in, n_out = len(rows) + len(params), len(outs)

    def body(*refs):
        res = fn(*[r[...] for r in refs[:n_in]])
        res = res if isinstance(res, (tuple, list)) else (res,)
        out_refs = refs[n_in:]
        for r, v in zip(out_refs[:n_out], res[:n_out]):
            r[...] = v.astype(r.dtype)
        first = pl.program_id(0) == 0
        for r, v in zip(out_refs[n_out:], res[n_out:]):
            _accumulate(r, v, first)

    in_specs = []
    for row in rows:
        arr, w, cb = row[:3]
        if len(row) == 4:
            in_specs.append(pl.BlockSpec((None, tm, w), lambda i, cb=cb, g=row[3]: (g, i, cb)))
        else:
            in_specs.append(pl.BlockSpec((tm, w), lambda i, cb=cb: (i, cb)))
    for p in params:
        in_specs.append(pl.BlockSpec(p.shape, lambda i, nd=p.ndim: (0,) * nd))
    out_specs = [pl.BlockSpec((tm, w), lambda i: (i, 0)) for w, _ in outs]
    out_specs += [pl.BlockSpec(s, lambda i, nd=len(s): (0,) * nd) for s in accs]
    out_shape = [jax.ShapeDtypeStruct((n_rows, w), d) for w, d in outs]
    out_shape += [jax.ShapeDtypeStruct(s, F32) for s in accs]
    return pl.pallas_call(
        body, name=name, grid=(n_rows // tm,), in_specs=in_specs, out_specs=out_specs, out_shape=out_shape,
        compiler_params=_params("arbitrary"),
    )(*[r[0] for r in rows], *params)


ROW_OUT_SUBLANES = 8


def _mm(name, a, b, *, ta=False, tb=False, b_by_chip=False, chips=(0, 4), res=(), rows=(), epi=None, out_dtypes=(F32,), n_row_outs=0,
        tm=1024, tn=1024, tk=1024):
    g_, m_, k_ = (a.shape[0], a.shape[2], a.shape[1]) if ta else a.shape
    if b_by_chip:
        n_ = b.shape[2] if tb else chips[1] * b.shape[3]
    else:
        n_ = b.shape[1] if tb else b.shape[2]
    tm, tn, tk = min(tm, m_), min(tn, n_), min(tk, k_)
    if b_by_chip:
        tn, tk = (tn, min(tk, b.shape[3])) if tb else (min(tn, b.shape[3]), tk)
        per_chip = b.shape[3] // (tk if tb else tn)
    nk, n_res, n_rows, n_out = k_ // tk, len(res), len(rows), len(out_dtypes)
    n_in = 2 + n_res + n_rows
    dims = (((0 if ta else 1,), (1 if tb else 0,)), ((), ()))
    if epi is None:
        epi = lambda acc: (acc,)

    def body(*refs):
        a_ref, b_ref = refs[:2]
        res_refs, out_refs, acc_ref = refs[2:n_in], refs[n_in:n_in + n_out], refs[-1]
        row_out_refs = refs[n_in + n_out:-1]
        k = pl.program_id(3)

        @pl.when(k == 0)
        def _():
            acc_ref[...] = jnp.zeros_like(acc_ref)

        acc_ref[...] += lax.dot_general(a_ref[...].astype(BF16), b_ref[...].astype(BF16), dims,
                                        preferred_element_type=F32)

        @pl.when(k == nk - 1)
        def _():
            vals = epi(acc_ref[...], *[r[...] for r in res_refs])
            for o, v in zip(out_refs, vals):
                o[...] = v.astype(o.dtype)
            for o, v in zip(row_out_refs, vals[n_out:]):
                o[...] = jnp.broadcast_to(v, o.shape)

    a_spec = (pl.BlockSpec((None, tk, tm), lambda g, i, j, k: (g, k, i)) if ta
              else pl.BlockSpec((None, tm, tk), lambda g, i, j, k: (g, i, k)))
    if b_by_chip and tb:
        b_spec = pl.BlockSpec((None, None, tn, tk), lambda g, i, j, k: (g, k // per_chip, j, k % per_chip))
    elif b_by_chip:
        b_spec = pl.BlockSpec((None, None, tk, tn), lambda g, i, j, k: (g, chips[0] + j // per_chip, k, j % per_chip))
    elif tb:
        b_spec = pl.BlockSpec((None, tn, tk), lambda g, i, j, k: (g, j, k))
    else:
        b_spec = pl.BlockSpec((None, tk, tn), lambda g, i, j, k: (g, k, j))
    res_specs = [pl.BlockSpec((None, tm, tn), lambda g, i, j, k, off=off: (g, i, j + off)) for _, off in res]
    row_specs = [pl.BlockSpec((None, 1, tn), lambda g, i, j, k: (g, 0, j))] * n_rows
    out_spec = pl.BlockSpec((None, tm, tn), lambda g, i, j, k: (g, i, j))
    row_out_spec = pl.BlockSpec((None, ROW_OUT_SUBLANES, tn), lambda g, i, j, k: (g, i, j))
    outs = pl.pallas_call(
        body, name=name, grid=(g_, m_ // tm, n_ // tn, nk),
        in_specs=[a_spec, b_spec, *res_specs, *row_specs], out_specs=[out_spec] * n_out + [row_out_spec] * n_row_outs,
        out_shape=[jax.ShapeDtypeStruct((g_, m_, n_), d) for d in out_dtypes]
        + [jax.ShapeDtypeStruct((g_, ROW_OUT_SUBLANES * (m_ // tm), n_), F32)] * n_row_outs,
        scratch_shapes=[pltpu.VMEM((tm, tn), F32)],
        compiler_params=_params("parallel", "parallel", "parallel", "arbitrary"),
    )(a, b, *[r for r, _ in res], *rows)
    return outs


def _mm2(name, a, b, *, res=(), rows=(), **kw):
    outs = _mm(name, a[None], b[None], res=[(r[None], off) for r, off in res], rows=[r[None] for r in rows], **kw)
    return [o[0] for o in outs]


def _sum_row_out(row_out):
    return jnp.sum(row_out[::ROW_OUT_SUBLANES], axis=0, keepdims=True)


@jax.custom_vjp
def _bdot(a, b):
    return jnp.dot(a.astype(BF16), b.astype(BF16), preferred_element_type=F32)


def _bdot_fwd(a, b):
    return _bdot(a, b), (a, b)


def _bdot_bwd(saved, g):
    a, b = saved
    gb = g.astype(BF16)
    da = lax.dot_general(gb, b.astype(BF16), (((1,), (1,)), ((), ())), preferred_element_type=F32)
    db = lax.dot_general(a.astype(BF16), gb, (((0,), (0,)), ((), ())), preferred_element_type=F32)
    return da.astype(a.dtype), db.astype(b.dtype)


_bdot.defvjp(_bdot_fwd, _bdot_bwd)


def _rms(x, g):
    return x * lax.rsqrt(jnp.mean(x * x, axis=-1, keepdims=True) + EPS) * g


def _layer_norm(x, g, b):
    mu = jnp.mean(x, axis=-1, keepdims=True)
    var = jnp.mean(jnp.square(x - mu), axis=-1, keepdims=True)
    return (x - mu) * lax.rsqrt(var + EPS) * g + b


def _glu_and_gate_fn(a, a_gate, cg, hx):
    return a * jax.nn.sigmoid(a_gate), cg * hx


def _branches_fn(ca, pooled, cc, bg, u, v, conf_b, conf_g, conf_beta, pool_scale, gm_g, gm_b, *mats):
    pool_w, ws, bs = mats[0:4], mats[4:8], mats[8:12]
    ya = jax.nn.silu(_layer_norm(ca + conf_b, conf_g, conf_beta))
    yb = jnp.concatenate([_bdot(pooled[:, g * GROUP:(g + 1) * GROUP], pool_w[g]) for g in range(4)], axis=1) * pool_scale
    yc = bg * cc
    vln = _layer_norm(v, gm_g, gm_b)
    causal = lax.broadcasted_iota(jnp.int32, (GROUP, GROUP), 0) >= lax.broadcasted_iota(jnp.int32, (GROUP, GROUP), 1)
    sg = jnp.concatenate(
        [_bdot(jnp.where(causal, ws[g], 0.0), vln[:, g * GROUP:(g + 1) * GROUP]) + bs[g] for g in range(4)], axis=1)
    yd = u * sg
    return ya, yb, yc, yd


def _branches_bwd_fn(*args):
    ins, dys = args[:6] + args[10:], args[6:10]
    _, vjp = jax.vjp(_branches_fn, *ins)
    grads = vjp(tuple(dys))
    return grads


def _glu_bwd_fn(a, a_gate, cg, hx, dga, dchx):
    _, vjp = jax.vjp(_glu_and_gate_fn, a, a_gate, cg, hx)
    return vjp((dga, dchx))


def _rms_bwd_epilogue(dh, x, dres, g):
    _, vjp = jax.vjp(_rms, x, g)
    dx, dg = vjp(dh)
    return dx + dres, dg


def _residual_then_rms(acc, residual, g):
    x = acc + residual
    return x, _rms(x, g)


def _loss_fn(x, target, g):
    def f(x, g):
        err = jnp.square(_rms(x, g) - target)
        return 0.5 * jnp.sum(jnp.mean(err, axis=-1, keepdims=True), axis=0, keepdims=True)

    loss, vjp = jax.vjp(f, x, g)
    dx, dg = vjp(jnp.ones((1, 1), F32))
    return dx, dg, jnp.broadcast_to(loss, (1, PACK_LANES))


def _adamw_fn(w, g, m, v):
    m = ADAM_B1 * m + (1.0 - ADAM_B1) * g
    v = ADAM_B2 * v + (1.0 - ADAM_B2) * jnp.square(g)
    m_hat = m / (1.0 - ADAM_B1 ** ADAM_STEP)
    v_hat = v / (1.0 - ADAM_B2 ** ADAM_STEP)
    delta = -ADAM_LR * (m_hat / (jnp.sqrt(v_hat) + ADAM_EPS) + ADAM_WD * w)
    return delta, m, v


def _group_pick(j):
    return [(j == g).astype(F32) for g in range(len(POOL_WINDOWS))]


def _accumulate_row(ref, row, value, first):
    @pl.when(first)
    def _():
        ref[pl.ds(row, 1), :] = value

    @pl.when(jnp.logical_not(first))
    def _():
        ref[pl.ds(row, 1), :] += value


SUBLANES = 8


def _causal_taps(x, taps):
    for r in range(min(SUBLANES, taps)):
        behind = x if r == 0 else pltpu.roll(x, r, 0)
        for q in range((taps - 1 - r) // SUBLANES + 1):
            start = HALO - SUBLANES * q
            yield SUBLANES * q + r, behind[start:start + SEQ_TILE, :]


def _rows_iota(n):
    return lax.broadcasted_iota(jnp.int32, (n, 1), 0)


def _seq_specs(n_time, causal):
    per = SEQ_TILE // HALO
    tile = pl.BlockSpec((SEQ_TILE, GROUP), lambda j, i: (i, j))
    if causal:
        halo = pl.BlockSpec((HALO, GROUP), lambda j, i: (jnp.maximum(i * per - 1, 0), j))
    else:
        halo = pl.BlockSpec((HALO, GROUP), lambda j, i: (jnp.minimum((i + 1) * per, n_time * per - 1), j))
    return tile, halo


def _seq_fwd(ga, proj, chx, conf_dw, sc_conv):
    s_len = ga.shape[0]
    n_time = s_len // SEQ_TILE
    ext = HALO + SEQ_TILE

    def body(ga_ref, ga_h, pin_ref, pin_h, chx_ref, chx_h, wc_ref, ws_ref, ca_ref, po_ref, cc_ref):
        j, i = pl.program_id(0), pl.program_id(1)
        keep = (i > 0).astype(F32)

        def extended(cur, halo):
            return jnp.concatenate([halo[...] * keep, cur[...]], axis=0)

        def conv(x, w_ref, taps):
            acc = None
            for d, rows in _causal_taps(x, taps):
                term = rows * w_ref[pl.ds(taps - 1 - d, 1), :]
                acc = term if acc is None else acc + term
            return acc

        ca_ref[...] = conv(extended(ga_ref, ga_h), wc_ref, CONF_KERNEL)
        cc_ref[...] = conv(extended(chx_ref, chx_h), ws_ref, SC_KERNEL)
        u = extended(pin_ref, pin_h)
        s1 = u + pltpu.roll(u, 1, 0)
        s2 = s1 + pltpu.roll(s1, 2, 0)
        s3 = s2 + pltpu.roll(s2, 4, 0)
        s4 = s3 + pltpu.roll(s3, 8, 0)
        pick = _group_pick(j)
        window_sum = s1 * pick[0] + s2 * pick[1] + s3 * pick[2] + s4 * pick[3]
        window = sum(float(w) * f for w, f in zip(POOL_WINDOWS, pick))
        pos = (_rows_iota(ext) + (i * SEQ_TILE - HALO + 1)).astype(F32)
        count = jnp.maximum(jnp.minimum(pos, window), 1.0)
        po_ref[...] = (window_sum / count - u)[HALO:, :]

    tile, halo = _seq_specs(n_time, causal=True)
    pin_tile = pl.BlockSpec((SEQ_TILE, GROUP), lambda j, i: (i, 2 * (MIX // GROUP) + j))
    pin_halo = pl.BlockSpec((HALO, GROUP), lambda j, i: (jnp.maximum(i * (SEQ_TILE // HALO) - 1, 0), 2 * (MIX // GROUP) + j))
    w_spec = lambda taps: pl.BlockSpec((taps, GROUP), lambda j, i: (0, j))
    return pl.pallas_call(
        body, name="seq_fwd", grid=(MIX // GROUP, n_time),
        in_specs=[tile, halo, pin_tile, pin_halo, tile, halo, w_spec(CONF_KERNEL), w_spec(SC_KERNEL)],
        out_specs=[tile] * 3, out_shape=[jax.ShapeDtypeStruct((s_len, MIX), F32)] * 3,
        compiler_params=_params("parallel", "arbitrary"),
    )(ga, ga, proj, proj, chx, chx, conf_dw, sc_conv)


def _seq_bwd(dca, dpooled, dcc, ga, chx, conf_dw, sc_conv):
    s_len = ga.shape[0]
    n_time = s_len // SEQ_TILE
    ext = HALO + SEQ_TILE

    def body(dca_ref, dca_h, dpo_ref, dpo_h, dcc_ref, dcc_h, ga_ref, ga_h, chx_ref, chx_h, wc_ref, ws_ref,
             dga_ref, dpin_ref, dchx_ref, dwc_ref, dws_ref):
        j, i = pl.program_id(0), pl.program_id(1)
        keep_prev = (i > 0).astype(F32)
        keep_next = (i < n_time - 1).astype(F32)

        def with_next(cur, halo):
            return jnp.concatenate([cur[...], halo[...] * keep_next], axis=0)

        def with_prev(cur, halo):
            return jnp.concatenate([halo[...] * keep_prev, cur[...]], axis=0)

        def conv_t(dy, w_ref, taps):
            acc = None
            for r in range(min(SUBLANES, taps)):
                ahead = dy if r == 0 else pltpu.roll(dy, ext - r, 0)
                for q in range((taps - 1 - r) // SUBLANES + 1):
                    term = ahead[SUBLANES * q:SUBLANES * q + SEQ_TILE, :] * w_ref[pl.ds(taps - 1 - (SUBLANES * q + r), 1), :]
                    acc = term if acc is None else acc + term
            return acc

        def tap_grads(dy_cur, x, dw_ref, taps):
            for d, rows in _causal_taps(x, taps):
                part = jnp.sum(dy_cur * rows, axis=0, keepdims=True)
                _accumulate_row(dw_ref, taps - 1 - d, part, i == 0)

        dga_ref[...] = conv_t(with_next(dca_ref, dca_h), wc_ref, CONF_KERNEL)
        dchx_ref[...] = conv_t(with_next(dcc_ref, dcc_h), ws_ref, SC_KERNEL)
        tap_grads(dca_ref[...], with_prev(ga_ref, ga_h), dwc_ref, CONF_KERNEL)
        tap_grads(dcc_ref[...], with_prev(chx_ref, chx_h), dws_ref, SC_KERNEL)

        dpo = with_next(dpo_ref, dpo_h)
        pick = _group_pick(j)
        window = sum(float(w) * f for w, f in zip(POOL_WINDOWS, pick))
        pos = (_rows_iota(ext) + (i * SEQ_TILE + 1)).astype(F32)
        r = dpo / jnp.minimum(pos, window)
        q1 = r + pltpu.roll(r, ext - 1, 0)
        q2 = q1 + pltpu.roll(q1, ext - 2, 0)
        q3 = q2 + pltpu.roll(q2, ext - 4, 0)
        q4 = q3 + pltpu.roll(q3, ext - 8, 0)
        q = q1 * pick[0] + q2 * pick[1] + q3 * pick[2] + q4 * pick[3]
        dpin_ref[...] = (q - dpo)[:SEQ_TILE, :].astype(dpin_ref.dtype)

    tile, nxt = _seq_specs(n_time, causal=False)
    _, prv = _seq_specs(n_time, causal=True)
    w_spec = lambda taps: pl.BlockSpec((taps, GROUP), lambda j, i: (0, j))
    return pl.pallas_call(
        body, name="seq_bwd", grid=(MIX // GROUP, n_time),
        in_specs=[tile, nxt, tile, nxt, tile, nxt, tile, prv, tile, prv, w_spec(CONF_KERNEL), w_spec(SC_KERNEL)],
        out_specs=[tile, tile, tile, w_spec(CONF_KERNEL), w_spec(SC_KERNEL)],
        out_shape=[jax.ShapeDtypeStruct((s_len, MIX), F32), jax.ShapeDtypeStruct((s_len, MIX), BF16),
                   jax.ShapeDtypeStruct((s_len, MIX), F32), jax.ShapeDtypeStruct(conf_dw.shape, F32),
                   jax.ShapeDtypeStruct(sc_conv.shape, F32)],
        compiler_params=_params("parallel", "arbitrary"),
    )(dca, dca, dpooled, dpooled, dcc, dcc, ga, ga, chx, chx, conf_dw, sc_conv)


MERGE_TILE = 512
MERGE_ROWS = 1024


def _merge_fwd(y, w_branch, gates):
    s_len, d_model = y.shape[1], w_branch.shape[2]
    t = min(MERGE_ROWS, s_len)
    per_branch = d_model // MERGE_TILE

    def body(y_ref, w_ref, gate_ref, out_ref, acc_ref):
        k = pl.program_id(2)

        @pl.when(k == 0)
        def _():
            acc_ref[...] = jnp.zeros_like(acc_ref)

        z = jnp.dot(y_ref[...], w_ref[...], preferred_element_type=F32)
        acc_ref[...] += gate_ref[...].astype(F32) * z

        @pl.when(k == 3)
        def _():
            out_ref[...] = acc_ref[...].astype(out_ref.dtype)

    return pl.pallas_call(
        body, name="merge_fwd", grid=(s_len // t, per_branch, 4),
        in_specs=[pl.BlockSpec((None, t, MIX), lambda i, j, k: (k, i, 0)),
                  pl.BlockSpec((None, MIX, MERGE_TILE), lambda i, j, k: (k, 0, j)),
                  pl.BlockSpec((t, MERGE_TILE), lambda i, j, k: (i, k * per_branch + j))],
        out_specs=pl.BlockSpec((t, MERGE_TILE), lambda i, j, k: (i, j)),
        out_shape=jax.ShapeDtypeStruct((s_len, d_model), BF16),
        scratch_shapes=[pltpu.VMEM((t, MERGE_TILE), F32)],
        compiler_params=_params("parallel", "parallel", "arbitrary"),
    )(y, w_branch, gates)


def _merge_bwd(y, w_branch, gates, dmerged):
    s_len, d_model = y.shape[1], w_branch.shape[2]
    t = min(MERGE_ROWS, s_len)
    per_branch = d_model // MERGE_TILE

    def body(y_ref, w_ref, gate_ref, dm_ref, dz_ref, dgate_ref):
        z = jnp.dot(y_ref[...], w_ref[...], preferred_element_type=F32)
        gate = gate_ref[...].astype(F32)
        dm = dm_ref[...]
        dz_ref[...] = (dm * gate).astype(dz_ref.dtype)
        dgate_ref[...] = (dm * z * gate * (1.0 - gate)).astype(dgate_ref.dtype)

    return pl.pallas_call(
        body, name="merge_bwd", grid=(s_len // t, per_branch, 4),
        in_specs=[pl.BlockSpec((None, t, MIX), lambda i, j, k: (k, i, 0)),
                  pl.BlockSpec((None, MIX, MERGE_TILE), lambda i, j, k: (k, 0, j)),
                  pl.BlockSpec((t, MERGE_TILE), lambda i, j, k: (i, k * per_branch + j)),
                  pl.BlockSpec((t, MERGE_TILE), lambda i, j, k: (i, j))],
        out_specs=[pl.BlockSpec((None, t, MERGE_TILE), lambda i, j, k: (k, i, j)),
                   pl.BlockSpec((t, MERGE_TILE), lambda i, j, k: (i, k * per_branch + j))],
        out_shape=[jax.ShapeDtypeStruct((4, s_len, d_model), BF16), jax.ShapeDtypeStruct((s_len, 4 * d_model), BF16)],
        compiler_params=_params("parallel", "parallel", "arbitrary"),
    )(y, w_branch, gates, dmerged)


def _row(v):
    return v.reshape(1, -1)


def _branch_params(w):
    mats = [w["pool_w"][g] for g in range(4)] + [w["gmlp_ws"][g] for g in range(4)]
    mats += [w["gmlp_bs"][g].reshape(GROUP, 1) for g in range(4)]
    return [_row(w["conf_dw_b"]), _row(w["conf_ln_g"]), _row(w["conf_ln_b"]), _row(w["pool_scale"]),
            _row(w["gmlp_ln_g"]), _row(w["gmlp_ln_b"])] + mats


def _layer_fwd(x0, p_l, w, fetch):
    d_model = x0.shape[1]
    (h1,) = _rowwise("rms_mix", _rms, [(x0, d_model, 0)], [_row(w["norm_mix"])], [(d_model, BF16)])
    w.update(fetch(0, h1))
    (proj,) = _mm2("proj", h1, w["w_in"], b_by_chip=True, chips=(0, 2))
    w.update(fetch(1, proj))
    (gates,) = _mm2("proj_gates", h1, w["w_in"], b_by_chip=True, chips=(2, 2), out_dtypes=(BF16,),
                    epi=lambda acc: (jax.nn.sigmoid(acc),))
    ga, chx = _rowwise("glu", _glu_and_gate_fn, [(proj, MIX, 0), (proj, MIX, 1), (proj, MIX, 4), (proj, MIX, 5)], [],
                       [(MIX, F32), (MIX, F32)])
    ca, pooled, cc = _seq_fwd(ga, proj, chx, w["conf_dw"], w["sc_conv"])
    ys = _rowwise("branches", _branches_fn,
                  [(ca, MIX, 0), (pooled, MIX, 0), (cc, MIX, 0), (proj, MIX, 3), (proj, MIX, 6), (proj, MIX, 7)],
                  _branch_params(w), [(MIX, BF16)] * 4, tm=GROUP)
    y = jnp.stack(ys)
    merged = _merge_fwd(y, w["w_branch"], gates)
    x1, h2 = _mm2("out_proj", merged, w["w_out"], res=[(x0, 0)], rows=[_row(w["norm_mlp"])], out_dtypes=(F32, BF16),
                  epi=_residual_then_rms)
    up, act = _mm2("mlp_up", h2, w["w_up"], b_by_chip=True, out_dtypes=(F32, BF16),
                   epi=lambda acc: (acc, jnp.square(jnp.maximum(acc, 0.0))))
    x2, h3 = _mm2("mlp_down", act, w["w_down"], res=[(x1, 0)], rows=[_row(w["norm_ple"])], out_dtypes=(F32, BF16),
                  epi=_residual_then_rms)
    (pg,) = _mm2("ple_gate", h3, w["w_ple_gate"])
    (x3,) = _mm2("ple", p_l, w["w_ple"], res=[(pg, 0), (x2, 0)],
                 epi=lambda acc, g, r: (acc * jax.nn.sigmoid(g) + r,))
    saved = dict(x0=x0, h1=h1, proj=proj, gates=gates, ga=ga, chx=chx, ca=ca, pooled=pooled, cc=cc, y=y, merged=merged, x1=x1,
                 h2=h2, up=up, act=act, x2=x2, h3=h3, pg=pg)
    return x3, saved


def _layer_bwd(dx3, p_l, w, s, send):
    d_model = dx3.shape[1]
    big, small = {}, {}

    def ple_epi(acc, dx, g):
        sig = jax.nn.sigmoid(g)
        return dx * sig, dx * acc * sig * (1.0 - sig)

    dpp, dpg = _mm2("ple_bwd", p_l, w["w_ple"], res=[(dx3, 0), (s["pg"], 0)], epi=ple_epi, out_dtypes=(BF16, BF16))
    (big["w_ple"],) = _mm2("dw_ple", p_l, dpp, ta=True, out_dtypes=(BF16,))
    (big["w_ple_gate"],) = _mm2("dw_ple_gate", s["h3"], dpg, ta=True, out_dtypes=(BF16,))
    dx2, dg = _mm2("dh_ple", dpg, w["w_ple_gate"], tb=True, res=[(s["x2"], 0), (dx3, 0)], rows=[_row(w["norm_ple"])],
                   epi=_rms_bwd_epilogue, n_row_outs=1)
    small["norm_ple"] = _sum_row_out(dg)

    (dup,) = _mm2("mlp_down_bwd", dx2, w["w_down"], tb=True, res=[(s["up"], 0)], out_dtypes=(BF16,),
                  epi=lambda acc, up: (acc * 2.0 * jnp.maximum(up, 0.0),))
    (big["w_down"],) = _mm2("dw_down", s["act"], dx2, ta=True, out_dtypes=(BF16,))
    (big["w_up"],) = _mm2("dw_up", s["h2"], dup, ta=True, out_dtypes=(BF16,))
    sent = send(0, [big[n] for n in GRAD_GROUPS[0]])
    dx1, dg = _mm2("mlp_up_bwd", dup, w["w_up"], tb=True, b_by_chip=True, res=[(s["x1"], 0), (dx2, 0)],
                   rows=[_row(w["norm_mlp"]) + sent], epi=_rms_bwd_epilogue, n_row_outs=1)
    small["norm_mlp"] = _sum_row_out(dg)

    (dmerged,) = _mm2("out_proj_bwd", dx1, w["w_out"], tb=True)
    (big["w_out"],) = _mm2("dw_out", s["merged"], dx1, ta=True, out_dtypes=(BF16,))
    dz, dgate = _merge_bwd(s["y"], w["w_branch"], s["gates"], dmerged)
    (dy,) = _mm("branch_bwd", dz, w["w_branch"], tb=True)
    (big["w_branch"],) = _mm("dw_branch", s["y"], dz, ta=True, out_dtypes=(BF16,))

    proj = s["proj"]
    params = _branch_params(w)
    grads = _rowwise(
        "branches_bwd", _branches_bwd_fn,
        [(s["ca"], MIX, 0), (s["pooled"], MIX, 0), (s["cc"], MIX, 0), (proj, MIX, 3), (proj, MIX, 6), (proj, MIX, 7)]
        + [(dy, MIX, 0, g) for g in range(4)], params,
        [(MIX, F32), (MIX, F32), (MIX, F32), (MIX, BF16), (MIX, BF16), (MIX, BF16)], [p.shape for p in params], tm=GROUP)
    dca, dpooled, dcc, dbg, du, dv = grads[:6]
    pg_ = grads[6:]
    small["conf_dw_b"], small["conf_ln_g"], small["conf_ln_b"], small["pool_scale"] = pg_[0], pg_[1], pg_[2], pg_[3]
    small["gmlp_ln_g"], small["gmlp_ln_b"] = pg_[4], pg_[5]
    small["pool_w"] = jnp.stack(pg_[6:10])
    small["gmlp_ws"] = jnp.stack(pg_[10:14])
    small["gmlp_bs"] = jnp.stack([b.reshape(GROUP) for b in pg_[14:18]])

    dga, dpin, dchx, small["conf_dw"], small["sc_conv"] = _seq_bwd(dca, dpooled, dcc, s["ga"], s["chx"], w["conf_dw"], w["sc_conv"])
    da, dag, dcg, dhx = _rowwise("glu_bwd", _glu_bwd_fn,
                                 [(proj, MIX, 0), (proj, MIX, 1), (proj, MIX, 4), (proj, MIX, 5), (dga, MIX, 0), (dchx, MIX, 0)],
                                 [], [(MIX, BF16)] * 4)
    dproj = jnp.concatenate([da, dag, dpin, dbg, dcg, dhx, du, dv, dgate], axis=1)
    (big["w_in"],) = _mm2("dw_in", s["h1"], dproj, ta=True, out_dtypes=(BF16,))
    sent = send(1, [big[n] for n in GRAD_GROUPS[1]])
    dx0, dg = _mm2("proj_bwd", dproj, w["w_in"], tb=True, b_by_chip=True, res=[(s["x0"], 0), (dx1, 0)],
                   rows=[_row(w["norm_mix"]) + sent], epi=_rms_bwd_epilogue, n_row_outs=1)
    small["norm_mix"] = _sum_row_out(dg)
    return dx0, small


def _position():
    return lax.axis_index("x"), lax.axis_index("y"), lax.axis_index("c")


def _other_chips(x, y):
    return [(1 - x, y), (x, 1 - y), (1 - x, 1 - y)]


def _region(ref, cuts):
    return ref.at[tuple(pl.ds(*cuts[a]) if a in cuts else slice(None) for a in range(len(ref.shape)))]


def _allgather_small(name, block):
    m_per, n = block.shape

    def body(x_ref, out_ref, send_sems, recv_sems, local_sem):
        x, y, c = _position()
        me, sibling = (x, y, c), (x, y, 1 - c)
        chips = _other_chips(x, y)

        def rows(px, py, pc):
            return out_ref.at[pl.ds((4 * px + 2 * py + pc) * m_per, m_per), :]

        def copy(k, blk, to, src=None):
            return pltpu.make_async_remote_copy(
                src_ref=rows(*blk) if src is None else src, dst_ref=rows(*blk), send_sem=send_sems.at[k],
                recv_sem=recv_sems.at[k], device_id=to, device_id_type=pl.DeviceIdType.MESH)

        mine = pltpu.make_async_copy(x_ref, rows(*me), local_sem)
        mine.start()
        first = [copy(0, me, sibling, src=x_ref)]
        first += [copy(1 + j, me, (*chip, c), src=x_ref) for j, chip in enumerate(chips)]
        for cp in first:
            cp.start()
        passed = [copy(4 + j, (*chip, c), sibling) for j, chip in enumerate(chips)]
        for j, chip in enumerate(chips):
            copy(1 + j, (*chip, c), me).wait_recv()
            passed[j].start()
        copy(0, sibling, me).wait_recv()
        for j, chip in enumerate(chips):
            copy(4 + j, (*chip, 1 - c), me).wait_recv()
        for cp in first + passed:
            cp.wait_send()
        mine.wait()

    return pl.pallas_call(
        body, name=name, out_shape=jax.ShapeDtypeStruct((8 * m_per, n), block.dtype), in_specs=[ANY], out_specs=ANY,
        scratch_shapes=[pltpu.SemaphoreType.DMA((7,)), pltpu.SemaphoreType.DMA((7,)), pltpu.SemaphoreType.DMA],
    )(block)


def _cuts(arr_shape, shard_axis, half_axis, shard=None, half=None):
    cuts = {}
    if shard is not None:
        n = arr_shape[shard_axis] // 4
        cuts[shard_axis] = (shard * n, n)
    if half is not None:
        n = arr_shape[half_axis] // 2
        cuts[half_axis] = (half * n, n)
    return cuts


def _full_shape(shard_shape, shard_axis):
    return tuple(4 * n if a == shard_axis else n for a, n in enumerate(shard_shape))


HBM = pl.BlockSpec(memory_space=pltpu.HBM)
SEM = pl.BlockSpec(memory_space=pltpu.SEMAPHORE)
DATAFLOW_EFFECT = pltpu.SideEffectType.DATAFLOW_SIDE_EFFECTING
TOKEN_SHAPE = (8, 128)


def _remote(src, dst, send_sems, recv_sems, k, peer):
    return pltpu.make_async_remote_copy(src_ref=src, dst_ref=dst, send_sem=send_sems.at[k], recv_sem=recv_sems.at[k],
                                        device_id=peer, device_id_type=pl.DeviceIdType.MESH)


def _split_start(name, sources, landings, plan):
    n_s, n_l = len(sources), len(landings)
    n_copies = len(plan([None] * n_s, [None] * n_l, None))

    def body(*refs):
        src, land = refs[:n_s], refs[n_s:n_s + n_l]
        send_sems, recv_sems, token = refs[n_s + n_l], refs[n_s + n_l + 1], refs[-1]
        for k, (s, d, peer) in enumerate(plan(src, land, _position())):
            _remote(s, d, send_sems, recv_sems, k, peer).start()
        token[...] = jnp.zeros_like(token)

    arrays = [pltpu.with_memory_space_constraint(a, pltpu.HBM) for a in (*sources, *landings)]
    outs = pl.pallas_call(
        body, name=name,
        out_shape=(pltpu.SemaphoreType.DMA((n_copies,)), pltpu.SemaphoreType.DMA((n_copies,)),
                   *[pltpu.HBM(a.shape, a.dtype) for a in arrays], jax.ShapeDtypeStruct(TOKEN_SHAPE, F32)),
        in_specs=[HBM] * (n_s + n_l),
        out_specs=(SEM, SEM, *[HBM] * (n_s + n_l), pl.BlockSpec(memory_space=pltpu.VMEM)),
        input_output_aliases={i: 2 + i for i in range(n_s + n_l)},
        compiler_params=pltpu.CompilerParams(has_side_effects=DATAFLOW_EFFECT),
    )(*arrays)
    return outs[0], outs[1], list(outs[2:2 + n_s]), list(outs[2 + n_s:2 + n_s + n_l]), outs[-1]


def _split_wait(name, started, after, plan):
    send_sems, recv_sems, sources, landings, _ = started
    n_s, n_l = len(sources), len(landings)

    def body(*refs):
        src, land = refs[:n_s], refs[n_s:n_s + n_l]
        send_sems_ref, recv_sems_ref = refs[n_s + n_l], refs[n_s + n_l + 1]
        for k, (s, d, peer) in enumerate(plan(src, land, _position())):
            cp = _remote(s, d, send_sems_ref, recv_sems_ref, k, peer)
            cp.wait_send()
            cp.wait_recv()

    outs = pl.pallas_call(
        body, name=name, out_shape=tuple(pltpu.HBM(a.shape, a.dtype) for a in (*sources, *landings)),
        in_specs=[*[HBM] * (n_s + n_l), SEM, SEM, ANY], out_specs=tuple([HBM] * (n_s + n_l)),
        input_output_aliases={i: i for i in range(n_s + n_l)},
        compiler_params=pltpu.CompilerParams(has_side_effects=DATAFLOW_EFFECT),
    )(*sources, *landings, send_sems, recv_sems, after)
    return list(outs[:n_s]), list(outs[n_s:])


def _exchange(name, inputs, out_shapes, plan, in_place=False):
    n_in = len(inputs)
    n_out = n_in if in_place else len(out_shapes)
    n_copies = len(plan([None] * n_in, [None] * n_out, None))

    def body(*refs):
        ins, outs = refs[:n_in], refs[n_in:n_in + n_out]
        send_sems, recv_sems = refs[n_in + n_out:]
        copies = [_remote(s, d, send_sems, recv_sems, k, peer)
                  for k, (s, d, peer) in enumerate(plan(ins, ins if in_place else outs, _position()))]
        for cp in copies:
            cp.start()
        for cp in copies:
            cp.wait()

    if in_place:
        out_shapes = [jax.ShapeDtypeStruct(a.shape, a.dtype) for a in inputs]
    return pl.pallas_call(
        body, name=name, in_specs=[ANY] * n_in, out_specs=[ANY] * n_out, out_shape=out_shapes,
        input_output_aliases={i: i for i in range(n_in)} if in_place else {},
        scratch_shapes=[pltpu.SemaphoreType.DMA((n_copies,)), pltpu.SemaphoreType.DMA((n_copies,))],
    )(*inputs)


def _half_rows(ref, half):
    n = ref.shape[0] // 2
    return ref.at[pl.ds(half * n, n)]


def _gather_plan(sources, landings, pos):
    if pos is None:
        return [None] * (3 * len(sources))
    x, y, c = pos
    copies = []
    for src, land in zip(sources, landings):
        for qx, qy in _other_chips(x, y):
            copies.append((_half_rows(src, c), _half_rows(land.at[2 * x + y], c), (qx, qy, c)))
    return copies


def _sibling_plan(landings, _, pos):
    if pos is None:
        return [None] * (3 * len(landings))
    x, y, c = pos
    copies = []
    for land in landings:
        for qx, qy in _other_chips(x, y):
            region = _half_rows(land.at[2 * qx + qy], c)
            copies.append((region, region, (x, y, 1 - c)))
    return copies


def _weights_start(tag, shards):
    landings = [lax.empty((4, *s.shape), BF16) for s in shards]
    return _split_start(f"weights_start_{tag}", shards, landings, _gather_plan)


def _weights_finish(tag, names, started, after):
    shards, landings = _split_wait(f"weights_wait_{tag}", started, after, _gather_plan)
    landings = _exchange("weights_to_sibling", landings, None, _sibling_plan, in_place=True)
    my_shard = 2 * lax.axis_index("x") + lax.axis_index("y")
    full = {}
    for name, shard, land in zip(names, shards, landings):
        sa = BIG_AXES[name][0]
        land = lax.dynamic_update_index_in_dim(land, shard, my_shard, axis=0)
        if name in BY_CHIP:
            full[name] = land
            continue
        axis = sa - 1
        moved = jnp.moveaxis(land, 0, axis)
        full[name] = moved.reshape(*moved.shape[:axis], 4 * moved.shape[axis + 1], *moved.shape[axis + 2:])
    return full


def _half_shape(shape, half_axis):
    return tuple(n // 2 if a == half_axis else n for a, n in enumerate(shape))


def _shard_shape(shape, shard_axis):
    return tuple(n // 4 if a == shard_axis else n for a, n in enumerate(shape))


def _flat2d(a):
    return a.reshape(-1, a.shape[-1])


ELEMENTWISE_TILE_ELEMENTS = 1 << 18


def _row_tile(rows, cols):
    tile = 1
    while rows % (2 * tile) == 0 and 2 * tile * cols <= ELEMENTWISE_TILE_ELEMENTS:
        tile *= 2
    assert tile >= 16 or tile == rows, (rows, cols)
    return tile


def _window_of(arr, axis, parts, index):
    flat = _flat2d(arr)
    rows, cols = flat.shape
    if axis == arr.ndim - 1:
        return flat, (rows, cols // parts), (0, index)
    assert axis == 0, axis
    return flat, (rows // parts, cols), (index, 0)


def _window_sum(name, source, window, index, others, out_dtype):
    w_rows, w_cols = window
    tm = _row_tile(w_rows, w_cols)
    blocks = w_rows // tm
    where = jnp.stack([jnp.asarray(i, jnp.int32) for i in index])

    def body(where_ref, src_ref, *refs):
        acc = src_ref[...].astype(F32)
        for r in refs[:-1]:
            acc = acc + r[...].astype(F32)
        refs[-1][...] = acc.astype(refs[-1].dtype)

    same = pl.BlockSpec((tm, w_cols), lambda i, where_ref: (i, 0))
    grid_spec = pltpu.PrefetchScalarGridSpec(
        num_scalar_prefetch=1, grid=(blocks,),
        in_specs=[pl.BlockSpec((tm, w_cols), lambda i, where_ref: (where_ref[0] * blocks + i, where_ref[1])), *[same] * len(others)],
        out_specs=same)
    return pl.pallas_call(body, name=name, grid_spec=grid_spec, out_shape=jax.ShapeDtypeStruct(window, out_dtype),
                          compiler_params=_params("arbitrary"))(where, source, *others)


def _sum_arrays(name, arrays, out_dtype):
    shape = arrays[0].shape
    flat = [_flat2d(a) for a in arrays]
    rows, cols = flat[0].shape
    tm = _row_tile(rows, cols)

    def fn(*tiles):
        acc = tiles[0].astype(F32)
        for t in tiles[1:]:
            acc = acc + t.astype(F32)
        return acc

    (out,) = _rowwise(name, fn, [(a, cols, 0) for a in flat], [], [(cols, out_dtype)], tm=tm)
    return out.reshape(shape)


def _grads_start(tag, names, grads):
    c = lax.axis_index("c")
    shapes = [g.shape for g in grads]
    axes = [(BIG_AXES[n][0] - 1, BIG_AXES[n][1] - 1) for n in names]

    def to_sibling(ins, outs, pos):
        if pos is None:
            return [None] * len(names)
        px, py, pc = pos
        return [(_region(ins[wi], _cuts(shapes[wi], sa, ha, half=1 - pc)), outs[wi], (px, py, 1 - pc))
                for wi, (sa, ha) in enumerate(axes)]

    landed = _exchange("grads_to_sibling", grads, [jax.ShapeDtypeStruct(_half_shape(s, ha), BF16) for s, (_, ha) in zip(shapes, axes)],
                       to_sibling)
    chip_sums = []
    for wi, name in enumerate(names):
        source, window, index = _window_of(grads[wi], axes[wi][1], 2, c)
        chip_sum = _window_sum("chip_sum_" + name, source, window, index, [_flat2d(landed[wi])], BF16)
        chip_sums.append(chip_sum.reshape(landed[wi].shape))

    def to_chips(sources, landings, pos):
        if pos is None:
            return [None] * (3 * len(names))
        px, py, pc = pos
        copies = []
        for wi, (sa, ha) in enumerate(axes):
            for j, (qx, qy) in enumerate(_other_chips(px, py)):
                piece = _region(sources[wi], _cuts(shapes[wi], sa, ha, shard=2 * qx + qy))
                copies.append((piece, landings[wi].at[j], (qx, qy, pc)))
        return copies

    landings = [lax.empty((3, *_shard_shape(t.shape, sa)), BF16) for t, (sa, _) in zip(chip_sums, axes)]
    return _split_start(f"grads_start_{tag}", chip_sums, landings, to_chips), to_chips


def _grads_finish(tag, names, started, after):
    chip_sums, pieces = _split_wait(f"grads_wait_{tag}", started[0], after, started[1])
    my_shard = 2 * lax.axis_index("x") + lax.axis_index("y")
    reduced = {}
    for name, chip_sum, piece in zip(names, chip_sums, pieces):
        source, window, index = _window_of(chip_sum, BIG_AXES[name][0] - 1, 4, my_shard)
        total = _window_sum("shard_sum_" + name, source, window, index, [_flat2d(piece[j]) for j in range(3)], F32)
        reduced[name] = total.reshape(piece.shape[1:])
    return reduced


def _join_halves(reduced):
    c = lax.axis_index("c")

    def halves_to_sibling(ins, outs, pos):
        if pos is None:
            return [None] * len(BIG)
        px, py, pc = pos
        return [(ins[wi], outs[wi], (px, py, 1 - pc)) for wi in range(len(BIG))]

    from_sibling = _exchange("grad_halves_to_sibling", reduced, [jax.ShapeDtypeStruct(r.shape, F32) for r in reduced],
                             halves_to_sibling)
    out = []
    for wi, (_, _, ha) in enumerate(BIG):
        low = jnp.where(c == 0, reduced[wi], from_sibling[wi])
        high = jnp.where(c == 0, from_sibling[wi], reduced[wi])
        out.append(jnp.concatenate([low, high], axis=ha))
    return out


def _pack(arrays):
    flat = jnp.concatenate([a.reshape(-1).astype(F32) for a in arrays])
    rows = -(-flat.shape[0] // PACK_LANES)
    rows = -(-rows // PACK_ROW_MULTIPLE) * PACK_ROW_MULTIPLE
    return jnp.pad(flat, (0, rows * PACK_LANES - flat.shape[0])).reshape(rows, PACK_LANES)


def _unpack(buf, shapes):
    flat, out, off = buf.reshape(-1), [], 0
    for s in shapes:
        n = 1
        for d in s:
            n *= d
        out.append(flat[off:off + n].reshape(s))
        off += n
    return out


def _sum8(name, gathered, rows):
    def body(in_ref, out_ref):
        acc = in_ref[0].astype(F32)
        for d in range(1, 8):
            acc = acc + in_ref[d].astype(F32)
        out_ref[...] = acc

    return pl.pallas_call(
        body, name=name, grid=(rows // PACK_ROW_MULTIPLE,),
        in_specs=[pl.BlockSpec((8, PACK_ROW_MULTIPLE, PACK_LANES), lambda i: (0, i, 0))],
        out_specs=pl.BlockSpec((PACK_ROW_MULTIPLE, PACK_LANES), lambda i: (i, 0)),
        out_shape=jax.ShapeDtypeStruct((rows, PACK_LANES), F32), compiler_params=_params("arbitrary"),
    )(gathered.reshape(8, rows, PACK_LANES))


def _adamw(name, w, g, m, v):
    shape = w.shape
    flat = [_flat2d(a) for a in (w, g, m, v)]
    rows, cols = flat[0].shape
    tm = _row_tile(rows, cols)
    outs = _rowwise(name, _adamw_fn, [(a, cols, 0) for a in flat], [], [(cols, F32)] * 3, tm=tm)
    return [o.reshape(shape) for o in outs]


WEIGHT_ORDER = ("norm_mix", "w_in", "conf_dw", "conf_dw_b", "conf_ln_g", "conf_ln_b", "pool_w", "pool_scale", "sc_conv",
                "gmlp_ln_g", "gmlp_ln_b", "gmlp_ws", "gmlp_bs", "w_branch", "w_out", "norm_mlp", "w_up", "w_down",
                "norm_ple", "w_ple", "w_ple_gate", "norm_final")


def _local_step(x, p, loss_target, weights, layer_big, full_small_sharded, on_big_grads):
    n_layers = p.shape[0]
    d_model = x.shape[1]
    layer_w, saved = [], []
    h = x
    for l in range(n_layers):
        w = {n: weights[n][l] for n in SMALL_REP if n != "norm_final"}
        w.update({n: full_small_sharded[n][l] for n in SMALL_SHARDED})
        layer_w.append(w)
        h, s = _layer_fwd(h, p[l], w, layer_big(l))
        saved.append(s)
    dx, d_final, loss = _rowwise("loss", _loss_fn, [(h, d_model, 0), (loss_target, d_model, 0)], [_row(weights["norm_final"])],
                                 [(d_model, F32)], [(1, d_model), (1, PACK_LANES)])
    small_grads = {n: [None] * n_layers for n in SMALL_REP + SMALL_SHARDED if n != "norm_final"}
    for l in reversed(range(n_layers)):
        dx, small = _layer_bwd(dx, p[l], layer_w[l], saved[l], on_big_grads(l))
        for n in small:
            small_grads[n][l] = small[n].reshape(weights[n].shape[1:]) if n in SMALL_REP else small[n]
    small_grads = {n: jnp.stack(v) for n, v in small_grads.items()}
    small_grads["norm_final"] = d_final.reshape(-1)
    return loss, dx, small_grads


def kernel(x, p, norm_mix, w_in, conf_dw, conf_dw_b, conf_ln_g, conf_ln_b, pool_w, pool_scale, sc_conv, gmlp_ln_g, gmlp_ln_b, gmlp_ws, gmlp_bs, w_branch, w_out, norm_mlp, w_up, w_down, norm_ple, w_ple, w_ple_gate, norm_final, loss_target, m_norm_mix, m_w_in, m_conf_dw, m_conf_dw_b, m_conf_ln_g, m_conf_ln_b, m_pool_w, m_pool_scale, m_sc_conv, m_gmlp_ln_g, m_gmlp_ln_b, m_gmlp_ws, m_gmlp_bs, m_w_branch, m_w_out, m_norm_mlp, m_w_up, m_w_down, m_norm_ple, m_w_ple, m_w_ple_gate, m_norm_final, v_norm_mix, v_w_in, v_conf_dw, v_conf_dw_b, v_conf_ln_g, v_conf_ln_b, v_pool_w, v_pool_scale, v_sc_conv, v_gmlp_ln_g, v_gmlp_ln_b, v_gmlp_ws, v_gmlp_bs, v_w_branch, v_w_out, v_norm_mlp, v_w_up, v_w_down, v_norm_ple, v_w_ple, v_w_ple_gate, v_norm_final):
    given = dict(locals())
    weights = {n: given[n] for n in WEIGHT_ORDER}
    mom_m = {n: given["m_" + n] for n in WEIGHT_ORDER}
    mom_v = {n: given["v_" + n] for n in WEIGHT_ORDER}
    my_shard = 2 * lax.axis_index("x") + lax.axis_index("y")

    n_layers = p.shape[0]
    shard_shapes = [weights[n].shape for n in SMALL_SHARDED]
    gathered = _allgather_small("allgather_small_weights", _pack([weights[n] for n in SMALL_SHARDED]))
    after_small = gathered[0, 0] * 0.0
    after_small = jnp.where(after_small == 0.0, after_small, 0.0)

    weight_copies = [[None] * len(WEIGHT_GROUPS) for _ in range(n_layers)]
    started = after_small
    for l in range(n_layers):
        for g, names in enumerate(WEIGHT_GROUPS):
            shards = [(weights[n][l] + started if i == 0 else weights[n][l]).astype(BF16) for i, n in enumerate(names)]
            weight_copies[l][g] = _weights_start(f"{l}_{g}", shards)
            started = weight_copies[l][g][4][0, 0]
    per_dev = gathered.reshape(8, -1, PACK_LANES)
    full_small_sharded = {}
    chip_parts = [_unpack(per_dev[2 * s], shard_shapes) for s in range(4)]
    for i, n in enumerate(SMALL_SHARDED):
        full_small_sharded[n] = jnp.concatenate([chip_parts[s][i] for s in range(4)], axis=-1)

    grad_copies = [[None] * len(GRAD_GROUPS) for _ in range(n_layers)]

    def layer_big(l):
        return lambda g, after: _weights_finish(f"{l}_{g}", WEIGHT_GROUPS[g], weight_copies[l][g], after)

    def on_big_grads(l):
        def send(g, group_grads):
            grad_copies[l][g] = _grads_start(f"{l}_{g}", GRAD_GROUPS[g], group_grads)
            return grad_copies[l][g][0][4][0, 0]
        return send

    loss_row, dx, small_grads = _local_step(x[0] + started, p[:, 0], loss_target[0], weights, layer_big, full_small_sharded,
                                            on_big_grads)
    loss = lax.psum(loss_row[0, 0], MESH_AXES)

    halves = [{} for _ in range(n_layers)]
    for l in reversed(range(n_layers)):
        for g, names in enumerate(GRAD_GROUPS):
            halves[l].update(_grads_finish(f"{l}_{g}", names, grad_copies[l][g], dx))
    big_reduced = _join_halves([jnp.stack([halves[l][n] for l in range(n_layers)]) for n, _, _ in BIG])
    grads = {n: g for (n, _, _), g in zip(BIG, big_reduced)}

    small_names = SMALL_REP + SMALL_SHARDED
    packed = _pack([small_grads[n] for n in small_names])
    rows = packed.shape[0]
    summed = _sum8("sum_small_grads", _allgather_small("allgather_small_grads", packed.astype(BF16)), rows)
    small_full_shapes = [weights[n].shape for n in SMALL_REP] + [full_small_sharded[n].shape for n in SMALL_SHARDED]
    for n, g in zip(small_names, _unpack(summed, small_full_shapes)):
        if n in SMALL_SHARDED:
            width = weights[n].shape[-1]
            g = lax.dynamic_slice_in_dim(g, my_shard * width, width, axis=g.ndim - 1)
        grads[n] = g

    delta, new_m, new_v = {}, {}, {}
    for n, _, _ in BIG:
        delta[n], new_m[n], new_v[n] = _adamw("adamw_" + n, weights[n], grads[n], mom_m[n], mom_v[n])
    for group, names in (("rep", SMALL_REP), ("sharded", SMALL_SHARDED)):
        outs = _adamw("adamw_small_" + group, *[_pack([src[n] for n in names]) for src in (weights, grads, mom_m, mom_v)])
        shapes = [weights[n].shape for n in names]
        for dst, buf in zip((delta, new_m, new_v), outs):
            for n, a in zip(names, _unpack(buf, shapes)):
                dst[n] = a

    return (loss, dx[None], *[grads[n] for n in WEIGHT_ORDER], *[delta[n] for n in WEIGHT_ORDER],
            *[new_m[n] for n in WEIGHT_ORDER], *[new_v[n] for n in WEIGHT_ORDER])
```

```python
import functools

import jax
import jax.numpy as jnp
from jax import lax
from jax.experimental import pallas as pl
from jax.experimental.pallas import tpu as pltpu

F32, BF16 = jnp.float32, jnp.bfloat16
EPS = 1e-6
ADAM_LR, ADAM_B1, ADAM_B2, ADAM_EPS, ADAM_WD, ADAM_STEP = 0.001, 0.9, 0.999, 1e-08, 0.01, 10
CONF_KERNEL, SC_KERNEL = 31, 3
POOL_WINDOWS = (2, 4, 8, 16)
MIX = 512
GROUP = 128
HALO = 32
SEQ_TILE = 256
VMEM_LIMIT_BYTES = 48 * 1024 * 1024
MESH_AXES = ("x", "y", "c")
ANY = pl.BlockSpec(memory_space=pl.ANY)

BIG = (("w_in", 2, 1), ("w_branch", 3, 1), ("w_out", 1, 2), ("w_up", 2, 1), ("w_down", 1, 2),
       ("w_ple", 2, 1), ("w_ple_gate", 1, 2))
SMALL_REP = ("norm_mix", "conf_dw_b", "conf_ln_g", "conf_ln_b", "pool_w", "pool_scale", "gmlp_ln_g",
             "gmlp_ln_b", "gmlp_ws", "gmlp_bs", "norm_mlp", "norm_ple", "norm_final")
SMALL_SHARDED = ("conf_dw", "sc_conv")
BY_CHIP = ("w_in", "w_up")
BIG_AXES = {name: (shard_axis, half_axis) for name, shard_axis, half_axis in BIG}
WEIGHT_GROUPS = (("w_in",), ("w_branch", "w_out", "w_up", "w_down", "w_ple_gate", "w_ple"))
GRAD_GROUPS = (("w_ple", "w_ple_gate", "w_down", "w_up"), ("w_out", "w_branch", "w_in"))
PACK_LANES = 128
PACK_ROW_MULTIPLE = 256


def _params(*semantics):
    return pltpu.CompilerParams(dimension_semantics=semantics, vmem_limit_bytes=VMEM_LIMIT_BYTES)


def _accumulate(ref, value, first):
    @pl.when(first)
    def _():
        ref[...] = value.astype(ref.dtype)

    @pl.when(jnp.logical_not(first))
    def _():
        ref[...] += value.astype(ref.dtype)


def _rowwise(name, fn, rows, params, outs, accs=(), tm=256):
    n_rows = rows[0][0].shape[-2]
    tm = min(tm, n_rows)
    n_in, n_out = len(rows) + len(params), len(outs)
    targets = [(k, o[2]) for k, o in enumerate(outs) if len(o) == 4]

    def body(*refs):
        res = fn(*[r[...] for r in refs[:n_in]])
        res = res if isinstance(res, (tuple, list)) else (res,)
        out_refs = refs[n_in + len(targets):]
        for r, v in zip(out_refs[:n_out], res[:n_out]):
            r[...] = v.astype(r.dtype)
        first = pl.program_id(0) == 0
        for r, v in zip(out_refs[n_out:], res[n_out:]):
            _accumulate(r, v, first)

    in_specs = []
    for row in rows:
        arr, w, cb = row[:3]
        if len(row) == 4:
            in_specs.append(pl.BlockSpec((None, tm, w), lambda i, cb=cb, g=row[3]: (g, i, cb)))
        else:
            in_specs.append(pl.BlockSpec((tm, w), lambda i, cb=cb: (i, cb)))
    for p in params:
        in_specs.append(pl.BlockSpec(p.shape, lambda i, nd=p.ndim: (0,) * nd))
    in_specs += [ANY] * len(targets)
    out_specs = [pl.BlockSpec((tm, o[0]), lambda i, cb=(o[3] if len(o) == 4 else 0): (i, cb)) for o in outs]
    out_specs += [pl.BlockSpec(s, lambda i, nd=len(s): (0,) * nd) for s in accs]
    out_shape = [jax.ShapeDtypeStruct(o[2].shape if len(o) == 4 else (n_rows, o[0]), o[1]) for o in outs]
    out_shape += [jax.ShapeDtypeStruct(s, F32) for s in accs]
    return pl.pallas_call(
        body, name=name, grid=(n_rows // tm,), in_specs=in_specs, out_specs=out_specs, out_shape=out_shape,
        input_output_aliases={n_in + t: k for t, (k, _) in enumerate(targets)},
        compiler_params=_params("arbitrary"),
    )(*[r[0] for r in rows], *params, *[t for _, t in targets])


ROW_OUT_SUBLANES = 8


def _mm(name, a, b, *, ta=False, tb=False, b_by_chip=False, chips=(0, 4), res=(), rows=(), epi=None, out_dtypes=(F32,), n_row_outs=0,
        tm=1024, tn=1024, tk=1024):
    g_, m_, k_ = (a.shape[0], a.shape[2], a.shape[1]) if ta else a.shape
    if b_by_chip:
        n_ = b.shape[2] if tb else chips[1] * b.shape[3]
    else:
        n_ = b.shape[1] if tb else b.shape[2]
    tm, tn, tk = min(tm, m_), min(tn, n_), min(tk, k_)
    if b_by_chip:
        tn, tk = (tn, min(tk, b.shape[3])) if tb else (min(tn, b.shape[3]), tk)
        per_chip = b.shape[3] // (tk if tb else tn)
    nk, n_res, n_rows, n_out = k_ // tk, len(res), len(rows), len(out_dtypes)
    n_in = 2 + n_res + n_rows
    dims = (((0 if ta else 1,), (1 if tb else 0,)), ((), ()))
    if epi is None:
        epi = lambda acc: (acc,)

    def body(*refs):
        a_ref, b_ref = refs[:2]
        res_refs, out_refs, acc_ref = refs[2:n_in], refs[n_in:n_in + n_out], refs[-1]
        row_out_refs = refs[n_in + n_out:-1]
        k = pl.program_id(3)

        @pl.when(k == 0)
        def _():
            acc_ref[...] = jnp.zeros_like(acc_ref)

        acc_ref[...] += lax.dot_general(a_ref[...].astype(BF16), b_ref[...].astype(BF16), dims,
                                        preferred_element_type=F32)

        @pl.when(k == nk - 1)
        def _():
            vals = epi(acc_ref[...], *[r[...] for r in res_refs])
            for o, v in zip(out_refs, vals):
                o[...] = v.astype(o.dtype)
            for o, v in zip(row_out_refs, vals[n_out:]):
                o[...] = jnp.broadcast_to(v, o.shape)

    a_spec = (pl.BlockSpec((None, tk, tm), lambda g, i, j, k: (g, k, i)) if ta
              else pl.BlockSpec((None, tm, tk), lambda g, i, j, k: (g, i, k)))
    if b_by_chip and tb:
        b_spec = pl.BlockSpec((None, None, tn, tk), lambda g, i, j, k: (g, k // per_chip, j, k % per_chip))
    elif b_by_chip:
        b_spec = pl.BlockSpec((None, None, tk, tn), lambda g, i, j, k: (g, chips[0] + j // per_chip, k, j % per_chip))
    elif tb:
        b_spec = pl.BlockSpec((None, tn, tk), lambda g, i, j, k: (g, j, k))
    else:
        b_spec = pl.BlockSpec((None, tk, tn), lambda g, i, j, k: (g, k, j))
    res_specs = [pl.BlockSpec((None, tm, tn), lambda g, i, j, k, off=off: (g, i, j + off)) for _, off in res]
    row_specs = [pl.BlockSpec((None, 1, tn), lambda g, i, j, k: (g, 0, j))] * n_rows
    out_spec = pl.BlockSpec((None, tm, tn), lambda g, i, j, k: (g, i, j))
    row_out_spec = pl.BlockSpec((None, ROW_OUT_SUBLANES, tn), lambda g, i, j, k: (g, i, j))
    outs = pl.pallas_call(
        body, name=name, grid=(g_, m_ // tm, n_ // tn, nk),
        in_specs=[a_spec, b_spec, *res_specs, *row_specs], out_specs=[out_spec] * n_out + [row_out_spec] * n_row_outs,
        out_shape=[jax.ShapeDtypeStruct((g_, m_, n_), d) for d in out_dtypes]
        + [jax.ShapeDtypeStruct((g_, ROW_OUT_SUBLANES * (m_ // tm), n_), F32)] * n_row_outs,
        scratch_shapes=[pltpu.VMEM((tm, tn), F32)],
        compiler_params=_params("parallel", "parallel", "parallel", "arbitrary"),
    )(a, b, *[r for r, _ in res], *rows)
    return outs


def _mm2(name, a, b, *, res=(), rows=(), **kw):
    outs = _mm(name, a[None], b[None], res=[(r[None], off) for r, off in res], rows=[r[None] for r in rows], **kw)
    return [o[0] for o in outs]


def _sum_row_out(row_out):
    return jnp.sum(row_out[::ROW_OUT_SUBLANES], axis=0, keepdims=True)


@jax.custom_vjp
def _bdot(a, b):
    return jnp.dot(a.astype(BF16), b.astype(BF16), preferred_element_type=F32)


def _bdot_fwd(a, b):
    return _bdot(a, b), (a, b)


def _bdot_bwd(saved, g):
    a, b = saved
    gb = g.astype(BF16)
    da = lax.dot_general(gb, b.astype(BF16), (((1,), (1,)), ((), ())), preferred_element_type=F32)
    db = lax.dot_general(a.astype(BF16), gb, (((0,), (0,)), ((), ())), preferred_element_type=F32)
    return da.astype(a.dtype), db.astype(b.dtype)


_bdot.defvjp(_bdot_fwd, _bdot_bwd)


def _rms(x, g):
    return x * lax.rsqrt(jnp.mean(x * x, axis=-1, keepdims=True) + EPS) * g


def _layer_norm(x, g, b):
    mu = jnp.mean(x, axis=-1, keepdims=True)
    var = jnp.mean(jnp.square(x - mu), axis=-1, keepdims=True)
    return (x - mu) * lax.rsqrt(var + EPS) * g + b


def _glu_and_gate_fn(a, a_gate, cg, hx):
    return a * jax.nn.sigmoid(a_gate), cg * hx


def _branches_fn(ca, pooled, cc, bg, u, v, conf_b, conf_g, conf_beta, pool_scale, gm_g, gm_b, *mats):
    pool_w, ws, bs = mats[0:4], mats[4:8], mats[8:12]
    ya = jax.nn.silu(_layer_norm(ca + conf_b, conf_g, conf_beta))
    yb = jnp.concatenate([_bdot(pooled[:, g * GROUP:(g + 1) * GROUP], pool_w[g]) for g in range(4)], axis=1) * pool_scale
    yc = bg * cc
    vln = _layer_norm(v, gm_g, gm_b)
    causal = lax.broadcasted_iota(jnp.int32, (GROUP, GROUP), 0) >= lax.broadcasted_iota(jnp.int32, (GROUP, GROUP), 1)
    sg = jnp.concatenate(
        [_bdot(jnp.where(causal, ws[g], 0.0), vln[:, g * GROUP:(g + 1) * GROUP]) + bs[g] for g in range(4)], axis=1)
    yd = u * sg
    return ya, yb, yc, yd


def _branches_bwd_fn(*args):
    ins, dys = args[:6] + args[10:], args[6:10]
    _, vjp = jax.vjp(_branches_fn, *ins)
    grads = vjp(tuple(dys))
    duv = jnp.concatenate([grads[4], grads[5]], axis=1)
    return (*grads[:4], duv, *grads[6:])


def _glu_bwd_fn(a, a_gate, cg, hx, dga, dchx, dpin, dbg):
    _, vjp = jax.vjp(_glu_and_gate_fn, a, a_gate, cg, hx)
    da, dag, dcg, dhx = vjp((dga, dchx))
    return jnp.concatenate([da, dag, dpin.astype(F32), dbg.astype(F32), dcg, dhx], axis=1)


def _rms_bwd_epilogue(dh, x, dres, g):
    _, vjp = jax.vjp(_rms, x, g)
    dx, dg = vjp(dh)
    return dx + dres, dg


def _residual_then_rms(acc, residual, g):
    x = acc + residual
    return x, _rms(x, g)


def _loss_fn(x, target, g):
    def f(x, g):
        err = jnp.square(_rms(x, g) - target)
        return 0.5 * jnp.sum(jnp.mean(err, axis=-1, keepdims=True), axis=0, keepdims=True)

    loss, vjp = jax.vjp(f, x, g)
    dx, dg = vjp(jnp.ones((1, 1), F32))
    return dx, dg, jnp.broadcast_to(loss, (1, PACK_LANES))


def _adamw_fn(w, g, m, v):
    m = ADAM_B1 * m + (1.0 - ADAM_B1) * g
    v = ADAM_B2 * v + (1.0 - ADAM_B2) * jnp.square(g)
    m_hat = m / (1.0 - ADAM_B1 ** ADAM_STEP)
    v_hat = v / (1.0 - ADAM_B2 ** ADAM_STEP)
    delta = -ADAM_LR * (m_hat / (jnp.sqrt(v_hat) + ADAM_EPS) + ADAM_WD * w)
    return delta, m, v


def _group_pick(j):
    return [(j == g).astype(F32) for g in range(len(POOL_WINDOWS))]


def _accumulate_row(ref, row, value, first):
    @pl.when(first)
    def _():
        ref[pl.ds(row, 1), :] = value

    @pl.when(jnp.logical_not(first))
    def _():
        ref[pl.ds(row, 1), :] += value


SUBLANES = 8


def _causal_taps(x, taps):
    for r in range(min(SUBLANES, taps)):
        behind = x if r == 0 else pltpu.roll(x, r, 0)
        for q in range((taps - 1 - r) // SUBLANES + 1):
            start = HALO - SUBLANES * q
            yield SUBLANES * q + r, behind[start:start + SEQ_TILE, :]


def _rows_iota(n):
    return lax.broadcasted_iota(jnp.int32, (n, 1), 0)


def _seq_specs(n_time, causal):
    per = SEQ_TILE // HALO
    tile = pl.BlockSpec((SEQ_TILE, GROUP), lambda j, i: (i, j))
    if causal:
        halo = pl.BlockSpec((HALO, GROUP), lambda j, i: (jnp.maximum(i * per - 1, 0), j))
    else:
        halo = pl.BlockSpec((HALO, GROUP), lambda j, i: (jnp.minimum((i + 1) * per, n_time * per - 1), j))
    return tile, halo


def _seq_fwd(ga, proj, chx, conf_dw, sc_conv):
    s_len = ga.shape[0]
    n_time = s_len // SEQ_TILE
    ext = HALO + SEQ_TILE

    def body(ga_ref, ga_h, pin_ref, pin_h, chx_ref, chx_h, wc_ref, ws_ref, ca_ref, po_ref, cc_ref):
        j, i = pl.program_id(0), pl.program_id(1)
        keep = (i > 0).astype(F32)

        def extended(cur, halo):
            return jnp.concatenate([halo[...] * keep, cur[...]], axis=0)

        def conv(x, w_ref, taps):
            acc = None
            for d, rows in _causal_taps(x, taps):
                term = rows * w_ref[pl.ds(taps - 1 - d, 1), :]
                acc = term if acc is None else acc + term
            return acc

        ca_ref[...] = conv(extended(ga_ref, ga_h), wc_ref, CONF_KERNEL)
        cc_ref[...] = conv(extended(chx_ref, chx_h), ws_ref, SC_KERNEL)
        u = extended(pin_ref, pin_h)
        s1 = u + pltpu.roll(u, 1, 0)
        s2 = s1 + pltpu.roll(s1, 2, 0)
        s3 = s2 + pltpu.roll(s2, 4, 0)
        s4 = s3 + pltpu.roll(s3, 8, 0)
        pick = _group_pick(j)
        window_sum = s1 * pick[0] + s2 * pick[1] + s3 * pick[2] + s4 * pick[3]
        window = sum(float(w) * f for w, f in zip(POOL_WINDOWS, pick))
        pos = (_rows_iota(ext) + (i * SEQ_TILE - HALO + 1)).astype(F32)
        count = jnp.maximum(jnp.minimum(pos, window), 1.0)
        po_ref[...] = (window_sum / count - u)[HALO:, :]

    tile, halo = _seq_specs(n_time, causal=True)
    pin_tile = pl.BlockSpec((SEQ_TILE, GROUP), lambda j, i: (i, 2 * (MIX // GROUP) + j))
    pin_halo = pl.BlockSpec((HALO, GROUP), lambda j, i: (jnp.maximum(i * (SEQ_TILE // HALO) - 1, 0), 2 * (MIX // GROUP) + j))
    w_spec = lambda taps: pl.BlockSpec((taps, GROUP), lambda j, i: (0, j))
    return pl.pallas_call(
        body, name="seq_fwd", grid=(MIX // GROUP, n_time),
        in_specs=[tile, halo, pin_tile, pin_halo, tile, halo, w_spec(CONF_KERNEL), w_spec(SC_KERNEL)],
        out_specs=[tile] * 3, out_shape=[jax.ShapeDtypeStruct((s_len, MIX), F32)] * 3,
        compiler_params=_params("parallel", "arbitrary"),
    )(ga, ga, proj, proj, chx, chx, conf_dw, sc_conv)


def _seq_bwd(dca, dpooled, dcc, ga, chx, conf_dw, sc_conv):
    s_len = ga.shape[0]
    n_time = s_len // SEQ_TILE
    ext = HALO + SEQ_TILE

    def body(dca_ref, dca_h, dpo_ref, dpo_h, dcc_ref, dcc_h, ga_ref, ga_h, chx_ref, chx_h, wc_ref, ws_ref,
             dga_ref, dpin_ref, dchx_ref, dwc_ref, dws_ref):
        j, i = pl.program_id(0), pl.program_id(1)
        keep_prev = (i > 0).astype(F32)
        keep_next = (i < n_time - 1).astype(F32)

        def with_next(cur, halo):
            return jnp.concatenate([cur[...], halo[...] * keep_next], axis=0)

        def with_prev(cur, halo):
            return jnp.concatenate([halo[...] * keep_prev, cur[...]], axis=0)

        def conv_t(dy, w_ref, taps):
            acc = None
            for r in range(min(SUBLANES, taps)):
                ahead = dy if r == 0 else pltpu.roll(dy, ext - r, 0)
                for q in range((taps - 1 - r) // SUBLANES + 1):
                    term = ahead[SUBLANES * q:SUBLANES * q + SEQ_TILE, :] * w_ref[pl.ds(taps - 1 - (SUBLANES * q + r), 1), :]
                    acc = term if acc is None else acc + term
            return acc

        def tap_grads(dy_cur, x, dw_ref, taps):
            for d, rows in _causal_taps(x, taps):
                part = jnp.sum(dy_cur * rows, axis=0, keepdims=True)
                _accumulate_row(dw_ref, taps - 1 - d, part, i == 0)

        dga_ref[...] = conv_t(with_next(dca_ref, dca_h), wc_ref, CONF_KERNEL)
        dchx_ref[...] = conv_t(with_next(dcc_ref, dcc_h), ws_ref, SC_KERNEL)
        tap_grads(dca_ref[...], with_prev(ga_ref, ga_h), dwc_ref, CONF_KERNEL)
        tap_grads(dcc_ref[...], with_prev(chx_ref, chx_h), dws_ref, SC_KERNEL)

        dpo = with_next(dpo_ref, dpo_h)
        pick = _group_pick(j)
        window = sum(float(w) * f for w, f in zip(POOL_WINDOWS, pick))
        pos = (_rows_iota(ext) + (i * SEQ_TILE + 1)).astype(F32)
        r = dpo / jnp.minimum(pos, window)
        q1 = r + pltpu.roll(r, ext - 1, 0)
        q2 = q1 + pltpu.roll(q1, ext - 2, 0)
        q3 = q2 + pltpu.roll(q2, ext - 4, 0)
        q4 = q3 + pltpu.roll(q3, ext - 8, 0)
        q = q1 * pick[0] + q2 * pick[1] + q3 * pick[2] + q4 * pick[3]
        dpin_ref[...] = (q - dpo)[:SEQ_TILE, :].astype(dpin_ref.dtype)

    tile, nxt = _seq_specs(n_time, causal=False)
    _, prv = _seq_specs(n_time, causal=True)
    w_spec = lambda taps: pl.BlockSpec((taps, GROUP), lambda j, i: (0, j))
    return pl.pallas_call(
        body, name="seq_bwd", grid=(MIX // GROUP, n_time),
        in_specs=[tile, nxt, tile, nxt, tile, nxt, tile, prv, tile, prv, w_spec(CONF_KERNEL), w_spec(SC_KERNEL)],
        out_specs=[tile, tile, tile, w_spec(CONF_KERNEL), w_spec(SC_KERNEL)],
        out_shape=[jax.ShapeDtypeStruct((s_len, MIX), F32), jax.ShapeDtypeStruct((s_len, MIX), BF16),
                   jax.ShapeDtypeStruct((s_len, MIX), F32), jax.ShapeDtypeStruct(conf_dw.shape, F32),
                   jax.ShapeDtypeStruct(sc_conv.shape, F32)],
        compiler_params=_params("parallel", "arbitrary"),
    )(dca, dca, dpooled, dpooled, dcc, dcc, ga, ga, chx, chx, conf_dw, sc_conv)


MERGE_TILE = 512
MERGE_ROWS = 1024


def _merge_fwd(y, w_branch, gates):
    s_len, d_model = y.shape[1], w_branch.shape[2]
    t = min(MERGE_ROWS, s_len)
    per_branch = d_model // MERGE_TILE

    def body(y_ref, w_ref, gate_ref, out_ref, acc_ref):
        k = pl.program_id(2)

        @pl.when(k == 0)
        def _():
            acc_ref[...] = jnp.zeros_like(acc_ref)

        z = jnp.dot(y_ref[...], w_ref[...], preferred_element_type=F32)
        acc_ref[...] += gate_ref[...].astype(F32) * z

        @pl.when(k == 3)
        def _():
            out_ref[...] = acc_ref[...].astype(out_ref.dtype)

    return pl.pallas_call(
        body, name="merge_fwd", grid=(s_len // t, per_branch, 4),
        in_specs=[pl.BlockSpec((None, t, MIX), lambda i, j, k: (k, i, 0)),
                  pl.BlockSpec((None, MIX, MERGE_TILE), lambda i, j, k: (k, 0, j)),
                  pl.BlockSpec((t, MERGE_TILE), lambda i, j, k: (i, k * per_branch + j))],
        out_specs=pl.BlockSpec((t, MERGE_TILE), lambda i, j, k: (i, j)),
        out_shape=jax.ShapeDtypeStruct((s_len, d_model), BF16),
        scratch_shapes=[pltpu.VMEM((t, MERGE_TILE), F32)],
        compiler_params=_params("parallel", "parallel", "arbitrary"),
    )(y, w_branch, gates)


def _merge_bwd(y, w_branch, gates, dmerged):
    s_len, d_model = y.shape[1], w_branch.shape[2]
    t = min(MERGE_ROWS, s_len)
    per_branch = d_model // MERGE_TILE

    def body(y_ref, w_ref, gate_ref, dm_ref, dz_ref, dgate_ref):
        z = jnp.dot(y_ref[...], w_ref[...], preferred_element_type=F32)
        gate = gate_ref[...].astype(F32)
        dm = dm_ref[...]
        dz_ref[...] = (dm * gate).astype(dz_ref.dtype)
        dgate_ref[...] = (dm * z * gate * (1.0 - gate)).astype(dgate_ref.dtype)

    return pl.pallas_call(
        body, name="merge_bwd", grid=(s_len // t, per_branch, 4),
        in_specs=[pl.BlockSpec((None, t, MIX), lambda i, j, k: (k, i, 0)),
                  pl.BlockSpec((None, MIX, MERGE_TILE), lambda i, j, k: (k, 0, j)),
                  pl.BlockSpec((t, MERGE_TILE), lambda i, j, k: (i, k * per_branch + j)),
                  pl.BlockSpec((t, MERGE_TILE), lambda i, j, k: (i, j))],
        out_specs=[pl.BlockSpec((None, t, MERGE_TILE), lambda i, j, k: (k, i, j)),
                   pl.BlockSpec((t, MERGE_TILE), lambda i, j, k: (i, (4 + k) * per_branch + j))],
        out_shape=[jax.ShapeDtypeStruct((4, s_len, d_model), BF16), jax.ShapeDtypeStruct((s_len, 8 * d_model), BF16)],
        compiler_params=_params("parallel", "parallel", "arbitrary"),
    )(y, w_branch, gates, dmerged)


def _row(v):
    return v.reshape(1, -1)


def _branch_params(w):
    mats = [w["pool_w"][g] for g in range(4)] + [w["gmlp_ws"][g] for g in range(4)]
    mats += [w["gmlp_bs"][g].reshape(GROUP, 1) for g in range(4)]
    return [_row(w["conf_dw_b"]), _row(w["conf_ln_g"]), _row(w["conf_ln_b"]), _row(w["pool_scale"]),
            _row(w["gmlp_ln_g"]), _row(w["gmlp_ln_b"])] + mats


def _layer_fwd(x0, p_l, w, fetch):
    d_model = x0.shape[1]
    (h1,) = _rowwise("rms_mix", _rms, [(x0, d_model, 0)], [_row(w["norm_mix"])], [(d_model, BF16)])
    w.update(fetch(0, h1))
    (proj,) = _mm2("proj", h1, w["w_in"], b_by_chip=True, chips=(0, 2))
    w.update(fetch(1, proj))
    (gates,) = _mm2("proj_gates", h1, w["w_in"], b_by_chip=True, chips=(2, 2), out_dtypes=(BF16,),
                    epi=lambda acc: (jax.nn.sigmoid(acc),))
    ga, chx = _rowwise("glu", _glu_and_gate_fn, [(proj, MIX, 0), (proj, MIX, 1), (proj, MIX, 4), (proj, MIX, 5)], [],
                       [(MIX, F32), (MIX, F32)])
    ca, pooled, cc = _seq_fwd(ga, proj, chx, w["conf_dw"], w["sc_conv"])
    ys = _rowwise("branches", _branches_fn,
                  [(ca, MIX, 0), (pooled, MIX, 0), (cc, MIX, 0), (proj, MIX, 3), (proj, MIX, 6), (proj, MIX, 7)],
                  _branch_params(w), [(MIX, BF16)] * 4, tm=GROUP)
    y = jnp.stack(ys)
    merged = _merge_fwd(y, w["w_branch"], gates)
    x1, h2 = _mm2("out_proj", merged, w["w_out"], res=[(x0, 0)], rows=[_row(w["norm_mlp"])], out_dtypes=(F32, BF16),
                  epi=_residual_then_rms)
    up, act = _mm2("mlp_up", h2, w["w_up"], b_by_chip=True, out_dtypes=(F32, BF16),
                   epi=lambda acc: (acc, jnp.square(jnp.maximum(acc, 0.0))))
    x2, h3 = _mm2("mlp_down", act, w["w_down"], res=[(x1, 0)], rows=[_row(w["norm_ple"])], out_dtypes=(F32, BF16),
                  epi=_residual_then_rms)
    (pg,) = _mm2("ple_gate", h3, w["w_ple_gate"])
    (x3,) = _mm2("ple", p_l, w["w_ple"], res=[(pg, 0), (x2, 0)],
                 epi=lambda acc, g, r: (acc * jax.nn.sigmoid(g) + r,))
    saved = dict(x0=x0, h1=h1, proj=proj, gates=gates, ga=ga, chx=chx, ca=ca, pooled=pooled, cc=cc, y=y, merged=merged, x1=x1,
                 h2=h2, up=up, act=act, x2=x2, h3=h3, pg=pg)
    return x3, saved


def _layer_bwd(dx3, p_l, w, s, send):
    d_model = dx3.shape[1]
    big, small = {}, {}

    def ple_epi(acc, dx, g):
        sig = jax.nn.sigmoid(g)
        return dx * sig, dx * acc * sig * (1.0 - sig)

    dpp, dpg = _mm2("ple_bwd", p_l, w["w_ple"], res=[(dx3, 0), (s["pg"], 0)], epi=ple_epi, out_dtypes=(BF16, BF16))
    (big["w_ple"],) = _mm2("dw_ple", p_l, dpp, ta=True, out_dtypes=(BF16,))
    (big["w_ple_gate"],) = _mm2("dw_ple_gate", s["h3"], dpg, ta=True, out_dtypes=(BF16,))
    dx2, dg = _mm2("dh_ple", dpg, w["w_ple_gate"], tb=True, res=[(s["x2"], 0), (dx3, 0)], rows=[_row(w["norm_ple"])],
                   epi=_rms_bwd_epilogue, n_row_outs=1)
    small["norm_ple"] = _sum_row_out(dg)

    (dup,) = _mm2("mlp_down_bwd", dx2, w["w_down"], tb=True, res=[(s["up"], 0)], out_dtypes=(BF16,),
                  epi=lambda acc, up: (acc * 2.0 * jnp.maximum(up, 0.0),))
    (big["w_down"],) = _mm2("dw_down", s["act"], dx2, ta=True, out_dtypes=(BF16,))
    (big["w_up"],) = _mm2("dw_up", s["h2"], dup, ta=True, out_dtypes=(BF16,))
    sent = send(0, [big[n] for n in GRAD_GROUPS[0]])
    dx1, dg = _mm2("mlp_up_bwd", dup, w["w_up"], tb=True, b_by_chip=True, res=[(s["x1"], 0), (dx2, 0)],
                   rows=[_row(w["norm_mlp"]) + sent], epi=_rms_bwd_epilogue, n_row_outs=1)
    small["norm_mlp"] = _sum_row_out(dg)

    (dmerged,) = _mm2("out_proj_bwd", dx1, w["w_out"], tb=True)
    (big["w_out"],) = _mm2("dw_out", s["merged"], dx1, ta=True, out_dtypes=(BF16,))
    dz, dproj = _merge_bwd(s["y"], w["w_branch"], s["gates"], dmerged)
    (dy,) = _mm("branch_bwd", dz, w["w_branch"], tb=True)
    (big["w_branch"],) = _mm("dw_branch", s["y"], dz, ta=True, out_dtypes=(BF16,))

    proj = s["proj"]
    params = _branch_params(w)
    grads = _rowwise(
        "branches_bwd", _branches_bwd_fn,
        [(s["ca"], MIX, 0), (s["pooled"], MIX, 0), (s["cc"], MIX, 0), (proj, MIX, 3), (proj, MIX, 6), (proj, MIX, 7)]
        + [(dy, MIX, 0, g) for g in range(4)], params,
        [(MIX, F32), (MIX, F32), (MIX, F32), (MIX, BF16), (2 * MIX, BF16, dproj, 3)], [p.shape for p in params], tm=GROUP)
    dca, dpooled, dcc, dbg, dproj = grads[:5]
    pg_ = grads[5:]
    small["conf_dw_b"], small["conf_ln_g"], small["conf_ln_b"], small["pool_scale"] = pg_[0], pg_[1], pg_[2], pg_[3]
    small["gmlp_ln_g"], small["gmlp_ln_b"] = pg_[4], pg_[5]
    small["pool_w"] = jnp.stack(pg_[6:10])
    small["gmlp_ws"] = jnp.stack(pg_[10:14])
    small["gmlp_bs"] = jnp.stack([b.reshape(GROUP) for b in pg_[14:18]])

    dga, dpin, dchx, small["conf_dw"], small["sc_conv"] = _seq_bwd(dca, dpooled, dcc, s["ga"], s["chx"], w["conf_dw"], w["sc_conv"])
    (dproj,) = _rowwise("glu_bwd", _glu_bwd_fn,
                        [(proj, MIX, 0), (proj, MIX, 1), (proj, MIX, 4), (proj, MIX, 5), (dga, MIX, 0), (dchx, MIX, 0),
                         (dpin, MIX, 0), (dbg, MIX, 0)], [], [(6 * MIX, BF16, dproj, 0)])
    (big["w_in"],) = _mm2("dw_in", s["h1"], dproj, ta=True, out_dtypes=(BF16,))
    sent = send(1, [big[n] for n in GRAD_GROUPS[1]])
    dx0, dg = _mm2("proj_bwd", dproj, w["w_in"], tb=True, b_by_chip=True, res=[(s["x0"], 0), (dx1, 0)],
                   rows=[_row(w["norm_mix"]) + sent], epi=_rms_bwd_epilogue, n_row_outs=1)
    small["norm_mix"] = _sum_row_out(dg)
    return dx0, small


def _position():
    return lax.axis_index("x"), lax.axis_index("y"), lax.axis_index("c")


def _other_chips(x, y):
    return [(1 - x, y), (x, 1 - y), (1 - x, 1 - y)]


def _region(ref, cuts):
    return ref.at[tuple(pl.ds(*cuts[a]) if a in cuts else slice(None) for a in range(len(ref.shape)))]


def _allgather_small(name, block):
    m_per, n = block.shape

    def body(x_ref, out_ref, send_sems, recv_sems, local_sem):
        x, y, c = _position()
        me, sibling = (x, y, c), (x, y, 1 - c)
        chips = _other_chips(x, y)

        def rows(px, py, pc):
            return out_ref.at[pl.ds((4 * px + 2 * py + pc) * m_per, m_per), :]

        def copy(k, blk, to, src=None):
            return pltpu.make_async_remote_copy(
                src_ref=rows(*blk) if src is None else src, dst_ref=rows(*blk), send_sem=send_sems.at[k],
                recv_sem=recv_sems.at[k], device_id=to, device_id_type=pl.DeviceIdType.MESH)

        mine = pltpu.make_async_copy(x_ref, rows(*me), local_sem)
        mine.start()
        first = [copy(0, me, sibling, src=x_ref)]
        first += [copy(1 + j, me, (*chip, c), src=x_ref) for j, chip in enumerate(chips)]
        for cp in first:
            cp.start()
        passed = [copy(4 + j, (*chip, c), sibling) for j, chip in enumerate(chips)]
        for j, chip in enumerate(chips):
            copy(1 + j, (*chip, c), me).wait_recv()
            passed[j].start()
        copy(0, sibling, me).wait_recv()
        for j, chip in enumerate(chips):
            copy(4 + j, (*chip, 1 - c), me).wait_recv()
        for cp in first + passed:
            cp.wait_send()
        mine.wait()

    return pl.pallas_call(
        body, name=name, out_shape=jax.ShapeDtypeStruct((8 * m_per, n), block.dtype), in_specs=[ANY], out_specs=ANY,
        scratch_shapes=[pltpu.SemaphoreType.DMA((7,)), pltpu.SemaphoreType.DMA((7,)), pltpu.SemaphoreType.DMA],
    )(block)


def _cuts(arr_shape, shard_axis, half_axis, shard=None, half=None):
    cuts = {}
    if shard is not None:
        n = arr_shape[shard_axis] // 4
        cuts[shard_axis] = (shard * n, n)
    if half is not None:
        n = arr_shape[half_axis] // 2
        cuts[half_axis] = (half * n, n)
    return cuts


def _full_shape(shard_shape, shard_axis):
    return tuple(4 * n if a == shard_axis else n for a, n in enumerate(shard_shape))


HBM = pl.BlockSpec(memory_space=pltpu.HBM)
SEM = pl.BlockSpec(memory_space=pltpu.SEMAPHORE)
DATAFLOW_EFFECT = pltpu.SideEffectType.DATAFLOW_SIDE_EFFECTING
TOKEN_SHAPE = (8, 128)


def _remote(src, dst, send_sems, recv_sems, k, peer):
    return pltpu.make_async_remote_copy(src_ref=src, dst_ref=dst, send_sem=send_sems.at[k], recv_sem=recv_sems.at[k],
                                        device_id=peer, device_id_type=pl.DeviceIdType.MESH)


def _split_start(name, sources, landings, plan):
    n_s, n_l = len(sources), len(landings)
    n_copies = len(plan([None] * n_s, [None] * n_l, None))

    def body(*refs):
        src, land = refs[:n_s], refs[n_s:n_s + n_l]
        send_sems, recv_sems, token = refs[n_s + n_l], refs[n_s + n_l + 1], refs[-1]
        for k, (s, d, peer) in enumerate(plan(src, land, _position())):
            _remote(s, d, send_sems, recv_sems, k, peer).start()
        token[...] = jnp.zeros_like(token)

    arrays = [pltpu.with_memory_space_constraint(a, pltpu.HBM) for a in (*sources, *landings)]
    outs = pl.pallas_call(
        body, name=name,
        out_shape=(pltpu.SemaphoreType.DMA((n_copies,)), pltpu.SemaphoreType.DMA((n_copies,)),
                   *[pltpu.HBM(a.shape, a.dtype) for a in arrays], jax.ShapeDtypeStruct(TOKEN_SHAPE, F32)),
        in_specs=[HBM] * (n_s + n_l),
        out_specs=(SEM, SEM, *[HBM] * (n_s + n_l), pl.BlockSpec(memory_space=pltpu.VMEM)),
        input_output_aliases={i: 2 + i for i in range(n_s + n_l)},
        compiler_params=pltpu.CompilerParams(has_side_effects=DATAFLOW_EFFECT),
    )(*arrays)
    return outs[0], outs[1], list(outs[2:2 + n_s]), list(outs[2 + n_s:2 + n_s + n_l]), outs[-1]


def _split_wait(name, started, after, plan):
    send_sems, recv_sems, sources, landings, _ = started
    n_s, n_l = len(sources), len(landings)

    def body(*refs):
        src, land = refs[:n_s], refs[n_s:n_s + n_l]
        send_sems_ref, recv_sems_ref = refs[n_s + n_l], refs[n_s + n_l + 1]
        for k, (s, d, peer) in enumerate(plan(src, land, _position())):
            cp = _remote(s, d, send_sems_ref, recv_sems_ref, k, peer)
            cp.wait_send()
            cp.wait_recv()

    outs = pl.pallas_call(
        body, name=name, out_shape=tuple(pltpu.HBM(a.shape, a.dtype) for a in (*sources, *landings)),
        in_specs=[*[HBM] * (n_s + n_l), SEM, SEM, ANY], out_specs=tuple([HBM] * (n_s + n_l)),
        input_output_aliases={i: i for i in range(n_s + n_l)},
        compiler_params=pltpu.CompilerParams(has_side_effects=DATAFLOW_EFFECT),
    )(*sources, *landings, send_sems, recv_sems, after)
    return list(outs[:n_s]), list(outs[n_s:])


def _exchange(name, inputs, out_shapes, plan, in_place=False):
    n_in = len(inputs)
    n_out = n_in if in_place else len(out_shapes)
    n_copies = len(plan([None] * n_in, [None] * n_out, None))

    def body(*refs):
        ins, outs = refs[:n_in], refs[n_in:n_in + n_out]
        send_sems, recv_sems = refs[n_in + n_out:]
        copies = [_remote(s, d, send_sems, recv_sems, k, peer)
                  for k, (s, d, peer) in enumerate(plan(ins, ins if in_place else outs, _position()))]
        for cp in copies:
            cp.start()
        for cp in copies:
            cp.wait()

    if in_place:
        out_shapes = [jax.ShapeDtypeStruct(a.shape, a.dtype) for a in inputs]
    return pl.pallas_call(
        body, name=name, in_specs=[ANY] * n_in, out_specs=[ANY] * n_out, out_shape=out_shapes,
        input_output_aliases={i: i for i in range(n_in)} if in_place else {},
        scratch_shapes=[pltpu.SemaphoreType.DMA((n_copies,)), pltpu.SemaphoreType.DMA((n_copies,))],
    )(*inputs)


def _half_rows(ref, half):
    n = ref.shape[0] // 2
    return ref.at[pl.ds(half * n, n)]


def _gather_plan(sources, landings, pos):
    if pos is None:
        return [None] * (3 * len(sources))
    x, y, c = pos
    copies = []
    for src, land in zip(sources, landings):
        for qx, qy in _other_chips(x, y):
            copies.append((_half_rows(src, c), _half_rows(land.at[2 * x + y], c), (qx, qy, c)))
    return copies


def _sibling_plan(landings, _, pos):
    if pos is None:
        return [None] * (3 * len(landings))
    x, y, c = pos
    copies = []
    for land in landings:
        for qx, qy in _other_chips(x, y):
            region = _half_rows(land.at[2 * qx + qy], c)
            copies.append((region, region, (x, y, 1 - c)))
    return copies


def _weights_start(tag, shards):
    landings = [lax.empty((4, *s.shape), BF16) for s in shards]
    return _split_start(f"weights_start_{tag}", shards, landings, _gather_plan)


def _weights_finish(tag, names, started, after):
    shards, landings = _split_wait(f"weights_wait_{tag}", started, after, _gather_plan)
    landings = _exchange("weights_to_sibling", landings, None, _sibling_plan, in_place=True)
    my_shard = 2 * lax.axis_index("x") + lax.axis_index("y")
    full = {}
    for name, shard, land in zip(names, shards, landings):
        sa = BIG_AXES[name][0]
        land = lax.dynamic_update_index_in_dim(land, shard, my_shard, axis=0)
        if name in BY_CHIP:
            full[name] = land
            continue
        axis = sa - 1
        moved = jnp.moveaxis(land, 0, axis)
        full[name] = moved.reshape(*moved.shape[:axis], 4 * moved.shape[axis + 1], *moved.shape[axis + 2:])
    return full


def _half_shape(shape, half_axis):
    return tuple(n // 2 if a == half_axis else n for a, n in enumerate(shape))


def _shard_shape(shape, shard_axis):
    return tuple(n // 4 if a == shard_axis else n for a, n in enumerate(shape))


def _flat2d(a):
    return a.reshape(-1, a.shape[-1])


ELEMENTWISE_TILE_ELEMENTS = 1 << 18


def _row_tile(rows, cols):
    tile = 1
    while rows % (2 * tile) == 0 and 2 * tile * cols <= ELEMENTWISE_TILE_ELEMENTS:
        tile *= 2
    assert tile >= 16 or tile == rows, (rows, cols)
    return tile


def _window_of(arr, axis, parts, index):
    flat = _flat2d(arr)
    rows, cols = flat.shape
    if axis == arr.ndim - 1:
        return flat, (rows, cols // parts), (0, index)
    assert axis == 0, axis
    return flat, (rows // parts, cols), (index, 0)


def _window_sum(name, source, window, index, others, out_dtype):
    w_rows, w_cols = window
    tm = _row_tile(w_rows, w_cols)
    blocks = w_rows // tm
    where = jnp.stack([jnp.asarray(i, jnp.int32) for i in index])

    def body(where_ref, src_ref, *refs):
        acc = src_ref[...].astype(F32)
        for r in refs[:-1]:
            acc = acc + r[...].astype(F32)
        refs[-1][...] = acc.astype(refs[-1].dtype)

    same = pl.BlockSpec((tm, w_cols), lambda i, where_ref: (i, 0))
    grid_spec = pltpu.PrefetchScalarGridSpec(
        num_scalar_prefetch=1, grid=(blocks,),
        in_specs=[pl.BlockSpec((tm, w_cols), lambda i, where_ref: (where_ref[0] * blocks + i, where_ref[1])), *[same] * len(others)],
        out_specs=same)
    return pl.pallas_call(body, name=name, grid_spec=grid_spec, out_shape=jax.ShapeDtypeStruct(window, out_dtype),
                          compiler_params=_params("arbitrary"))(where, source, *others)


def _sum_arrays(name, arrays, out_dtype):
    shape = arrays[0].shape
    flat = [_flat2d(a) for a in arrays]
    rows, cols = flat[0].shape
    tm = _row_tile(rows, cols)

    def fn(*tiles):
        acc = tiles[0].astype(F32)
        for t in tiles[1:]:
            acc = acc + t.astype(F32)
        return acc

    (out,) = _rowwise(name, fn, [(a, cols, 0) for a in flat], [], [(cols, out_dtype)], tm=tm)
    return out.reshape(shape)


def _grads_start(tag, names, grads):
    c = lax.axis_index("c")
    shapes = [g.shape for g in grads]
    axes = [(BIG_AXES[n][0] - 1, BIG_AXES[n][1] - 1) for n in names]

    def to_sibling(ins, outs, pos):
        if pos is None:
            return [None] * len(names)
        px, py, pc = pos
        return [(_region(ins[wi], _cuts(shapes[wi], sa, ha, half=1 - pc)), outs[wi], (px, py, 1 - pc))
                for wi, (sa, ha) in enumerate(axes)]

    landed = _exchange("grads_to_sibling", grads, [jax.ShapeDtypeStruct(_half_shape(s, ha), BF16) for s, (_, ha) in zip(shapes, axes)],
                       to_sibling)
    chip_sums = []
    for wi, name in enumerate(names):
        source, window, index = _window_of(grads[wi], axes[wi][1], 2, c)
        chip_sum = _window_sum("chip_sum_" + name, source, window, index, [_flat2d(landed[wi])], BF16)
        chip_sums.append(chip_sum.reshape(landed[wi].shape))

    def to_chips(sources, landings, pos):
        if pos is None:
            return [None] * (3 * len(names))
        px, py, pc = pos
        copies = []
        for wi, (sa, ha) in enumerate(axes):
            for j, (qx, qy) in enumerate(_other_chips(px, py)):
                piece = _region(sources[wi], _cuts(shapes[wi], sa, ha, shard=2 * qx + qy))
                copies.append((piece, landings[wi].at[j], (qx, qy, pc)))
        return copies

    landings = [lax.empty((3, *_shard_shape(t.shape, sa)), BF16) for t, (sa, _) in zip(chip_sums, axes)]
    return _split_start(f"grads_start_{tag}", chip_sums, landings, to_chips), to_chips


def _grads_finish(tag, names, started, after):
    chip_sums, pieces = _split_wait(f"grads_wait_{tag}", started[0], after, started[1])
    my_shard = 2 * lax.axis_index("x") + lax.axis_index("y")
    reduced = {}
    for name, chip_sum, piece in zip(names, chip_sums, pieces):
        source, window, index = _window_of(chip_sum, BIG_AXES[name][0] - 1, 4, my_shard)
        total = _window_sum("shard_sum_" + name, source, window, index, [_flat2d(piece[j]) for j in range(3)], F32)
        reduced[name] = total.reshape(piece.shape[1:])
    return reduced


def _join_halves(halves):
    c = lax.axis_index("c")
    mine = [halves[l][name] for l in range(len(halves)) for name, _, _ in BIG]

    def halves_to_sibling(ins, outs, pos):
        if pos is None:
            return [None] * len(mine)
        px, py, pc = pos
        return [(ins[k], outs[k], (px, py, 1 - pc)) for k in range(len(mine))]

    theirs = _exchange("grad_halves_to_sibling", mine, [jax.ShapeDtypeStruct(r.shape, F32) for r in mine], halves_to_sibling)
    out = []
    for wi, (_, _, ha) in enumerate(BIG):
        layers = []
        for l in range(len(halves)):
            a, b = mine[l * len(BIG) + wi], theirs[l * len(BIG) + wi]
            layers.append(jnp.concatenate([jnp.where(c == 0, a, b), jnp.where(c == 0, b, a)], axis=ha - 1))
        out.append(jnp.stack(layers))
    return out


def _pack(arrays):
    flat = jnp.concatenate([a.reshape(-1).astype(F32) for a in arrays])
    rows = -(-flat.shape[0] // PACK_LANES)
    rows = -(-rows // PACK_ROW_MULTIPLE) * PACK_ROW_MULTIPLE
    return jnp.pad(flat, (0, rows * PACK_LANES - flat.shape[0])).reshape(rows, PACK_LANES)


def _unpack(buf, shapes):
    flat, out, off = buf.reshape(-1), [], 0
    for s in shapes:
        n = 1
        for d in s:
            n *= d
        out.append(flat[off:off + n].reshape(s))
        off += n
    return out


def _sum8(name, gathered, rows):
    def body(in_ref, out_ref):
        acc = in_ref[0].astype(F32)
        for d in range(1, 8):
            acc = acc + in_ref[d].astype(F32)
        out_ref[...] = acc

    return pl.pallas_call(
        body, name=name, grid=(rows // PACK_ROW_MULTIPLE,),
        in_specs=[pl.BlockSpec((8, PACK_ROW_MULTIPLE, PACK_LANES), lambda i: (0, i, 0))],
        out_specs=pl.BlockSpec((PACK_ROW_MULTIPLE, PACK_LANES), lambda i: (i, 0)),
        out_shape=jax.ShapeDtypeStruct((rows, PACK_LANES), F32), compiler_params=_params("arbitrary"),
    )(gathered.reshape(8, rows, PACK_LANES))


def _adamw(name, w, g, m, v):
    shape = w.shape
    flat = [_flat2d(a) for a in (w, g, m, v)]
    rows, cols = flat[0].shape
    tm = _row_tile(rows, cols)
    outs = _rowwise(name, _adamw_fn, [(a, cols, 0) for a in flat], [], [(cols, F32)] * 3, tm=tm)
    return [o.reshape(shape) for o in outs]


WEIGHT_ORDER = ("norm_mix", "w_in", "conf_dw", "conf_dw_b", "conf_ln_g", "conf_ln_b", "pool_w", "pool_scale", "sc_conv",
                "gmlp_ln_g", "gmlp_ln_b", "gmlp_ws", "gmlp_bs", "w_branch", "w_out", "norm_mlp", "w_up", "w_down",
                "norm_ple", "w_ple", "w_ple_gate", "norm_final")


def _local_step(x, p, loss_target, weights, layer_big, full_small_sharded, on_big_grads):
    n_layers = p.shape[0]
    d_model = x.shape[1]
    layer_w, saved = [], []
    h = x
    for l in range(n_layers):
        w = {n: weights[n][l] for n in SMALL_REP if n != "norm_final"}
        w.update({n: full_small_sharded[n][l] for n in SMALL_SHARDED})
        layer_w.append(w)
        h, s = _layer_fwd(h, p[l], w, layer_big(l))
        saved.append(s)
    dx, d_final, loss = _rowwise("loss", _loss_fn, [(h, d_model, 0), (loss_target, d_model, 0)], [_row(weights["norm_final"])],
                                 [(d_model, F32)], [(1, d_model), (1, PACK_LANES)])
    small_grads = {n: [None] * n_layers for n in SMALL_REP + SMALL_SHARDED if n != "norm_final"}
    for l in reversed(range(n_layers)):
        dx, small = _layer_bwd(dx, p[l], layer_w[l], saved[l], on_big_grads(l))
        for n in small:
            small_grads[n][l] = small[n].reshape(weights[n].shape[1:]) if n in SMALL_REP else small[n]
    small_grads = {n: jnp.stack(v) for n, v in small_grads.items()}
    small_grads["norm_final"] = d_final.reshape(-1)
    return loss, dx, small_grads


def kernel(x, p, norm_mix, w_in, conf_dw, conf_dw_b, conf_ln_g, conf_ln_b, pool_w, pool_scale, sc_conv, gmlp_ln_g, gmlp_ln_b, gmlp_ws, gmlp_bs, w_branch, w_out, norm_mlp, w_up, w_down, norm_ple, w_ple, w_ple_gate, norm_final, loss_target, m_norm_mix, m_w_in, m_conf_dw, m_conf_dw_b, m_conf_ln_g, m_conf_ln_b, m_pool_w, m_pool_scale, m_sc_conv, m_gmlp_ln_g, m_gmlp_ln_b, m_gmlp_ws, m_gmlp_bs, m_w_branch, m_w_out, m_norm_mlp, m_w_up, m_w_down, m_norm_ple, m_w_ple, m_w_ple_gate, m_norm_final, v_norm_mix, v_w_in, v_conf_dw, v_conf_dw_b, v_conf_ln_g, v_conf_ln_b, v_pool_w, v_pool_scale, v_sc_conv, v_gmlp_ln_g, v_gmlp_ln_b, v_gmlp_ws, v_gmlp_bs, v_w_branch, v_w_out, v_norm_mlp, v_w_up, v_w_down, v_norm_ple, v_w_ple, v_w_ple_gate, v_norm_final):
    given = dict(locals())
    weights = {n: given[n] for n in WEIGHT_ORDER}
    mom_m = {n: given["m_" + n] for n in WEIGHT_ORDER}
    mom_v = {n: given["v_" + n] for n in WEIGHT_ORDER}
    my_shard = 2 * lax.axis_index("x") + lax.axis_index("y")

    n_layers = p.shape[0]
    shard_shapes = [weights[n].shape for n in SMALL_SHARDED]
    gathered = _allgather_small("allgather_small_weights", _pack([weights[n] for n in SMALL_SHARDED]))
    after_small = gathered[0, 0] * 0.0
    after_small = jnp.where(after_small == 0.0, after_small, 0.0)

    weight_copies = [[None] * len(WEIGHT_GROUPS) for _ in range(n_layers)]
    started = after_small
    for l in range(n_layers):
        for g, names in enumerate(WEIGHT_GROUPS):
            shards = [(weights[n][l] + started if i == 0 else weights[n][l]).astype(BF16) for i, n in enumerate(names)]
            weight_copies[l][g] = _weights_start(f"{l}_{g}", shards)
            started = weight_copies[l][g][4][0, 0]
    per_dev = gathered.reshape(8, -1, PACK_LANES)
    full_small_sharded = {}
    chip_parts = [_unpack(per_dev[2 * s], shard_shapes) for s in range(4)]
    for i, n in enumerate(SMALL_SHARDED):
        full_small_sharded[n] = jnp.concatenate([chip_parts[s][i] for s in range(4)], axis=-1)

    grad_copies = [[None] * len(GRAD_GROUPS) for _ in range(n_layers)]

    def layer_big(l):
        return lambda g, after: _weights_finish(f"{l}_{g}", WEIGHT_GROUPS[g], weight_copies[l][g], after)

    def on_big_grads(l):
        def send(g, group_grads):
            grad_copies[l][g] = _grads_start(f"{l}_{g}", GRAD_GROUPS[g], group_grads)
            return grad_copies[l][g][0][4][0, 0]
        return send

    loss_row, dx, small_grads = _local_step(x[0] + started, p[:, 0], loss_target[0], weights, layer_big, full_small_sharded,
                                            on_big_grads)
    loss = lax.psum(loss_row[0, 0], MESH_AXES)

    halves = [{} for _ in range(n_layers)]
    for l in reversed(range(n_layers)):
        for g, names in enumerate(GRAD_GROUPS):
            halves[l].update(_grads_finish(f"{l}_{g}", names, grad_copies[l][g], dx))
    big_reduced = _join_halves(halves)
    grads = {n: g for (n, _, _), g in zip(BIG, big_reduced)}

    small_names = SMALL_REP + SMALL_SHARDED
    packed = _pack([small_grads[n] for n in small_names])
    rows = packed.shape[0]
    summed = _sum8("sum_small_grads", _allgather_small("allgather_small_grads", packed.astype(BF16)), rows)
    small_full_shapes = [weights[n].shape for n in SMALL_REP] + [full_small_sharded[n].shape for n in SMALL_SHARDED]
    for n, g in zip(small_names, _unpack(summed, small_full_shapes)):
        if n in SMALL_SHARDED:
            width = weights[n].shape[-1]
            g = lax.dynamic_slice_in_dim(g, my_shard * width, width, axis=g.ndim - 1)
        grads[n] = g

    delta, new_m, new_v = {}, {}, {}
    for n, _, _ in BIG:
        delta[n], new_m[n], new_v[n] = _adamw("adamw_" + n, weights[n], grads[n], mom_m[n], mom_v[n])
    for group, names in (("rep", SMALL_REP), ("sharded", SMALL_SHARDED)):
        outs = _adamw("adamw_small_" + group, *[_pack([src[n] for n in names]) for src in (weights, grads, mom_m, mom_v)])
        shapes = [weights[n].shape for n in names]
        for dst, buf in zip((delta, new_m, new_v), outs):
            for n, a in zip(names, _unpack(buf, shapes)):
                dst[n] = a

    return (loss, dx[None], *[grads[n] for n in WEIGHT_ORDER], *[delta[n] for n in WEIGHT_ORDER],
            *[new_m[n] for n in WEIGHT_ORDER], *[new_v[n] for n in WEIGHT_ORDER])
```

```python
import functools

import jax
import jax.numpy as jnp
from jax import lax
from jax.experimental import pallas as pl
from jax.experimental.pallas import tpu as pltpu

F32, BF16 = jnp.float32, jnp.bfloat16
EPS = 1e-6
ADAM_LR, ADAM_B1, ADAM_B2, ADAM_EPS, ADAM_WD, ADAM_STEP = 0.001, 0.9, 0.999, 1e-08, 0.01, 10
CONF_KERNEL, SC_KERNEL = 31, 3
POOL_WINDOWS = (2, 4, 8, 16)
MIX = 512
GROUP = 128
HALO = 32
SEQ_TILE = 256
BRANCH_ROWS = 512
VMEM_LIMIT_BYTES = 48 * 1024 * 1024
MESH_AXES = ("x", "y", "c")
ANY = pl.BlockSpec(memory_space=pl.ANY)

BIG = (("w_in", 2, 1), ("w_branch", 3, 1), ("w_out", 1, 2), ("w_up", 2, 1), ("w_down", 1, 2),
       ("w_ple", 2, 1), ("w_ple_gate", 1, 2))
SMALL_REP = ("norm_mix", "conf_dw_b", "conf_ln_g", "conf_ln_b", "pool_w", "pool_scale", "gmlp_ln_g",
             "gmlp_ln_b", "gmlp_ws", "gmlp_bs", "norm_mlp", "norm_ple", "norm_final")
SMALL_SHARDED = ("conf_dw", "sc_conv")
BY_CHIP = ("w_in", "w_up")
BIG_AXES = {name: (shard_axis, half_axis) for name, shard_axis, half_axis in BIG}
WEIGHT_GROUPS = (("w_in",), ("w_branch", "w_out", "w_up", "w_down", "w_ple_gate", "w_ple"))
GRAD_GROUPS = (("w_ple", "w_ple_gate", "w_down", "w_up"), ("w_out", "w_branch", "w_in"))
PACK_LANES = 128
PACK_ROW_MULTIPLE = 256


def _params(*semantics):
    return pltpu.CompilerParams(dimension_semantics=semantics, vmem_limit_bytes=VMEM_LIMIT_BYTES)


def _accumulate(ref, value, first):
    @pl.when(first)
    def _():
        ref[...] = value.astype(ref.dtype)

    @pl.when(jnp.logical_not(first))
    def _():
        ref[...] += value.astype(ref.dtype)


def _rowwise(name, fn, rows, params, outs, accs=(), tm=256):
    n_rows = rows[0][0].shape[-2]
    tm = min(tm, n_rows)
    n_in, n_out = len(rows) + len(params), len(outs)
    targets = [(k, o[2]) for k, o in enumerate(outs) if len(o) == 4]

    def body(*refs):
        res = fn(*[r[...] for r in refs[:n_in]])
        res = res if isinstance(res, (tuple, list)) else (res,)
        out_refs = refs[n_in + len(targets):]
        for r, v in zip(out_refs[:n_out], res[:n_out]):
            r[...] = v.astype(r.dtype)
        first = pl.program_id(0) == 0
        for r, v in zip(out_refs[n_out:], res[n_out:]):
            _accumulate(r, v, first)

    in_specs = []
    for row in rows:
        arr, w, cb = row[:3]
        if len(row) == 4:
            in_specs.append(pl.BlockSpec((None, tm, w), lambda i, cb=cb, g=row[3]: (g, i, cb)))
        else:
            in_specs.append(pl.BlockSpec((tm, w), lambda i, cb=cb: (i, cb)))
    for p in params:
        in_specs.append(pl.BlockSpec(p.shape, lambda i, nd=p.ndim: (0,) * nd))
    in_specs += [ANY] * len(targets)
    out_specs = [pl.BlockSpec((tm, o[0]), lambda i, cb=(o[3] if len(o) == 4 else 0): (i, cb)) for o in outs]
    out_specs += [pl.BlockSpec(s, lambda i, nd=len(s): (0,) * nd) for s in accs]
    out_shape = [jax.ShapeDtypeStruct(o[2].shape if len(o) == 4 else (n_rows, o[0]), o[1]) for o in outs]
    out_shape += [jax.ShapeDtypeStruct(s, F32) for s in accs]
    return pl.pallas_call(
        body, name=name, grid=(n_rows // tm,), in_specs=in_specs, out_specs=out_specs, out_shape=out_shape,
        input_output_aliases={n_in + t: k for t, (k, _) in enumerate(targets)},
        compiler_params=_params("arbitrary"),
    )(*[r[0] for r in rows], *params, *[t for _, t in targets])


ROW_OUT_SUBLANES = 8


def _mm(name, a, b, *, ta=False, tb=False, b_by_chip=False, chips=(0, 4), res=(), rows=(), epi=None, out_dtypes=(F32,), n_row_outs=0,
        tm=1024, tn=1024, tk=1024):
    g_, m_, k_ = (a.shape[0], a.shape[2], a.shape[1]) if ta else a.shape
    if b_by_chip:
        n_ = b.shape[2] if tb else chips[1] * b.shape[3]
    else:
        n_ = b.shape[1] if tb else b.shape[2]
    tm, tn, tk = min(tm, m_), min(tn, n_), min(tk, k_)
    if b_by_chip:
        tn, tk = (tn, min(tk, b.shape[3])) if tb else (min(tn, b.shape[3]), tk)
        per_chip = b.shape[3] // (tk if tb else tn)
    nk, n_res, n_rows, n_out = k_ // tk, len(res), len(rows), len(out_dtypes)
    n_in = 2 + n_res + n_rows
    dims = (((0 if ta else 1,), (1 if tb else 0,)), ((), ()))
    if epi is None:
        epi = lambda acc: (acc,)

    def body(*refs):
        a_ref, b_ref = refs[:2]
        res_refs, out_refs, acc_ref = refs[2:n_in], refs[n_in:n_in + n_out], refs[-1]
        row_out_refs = refs[n_in + n_out:-1]
        k = pl.program_id(3)

        @pl.when(k == 0)
        def _():
            acc_ref[...] = jnp.zeros_like(acc_ref)

        acc_ref[...] += lax.dot_general(a_ref[...].astype(BF16), b_ref[...].astype(BF16), dims,
                                        preferred_element_type=F32)

        @pl.when(k == nk - 1)
        def _():
            vals = epi(acc_ref[...], *[r[...] for r in res_refs])
            for o, v in zip(out_refs, vals):
                o[...] = v.astype(o.dtype)
            for o, v in zip(row_out_refs, vals[n_out:]):
                o[...] = jnp.broadcast_to(v, o.shape)

    a_spec = (pl.BlockSpec((None, tk, tm), lambda g, i, j, k: (g, k, i)) if ta
              else pl.BlockSpec((None, tm, tk), lambda g, i, j, k: (g, i, k)))
    if b_by_chip and tb:
        b_spec = pl.BlockSpec((None, None, tn, tk), lambda g, i, j, k: (g, k // per_chip, j, k % per_chip))
    elif b_by_chip:
        b_spec = pl.BlockSpec((None, None, tk, tn), lambda g, i, j, k: (g, chips[0] + j // per_chip, k, j % per_chip))
    elif tb:
        b_spec = pl.BlockSpec((None, tn, tk), lambda g, i, j, k: (g, j, k))
    else:
        b_spec = pl.BlockSpec((None, tk, tn), lambda g, i, j, k: (g, k, j))
    res_specs = [pl.BlockSpec((None, tm, tn), lambda g, i, j, k, off=off: (g, i, j + off)) for _, off in res]
    row_specs = [pl.BlockSpec((None, 1, tn), lambda g, i, j, k: (g, 0, j))] * n_rows
    out_spec = pl.BlockSpec((None, tm, tn), lambda g, i, j, k: (g, i, j))
    row_out_spec = pl.BlockSpec((None, ROW_OUT_SUBLANES, tn), lambda g, i, j, k: (g, i, j))
    outs = pl.pallas_call(
        body, name=name, grid=(g_, m_ // tm, n_ // tn, nk),
        in_specs=[a_spec, b_spec, *res_specs, *row_specs], out_specs=[out_spec] * n_out + [row_out_spec] * n_row_outs,
        out_shape=[jax.ShapeDtypeStruct((g_, m_, n_), d) for d in out_dtypes]
        + [jax.ShapeDtypeStruct((g_, ROW_OUT_SUBLANES * (m_ // tm), n_), F32)] * n_row_outs,
        scratch_shapes=[pltpu.VMEM((tm, tn), F32)],
        compiler_params=_params("parallel", "parallel", "parallel", "arbitrary"),
    )(a, b, *[r for r, _ in res], *rows)
    return outs


def _mm2(name, a, b, *, res=(), rows=(), **kw):
    outs = _mm(name, a[None], b[None], res=[(r[None], off) for r, off in res], rows=[r[None] for r in rows], **kw)
    return [o[0] for o in outs]


def _sum_row_out(row_out):
    return jnp.sum(row_out[::ROW_OUT_SUBLANES], axis=0, keepdims=True)


@jax.custom_vjp
def _bdot(a, b):
    return jnp.dot(a.astype(BF16), b.astype(BF16), preferred_element_type=F32)


def _bdot_fwd(a, b):
    return _bdot(a, b), (a, b)


def _bdot_bwd(saved, g):
    a, b = saved
    gb = g.astype(BF16)
    da = lax.dot_general(gb, b.astype(BF16), (((1,), (1,)), ((), ())), preferred_element_type=F32)
    db = lax.dot_general(a.astype(BF16), gb, (((0,), (0,)), ((), ())), preferred_element_type=F32)
    return da.astype(a.dtype), db.astype(b.dtype)


_bdot.defvjp(_bdot_fwd, _bdot_bwd)


def _rms(x, g):
    return x * lax.rsqrt(jnp.mean(x * x, axis=-1, keepdims=True) + EPS) * g


def _layer_norm(x, g, b):
    mu = jnp.mean(x, axis=-1, keepdims=True)
    var = jnp.mean(jnp.square(x - mu), axis=-1, keepdims=True)
    return (x - mu) * lax.rsqrt(var + EPS) * g + b


def _glu_and_gate_fn(a, a_gate, cg, hx):
    return a * jax.nn.sigmoid(a_gate), cg * hx


def _branches_fn(ca, pooled, cc, bg, u, v, conf_b, conf_g, conf_beta, pool_scale, gm_g, gm_b, *mats):
    pool_w, ws, bs = mats[0:4], mats[4:8], mats[8:12]
    ya = jax.nn.silu(_layer_norm(ca + conf_b, conf_g, conf_beta))
    yb = jnp.concatenate([_bdot(pooled[:, g * GROUP:(g + 1) * GROUP], pool_w[g]) for g in range(4)], axis=1) * pool_scale
    yc = bg * cc
    vln = _layer_norm(v, gm_g, gm_b)
    causal = lax.broadcasted_iota(jnp.int32, (GROUP, GROUP), 0) >= lax.broadcasted_iota(jnp.int32, (GROUP, GROUP), 1)
    chunks = v.shape[0] // GROUP

    def spatial(g):
        vg = vln[:, g * GROUP:(g + 1) * GROUP]
        wide = jnp.concatenate([vg[c * GROUP:(c + 1) * GROUP, :] for c in range(chunks)], axis=1)
        out = _bdot(jnp.where(causal, ws[g], 0.0), wide) + bs[g]
        return jnp.concatenate([out[:, c * GROUP:(c + 1) * GROUP] for c in range(chunks)], axis=0)

    sg = jnp.concatenate([spatial(g) for g in range(4)], axis=1)
    yd = u * sg
    return ya, yb, yc, yd


def _branches_bwd_fn(*args):
    ins, dys = args[:6] + args[10:], args[6:10]
    _, vjp = jax.vjp(_branches_fn, *ins)
    grads = vjp(tuple(dys))
    duv = jnp.concatenate([grads[4], grads[5]], axis=1)
    return (*grads[:4], duv, *grads[6:])


def _glu_bwd_fn(a, a_gate, cg, hx, dga, dchx, dpin, dbg):
    _, vjp = jax.vjp(_glu_and_gate_fn, a, a_gate, cg, hx)
    da, dag, dcg, dhx = vjp((dga, dchx))
    return jnp.concatenate([da, dag, dpin.astype(F32), dbg.astype(F32), dcg, dhx], axis=1)


def _rms_bwd_epilogue(dh, x, dres, g):
    _, vjp = jax.vjp(_rms, x, g)
    dx, dg = vjp(dh)
    return dx + dres, dg


def _residual_then_rms(acc, residual, g):
    x = acc + residual
    return x, _rms(x, g)


def _loss_fn(x, target, g):
    def f(x, g):
        err = jnp.square(_rms(x, g) - target)
        return 0.5 * jnp.sum(jnp.mean(err, axis=-1, keepdims=True), axis=0, keepdims=True)

    loss, vjp = jax.vjp(f, x, g)
    dx, dg = vjp(jnp.ones((1, 1), F32))
    return dx, dg, jnp.broadcast_to(loss, (1, PACK_LANES))


def _adamw_fn(w, g, m, v):
    m = ADAM_B1 * m + (1.0 - ADAM_B1) * g
    v = ADAM_B2 * v + (1.0 - ADAM_B2) * jnp.square(g)
    m_hat = m / (1.0 - ADAM_B1 ** ADAM_STEP)
    v_hat = v / (1.0 - ADAM_B2 ** ADAM_STEP)
    delta = -ADAM_LR * (m_hat / (jnp.sqrt(v_hat) + ADAM_EPS) + ADAM_WD * w)
    return delta, m, v


def _group_pick(j):
    return [(j == g).astype(F32) for g in range(len(POOL_WINDOWS))]


SUBLANES = 8


def _causal_taps(x, taps):
    for r in range(min(SUBLANES, taps)):
        behind = x if r == 0 else pltpu.roll(x, r, 0)
        for q in range((taps - 1 - r) // SUBLANES + 1):
            start = HALO - SUBLANES * q
            yield SUBLANES * q + r, behind[start:start + SEQ_TILE, :]


def _rows_iota(n):
    return lax.broadcasted_iota(jnp.int32, (n, 1), 0)


def _seq_specs(n_time, causal):
    per = SEQ_TILE // HALO
    tile = pl.BlockSpec((SEQ_TILE, GROUP), lambda j, i: (i, j))
    if causal:
        halo = pl.BlockSpec((HALO, GROUP), lambda j, i: (jnp.maximum(i * per - 1, 0), j))
    else:
        halo = pl.BlockSpec((HALO, GROUP), lambda j, i: (jnp.minimum((i + 1) * per, n_time * per - 1), j))
    return tile, halo


def _seq_fwd(ga, proj, chx, conf_dw, sc_conv):
    s_len = ga.shape[0]
    n_time = s_len // SEQ_TILE
    ext = HALO + SEQ_TILE

    def body(ga_ref, ga_h, pin_ref, pin_h, chx_ref, chx_h, wc_ref, ws_ref, ca_ref, po_ref, cc_ref):
        j, i = pl.program_id(0), pl.program_id(1)
        keep = (i > 0).astype(F32)

        def extended(cur, halo):
            return jnp.concatenate([halo[...] * keep, cur[...]], axis=0)

        def conv(x, w_ref, taps):
            acc = None
            for d, rows in _causal_taps(x, taps):
                term = rows * w_ref[pl.ds(taps - 1 - d, 1), :]
                acc = term if acc is None else acc + term
            return acc

        ca_ref[...] = conv(extended(ga_ref, ga_h), wc_ref, CONF_KERNEL)
        cc_ref[...] = conv(extended(chx_ref, chx_h), ws_ref, SC_KERNEL)
        u = extended(pin_ref, pin_h)
        s1 = u + pltpu.roll(u, 1, 0)
        s2 = s1 + pltpu.roll(s1, 2, 0)
        s3 = s2 + pltpu.roll(s2, 4, 0)
        s4 = s3 + pltpu.roll(s3, 8, 0)
        pick = _group_pick(j)
        window_sum = s1 * pick[0] + s2 * pick[1] + s3 * pick[2] + s4 * pick[3]
        window = sum(float(w) * f for w, f in zip(POOL_WINDOWS, pick))
        pos = (_rows_iota(ext) + (i * SEQ_TILE - HALO + 1)).astype(F32)
        count = jnp.maximum(jnp.minimum(pos, window), 1.0)
        po_ref[...] = (window_sum / count - u)[HALO:, :]

    tile, halo = _seq_specs(n_time, causal=True)
    pin_tile = pl.BlockSpec((SEQ_TILE, GROUP), lambda j, i: (i, 2 * (MIX // GROUP) + j))
    pin_halo = pl.BlockSpec((HALO, GROUP), lambda j, i: (jnp.maximum(i * (SEQ_TILE // HALO) - 1, 0), 2 * (MIX // GROUP) + j))
    w_spec = lambda taps: pl.BlockSpec((taps, GROUP), lambda j, i: (0, j))
    return pl.pallas_call(
        body, name="seq_fwd", grid=(MIX // GROUP, n_time),
        in_specs=[tile, halo, pin_tile, pin_halo, tile, halo, w_spec(CONF_KERNEL), w_spec(SC_KERNEL)],
        out_specs=[tile] * 3, out_shape=[jax.ShapeDtypeStruct((s_len, MIX), F32)] * 3,
        compiler_params=_params("parallel", "arbitrary"),
    )(ga, ga, proj, proj, chx, chx, conf_dw, sc_conv)


def _seq_bwd(dca, dpooled, dcc, ga, chx, conf_dw, sc_conv):
    s_len = ga.shape[0]
    n_time = s_len // SEQ_TILE
    ext = HALO + SEQ_TILE

    def body(dca_ref, dca_h, dpo_ref, dpo_h, dcc_ref, dcc_h, ga_ref, ga_h, chx_ref, chx_h, wc_ref, ws_ref,
             dga_ref, dpin_ref, dchx_ref, dwc_ref, dws_ref, acc_c, acc_s):
        j, i = pl.program_id(0), pl.program_id(1)
        keep_prev = (i > 0).astype(F32)
        keep_next = (i < n_time - 1).astype(F32)

        def with_next(cur, halo):
            return jnp.concatenate([cur[...], halo[...] * keep_next], axis=0)

        def with_prev(cur, halo):
            return jnp.concatenate([halo[...] * keep_prev, cur[...]], axis=0)

        def conv_t(dy, w_ref, taps):
            acc = None
            for r in range(min(SUBLANES, taps)):
                ahead = dy if r == 0 else pltpu.roll(dy, ext - r, 0)
                for q in range((taps - 1 - r) // SUBLANES + 1):
                    term = ahead[SUBLANES * q:SUBLANES * q + SEQ_TILE, :] * w_ref[pl.ds(taps - 1 - (SUBLANES * q + r), 1), :]
                    acc = term if acc is None else acc + term
            return acc

        def tap_grads(dy_cur, x, acc_ref, dw_ref, taps):
            @pl.when(i == 0)
            def _():
                acc_ref[...] = jnp.zeros_like(acc_ref)

            for d, rows in _causal_taps(x, taps):
                prod = dy_cur * rows
                part = prod[:SUBLANES, :]
                for g in range(1, SEQ_TILE // SUBLANES):
                    part = part + prod[g * SUBLANES:(g + 1) * SUBLANES, :]
                acc_ref[pl.ds((taps - 1 - d) * SUBLANES, SUBLANES), :] += part

            @pl.when(i == n_time - 1)
            def _():
                for k in range(taps):
                    dw_ref[pl.ds(k, 1), :] = jnp.sum(acc_ref[pl.ds(k * SUBLANES, SUBLANES), :], axis=0, keepdims=True)

        dga_ref[...] = conv_t(with_next(dca_ref, dca_h), wc_ref, CONF_KERNEL)
        dchx_ref[...] = conv_t(with_next(dcc_ref, dcc_h), ws_ref, SC_KERNEL)
        tap_grads(dca_ref[...], with_prev(ga_ref, ga_h), acc_c, dwc_ref, CONF_KERNEL)
        tap_grads(dcc_ref[...], with_prev(chx_ref, chx_h), acc_s, dws_ref, SC_KERNEL)

        dpo = with_next(dpo_ref, dpo_h)
        pick = _group_pick(j)
        window = sum(float(w) * f for w, f in zip(POOL_WINDOWS, pick))
        pos = (_rows_iota(ext) + (i * SEQ_TILE + 1)).astype(F32)
        r = dpo / jnp.minimum(pos, window)
        q1 = r + pltpu.roll(r, ext - 1, 0)
        q2 = q1 + pltpu.roll(q1, ext - 2, 0)
        q3 = q2 + pltpu.roll(q2, ext - 4, 0)
        q4 = q3 + pltpu.roll(q3, ext - 8, 0)
        q = q1 * pick[0] + q2 * pick[1] + q3 * pick[2] + q4 * pick[3]
        dpin_ref[...] = (q - dpo)[:SEQ_TILE, :].astype(dpin_ref.dtype)

    tile, nxt = _seq_specs(n_time, causal=False)
    _, prv = _seq_specs(n_time, causal=True)
    w_spec = lambda taps: pl.BlockSpec((taps, GROUP), lambda j, i: (0, j))
    return pl.pallas_call(
        body, name="seq_bwd", grid=(MIX // GROUP, n_time),
        in_specs=[tile, nxt, tile, nxt, tile, nxt, tile, prv, tile, prv, w_spec(CONF_KERNEL), w_spec(SC_KERNEL)],
        out_specs=[tile, tile, tile, w_spec(CONF_KERNEL), w_spec(SC_KERNEL)],
        out_shape=[jax.ShapeDtypeStruct((s_len, MIX), F32), jax.ShapeDtypeStruct((s_len, MIX), BF16),
                   jax.ShapeDtypeStruct((s_len, MIX), F32), jax.ShapeDtypeStruct(conf_dw.shape, F32),
                   jax.ShapeDtypeStruct(sc_conv.shape, F32)],
        scratch_shapes=[pltpu.VMEM((CONF_KERNEL * SUBLANES, GROUP), F32), pltpu.VMEM((SC_KERNEL * SUBLANES, GROUP), F32)],
        compiler_params=_params("parallel", "arbitrary"),
    )(dca, dca, dpooled, dpooled, dcc, dcc, ga, ga, chx, chx, conf_dw, sc_conv)


MERGE_TILE = 512
MERGE_ROWS = 1024


def _merge_fwd(y, w_branch, gates):
    s_len, d_model = y.shape[1], w_branch.shape[2]
    t = min(MERGE_ROWS, s_len)
    per_branch = d_model // MERGE_TILE

    def body(y_ref, w_ref, gate_ref, out_ref, acc_ref):
        k = pl.program_id(2)

        @pl.when(k == 0)
        def _():
            acc_ref[...] = jnp.zeros_like(acc_ref)

        z = jnp.dot(y_ref[...], w_ref[...], preferred_element_type=F32)
        acc_ref[...] += gate_ref[...].astype(F32) * z

        @pl.when(k == 3)
        def _():
            out_ref[...] = acc_ref[...].astype(out_ref.dtype)

    return pl.pallas_call(
        body, name="merge_fwd", grid=(s_len // t, per_branch, 4),
        in_specs=[pl.BlockSpec((None, t, MIX), lambda i, j, k: (k, i, 0)),
                  pl.BlockSpec((None, MIX, MERGE_TILE), lambda i, j, k: (k, 0, j)),
                  pl.BlockSpec((t, MERGE_TILE), lambda i, j, k: (i, k * per_branch + j))],
        out_specs=pl.BlockSpec((t, MERGE_TILE), lambda i, j, k: (i, j)),
        out_shape=jax.ShapeDtypeStruct((s_len, d_model), BF16),
        scratch_shapes=[pltpu.VMEM((t, MERGE_TILE), F32)],
        compiler_params=_params("parallel", "parallel", "arbitrary"),
    )(y, w_branch, gates)


def _merge_bwd(y, w_branch, gates, dmerged):
    s_len, d_model = y.shape[1], w_branch.shape[2]
    t = min(MERGE_ROWS, s_len)
    per_branch = d_model // MERGE_TILE

    def body(y_ref, w_ref, gate_ref, dm_ref, dz_ref, dgate_ref):
        z = jnp.dot(y_ref[...], w_ref[...], preferred_element_type=F32)
        gate = gate_ref[...].astype(F32)
        dm = dm_ref[...]
        dz_ref[...] = (dm * gate).astype(dz_ref.dtype)
        dgate_ref[...] = (dm * z * gate * (1.0 - gate)).astype(dgate_ref.dtype)

    return pl.pallas_call(
        body, name="merge_bwd", grid=(s_len // t, per_branch, 4),
        in_specs=[pl.BlockSpec((None, t, MIX), lambda i, j, k: (k, i, 0)),
                  pl.BlockSpec((None, MIX, MERGE_TILE), lambda i, j, k: (k, 0, j)),
                  pl.BlockSpec((t, MERGE_TILE), lambda i, j, k: (i, k * per_branch + j)),
                  pl.BlockSpec((t, MERGE_TILE), lambda i, j, k: (i, j))],
        out_specs=[pl.BlockSpec((None, t, MERGE_TILE), lambda i, j, k: (k, i, j)),
                   pl.BlockSpec((t, MERGE_TILE), lambda i, j, k: (i, (4 + k) * per_branch + j))],
        out_shape=[jax.ShapeDtypeStruct((4, s_len, d_model), BF16), jax.ShapeDtypeStruct((s_len, 8 * d_model), BF16)],
        compiler_params=_params("parallel", "parallel", "arbitrary"),
    )(y, w_branch, gates, dmerged)


def _row(v):
    return v.reshape(1, -1)


def _branch_params(w):
    mats = [w["pool_w"][g] for g in range(4)] + [w["gmlp_ws"][g] for g in range(4)]
    mats += [w["gmlp_bs"][g].reshape(GROUP, 1) for g in range(4)]
    return [_row(w["conf_dw_b"]), _row(w["conf_ln_g"]), _row(w["conf_ln_b"]), _row(w["pool_scale"]),
            _row(w["gmlp_ln_g"]), _row(w["gmlp_ln_b"])] + mats


def _layer_fwd(x0, p_l, w, fetch):
    d_model = x0.shape[1]
    (h1,) = _rowwise("rms_mix", _rms, [(x0, d_model, 0)], [_row(w["norm_mix"])], [(d_model, BF16)])
    w.update(fetch(0, h1))
    (proj,) = _mm2("proj", h1, w["w_in"], b_by_chip=True, chips=(0, 2))
    w.update(fetch(1, proj))
    (gates,) = _mm2("proj_gates", h1, w["w_in"], b_by_chip=True, chips=(2, 2), out_dtypes=(BF16,),
                    epi=lambda acc: (jax.nn.sigmoid(acc),))
    ga, chx = _rowwise("glu", _glu_and_gate_fn, [(proj, MIX, 0), (proj, MIX, 1), (proj, MIX, 4), (proj, MIX, 5)], [],
                       [(MIX, F32), (MIX, F32)])
    ca, pooled, cc = _seq_fwd(ga, proj, chx, w["conf_dw"], w["sc_conv"])
    ys = _rowwise("branches", _branches_fn,
                  [(ca, MIX, 0), (pooled, MIX, 0), (cc, MIX, 0), (proj, MIX, 3), (proj, MIX, 6), (proj, MIX, 7)],
                  _branch_params(w), [(MIX, BF16)] * 4, tm=BRANCH_ROWS)
    y = jnp.stack(ys)
    merged = _merge_fwd(y, w["w_branch"], gates)
    x1, h2 = _mm2("out_proj", merged, w["w_out"], res=[(x0, 0)], rows=[_row(w["norm_mlp"])], out_dtypes=(F32, BF16),
                  epi=_residual_then_rms)
    (act,) = _mm2("mlp_up", h2, w["w_up"], b_by_chip=True, out_dtypes=(BF16,),
                  epi=lambda acc: (jnp.square(jnp.maximum(acc, 0.0)),))
    x2, h3 = _mm2("mlp_down", act, w["w_down"], res=[(x1, 0)], rows=[_row(w["norm_ple"])], out_dtypes=(F32, BF16),
                  epi=_residual_then_rms)
    (pg,) = _mm2("ple_gate", h3, w["w_ple_gate"])
    (x3,) = _mm2("ple", p_l, w["w_ple"], res=[(pg, 0), (x2, 0)],
                 epi=lambda acc, g, r: (acc * jax.nn.sigmoid(g) + r,))
    saved = dict(x0=x0, h1=h1, proj=proj, gates=gates, ga=ga, chx=chx, ca=ca, pooled=pooled, cc=cc, y=y, merged=merged, x1=x1,
                 h2=h2, act=act, x2=x2, h3=h3, pg=pg)
    return x3, saved


def _layer_bwd(dx3, p_l, w, s, send):
    d_model = dx3.shape[1]
    big, small = {}, {}

    def ple_epi(acc, dx, g):
        sig = jax.nn.sigmoid(g)
        return dx * sig, dx * acc * sig * (1.0 - sig)

    dpp, dpg = _mm2("ple_bwd", p_l, w["w_ple"], res=[(dx3, 0), (s["pg"], 0)], epi=ple_epi, out_dtypes=(BF16, BF16))
    (big["w_ple"],) = _mm2("dw_ple", p_l, dpp, ta=True, out_dtypes=(BF16,))
    (big["w_ple_gate"],) = _mm2("dw_ple_gate", s["h3"], dpg, ta=True, out_dtypes=(BF16,))
    dx2, dg = _mm2("dh_ple", dpg, w["w_ple_gate"], tb=True, res=[(s["x2"], 0), (dx3, 0)], rows=[_row(w["norm_ple"])],
                   epi=_rms_bwd_epilogue, n_row_outs=1)
    small["norm_ple"] = _sum_row_out(dg)

    (dup,) = _mm2("mlp_down_bwd", dx2, w["w_down"], tb=True, res=[(s["act"], 0)], out_dtypes=(BF16,),
                  epi=lambda acc, act: (acc * 2.0 * jnp.sqrt(act.astype(F32)),))
    (big["w_down"],) = _mm2("dw_down", s["act"], dx2, ta=True, out_dtypes=(BF16,))
    (big["w_up"],) = _mm2("dw_up", s["h2"], dup, ta=True, out_dtypes=(BF16,))
    sent = send(0, [big[n] for n in GRAD_GROUPS[0]])
    dx1, dg = _mm2("mlp_up_bwd", dup, w["w_up"], tb=True, b_by_chip=True, res=[(s["x1"], 0), (dx2, 0)],
                   rows=[_row(w["norm_mlp"]) + sent], epi=_rms_bwd_epilogue, n_row_outs=1)
    small["norm_mlp"] = _sum_row_out(dg)

    (dmerged,) = _mm2("out_proj_bwd", dx1, w["w_out"], tb=True)
    (big["w_out"],) = _mm2("dw_out", s["merged"], dx1, ta=True, out_dtypes=(BF16,))
    dz, dproj = _merge_bwd(s["y"], w["w_branch"], s["gates"], dmerged)
    (dy,) = _mm("branch_bwd", dz, w["w_branch"], tb=True)
    (big["w_branch"],) = _mm("dw_branch", s["y"], dz, ta=True, out_dtypes=(BF16,))

    proj = s["proj"]
    params = _branch_params(w)
    grads = _rowwise(
        "branches_bwd", _branches_bwd_fn,
        [(s["ca"], MIX, 0), (s["pooled"], MIX, 0), (s["cc"], MIX, 0), (proj, MIX, 3), (proj, MIX, 6), (proj, MIX, 7)]
        + [(dy, MIX, 0, g) for g in range(4)], params,
        [(MIX, F32), (MIX, F32), (MIX, F32), (MIX, BF16), (2 * MIX, BF16, dproj, 3)], [p.shape for p in params], tm=BRANCH_ROWS)
    dca, dpooled, dcc, dbg, dproj = grads[:5]
    pg_ = grads[5:]
    small["conf_dw_b"], small["conf_ln_g"], small["conf_ln_b"], small["pool_scale"] = pg_[0], pg_[1], pg_[2], pg_[3]
    small["gmlp_ln_g"], small["gmlp_ln_b"] = pg_[4], pg_[5]
    small["pool_w"] = jnp.stack(pg_[6:10])
    small["gmlp_ws"] = jnp.stack(pg_[10:14])
    small["gmlp_bs"] = jnp.stack([b.reshape(GROUP) for b in pg_[14:18]])

    dga, dpin, dchx, small["conf_dw"], small["sc_conv"] = _seq_bwd(dca, dpooled, dcc, s["ga"], s["chx"], w["conf_dw"], w["sc_conv"])
    (dproj,) = _rowwise("glu_bwd", _glu_bwd_fn,
                        [(proj, MIX, 0), (proj, MIX, 1), (proj, MIX, 4), (proj, MIX, 5), (dga, MIX, 0), (dchx, MIX, 0),
                         (dpin, MIX, 0), (dbg, MIX, 0)], [], [(6 * MIX, BF16, dproj, 0)])
    (big["w_in"],) = _mm2("dw_in", s["h1"], dproj, ta=True, out_dtypes=(BF16,))
    sent = send(1, [big[n] for n in GRAD_GROUPS[1]])
    dx0, dg = _mm2("proj_bwd", dproj, w["w_in"], tb=True, b_by_chip=True, res=[(s["x0"], 0), (dx1, 0)],
                   rows=[_row(w["norm_mix"]) + sent], epi=_rms_bwd_epilogue, n_row_outs=1)
    small["norm_mix"] = _sum_row_out(dg)
    return dx0, small


def _position():
    return lax.axis_index("x"), lax.axis_index("y"), lax.axis_index("c")


def _other_chips(x, y):
    return [(1 - x, y), (x, 1 - y), (1 - x, 1 - y)]


def _region(ref, cuts):
    return ref.at[tuple(pl.ds(*cuts[a]) if a in cuts else slice(None) for a in range(len(ref.shape)))]


def _allgather_small(name, block):
    m_per, n = block.shape

    def body(x_ref, out_ref, send_sems, recv_sems, local_sem):
        x, y, c = _position()
        me, sibling = (x, y, c), (x, y, 1 - c)
        chips = _other_chips(x, y)

        def rows(px, py, pc):
            return out_ref.at[pl.ds((4 * px + 2 * py + pc) * m_per, m_per), :]

        def copy(k, blk, to, src=None):
            return pltpu.make_async_remote_copy(
                src_ref=rows(*blk) if src is None else src, dst_ref=rows(*blk), send_sem=send_sems.at[k],
                recv_sem=recv_sems.at[k], device_id=to, device_id_type=pl.DeviceIdType.MESH)

        mine = pltpu.make_async_copy(x_ref, rows(*me), local_sem)
        mine.start()
        first = [copy(0, me, sibling, src=x_ref)]
        first += [copy(1 + j, me, (*chip, c), src=x_ref) for j, chip in enumerate(chips)]
        for cp in first:
            cp.start()
        passed = [copy(4 + j, (*chip, c), sibling) for j, chip in enumerate(chips)]
        for j, chip in enumerate(chips):
            copy(1 + j, (*chip, c), me).wait_recv()
            passed[j].start()
        copy(0, sibling, me).wait_recv()
        for j, chip in enumerate(chips):
            copy(4 + j, (*chip, 1 - c), me).wait_recv()
        for cp in first + passed:
            cp.wait_send()
        mine.wait()

    return pl.pallas_call(
        body, name=name, out_shape=jax.ShapeDtypeStruct((8 * m_per, n), block.dtype), in_specs=[ANY], out_specs=ANY,
        scratch_shapes=[pltpu.SemaphoreType.DMA((7,)), pltpu.SemaphoreType.DMA((7,)), pltpu.SemaphoreType.DMA],
    )(block)


def _cuts(arr_shape, shard_axis, half_axis, shard=None, half=None):
    cuts = {}
    if shard is not None:
        n = arr_shape[shard_axis] // 4
        cuts[shard_axis] = (shard * n, n)
    if half is not None:
        n = arr_shape[half_axis] // 2
        cuts[half_axis] = (half * n, n)
    return cuts


def _full_shape(shard_shape, shard_axis):
    return tuple(4 * n if a == shard_axis else n for a, n in enumerate(shard_shape))


HBM = pl.BlockSpec(memory_space=pltpu.HBM)
SEM = pl.BlockSpec(memory_space=pltpu.SEMAPHORE)
DATAFLOW_EFFECT = pltpu.SideEffectType.DATAFLOW_SIDE_EFFECTING
TOKEN_SHAPE = (8, 128)


def _remote(src, dst, send_sems, recv_sems, k, peer):
    return pltpu.make_async_remote_copy(src_ref=src, dst_ref=dst, send_sem=send_sems.at[k], recv_sem=recv_sems.at[k],
                                        device_id=peer, device_id_type=pl.DeviceIdType.MESH)


def _split_start(name, sources, landings, plan):
    n_s, n_l = len(sources), len(landings)
    n_copies = len(plan([None] * n_s, [None] * n_l, None))

    def body(*refs):
        src, land = refs[:n_s], refs[n_s:n_s + n_l]
        send_sems, recv_sems, token = refs[n_s + n_l], refs[n_s + n_l + 1], refs[-1]
        for k, (s, d, peer) in enumerate(plan(src, land, _position())):
            _remote(s, d, send_sems, recv_sems, k, peer).start()
        token[...] = jnp.zeros_like(token)

    arrays = [pltpu.with_memory_space_constraint(a, pltpu.HBM) for a in (*sources, *landings)]
    outs = pl.pallas_call(
        body, name=name,
        out_shape=(pltpu.SemaphoreType.DMA((n_copies,)), pltpu.SemaphoreType.DMA((n_copies,)),
                   *[pltpu.HBM(a.shape, a.dtype) for a in arrays], jax.ShapeDtypeStruct(TOKEN_SHAPE, F32)),
        in_specs=[HBM] * (n_s + n_l),
        out_specs=(SEM, SEM, *[HBM] * (n_s + n_l), pl.BlockSpec(memory_space=pltpu.VMEM)),
        input_output_aliases={i: 2 + i for i in range(n_s + n_l)},
        compiler_params=pltpu.CompilerParams(has_side_effects=DATAFLOW_EFFECT),
    )(*arrays)
    return outs[0], outs[1], list(outs[2:2 + n_s]), list(outs[2 + n_s:2 + n_s + n_l]), outs[-1]


def _split_wait(name, started, after, plan):
    send_sems, recv_sems, sources, landings, _ = started
    n_s, n_l = len(sources), len(landings)

    def body(*refs):
        src, land = refs[:n_s], refs[n_s:n_s + n_l]
        send_sems_ref, recv_sems_ref = refs[n_s + n_l], refs[n_s + n_l + 1]
        for k, (s, d, peer) in enumerate(plan(src, land, _position())):
            cp = _remote(s, d, send_sems_ref, recv_sems_ref, k, peer)
            cp.wait_send()
            cp.wait_recv()

    outs = pl.pallas_call(
        body, name=name, out_shape=tuple(pltpu.HBM(a.shape, a.dtype) for a in (*sources, *landings)),
        in_specs=[*[HBM] * (n_s + n_l), SEM, SEM, ANY], out_specs=tuple([HBM] * (n_s + n_l)),
        input_output_aliases={i: i for i in range(n_s + n_l)},
        compiler_params=pltpu.CompilerParams(has_side_effects=DATAFLOW_EFFECT),
    )(*sources, *landings, send_sems, recv_sems, after)
    return list(outs[:n_s]), list(outs[n_s:])


def _exchange(name, inputs, out_shapes, plan, in_place=False):
    n_in = len(inputs)
    n_out = n_in if in_place else len(out_shapes)
    n_copies = len(plan([None] * n_in, [None] * n_out, None))

    def body(*refs):
        ins, outs = refs[:n_in], refs[n_in:n_in + n_out]
        send_sems, recv_sems = refs[n_in + n_out:]
        copies = [_remote(s, d, send_sems, recv_sems, k, peer)
                  for k, (s, d, peer) in enumerate(plan(ins, ins if in_place else outs, _position()))]
        for cp in copies:
            cp.start()
        for cp in copies:
            cp.wait()

    if in_place:
        out_shapes = [jax.ShapeDtypeStruct(a.shape, a.dtype) for a in inputs]
    return pl.pallas_call(
        body, name=name, in_specs=[ANY] * n_in, out_specs=[ANY] * n_out, out_shape=out_shapes,
        input_output_aliases={i: i for i in range(n_in)} if in_place else {},
        scratch_shapes=[pltpu.SemaphoreType.DMA((n_copies,)), pltpu.SemaphoreType.DMA((n_copies,))],
    )(*inputs)


def _half_rows(ref, half):
    n = ref.shape[0] // 2
    return ref.at[pl.ds(half * n, n)]


def _gather_plan(sources, landings, pos):
    if pos is None:
        return [None] * (3 * len(sources))
    x, y, c = pos
    copies = []
    for src, land in zip(sources, landings):
        for qx, qy in _other_chips(x, y):
            copies.append((_half_rows(src, c), _half_rows(land.at[2 * x + y], c), (qx, qy, c)))
    return copies


def _sibling_plan(landings, _, pos):
    if pos is None:
        return [None] * (3 * len(landings))
    x, y, c = pos
    copies = []
    for land in landings:
        for qx, qy in _other_chips(x, y):
            region = _half_rows(land.at[2 * qx + qy], c)
            copies.append((region, region, (x, y, 1 - c)))
    return copies


def _weights_start(tag, shards):
    landings = [lax.empty((4, *s.shape), BF16) for s in shards]
    return _split_start(f"weights_start_{tag}", shards, landings, _gather_plan)


def _weights_finish(tag, names, started, after):
    shards, landings = _split_wait(f"weights_wait_{tag}", started, after, _gather_plan)
    landings = _exchange("weights_to_sibling", landings, None, _sibling_plan, in_place=True)
    my_shard = 2 * lax.axis_index("x") + lax.axis_index("y")
    full = {}
    for name, shard, land in zip(names, shards, landings):
        sa = BIG_AXES[name][0]
        land = lax.dynamic_update_index_in_dim(land, shard, my_shard, axis=0)
        if name in BY_CHIP:
            full[name] = land
            continue
        axis = sa - 1
        moved = jnp.moveaxis(land, 0, axis)
        full[name] = moved.reshape(*moved.shape[:axis], 4 * moved.shape[axis + 1], *moved.shape[axis + 2:])
    return full


def _half_shape(shape, half_axis):
    return tuple(n // 2 if a == half_axis else n for a, n in enumerate(shape))


def _shard_shape(shape, shard_axis):
    return tuple(n // 4 if a == shard_axis else n for a, n in enumerate(shape))


def _flat2d(a):
    return a.reshape(-1, a.shape[-1])


ELEMENTWISE_TILE_ELEMENTS = 1 << 18


def _row_tile(rows, cols):
    tile = 1
    while rows % (2 * tile) == 0 and 2 * tile * cols <= ELEMENTWISE_TILE_ELEMENTS:
        tile *= 2
    assert tile >= 16 or tile == rows, (rows, cols)
    return tile


def _window_of(arr, axis, parts, index):
    flat = _flat2d(arr)
    rows, cols = flat.shape
    if axis == arr.ndim - 1:
        return flat, (rows, cols // parts), (0, index)
    assert axis == 0, axis
    return flat, (rows // parts, cols), (index, 0)


def _window_sum(name, source, window, index, others, out_dtype):
    w_rows, w_cols = window
    tm = _row_tile(w_rows, w_cols)
    blocks = w_rows // tm
    where = jnp.stack([jnp.asarray(i, jnp.int32) for i in index])

    def body(where_ref, src_ref, *refs):
        acc = src_ref[...].astype(F32)
        for r in refs[:-1]:
            acc = acc + r[...].astype(F32)
        refs[-1][...] = acc.astype(refs[-1].dtype)

    same = pl.BlockSpec((tm, w_cols), lambda i, where_ref: (i, 0))
    grid_spec = pltpu.PrefetchScalarGridSpec(
        num_scalar_prefetch=1, grid=(blocks,),
        in_specs=[pl.BlockSpec((tm, w_cols), lambda i, where_ref: (where_ref[0] * blocks + i, where_ref[1])), *[same] * len(others)],
        out_specs=same)
    return pl.pallas_call(body, name=name, grid_spec=grid_spec, out_shape=jax.ShapeDtypeStruct(window, out_dtype),
                          compiler_params=_params("arbitrary"))(where, source, *others)


def _sum_arrays(name, arrays, out_dtype):
    shape = arrays[0].shape
    flat = [_flat2d(a) for a in arrays]
    rows, cols = flat[0].shape
    tm = _row_tile(rows, cols)

    def fn(*tiles):
        acc = tiles[0].astype(F32)
        for t in tiles[1:]:
            acc = acc + t.astype(F32)
        return acc

    (out,) = _rowwise(name, fn, [(a, cols, 0) for a in flat], [], [(cols, out_dtype)], tm=tm)
    return out.reshape(shape)


def _grads_start(tag, names, grads):
    c = lax.axis_index("c")
    shapes = [g.shape for g in grads]
    axes = [(BIG_AXES[n][0] - 1, BIG_AXES[n][1] - 1) for n in names]

    def to_sibling(ins, outs, pos):
        if pos is None:
            return [None] * len(names)
        px, py, pc = pos
        return [(_region(ins[wi], _cuts(shapes[wi], sa, ha, half=1 - pc)), outs[wi], (px, py, 1 - pc))
                for wi, (sa, ha) in enumerate(axes)]

    landed = _exchange("grads_to_sibling", grads, [jax.ShapeDtypeStruct(_half_shape(s, ha), BF16) for s, (_, ha) in zip(shapes, axes)],
                       to_sibling)
    chip_sums = []
    for wi, name in enumerate(names):
        source, window, index = _window_of(grads[wi], axes[wi][1], 2, c)
        chip_sum = _window_sum("chip_sum_" + name, source, window, index, [_flat2d(landed[wi])], BF16)
        chip_sums.append(chip_sum.reshape(landed[wi].shape))

    def to_chips(sources, landings, pos):
        if pos is None:
            return [None] * (3 * len(names))
        px, py, pc = pos
        copies = []
        for wi, (sa, ha) in enumerate(axes):
            for j, (qx, qy) in enumerate(_other_chips(px, py)):
                piece = _region(sources[wi], _cuts(shapes[wi], sa, ha, shard=2 * qx + qy))
                copies.append((piece, landings[wi].at[j], (qx, qy, pc)))
        return copies

    landings = [lax.empty((3, *_shard_shape(t.shape, sa)), BF16) for t, (sa, _) in zip(chip_sums, axes)]
    return _split_start(f"grads_start_{tag}", chip_sums, landings, to_chips), to_chips


def _grads_finish(tag, names, started, after):
    chip_sums, pieces = _split_wait(f"grads_wait_{tag}", started[0], after, started[1])
    my_shard = 2 * lax.axis_index("x") + lax.axis_index("y")
    reduced = {}
    for name, chip_sum, piece in zip(names, chip_sums, pieces):
        source, window, index = _window_of(chip_sum, BIG_AXES[name][0] - 1, 4, my_shard)
        total = _window_sum("shard_sum_" + name, source, window, index, [_flat2d(piece[j]) for j in range(3)], F32)
        reduced[name] = total.reshape(piece.shape[1:])
    return reduced


def _join_halves(halves):
    c = lax.axis_index("c")
    mine = [halves[l][name] for l in range(len(halves)) for name, _, _ in BIG]

    def halves_to_sibling(ins, outs, pos):
        if pos is None:
            return [None] * len(mine)
        px, py, pc = pos
        return [(ins[k], outs[k], (px, py, 1 - pc)) for k in range(len(mine))]

    theirs = _exchange("grad_halves_to_sibling", mine, [jax.ShapeDtypeStruct(r.shape, F32) for r in mine], halves_to_sibling)
    out = []
    for wi, (_, _, ha) in enumerate(BIG):
        layers = []
        for l in range(len(halves)):
            a, b = mine[l * len(BIG) + wi], theirs[l * len(BIG) + wi]
            layers.append(jnp.concatenate([jnp.where(c == 0, a, b), jnp.where(c == 0, b, a)], axis=ha - 1))
        out.append(jnp.stack(layers))
    return out


def _pack(arrays):
    flat = jnp.concatenate([a.reshape(-1).astype(F32) for a in arrays])
    rows = -(-flat.shape[0] // PACK_LANES)
    rows = -(-rows // PACK_ROW_MULTIPLE) * PACK_ROW_MULTIPLE
    return jnp.pad(flat, (0, rows * PACK_LANES - flat.shape[0])).reshape(rows, PACK_LANES)


def _unpack(buf, shapes):
    flat, out, off = buf.reshape(-1), [], 0
    for s in shapes:
        n = 1
        for d in s:
            n *= d
        out.append(flat[off:off + n].reshape(s))
        off += n
    return out


def _sum8(name, gathered, rows):
    def body(in_ref, out_ref):
        acc = in_ref[0].astype(F32)
        for d in range(1, 8):
            acc = acc + in_ref[d].astype(F32)
        out_ref[...] = acc

    return pl.pallas_call(
        body, name=name, grid=(rows // PACK_ROW_MULTIPLE,),
        in_specs=[pl.BlockSpec((8, PACK_ROW_MULTIPLE, PACK_LANES), lambda i: (0, i, 0))],
        out_specs=pl.BlockSpec((PACK_ROW_MULTIPLE, PACK_LANES), lambda i: (i, 0)),
        out_shape=jax.ShapeDtypeStruct((rows, PACK_LANES), F32), compiler_params=_params("arbitrary"),
    )(gathered.reshape(8, rows, PACK_LANES))


def _adamw(name, w, g, m, v):
    shape = w.shape
    flat = [_flat2d(a) for a in (w, g, m, v)]
    rows, cols = flat[0].shape
    tm = _row_tile(rows, cols)
    outs = _rowwise(name, _adamw_fn, [(a, cols, 0) for a in flat], [], [(cols, F32)] * 3, tm=tm)
    return [o.reshape(shape) for o in outs]


WEIGHT_ORDER = ("norm_mix", "w_in", "conf_dw", "conf_dw_b", "conf_ln_g", "conf_ln_b", "pool_w", "pool_scale", "sc_conv",
                "gmlp_ln_g", "gmlp_ln_b", "gmlp_ws", "gmlp_bs", "w_branch", "w_out", "norm_mlp", "w_up", "w_down",
                "norm_ple", "w_ple", "w_ple_gate", "norm_final")


def _local_step(x, p, loss_target, weights, layer_big, full_small_sharded, on_big_grads):
    n_layers = p.shape[0]
    d_model = x.shape[1]
    layer_w, saved = [], []
    h = x
    for l in range(n_layers):
        w = {n: weights[n][l] for n in SMALL_REP if n != "norm_final"}
        w.update({n: full_small_sharded[n][l] for n in SMALL_SHARDED})
        layer_w.append(w)
        h, s = _layer_fwd(h, p[l], w, layer_big(l))
        saved.append(s)
    dx, d_final, loss = _rowwise("loss", _loss_fn, [(h, d_model, 0), (loss_target, d_model, 0)], [_row(weights["norm_final"])],
                                 [(d_model, F32)], [(1, d_model), (1, PACK_LANES)])
    small_grads = {n: [None] * n_layers for n in SMALL_REP + SMALL_SHARDED if n != "norm_final"}
    for l in reversed(range(n_layers)):
        dx, small = _layer_bwd(dx, p[l], layer_w[l], saved[l], on_big_grads(l))
        for n in small:
            small_grads[n][l] = small[n].reshape(weights[n].shape[1:]) if n in SMALL_REP else small[n]
    small_grads = {n: jnp.stack(v) for n, v in small_grads.items()}
    small_grads["norm_final"] = d_final.reshape(-1)
    return loss, dx, small_grads


def kernel(x, p, norm_mix, w_in, conf_dw, conf_dw_b, conf_ln_g, conf_ln_b, pool_w, pool_scale, sc_conv, gmlp_ln_g, gmlp_ln_b, gmlp_ws, gmlp_bs, w_branch, w_out, norm_mlp, w_up, w_down, norm_ple, w_ple, w_ple_gate, norm_final, loss_target, m_norm_mix, m_w_in, m_conf_dw, m_conf_dw_b, m_conf_ln_g, m_conf_ln_b, m_pool_w, m_pool_scale, m_sc_conv, m_gmlp_ln_g, m_gmlp_ln_b, m_gmlp_ws, m_gmlp_bs, m_w_branch, m_w_out, m_norm_mlp, m_w_up, m_w_down, m_norm_ple, m_w_ple, m_w_ple_gate, m_norm_final, v_norm_mix, v_w_in, v_conf_dw, v_conf_dw_b, v_conf_ln_g, v_conf_ln_b, v_pool_w, v_pool_scale, v_sc_conv, v_gmlp_ln_g, v_gmlp_ln_b, v_gmlp_ws, v_gmlp_bs, v_w_branch, v_w_out, v_norm_mlp, v_w_up, v_w_down, v_norm_ple, v_w_ple, v_w_ple_gate, v_norm_final):
    given = dict(locals())
    weights = {n: given[n] for n in WEIGHT_ORDER}
    mom_m = {n: given["m_" + n] for n in WEIGHT_ORDER}
    mom_v = {n: given["v_" + n] for n in WEIGHT_ORDER}
    my_shard = 2 * lax.axis_index("x") + lax.axis_index("y")

    n_layers = p.shape[0]
    shard_shapes = [weights[n].shape for n in SMALL_SHARDED]
    gathered = _allgather_small("allgather_small_weights", _pack([weights[n] for n in SMALL_SHARDED]))
    after_small = gathered[0, 0] * 0.0
    after_small = jnp.where(after_small == 0.0, after_small, 0.0)

    weight_copies = [[None] * len(WEIGHT_GROUPS) for _ in range(n_layers)]
    started = after_small
    for l in range(n_layers):
        for g, names in enumerate(WEIGHT_GROUPS):
            shards = [(weights[n][l] + started if i == 0 else weights[n][l]).astype(BF16) for i, n in enumerate(names)]
            weight_copies[l][g] = _weights_start(f"{l}_{g}", shards)
            started = weight_copies[l][g][4][0, 0]
    per_dev = gathered.reshape(8, -1, PACK_LANES)
    full_small_sharded = {}
    chip_parts = [_unpack(per_dev[2 * s], shard_shapes) for s in range(4)]
    for i, n in enumerate(SMALL_SHARDED):
        full_small_sharded[n] = jnp.concatenate([chip_parts[s][i] for s in range(4)], axis=-1)

    grad_copies = [[None] * len(GRAD_GROUPS) for _ in range(n_layers)]

    def layer_big(l):
        return lambda g, after: _weights_finish(f"{l}_{g}", WEIGHT_GROUPS[g], weight_copies[l][g], after)

    def on_big_grads(l):
        def send(g, group_grads):
            grad_copies[l][g] = _grads_start(f"{l}_{g}", GRAD_GROUPS[g], group_grads)
            return grad_copies[l][g][0][4][0, 0]
        return send

    loss_row, dx, small_grads = _local_step(x[0] + started, p[:, 0], loss_target[0], weights, layer_big, full_small_sharded,
                                            on_big_grads)
    loss = lax.psum(loss_row[0, 0], MESH_AXES)

    halves = [{} for _ in range(n_layers)]
    for l in reversed(range(n_layers)):
        for g, names in enumerate(GRAD_GROUPS):
            halves[l].update(_grads_finish(f"{l}_{g}", names, grad_copies[l][g], dx))
    big_reduced = _join_halves(halves)
    grads = {n: g for (n, _, _), g in zip(BIG, big_reduced)}

    small_names = SMALL_REP + SMALL_SHARDED
    packed = _pack([small_grads[n] for n in small_names])
    rows = packed.shape[0]
    summed = _sum8("sum_small_grads", _allgather_small("allgather_small_grads", packed.astype(BF16)), rows)
    small_full_shapes = [weights[n].shape for n in SMALL_REP] + [full_small_sharded[n].shape for n in SMALL_SHARDED]
    for n, g in zip(small_names, _unpack(summed, small_full_shapes)):
        if n in SMALL_SHARDED:
            width = weights[n].shape[-1]
            g = lax.dynamic_slice_in_dim(g, my_shard * width, width, axis=g.ndim - 1)
        grads[n] = g

    delta, new_m, new_v = {}, {}, {}
    for n, _, _ in BIG:
        delta[n], new_m[n], new_v[n] = _adamw("adamw_" + n, weights[n], grads[n], mom_m[n], mom_v[n])
    for group, names in (("rep", SMALL_REP), ("sharded", SMALL_SHARDED)):
        outs = _adamw("adamw_small_" + group, *[_pack([src[n] for n in names]) for src in (weights, grads, mom_m, mom_v)])
        shapes = [weights[n].shape for n in names]
        for dst, buf in zip((delta, new_m, new_v), outs):
            for n, a in zip(names, _unpack(buf, shapes)):
                dst[n] = a

    return (loss, dx[None], *[grads[n] for n in WEIGHT_ORDER], *[delta[n] for n in WEIGHT_ORDER],
            *[new_m[n] for n in WEIGHT_ORDER], *[new_v[n] for n in WEIGHT_ORDER])
```

```python
import functools

import jax
import jax.numpy as jnp
from jax import lax
from jax.experimental import pallas as pl
from jax.experimental.pallas import tpu as pltpu

F32, BF16 = jnp.float32, jnp.bfloat16
EPS = 1e-6
ADAM_LR, ADAM_B1, ADAM_B2, ADAM_EPS, ADAM_WD, ADAM_STEP = 0.001, 0.9, 0.999, 1e-08, 0.01, 10
CONF_KERNEL, SC_KERNEL = 31, 3
POOL_WINDOWS = (2, 4, 8, 16)
MIX = 512
GROUP = 128
HALO = 32
SEQ_TILE = 256
BRANCH_ROWS = 512
VMEM_LIMIT_BYTES = 48 * 1024 * 1024
MESH_AXES = ("x", "y", "c")
ANY = pl.BlockSpec(memory_space=pl.ANY)

BIG = (("w_in", 2, 1), ("w_branch", 3, 1), ("w_out", 1, 2), ("w_up", 2, 1), ("w_down", 1, 2),
       ("w_ple", 2, 1), ("w_ple_gate", 1, 2))
SMALL_REP = ("norm_mix", "conf_dw_b", "conf_ln_g", "conf_ln_b", "pool_w", "pool_scale", "gmlp_ln_g",
             "gmlp_ln_b", "gmlp_ws", "gmlp_bs", "norm_mlp", "norm_ple", "norm_final")
SMALL_SHARDED = ("conf_dw", "sc_conv")
BY_CHIP = ("w_in", "w_up")
BIG_AXES = {name: (shard_axis, half_axis) for name, shard_axis, half_axis in BIG}
WEIGHT_GROUPS = (("w_in",), ("w_branch", "w_out", "w_up", "w_down", "w_ple_gate", "w_ple"))
GRAD_GROUPS = (("w_ple", "w_ple_gate", "w_down", "w_up"), ("w_out", "w_branch", "w_in"))
PACK_LANES = 128
PACK_ROW_MULTIPLE = 256


def _params(*semantics):
    return pltpu.CompilerParams(dimension_semantics=semantics, vmem_limit_bytes=VMEM_LIMIT_BYTES)


def _accumulate(ref, value, first):
    @pl.when(first)
    def _():
        ref[...] = value.astype(ref.dtype)

    @pl.when(jnp.logical_not(first))
    def _():
        ref[...] += value.astype(ref.dtype)


def _rowwise(name, fn, rows, params, outs, accs=(), tm=256):
    n_rows = rows[0][0].shape[-2]
    tm = min(tm, n_rows)
    n_in, n_out = len(rows) + len(params), len(outs)
    targets = [(k, o[2]) for k, o in enumerate(outs) if len(o) == 4]

    def body(*refs):
        res = fn(*[r[...] for r in refs[:n_in]])
        res = res if isinstance(res, (tuple, list)) else (res,)
        out_refs = refs[n_in + len(targets):]
        for r, v in zip(out_refs[:n_out], res[:n_out]):
            r[...] = v.astype(r.dtype)
        first = pl.program_id(0) == 0
        for r, v in zip(out_refs[n_out:], res[n_out:]):
            _accumulate(r, v, first)

    in_specs = []
    for row in rows:
        arr, w, cb = row[:3]
        if len(row) == 4:
            in_specs.append(pl.BlockSpec((None, tm, w), lambda i, cb=cb, g=row[3]: (g, i, cb)))
        else:
            in_specs.append(pl.BlockSpec((tm, w), lambda i, cb=cb: (i, cb)))
    for p in params:
        in_specs.append(pl.BlockSpec(p.shape, lambda i, nd=p.ndim: (0,) * nd))
    in_specs += [ANY] * len(targets)
    out_specs = [pl.BlockSpec((tm, o[0]), lambda i, cb=(o[3] if len(o) == 4 else 0): (i, cb)) for o in outs]
    out_specs += [pl.BlockSpec(s, lambda i, nd=len(s): (0,) * nd) for s in accs]
    out_shape = [jax.ShapeDtypeStruct(o[2].shape if len(o) == 4 else (n_rows, o[0]), o[1]) for o in outs]
    out_shape += [jax.ShapeDtypeStruct(s, F32) for s in accs]
    return pl.pallas_call(
        body, name=name, grid=(n_rows // tm,), in_specs=in_specs, out_specs=out_specs, out_shape=out_shape,
        input_output_aliases={n_in + t: k for t, (k, _) in enumerate(targets)},
        compiler_params=_params("arbitrary"),
    )(*[r[0] for r in rows], *params, *[t for _, t in targets])


ROW_OUT_SUBLANES = 8


def _mm(name, a, b, *, ta=False, tb=False, b_by_chip=False, chips=(0, 4), res=(), rows=(), epi=None, out_dtypes=(F32,), n_row_outs=0,
        tm=1024, tn=1024, tk=1024):
    g_, m_, k_ = (a.shape[0], a.shape[2], a.shape[1]) if ta else a.shape
    if b_by_chip:
        n_ = b.shape[2] if tb else chips[1] * b.shape[3]
    else:
        n_ = b.shape[1] if tb else b.shape[2]
    tm, tn, tk = min(tm, m_), min(tn, n_), min(tk, k_)
    if b_by_chip:
        tn, tk = (tn, min(tk, b.shape[3])) if tb else (min(tn, b.shape[3]), tk)
        per_chip = b.shape[3] // (tk if tb else tn)
    nk, n_res, n_rows, n_out = k_ // tk, len(res), len(rows), len(out_dtypes)
    n_in = 2 + n_res + n_rows
    dims = (((0 if ta else 1,), (1 if tb else 0,)), ((), ()))
    if epi is None:
        epi = lambda acc: (acc,)

    def body(*refs):
        a_ref, b_ref = refs[:2]
        res_refs, out_refs, acc_ref = refs[2:n_in], refs[n_in:n_in + n_out], refs[-1]
        row_out_refs = refs[n_in + n_out:-1]
        k = pl.program_id(3)

        @pl.when(k == 0)
        def _():
            acc_ref[...] = jnp.zeros_like(acc_ref)

        acc_ref[...] += lax.dot_general(a_ref[...].astype(BF16), b_ref[...].astype(BF16), dims,
                                        preferred_element_type=F32)

        @pl.when(k == nk - 1)
        def _():
            vals = epi(acc_ref[...], *[r[...] for r in res_refs])
            for o, v in zip(out_refs, vals):
                o[...] = v.astype(o.dtype)
            for o, v in zip(row_out_refs, vals[n_out:]):
                o[...] = jnp.broadcast_to(v, o.shape)

    a_spec = (pl.BlockSpec((None, tk, tm), lambda g, i, j, k: (g, k, i)) if ta
              else pl.BlockSpec((None, tm, tk), lambda g, i, j, k: (g, i, k)))
    if b_by_chip and tb:
        b_spec = pl.BlockSpec((None, None, tn, tk), lambda g, i, j, k: (g, k // per_chip, j, k % per_chip))
    elif b_by_chip:
        b_spec = pl.BlockSpec((None, None, tk, tn), lambda g, i, j, k: (g, chips[0] + j // per_chip, k, j % per_chip))
    elif tb:
        b_spec = pl.BlockSpec((None, tn, tk), lambda g, i, j, k: (g, j, k))
    else:
        b_spec = pl.BlockSpec((None, tk, tn), lambda g, i, j, k: (g, k, j))
    res_specs = [pl.BlockSpec((None, tm, tn), lambda g, i, j, k, off=off: (g, i, j + off)) for _, off in res]
    row_specs = [pl.BlockSpec((None, 1, tn), lambda g, i, j, k: (g, 0, j))] * n_rows
    out_spec = pl.BlockSpec((None, tm, tn), lambda g, i, j, k: (g, i, j))
    row_out_spec = pl.BlockSpec((None, ROW_OUT_SUBLANES, tn), lambda g, i, j, k: (g, i, j))
    outs = pl.pallas_call(
        body, name=name, grid=(g_, m_ // tm, n_ // tn, nk),
        in_specs=[a_spec, b_spec, *res_specs, *row_specs], out_specs=[out_spec] * n_out + [row_out_spec] * n_row_outs,
        out_shape=[jax.ShapeDtypeStruct((g_, m_, n_), d) for d in out_dtypes]
        + [jax.ShapeDtypeStruct((g_, ROW_OUT_SUBLANES * (m_ // tm), n_), F32)] * n_row_outs,
        scratch_shapes=[pltpu.VMEM((tm, tn), F32)],
        compiler_params=_params("parallel", "parallel", "parallel", "arbitrary"),
    )(a, b, *[r for r, _ in res], *rows)
    return outs


def _mm2(name, a, b, *, res=(), rows=(), **kw):
    outs = _mm(name, a[None], b[None], res=[(r[None], off) for r, off in res], rows=[r[None] for r in rows], **kw)
    return [o[0] for o in outs]


def _sum_row_out(row_out):
    return jnp.sum(row_out[::ROW_OUT_SUBLANES], axis=0, keepdims=True)


@jax.custom_vjp
def _bdot(a, b):
    return jnp.dot(a.astype(BF16), b.astype(BF16), preferred_element_type=F32)


def _bdot_fwd(a, b):
    return _bdot(a, b), (a, b)


def _bdot_bwd(saved, g):
    a, b = saved
    gb = g.astype(BF16)
    da = lax.dot_general(gb, b.astype(BF16), (((1,), (1,)), ((), ())), preferred_element_type=F32)
    db = lax.dot_general(a.astype(BF16), gb, (((0,), (0,)), ((), ())), preferred_element_type=F32)
    return da.astype(a.dtype), db.astype(b.dtype)


_bdot.defvjp(_bdot_fwd, _bdot_bwd)


def _rms(x, g):
    return x * lax.rsqrt(jnp.mean(x * x, axis=-1, keepdims=True) + EPS) * g


def _layer_norm(x, g, b):
    mu = jnp.mean(x, axis=-1, keepdims=True)
    var = jnp.mean(jnp.square(x - mu), axis=-1, keepdims=True)
    return (x - mu) * lax.rsqrt(var + EPS) * g + b


def _glu_and_gate_fn(a, a_gate, cg, hx):
    return a * jax.nn.sigmoid(a_gate), cg * hx


def _branches_fn(ca, pooled, cc, bg, u, v, conf_b, conf_g, conf_beta, pool_scale, gm_g, gm_b, *mats):
    pool_w, ws, bs = mats[0:4], mats[4:8], mats[8:12]
    ya = jax.nn.silu(_layer_norm(ca + conf_b, conf_g, conf_beta))
    yb = jnp.concatenate([_bdot(pooled[:, g * GROUP:(g + 1) * GROUP], pool_w[g]) for g in range(4)], axis=1) * pool_scale
    yc = bg * cc
    vln = _layer_norm(v, gm_g, gm_b)
    causal = lax.broadcasted_iota(jnp.int32, (GROUP, GROUP), 0) >= lax.broadcasted_iota(jnp.int32, (GROUP, GROUP), 1)
    chunks = v.shape[0] // GROUP

    def spatial(g):
        vg = vln[:, g * GROUP:(g + 1) * GROUP]
        wide = jnp.concatenate([vg[c * GROUP:(c + 1) * GROUP, :] for c in range(chunks)], axis=1)
        out = _bdot(jnp.where(causal, ws[g], 0.0), wide) + bs[g]
        return jnp.concatenate([out[:, c * GROUP:(c + 1) * GROUP] for c in range(chunks)], axis=0)

    sg = jnp.concatenate([spatial(g) for g in range(4)], axis=1)
    yd = u * sg
    return ya, yb, yc, yd


def _branches_bwd_fn(*args):
    ins, dys = args[:6] + args[10:], args[6:10]
    _, vjp = jax.vjp(_branches_fn, *ins)
    grads = vjp(tuple(dys))
    duv = jnp.concatenate([grads[4], grads[5]], axis=1)
    return (*grads[:4], duv, *grads[6:])


def _glu_bwd_fn(a, a_gate, cg, hx, dga, dchx, dpin, dbg):
    _, vjp = jax.vjp(_glu_and_gate_fn, a, a_gate, cg, hx)
    da, dag, dcg, dhx = vjp((dga, dchx))
    return jnp.concatenate([da, dag, dpin.astype(F32), dbg.astype(F32), dcg, dhx], axis=1)


def _rms_bwd_epilogue(dh, x, dres, g):
    _, vjp = jax.vjp(_rms, x, g)
    dx, dg = vjp(dh)
    return dx + dres, dg


def _residual_then_rms(acc, residual, g):
    x = acc + residual
    return x, _rms(x, g)


def _loss_fn(x, target, g):
    def f(x, g):
        err = jnp.square(_rms(x, g) - target)
        return 0.5 * jnp.sum(jnp.mean(err, axis=-1, keepdims=True), axis=0, keepdims=True)

    loss, vjp = jax.vjp(f, x, g)
    dx, dg = vjp(jnp.ones((1, 1), F32))
    return dx, dg, jnp.broadcast_to(loss, (1, PACK_LANES))


def _adamw_fn(w, g, m, v):
    m = ADAM_B1 * m + (1.0 - ADAM_B1) * g
    v = ADAM_B2 * v + (1.0 - ADAM_B2) * jnp.square(g)
    m_hat = m / (1.0 - ADAM_B1 ** ADAM_STEP)
    v_hat = v / (1.0 - ADAM_B2 ** ADAM_STEP)
    delta = -ADAM_LR * (m_hat / (jnp.sqrt(v_hat) + ADAM_EPS) + ADAM_WD * w)
    return delta, m, v


def _group_pick(j):
    return [(j == g).astype(F32) for g in range(len(POOL_WINDOWS))]


SUBLANES = 8


def _causal_taps(x, taps):
    for r in range(min(SUBLANES, taps)):
        behind = x if r == 0 else pltpu.roll(x, r, 0)
        for q in range((taps - 1 - r) // SUBLANES + 1):
            start = HALO - SUBLANES * q
            yield SUBLANES * q + r, behind[start:start + SEQ_TILE, :]


def _rows_iota(n):
    return lax.broadcasted_iota(jnp.int32, (n, 1), 0)


def _seq_specs(n_time, causal):
    per = SEQ_TILE // HALO
    tile = pl.BlockSpec((SEQ_TILE, GROUP), lambda j, i: (i, j))
    if causal:
        halo = pl.BlockSpec((HALO, GROUP), lambda j, i: (jnp.maximum(i * per - 1, 0), j))
    else:
        halo = pl.BlockSpec((HALO, GROUP), lambda j, i: (jnp.minimum((i + 1) * per, n_time * per - 1), j))
    return tile, halo


def _seq_fwd(ga, proj, chx, conf_dw, sc_conv):
    s_len = ga.shape[0]
    n_time = s_len // SEQ_TILE
    ext = HALO + SEQ_TILE

    def body(ga_ref, ga_h, pin_ref, pin_h, chx_ref, chx_h, wc_ref, ws_ref, ca_ref, po_ref, cc_ref):
        j, i = pl.program_id(0), pl.program_id(1)
        keep = (i > 0).astype(F32)

        def extended(cur, halo):
            return jnp.concatenate([halo[...] * keep, cur[...]], axis=0)

        def conv(x, w_ref, taps):
            acc = None
            for d, rows in _causal_taps(x, taps):
                term = rows * w_ref[pl.ds(taps - 1 - d, 1), :]
                acc = term if acc is None else acc + term
            return acc

        ca_ref[...] = conv(extended(ga_ref, ga_h), wc_ref, CONF_KERNEL)
        cc_ref[...] = conv(extended(chx_ref, chx_h), ws_ref, SC_KERNEL)
        u = extended(pin_ref, pin_h)
        s1 = u + pltpu.roll(u, 1, 0)
        s2 = s1 + pltpu.roll(s1, 2, 0)
        s3 = s2 + pltpu.roll(s2, 4, 0)
        s4 = s3 + pltpu.roll(s3, 8, 0)
        pick = _group_pick(j)
        window_sum = s1 * pick[0] + s2 * pick[1] + s3 * pick[2] + s4 * pick[3]
        window = sum(float(w) * f for w, f in zip(POOL_WINDOWS, pick))
        pos = (_rows_iota(ext) + (i * SEQ_TILE - HALO + 1)).astype(F32)
        count = jnp.maximum(jnp.minimum(pos, window), 1.0)
        po_ref[...] = (window_sum / count - u)[HALO:, :]

    tile, halo = _seq_specs(n_time, causal=True)
    pin_tile = pl.BlockSpec((SEQ_TILE, GROUP), lambda j, i: (i, 2 * (MIX // GROUP) + j))
    pin_halo = pl.BlockSpec((HALO, GROUP), lambda j, i: (jnp.maximum(i * (SEQ_TILE // HALO) - 1, 0), 2 * (MIX // GROUP) + j))
    w_spec = lambda taps: pl.BlockSpec((taps, GROUP), lambda j, i: (0, j))
    return pl.pallas_call(
        body, name="seq_fwd", grid=(MIX // GROUP, n_time),
        in_specs=[tile, halo, pin_tile, pin_halo, tile, halo, w_spec(CONF_KERNEL), w_spec(SC_KERNEL)],
        out_specs=[tile] * 3, out_shape=[jax.ShapeDtypeStruct((s_len, MIX), F32)] * 3,
        compiler_params=_params("parallel", "arbitrary"),
    )(ga, ga, proj, proj, chx, chx, conf_dw, sc_conv)


def _seq_bwd(dca, dpooled, dcc, ga, chx, conf_dw, sc_conv):
    s_len = ga.shape[0]
    n_time = s_len // SEQ_TILE
    ext = HALO + SEQ_TILE

    def body(dca_ref, dca_h, dpo_ref, dpo_h, dcc_ref, dcc_h, ga_ref, ga_h, chx_ref, chx_h, wc_ref, ws_ref,
             dga_ref, dpin_ref, dchx_ref, dwc_ref, dws_ref, acc_c, acc_s):
        j, i = pl.program_id(0), pl.program_id(1)
        keep_prev = (i > 0).astype(F32)
        keep_next = (i < n_time - 1).astype(F32)

        def with_next(cur, halo):
            return jnp.concatenate([cur[...], halo[...] * keep_next], axis=0)

        def with_prev(cur, halo):
            return jnp.concatenate([halo[...] * keep_prev, cur[...]], axis=0)

        def conv_t(dy, w_ref, taps):
            acc = None
            for r in range(min(SUBLANES, taps)):
                ahead = dy if r == 0 else pltpu.roll(dy, ext - r, 0)
                for q in range((taps - 1 - r) // SUBLANES + 1):
                    term = ahead[SUBLANES * q:SUBLANES * q + SEQ_TILE, :] * w_ref[pl.ds(taps - 1 - (SUBLANES * q + r), 1), :]
                    acc = term if acc is None else acc + term
            return acc

        def tap_grads(dy_cur, x, acc_ref, dw_ref, taps):
            @pl.when(i == 0)
            def _():
                acc_ref[...] = jnp.zeros_like(acc_ref)

            for d, rows in _causal_taps(x, taps):
                prod = dy_cur * rows
                part = prod[:SUBLANES, :]
                for g in range(1, SEQ_TILE // SUBLANES):
                    part = part + prod[g * SUBLANES:(g + 1) * SUBLANES, :]
                acc_ref[pl.ds((taps - 1 - d) * SUBLANES, SUBLANES), :] += part

            @pl.when(i == n_time - 1)
            def _():
                for k in range(taps):
                    dw_ref[pl.ds(k, 1), :] = jnp.sum(acc_ref[pl.ds(k * SUBLANES, SUBLANES), :], axis=0, keepdims=True)

        dga_ref[...] = conv_t(with_next(dca_ref, dca_h), wc_ref, CONF_KERNEL)
        dchx_ref[...] = conv_t(with_next(dcc_ref, dcc_h), ws_ref, SC_KERNEL)
        tap_grads(dca_ref[...], with_prev(ga_ref, ga_h), acc_c, dwc_ref, CONF_KERNEL)
        tap_grads(dcc_ref[...], with_prev(chx_ref, chx_h), acc_s, dws_ref, SC_KERNEL)

        dpo = with_next(dpo_ref, dpo_h)
        pick = _group_pick(j)
        window = sum(float(w) * f for w, f in zip(POOL_WINDOWS, pick))
        pos = (_rows_iota(ext) + (i * SEQ_TILE + 1)).astype(F32)
        r = dpo / jnp.minimum(pos, window)
        q1 = r + pltpu.roll(r, ext - 1, 0)
        q2 = q1 + pltpu.roll(q1, ext - 2, 0)
        q3 = q2 + pltpu.roll(q2, ext - 4, 0)
        q4 = q3 + pltpu.roll(q3, ext - 8, 0)
        q = q1 * pick[0] + q2 * pick[1] + q3 * pick[2] + q4 * pick[3]
        dpin_ref[...] = (q - dpo)[:SEQ_TILE, :].astype(dpin_ref.dtype)

    tile, nxt = _seq_specs(n_time, causal=False)
    _, prv = _seq_specs(n_time, causal=True)
    w_spec = lambda taps: pl.BlockSpec((taps, GROUP), lambda j, i: (0, j))
    return pl.pallas_call(
        body, name="seq_bwd", grid=(MIX // GROUP, n_time),
        in_specs=[tile, nxt, tile, nxt, tile, nxt, tile, prv, tile, prv, w_spec(CONF_KERNEL), w_spec(SC_KERNEL)],
        out_specs=[tile, tile, tile, w_spec(CONF_KERNEL), w_spec(SC_KERNEL)],
        out_shape=[jax.ShapeDtypeStruct((s_len, MIX), F32), jax.ShapeDtypeStruct((s_len, MIX), BF16),
                   jax.ShapeDtypeStruct((s_len, MIX), F32), jax.ShapeDtypeStruct(conf_dw.shape, F32),
                   jax.ShapeDtypeStruct(sc_conv.shape, F32)],
        scratch_shapes=[pltpu.VMEM((CONF_KERNEL * SUBLANES, GROUP), F32), pltpu.VMEM((SC_KERNEL * SUBLANES, GROUP), F32)],
        compiler_params=_params("parallel", "arbitrary"),
    )(dca, dca, dpooled, dpooled, dcc, dcc, ga, ga, chx, chx, conf_dw, sc_conv)


MERGE_TILE = 512
MERGE_ROWS = 1024


def _merge_fwd(y, w_branch, gates):
    s_len, d_model = y.shape[1], w_branch.shape[2]
    t = min(MERGE_ROWS, s_len)
    per_branch = d_model // MERGE_TILE

    def body(y_ref, w_ref, gate_ref, out_ref, acc_ref):
        k = pl.program_id(2)

        @pl.when(k == 0)
        def _():
            acc_ref[...] = jnp.zeros_like(acc_ref)

        z = jnp.dot(y_ref[...], w_ref[...], preferred_element_type=F32)
        acc_ref[...] += gate_ref[...].astype(F32) * z

        @pl.when(k == 3)
        def _():
            out_ref[...] = acc_ref[...].astype(out_ref.dtype)

    return pl.pallas_call(
        body, name="merge_fwd", grid=(s_len // t, per_branch, 4),
        in_specs=[pl.BlockSpec((None, t, MIX), lambda i, j, k: (k, i, 0)),
                  pl.BlockSpec((None, MIX, MERGE_TILE), lambda i, j, k: (k, 0, j)),
                  pl.BlockSpec((t, MERGE_TILE), lambda i, j, k: (i, k * per_branch + j))],
        out_specs=pl.BlockSpec((t, MERGE_TILE), lambda i, j, k: (i, j)),
        out_shape=jax.ShapeDtypeStruct((s_len, d_model), BF16),
        scratch_shapes=[pltpu.VMEM((t, MERGE_TILE), F32)],
        compiler_params=_params("parallel", "parallel", "arbitrary"),
    )(y, w_branch, gates)


def _merge_bwd(y, w_branch, gates, dmerged):
    s_len, d_model = y.shape[1], w_branch.shape[2]
    t = min(MERGE_ROWS, s_len)
    per_branch = d_model // MERGE_TILE

    def body(y_ref, w_ref, gate_ref, dm_ref, dz_ref, dgate_ref):
        z = jnp.dot(y_ref[...], w_ref[...], preferred_element_type=F32)
        gate = gate_ref[...].astype(F32)
        dm = dm_ref[...]
        dz_ref[...] = (dm * gate).astype(dz_ref.dtype)
        dgate_ref[...] = (dm * z * gate * (1.0 - gate)).astype(dgate_ref.dtype)

    return pl.pallas_call(
        body, name="merge_bwd", grid=(s_len // t, per_branch, 4),
        in_specs=[pl.BlockSpec((None, t, MIX), lambda i, j, k: (k, i, 0)),
                  pl.BlockSpec((None, MIX, MERGE_TILE), lambda i, j, k: (k, 0, j)),
                  pl.BlockSpec((t, MERGE_TILE), lambda i, j, k: (i, k * per_branch + j)),
                  pl.BlockSpec((t, MERGE_TILE), lambda i, j, k: (i, j))],
        out_specs=[pl.BlockSpec((None, t, MERGE_TILE), lambda i, j, k: (k, i, j)),
                   pl.BlockSpec((t, MERGE_TILE), lambda i, j, k: (i, (4 + k) * per_branch + j))],
        out_shape=[jax.ShapeDtypeStruct((4, s_len, d_model), BF16), jax.ShapeDtypeStruct((s_len, 8 * d_model), BF16)],
        compiler_params=_params("parallel", "parallel", "arbitrary"),
    )(y, w_branch, gates, dmerged)


def _row(v):
    return v.reshape(1, -1)


def _branch_params(w):
    mats = [w["pool_w"][g] for g in range(4)] + [w["gmlp_ws"][g] for g in range(4)]
    mats += [w["gmlp_bs"][g].reshape(GROUP, 1) for g in range(4)]
    return [_row(w["conf_dw_b"]), _row(w["conf_ln_g"]), _row(w["conf_ln_b"]), _row(w["pool_scale"]),
            _row(w["gmlp_ln_g"]), _row(w["gmlp_ln_b"])] + mats


def _layer_fwd(x0, p_l, w, fetch):
    d_model = x0.shape[1]
    (h1,) = _rowwise("rms_mix", _rms, [(x0, d_model, 0)], [_row(w["norm_mix"])], [(d_model, BF16)])
    w.update(fetch(0, h1))
    (proj,) = _mm2("proj", h1, w["w_in"], b_by_chip=True, chips=(0, 2))
    w.update(fetch(1, proj))
    (gates,) = _mm2("proj_gates", h1, w["w_in"], b_by_chip=True, chips=(2, 2), out_dtypes=(BF16,),
                    epi=lambda acc: (jax.nn.sigmoid(acc),))
    ga, chx = _rowwise("glu", _glu_and_gate_fn, [(proj, MIX, 0), (proj, MIX, 1), (proj, MIX, 4), (proj, MIX, 5)], [],
                       [(MIX, F32), (MIX, F32)])
    ca, pooled, cc = _seq_fwd(ga, proj, chx, w["conf_dw"], w["sc_conv"])
    ys = _rowwise("branches", _branches_fn,
                  [(ca, MIX, 0), (pooled, MIX, 0), (cc, MIX, 0), (proj, MIX, 3), (proj, MIX, 6), (proj, MIX, 7)],
                  _branch_params(w), [(MIX, BF16)] * 4, tm=BRANCH_ROWS)
    y = jnp.stack(ys)
    merged = _merge_fwd(y, w["w_branch"], gates)
    x1, h2 = _mm2("out_proj", merged, w["w_out"], res=[(x0, 0)], rows=[_row(w["norm_mlp"])], out_dtypes=(F32, BF16),
                  epi=_residual_then_rms)
    (act,) = _mm2("mlp_up", h2, w["w_up"], b_by_chip=True, out_dtypes=(BF16,),
                  epi=lambda acc: (jnp.square(jnp.maximum(acc, 0.0)),))
    x2, h3 = _mm2("mlp_down", act, w["w_down"], res=[(x1, 0)], rows=[_row(w["norm_ple"])], out_dtypes=(F32, BF16),
                  epi=_residual_then_rms)
    (pg,) = _mm2("ple_gate", h3, w["w_ple_gate"])
    (x3,) = _mm2("ple", p_l, w["w_ple"], res=[(pg, 0), (x2, 0)],
                 epi=lambda acc, g, r: (acc * jax.nn.sigmoid(g) + r,))
    saved = dict(x0=x0, h1=h1, proj=proj, gates=gates, ga=ga, chx=chx, ca=ca, pooled=pooled, cc=cc, y=y, merged=merged, x1=x1,
                 h2=h2, act=act, x2=x2, h3=h3, pg=pg)
    return x3, saved


def _layer_bwd(dx3, p_l, w, s, send):
    d_model = dx3.shape[1]
    big, small = {}, {}

    def ple_epi(acc, dx, g):
        sig = jax.nn.sigmoid(g)
        return dx * sig, dx * acc * sig * (1.0 - sig)

    dpp, dpg = _mm2("ple_bwd", p_l, w["w_ple"], res=[(dx3, 0), (s["pg"], 0)], epi=ple_epi, out_dtypes=(BF16, BF16))
    (big["w_ple"],) = _mm2("dw_ple", p_l, dpp, ta=True, out_dtypes=(BF16,))
    (big["w_ple_gate"],) = _mm2("dw_ple_gate", s["h3"], dpg, ta=True, out_dtypes=(BF16,))
    dx2, dg = _mm2("dh_ple", dpg, w["w_ple_gate"], tb=True, res=[(s["x2"], 0), (dx3, 0)], rows=[_row(w["norm_ple"])],
                   epi=_rms_bwd_epilogue, n_row_outs=1)
    small["norm_ple"] = _sum_row_out(dg)

    (dup,) = _mm2("mlp_down_bwd", dx2, w["w_down"], tb=True, res=[(s["act"], 0)], out_dtypes=(BF16,),
                  epi=lambda acc, act: (acc * 2.0 * jnp.sqrt(act.astype(F32)),))
    (big["w_down"],) = _mm2("dw_down", s["act"], dx2, ta=True, out_dtypes=(BF16,))
    (big["w_up"],) = _mm2("dw_up", s["h2"], dup, ta=True, out_dtypes=(BF16,))
    sent = send(0, [big[n] for n in GRAD_GROUPS[0]])
    dx1, dg = _mm2("mlp_up_bwd", dup, w["w_up"], tb=True, b_by_chip=True, res=[(s["x1"], 0), (dx2, 0)],
                   rows=[_row(w["norm_mlp"]) + sent], epi=_rms_bwd_epilogue, n_row_outs=1)
    small["norm_mlp"] = _sum_row_out(dg)

    (dmerged,) = _mm2("out_proj_bwd", dx1, w["w_out"], tb=True)
    (big["w_out"],) = _mm2("dw_out", s["merged"], dx1, ta=True, out_dtypes=(BF16,))
    dz, dproj = _merge_bwd(s["y"], w["w_branch"], s["gates"], dmerged)
    (dy,) = _mm("branch_bwd", dz, w["w_branch"], tb=True)
    (big["w_branch"],) = _mm("dw_branch", s["y"], dz, ta=True, out_dtypes=(BF16,))

    proj = s["proj"]
    params = _branch_params(w)
    grads = _rowwise(
        "branches_bwd", _branches_bwd_fn,
        [(s["ca"], MIX, 0), (s["pooled"], MIX, 0), (s["cc"], MIX, 0), (proj, MIX, 3), (proj, MIX, 6), (proj, MIX, 7)]
        + [(dy, MIX, 0, g) for g in range(4)], params,
        [(MIX, F32), (MIX, F32), (MIX, F32), (MIX, BF16), (2 * MIX, BF16, dproj, 3)], [p.shape for p in params], tm=BRANCH_ROWS)
    dca, dpooled, dcc, dbg, dproj = grads[:5]
    pg_ = grads[5:]
    small["conf_dw_b"], small["conf_ln_g"], small["conf_ln_b"], small["pool_scale"] = pg_[0], pg_[1], pg_[2], pg_[3]
    small["gmlp_ln_g"], small["gmlp_ln_b"] = pg_[4], pg_[5]
    small["pool_w"] = jnp.stack(pg_[6:10])
    small["gmlp_ws"] = jnp.stack(pg_[10:14])
    small["gmlp_bs"] = jnp.stack([b.reshape(GROUP) for b in pg_[14:18]])

    dga, dpin, dchx, small["conf_dw"], small["sc_conv"] = _seq_bwd(dca, dpooled, dcc, s["ga"], s["chx"], w["conf_dw"], w["sc_conv"])
    (dproj,) = _rowwise("glu_bwd", _glu_bwd_fn,
                        [(proj, MIX, 0), (proj, MIX, 1), (proj, MIX, 4), (proj, MIX, 5), (dga, MIX, 0), (dchx, MIX, 0),
                         (dpin, MIX, 0), (dbg, MIX, 0)], [], [(6 * MIX, BF16, dproj, 0)])
    (big["w_in"],) = _mm2("dw_in", s["h1"], dproj, ta=True, out_dtypes=(BF16,))
    sent = send(1, [big[n] for n in GRAD_GROUPS[1]])
    dx0, dg = _mm2("proj_bwd", dproj, w["w_in"], tb=True, b_by_chip=True, res=[(s["x0"], 0), (dx1, 0)],
                   rows=[_row(w["norm_mix"]) + sent], epi=_rms_bwd_epilogue, n_row_outs=1)
    small["norm_mix"] = _sum_row_out(dg)
    return dx0, small


def _position():
    return lax.axis_index("x"), lax.axis_index("y"), lax.axis_index("c")


def _other_chips(x, y):
    return [(1 - x, y), (x, 1 - y), (1 - x, 1 - y)]


def _region(ref, cuts):
    return ref.at[tuple(pl.ds(*cuts[a]) if a in cuts else slice(None) for a in range(len(ref.shape)))]


def _allgather_small(name, block):
    m_per, n = block.shape

    def body(x_ref, out_ref, send_sems, recv_sems, local_sem):
        x, y, c = _position()
        me, sibling = (x, y, c), (x, y, 1 - c)
        chips = _other_chips(x, y)

        def rows(px, py, pc):
            return out_ref.at[pl.ds((4 * px + 2 * py + pc) * m_per, m_per), :]

        def copy(k, blk, to, src=None):
            return pltpu.make_async_remote_copy(
                src_ref=rows(*blk) if src is None else src, dst_ref=rows(*blk), send_sem=send_sems.at[k],
                recv_sem=recv_sems.at[k], device_id=to, device_id_type=pl.DeviceIdType.MESH)

        mine = pltpu.make_async_copy(x_ref, rows(*me), local_sem)
        mine.start()
        first = [copy(0, me, sibling, src=x_ref)]
        first += [copy(1 + j, me, (*chip, c), src=x_ref) for j, chip in enumerate(chips)]
        for cp in first:
            cp.start()
        passed = [copy(4 + j, (*chip, c), sibling) for j, chip in enumerate(chips)]
        for j, chip in enumerate(chips):
            copy(1 + j, (*chip, c), me).wait_recv()
            passed[j].start()
        copy(0, sibling, me).wait_recv()
        for j, chip in enumerate(chips):
            copy(4 + j, (*chip, 1 - c), me).wait_recv()
        for cp in first + passed:
            cp.wait_send()
        mine.wait()

    return pl.pallas_call(
        body, name=name, out_shape=jax.ShapeDtypeStruct((8 * m_per, n), block.dtype), in_specs=[ANY], out_specs=ANY,
        scratch_shapes=[pltpu.SemaphoreType.DMA((7,)), pltpu.SemaphoreType.DMA((7,)), pltpu.SemaphoreType.DMA],
    )(block)


def _cuts(arr_shape, shard_axis, half_axis, shard=None, half=None):
    cuts = {}
    if shard is not None:
        n = arr_shape[shard_axis] // 4
        cuts[shard_axis] = (shard * n, n)
    if half is not None:
        n = arr_shape[half_axis] // 2
        cuts[half_axis] = (half * n, n)
    return cuts


def _full_shape(shard_shape, shard_axis):
    return tuple(4 * n if a == shard_axis else n for a, n in enumerate(shard_shape))


HBM = pl.BlockSpec(memory_space=pltpu.HBM)
SEM = pl.BlockSpec(memory_space=pltpu.SEMAPHORE)
DATAFLOW_EFFECT = pltpu.SideEffectType.DATAFLOW_SIDE_EFFECTING
TOKEN_SHAPE = (8, 128)


def _remote(src, dst, send_sems, recv_sems, k, peer):
    return pltpu.make_async_remote_copy(src_ref=src, dst_ref=dst, send_sem=send_sems.at[k], recv_sem=recv_sems.at[k],
                                        device_id=peer, device_id_type=pl.DeviceIdType.MESH)


def _split_start(name, sources, landings, plan):
    n_s, n_l = len(sources), len(landings)
    n_copies = len(plan([None] * n_s, [None] * n_l, None))

    def body(*refs):
        src, land = refs[:n_s], refs[n_s:n_s + n_l]
        send_sems, recv_sems, token = refs[n_s + n_l], refs[n_s + n_l + 1], refs[-1]
        for k, (s, d, peer) in enumerate(plan(src, land, _position())):
            _remote(s, d, send_sems, recv_sems, k, peer).start()
        token[...] = jnp.zeros_like(token)

    arrays = [pltpu.with_memory_space_constraint(a, pltpu.HBM) for a in (*sources, *landings)]
    outs = pl.pallas_call(
        body, name=name,
        out_shape=(pltpu.SemaphoreType.DMA((n_copies,)), pltpu.SemaphoreType.DMA((n_copies,)),
                   *[pltpu.HBM(a.shape, a.dtype) for a in arrays], jax.ShapeDtypeStruct(TOKEN_SHAPE, F32)),
        in_specs=[HBM] * (n_s + n_l),
        out_specs=(SEM, SEM, *[HBM] * (n_s + n_l), pl.BlockSpec(memory_space=pltpu.VMEM)),
        input_output_aliases={i: 2 + i for i in range(n_s + n_l)},
        compiler_params=pltpu.CompilerParams(has_side_effects=DATAFLOW_EFFECT),
    )(*arrays)
    return outs[0], outs[1], list(outs[2:2 + n_s]), list(outs[2 + n_s:2 + n_s + n_l]), outs[-1]


def _split_wait(name, started, after, plan):
    send_sems, recv_sems, sources, landings, _ = started
    n_s, n_l = len(sources), len(landings)

    def body(*refs):
        src, land = refs[:n_s], refs[n_s:n_s + n_l]
        send_sems_ref, recv_sems_ref = refs[n_s + n_l], refs[n_s + n_l + 1]
        for k, (s, d, peer) in enumerate(plan(src, land, _position())):
            cp = _remote(s, d, send_sems_ref, recv_sems_ref, k, peer)
            cp.wait_send()
            cp.wait_recv()

    outs = pl.pallas_call(
        body, name=name, out_shape=tuple(pltpu.HBM(a.shape, a.dtype) for a in (*sources, *landings)),
        in_specs=[*[HBM] * (n_s + n_l), SEM, SEM, ANY], out_specs=tuple([HBM] * (n_s + n_l)),
        input_output_aliases={i: i for i in range(n_s + n_l)},
        compiler_params=pltpu.CompilerParams(has_side_effects=DATAFLOW_EFFECT),
    )(*sources, *landings, send_sems, recv_sems, after)
    return list(outs[:n_s]), list(outs[n_s:])


def _exchange(name, inputs, out_shapes, plan, in_place=False):
    n_in = len(inputs)
    n_out = n_in if in_place else len(out_shapes)
    n_copies = len(plan([None] * n_in, [None] * n_out, None))

    def body(*refs):
        ins, outs = refs[:n_in], refs[n_in:n_in + n_out]
        send_sems, recv_sems = refs[n_in + n_out:]
        copies = [_remote(s, d, send_sems, recv_sems, k, peer)
                  for k, (s, d, peer) in enumerate(plan(ins, ins if in_place else outs, _position()))]
        for cp in copies:
            cp.start()
        for cp in copies:
            cp.wait()

    if in_place:
        out_shapes = [jax.ShapeDtypeStruct(a.shape, a.dtype) for a in inputs]
    return pl.pallas_call(
        body, name=name, in_specs=[ANY] * n_in, out_specs=[ANY] * n_out, out_shape=out_shapes,
        input_output_aliases={i: i for i in range(n_in)} if in_place else {},
        scratch_shapes=[pltpu.SemaphoreType.DMA((n_copies,)), pltpu.SemaphoreType.DMA((n_copies,))],
    )(*inputs)


def _half_rows(ref, half):
    n = ref.shape[0] // 2
    return ref.at[pl.ds(half * n, n)]


def _gather_plan(sources, landings, pos):
    if pos is None:
        return [None] * (3 * len(sources))
    x, y, c = pos
    copies = []
    for src, land in zip(sources, landings):
        for qx, qy in _other_chips(x, y):
            copies.append((_half_rows(src, c), _half_rows(land.at[2 * x + y], c), (qx, qy, c)))
    return copies


def _sibling_plan(landings, _, pos):
    if pos is None:
        return [None] * (3 * len(landings))
    x, y, c = pos
    copies = []
    for land in landings:
        for qx, qy in _other_chips(x, y):
            region = _half_rows(land.at[2 * qx + qy], c)
            copies.append((region, region, (x, y, 1 - c)))
    return copies


def _weights_start(tag, shards):
    landings = [lax.empty((4, *s.shape), BF16) for s in shards]
    return _split_start(f"weights_start_{tag}", shards, landings, _gather_plan)


def _weights_finish(tag, names, started, after):
    shards, landings = _split_wait(f"weights_wait_{tag}", started, after, _gather_plan)
    landings = _exchange("weights_to_sibling", landings, None, _sibling_plan, in_place=True)
    my_shard = 2 * lax.axis_index("x") + lax.axis_index("y")
    full = {}
    for name, shard, land in zip(names, shards, landings):
        sa = BIG_AXES[name][0]
        land = lax.dynamic_update_index_in_dim(land, shard, my_shard, axis=0)
        if name in BY_CHIP:
            full[name] = land
            continue
        axis = sa - 1
        moved = jnp.moveaxis(land, 0, axis)
        full[name] = moved.reshape(*moved.shape[:axis], 4 * moved.shape[axis + 1], *moved.shape[axis + 2:])
    return full


def _half_shape(shape, half_axis):
    return tuple(n // 2 if a == half_axis else n for a, n in enumerate(shape))


def _shard_shape(shape, shard_axis):
    return tuple(n // 4 if a == shard_axis else n for a, n in enumerate(shape))


def _flat2d(a):
    return a.reshape(-1, a.shape[-1])


ELEMENTWISE_TILE_ELEMENTS = 1 << 18


def _row_tile(rows, cols):
    tile = 1
    while rows % (2 * tile) == 0 and 2 * tile * cols <= ELEMENTWISE_TILE_ELEMENTS:
        tile *= 2
    assert tile >= 16 or tile == rows, (rows, cols)
    return tile


def _window_of(arr, axis, parts, index):
    flat = _flat2d(arr)
    rows, cols = flat.shape
    if axis == arr.ndim - 1:
        return flat, (rows, cols // parts), (0, index)
    assert axis == 0, axis
    return flat, (rows // parts, cols), (index, 0)


def _window_sum(name, source, window, index, others, out_dtype):
    w_rows, w_cols = window
    tm = _row_tile(w_rows, w_cols)
    blocks = w_rows // tm
    where = jnp.stack([jnp.asarray(i, jnp.int32) for i in index])

    def body(where_ref, src_ref, *refs):
        acc = src_ref[...].astype(F32)
        for r in refs[:-1]:
            acc = acc + r[...].astype(F32)
        refs[-1][...] = acc.astype(refs[-1].dtype)

    same = pl.BlockSpec((tm, w_cols), lambda i, where_ref: (i, 0))
    grid_spec = pltpu.PrefetchScalarGridSpec(
        num_scalar_prefetch=1, grid=(blocks,),
        in_specs=[pl.BlockSpec((tm, w_cols), lambda i, where_ref: (where_ref[0] * blocks + i, where_ref[1])), *[same] * len(others)],
        out_specs=same)
    return pl.pallas_call(body, name=name, grid_spec=grid_spec, out_shape=jax.ShapeDtypeStruct(window, out_dtype),
                          compiler_params=_params("arbitrary"))(where, source, *others)


def _sum_arrays(name, arrays, out_dtype):
    shape = arrays[0].shape
    flat = [_flat2d(a) for a in arrays]
    rows, cols = flat[0].shape
    tm = _row_tile(rows, cols)

    def fn(*tiles):
        acc = tiles[0].astype(F32)
        for t in tiles[1:]:
            acc = acc + t.astype(F32)
        return acc

    (out,) = _rowwise(name, fn, [(a, cols, 0) for a in flat], [], [(cols, out_dtype)], tm=tm)
    return out.reshape(shape)


def _grads_start(tag, names, grads):
    c = lax.axis_index("c")
    shapes = [g.shape for g in grads]
    axes = [(BIG_AXES[n][0] - 1, BIG_AXES[n][1] - 1) for n in names]

    def to_sibling(ins, outs, pos):
        if pos is None:
            return [None] * len(names)
        px, py, pc = pos
        return [(_region(ins[wi], _cuts(shapes[wi], sa, ha, half=1 - pc)), outs[wi], (px, py, 1 - pc))
                for wi, (sa, ha) in enumerate(axes)]

    landed = _exchange("grads_to_sibling", grads, [jax.ShapeDtypeStruct(_half_shape(s, ha), BF16) for s, (_, ha) in zip(shapes, axes)],
                       to_sibling)
    chip_sums = []
    for wi, name in enumerate(names):
        source, window, index = _window_of(grads[wi], axes[wi][1], 2, c)
        chip_sum = _window_sum("chip_sum_" + name, source, window, index, [_flat2d(landed[wi])], BF16)
        chip_sums.append(chip_sum.reshape(landed[wi].shape))

    def to_chips(sources, landings, pos):
        if pos is None:
            return [None] * (3 * len(names))
        px, py, pc = pos
        copies = []
        for wi, (sa, ha) in enumerate(axes):
            for j, (qx, qy) in enumerate(_other_chips(px, py)):
                piece = _region(sources[wi], _cuts(shapes[wi], sa, ha, shard=2 * qx + qy))
                copies.append((piece, landings[wi].at[j], (qx, qy, pc)))
        return copies

    landings = [lax.empty((3, *_shard_shape(t.shape, sa)), BF16) for t, (sa, _) in zip(chip_sums, axes)]
    return _split_start(f"grads_start_{tag}", chip_sums, landings, to_chips), to_chips


def _grads_finish(tag, names, started, after):
    chip_sums, pieces = _split_wait(f"grads_wait_{tag}", started[0], after, started[1])
    my_shard = 2 * lax.axis_index("x") + lax.axis_index("y")
    reduced = {}
    for name, chip_sum, piece in zip(names, chip_sums, pieces):
        source, window, index = _window_of(chip_sum, BIG_AXES[name][0] - 1, 4, my_shard)
        total = _window_sum("shard_sum_" + name, source, window, index, [_flat2d(piece[j]) for j in range(3)], F32)
        reduced[name] = total.reshape(piece.shape[1:])
    return reduced


def _join_halves(halves, names):
    c = lax.axis_index("c")
    mine = [halves[l][name] for l in range(len(halves)) for name in names]

    def halves_to_sibling(ins, outs, pos):
        if pos is None:
            return [None] * len(mine)
        px, py, pc = pos
        return [(ins[k], outs[k], (px, py, 1 - pc)) for k in range(len(mine))]

    theirs = _exchange("grad_halves_to_sibling", mine, [jax.ShapeDtypeStruct(r.shape, F32) for r in mine], halves_to_sibling)
    out = []
    for wi, name in enumerate(names):
        ha = BIG_AXES[name][1]
        layers = []
        for l in range(len(halves)):
            a, b = mine[l * len(names) + wi], theirs[l * len(names) + wi]
            layers.append(jnp.concatenate([jnp.where(c == 0, a, b), jnp.where(c == 0, b, a)], axis=ha - 1))
        out.append(jnp.stack(layers))
    return out


def _pack(arrays):
    flat = jnp.concatenate([a.reshape(-1).astype(F32) for a in arrays])
    rows = -(-flat.shape[0] // PACK_LANES)
    rows = -(-rows // PACK_ROW_MULTIPLE) * PACK_ROW_MULTIPLE
    return jnp.pad(flat, (0, rows * PACK_LANES - flat.shape[0])).reshape(rows, PACK_LANES)


def _unpack(buf, shapes):
    flat, out, off = buf.reshape(-1), [], 0
    for s in shapes:
        n = 1
        for d in s:
            n *= d
        out.append(flat[off:off + n].reshape(s))
        off += n
    return out


def _sum8(name, gathered, rows):
    def body(in_ref, out_ref):
        acc = in_ref[0].astype(F32)
        for d in range(1, 8):
            acc = acc + in_ref[d].astype(F32)
        out_ref[...] = acc

    return pl.pallas_call(
        body, name=name, grid=(rows // PACK_ROW_MULTIPLE,),
        in_specs=[pl.BlockSpec((8, PACK_ROW_MULTIPLE, PACK_LANES), lambda i: (0, i, 0))],
        out_specs=pl.BlockSpec((PACK_ROW_MULTIPLE, PACK_LANES), lambda i: (i, 0)),
        out_shape=jax.ShapeDtypeStruct((rows, PACK_LANES), F32), compiler_params=_params("arbitrary"),
    )(gathered.reshape(8, rows, PACK_LANES))


def _adamw(name, w, g, m, v):
    shape = w.shape
    flat = [_flat2d(a) for a in (w, g, m, v)]
    rows, cols = flat[0].shape
    tm = _row_tile(rows, cols)
    outs = _rowwise(name, _adamw_fn, [(a, cols, 0) for a in flat], [], [(cols, F32)] * 3, tm=tm)
    return [o.reshape(shape) for o in outs]


WEIGHT_ORDER = ("norm_mix", "w_in", "conf_dw", "conf_dw_b", "conf_ln_g", "conf_ln_b", "pool_w", "pool_scale", "sc_conv",
                "gmlp_ln_g", "gmlp_ln_b", "gmlp_ws", "gmlp_bs", "w_branch", "w_out", "norm_mlp", "w_up", "w_down",
                "norm_ple", "w_ple", "w_ple_gate", "norm_final")


def _local_step(x, p, loss_target, weights, layer_big, full_small_sharded, on_big_grads):
    n_layers = p.shape[0]
    d_model = x.shape[1]
    layer_w, saved = [], []
    h = x
    for l in range(n_layers):
        w = {n: weights[n][l] for n in SMALL_REP if n != "norm_final"}
        w.update({n: full_small_sharded[n][l] for n in SMALL_SHARDED})
        layer_w.append(w)
        h, s = _layer_fwd(h, p[l], w, layer_big(l))
        saved.append(s)
    dx, d_final, loss = _rowwise("loss", _loss_fn, [(h, d_model, 0), (loss_target, d_model, 0)], [_row(weights["norm_final"])],
                                 [(d_model, F32)], [(1, d_model), (1, PACK_LANES)])
    small_grads = {n: [None] * n_layers for n in SMALL_REP + SMALL_SHARDED if n != "norm_final"}
    for l in reversed(range(n_layers)):
        dx, small = _layer_bwd(dx, p[l], layer_w[l], saved[l], on_big_grads(l))
        for n in small:
            small_grads[n][l] = small[n].reshape(weights[n].shape[1:]) if n in SMALL_REP else small[n]
    small_grads = {n: jnp.stack(v) for n, v in small_grads.items()}
    small_grads["norm_final"] = d_final.reshape(-1)
    return loss, dx, small_grads


def kernel(x, p, norm_mix, w_in, conf_dw, conf_dw_b, conf_ln_g, conf_ln_b, pool_w, pool_scale, sc_conv, gmlp_ln_g, gmlp_ln_b, gmlp_ws, gmlp_bs, w_branch, w_out, norm_mlp, w_up, w_down, norm_ple, w_ple, w_ple_gate, norm_final, loss_target, m_norm_mix, m_w_in, m_conf_dw, m_conf_dw_b, m_conf_ln_g, m_conf_ln_b, m_pool_w, m_pool_scale, m_sc_conv, m_gmlp_ln_g, m_gmlp_ln_b, m_gmlp_ws, m_gmlp_bs, m_w_branch, m_w_out, m_norm_mlp, m_w_up, m_w_down, m_norm_ple, m_w_ple, m_w_ple_gate, m_norm_final, v_norm_mix, v_w_in, v_conf_dw, v_conf_dw_b, v_conf_ln_g, v_conf_ln_b, v_pool_w, v_pool_scale, v_sc_conv, v_gmlp_ln_g, v_gmlp_ln_b, v_gmlp_ws, v_gmlp_bs, v_w_branch, v_w_out, v_norm_mlp, v_w_up, v_w_down, v_norm_ple, v_w_ple, v_w_ple_gate, v_norm_final):
    given = dict(locals())
    weights = {n: given[n] for n in WEIGHT_ORDER}
    mom_m = {n: given["m_" + n] for n in WEIGHT_ORDER}
    mom_v = {n: given["v_" + n] for n in WEIGHT_ORDER}
    my_shard = 2 * lax.axis_index("x") + lax.axis_index("y")

    n_layers = p.shape[0]
    shard_shapes = [weights[n].shape for n in SMALL_SHARDED]
    gathered = _allgather_small("allgather_small_weights", _pack([weights[n] for n in SMALL_SHARDED]))
    after_small = gathered[0, 0] * 0.0
    after_small = jnp.where(after_small == 0.0, after_small, 0.0)

    weight_copies = [[None] * len(WEIGHT_GROUPS) for _ in range(n_layers)]
    started = after_small
    for l in range(n_layers):
        for g, names in enumerate(WEIGHT_GROUPS):
            shards = [(weights[n][l] + started if i == 0 else weights[n][l]).astype(BF16) for i, n in enumerate(names)]
            weight_copies[l][g] = _weights_start(f"{l}_{g}", shards)
            started = weight_copies[l][g][4][0, 0]
    per_dev = gathered.reshape(8, -1, PACK_LANES)
    full_small_sharded = {}
    chip_parts = [_unpack(per_dev[2 * s], shard_shapes) for s in range(4)]
    for i, n in enumerate(SMALL_SHARDED):
        full_small_sharded[n] = jnp.concatenate([chip_parts[s][i] for s in range(4)], axis=-1)

    grad_copies = [[None] * len(GRAD_GROUPS) for _ in range(n_layers)]

    def layer_big(l):
        return lambda g, after: _weights_finish(f"{l}_{g}", WEIGHT_GROUPS[g], weight_copies[l][g], after)

    def on_big_grads(l):
        def send(g, group_grads):
            grad_copies[l][g] = _grads_start(f"{l}_{g}", GRAD_GROUPS[g], group_grads)
            return grad_copies[l][g][0][4][0, 0]
        return send

    loss_row, dx, small_grads = _local_step(x[0] + started, p[:, 0], loss_target[0], weights, layer_big, full_small_sharded,
                                            on_big_grads)
    loss = lax.psum(loss_row[0, 0], MESH_AXES)

    grads, delta, new_m, new_v = {}, {}, {}, {}
    halves = [{} for _ in range(n_layers)]
    last_sent = (0, len(GRAD_GROUPS) - 1)
    for l in reversed(range(n_layers)):
        for g, names in enumerate(GRAD_GROUPS):
            if (l, g) != last_sent:
                halves[l].update(_grads_finish(f"{l}_{g}", names, grad_copies[l][g], dx))
    behind = dx
    for g, names in enumerate(GRAD_GROUPS):
        if g == last_sent[1]:
            halves[0].update(_grads_finish(f"0_{g}", names, grad_copies[0][g], behind))
        for n, joined in zip(names, _join_halves(halves, names)):
            grads[n] = joined
            delta[n], new_m[n], new_v[n] = _adamw("adamw_" + n, weights[n], joined, mom_m[n], mom_v[n])
            behind = delta[n]

    small_names = SMALL_REP + SMALL_SHARDED
    after_big = behind[(0,) * behind.ndim] * 0.0
    after_big = jnp.where(after_big == 0.0, after_big, 0.0)
    packed = _pack([small_grads[n] for n in small_names]) + after_big
    rows = packed.shape[0]
    summed = _sum8("sum_small_grads", _allgather_small("allgather_small_grads", packed.astype(BF16)), rows)
    small_full_shapes = [weights[n].shape for n in SMALL_REP] + [full_small_sharded[n].shape for n in SMALL_SHARDED]
    for n, g in zip(small_names, _unpack(summed, small_full_shapes)):
        if n in SMALL_SHARDED:
            width = weights[n].shape[-1]
            g = lax.dynamic_slice_in_dim(g, my_shard * width, width, axis=g.ndim - 1)
        grads[n] = g

    for group, names in (("rep", SMALL_REP), ("sharded", SMALL_SHARDED)):
        outs = _adamw("adamw_small_" + group, *[_pack([src[n] for n in names]) for src in (weights, grads, mom_m, mom_v)])
        shapes = [weights[n].shape for n in names]
        for dst, buf in zip((delta, new_m, new_v), outs):
            for n, a in zip(names, _unpack(buf, shapes)):
                dst[n] = a

    return (loss, dx[None], *[grads[n] for n in WEIGHT_ORDER], *[delta[n] for n in WEIGHT_ORDER],
            *[new_m[n] for n in WEIGHT_ORDER], *[new_v[n] for n in WEIGHT_ORDER])
```

```python
import jax
import jax.numpy as jnp
from jax import lax
from jax.experimental import pallas as pl
from jax.experimental.pallas import tpu as pltpu

F32, BF16 = jnp.float32, jnp.bfloat16
EPS = 1e-6
ADAM_LR, ADAM_B1, ADAM_B2, ADAM_EPS, ADAM_WD, ADAM_STEP = 0.001, 0.9, 0.999, 1e-08, 0.01, 10
CONF_KERNEL, SC_KERNEL = 31, 3
POOL_WINDOWS = (2, 4, 8, 16)
MIX = 512
GROUP = 128
HALO = 32
SEQ_TILE = 256
BRANCH_ROWS = 512
VMEM_LIMIT_BYTES = 48 * 1024 * 1024
MESH_AXES = ("x", "y", "c")
ANY = pl.BlockSpec(memory_space=pl.ANY)

BIG = (("w_in", 2, 1), ("w_branch", 3, 1), ("w_out", 1, 2), ("w_up", 2, 1), ("w_down", 1, 2),
       ("w_ple", 2, 1), ("w_ple_gate", 1, 2))
SMALL_REP = ("norm_mix", "conf_dw_b", "conf_ln_g", "conf_ln_b", "pool_w", "pool_scale", "gmlp_ln_g",
             "gmlp_ln_b", "gmlp_ws", "gmlp_bs", "norm_mlp", "norm_ple", "norm_final")
SMALL_SHARDED = ("conf_dw", "sc_conv")
BY_CHIP = ("w_in", "w_up")
BIG_AXES = {name: (shard_axis, half_axis) for name, shard_axis, half_axis in BIG}
WEIGHT_GROUPS = (("w_in",), ("w_branch", "w_out", "w_up", "w_down", "w_ple_gate", "w_ple"))
GRAD_GROUPS = (("w_ple", "w_ple_gate", "w_down", "w_up"), ("w_out", "w_branch", "w_in"))
PACK_LANES = 128
PACK_ROW_MULTIPLE = 256


def _params(*semantics):
    return pltpu.CompilerParams(dimension_semantics=semantics, vmem_limit_bytes=VMEM_LIMIT_BYTES)


def _accumulate(ref, value, first):
    @pl.when(first)
    def _():
        ref[...] = value.astype(ref.dtype)

    @pl.when(jnp.logical_not(first))
    def _():
        ref[...] += value.astype(ref.dtype)


def _rowwise(name, fn, rows, params, outs, accs=(), tm=256):
    n_rows = rows[0][0].shape[-2]
    tm = min(tm, n_rows)
    n_in, n_out = len(rows) + len(params), len(outs)
    targets = [(k, o[2]) for k, o in enumerate(outs) if len(o) == 4]

    def body(*refs):
        res = fn(*[r[...] for r in refs[:n_in]])
        res = res if isinstance(res, (tuple, list)) else (res,)
        out_refs = refs[n_in + len(targets):]
        for r, v in zip(out_refs[:n_out], res[:n_out]):
            r[...] = v.astype(r.dtype)
        first = pl.program_id(0) == 0
        for r, v in zip(out_refs[n_out:], res[n_out:]):
            _accumulate(r, v, first)

    in_specs = []
    for row in rows:
        arr, w, cb = row[:3]
        if len(row) == 4:
            in_specs.append(pl.BlockSpec((None, tm, w), lambda i, cb=cb, g=row[3]: (g, i, cb)))
        else:
            in_specs.append(pl.BlockSpec((tm, w), lambda i, cb=cb: (i, cb)))
    for p in params:
        in_specs.append(pl.BlockSpec(p.shape, lambda i, nd=p.ndim: (0,) * nd))
    in_specs += [ANY] * len(targets)
    out_specs = [pl.BlockSpec((tm, o[0]), lambda i, cb=(o[3] if len(o) == 4 else 0): (i, cb)) for o in outs]
    out_specs += [pl.BlockSpec(s, lambda i, nd=len(s): (0,) * nd) for s in accs]
    out_shape = [jax.ShapeDtypeStruct(o[2].shape if len(o) == 4 else (n_rows, o[0]), o[1]) for o in outs]
    out_shape += [jax.ShapeDtypeStruct(s, F32) for s in accs]
    return pl.pallas_call(
        body, name=name, grid=(n_rows // tm,), in_specs=in_specs, out_specs=out_specs, out_shape=out_shape,
        input_output_aliases={n_in + t: k for t, (k, _) in enumerate(targets)},
        compiler_params=_params("arbitrary"),
    )(*[r[0] for r in rows], *params, *[t for _, t in targets])


ROW_OUT_SUBLANES = 8


def _mm(name, a, b, *, ta=False, tb=False, a_cols=0, b_by_chip=False, chips=(0, 4), res=(), rows=(), epi=None, out_dtypes=(F32,), n_row_outs=0,
        tm=1024, tn=1024, tk=1024):
    if a_cols:
        assert ta
        g_, m_, k_ = a.shape[1] // a_cols, a_cols, a.shape[0]
    else:
        g_, m_, k_ = (a.shape[0], a.shape[2], a.shape[1]) if ta else a.shape
    if b_by_chip:
        n_ = b.shape[2] if tb else chips[1] * b.shape[3]
    else:
        n_ = b.shape[1] if tb else b.shape[2]
    tm, tn, tk = min(tm, m_), min(tn, n_), min(tk, k_)
    if b_by_chip:
        tn, tk = (tn, min(tk, b.shape[3])) if tb else (min(tn, b.shape[3]), tk)
        per_chip = b.shape[3] // (tk if tb else tn)
    nk, n_res, n_rows, n_out = k_ // tk, len(res), len(rows), len(out_dtypes)
    n_in = 2 + n_res + n_rows
    dims = (((0 if ta else 1,), (1 if tb else 0,)), ((), ()))
    if epi is None:
        epi = lambda acc: (acc,)

    def body(*refs):
        a_ref, b_ref = refs[:2]
        res_refs, out_refs, acc_ref = refs[2:n_in], refs[n_in:n_in + n_out], refs[-1]
        row_out_refs = refs[n_in + n_out:-1]
        k = pl.program_id(3)

        @pl.when(k == 0)
        def _():
            acc_ref[...] = jnp.zeros_like(acc_ref)

        acc_ref[...] += lax.dot_general(a_ref[...].astype(BF16), b_ref[...].astype(BF16), dims,
                                        preferred_element_type=F32)

        @pl.when(k == nk - 1)
        def _():
            vals = epi(acc_ref[...], *[r[...] for r in res_refs])
            for o, v in zip(out_refs, vals):
                o[...] = v.astype(o.dtype)
            for o, v in zip(row_out_refs, vals[n_out:]):
                o[...] = jnp.broadcast_to(v, o.shape)

    if a_cols:
        a_spec = pl.BlockSpec((tk, tm), lambda g, i, j, k: (k, g * (m_ // tm) + i))
    elif ta:
        a_spec = pl.BlockSpec((None, tk, tm), lambda g, i, j, k: (g, k, i))
    else:
        a_spec = pl.BlockSpec((None, tm, tk), lambda g, i, j, k: (g, i, k))
    if b_by_chip and tb:
        b_spec = pl.BlockSpec((None, None, tn, tk), lambda g, i, j, k: (g, k // per_chip, j, k % per_chip))
    elif b_by_chip:
        b_spec = pl.BlockSpec((None, None, tk, tn), lambda g, i, j, k: (g, chips[0] + j // per_chip, k, j % per_chip))
    elif tb:
        b_spec = pl.BlockSpec((None, tn, tk), lambda g, i, j, k: (g, j, k))
    else:
        b_spec = pl.BlockSpec((None, tk, tn), lambda g, i, j, k: (g, k, j))
    res_specs = [pl.BlockSpec((None, tm, tn), lambda g, i, j, k, off=off: (g, i, j + off)) for _, off in res]
    row_specs = [pl.BlockSpec((None, 1, tn), lambda g, i, j, k: (g, 0, j))] * n_rows
    out_spec = pl.BlockSpec((None, tm, tn), lambda g, i, j, k: (g, i, j))
    row_out_spec = pl.BlockSpec((None, ROW_OUT_SUBLANES, tn), lambda g, i, j, k: (g, i, j))
    outs = pl.pallas_call(
        body, name=name, grid=(g_, m_ // tm, n_ // tn, nk),
        in_specs=[a_spec, b_spec, *res_specs, *row_specs], out_specs=[out_spec] * n_out + [row_out_spec] * n_row_outs,
        out_shape=[jax.ShapeDtypeStruct((g_, m_, n_), d) for d in out_dtypes]
        + [jax.ShapeDtypeStruct((g_, ROW_OUT_SUBLANES * (m_ // tm), n_), F32)] * n_row_outs,
        scratch_shapes=[pltpu.VMEM((tm, tn), F32)],
        compiler_params=_params("parallel", "parallel", "parallel", "arbitrary"),
    )(a, b, *[r for r, _ in res], *rows)
    return outs


def _mm2(name, a, b, *, res=(), rows=(), **kw):
    outs = _mm(name, a[None], b[None], res=[(r[None], off) for r, off in res], rows=[r[None] for r in rows], **kw)
    return [o[0] for o in outs]


def _sum_row_out(row_out):
    return jnp.sum(row_out[::ROW_OUT_SUBLANES], axis=0, keepdims=True)


@jax.custom_vjp
def _bdot(a, b):
    return jnp.dot(a.astype(BF16), b.astype(BF16), preferred_element_type=F32)


def _bdot_fwd(a, b):
    return _bdot(a, b), (a, b)


def _bdot_bwd(saved, g):
    a, b = saved
    gb = g.astype(BF16)
    da = lax.dot_general(gb, b.astype(BF16), (((1,), (1,)), ((), ())), preferred_element_type=F32)
    db = lax.dot_general(a.astype(BF16), gb, (((0,), (0,)), ((), ())), preferred_element_type=F32)
    return da.astype(a.dtype), db.astype(b.dtype)


_bdot.defvjp(_bdot_fwd, _bdot_bwd)


def _rms(x, g):
    return x * lax.rsqrt(jnp.mean(x * x, axis=-1, keepdims=True) + EPS) * g


def _layer_norm(x, g, b):
    mu = jnp.mean(x, axis=-1, keepdims=True)
    var = jnp.mean(jnp.square(x - mu), axis=-1, keepdims=True)
    return (x - mu) * lax.rsqrt(var + EPS) * g + b


def _glu_and_gate_fn(a, a_gate, cg, hx):
    return a * jax.nn.sigmoid(a_gate), cg * hx


def _branches_fn(ca, pooled, cc, bg, u, v, conf_b, conf_g, conf_beta, pool_scale, gm_g, gm_b, *mats):
    pool_w, ws, bs = mats[0:4], mats[4:8], mats[8:12]
    ya = jax.nn.silu(_layer_norm(ca + conf_b, conf_g, conf_beta))
    yb = jnp.concatenate([_bdot(pooled[:, g * GROUP:(g + 1) * GROUP], pool_w[g]) for g in range(4)], axis=1) * pool_scale
    yc = bg * cc
    vln = _layer_norm(v, gm_g, gm_b)
    causal = lax.broadcasted_iota(jnp.int32, (GROUP, GROUP), 0) >= lax.broadcasted_iota(jnp.int32, (GROUP, GROUP), 1)
    chunks = v.shape[0] // GROUP

    def spatial(g):
        vg = vln[:, g * GROUP:(g + 1) * GROUP]
        wide = jnp.concatenate([vg[c * GROUP:(c + 1) * GROUP, :] for c in range(chunks)], axis=1)
        out = _bdot(jnp.where(causal, ws[g], 0.0), wide) + bs[g]
        return jnp.concatenate([out[:, c * GROUP:(c + 1) * GROUP] for c in range(chunks)], axis=0)

    sg = jnp.concatenate([spatial(g) for g in range(4)], axis=1)
    yd = u * sg
    return ya, yb, yc, yd


def _branches_joined_fn(*args):
    return jnp.concatenate(_branches_fn(*args), axis=1)


def _branches_bwd_fn(*args):
    ins, dys = args[:6] + args[10:], args[6:10]
    _, vjp = jax.vjp(_branches_fn, *ins)
    grads = vjp(tuple(dys))
    duv = jnp.concatenate([grads[4], grads[5]], axis=1)
    return (*grads[:4], duv, *grads[6:])


def _glu_bwd_fn(a, a_gate, cg, hx, dga, dchx, dpin, dbg):
    _, vjp = jax.vjp(_glu_and_gate_fn, a, a_gate, cg, hx)
    da, dag, dcg, dhx = vjp((dga, dchx))
    return jnp.concatenate([da, dag, dpin.astype(F32), dbg.astype(F32), dcg, dhx], axis=1)


def _rms_bwd_epilogue(dh, x, dres, g):
    _, vjp = jax.vjp(_rms, x, g)
    dx, dg = vjp(dh)
    return dx + dres, dg


def _residual_then_rms(acc, residual, g):
    x = acc + residual
    return x, _rms(x, g)


def _loss_fn(x, target, g):
    def f(x, g):
        err = jnp.square(_rms(x, g) - target)
        return 0.5 * jnp.sum(jnp.mean(err, axis=-1, keepdims=True), axis=0, keepdims=True)

    loss, vjp = jax.vjp(f, x, g)
    dx, dg = vjp(jnp.ones((1, 1), F32))
    return dx, dg, jnp.broadcast_to(loss, (1, PACK_LANES))


def _adamw_fn(w, g, m, v):
    m = ADAM_B1 * m + (1.0 - ADAM_B1) * g
    v = ADAM_B2 * v + (1.0 - ADAM_B2) * jnp.square(g)
    m_hat = m / (1.0 - ADAM_B1 ** ADAM_STEP)
    v_hat = v / (1.0 - ADAM_B2 ** ADAM_STEP)
    delta = -ADAM_LR * (m_hat / (jnp.sqrt(v_hat) + ADAM_EPS) + ADAM_WD * w)
    return delta, m, v


def _group_pick(j):
    return [(j == g).astype(F32) for g in range(len(POOL_WINDOWS))]


SUBLANES = 8


def _causal_taps(x, taps):
    for r in range(min(SUBLANES, taps)):
        behind = x if r == 0 else pltpu.roll(x, r, 0)
        for q in range((taps - 1 - r) // SUBLANES + 1):
            start = HALO - SUBLANES * q
            yield SUBLANES * q + r, behind[start:start + SEQ_TILE, :]


def _rows_iota(n):
    return lax.broadcasted_iota(jnp.int32, (n, 1), 0)


def _seq_specs(n_time, causal):
    per = SEQ_TILE // HALO
    tile = pl.BlockSpec((SEQ_TILE, GROUP), lambda j, i: (i, j))
    if causal:
        halo = pl.BlockSpec((HALO, GROUP), lambda j, i: (jnp.maximum(i * per - 1, 0), j))
    else:
        halo = pl.BlockSpec((HALO, GROUP), lambda j, i: (jnp.minimum((i + 1) * per, n_time * per - 1), j))
    return tile, halo


def _seq_fwd(ga, proj, chx, conf_dw, sc_conv):
    s_len = ga.shape[0]
    n_time = s_len // SEQ_TILE
    ext = HALO + SEQ_TILE

    def body(ga_ref, ga_h, pin_ref, pin_h, chx_ref, chx_h, wc_ref, ws_ref, ca_ref, po_ref, cc_ref):
        j, i = pl.program_id(0), pl.program_id(1)
        keep = (i > 0).astype(F32)

        def extended(cur, halo):
            return jnp.concatenate([halo[...] * keep, cur[...]], axis=0)

        def conv(x, w_ref, taps):
            acc = None
            for d, rows in _causal_taps(x, taps):
                term = rows * w_ref[pl.ds(taps - 1 - d, 1), :]
                acc = term if acc is None else acc + term
            return acc

        ca_ref[...] = conv(extended(ga_ref, ga_h), wc_ref, CONF_KERNEL)
        cc_ref[...] = conv(extended(chx_ref, chx_h), ws_ref, SC_KERNEL)
        u = extended(pin_ref, pin_h)
        s1 = u + pltpu.roll(u, 1, 0)
        s2 = s1 + pltpu.roll(s1, 2, 0)
        s3 = s2 + pltpu.roll(s2, 4, 0)
        s4 = s3 + pltpu.roll(s3, 8, 0)
        pick = _group_pick(j)
        window_sum = s1 * pick[0] + s2 * pick[1] + s3 * pick[2] + s4 * pick[3]
        window = sum(float(w) * f for w, f in zip(POOL_WINDOWS, pick))
        pos = (_rows_iota(ext) + (i * SEQ_TILE - HALO + 1)).astype(F32)
        count = jnp.maximum(jnp.minimum(pos, window), 1.0)
        po_ref[...] = (window_sum / count - u)[HALO:, :]

    tile, halo = _seq_specs(n_time, causal=True)
    pin_tile = pl.BlockSpec((SEQ_TILE, GROUP), lambda j, i: (i, 2 * (MIX // GROUP) + j))
    pin_halo = pl.BlockSpec((HALO, GROUP), lambda j, i: (jnp.maximum(i * (SEQ_TILE // HALO) - 1, 0), 2 * (MIX // GROUP) + j))
    w_spec = lambda taps: pl.BlockSpec((taps, GROUP), lambda j, i: (0, j))
    return pl.pallas_call(
        body, name="seq_fwd", grid=(MIX // GROUP, n_time),
        in_specs=[tile, halo, pin_tile, pin_halo, tile, halo, w_spec(CONF_KERNEL), w_spec(SC_KERNEL)],
        out_specs=[tile] * 3, out_shape=[jax.ShapeDtypeStruct((s_len, MIX), F32)] * 3,
        compiler_params=_params("parallel", "arbitrary"),
    )(ga, ga, proj, proj, chx, chx, conf_dw, sc_conv)


def _seq_bwd(dca, dpooled, dcc, ga, chx, conf_dw, sc_conv):
    s_len = ga.shape[0]
    n_time = s_len // SEQ_TILE
    ext = HALO + SEQ_TILE

    def body(dca_ref, dca_h, dpo_ref, dpo_h, dcc_ref, dcc_h, ga_ref, ga_h, chx_ref, chx_h, wc_ref, ws_ref,
             dga_ref, dpin_ref, dchx_ref, dwc_ref, dws_ref, acc_c, acc_s):
        j, i = pl.program_id(0), pl.program_id(1)
        keep_prev = (i > 0).astype(F32)
        keep_next = (i < n_time - 1).astype(F32)

        def with_next(cur, halo):
            return jnp.concatenate([cur[...], halo[...] * keep_next], axis=0)

        def with_prev(cur, halo):
            return jnp.concatenate([halo[...] * keep_prev, cur[...]], axis=0)

        def conv_t(dy, w_ref, taps):
            acc = None
            for r in range(min(SUBLANES, taps)):
                ahead = dy if r == 0 else pltpu.roll(dy, ext - r, 0)
                for q in range((taps - 1 - r) // SUBLANES + 1):
                    term = ahead[SUBLANES * q:SUBLANES * q + SEQ_TILE, :] * w_ref[pl.ds(taps - 1 - (SUBLANES * q + r), 1), :]
                    acc = term if acc is None else acc + term
            return acc

        def tap_grads(dy_cur, x, acc_ref, dw_ref, taps):
            @pl.when(i == 0)
            def _():
                acc_ref[...] = jnp.zeros_like(acc_ref)

            for d, rows in _causal_taps(x, taps):
                prod = dy_cur * rows
                part = prod[:SUBLANES, :]
                for g in range(1, SEQ_TILE // SUBLANES):
                    part = part + prod[g * SUBLANES:(g + 1) * SUBLANES, :]
                acc_ref[pl.ds((taps - 1 - d) * SUBLANES, SUBLANES), :] += part

            @pl.when(i == n_time - 1)
            def _():
                for k in range(taps):
                    dw_ref[pl.ds(k, 1), :] = jnp.sum(acc_ref[pl.ds(k * SUBLANES, SUBLANES), :], axis=0, keepdims=True)

        dga_ref[...] = conv_t(with_next(dca_ref, dca_h), wc_ref, CONF_KERNEL)
        dchx_ref[...] = conv_t(with_next(dcc_ref, dcc_h), ws_ref, SC_KERNEL)
        tap_grads(dca_ref[...], with_prev(ga_ref, ga_h), acc_c, dwc_ref, CONF_KERNEL)
        tap_grads(dcc_ref[...], with_prev(chx_ref, chx_h), acc_s, dws_ref, SC_KERNEL)

        dpo = with_next(dpo_ref, dpo_h)
        pick = _group_pick(j)
        window = sum(float(w) * f for w, f in zip(POOL_WINDOWS, pick))
        pos = (_rows_iota(ext) + (i * SEQ_TILE + 1)).astype(F32)
        r = dpo / jnp.minimum(pos, window)
        q1 = r + pltpu.roll(r, ext - 1, 0)
        q2 = q1 + pltpu.roll(q1, ext - 2, 0)
        q3 = q2 + pltpu.roll(q2, ext - 4, 0)
        q4 = q3 + pltpu.roll(q3, ext - 8, 0)
        q = q1 * pick[0] + q2 * pick[1] + q3 * pick[2] + q4 * pick[3]
        dpin_ref[...] = (q - dpo)[:SEQ_TILE, :].astype(dpin_ref.dtype)

    tile, nxt = _seq_specs(n_time, causal=False)
    _, prv = _seq_specs(n_time, causal=True)
    w_spec = lambda taps: pl.BlockSpec((taps, GROUP), lambda j, i: (0, j))
    return pl.pallas_call(
        body, name="seq_bwd", grid=(MIX // GROUP, n_time),
        in_specs=[tile, nxt, tile, nxt, tile, nxt, tile, prv, tile, prv, w_spec(CONF_KERNEL), w_spec(SC_KERNEL)],
        out_specs=[tile, tile, tile, w_spec(CONF_KERNEL), w_spec(SC_KERNEL)],
        out_shape=[jax.ShapeDtypeStruct((s_len, MIX), F32), jax.ShapeDtypeStruct((s_len, MIX), BF16),
                   jax.ShapeDtypeStruct((s_len, MIX), F32), jax.ShapeDtypeStruct(conf_dw.shape, F32),
                   jax.ShapeDtypeStruct(sc_conv.shape, F32)],
        scratch_shapes=[pltpu.VMEM((CONF_KERNEL * SUBLANES, GROUP), F32), pltpu.VMEM((SC_KERNEL * SUBLANES, GROUP), F32)],
        compiler_params=_params("parallel", "arbitrary"),
    )(dca, dca, dpooled, dpooled, dcc, dcc, ga, ga, chx, chx, conf_dw, sc_conv)


MERGE_TILE = 512
MERGE_ROWS = 1024


def _merge_fwd(y, w_branch, gates):
    s_len, d_model = y.shape[0], w_branch.shape[2]
    t = min(MERGE_ROWS, s_len)
    per_branch = d_model // MERGE_TILE

    def body(y_ref, w_ref, gate_ref, out_ref, acc_ref):
        k = pl.program_id(2)

        @pl.when(k == 0)
        def _():
            acc_ref[...] = jnp.zeros_like(acc_ref)

        z = jnp.dot(y_ref[...], w_ref[...], preferred_element_type=F32)
        acc_ref[...] += gate_ref[...].astype(F32) * z

        @pl.when(k == 3)
        def _():
            out_ref[...] = acc_ref[...].astype(out_ref.dtype)

    return pl.pallas_call(
        body, name="merge_fwd", grid=(s_len // t, per_branch, 4),
        in_specs=[pl.BlockSpec((t, MIX), lambda i, j, k: (i, k)),
                  pl.BlockSpec((None, MIX, MERGE_TILE), lambda i, j, k: (k, 0, j)),
                  pl.BlockSpec((t, MERGE_TILE), lambda i, j, k: (i, k * per_branch + j))],
        out_specs=pl.BlockSpec((t, MERGE_TILE), lambda i, j, k: (i, j)),
        out_shape=jax.ShapeDtypeStruct((s_len, d_model), BF16),
        scratch_shapes=[pltpu.VMEM((t, MERGE_TILE), F32)],
        compiler_params=_params("parallel", "parallel", "arbitrary"),
    )(y, w_branch, gates)


def _merge_bwd(y, w_branch, gates, dmerged):
    s_len, d_model = y.shape[0], w_branch.shape[2]
    t = min(MERGE_ROWS, s_len)
    per_branch = d_model // MERGE_TILE

    def body(y_ref, w_ref, gate_ref, dm_ref, dz_ref, dgate_ref):
        z = jnp.dot(y_ref[...], w_ref[...], preferred_element_type=F32)
        gate = gate_ref[...].astype(F32)
        dm = dm_ref[...]
        dz_ref[...] = (dm * gate).astype(dz_ref.dtype)
        dgate_ref[...] = (dm * z * gate * (1.0 - gate)).astype(dgate_ref.dtype)

    return pl.pallas_call(
        body, name="merge_bwd", grid=(s_len // t, per_branch, 4),
        in_specs=[pl.BlockSpec((t, MIX), lambda i, j, k: (i, k)),
                  pl.BlockSpec((None, MIX, MERGE_TILE), lambda i, j, k: (k, 0, j)),
                  pl.BlockSpec((t, MERGE_TILE), lambda i, j, k: (i, k * per_branch + j)),
                  pl.BlockSpec((t, MERGE_TILE), lambda i, j, k: (i, j))],
        out_specs=[pl.BlockSpec((None, t, MERGE_TILE), lambda i, j, k: (k, i, j)),
                   pl.BlockSpec((t, MERGE_TILE), lambda i, j, k: (i, (4 + k) * per_branch + j))],
        out_shape=[jax.ShapeDtypeStruct((4, s_len, d_model), BF16), jax.ShapeDtypeStruct((s_len, 8 * d_model), BF16)],
        compiler_params=_params("parallel", "parallel", "arbitrary"),
    )(y, w_branch, gates, dmerged)


def _row(v):
    return v.reshape(1, -1)


def _branch_params(w):
    mats = [w["pool_w"][g] for g in range(4)] + [w["gmlp_ws"][g] for g in range(4)]
    mats += [w["gmlp_bs"][g].reshape(GROUP, 1) for g in range(4)]
    return [_row(w["conf_dw_b"]), _row(w["conf_ln_g"]), _row(w["conf_ln_b"]), _row(w["pool_scale"]),
            _row(w["gmlp_ln_g"]), _row(w["gmlp_ln_b"])] + mats


def _layer_fwd(x0, p_l, w, fetch):
    d_model = x0.shape[1]
    (h1,) = _rowwise("rms_mix", _rms, [(x0, d_model, 0)], [_row(w["norm_mix"])], [(d_model, BF16)])
    w.update(fetch(0, h1))
    (proj,) = _mm2("proj", h1, w["w_in"], b_by_chip=True, chips=(0, 2))
    w.update(fetch(1, proj))
    (gates,) = _mm2("proj_gates", h1, w["w_in"], b_by_chip=True, chips=(2, 2), out_dtypes=(BF16,),
                    epi=lambda acc: (jax.nn.sigmoid(acc),))
    ga, chx = _rowwise("glu", _glu_and_gate_fn, [(proj, MIX, 0), (proj, MIX, 1), (proj, MIX, 4), (proj, MIX, 5)], [],
                       [(MIX, F32), (MIX, F32)])
    ca, pooled, cc = _seq_fwd(ga, proj, chx, w["conf_dw"], w["sc_conv"])
    (y,) = _rowwise("branches", _branches_joined_fn,
                    [(ca, MIX, 0), (pooled, MIX, 0), (cc, MIX, 0), (proj, MIX, 3), (proj, MIX, 6), (proj, MIX, 7)],
                    _branch_params(w), [(4 * MIX, BF16)], tm=BRANCH_ROWS)
    merged = _merge_fwd(y, w["w_branch"], gates)
    x1, h2 = _mm2("out_proj", merged, w["w_out"], res=[(x0, 0)], rows=[_row(w["norm_mlp"])], out_dtypes=(F32, BF16),
                  epi=_residual_then_rms)
    (act,) = _mm2("mlp_up", h2, w["w_up"], b_by_chip=True, out_dtypes=(BF16,),
                  epi=lambda acc: (jnp.square(jnp.maximum(acc, 0.0)),))
    x2, h3 = _mm2("mlp_down", act, w["w_down"], res=[(x1, 0)], rows=[_row(w["norm_ple"])], out_dtypes=(F32, BF16),
                  epi=_residual_then_rms)
    (pg,) = _mm2("ple_gate", h3, w["w_ple_gate"])
    (x3,) = _mm2("ple", p_l, w["w_ple"], res=[(pg, 0), (x2, 0)],
                 epi=lambda acc, g, r: (acc * jax.nn.sigmoid(g) + r,))
    saved = dict(x0=x0, h1=h1, proj=proj, gates=gates, ga=ga, chx=chx, ca=ca, pooled=pooled, cc=cc, y=y, merged=merged, x1=x1,
                 h2=h2, act=act, x2=x2, h3=h3, pg=pg)
    return x3, saved


def _layer_bwd(dx3, p_l, w, s, send):
    d_model = dx3.shape[1]
    big, small = {}, {}

    def ple_epi(acc, dx, g):
        sig = jax.nn.sigmoid(g)
        return dx * sig, dx * acc * sig * (1.0 - sig)

    dpp, dpg = _mm2("ple_bwd", p_l, w["w_ple"], res=[(dx3, 0), (s["pg"], 0)], epi=ple_epi, out_dtypes=(BF16, BF16))
    (big["w_ple"],) = _mm2("dw_ple", p_l, dpp, ta=True, out_dtypes=(BF16,))
    (big["w_ple_gate"],) = _mm2("dw_ple_gate", s["h3"], dpg, ta=True, out_dtypes=(BF16,))
    dx2, dg = _mm2("dh_ple", dpg, w["w_ple_gate"], tb=True, res=[(s["x2"], 0), (dx3, 0)], rows=[_row(w["norm_ple"])],
                   epi=_rms_bwd_epilogue, n_row_outs=1)
    small["norm_ple"] = _sum_row_out(dg)

    (dup,) = _mm2("mlp_down_bwd", dx2, w["w_down"], tb=True, res=[(s["act"], 0)], out_dtypes=(BF16,),
                  epi=lambda acc, act: (acc * 2.0 * jnp.sqrt(act.astype(F32)),))
    (big["w_down"],) = _mm2("dw_down", s["act"], dx2, ta=True, out_dtypes=(BF16,))
    (big["w_up"],) = _mm2("dw_up", s["h2"], dup, ta=True, out_dtypes=(BF16,))
    sent = send(0, [big[n] for n in GRAD_GROUPS[0]])
    dx1, dg = _mm2("mlp_up_bwd", dup, w["w_up"], tb=True, b_by_chip=True, res=[(s["x1"], 0), (dx2, 0)],
                   rows=[_row(w["norm_mlp"]) + sent], epi=_rms_bwd_epilogue, n_row_outs=1)
    small["norm_mlp"] = _sum_row_out(dg)

    (dmerged,) = _mm2("out_proj_bwd", dx1, w["w_out"], tb=True)
    (big["w_out"],) = _mm2("dw_out", s["merged"], dx1, ta=True, out_dtypes=(BF16,))
    dz, dproj = _merge_bwd(s["y"], w["w_branch"], s["gates"], dmerged)
    (dy,) = _mm("branch_bwd", dz, w["w_branch"], tb=True)
    (big["w_branch"],) = _mm("dw_branch", s["y"], dz, ta=True, a_cols=MIX, out_dtypes=(BF16,))

    proj = s["proj"]
    params = _branch_params(w)
    grads = _rowwise(
        "branches_bwd", _branches_bwd_fn,
        [(s["ca"], MIX, 0), (s["pooled"], MIX, 0), (s["cc"], MIX, 0), (proj, MIX, 3), (proj, MIX, 6), (proj, MIX, 7)]
        + [(dy, MIX, 0, g) for g in range(4)], params,
        [(MIX, F32), (MIX, F32), (MIX, F32), (MIX, BF16), (2 * MIX, BF16, dproj, 3)], [p.shape for p in params], tm=BRANCH_ROWS)
    dca, dpooled, dcc, dbg, dproj = grads[:5]
    pg_ = grads[5:]
    small["conf_dw_b"], small["conf_ln_g"], small["conf_ln_b"], small["pool_scale"] = pg_[0], pg_[1], pg_[2], pg_[3]
    small["gmlp_ln_g"], small["gmlp_ln_b"] = pg_[4], pg_[5]
    small["pool_w"] = jnp.stack(pg_[6:10])
    small["gmlp_ws"] = jnp.stack(pg_[10:14])
    small["gmlp_bs"] = jnp.stack([b.reshape(GROUP) for b in pg_[14:18]])

    dga, dpin, dchx, small["conf_dw"], small["sc_conv"] = _seq_bwd(dca, dpooled, dcc, s["ga"], s["chx"], w["conf_dw"], w["sc_conv"])
    (dproj,) = _rowwise("glu_bwd", _glu_bwd_fn,
                        [(proj, MIX, 0), (proj, MIX, 1), (proj, MIX, 4), (proj, MIX, 5), (dga, MIX, 0), (dchx, MIX, 0),
                         (dpin, MIX, 0), (dbg, MIX, 0)], [], [(6 * MIX, BF16, dproj, 0)])
    (big["w_in"],) = _mm2("dw_in", s["h1"], dproj, ta=True, out_dtypes=(BF16,))
    sent = send(1, [big[n] for n in GRAD_GROUPS[1]])
    dx0, dg = _mm2("proj_bwd", dproj, w["w_in"], tb=True, b_by_chip=True, res=[(s["x0"], 0), (dx1, 0)],
                   rows=[_row(w["norm_mix"]) + sent], epi=_rms_bwd_epilogue, n_row_outs=1)
    small["norm_mix"] = _sum_row_out(dg)
    return dx0, small


def _position():
    return lax.axis_index("x"), lax.axis_index("y"), lax.axis_index("c")


def _other_chips(x, y):
    return [(1 - x, y), (x, 1 - y), (1 - x, 1 - y)]


def _region(ref, cuts):
    return ref.at[tuple(pl.ds(*cuts[a]) if a in cuts else slice(None) for a in range(len(ref.shape)))]


def _allgather_small(name, block):
    m_per, n = block.shape

    def body(x_ref, out_ref, send_sems, recv_sems, local_sem):
        x, y, c = _position()
        me, sibling = (x, y, c), (x, y, 1 - c)
        chips = _other_chips(x, y)

        def rows(px, py, pc):
            return out_ref.at[pl.ds((4 * px + 2 * py + pc) * m_per, m_per), :]

        def copy(k, blk, to, src=None):
            return pltpu.make_async_remote_copy(
                src_ref=rows(*blk) if src is None else src, dst_ref=rows(*blk), send_sem=send_sems.at[k],
                recv_sem=recv_sems.at[k], device_id=to, device_id_type=pl.DeviceIdType.MESH)

        mine = pltpu.make_async_copy(x_ref, rows(*me), local_sem)
        mine.start()
        first = [copy(0, me, sibling, src=x_ref)]
        first += [copy(1 + j, me, (*chip, c), src=x_ref) for j, chip in enumerate(chips)]
        for cp in first:
            cp.start()
        passed = [copy(4 + j, (*chip, c), sibling) for j, chip in enumerate(chips)]
        for j, chip in enumerate(chips):
            copy(1 + j, (*chip, c), me).wait_recv()
            passed[j].start()
        copy(0, sibling, me).wait_recv()
        for j, chip in enumerate(chips):
            copy(4 + j, (*chip, 1 - c), me).wait_recv()
        for cp in first + passed:
            cp.wait_send()
        mine.wait()

    return pl.pallas_call(
        body, name=name, out_shape=jax.ShapeDtypeStruct((8 * m_per, n), block.dtype), in_specs=[ANY], out_specs=ANY,
        scratch_shapes=[pltpu.SemaphoreType.DMA((7,)), pltpu.SemaphoreType.DMA((7,)), pltpu.SemaphoreType.DMA],
    )(block)


def _cuts(arr_shape, shard_axis, half_axis, shard=None, half=None):
    cuts = {}
    if shard is not None:
        n = arr_shape[shard_axis] // 4
        cuts[shard_axis] = (shard * n, n)
    if half is not None:
        n = arr_shape[half_axis] // 2
        cuts[half_axis] = (half * n, n)
    return cuts


def _full_shape(shard_shape, shard_axis):
    return tuple(4 * n if a == shard_axis else n for a, n in enumerate(shard_shape))


HBM = pl.BlockSpec(memory_space=pltpu.HBM)
SEM = pl.BlockSpec(memory_space=pltpu.SEMAPHORE)
DATAFLOW_EFFECT = pltpu.SideEffectType.DATAFLOW_SIDE_EFFECTING
TOKEN_SHAPE = (8, 128)


def _remote(src, dst, send_sems, recv_sems, k, peer):
    return pltpu.make_async_remote_copy(src_ref=src, dst_ref=dst, send_sem=send_sems.at[k], recv_sem=recv_sems.at[k],
                                        device_id=peer, device_id_type=pl.DeviceIdType.MESH)


def _split_start(name, sources, landings, plan):
    n_s, n_l = len(sources), len(landings)
    n_copies = len(plan([None] * n_s, [None] * n_l, None))

    def body(*refs):
        src, land = refs[:n_s], refs[n_s:n_s + n_l]
        send_sems, recv_sems, token = refs[n_s + n_l], refs[n_s + n_l + 1], refs[-1]
        for k, (s, d, peer) in enumerate(plan(src, land, _position())):
            _remote(s, d, send_sems, recv_sems, k, peer).start()
        token[...] = jnp.zeros_like(token)

    arrays = [pltpu.with_memory_space_constraint(a, pltpu.HBM) for a in (*sources, *landings)]
    outs = pl.pallas_call(
        body, name=name,
        out_shape=(pltpu.SemaphoreType.DMA((n_copies,)), pltpu.SemaphoreType.DMA((n_copies,)),
                   *[pltpu.HBM(a.shape, a.dtype) for a in arrays], jax.ShapeDtypeStruct(TOKEN_SHAPE, F32)),
        in_specs=[HBM] * (n_s + n_l),
        out_specs=(SEM, SEM, *[HBM] * (n_s + n_l), pl.BlockSpec(memory_space=pltpu.VMEM)),
        input_output_aliases={i: 2 + i for i in range(n_s + n_l)},
        compiler_params=pltpu.CompilerParams(has_side_effects=DATAFLOW_EFFECT),
    )(*arrays)
    return outs[0], outs[1], list(outs[2:2 + n_s]), list(outs[2 + n_s:2 + n_s + n_l]), outs[-1]


def _split_wait(name, started, after, plan):
    send_sems, recv_sems, sources, landings, _ = started
    n_s, n_l = len(sources), len(landings)

    def body(*refs):
        src, land = refs[:n_s], refs[n_s:n_s + n_l]
        send_sems_ref, recv_sems_ref = refs[n_s + n_l], refs[n_s + n_l + 1]
        for k, (s, d, peer) in enumerate(plan(src, land, _position())):
            cp = _remote(s, d, send_sems_ref, recv_sems_ref, k, peer)
            cp.wait_send()
            cp.wait_recv()

    outs = pl.pallas_call(
        body, name=name, out_shape=tuple(pltpu.HBM(a.shape, a.dtype) for a in (*sources, *landings)),
        in_specs=[*[HBM] * (n_s + n_l), SEM, SEM, ANY], out_specs=tuple([HBM] * (n_s + n_l)),
        input_output_aliases={i: i for i in range(n_s + n_l)},
        compiler_params=pltpu.CompilerParams(has_side_effects=DATAFLOW_EFFECT),
    )(*sources, *landings, send_sems, recv_sems, after)
    return list(outs[:n_s]), list(outs[n_s:])


def _exchange(name, inputs, out_shapes, plan, in_place=False):
    n_in = len(inputs)
    n_out = n_in if in_place else len(out_shapes)
    n_copies = len(plan([None] * n_in, [None] * n_out, None))

    def body(*refs):
        ins, outs = refs[:n_in], refs[n_in:n_in + n_out]
        send_sems, recv_sems = refs[n_in + n_out:]
        copies = [_remote(s, d, send_sems, recv_sems, k, peer)
                  for k, (s, d, peer) in enumerate(plan(ins, ins if in_place else outs, _position()))]
        for cp in copies:
            cp.start()
        for cp in copies:
            cp.wait()

    if in_place:
        out_shapes = [jax.ShapeDtypeStruct(a.shape, a.dtype) for a in inputs]
    return pl.pallas_call(
        body, name=name, in_specs=[ANY] * n_in, out_specs=[ANY] * n_out, out_shape=out_shapes,
        input_output_aliases={i: i for i in range(n_in)} if in_place else {},
        scratch_shapes=[pltpu.SemaphoreType.DMA((n_copies,)), pltpu.SemaphoreType.DMA((n_copies,))],
    )(*inputs)


def _half_rows(ref, half):
    n = ref.shape[0] // 2
    return ref.at[pl.ds(half * n, n)]


def _gather_plan(sources, landings, pos):
    if pos is None:
        return [None] * (3 * len(sources))
    x, y, c = pos
    copies = []
    for src, land in zip(sources, landings):
        for qx, qy in _other_chips(x, y):
            copies.append((_half_rows(src, c), _half_rows(land.at[2 * x + y], c), (qx, qy, c)))
    return copies


def _sibling_plan(landings, _, pos):
    if pos is None:
        return [None] * (3 * len(landings))
    x, y, c = pos
    copies = []
    for land in landings:
        for qx, qy in _other_chips(x, y):
            region = _half_rows(land.at[2 * qx + qy], c)
            copies.append((region, region, (x, y, 1 - c)))
    return copies


def _weights_start(tag, shards):
    landings = [lax.empty((4, *s.shape), BF16) for s in shards]
    return _split_start(f"weights_start_{tag}", shards, landings, _gather_plan)


def _weights_finish(tag, names, started, after):
    shards, landings = _split_wait(f"weights_wait_{tag}", started, after, _gather_plan)
    landings = _exchange("weights_to_sibling", landings, None, _sibling_plan, in_place=True)
    my_shard = 2 * lax.axis_index("x") + lax.axis_index("y")
    full = {}
    for name, shard, land in zip(names, shards, landings):
        sa = BIG_AXES[name][0]
        land = lax.dynamic_update_index_in_dim(land, shard, my_shard, axis=0)
        if name in BY_CHIP:
            full[name] = land
            continue
        axis = sa - 1
        moved = jnp.moveaxis(land, 0, axis)
        full[name] = moved.reshape(*moved.shape[:axis], 4 * moved.shape[axis + 1], *moved.shape[axis + 2:])
    return full


def _half_shape(shape, half_axis):
    return tuple(n // 2 if a == half_axis else n for a, n in enumerate(shape))


def _shard_shape(shape, shard_axis):
    return tuple(n // 4 if a == shard_axis else n for a, n in enumerate(shape))


def _flat2d(a):
    return a.reshape(-1, a.shape[-1])


ELEMENTWISE_TILE_ELEMENTS = 1 << 18


def _row_tile(rows, cols):
    tile = 1
    while rows % (2 * tile) == 0 and 2 * tile * cols <= ELEMENTWISE_TILE_ELEMENTS:
        tile *= 2
    assert tile >= 16 or tile == rows, (rows, cols)
    return tile


def _window_of(arr, axis, parts, index):
    flat = _flat2d(arr)
    rows, cols = flat.shape
    if axis == arr.ndim - 1:
        return flat, (rows, cols // parts), (0, index)
    assert axis == 0, axis
    return flat, (rows // parts, cols), (index, 0)


def _window_sum(name, source, window, index, others, out_dtype):
    w_rows, w_cols = window
    tm = _row_tile(w_rows, w_cols)
    blocks = w_rows // tm
    where = jnp.stack([jnp.asarray(i, jnp.int32) for i in index])

    def body(where_ref, src_ref, *refs):
        acc = src_ref[...].astype(F32)
        for r in refs[:-1]:
            acc = acc + r[...].astype(F32)
        refs[-1][...] = acc.astype(refs[-1].dtype)

    same = pl.BlockSpec((tm, w_cols), lambda i, where_ref: (i, 0))
    grid_spec = pltpu.PrefetchScalarGridSpec(
        num_scalar_prefetch=1, grid=(blocks,),
        in_specs=[pl.BlockSpec((tm, w_cols), lambda i, where_ref: (where_ref[0] * blocks + i, where_ref[1])), *[same] * len(others)],
        out_specs=same)
    return pl.pallas_call(body, name=name, grid_spec=grid_spec, out_shape=jax.ShapeDtypeStruct(window, out_dtype),
                          compiler_params=_params("arbitrary"))(where, source, *others)


def _sum_arrays(name, arrays, out_dtype):
    shape = arrays[0].shape
    flat = [_flat2d(a) for a in arrays]
    rows, cols = flat[0].shape
    tm = _row_tile(rows, cols)

    def fn(*tiles):
        acc = tiles[0].astype(F32)
        for t in tiles[1:]:
            acc = acc + t.astype(F32)
        return acc

    (out,) = _rowwise(name, fn, [(a, cols, 0) for a in flat], [], [(cols, out_dtype)], tm=tm)
    return out.reshape(shape)


def _grads_start(tag, names, grads):
    c = lax.axis_index("c")
    shapes = [g.shape for g in grads]
    axes = [(BIG_AXES[n][0] - 1, BIG_AXES[n][1] - 1) for n in names]

    def to_sibling(ins, outs, pos):
        if pos is None:
            return [None] * len(names)
        px, py, pc = pos
        return [(_region(ins[wi], _cuts(shapes[wi], sa, ha, half=1 - pc)), outs[wi], (px, py, 1 - pc))
                for wi, (sa, ha) in enumerate(axes)]

    landed = _exchange("grads_to_sibling", grads, [jax.ShapeDtypeStruct(_half_shape(s, ha), BF16) for s, (_, ha) in zip(shapes, axes)],
                       to_sibling)
    chip_sums = []
    for wi, name in enumerate(names):
        source, window, index = _window_of(grads[wi], axes[wi][1], 2, c)
        chip_sum = _window_sum("chip_sum_" + name, source, window, index, [_flat2d(landed[wi])], BF16)
        chip_sums.append(chip_sum.reshape(landed[wi].shape))

    def to_chips(sources, landings, pos):
        if pos is None:
            return [None] * (3 * len(names))
        px, py, pc = pos
        copies = []
        for wi, (sa, ha) in enumerate(axes):
            for j, (qx, qy) in enumerate(_other_chips(px, py)):
                piece = _region(sources[wi], _cuts(shapes[wi], sa, ha, shard=2 * qx + qy))
                copies.append((piece, landings[wi].at[j], (qx, qy, pc)))
        return copies

    landings = [lax.empty((3, *_shard_shape(t.shape, sa)), BF16) for t, (sa, _) in zip(chip_sums, axes)]
    return _split_start(f"grads_start_{tag}", chip_sums, landings, to_chips), to_chips


def _grads_finish(tag, names, started, after):
    chip_sums, pieces = _split_wait(f"grads_wait_{tag}", started[0], after, started[1])
    my_shard = 2 * lax.axis_index("x") + lax.axis_index("y")
    reduced = {}
    for name, chip_sum, piece in zip(names, chip_sums, pieces):
        source, window, index = _window_of(chip_sum, BIG_AXES[name][0] - 1, 4, my_shard)
        total = _window_sum("shard_sum_" + name, source, window, index, [_flat2d(piece[j]) for j in range(3)], F32)
        reduced[name] = total.reshape(piece.shape[1:])
    return reduced


def _join_halves(halves, names):
    c = lax.axis_index("c")
    mine = [halves[l][name] for l in range(len(halves)) for name in names]

    def halves_to_sibling(ins, outs, pos):
        if pos is None:
            return [None] * len(mine)
        px, py, pc = pos
        return [(ins[k], outs[k], (px, py, 1 - pc)) for k in range(len(mine))]

    theirs = _exchange("grad_halves_to_sibling", mine, [jax.ShapeDtypeStruct(r.shape, F32) for r in mine], halves_to_sibling)
    out = []
    for wi, name in enumerate(names):
        ha = BIG_AXES[name][1]
        layers = []
        for l in range(len(halves)):
            a, b = mine[l * len(names) + wi], theirs[l * len(names) + wi]
            layers.append(jnp.concatenate([jnp.where(c == 0, a, b), jnp.where(c == 0, b, a)], axis=ha - 1))
        out.append(jnp.stack(layers))
    return out


def _pack(arrays):
    flat = jnp.concatenate([a.reshape(-1).astype(F32) for a in arrays])
    rows = -(-flat.shape[0] // PACK_LANES)
    rows = -(-rows // PACK_ROW_MULTIPLE) * PACK_ROW_MULTIPLE
    return jnp.pad(flat, (0, rows * PACK_LANES - flat.shape[0])).reshape(rows, PACK_LANES)


def _unpack(buf, shapes):
    flat, out, off = buf.reshape(-1), [], 0
    for s in shapes:
        n = 1
        for d in s:
            n *= d
        out.append(flat[off:off + n].reshape(s))
        off += n
    return out


def _sum8(name, gathered, rows):
    def body(in_ref, out_ref):
        acc = in_ref[0].astype(F32)
        for d in range(1, 8):
            acc = acc + in_ref[d].astype(F32)
        out_ref[...] = acc

    return pl.pallas_call(
        body, name=name, grid=(rows // PACK_ROW_MULTIPLE,),
        in_specs=[pl.BlockSpec((8, PACK_ROW_MULTIPLE, PACK_LANES), lambda i: (0, i, 0))],
        out_specs=pl.BlockSpec((PACK_ROW_MULTIPLE, PACK_LANES), lambda i: (i, 0)),
        out_shape=jax.ShapeDtypeStruct((rows, PACK_LANES), F32), compiler_params=_params("arbitrary"),
    )(gathered.reshape(8, rows, PACK_LANES))


def _adamw(name, w, g, m, v):
    shape = w.shape
    flat = [_flat2d(a) for a in (w, g, m, v)]
    rows, cols = flat[0].shape
    tm = _row_tile(rows, cols)
    outs = _rowwise(name, _adamw_fn, [(a, cols, 0) for a in flat], [], [(cols, F32)] * 3, tm=tm)
    return [o.reshape(shape) for o in outs]


WEIGHT_ORDER = ("norm_mix", "w_in", "conf_dw", "conf_dw_b", "conf_ln_g", "conf_ln_b", "pool_w", "pool_scale", "sc_conv",
                "gmlp_ln_g", "gmlp_ln_b", "gmlp_ws", "gmlp_bs", "w_branch", "w_out", "norm_mlp", "w_up", "w_down",
                "norm_ple", "w_ple", "w_ple_gate", "norm_final")


def _local_step(x, p, loss_target, weights, layer_big, full_small_sharded, on_big_grads):
    n_layers = p.shape[0]
    d_model = x.shape[1]
    layer_w, saved = [], []
    h = x
    for l in range(n_layers):
        w = {n: weights[n][l] for n in SMALL_REP if n != "norm_final"}
        w.update({n: full_small_sharded[n][l] for n in SMALL_SHARDED})
        layer_w.append(w)
        h, s = _layer_fwd(h, p[l], w, layer_big(l))
        saved.append(s)
    dx, d_final, loss = _rowwise("loss", _loss_fn, [(h, d_model, 0), (loss_target, d_model, 0)], [_row(weights["norm_final"])],
                                 [(d_model, F32)], [(1, d_model), (1, PACK_LANES)])
    small_grads = {n: [None] * n_layers for n in SMALL_REP + SMALL_SHARDED if n != "norm_final"}
    for l in reversed(range(n_layers)):
        dx, small = _layer_bwd(dx, p[l], layer_w[l], saved[l], on_big_grads(l))
        for n in small:
            small_grads[n][l] = small[n].reshape(weights[n].shape[1:]) if n in SMALL_REP else small[n]
    small_grads = {n: jnp.stack(v) for n, v in small_grads.items()}
    small_grads["norm_final"] = d_final.reshape(-1)
    return loss, dx, small_grads


def kernel(x, p, norm_mix, w_in, conf_dw, conf_dw_b, conf_ln_g, conf_ln_b, pool_w, pool_scale, sc_conv, gmlp_ln_g, gmlp_ln_b, gmlp_ws, gmlp_bs, w_branch, w_out, norm_mlp, w_up, w_down, norm_ple, w_ple, w_ple_gate, norm_final, loss_target, m_norm_mix, m_w_in, m_conf_dw, m_conf_dw_b, m_conf_ln_g, m_conf_ln_b, m_pool_w, m_pool_scale, m_sc_conv, m_gmlp_ln_g, m_gmlp_ln_b, m_gmlp_ws, m_gmlp_bs, m_w_branch, m_w_out, m_norm_mlp, m_w_up, m_w_down, m_norm_ple, m_w_ple, m_w_ple_gate, m_norm_final, v_norm_mix, v_w_in, v_conf_dw, v_conf_dw_b, v_conf_ln_g, v_conf_ln_b, v_pool_w, v_pool_scale, v_sc_conv, v_gmlp_ln_g, v_gmlp_ln_b, v_gmlp_ws, v_gmlp_bs, v_w_branch, v_w_out, v_norm_mlp, v_w_up, v_w_down, v_norm_ple, v_w_ple, v_w_ple_gate, v_norm_final):
    given = dict(locals())
    weights = {n: given[n] for n in WEIGHT_ORDER}
    mom_m = {n: given["m_" + n] for n in WEIGHT_ORDER}
    mom_v = {n: given["v_" + n] for n in WEIGHT_ORDER}
    my_shard = 2 * lax.axis_index("x") + lax.axis_index("y")

    n_layers = p.shape[0]
    shard_shapes = [weights[n].shape for n in SMALL_SHARDED]
    gathered = _allgather_small("allgather_small_weights", _pack([weights[n] for n in SMALL_SHARDED]))
    after_small = gathered[0, 0] * 0.0
    after_small = jnp.where(after_small == 0.0, after_small, 0.0)

    weight_copies = [[None] * len(WEIGHT_GROUPS) for _ in range(n_layers)]
    started = after_small
    for l in range(n_layers):
        for g, names in enumerate(WEIGHT_GROUPS):
            shards = [(weights[n][l] + started if i == 0 else weights[n][l]).astype(BF16) for i, n in enumerate(names)]
            weight_copies[l][g] = _weights_start(f"{l}_{g}", shards)
            started = weight_copies[l][g][4][0, 0]
    per_dev = gathered.reshape(8, -1, PACK_LANES)
    full_small_sharded = {}
    chip_parts = [_unpack(per_dev[2 * s], shard_shapes) for s in range(4)]
    for i, n in enumerate(SMALL_SHARDED):
        full_small_sharded[n] = jnp.concatenate([chip_parts[s][i] for s in range(4)], axis=-1)

    grad_copies = [[None] * len(GRAD_GROUPS) for _ in range(n_layers)]

    def layer_big(l):
        return lambda g, after: _weights_finish(f"{l}_{g}", WEIGHT_GROUPS[g], weight_copies[l][g], after)

    def on_big_grads(l):
        def send(g, group_grads):
            grad_copies[l][g] = _grads_start(f"{l}_{g}", GRAD_GROUPS[g], group_grads)
            return grad_copies[l][g][0][4][0, 0]
        return send

    loss_row, dx, small_grads = _local_step(x[0] + started, p[:, 0], loss_target[0], weights, layer_big, full_small_sharded,
                                            on_big_grads)
    loss = lax.psum(loss_row[0, 0], MESH_AXES)

    grads, delta, new_m, new_v = {}, {}, {}, {}
    halves = [{} for _ in range(n_layers)]
    last_sent = (0, len(GRAD_GROUPS) - 1)
    for l in reversed(range(n_layers)):
        for g, names in enumerate(GRAD_GROUPS):
            if (l, g) != last_sent:
                halves[l].update(_grads_finish(f"{l}_{g}", names, grad_copies[l][g], dx))
    behind = dx
    for g, names in enumerate(GRAD_GROUPS):
        if g == last_sent[1]:
            halves[0].update(_grads_finish(f"0_{g}", names, grad_copies[0][g], behind))
        for n, joined in zip(names, _join_halves(halves, names)):
            grads[n] = joined
            delta[n], new_m[n], new_v[n] = _adamw("adamw_" + n, weights[n], joined, mom_m[n], mom_v[n])
            behind = delta[n]

    small_names = SMALL_REP + SMALL_SHARDED
    after_big = behind[(0,) * behind.ndim] * 0.0
    after_big = jnp.where(after_big == 0.0, after_big, 0.0)
    packed = _pack([small_grads[n] for n in small_names]) + after_big
    rows = packed.shape[0]
    summed = _sum8("sum_small_grads", _allgather_small("allgather_small_grads", packed.astype(BF16)), rows)
    small_full_shapes = [weights[n].shape for n in SMALL_REP] + [full_small_sharded[n].shape for n in SMALL_SHARDED]
    for n, g in zip(small_names, _unpack(summed, small_full_shapes)):
        if n in SMALL_SHARDED:
            width = weights[n].shape[-1]
            g = lax.dynamic_slice_in_dim(g, my_shard * width, width, axis=g.ndim - 1)
        grads[n] = g

    for group, names in (("rep", SMALL_REP), ("sharded", SMALL_SHARDED)):
        outs = _adamw("adamw_small_" + group, *[_pack([src[n] for n in names]) for src in (weights, grads, mom_m, mom_v)])
        shapes = [weights[n].shape for n in names]
        for dst, buf in zip((delta, new_m, new_v), outs):
            for n, a in zip(names, _unpack(buf, shapes)):
                dst[n] = a

    return (loss, dx[None], *[grads[n] for n in WEIGHT_ORDER], *[delta[n] for n in WEIGHT_ORDER],
            *[new_m[n] for n in WEIGHT_ORDER], *[new_v[n] for n in WEIGHT_ORDER])
```

```python
import jax
import jax.numpy as jnp
from jax import lax
from jax.experimental import pallas as pl
from jax.experimental.pallas import tpu as pltpu

F32, BF16 = jnp.float32, jnp.bfloat16
EPS = 1e-6
ADAM_LR, ADAM_B1, ADAM_B2, ADAM_EPS, ADAM_WD, ADAM_STEP = 0.001, 0.9, 0.999, 1e-08, 0.01, 10
CONF_KERNEL, SC_KERNEL = 31, 3
POOL_WINDOWS = (2, 4, 8, 16)
MIX = 512
GROUP = 128
HALO = 32
SEQ_TILE = 256
BRANCH_ROWS = 512
VMEM_LIMIT_BYTES = 48 * 1024 * 1024
MESH_AXES = ("x", "y", "c")
ANY = pl.BlockSpec(memory_space=pl.ANY)

BIG = (("w_in", 2, 1), ("w_branch", 3, 1), ("w_out", 1, 2), ("w_up", 2, 1), ("w_down", 1, 2),
       ("w_ple", 2, 1), ("w_ple_gate", 1, 2))
SMALL_REP = ("norm_mix", "conf_dw_b", "conf_ln_g", "conf_ln_b", "pool_w", "pool_scale", "gmlp_ln_g",
             "gmlp_ln_b", "gmlp_ws", "gmlp_bs", "norm_mlp", "norm_ple", "norm_final")
SMALL_SHARDED = ("conf_dw", "sc_conv")
BY_CHIP = ("w_in", "w_up")
BIG_AXES = {name: (shard_axis, half_axis) for name, shard_axis, half_axis in BIG}
WEIGHT_GROUPS = (("w_in",), ("w_branch", "w_out", "w_up", "w_down", "w_ple_gate", "w_ple"))
GRAD_GROUPS = (("w_ple", "w_ple_gate", "w_down", "w_up"), ("w_out", "w_branch", "w_in"))
PACK_LANES = 128
PACK_ROW_MULTIPLE = 256


def _params(*semantics):
    return pltpu.CompilerParams(dimension_semantics=semantics, vmem_limit_bytes=VMEM_LIMIT_BYTES)


def _accumulate(ref, value, first):
    @pl.when(first)
    def _():
        ref[...] = value.astype(ref.dtype)

    @pl.when(jnp.logical_not(first))
    def _():
        ref[...] += value.astype(ref.dtype)


def _rowwise(name, fn, rows, params, outs, accs=(), tm=256):
    n_rows = rows[0][0].shape[-2]
    tm = min(tm, n_rows)
    n_in, n_out = len(rows) + len(params), len(outs)
    targets = [(k, o[2]) for k, o in enumerate(outs) if len(o) == 4]

    def body(*refs):
        res = fn(*[r[...] for r in refs[:n_in]])
        res = res if isinstance(res, (tuple, list)) else (res,)
        out_refs = refs[n_in + len(targets):]
        for r, v in zip(out_refs[:n_out], res[:n_out]):
            r[...] = v.astype(r.dtype)
        first = pl.program_id(0) == 0
        for r, v in zip(out_refs[n_out:], res[n_out:]):
            _accumulate(r, v, first)

    in_specs = []
    for row in rows:
        arr, w, cb = row[:3]
        if len(row) == 4:
            in_specs.append(pl.BlockSpec((None, tm, w), lambda i, cb=cb, g=row[3]: (g, i, cb)))
        else:
            in_specs.append(pl.BlockSpec((tm, w), lambda i, cb=cb: (i, cb)))
    for p in params:
        in_specs.append(pl.BlockSpec(p.shape, lambda i, nd=p.ndim: (0,) * nd))
    in_specs += [ANY] * len(targets)
    out_specs = [pl.BlockSpec((tm, o[0]), lambda i, cb=(o[3] if len(o) == 4 else 0): (i, cb)) for o in outs]
    out_specs += [pl.BlockSpec(s, lambda i, nd=len(s): (0,) * nd) for s in accs]
    out_shape = [jax.ShapeDtypeStruct(o[2].shape if len(o) == 4 else (n_rows, o[0]), o[1]) for o in outs]
    out_shape += [jax.ShapeDtypeStruct(s, F32) for s in accs]
    return pl.pallas_call(
        body, name=name, grid=(n_rows // tm,), in_specs=in_specs, out_specs=out_specs, out_shape=out_shape,
        input_output_aliases={n_in + t: k for t, (k, _) in enumerate(targets)},
        compiler_params=_params("arbitrary"),
    )(*[r[0] for r in rows], *params, *[t for _, t in targets])


ROW_OUT_SUBLANES = 8


def _mm(name, a, b, *, ta=False, tb=False, a_cols=0, b_by_chip=False, chips=(0, 4), res=(), rows=(), epi=None, out_dtypes=(F32,), n_row_outs=0,
        tm=1024, tn=1024, tk=1024):
    if a_cols:
        assert ta
        g_, m_, k_ = a.shape[1] // a_cols, a_cols, a.shape[0]
    else:
        g_, m_, k_ = (a.shape[0], a.shape[2], a.shape[1]) if ta else a.shape
    if b_by_chip:
        n_ = b.shape[2] if tb else chips[1] * b.shape[3]
    else:
        n_ = b.shape[1] if tb else b.shape[2]
    if ta:
        tk = 2 * tk
    tm, tn, tk = min(tm, m_), min(tn, n_), min(tk, k_)
    if b_by_chip:
        tn, tk = (tn, min(tk, b.shape[3])) if tb else (min(tn, b.shape[3]), tk)
        per_chip = b.shape[3] // (tk if tb else tn)
    nk, n_res, n_rows, n_out = k_ // tk, len(res), len(rows), len(out_dtypes)
    n_in = 2 + n_res + n_rows
    dims = (((0 if ta else 1,), (1 if tb else 0,)), ((), ()))
    if epi is None:
        epi = lambda acc: (acc,)

    def body(*refs):
        a_ref, b_ref = refs[:2]
        res_refs, out_refs, acc_ref = refs[2:n_in], refs[n_in:n_in + n_out], refs[-1]
        row_out_refs = refs[n_in + n_out:-1]
        k = pl.program_id(3)
        product = lax.dot_general(a_ref[...].astype(BF16), b_ref[...].astype(BF16), dims, preferred_element_type=F32)

        def finish(acc):
            vals = epi(acc, *[r[...] for r in res_refs])
            for o, v in zip(out_refs, vals):
                o[...] = v.astype(o.dtype)
            for o, v in zip(row_out_refs, vals[n_out:]):
                o[...] = jnp.broadcast_to(v, o.shape)

        if nk == 1:
            finish(product)
            return

        @pl.when(k == 0)
        def _():
            acc_ref[...] = product

        @pl.when(k > 0)
        def _():
            acc_ref[...] += product

        @pl.when(k == nk - 1)
        def _():
            finish(acc_ref[...])

    if a_cols:
        a_spec = pl.BlockSpec((tk, tm), lambda g, i, j, k: (k, g * (m_ // tm) + i))
    elif ta:
        a_spec = pl.BlockSpec((None, tk, tm), lambda g, i, j, k: (g, k, i))
    else:
        a_spec = pl.BlockSpec((None, tm, tk), lambda g, i, j, k: (g, i, k))
    if b_by_chip and tb:
        b_spec = pl.BlockSpec((None, None, tn, tk), lambda g, i, j, k: (g, k // per_chip, j, k % per_chip))
    elif b_by_chip:
        b_spec = pl.BlockSpec((None, None, tk, tn), lambda g, i, j, k: (g, chips[0] + j // per_chip, k, j % per_chip))
    elif tb:
        b_spec = pl.BlockSpec((None, tn, tk), lambda g, i, j, k: (g, j, k))
    else:
        b_spec = pl.BlockSpec((None, tk, tn), lambda g, i, j, k: (g, k, j))
    res_specs = [pl.BlockSpec((None, tm, tn), lambda g, i, j, k, off=off: (g, i, j + off)) for _, off in res]
    row_specs = [pl.BlockSpec((None, 1, tn), lambda g, i, j, k: (g, 0, j))] * n_rows
    out_spec = pl.BlockSpec((None, tm, tn), lambda g, i, j, k: (g, i, j))
    row_out_spec = pl.BlockSpec((None, ROW_OUT_SUBLANES, tn), lambda g, i, j, k: (g, i, j))
    outs = pl.pallas_call(
        body, name=name, grid=(g_, m_ // tm, n_ // tn, nk),
        in_specs=[a_spec, b_spec, *res_specs, *row_specs], out_specs=[out_spec] * n_out + [row_out_spec] * n_row_outs,
        out_shape=[jax.ShapeDtypeStruct((g_, m_, n_), d) for d in out_dtypes]
        + [jax.ShapeDtypeStruct((g_, ROW_OUT_SUBLANES * (m_ // tm), n_), F32)] * n_row_outs,
        scratch_shapes=[pltpu.VMEM((tm, tn), F32)],
        compiler_params=_params("parallel", "parallel", "parallel", "arbitrary"),
    )(a, b, *[r for r, _ in res], *rows)
    return outs


def _mm2(name, a, b, *, res=(), rows=(), **kw):
    outs = _mm(name, a[None], b[None], res=[(r[None], off) for r, off in res], rows=[r[None] for r in rows], **kw)
    return [o[0] for o in outs]


def _sum_row_out(row_out):
    return jnp.sum(row_out[::ROW_OUT_SUBLANES], axis=0, keepdims=True)


@jax.custom_vjp
def _bdot(a, b):
    return jnp.dot(a.astype(BF16), b.astype(BF16), preferred_element_type=F32)


def _bdot_fwd(a, b):
    return _bdot(a, b), (a, b)


def _bdot_bwd(saved, g):
    a, b = saved
    gb = g.astype(BF16)
    da = lax.dot_general(gb, b.astype(BF16), (((1,), (1,)), ((), ())), preferred_element_type=F32)
    db = lax.dot_general(a.astype(BF16), gb, (((0,), (0,)), ((), ())), preferred_element_type=F32)
    return da.astype(a.dtype), db.astype(b.dtype)


_bdot.defvjp(_bdot_fwd, _bdot_bwd)


def _rms(x, g):
    return x * lax.rsqrt(jnp.mean(x * x, axis=-1, keepdims=True) + EPS) * g


def _layer_norm(x, g, b):
    mu = jnp.mean(x, axis=-1, keepdims=True)
    var = jnp.mean(jnp.square(x - mu), axis=-1, keepdims=True)
    return (x - mu) * lax.rsqrt(var + EPS) * g + b


def _glu_and_gate_fn(a, a_gate, cg, hx):
    return a * jax.nn.sigmoid(a_gate), cg * hx


def _branches_fn(ca, pooled, cc, bg, u, v, conf_b, conf_g, conf_beta, pool_scale, gm_g, gm_b, *mats):
    pool_w, ws, bs = mats[0:4], mats[4:8], mats[8:12]
    ya = jax.nn.silu(_layer_norm(ca + conf_b, conf_g, conf_beta))
    yb = jnp.concatenate([_bdot(pooled[:, g * GROUP:(g + 1) * GROUP], pool_w[g]) for g in range(4)], axis=1) * pool_scale
    yc = bg * cc
    vln = _layer_norm(v, gm_g, gm_b)
    causal = lax.broadcasted_iota(jnp.int32, (GROUP, GROUP), 0) >= lax.broadcasted_iota(jnp.int32, (GROUP, GROUP), 1)
    chunks = v.shape[0] // GROUP

    def spatial(g):
        vg = vln[:, g * GROUP:(g + 1) * GROUP]
        wide = jnp.concatenate([vg[c * GROUP:(c + 1) * GROUP, :] for c in range(chunks)], axis=1)
        out = _bdot(jnp.where(causal, ws[g], 0.0), wide) + bs[g]
        return jnp.concatenate([out[:, c * GROUP:(c + 1) * GROUP] for c in range(chunks)], axis=0)

    sg = jnp.concatenate([spatial(g) for g in range(4)], axis=1)
    yd = u * sg
    return ya, yb, yc, yd


def _branches_joined_fn(*args):
    return jnp.concatenate(_branches_fn(*args), axis=1)


def _branches_bwd_fn(*args):
    ins, dys = args[:6] + args[10:], args[6:10]
    _, vjp = jax.vjp(_branches_fn, *ins)
    grads = vjp(tuple(dys))
    duv = jnp.concatenate([grads[4], grads[5]], axis=1)
    return (*grads[:4], duv, *grads[6:])


def _glu_bwd_fn(a, a_gate, cg, hx, dga, dchx, dpin, dbg):
    _, vjp = jax.vjp(_glu_and_gate_fn, a, a_gate, cg, hx)
    da, dag, dcg, dhx = vjp((dga, dchx))
    return jnp.concatenate([da, dag, dpin.astype(F32), dbg.astype(F32), dcg, dhx], axis=1)


def _rms_bwd_epilogue(dh, x, dres, g):
    _, vjp = jax.vjp(_rms, x, g)
    dx, dg = vjp(dh)
    return dx + dres, dg


def _residual_then_rms(acc, residual, g):
    x = acc + residual
    return x, _rms(x, g)


def _loss_fn(x, target, g):
    def f(x, g):
        err = jnp.square(_rms(x, g) - target)
        return 0.5 * jnp.sum(jnp.mean(err, axis=-1, keepdims=True), axis=0, keepdims=True)

    loss, vjp = jax.vjp(f, x, g)
    dx, dg = vjp(jnp.ones((1, 1), F32))
    return dx, dg, jnp.broadcast_to(loss, (1, PACK_LANES))


def _adamw_fn(w, g, m, v):
    m = ADAM_B1 * m + (1.0 - ADAM_B1) * g
    v = ADAM_B2 * v + (1.0 - ADAM_B2) * jnp.square(g)
    m_hat = m / (1.0 - ADAM_B1 ** ADAM_STEP)
    v_hat = v / (1.0 - ADAM_B2 ** ADAM_STEP)
    delta = -ADAM_LR * (m_hat / (jnp.sqrt(v_hat) + ADAM_EPS) + ADAM_WD * w)
    return delta, m, v


def _group_pick(j):
    return [(j == g).astype(F32) for g in range(len(POOL_WINDOWS))]


SUBLANES = 8


def _causal_taps(x, taps):
    for r in range(min(SUBLANES, taps)):
        behind = x if r == 0 else pltpu.roll(x, r, 0)
        for q in range((taps - 1 - r) // SUBLANES + 1):
            start = HALO - SUBLANES * q
            yield SUBLANES * q + r, behind[start:start + SEQ_TILE, :]


def _rows_iota(n):
    return lax.broadcasted_iota(jnp.int32, (n, 1), 0)


def _seq_specs(n_time, causal):
    per = SEQ_TILE // HALO
    tile = pl.BlockSpec((SEQ_TILE, GROUP), lambda j, i: (i, j))
    if causal:
        halo = pl.BlockSpec((HALO, GROUP), lambda j, i: (jnp.maximum(i * per - 1, 0), j))
    else:
        halo = pl.BlockSpec((HALO, GROUP), lambda j, i: (jnp.minimum((i + 1) * per, n_time * per - 1), j))
    return tile, halo


def _seq_fwd(ga, proj, chx, conf_dw, sc_conv):
    s_len = ga.shape[0]
    n_time = s_len // SEQ_TILE
    ext = HALO + SEQ_TILE

    def body(ga_ref, ga_h, pin_ref, pin_h, chx_ref, chx_h, wc_ref, ws_ref, ca_ref, po_ref, cc_ref):
        j, i = pl.program_id(0), pl.program_id(1)
        keep = (i > 0).astype(F32)

        def extended(cur, halo):
            return jnp.concatenate([halo[...] * keep, cur[...]], axis=0)

        def conv(x, w_ref, taps):
            acc = None
            for d, rows in _causal_taps(x, taps):
                term = rows * w_ref[pl.ds(taps - 1 - d, 1), :]
                acc = term if acc is None else acc + term
            return acc

        ca_ref[...] = conv(extended(ga_ref, ga_h), wc_ref, CONF_KERNEL)
        cc_ref[...] = conv(extended(chx_ref, chx_h), ws_ref, SC_KERNEL)
        u = extended(pin_ref, pin_h)
        s1 = u + pltpu.roll(u, 1, 0)
        s2 = s1 + pltpu.roll(s1, 2, 0)
        s3 = s2 + pltpu.roll(s2, 4, 0)
        s4 = s3 + pltpu.roll(s3, 8, 0)
        pick = _group_pick(j)
        window_sum = s1 * pick[0] + s2 * pick[1] + s3 * pick[2] + s4 * pick[3]
        window = sum(float(w) * f for w, f in zip(POOL_WINDOWS, pick))
        pos = (_rows_iota(ext) + (i * SEQ_TILE - HALO + 1)).astype(F32)
        count = jnp.maximum(jnp.minimum(pos, window), 1.0)
        po_ref[...] = (window_sum / count - u)[HALO:, :]

    tile, halo = _seq_specs(n_time, causal=True)
    pin_tile = pl.BlockSpec((SEQ_TILE, GROUP), lambda j, i: (i, 2 * (MIX // GROUP) + j))
    pin_halo = pl.BlockSpec((HALO, GROUP), lambda j, i: (jnp.maximum(i * (SEQ_TILE // HALO) - 1, 0), 2 * (MIX // GROUP) + j))
    w_spec = lambda taps: pl.BlockSpec((taps, GROUP), lambda j, i: (0, j))
    return pl.pallas_call(
        body, name="seq_fwd", grid=(MIX // GROUP, n_time),
        in_specs=[tile, halo, pin_tile, pin_halo, tile, halo, w_spec(CONF_KERNEL), w_spec(SC_KERNEL)],
        out_specs=[tile] * 3, out_shape=[jax.ShapeDtypeStruct((s_len, MIX), F32)] * 3,
        compiler_params=_params("parallel", "arbitrary"),
    )(ga, ga, proj, proj, chx, chx, conf_dw, sc_conv)


def _seq_bwd(dca, dpooled, dcc, ga, chx, conf_dw, sc_conv):
    s_len = ga.shape[0]
    n_time = s_len // SEQ_TILE
    ext = HALO + SEQ_TILE

    def body(dca_ref, dca_h, dpo_ref, dpo_h, dcc_ref, dcc_h, ga_ref, ga_h, chx_ref, chx_h, wc_ref, ws_ref,
             dga_ref, dpin_ref, dchx_ref, dwc_ref, dws_ref, acc_c, acc_s):
        j, i = pl.program_id(0), pl.program_id(1)
        keep_prev = (i > 0).astype(F32)
        keep_next = (i < n_time - 1).astype(F32)

        def with_next(cur, halo):
            return jnp.concatenate([cur[...], halo[...] * keep_next], axis=0)

        def with_prev(cur, halo):
            return jnp.concatenate([halo[...] * keep_prev, cur[...]], axis=0)

        def conv_t(dy, w_ref, taps):
            acc = None
            for r in range(min(SUBLANES, taps)):
                ahead = dy if r == 0 else pltpu.roll(dy, ext - r, 0)
                for q in range((taps - 1 - r) // SUBLANES + 1):
                    term = ahead[SUBLANES * q:SUBLANES * q + SEQ_TILE, :] * w_ref[pl.ds(taps - 1 - (SUBLANES * q + r), 1), :]
                    acc = term if acc is None else acc + term
            return acc

        def tap_grads(dy_cur, x, acc_ref, dw_ref, taps):
            @pl.when(i == 0)
            def _():
                acc_ref[...] = jnp.zeros_like(acc_ref)

            for d, rows in _causal_taps(x, taps):
                prod = dy_cur * rows
                part = prod[:SUBLANES, :]
                for g in range(1, SEQ_TILE // SUBLANES):
                    part = part + prod[g * SUBLANES:(g + 1) * SUBLANES, :]
                acc_ref[pl.ds((taps - 1 - d) * SUBLANES, SUBLANES), :] += part

            @pl.when(i == n_time - 1)
            def _():
                for k in range(taps):
                    dw_ref[pl.ds(k, 1), :] = jnp.sum(acc_ref[pl.ds(k * SUBLANES, SUBLANES), :], axis=0, keepdims=True)

        dga_ref[...] = conv_t(with_next(dca_ref, dca_h), wc_ref, CONF_KERNEL)
        dchx_ref[...] = conv_t(with_next(dcc_ref, dcc_h), ws_ref, SC_KERNEL)
        tap_grads(dca_ref[...], with_prev(ga_ref, ga_h), acc_c, dwc_ref, CONF_KERNEL)
        tap_grads(dcc_ref[...], with_prev(chx_ref, chx_h), acc_s, dws_ref, SC_KERNEL)

        dpo = with_next(dpo_ref, dpo_h)
        pick = _group_pick(j)
        window = sum(float(w) * f for w, f in zip(POOL_WINDOWS, pick))
        pos = (_rows_iota(ext) + (i * SEQ_TILE + 1)).astype(F32)
        r = dpo / jnp.minimum(pos, window)
        q1 = r + pltpu.roll(r, ext - 1, 0)
        q2 = q1 + pltpu.roll(q1, ext - 2, 0)
        q3 = q2 + pltpu.roll(q2, ext - 4, 0)
        q4 = q3 + pltpu.roll(q3, ext - 8, 0)
        q = q1 * pick[0] + q2 * pick[1] + q3 * pick[2] + q4 * pick[3]
        dpin_ref[...] = (q - dpo)[:SEQ_TILE, :].astype(dpin_ref.dtype)

    tile, nxt = _seq_specs(n_time, causal=False)
    _, prv = _seq_specs(n_time, causal=True)
    w_spec = lambda taps: pl.BlockSpec((taps, GROUP), lambda j, i: (0, j))
    return pl.pallas_call(
        body, name="seq_bwd", grid=(MIX // GROUP, n_time),
        in_specs=[tile, nxt, tile, nxt, tile, nxt, tile, prv, tile, prv, w_spec(CONF_KERNEL), w_spec(SC_KERNEL)],
        out_specs=[tile, tile, tile, w_spec(CONF_KERNEL), w_spec(SC_KERNEL)],
        out_shape=[jax.ShapeDtypeStruct((s_len, MIX), F32), jax.ShapeDtypeStruct((s_len, MIX), BF16),
                   jax.ShapeDtypeStruct((s_len, MIX), F32), jax.ShapeDtypeStruct(conf_dw.shape, F32),
                   jax.ShapeDtypeStruct(sc_conv.shape, F32)],
        scratch_shapes=[pltpu.VMEM((CONF_KERNEL * SUBLANES, GROUP), F32), pltpu.VMEM((SC_KERNEL * SUBLANES, GROUP), F32)],
        compiler_params=_params("parallel", "arbitrary"),
    )(dca, dca, dpooled, dpooled, dcc, dcc, ga, ga, chx, chx, conf_dw, sc_conv)


MERGE_TILE = 512
MERGE_ROWS = 1024


def _merge_fwd(y, w_branch, gates):
    s_len, d_model = y.shape[0], w_branch.shape[2]
    t = min(MERGE_ROWS, s_len)
    per_branch = d_model // MERGE_TILE

    def body(y_ref, w_ref, gate_ref, out_ref, acc_ref):
        k = pl.program_id(2)

        @pl.when(k == 0)
        def _():
            acc_ref[...] = jnp.zeros_like(acc_ref)

        z = jnp.dot(y_ref[...], w_ref[...], preferred_element_type=F32)
        acc_ref[...] += gate_ref[...].astype(F32) * z

        @pl.when(k == 3)
        def _():
            out_ref[...] = acc_ref[...].astype(out_ref.dtype)

    return pl.pallas_call(
        body, name="merge_fwd", grid=(s_len // t, per_branch, 4),
        in_specs=[pl.BlockSpec((t, MIX), lambda i, j, k: (i, k)),
                  pl.BlockSpec((None, MIX, MERGE_TILE), lambda i, j, k: (k, 0, j)),
                  pl.BlockSpec((t, MERGE_TILE), lambda i, j, k: (i, k * per_branch + j))],
        out_specs=pl.BlockSpec((t, MERGE_TILE), lambda i, j, k: (i, j)),
        out_shape=jax.ShapeDtypeStruct((s_len, d_model), BF16),
        scratch_shapes=[pltpu.VMEM((t, MERGE_TILE), F32)],
        compiler_params=_params("parallel", "parallel", "arbitrary"),
    )(y, w_branch, gates)


def _merge_bwd(y, w_branch, gates, dmerged):
    s_len, d_model = y.shape[0], w_branch.shape[2]
    t = min(MERGE_ROWS, s_len)
    per_branch = d_model // MERGE_TILE

    def body(y_ref, w_ref, gate_ref, dm_ref, dz_ref, dgate_ref):
        z = jnp.dot(y_ref[...], w_ref[...], preferred_element_type=F32)
        gate = gate_ref[...].astype(F32)
        dm = dm_ref[...]
        dz_ref[...] = (dm * gate).astype(dz_ref.dtype)
        dgate_ref[...] = (dm * z * gate * (1.0 - gate)).astype(dgate_ref.dtype)

    return pl.pallas_call(
        body, name="merge_bwd", grid=(s_len // t, per_branch, 4),
        in_specs=[pl.BlockSpec((t, MIX), lambda i, j, k: (i, k)),
                  pl.BlockSpec((None, MIX, MERGE_TILE), lambda i, j, k: (k, 0, j)),
                  pl.BlockSpec((t, MERGE_TILE), lambda i, j, k: (i, k * per_branch + j)),
                  pl.BlockSpec((t, MERGE_TILE), lambda i, j, k: (i, j))],
        out_specs=[pl.BlockSpec((None, t, MERGE_TILE), lambda i, j, k: (k, i, j)),
                   pl.BlockSpec((t, MERGE_TILE), lambda i, j, k: (i, (4 + k) * per_branch + j))],
        out_shape=[jax.ShapeDtypeStruct((4, s_len, d_model), BF16), jax.ShapeDtypeStruct((s_len, 8 * d_model), BF16)],
        compiler_params=_params("parallel", "parallel", "arbitrary"),
    )(y, w_branch, gates, dmerged)


def _row(v):
    return v.reshape(1, -1)


def _branch_params(w):
    mats = [w["pool_w"][g] for g in range(4)] + [w["gmlp_ws"][g] for g in range(4)]
    mats += [w["gmlp_bs"][g].reshape(GROUP, 1) for g in range(4)]
    return [_row(w["conf_dw_b"]), _row(w["conf_ln_g"]), _row(w["conf_ln_b"]), _row(w["pool_scale"]),
            _row(w["gmlp_ln_g"]), _row(w["gmlp_ln_b"])] + mats


def _layer_fwd(x0, p_l, w, fetch):
    d_model = x0.shape[1]
    (h1,) = _rowwise("rms_mix", _rms, [(x0, d_model, 0)], [_row(w["norm_mix"])], [(d_model, BF16)])
    w.update(fetch(0, h1))
    (proj,) = _mm2("proj", h1, w["w_in"], b_by_chip=True, chips=(0, 2))
    w.update(fetch(1, proj))
    (gates,) = _mm2("proj_gates", h1, w["w_in"], b_by_chip=True, chips=(2, 2), out_dtypes=(BF16,),
                    epi=lambda acc: (jax.nn.sigmoid(acc),))
    ga, chx = _rowwise("glu", _glu_and_gate_fn, [(proj, MIX, 0), (proj, MIX, 1), (proj, MIX, 4), (proj, MIX, 5)], [],
                       [(MIX, F32), (MIX, F32)])
    ca, pooled, cc = _seq_fwd(ga, proj, chx, w["conf_dw"], w["sc_conv"])
    (y,) = _rowwise("branches", _branches_joined_fn,
                    [(ca, MIX, 0), (pooled, MIX, 0), (cc, MIX, 0), (proj, MIX, 3), (proj, MIX, 6), (proj, MIX, 7)],
                    _branch_params(w), [(4 * MIX, BF16)], tm=BRANCH_ROWS)
    merged = _merge_fwd(y, w["w_branch"], gates)
    x1, h2 = _mm2("out_proj", merged, w["w_out"], res=[(x0, 0)], rows=[_row(w["norm_mlp"])], out_dtypes=(F32, BF16),
                  epi=_residual_then_rms)
    (act,) = _mm2("mlp_up", h2, w["w_up"], b_by_chip=True, out_dtypes=(BF16,),
                  epi=lambda acc: (jnp.square(jnp.maximum(acc, 0.0)),))
    x2, h3 = _mm2("mlp_down", act, w["w_down"], res=[(x1, 0)], rows=[_row(w["norm_ple"])], out_dtypes=(F32, BF16),
                  epi=_residual_then_rms)
    (pg,) = _mm2("ple_gate", h3, w["w_ple_gate"])
    (x3,) = _mm2("ple", p_l, w["w_ple"], res=[(pg, 0), (x2, 0)],
                 epi=lambda acc, g, r: (acc * jax.nn.sigmoid(g) + r,))
    saved = dict(x0=x0, h1=h1, proj=proj, gates=gates, ga=ga, chx=chx, ca=ca, pooled=pooled, cc=cc, y=y, merged=merged, x1=x1,
                 h2=h2, act=act, x2=x2, h3=h3, pg=pg)
    return x3, saved


def _layer_bwd(dx3, p_l, w, s, send):
    d_model = dx3.shape[1]
    big, small = {}, {}

    def ple_epi(acc, dx, g):
        sig = jax.nn.sigmoid(g)
        return dx * sig, dx * acc * sig * (1.0 - sig)

    dpp, dpg = _mm2("ple_bwd", p_l, w["w_ple"], res=[(dx3, 0), (s["pg"], 0)], epi=ple_epi, out_dtypes=(BF16, BF16))
    (big["w_ple"],) = _mm2("dw_ple", p_l, dpp, ta=True, out_dtypes=(BF16,))
    (big["w_ple_gate"],) = _mm2("dw_ple_gate", s["h3"], dpg, ta=True, out_dtypes=(BF16,))
    dx2, dg = _mm2("dh_ple", dpg, w["w_ple_gate"], tb=True, res=[(s["x2"], 0), (dx3, 0)], rows=[_row(w["norm_ple"])],
                   epi=_rms_bwd_epilogue, n_row_outs=1)
    small["norm_ple"] = _sum_row_out(dg)

    (dup,) = _mm2("mlp_down_bwd", dx2, w["w_down"], tb=True, res=[(s["act"], 0)], out_dtypes=(BF16,),
                  epi=lambda acc, act: (acc * 2.0 * jnp.sqrt(act.astype(F32)),))
    (big["w_down"],) = _mm2("dw_down", s["act"], dx2, ta=True, out_dtypes=(BF16,))
    (big["w_up"],) = _mm2("dw_up", s["h2"], dup, ta=True, out_dtypes=(BF16,))
    sent = send(0, [big[n] for n in GRAD_GROUPS[0]])
    dx1, dg = _mm2("mlp_up_bwd", dup, w["w_up"], tb=True, b_by_chip=True, res=[(s["x1"], 0), (dx2, 0)],
                   rows=[_row(w["norm_mlp"]) + sent], epi=_rms_bwd_epilogue, n_row_outs=1)
    small["norm_mlp"] = _sum_row_out(dg)

    (dmerged,) = _mm2("out_proj_bwd", dx1, w["w_out"], tb=True)
    (big["w_out"],) = _mm2("dw_out", s["merged"], dx1, ta=True, out_dtypes=(BF16,))
    dz, dproj = _merge_bwd(s["y"], w["w_branch"], s["gates"], dmerged)
    (dy,) = _mm("branch_bwd", dz, w["w_branch"], tb=True)
    (big["w_branch"],) = _mm("dw_branch", s["y"], dz, ta=True, a_cols=MIX, out_dtypes=(BF16,))

    proj = s["proj"]
    params = _branch_params(w)
    grads = _rowwise(
        "branches_bwd", _branches_bwd_fn,
        [(s["ca"], MIX, 0), (s["pooled"], MIX, 0), (s["cc"], MIX, 0), (proj, MIX, 3), (proj, MIX, 6), (proj, MIX, 7)]
        + [(dy, MIX, 0, g) for g in range(4)], params,
        [(MIX, F32), (MIX, F32), (MIX, F32), (MIX, BF16), (2 * MIX, BF16, dproj, 3)], [p.shape for p in params], tm=BRANCH_ROWS)
    dca, dpooled, dcc, dbg, dproj = grads[:5]
    pg_ = grads[5:]
    small["conf_dw_b"], small["conf_ln_g"], small["conf_ln_b"], small["pool_scale"] = pg_[0], pg_[1], pg_[2], pg_[3]
    small["gmlp_ln_g"], small["gmlp_ln_b"] = pg_[4], pg_[5]
    small["pool_w"] = jnp.stack(pg_[6:10])
    small["gmlp_ws"] = jnp.stack(pg_[10:14])
    small["gmlp_bs"] = jnp.stack([b.reshape(GROUP) for b in pg_[14:18]])

    dga, dpin, dchx, small["conf_dw"], small["sc_conv"] = _seq_bwd(dca, dpooled, dcc, s["ga"], s["chx"], w["conf_dw"], w["sc_conv"])
    (dproj,) = _rowwise("glu_bwd", _glu_bwd_fn,
                        [(proj, MIX, 0), (proj, MIX, 1), (proj, MIX, 4), (proj, MIX, 5), (dga, MIX, 0), (dchx, MIX, 0),
                         (dpin, MIX, 0), (dbg, MIX, 0)], [], [(6 * MIX, BF16, dproj, 0)])
    (big["w_in"],) = _mm2("dw_in", s["h1"], dproj, ta=True, out_dtypes=(BF16,))
    sent = send(1, [big[n] for n in GRAD_GROUPS[1]])
    dx0, dg = _mm2("proj_bwd", dproj, w["w_in"], tb=True, b_by_chip=True, res=[(s["x0"], 0), (dx1, 0)],
                   rows=[_row(w["norm_mix"]) + sent], epi=_rms_bwd_epilogue, n_row_outs=1)
    small["norm_mix"] = _sum_row_out(dg)
    return dx0, small


def _position():
    return lax.axis_index("x"), lax.axis_index("y"), lax.axis_index("c")


def _other_chips(x, y):
    return [(1 - x, y), (x, 1 - y), (1 - x, 1 - y)]


def _region(ref, cuts):
    return ref.at[tuple(pl.ds(*cuts[a]) if a in cuts else slice(None) for a in range(len(ref.shape)))]


def _allgather_small(name, block):
    m_per, n = block.shape

    def body(x_ref, out_ref, send_sems, recv_sems, local_sem):
        x, y, c = _position()
        me, sibling = (x, y, c), (x, y, 1 - c)
        chips = _other_chips(x, y)

        def rows(px, py, pc):
            return out_ref.at[pl.ds((4 * px + 2 * py + pc) * m_per, m_per), :]

        def copy(k, blk, to, src=None):
            return pltpu.make_async_remote_copy(
                src_ref=rows(*blk) if src is None else src, dst_ref=rows(*blk), send_sem=send_sems.at[k],
                recv_sem=recv_sems.at[k], device_id=to, device_id_type=pl.DeviceIdType.MESH)

        mine = pltpu.make_async_copy(x_ref, rows(*me), local_sem)
        mine.start()
        first = [copy(0, me, sibling, src=x_ref)]
        first += [copy(1 + j, me, (*chip, c), src=x_ref) for j, chip in enumerate(chips)]
        for cp in first:
            cp.start()
        passed = [copy(4 + j, (*chip, c), sibling) for j, chip in enumerate(chips)]
        for j, chip in enumerate(chips):
            copy(1 + j, (*chip, c), me).wait_recv()
            passed[j].start()
        copy(0, sibling, me).wait_recv()
        for j, chip in enumerate(chips):
            copy(4 + j, (*chip, 1 - c), me).wait_recv()
        for cp in first + passed:
            cp.wait_send()
        mine.wait()

    return pl.pallas_call(
        body, name=name, out_shape=jax.ShapeDtypeStruct((8 * m_per, n), block.dtype), in_specs=[ANY], out_specs=ANY,
        scratch_shapes=[pltpu.SemaphoreType.DMA((7,)), pltpu.SemaphoreType.DMA((7,)), pltpu.SemaphoreType.DMA],
    )(block)


def _cuts(arr_shape, shard_axis, half_axis, shard=None, half=None):
    cuts = {}
    if shard is not None:
        n = arr_shape[shard_axis] // 4
        cuts[shard_axis] = (shard * n, n)
    if half is not None:
        n = arr_shape[half_axis] // 2
        cuts[half_axis] = (half * n, n)
    return cuts


def _full_shape(shard_shape, shard_axis):
    return tuple(4 * n if a == shard_axis else n for a, n in enumerate(shard_shape))


HBM = pl.BlockSpec(memory_space=pltpu.HBM)
SEM = pl.BlockSpec(memory_space=pltpu.SEMAPHORE)
DATAFLOW_EFFECT = pltpu.SideEffectType.DATAFLOW_SIDE_EFFECTING
TOKEN_SHAPE = (8, 128)


def _remote(src, dst, send_sems, recv_sems, k, peer):
    return pltpu.make_async_remote_copy(src_ref=src, dst_ref=dst, send_sem=send_sems.at[k], recv_sem=recv_sems.at[k],
                                        device_id=peer, device_id_type=pl.DeviceIdType.MESH)


def _split_start(name, sources, landings, plan):
    n_s, n_l = len(sources), len(landings)
    n_copies = len(plan([None] * n_s, [None] * n_l, None))

    def body(*refs):
        src, land = refs[:n_s], refs[n_s:n_s + n_l]
        send_sems, recv_sems, token = refs[n_s + n_l], refs[n_s + n_l + 1], refs[-1]
        for k, (s, d, peer) in enumerate(plan(src, land, _position())):
            _remote(s, d, send_sems, recv_sems, k, peer).start()
        token[...] = jnp.zeros_like(token)

    arrays = [pltpu.with_memory_space_constraint(a, pltpu.HBM) for a in (*sources, *landings)]
    outs = pl.pallas_call(
        body, name=name,
        out_shape=(pltpu.SemaphoreType.DMA((n_copies,)), pltpu.SemaphoreType.DMA((n_copies,)),
                   *[pltpu.HBM(a.shape, a.dtype) for a in arrays], jax.ShapeDtypeStruct(TOKEN_SHAPE, F32)),
        in_specs=[HBM] * (n_s + n_l),
        out_specs=(SEM, SEM, *[HBM] * (n_s + n_l), pl.BlockSpec(memory_space=pltpu.VMEM)),
        input_output_aliases={i: 2 + i for i in range(n_s + n_l)},
        compiler_params=pltpu.CompilerParams(has_side_effects=DATAFLOW_EFFECT),
    )(*arrays)
    return outs[0], outs[1], list(outs[2:2 + n_s]), list(outs[2 + n_s:2 + n_s + n_l]), outs[-1]


def _split_wait(name, started, after, plan):
    send_sems, recv_sems, sources, landings, _ = started
    n_s, n_l = len(sources), len(landings)

    def body(*refs):
        src, land = refs[:n_s], refs[n_s:n_s + n_l]
        send_sems_ref, recv_sems_ref = refs[n_s + n_l], refs[n_s + n_l + 1]
        for k, (s, d, peer) in enumerate(plan(src, land, _position())):
            cp = _remote(s, d, send_sems_ref, recv_sems_ref, k, peer)
            cp.wait_send()
            cp.wait_recv()

    outs = pl.pallas_call(
        body, name=name, out_shape=tuple(pltpu.HBM(a.shape, a.dtype) for a in (*sources, *landings)),
        in_specs=[*[HBM] * (n_s + n_l), SEM, SEM, ANY], out_specs=tuple([HBM] * (n_s + n_l)),
        input_output_aliases={i: i for i in range(n_s + n_l)},
        compiler_params=pltpu.CompilerParams(has_side_effects=DATAFLOW_EFFECT),
    )(*sources, *landings, send_sems, recv_sems, after)
    return list(outs[:n_s]), list(outs[n_s:])


def _exchange(name, inputs, out_shapes, plan, in_place=False):
    n_in = len(inputs)
    n_out = n_in if in_place else len(out_shapes)
    n_copies = len(plan([None] * n_in, [None] * n_out, None))

    def body(*refs):
        ins, outs = refs[:n_in], refs[n_in:n_in + n_out]
        send_sems, recv_sems = refs[n_in + n_out:]
        copies = [_remote(s, d, send_sems, recv_sems, k, peer)
                  for k, (s, d, peer) in enumerate(plan(ins, ins if in_place else outs, _position()))]
        for cp in copies:
            cp.start()
        for cp in copies:
            cp.wait()

    if in_place:
        out_shapes = [jax.ShapeDtypeStruct(a.shape, a.dtype) for a in inputs]
    return pl.pallas_call(
        body, name=name, in_specs=[ANY] * n_in, out_specs=[ANY] * n_out, out_shape=out_shapes,
        input_output_aliases={i: i for i in range(n_in)} if in_place else {},
        scratch_shapes=[pltpu.SemaphoreType.DMA((n_copies,)), pltpu.SemaphoreType.DMA((n_copies,))],
    )(*inputs)


def _half_rows(ref, half):
    n = ref.shape[0] // 2
    return ref.at[pl.ds(half * n, n)]


def _gather_plan(sources, landings, pos):
    if pos is None:
        return [None] * (3 * len(sources))
    x, y, c = pos
    copies = []
    for src, land in zip(sources, landings):
        for qx, qy in _other_chips(x, y):
            copies.append((_half_rows(src, c), _half_rows(land.at[2 * x + y], c), (qx, qy, c)))
    return copies


def _sibling_plan(landings, _, pos):
    if pos is None:
        return [None] * (3 * len(landings))
    x, y, c = pos
    copies = []
    for land in landings:
        for qx, qy in _other_chips(x, y):
            region = _half_rows(land.at[2 * qx + qy], c)
            copies.append((region, region, (x, y, 1 - c)))
    return copies


def _weights_start(tag, shards):
    landings = [lax.empty((4, *s.shape), BF16) for s in shards]
    return _split_start(f"weights_start_{tag}", shards, landings, _gather_plan)


def _weights_finish(tag, names, started, after):
    shards, landings = _split_wait(f"weights_wait_{tag}", started, after, _gather_plan)
    landings = _exchange("weights_to_sibling", landings, None, _sibling_plan, in_place=True)
    my_shard = 2 * lax.axis_index("x") + lax.axis_index("y")
    full = {}
    for name, shard, land in zip(names, shards, landings):
        sa = BIG_AXES[name][0]
        land = lax.dynamic_update_index_in_dim(land, shard, my_shard, axis=0)
        if name in BY_CHIP:
            full[name] = land
            continue
        axis = sa - 1
        moved = jnp.moveaxis(land, 0, axis)
        full[name] = moved.reshape(*moved.shape[:axis], 4 * moved.shape[axis + 1], *moved.shape[axis + 2:])
    return full


def _half_shape(shape, half_axis):
    return tuple(n // 2 if a == half_axis else n for a, n in enumerate(shape))


def _shard_shape(shape, shard_axis):
    return tuple(n // 4 if a == shard_axis else n for a, n in enumerate(shape))


def _flat2d(a):
    return a.reshape(-1, a.shape[-1])


ELEMENTWISE_TILE_ELEMENTS = 1 << 18


def _row_tile(rows, cols):
    tile = 1
    while rows % (2 * tile) == 0 and 2 * tile * cols <= ELEMENTWISE_TILE_ELEMENTS:
        tile *= 2
    assert tile >= 16 or tile == rows, (rows, cols)
    return tile


def _window_of(arr, axis, parts, index):
    flat = _flat2d(arr)
    rows, cols = flat.shape
    if axis == arr.ndim - 1:
        return flat, (rows, cols // parts), (0, index)
    assert axis == 0, axis
    return flat, (rows // parts, cols), (index, 0)


def _window_sum(name, source, window, index, others, out_dtype):
    w_rows, w_cols = window
    tm = _row_tile(w_rows, w_cols)
    blocks = w_rows // tm
    where = jnp.stack([jnp.asarray(i, jnp.int32) for i in index])

    def body(where_ref, src_ref, *refs):
        acc = src_ref[...].astype(F32)
        for r in refs[:-1]:
            acc = acc + r[...].astype(F32)
        refs[-1][...] = acc.astype(refs[-1].dtype)

    same = pl.BlockSpec((tm, w_cols), lambda i, where_ref: (i, 0))
    grid_spec = pltpu.PrefetchScalarGridSpec(
        num_scalar_prefetch=1, grid=(blocks,),
        in_specs=[pl.BlockSpec((tm, w_cols), lambda i, where_ref: (where_ref[0] * blocks + i, where_ref[1])), *[same] * len(others)],
        out_specs=same)
    return pl.pallas_call(body, name=name, grid_spec=grid_spec, out_shape=jax.ShapeDtypeStruct(window, out_dtype),
                          compiler_params=_params("arbitrary"))(where, source, *others)


def _sum_arrays(name, arrays, out_dtype):
    shape = arrays[0].shape
    flat = [_flat2d(a) for a in arrays]
    rows, cols = flat[0].shape
    tm = _row_tile(rows, cols)

    def fn(*tiles):
        acc = tiles[0].astype(F32)
        for t in tiles[1:]:
            acc = acc + t.astype(F32)
        return acc

    (out,) = _rowwise(name, fn, [(a, cols, 0) for a in flat], [], [(cols, out_dtype)], tm=tm)
    return out.reshape(shape)


def _grads_start(tag, names, grads):
    c = lax.axis_index("c")
    shapes = [g.shape for g in grads]
    axes = [(BIG_AXES[n][0] - 1, BIG_AXES[n][1] - 1) for n in names]

    def to_sibling(ins, outs, pos):
        if pos is None:
            return [None] * len(names)
        px, py, pc = pos
        return [(_region(ins[wi], _cuts(shapes[wi], sa, ha, half=1 - pc)), outs[wi], (px, py, 1 - pc))
                for wi, (sa, ha) in enumerate(axes)]

    landed = _exchange("grads_to_sibling", grads, [jax.ShapeDtypeStruct(_half_shape(s, ha), BF16) for s, (_, ha) in zip(shapes, axes)],
                       to_sibling)
    chip_sums = []
    for wi, name in enumerate(names):
        source, window, index = _window_of(grads[wi], axes[wi][1], 2, c)
        chip_sum = _window_sum("chip_sum_" + name, source, window, index, [_flat2d(landed[wi])], BF16)
        chip_sums.append(chip_sum.reshape(landed[wi].shape))

    def to_chips(sources, landings, pos):
        if pos is None:
            return [None] * (3 * len(names))
        px, py, pc = pos
        copies = []
        for wi, (sa, ha) in enumerate(axes):
            for j, (qx, qy) in enumerate(_other_chips(px, py)):
                piece = _region(sources[wi], _cuts(shapes[wi], sa, ha, shard=2 * qx + qy))
                copies.append((piece, landings[wi].at[j], (qx, qy, pc)))
        return copies

    landings = [lax.empty((3, *_shard_shape(t.shape, sa)), BF16) for t, (sa, _) in zip(chip_sums, axes)]
    return _split_start(f"grads_start_{tag}", chip_sums, landings, to_chips), to_chips


def _grads_finish(tag, names, started, after):
    chip_sums, pieces = _split_wait(f"grads_wait_{tag}", started[0], after, started[1])
    my_shard = 2 * lax.axis_index("x") + lax.axis_index("y")
    reduced = {}
    for name, chip_sum, piece in zip(names, chip_sums, pieces):
        source, window, index = _window_of(chip_sum, BIG_AXES[name][0] - 1, 4, my_shard)
        total = _window_sum("shard_sum_" + name, source, window, index, [_flat2d(piece[j]) for j in range(3)], F32)
        reduced[name] = total.reshape(piece.shape[1:])
    return reduced


def _join_halves(halves, names):
    c = lax.axis_index("c")
    mine = [halves[l][name] for l in range(len(halves)) for name in names]

    def halves_to_sibling(ins, outs, pos):
        if pos is None:
            return [None] * len(mine)
        px, py, pc = pos
        return [(ins[k], outs[k], (px, py, 1 - pc)) for k in range(len(mine))]

    theirs = _exchange("grad_halves_to_sibling", mine, [jax.ShapeDtypeStruct(r.shape, F32) for r in mine], halves_to_sibling)
    out = []
    for wi, name in enumerate(names):
        ha = BIG_AXES[name][1]
        layers = []
        for l in range(len(halves)):
            a, b = mine[l * len(names) + wi], theirs[l * len(names) + wi]
            layers.append(jnp.concatenate([jnp.where(c == 0, a, b), jnp.where(c == 0, b, a)], axis=ha - 1))
        out.append(jnp.stack(layers))
    return out


def _pack(arrays):
    flat = jnp.concatenate([a.reshape(-1).astype(F32) for a in arrays])
    rows = -(-flat.shape[0] // PACK_LANES)
    rows = -(-rows // PACK_ROW_MULTIPLE) * PACK_ROW_MULTIPLE
    return jnp.pad(flat, (0, rows * PACK_LANES - flat.shape[0])).reshape(rows, PACK_LANES)


def _unpack(buf, shapes):
    flat, out, off = buf.reshape(-1), [], 0
    for s in shapes:
        n = 1
        for d in s:
            n *= d
        out.append(flat[off:off + n].reshape(s))
        off += n
    return out


def _sum8(name, gathered, rows):
    def body(in_ref, out_ref):
        acc = in_ref[0].astype(F32)
        for d in range(1, 8):
            acc = acc + in_ref[d].astype(F32)
        out_ref[...] = acc

    return pl.pallas_call(
        body, name=name, grid=(rows // PACK_ROW_MULTIPLE,),
        in_specs=[pl.BlockSpec((8, PACK_ROW_MULTIPLE, PACK_LANES), lambda i: (0, i, 0))],
        out_specs=pl.BlockSpec((PACK_ROW_MULTIPLE, PACK_LANES), lambda i: (i, 0)),
        out_shape=jax.ShapeDtypeStruct((rows, PACK_LANES), F32), compiler_params=_params("arbitrary"),
    )(gathered.reshape(8, rows, PACK_LANES))


def _adamw(name, w, g, m, v):
    shape = w.shape
    flat = [_flat2d(a) for a in (w, g, m, v)]
    rows, cols = flat[0].shape
    tm = _row_tile(rows, cols)
    outs = _rowwise(name, _adamw_fn, [(a, cols, 0) for a in flat], [], [(cols, F32)] * 3, tm=tm)
    return [o.reshape(shape) for o in outs]


WEIGHT_ORDER = ("norm_mix", "w_in", "conf_dw", "conf_dw_b", "conf_ln_g", "conf_ln_b", "pool_w", "pool_scale", "sc_conv",
                "gmlp_ln_g", "gmlp_ln_b", "gmlp_ws", "gmlp_bs", "w_branch", "w_out", "norm_mlp", "w_up", "w_down",
                "norm_ple", "w_ple", "w_ple_gate", "norm_final")


def _local_step(x, p, loss_target, weights, layer_big, full_small_sharded, on_big_grads):
    n_layers = p.shape[0]
    d_model = x.shape[1]
    layer_w, saved = [], []
    h = x
    for l in range(n_layers):
        w = {n: weights[n][l] for n in SMALL_REP if n != "norm_final"}
        w.update({n: full_small_sharded[n][l] for n in SMALL_SHARDED})
        layer_w.append(w)
        h, s = _layer_fwd(h, p[l], w, layer_big(l))
        saved.append(s)
    dx, d_final, loss = _rowwise("loss", _loss_fn, [(h, d_model, 0), (loss_target, d_model, 0)], [_row(weights["norm_final"])],
                                 [(d_model, F32)], [(1, d_model), (1, PACK_LANES)])
    small_grads = {n: [None] * n_layers for n in SMALL_REP + SMALL_SHARDED if n != "norm_final"}
    for l in reversed(range(n_layers)):
        dx, small = _layer_bwd(dx, p[l], layer_w[l], saved[l], on_big_grads(l))
        for n in small:
            small_grads[n][l] = small[n].reshape(weights[n].shape[1:]) if n in SMALL_REP else small[n]
    small_grads = {n: jnp.stack(v) for n, v in small_grads.items()}
    small_grads["norm_final"] = d_final.reshape(-1)
    return loss, dx, small_grads


def kernel(x, p, norm_mix, w_in, conf_dw, conf_dw_b, conf_ln_g, conf_ln_b, pool_w, pool_scale, sc_conv, gmlp_ln_g, gmlp_ln_b, gmlp_ws, gmlp_bs, w_branch, w_out, norm_mlp, w_up, w_down, norm_ple, w_ple, w_ple_gate, norm_final, loss_target, m_norm_mix, m_w_in, m_conf_dw, m_conf_dw_b, m_conf_ln_g, m_conf_ln_b, m_pool_w, m_pool_scale, m_sc_conv, m_gmlp_ln_g, m_gmlp_ln_b, m_gmlp_ws, m_gmlp_bs, m_w_branch, m_w_out, m_norm_mlp, m_w_up, m_w_down, m_norm_ple, m_w_ple, m_w_ple_gate, m_norm_final, v_norm_mix, v_w_in, v_conf_dw, v_conf_dw_b, v_conf_ln_g, v_conf_ln_b, v_pool_w, v_pool_scale, v_sc_conv, v_gmlp_ln_g, v_gmlp_ln_b, v_gmlp_ws, v_gmlp_bs, v_w_branch, v_w_out, v_norm_mlp, v_w_up, v_w_down, v_norm_ple, v_w_ple, v_w_ple_gate, v_norm_final):
    given = dict(locals())
    weights = {n: given[n] for n in WEIGHT_ORDER}
    mom_m = {n: given["m_" + n] for n in WEIGHT_ORDER}
    mom_v = {n: given["v_" + n] for n in WEIGHT_ORDER}
    my_shard = 2 * lax.axis_index("x") + lax.axis_index("y")

    n_layers = p.shape[0]
    shard_shapes = [weights[n].shape for n in SMALL_SHARDED]
    gathered = _allgather_small("allgather_small_weights", _pack([weights[n] for n in SMALL_SHARDED]))
    after_small = gathered[0, 0] * 0.0
    after_small = jnp.where(after_small == 0.0, after_small, 0.0)

    weight_copies = [[None] * len(WEIGHT_GROUPS) for _ in range(n_layers)]
    started = after_small
    for l in range(n_layers):
        for g, names in enumerate(WEIGHT_GROUPS):
            shards = [(weights[n][l] + started if i == 0 else weights[n][l]).astype(BF16) for i, n in enumerate(names)]
            weight_copies[l][g] = _weights_start(f"{l}_{g}", shards)
            started = weight_copies[l][g][4][0, 0]
    per_dev = gathered.reshape(8, -1, PACK_LANES)
    full_small_sharded = {}
    chip_parts = [_unpack(per_dev[2 * s], shard_shapes) for s in range(4)]
    for i, n in enumerate(SMALL_SHARDED):
        full_small_sharded[n] = jnp.concatenate([chip_parts[s][i] for s in range(4)], axis=-1)

    grad_copies = [[None] * len(GRAD_GROUPS) for _ in range(n_layers)]

    def layer_big(l):
        return lambda g, after: _weights_finish(f"{l}_{g}", WEIGHT_GROUPS[g], weight_copies[l][g], after)

    def on_big_grads(l):
        def send(g, group_grads):
            grad_copies[l][g] = _grads_start(f"{l}_{g}", GRAD_GROUPS[g], group_grads)
            return grad_copies[l][g][0][4][0, 0]
        return send

    loss_row, dx, small_grads = _local_step(x[0] + started, p[:, 0], loss_target[0], weights, layer_big, full_small_sharded,
                                            on_big_grads)
    loss = lax.psum(loss_row[0, 0], MESH_AXES)

    grads, delta, new_m, new_v = {}, {}, {}, {}
    halves = [{} for _ in range(n_layers)]
    last_sent = (0, len(GRAD_GROUPS) - 1)
    for l in reversed(range(n_layers)):
        for g, names in enumerate(GRAD_GROUPS):
            if (l, g) != last_sent:
                halves[l].update(_grads_finish(f"{l}_{g}", names, grad_copies[l][g], dx))
    behind = dx
    for g, names in enumerate(GRAD_GROUPS):
        if g == last_sent[1]:
            halves[0].update(_grads_finish(f"0_{g}", names, grad_copies[0][g], behind))
        for n, joined in zip(names, _join_halves(halves, names)):
            grads[n] = joined
            delta[n], new_m[n], new_v[n] = _adamw("adamw_" + n, weights[n], joined, mom_m[n], mom_v[n])
            behind = delta[n]

    small_names = SMALL_REP + SMALL_SHARDED
    after_big = behind[(0,) * behind.ndim] * 0.0
    after_big = jnp.where(after_big == 0.0, after_big, 0.0)
    packed = _pack([small_grads[n] for n in small_names]) + after_big
    rows = packed.shape[0]
    summed = _sum8("sum_small_grads", _allgather_small("allgather_small_grads", packed.astype(BF16)), rows)
    small_full_shapes = [weights[n].shape for n in SMALL_REP] + [full_small_sharded[n].shape for n in SMALL_SHARDED]
    for n, g in zip(small_names, _unpack(summed, small_full_shapes)):
        if n in SMALL_SHARDED:
            width = weights[n].shape[-1]
            g = lax.dynamic_slice_in_dim(g, my_shard * width, width, axis=g.ndim - 1)
        grads[n] = g

    for group, names in (("rep", SMALL_REP), ("sharded", SMALL_SHARDED)):
        outs = _adamw("adamw_small_" + group, *[_pack([src[n] for n in names]) for src in (weights, grads, mom_m, mom_v)])
        shapes = [weights[n].shape for n in names]
        for dst, buf in zip((delta, new_m, new_v), outs):
            for n, a in zip(names, _unpack(buf, shapes)):
                dst[n] = a

    return (loss, dx[None], *[grads[n] for n in WEIGHT_ORDER], *[delta[n] for n in WEIGHT_ORDER],
            *[new_m[n] for n in WEIGHT_ORDER], *[new_v[n] for n in WEIGHT_ORDER])
```

```python
import jax
import jax.numpy as jnp
from jax import lax
from jax.experimental import pallas as pl
from jax.experimental.pallas import tpu as pltpu

F32, BF16 = jnp.float32, jnp.bfloat16
EPS = 1e-6
ADAM_LR, ADAM_B1, ADAM_B2, ADAM_EPS, ADAM_WD, ADAM_STEP = 0.001, 0.9, 0.999, 1e-08, 0.01, 10
CONF_KERNEL, SC_KERNEL = 31, 3
POOL_WINDOWS = (2, 4, 8, 16)
MIX = 512
GROUP = 128
HALO = 32
SEQ_TILE = 256
BRANCH_ROWS = 512
VMEM_LIMIT_BYTES = 48 * 1024 * 1024
MESH_AXES = ("x", "y", "c")
ANY = pl.BlockSpec(memory_space=pl.ANY)

BIG = (("w_in", 2, 1), ("w_branch", 3, 1), ("w_out", 1, 2), ("w_up", 2, 1), ("w_down", 1, 2),
       ("w_ple", 2, 1), ("w_ple_gate", 1, 2))
SMALL_REP = ("norm_mix", "conf_dw_b", "conf_ln_g", "conf_ln_b", "pool_w", "pool_scale", "gmlp_ln_g",
             "gmlp_ln_b", "gmlp_ws", "gmlp_bs", "norm_mlp", "norm_ple", "norm_final")
SMALL_SHARDED = ("conf_dw", "sc_conv")
BY_CHIP = ("w_in", "w_up")
BIG_AXES = {name: (shard_axis, half_axis) for name, shard_axis, half_axis in BIG}
WEIGHT_GROUPS = (("w_in",), ("w_branch", "w_out", "w_up", "w_down", "w_ple_gate", "w_ple"))
GRAD_GROUPS = (("w_ple", "w_ple_gate", "w_down", "w_up"), ("w_out", "w_branch", "w_in"))
PACK_LANES = 128
PACK_ROW_MULTIPLE = 256


def _params(*semantics):
    return pltpu.CompilerParams(dimension_semantics=semantics, vmem_limit_bytes=VMEM_LIMIT_BYTES)


def _accumulate(ref, value, first):
    @pl.when(first)
    def _():
        ref[...] = value.astype(ref.dtype)

    @pl.when(jnp.logical_not(first))
    def _():
        ref[...] += value.astype(ref.dtype)


def _rowwise(name, fn, rows, params, outs, accs=(), tm=256):
    n_rows = rows[0][0].shape[-2]
    tm = min(tm, n_rows)
    n_in, n_out = len(rows) + len(params), len(outs)
    targets = [(k, o[2]) for k, o in enumerate(outs) if len(o) == 4]

    def body(*refs):
        res = fn(*[r[...] for r in refs[:n_in]])
        res = res if isinstance(res, (tuple, list)) else (res,)
        out_refs = refs[n_in + len(targets):]
        for r, v in zip(out_refs[:n_out], res[:n_out]):
            r[...] = v.astype(r.dtype)
        first = pl.program_id(0) == 0
        for r, v in zip(out_refs[n_out:], res[n_out:]):
            _accumulate(r, v, first)

    in_specs = []
    for row in rows:
        arr, w, cb = row[:3]
        if len(row) == 4:
            in_specs.append(pl.BlockSpec((None, tm, w), lambda i, cb=cb, g=row[3]: (g, i, cb)))
        else:
            in_specs.append(pl.BlockSpec((tm, w), lambda i, cb=cb: (i, cb)))
    for p in params:
        in_specs.append(pl.BlockSpec(p.shape, lambda i, nd=p.ndim: (0,) * nd))
    in_specs += [ANY] * len(targets)
    out_specs = [pl.BlockSpec((tm, o[0]), lambda i, cb=(o[3] if len(o) == 4 else 0): (i, cb)) for o in outs]
    out_specs += [pl.BlockSpec(s, lambda i, nd=len(s): (0,) * nd) for s in accs]
    out_shape = [jax.ShapeDtypeStruct(o[2].shape if len(o) == 4 else (n_rows, o[0]), o[1]) for o in outs]
    out_shape += [jax.ShapeDtypeStruct(s, F32) for s in accs]
    return pl.pallas_call(
        body, name=name, grid=(n_rows // tm,), in_specs=in_specs, out_specs=out_specs, out_shape=out_shape,
        input_output_aliases={n_in + t: k for t, (k, _) in enumerate(targets)},
        compiler_params=_params("arbitrary"),
    )(*[r[0] for r in rows], *params, *[t for _, t in targets])


ROW_OUT_SUBLANES = 8


def _mm(name, a, b, *, ta=False, tb=False, a_cols=0, b_by_chip=False, chips=(0, 4), res=(), rows=(), epi=None, out_dtypes=(F32,), n_row_outs=0,
        tm=1024, tn=1024, tk=1024):
    if a_cols:
        assert ta
        g_, m_, k_ = a.shape[1] // a_cols, a_cols, a.shape[0]
    else:
        g_, m_, k_ = (a.shape[0], a.shape[2], a.shape[1]) if ta else a.shape
    if b_by_chip:
        n_ = b.shape[2] if tb else chips[1] * b.shape[3]
    else:
        n_ = b.shape[1] if tb else b.shape[2]
    if ta:
        tk = 2 * tk
    tm, tn, tk = min(tm, m_), min(tn, n_), min(tk, k_)
    if b_by_chip:
        tn, tk = (tn, min(tk, b.shape[3])) if tb else (min(tn, b.shape[3]), tk)
        per_chip = b.shape[3] // (tk if tb else tn)
    nk, n_res, n_rows, n_out = k_ // tk, len(res), len(rows), len(out_dtypes)
    n_in = 2 + n_res + n_rows
    dims = (((0 if ta else 1,), (1 if tb else 0,)), ((), ()))
    if epi is None:
        epi = lambda acc: (acc,)

    def body(*refs):
        a_ref, b_ref = refs[:2]
        res_refs, out_refs, acc_ref = refs[2:n_in], refs[n_in:n_in + n_out], refs[-1]
        row_out_refs = refs[n_in + n_out:-1]
        k = pl.program_id(3)

        def product():
            return lax.dot_general(a_ref[...].astype(BF16), b_ref[...].astype(BF16), dims, preferred_element_type=F32)

        def finish(acc):
            vals = epi(acc, *[r[...] for r in res_refs])
            for o, v in zip(out_refs, vals):
                o[...] = v.astype(o.dtype)
            for o, v in zip(row_out_refs, vals[n_out:]):
                o[...] = jnp.broadcast_to(v, o.shape)

        if nk == 1:
            finish(product())
            return

        @pl.when(k == 0)
        def _():
            acc_ref[...] = jnp.zeros_like(acc_ref)

        acc_ref[...] += product()

        @pl.when(k == nk - 1)
        def _():
            finish(acc_ref[...])

    if a_cols:
        a_spec = pl.BlockSpec((tk, tm), lambda g, i, j, k: (k, g * (m_ // tm) + i))
    elif ta:
        a_spec = pl.BlockSpec((None, tk, tm), lambda g, i, j, k: (g, k, i))
    else:
        a_spec = pl.BlockSpec((None, tm, tk), lambda g, i, j, k: (g, i, k))
    if b_by_chip and tb:
        b_spec = pl.BlockSpec((None, None, tn, tk), lambda g, i, j, k: (g, k // per_chip, j, k % per_chip))
    elif b_by_chip:
        b_spec = pl.BlockSpec((None, None, tk, tn), lambda g, i, j, k: (g, chips[0] + j // per_chip, k, j % per_chip))
    elif tb:
        b_spec = pl.BlockSpec((None, tn, tk), lambda g, i, j, k: (g, j, k))
    else:
        b_spec = pl.BlockSpec((None, tk, tn), lambda g, i, j, k: (g, k, j))
    res_specs = [pl.BlockSpec((None, tm, tn), lambda g, i, j, k, off=off: (g, i, j + off)) for _, off in res]
    row_specs = [pl.BlockSpec((None, 1, tn), lambda g, i, j, k: (g, 0, j))] * n_rows
    out_spec = pl.BlockSpec((None, tm, tn), lambda g, i, j, k: (g, i, j))
    row_out_spec = pl.BlockSpec((None, ROW_OUT_SUBLANES, tn), lambda g, i, j, k: (g, i, j))
    outs = pl.pallas_call(
        body, name=name, grid=(g_, m_ // tm, n_ // tn, nk),
        in_specs=[a_spec, b_spec, *res_specs, *row_specs], out_specs=[out_spec] * n_out + [row_out_spec] * n_row_outs,
        out_shape=[jax.ShapeDtypeStruct((g_, m_, n_), d) for d in out_dtypes]
        + [jax.ShapeDtypeStruct((g_, ROW_OUT_SUBLANES * (m_ // tm), n_), F32)] * n_row_outs,
        scratch_shapes=[pltpu.VMEM((tm, tn), F32)],
        compiler_params=_params("parallel", "parallel", "parallel", "arbitrary"),
    )(a, b, *[r for r, _ in res], *rows)
    return outs


def _mm2(name, a, b, *, res=(), rows=(), **kw):
    outs = _mm(name, a[None], b[None], res=[(r[None], off) for r, off in res], rows=[r[None] for r in rows], **kw)
    return [o[0] for o in outs]


def _sum_row_out(row_out):
    return jnp.sum(row_out[::ROW_OUT_SUBLANES], axis=0, keepdims=True)


@jax.custom_vjp
def _bdot(a, b):
    return jnp.dot(a.astype(BF16), b.astype(BF16), preferred_element_type=F32)


def _bdot_fwd(a, b):
    return _bdot(a, b), (a, b)


def _bdot_bwd(saved, g):
    a, b = saved
    gb = g.astype(BF16)
    da = lax.dot_general(gb, b.astype(BF16), (((1,), (1,)), ((), ())), preferred_element_type=F32)
    db = lax.dot_general(a.astype(BF16), gb, (((0,), (0,)), ((), ())), preferred_element_type=F32)
    return da.astype(a.dtype), db.astype(b.dtype)


_bdot.defvjp(_bdot_fwd, _bdot_bwd)


def _rms(x, g):
    return x * lax.rsqrt(jnp.mean(x * x, axis=-1, keepdims=True) + EPS) * g


def _layer_norm(x, g, b):
    mu = jnp.mean(x, axis=-1, keepdims=True)
    var = jnp.mean(jnp.square(x - mu), axis=-1, keepdims=True)
    return (x - mu) * lax.rsqrt(var + EPS) * g + b


def _glu_and_gate_fn(a, a_gate, cg, hx):
    return a * jax.nn.sigmoid(a_gate), cg * hx


def _branches_fn(ca, pooled, cc, bg, u, v, conf_b, conf_g, conf_beta, pool_scale, gm_g, gm_b, *mats):
    pool_w, ws, bs = mats[0:4], mats[4:8], mats[8:12]
    ya = jax.nn.silu(_layer_norm(ca + conf_b, conf_g, conf_beta))
    yb = jnp.concatenate([_bdot(pooled[:, g * GROUP:(g + 1) * GROUP], pool_w[g]) for g in range(4)], axis=1) * pool_scale
    yc = bg * cc
    vln = _layer_norm(v, gm_g, gm_b)
    causal = lax.broadcasted_iota(jnp.int32, (GROUP, GROUP), 0) >= lax.broadcasted_iota(jnp.int32, (GROUP, GROUP), 1)
    chunks = v.shape[0] // GROUP

    def spatial(g):
        vg = vln[:, g * GROUP:(g + 1) * GROUP]
        wide = jnp.concatenate([vg[c * GROUP:(c + 1) * GROUP, :] for c in range(chunks)], axis=1)
        out = _bdot(jnp.where(causal, ws[g], 0.0), wide) + bs[g]
        return jnp.concatenate([out[:, c * GROUP:(c + 1) * GROUP] for c in range(chunks)], axis=0)

    sg = jnp.concatenate([spatial(g) for g in range(4)], axis=1)
    yd = u * sg
    return ya, yb, yc, yd


def _branches_joined_fn(*args):
    return jnp.concatenate(_branches_fn(*args), axis=1)


def _branches_bwd_fn(*args):
    ins, dys = args[:6] + args[10:], args[6:10]
    _, vjp = jax.vjp(_branches_fn, *ins)
    grads = vjp(tuple(dys))
    duv = jnp.concatenate([grads[4], grads[5]], axis=1)
    return (*grads[:4], duv, *grads[6:])


def _glu_bwd_fn(a, a_gate, cg, hx, dga, dchx, dpin, dbg):
    _, vjp = jax.vjp(_glu_and_gate_fn, a, a_gate, cg, hx)
    da, dag, dcg, dhx = vjp((dga, dchx))
    return jnp.concatenate([da, dag, dpin.astype(F32), dbg.astype(F32), dcg, dhx], axis=1)


def _rms_bwd_epilogue(dh, x, dres, g):
    _, vjp = jax.vjp(_rms, x, g)
    dx, dg = vjp(dh)
    return dx + dres, dg


def _residual_then_rms(acc, residual, g):
    x = acc + residual
    return x, _rms(x, g)


def _loss_fn(x, target, g):
    def f(x, g):
        err = jnp.square(_rms(x, g) - target)
        return 0.5 * jnp.sum(jnp.mean(err, axis=-1, keepdims=True), axis=0, keepdims=True)

    loss, vjp = jax.vjp(f, x, g)
    dx, dg = vjp(jnp.ones((1, 1), F32))
    return dx, dg, jnp.broadcast_to(loss, (1, PACK_LANES))


def _adamw_fn(w, g, m, v):
    m = ADAM_B1 * m + (1.0 - ADAM_B1) * g
    v = ADAM_B2 * v + (1.0 - ADAM_B2) * jnp.square(g)
    m_hat = m / (1.0 - ADAM_B1 ** ADAM_STEP)
    v_hat = v / (1.0 - ADAM_B2 ** ADAM_STEP)
    delta = -ADAM_LR * (m_hat / (jnp.sqrt(v_hat) + ADAM_EPS) + ADAM_WD * w)
    return delta, m, v


def _group_pick(j):
    return [(j == g).astype(F32) for g in range(len(POOL_WINDOWS))]


SUBLANES = 8


def _causal_taps(x, taps):
    for r in range(min(SUBLANES, taps)):
        behind = x if r == 0 else pltpu.roll(x, r, 0)
        for q in range((taps - 1 - r) // SUBLANES + 1):
            start = HALO - SUBLANES * q
            yield SUBLANES * q + r, behind[start:start + SEQ_TILE, :]


def _rows_iota(n):
    return lax.broadcasted_iota(jnp.int32, (n, 1), 0)


def _seq_specs(n_time, causal):
    per = SEQ_TILE // HALO
    tile = pl.BlockSpec((SEQ_TILE, GROUP), lambda j, i: (i, j))
    if causal:
        halo = pl.BlockSpec((HALO, GROUP), lambda j, i: (jnp.maximum(i * per - 1, 0), j))
    else:
        halo = pl.BlockSpec((HALO, GROUP), lambda j, i: (jnp.minimum((i + 1) * per, n_time * per - 1), j))
    return tile, halo


def _seq_fwd(ga, proj, chx, conf_dw, sc_conv):
    s_len = ga.shape[0]
    n_time = s_len // SEQ_TILE
    ext = HALO + SEQ_TILE

    def body(ga_ref, ga_h, pin_ref, pin_h, chx_ref, chx_h, wc_ref, ws_ref, ca_ref, po_ref, cc_ref):
        j, i = pl.program_id(0), pl.program_id(1)
        keep = (i > 0).astype(F32)

        def extended(cur, halo):
            return jnp.concatenate([halo[...] * keep, cur[...]], axis=0)

        def conv(x, w_ref, taps):
            acc = None
            for d, rows in _causal_taps(x, taps):
                term = rows * w_ref[pl.ds(taps - 1 - d, 1), :]
                acc = term if acc is None else acc + term
            return acc

        ca_ref[...] = conv(extended(ga_ref, ga_h), wc_ref, CONF_KERNEL)
        cc_ref[...] = conv(extended(chx_ref, chx_h), ws_ref, SC_KERNEL)
        u = extended(pin_ref, pin_h)
        s1 = u + pltpu.roll(u, 1, 0)
        s2 = s1 + pltpu.roll(s1, 2, 0)
        s3 = s2 + pltpu.roll(s2, 4, 0)
        s4 = s3 + pltpu.roll(s3, 8, 0)
        pick = _group_pick(j)
        window_sum = s1 * pick[0] + s2 * pick[1] + s3 * pick[2] + s4 * pick[3]
        window = sum(float(w) * f for w, f in zip(POOL_WINDOWS, pick))
        pos = (_rows_iota(ext) + (i * SEQ_TILE - HALO + 1)).astype(F32)
        count = jnp.maximum(jnp.minimum(pos, window), 1.0)
        po_ref[...] = (window_sum / count - u)[HALO:, :]

    tile, halo = _seq_specs(n_time, causal=True)
    pin_tile = pl.BlockSpec((SEQ_TILE, GROUP), lambda j, i: (i, 2 * (MIX // GROUP) + j))
    pin_halo = pl.BlockSpec((HALO, GROUP), lambda j, i: (jnp.maximum(i * (SEQ_TILE // HALO) - 1, 0), 2 * (MIX // GROUP) + j))
    w_spec = lambda taps: pl.BlockSpec((taps, GROUP), lambda j, i: (0, j))
    return pl.pallas_call(
        body, name="seq_fwd", grid=(MIX // GROUP, n_time),
        in_specs=[tile, halo, pin_tile, pin_halo, tile, halo, w_spec(CONF_KERNEL), w_spec(SC_KERNEL)],
        out_specs=[tile] * 3, out_shape=[jax.ShapeDtypeStruct((s_len, MIX), F32)] * 3,
        compiler_params=_params("parallel", "arbitrary"),
    )(ga, ga, proj, proj, chx, chx, conf_dw, sc_conv)


def _seq_bwd(dca, dpooled, dcc, ga, chx, conf_dw, sc_conv):
    s_len = ga.shape[0]
    n_time = s_len // SEQ_TILE
    ext = HALO + SEQ_TILE

    def body(dca_ref, dca_h, dpo_ref, dpo_h, dcc_ref, dcc_h, ga_ref, ga_h, chx_ref, chx_h, wc_ref, ws_ref,
             dga_ref, dpin_ref, dchx_ref, dwc_ref, dws_ref, acc_c, acc_s):
        j, i = pl.program_id(0), pl.program_id(1)
        keep_prev = (i > 0).astype(F32)
        keep_next = (i < n_time - 1).astype(F32)

        def with_next(cur, halo):
            return jnp.concatenate([cur[...], halo[...] * keep_next], axis=0)

        def with_prev(cur, halo):
            return jnp.concatenate([halo[...] * keep_prev, cur[...]], axis=0)

        def conv_t(dy, w_ref, taps):
            acc = None
            for r in range(min(SUBLANES, taps)):
                ahead = dy if r == 0 else pltpu.roll(dy, ext - r, 0)
                for q in range((taps - 1 - r) // SUBLANES + 1):
                    term = ahead[SUBLANES * q:SUBLANES * q + SEQ_TILE, :] * w_ref[pl.ds(taps - 1 - (SUBLANES * q + r), 1), :]
                    acc = term if acc is None else acc + term
            return acc

        def tap_grads(dy_cur, x, acc_ref, dw_ref, taps):
            @pl.when(i == 0)
            def _():
                acc_ref[...] = jnp.zeros_like(acc_ref)

            for d, rows in _causal_taps(x, taps):
                prod = dy_cur * rows
                part = prod[:SUBLANES, :]
                for g in range(1, SEQ_TILE // SUBLANES):
                    part = part + prod[g * SUBLANES:(g + 1) * SUBLANES, :]
                acc_ref[pl.ds((taps - 1 - d) * SUBLANES, SUBLANES), :] += part

            @pl.when(i == n_time - 1)
            def _():
                for k in range(taps):
                    dw_ref[pl.ds(k, 1), :] = jnp.sum(acc_ref[pl.ds(k * SUBLANES, SUBLANES), :], axis=0, keepdims=True)

        dga_ref[...] = conv_t(with_next(dca_ref, dca_h), wc_ref, CONF_KERNEL)
        dchx_ref[...] = conv_t(with_next(dcc_ref, dcc_h), ws_ref, SC_KERNEL)
        tap_grads(dca_ref[...], with_prev(ga_ref, ga_h), acc_c, dwc_ref, CONF_KERNEL)
        tap_grads(dcc_ref[...], with_prev(chx_ref, chx_h), acc_s, dws_ref, SC_KERNEL)

        dpo = with_next(dpo_ref, dpo_h)
        pick = _group_pick(j)
        window = sum(float(w) * f for w, f in zip(POOL_WINDOWS, pick))
        pos = (_rows_iota(ext) + (i * SEQ_TILE + 1)).astype(F32)
        r = dpo / jnp.minimum(pos, window)
        q1 = r + pltpu.roll(r, ext - 1, 0)
        q2 = q1 + pltpu.roll(q1, ext - 2, 0)
        q3 = q2 + pltpu.roll(q2, ext - 4, 0)
        q4 = q3 + pltpu.roll(q3, ext - 8, 0)
        q = q1 * pick[0] + q2 * pick[1] + q3 * pick[2] + q4 * pick[3]
        dpin_ref[...] = (q - dpo)[:SEQ_TILE, :].astype(dpin_ref.dtype)

    tile, nxt = _seq_specs(n_time, causal=False)
    _, prv = _seq_specs(n_time, causal=True)
    w_spec = lambda taps: pl.BlockSpec((taps, GROUP), lambda j, i: (0, j))
    return pl.pallas_call(
        body, name="seq_bwd", grid=(MIX // GROUP, n_time),
        in_specs=[tile, nxt, tile, nxt, tile, nxt, tile, prv, tile, prv, w_spec(CONF_KERNEL), w_spec(SC_KERNEL)],
        out_specs=[tile, tile, tile, w_spec(CONF_KERNEL), w_spec(SC_KERNEL)],
        out_shape=[jax.ShapeDtypeStruct((s_len, MIX), F32), jax.ShapeDtypeStruct((s_len, MIX), BF16),
                   jax.ShapeDtypeStruct((s_len, MIX), F32), jax.ShapeDtypeStruct(conf_dw.shape, F32),
                   jax.ShapeDtypeStruct(sc_conv.shape, F32)],
        scratch_shapes=[pltpu.VMEM((CONF_KERNEL * SUBLANES, GROUP), F32), pltpu.VMEM((SC_KERNEL * SUBLANES, GROUP), F32)],
        compiler_params=_params("parallel", "arbitrary"),
    )(dca, dca, dpooled, dpooled, dcc, dcc, ga, ga, chx, chx, conf_dw, sc_conv)


MERGE_TILE = 512
MERGE_ROWS = 1024


def _merge_fwd(y, w_branch, gates):
    s_len, d_model = y.shape[0], w_branch.shape[2]
    t = min(MERGE_ROWS, s_len)
    per_branch = d_model // MERGE_TILE

    def body(y_ref, w_ref, gate_ref, out_ref, acc_ref):
        k = pl.program_id(2)

        @pl.when(k == 0)
        def _():
            acc_ref[...] = jnp.zeros_like(acc_ref)

        z = jnp.dot(y_ref[...], w_ref[...], preferred_element_type=F32)
        acc_ref[...] += gate_ref[...].astype(F32) * z

        @pl.when(k == 3)
        def _():
            out_ref[...] = acc_ref[...].astype(out_ref.dtype)

    return pl.pallas_call(
        body, name="merge_fwd", grid=(s_len // t, per_branch, 4),
        in_specs=[pl.BlockSpec((t, MIX), lambda i, j, k: (i, k)),
                  pl.BlockSpec((None, MIX, MERGE_TILE), lambda i, j, k: (k, 0, j)),
                  pl.BlockSpec((t, MERGE_TILE), lambda i, j, k: (i, k * per_branch + j))],
        out_specs=pl.BlockSpec((t, MERGE_TILE), lambda i, j, k: (i, j)),
        out_shape=jax.ShapeDtypeStruct((s_len, d_model), BF16),
        scratch_shapes=[pltpu.VMEM((t, MERGE_TILE), F32)],
        compiler_params=_params("parallel", "parallel", "arbitrary"),
    )(y, w_branch, gates)


def _merge_bwd(y, w_branch, gates, dmerged):
    s_len, d_model = y.shape[0], w_branch.shape[2]
    t = min(MERGE_ROWS, s_len)
    per_branch = d_model // MERGE_TILE

    def body(y_ref, w_ref, gate_ref, dm_ref, dz_ref, dgate_ref):
        z = jnp.dot(y_ref[...], w_ref[...], preferred_element_type=F32)
        gate = gate_ref[...].astype(F32)
        dm = dm_ref[...]
        dz_ref[...] = (dm * gate).astype(dz_ref.dtype)
        dgate_ref[...] = (dm * z * gate * (1.0 - gate)).astype(dgate_ref.dtype)

    return pl.pallas_call(
        body, name="merge_bwd", grid=(s_len // t, per_branch, 4),
        in_specs=[pl.BlockSpec((t, MIX), lambda i, j, k: (i, k)),
                  pl.BlockSpec((None, MIX, MERGE_TILE), lambda i, j, k: (k, 0, j)),
                  pl.BlockSpec((t, MERGE_TILE), lambda i, j, k: (i, k * per_branch + j)),
                  pl.BlockSpec((t, MERGE_TILE), lambda i, j, k: (i, j))],
        out_specs=[pl.BlockSpec((None, t, MERGE_TILE), lambda i, j, k: (k, i, j)),
                   pl.BlockSpec((t, MERGE_TILE), lambda i, j, k: (i, (4 + k) * per_branch + j))],
        out_shape=[jax.ShapeDtypeStruct((4, s_len, d_model), BF16), jax.ShapeDtypeStruct((s_len, 8 * d_model), BF16)],
        compiler_params=_params("parallel", "parallel", "arbitrary"),
    )(y, w_branch, gates, dmerged)


def _row(v):
    return v.reshape(1, -1)


def _branch_params(w):
    mats = [w["pool_w"][g] for g in range(4)] + [w["gmlp_ws"][g] for g in range(4)]
    mats += [w["gmlp_bs"][g].reshape(GROUP, 1) for g in range(4)]
    return [_row(w["conf_dw_b"]), _row(w["conf_ln_g"]), _row(w["conf_ln_b"]), _row(w["pool_scale"]),
            _row(w["gmlp_ln_g"]), _row(w["gmlp_ln_b"])] + mats


def _layer_fwd(x0, p_l, w, fetch):
    d_model = x0.shape[1]
    (h1,) = _rowwise("rms_mix", _rms, [(x0, d_model, 0)], [_row(w["norm_mix"])], [(d_model, BF16)])
    w.update(fetch(0, h1))
    (proj,) = _mm2("proj", h1, w["w_in"], b_by_chip=True, chips=(0, 2))
    w.update(fetch(1, proj))
    (gates,) = _mm2("proj_gates", h1, w["w_in"], b_by_chip=True, chips=(2, 2), out_dtypes=(BF16,),
                    epi=lambda acc: (jax.nn.sigmoid(acc),))
    ga, chx = _rowwise("glu", _glu_and_gate_fn, [(proj, MIX, 0), (proj, MIX, 1), (proj, MIX, 4), (proj, MIX, 5)], [],
                       [(MIX, F32), (MIX, F32)])
    ca, pooled, cc = _seq_fwd(ga, proj, chx, w["conf_dw"], w["sc_conv"])
    (y,) = _rowwise("branches", _branches_joined_fn,
                    [(ca, MIX, 0), (pooled, MIX, 0), (cc, MIX, 0), (proj, MIX, 3), (proj, MIX, 6), (proj, MIX, 7)],
                    _branch_params(w), [(4 * MIX, BF16)], tm=BRANCH_ROWS)
    merged = _merge_fwd(y, w["w_branch"], gates)
    x1, h2 = _mm2("out_proj", merged, w["w_out"], res=[(x0, 0)], rows=[_row(w["norm_mlp"])], out_dtypes=(F32, BF16),
                  epi=_residual_then_rms)
    (act,) = _mm2("mlp_up", h2, w["w_up"], b_by_chip=True, out_dtypes=(BF16,),
                  epi=lambda acc: (jnp.square(jnp.maximum(acc, 0.0)),))
    x2, h3 = _mm2("mlp_down", act, w["w_down"], res=[(x1, 0)], rows=[_row(w["norm_ple"])], out_dtypes=(F32, BF16),
                  epi=_residual_then_rms)
    (pg,) = _mm2("ple_gate", h3, w["w_ple_gate"])
    (x3,) = _mm2("ple", p_l, w["w_ple"], res=[(pg, 0), (x2, 0)],
                 epi=lambda acc, g, r: (acc * jax.nn.sigmoid(g) + r,))
    saved = dict(x0=x0, h1=h1, proj=proj, gates=gates, ga=ga, chx=chx, ca=ca, pooled=pooled, cc=cc, y=y, merged=merged, x1=x1,
                 h2=h2, act=act, x2=x2, h3=h3, pg=pg)
    return x3, saved


def _layer_bwd(dx3, p_l, w, s, send):
    d_model = dx3.shape[1]
    big, small = {}, {}

    def ple_epi(acc, dx, g):
        sig = jax.nn.sigmoid(g)
        return dx * sig, dx * acc * sig * (1.0 - sig)

    dpp, dpg = _mm2("ple_bwd", p_l, w["w_ple"], res=[(dx3, 0), (s["pg"], 0)], epi=ple_epi, out_dtypes=(BF16, BF16))
    (big["w_ple"],) = _mm2("dw_ple", p_l, dpp, ta=True, out_dtypes=(BF16,))
    (big["w_ple_gate"],) = _mm2("dw_ple_gate", s["h3"], dpg, ta=True, out_dtypes=(BF16,))
    dx2, dg = _mm2("dh_ple", dpg, w["w_ple_gate"], tb=True, res=[(s["x2"], 0), (dx3, 0)], rows=[_row(w["norm_ple"])],
                   epi=_rms_bwd_epilogue, n_row_outs=1)
    small["norm_ple"] = _sum_row_out(dg)

    (dup,) = _mm2("mlp_down_bwd", dx2, w["w_down"], tb=True, res=[(s["act"], 0)], out_dtypes=(BF16,),
                  epi=lambda acc, act: (acc * 2.0 * jnp.sqrt(act.astype(F32)),))
    (big["w_down"],) = _mm2("dw_down", s["act"], dx2, ta=True, out_dtypes=(BF16,))
    (big["w_up"],) = _mm2("dw_up", s["h2"], dup, ta=True, out_dtypes=(BF16,))
    sent = send(0, [big[n] for n in GRAD_GROUPS[0]])
    dx1, dg = _mm2("mlp_up_bwd", dup, w["w_up"], tb=True, b_by_chip=True, res=[(s["x1"], 0), (dx2, 0)],
                   rows=[_row(w["norm_mlp"]) + sent], epi=_rms_bwd_epilogue, n_row_outs=1)
    small["norm_mlp"] = _sum_row_out(dg)

    (dmerged,) = _mm2("out_proj_bwd", dx1, w["w_out"], tb=True)
    (big["w_out"],) = _mm2("dw_out", s["merged"], dx1, ta=True, out_dtypes=(BF16,))
    dz, dproj = _merge_bwd(s["y"], w["w_branch"], s["gates"], dmerged)
    (dy,) = _mm("branch_bwd", dz, w["w_branch"], tb=True)
    (big["w_branch"],) = _mm("dw_branch", s["y"], dz, ta=True, a_cols=MIX, out_dtypes=(BF16,))

    proj = s["proj"]
    params = _branch_params(w)
    grads = _rowwise(
        "branches_bwd", _branches_bwd_fn,
        [(s["ca"], MIX, 0), (s["pooled"], MIX, 0), (s["cc"], MIX, 0), (proj, MIX, 3), (proj, MIX, 6), (proj, MIX, 7)]
        + [(dy, MIX, 0, g) for g in range(4)], params,
        [(MIX, F32), (MIX, F32), (MIX, F32), (MIX, BF16), (2 * MIX, BF16, dproj, 3)], [p.shape for p in params], tm=BRANCH_ROWS)
    dca, dpooled, dcc, dbg, dproj = grads[:5]
    pg_ = grads[5:]
    small["conf_dw_b"], small["conf_ln_g"], small["conf_ln_b"], small["pool_scale"] = pg_[0], pg_[1], pg_[2], pg_[3]
    small["gmlp_ln_g"], small["gmlp_ln_b"] = pg_[4], pg_[5]
    small["pool_w"] = jnp.stack(pg_[6:10])
    small["gmlp_ws"] = jnp.stack(pg_[10:14])
    small["gmlp_bs"] = jnp.stack([b.reshape(GROUP) for b in pg_[14:18]])

    dga, dpin, dchx, small["conf_dw"], small["sc_conv"] = _seq_bwd(dca, dpooled, dcc, s["ga"], s["chx"], w["conf_dw"], w["sc_conv"])
    (dproj,) = _rowwise("glu_bwd", _glu_bwd_fn,
                        [(proj, MIX, 0), (proj, MIX, 1), (proj, MIX, 4), (proj, MIX, 5), (dga, MIX, 0), (dchx, MIX, 0),
                         (dpin, MIX, 0), (dbg, MIX, 0)], [], [(6 * MIX, BF16, dproj, 0)])
    (big["w_in"],) = _mm2("dw_in", s["h1"], dproj, ta=True, out_dtypes=(BF16,))
    sent = send(1, [big[n] for n in GRAD_GROUPS[1]])
    dx0, dg = _mm2("proj_bwd", dproj, w["w_in"], tb=True, b_by_chip=True, res=[(s["x0"], 0), (dx1, 0)],
                   rows=[_row(w["norm_mix"]) + sent], epi=_rms_bwd_epilogue, n_row_outs=1)
    small["norm_mix"] = _sum_row_out(dg)
    return dx0, small


def _position():
    return lax.axis_index("x"), lax.axis_index("y"), lax.axis_index("c")


def _other_chips(x, y):
    return [(1 - x, y), (x, 1 - y), (1 - x, 1 - y)]


def _region(ref, cuts):
    return ref.at[tuple(pl.ds(*cuts[a]) if a in cuts else slice(None) for a in range(len(ref.shape)))]


def _allgather_small(name, block):
    m_per, n = block.shape

    def body(x_ref, out_ref, send_sems, recv_sems, local_sem):
        x, y, c = _position()
        me, sibling = (x, y, c), (x, y, 1 - c)
        chips = _other_chips(x, y)

        def rows(px, py, pc):
            return out_ref.at[pl.ds((4 * px + 2 * py + pc) * m_per, m_per), :]

        def copy(k, blk, to, src=None):
            return pltpu.make_async_remote_copy(
                src_ref=rows(*blk) if src is None else src, dst_ref=rows(*blk), send_sem=send_sems.at[k],
                recv_sem=recv_sems.at[k], device_id=to, device_id_type=pl.DeviceIdType.MESH)

        mine = pltpu.make_async_copy(x_ref, rows(*me), local_sem)
        mine.start()
        first = [copy(0, me, sibling, src=x_ref)]
        first += [copy(1 + j, me, (*chip, c), src=x_ref) for j, chip in enumerate(chips)]
        for cp in first:
            cp.start()
        passed = [copy(4 + j, (*chip, c), sibling) for j, chip in enumerate(chips)]
        for j, chip in enumerate(chips):
            copy(1 + j, (*chip, c), me).wait_recv()
            passed[j].start()
        copy(0, sibling, me).wait_recv()
        for j, chip in enumerate(chips):
            copy(4 + j, (*chip, 1 - c), me).wait_recv()
        for cp in first + passed:
            cp.wait_send()
        mine.wait()

    return pl.pallas_call(
        body, name=name, out_shape=jax.ShapeDtypeStruct((8 * m_per, n), block.dtype), in_specs=[ANY], out_specs=ANY,
        scratch_shapes=[pltpu.SemaphoreType.DMA((7,)), pltpu.SemaphoreType.DMA((7,)), pltpu.SemaphoreType.DMA],
    )(block)


def _cuts(arr_shape, shard_axis, half_axis, shard=None, half=None):
    cuts = {}
    if shard is not None:
        n = arr_shape[shard_axis] // 4
        cuts[shard_axis] = (shard * n, n)
    if half is not None:
        n = arr_shape[half_axis] // 2
        cuts[half_axis] = (half * n, n)
    return cuts


def _full_shape(shard_shape, shard_axis):
    return tuple(4 * n if a == shard_axis else n for a, n in enumerate(shard_shape))


HBM = pl.BlockSpec(memory_space=pltpu.HBM)
SEM = pl.BlockSpec(memory_space=pltpu.SEMAPHORE)
DATAFLOW_EFFECT = pltpu.SideEffectType.DATAFLOW_SIDE_EFFECTING
TOKEN_SHAPE = (8, 128)


def _remote(src, dst, send_sems, recv_sems, k, peer):
    return pltpu.make_async_remote_copy(src_ref=src, dst_ref=dst, send_sem=send_sems.at[k], recv_sem=recv_sems.at[k],
                                        device_id=peer, device_id_type=pl.DeviceIdType.MESH)


def _split_start(name, sources, landings, plan):
    n_s, n_l = len(sources), len(landings)
    n_copies = len(plan([None] * n_s, [None] * n_l, None))

    def body(*refs):
        src, land = refs[:n_s], refs[n_s:n_s + n_l]
        send_sems, recv_sems, token = refs[n_s + n_l], refs[n_s + n_l + 1], refs[-1]
        for k, (s, d, peer) in enumerate(plan(src, land, _position())):
            _remote(s, d, send_sems, recv_sems, k, peer).start()
        token[...] = jnp.zeros_like(token)

    arrays = [pltpu.with_memory_space_constraint(a, pltpu.HBM) for a in (*sources, *landings)]
    outs = pl.pallas_call(
        body, name=name,
        out_shape=(pltpu.SemaphoreType.DMA((n_copies,)), pltpu.SemaphoreType.DMA((n_copies,)),
                   *[pltpu.HBM(a.shape, a.dtype) for a in arrays], jax.ShapeDtypeStruct(TOKEN_SHAPE, F32)),
        in_specs=[HBM] * (n_s + n_l),
        out_specs=(SEM, SEM, *[HBM] * (n_s + n_l), pl.BlockSpec(memory_space=pltpu.VMEM)),
        input_output_aliases={i: 2 + i for i in range(n_s + n_l)},
        compiler_params=pltpu.CompilerParams(has_side_effects=DATAFLOW_EFFECT),
    )(*arrays)
    return outs[0], outs[1], list(outs[2:2 + n_s]), list(outs[2 + n_s:2 + n_s + n_l]), outs[-1]


def _split_wait(name, started, after, plan):
    send_sems, recv_sems, sources, landings, _ = started
    n_s, n_l = len(sources), len(landings)

    def body(*refs):
        src, land = refs[:n_s], refs[n_s:n_s + n_l]
        send_sems_ref, recv_sems_ref = refs[n_s + n_l], refs[n_s + n_l + 1]
        for k, (s, d, peer) in enumerate(plan(src, land, _position())):
            cp = _remote(s, d, send_sems_ref, recv_sems_ref, k, peer)
            cp.wait_send()
            cp.wait_recv()

    outs = pl.pallas_call(
        body, name=name, out_shape=tuple(pltpu.HBM(a.shape, a.dtype) for a in (*sources, *landings)),
        in_specs=[*[HBM] * (n_s + n_l), SEM, SEM, ANY], out_specs=tuple([HBM] * (n_s + n_l)),
        input_output_aliases={i: i for i in range(n_s + n_l)},
        compiler_params=pltpu.CompilerParams(has_side_effects=DATAFLOW_EFFECT),
    )(*sources, *landings, send_sems, recv_sems, after)
    return list(outs[:n_s]), list(outs[n_s:])


def _exchange(name, inputs, out_shapes, plan, in_place=False):
    n_in = len(inputs)
    n_out = n_in if in_place else len(out_shapes)
    n_copies = len(plan([None] * n_in, [None] * n_out, None))

    def body(*refs):
        ins, outs = refs[:n_in], refs[n_in:n_in + n_out]
        send_sems, recv_sems = refs[n_in + n_out:]
        copies = [_remote(s, d, send_sems, recv_sems, k, peer)
                  for k, (s, d, peer) in enumerate(plan(ins, ins if in_place else outs, _position()))]
        for cp in copies:
            cp.start()
        for cp in copies:
            cp.wait()

    if in_place:
        out_shapes = [jax.ShapeDtypeStruct(a.shape, a.dtype) for a in inputs]
    return pl.pallas_call(
        body, name=name, in_specs=[ANY] * n_in, out_specs=[ANY] * n_out, out_shape=out_shapes,
        input_output_aliases={i: i for i in range(n_in)} if in_place else {},
        scratch_shapes=[pltpu.SemaphoreType.DMA((n_copies,)), pltpu.SemaphoreType.DMA((n_copies,))],
    )(*inputs)


def _half_rows(ref, half):
    n = ref.shape[0] // 2
    return ref.at[pl.ds(half * n, n)]


def _gather_plan(sources, landings, pos):
    if pos is None:
        return [None] * (3 * len(sources))
    x, y, c = pos
    copies = []
    for src, land in zip(sources, landings):
        for qx, qy in _other_chips(x, y):
            copies.append((_half_rows(src, c), _half_rows(land.at[2 * x + y], c), (qx, qy, c)))
    return copies


def _sibling_plan(landings, _, pos):
    if pos is None:
        return [None] * (3 * len(landings))
    x, y, c = pos
    copies = []
    for land in landings:
        for qx, qy in _other_chips(x, y):
            region = _half_rows(land.at[2 * qx + qy], c)
            copies.append((region, region, (x, y, 1 - c)))
    return copies


def _weights_start(tag, shards):
    landings = [lax.empty((4, *s.shape), BF16) for s in shards]
    return _split_start(f"weights_start_{tag}", shards, landings, _gather_plan)


def _weights_finish(tag, names, started, after):
    shards, landings = _split_wait(f"weights_wait_{tag}", started, after, _gather_plan)
    landings = _exchange("weights_to_sibling", landings, None, _sibling_plan, in_place=True)
    my_shard = 2 * lax.axis_index("x") + lax.axis_index("y")
    full = {}
    for name, shard, land in zip(names, shards, landings):
        sa = BIG_AXES[name][0]
        land = lax.dynamic_update_index_in_dim(land, shard, my_shard, axis=0)
        if name in BY_CHIP:
            full[name] = land
            continue
        axis = sa - 1
        moved = jnp.moveaxis(land, 0, axis)
        full[name] = moved.reshape(*moved.shape[:axis], 4 * moved.shape[axis + 1], *moved.shape[axis + 2:])
    return full


def _half_shape(shape, half_axis):
    return tuple(n // 2 if a == half_axis else n for a, n in enumerate(shape))


def _shard_shape(shape, shard_axis):
    return tuple(n // 4 if a == shard_axis else n for a, n in enumerate(shape))


def _flat2d(a):
    return a.reshape(-1, a.shape[-1])


ELEMENTWISE_TILE_ELEMENTS = 1 << 18


def _row_tile(rows, cols):
    tile = 1
    while rows % (2 * tile) == 0 and 2 * tile * cols <= ELEMENTWISE_TILE_ELEMENTS:
        tile *= 2
    assert tile >= 16 or tile == rows, (rows, cols)
    return tile


def _window_of(arr, axis, parts, index):
    flat = _flat2d(arr)
    rows, cols = flat.shape
    if axis == arr.ndim - 1:
        return flat, (rows, cols // parts), (0, index)
    assert axis == 0, axis
    return flat, (rows // parts, cols), (index, 0)


def _window_sum(name, source, window, index, others, out_dtype):
    w_rows, w_cols = window
    tm = _row_tile(w_rows, w_cols)
    blocks = w_rows // tm
    where = jnp.stack([jnp.asarray(i, jnp.int32) for i in index])

    def body(where_ref, src_ref, *refs):
        acc = src_ref[...].astype(F32)
        for r in refs[:-1]:
            acc = acc + r[...].astype(F32)
        refs[-1][...] = acc.astype(refs[-1].dtype)

    same = pl.BlockSpec((tm, w_cols), lambda i, where_ref: (i, 0))
    grid_spec = pltpu.PrefetchScalarGridSpec(
        num_scalar_prefetch=1, grid=(blocks,),
        in_specs=[pl.BlockSpec((tm, w_cols), lambda i, where_ref: (where_ref[0] * blocks + i, where_ref[1])), *[same] * len(others)],
        out_specs=same)
    return pl.pallas_call(body, name=name, grid_spec=grid_spec, out_shape=jax.ShapeDtypeStruct(window, out_dtype),
                          compiler_params=_params("arbitrary"))(where, source, *others)


def _sum_arrays(name, arrays, out_dtype):
    shape = arrays[0].shape
    flat = [_flat2d(a) for a in arrays]
    rows, cols = flat[0].shape
    tm = _row_tile(rows, cols)

    def fn(*tiles):
        acc = tiles[0].astype(F32)
        for t in tiles[1:]:
            acc = acc + t.astype(F32)
        return acc

    (out,) = _rowwise(name, fn, [(a, cols, 0) for a in flat], [], [(cols, out_dtype)], tm=tm)
    return out.reshape(shape)


def _grads_start(tag, names, grads):
    c = lax.axis_index("c")
    shapes = [g.shape for g in grads]
    axes = [(BIG_AXES[n][0] - 1, BIG_AXES[n][1] - 1) for n in names]

    def to_sibling(ins, outs, pos):
        if pos is None:
            return [None] * len(names)
        px, py, pc = pos
        return [(_region(ins[wi], _cuts(shapes[wi], sa, ha, half=1 - pc)), outs[wi], (px, py, 1 - pc))
                for wi, (sa, ha) in enumerate(axes)]

    landed = _exchange("grads_to_sibling", grads, [jax.ShapeDtypeStruct(_half_shape(s, ha), BF16) for s, (_, ha) in zip(shapes, axes)],
                       to_sibling)
    chip_sums = []
    for wi, name in enumerate(names):
        source, window, index = _window_of(grads[wi], axes[wi][1], 2, c)
        chip_sum = _window_sum("chip_sum_" + name, source, window, index, [_flat2d(landed[wi])], BF16)
        chip_sums.append(chip_sum.reshape(landed[wi].shape))

    def to_chips(sources, landings, pos):
        if pos is None:
            return [None] * (3 * len(names))
        px, py, pc = pos
        copies = []
        for wi, (sa, ha) in enumerate(axes):
            for j, (qx, qy) in enumerate(_other_chips(px, py)):
                piece = _region(sources[wi], _cuts(shapes[wi], sa, ha, shard=2 * qx + qy))
                copies.append((piece, landings[wi].at[j], (qx, qy, pc)))
        return copies

    landings = [lax.empty((3, *_shard_shape(t.shape, sa)), BF16) for t, (sa, _) in zip(chip_sums, axes)]
    return _split_start(f"grads_start_{tag}", chip_sums, landings, to_chips), to_chips


def _grads_finish(tag, names, started, after):
    chip_sums, pieces = _split_wait(f"grads_wait_{tag}", started[0], after, started[1])
    my_shard = 2 * lax.axis_index("x") + lax.axis_index("y")
    reduced = {}
    for name, chip_sum, piece in zip(names, chip_sums, pieces):
        source, window, index = _window_of(chip_sum, BIG_AXES[name][0] - 1, 4, my_shard)
        total = _window_sum("shard_sum_" + name, source, window, index, [_flat2d(piece[j]) for j in range(3)], F32)
        reduced[name] = total.reshape(piece.shape[1:])
    return reduced


def _join_halves(halves, names):
    c = lax.axis_index("c")
    mine = [halves[l][name] for l in range(len(halves)) for name in names]

    def halves_to_sibling(ins, outs, pos):
        if pos is None:
            return [None] * len(mine)
        px, py, pc = pos
        return [(ins[k], outs[k], (px, py, 1 - pc)) for k in range(len(mine))]

    theirs = _exchange("grad_halves_to_sibling", mine, [jax.ShapeDtypeStruct(r.shape, F32) for r in mine], halves_to_sibling)
    out = []
    for wi, name in enumerate(names):
        ha = BIG_AXES[name][1]
        layers = []
        for l in range(len(halves)):
            a, b = mine[l * len(names) + wi], theirs[l * len(names) + wi]
            layers.append(jnp.concatenate([jnp.where(c == 0, a, b), jnp.where(c == 0, b, a)], axis=ha - 1))
        out.append(jnp.stack(layers))
    return out


def _pack(arrays):
    flat = jnp.concatenate([a.reshape(-1).astype(F32) for a in arrays])
    rows = -(-flat.shape[0] // PACK_LANES)
    rows = -(-rows // PACK_ROW_MULTIPLE) * PACK_ROW_MULTIPLE
    return jnp.pad(flat, (0, rows * PACK_LANES - flat.shape[0])).reshape(rows, PACK_LANES)


def _unpack(buf, shapes):
    flat, out, off = buf.reshape(-1), [], 0
    for s in shapes:
        n = 1
        for d in s:
            n *= d
        out.append(flat[off:off + n].reshape(s))
        off += n
    return out


def _sum8(name, gathered, rows):
    def body(in_ref, out_ref):
        acc = in_ref[0].astype(F32)
        for d in range(1, 8):
            acc = acc + in_ref[d].astype(F32)
        out_ref[...] = acc

    return pl.pallas_call(
        body, name=name, grid=(rows // PACK_ROW_MULTIPLE,),
        in_specs=[pl.BlockSpec((8, PACK_ROW_MULTIPLE, PACK_LANES), lambda i: (0, i, 0))],
        out_specs=pl.BlockSpec((PACK_ROW_MULTIPLE, PACK_LANES), lambda i: (i, 0)),
        out_shape=jax.ShapeDtypeStruct((rows, PACK_LANES), F32), compiler_params=_params("arbitrary"),
    )(gathered.reshape(8, rows, PACK_LANES))


def _adamw(name, w, g, m, v):
    shape = w.shape
    flat = [_flat2d(a) for a in (w, g, m, v)]
    rows, cols = flat[0].shape
    tm = _row_tile(rows, cols)
    outs = _rowwise(name, _adamw_fn, [(a, cols, 0) for a in flat], [], [(cols, F32)] * 3, tm=tm)
    return [o.reshape(shape) for o in outs]


WEIGHT_ORDER = ("norm_mix", "w_in", "conf_dw", "conf_dw_b", "conf_ln_g", "conf_ln_b", "pool_w", "pool_scale", "sc_conv",
                "gmlp_ln_g", "gmlp_ln_b", "gmlp_ws", "gmlp_bs", "w_branch", "w_out", "norm_mlp", "w_up", "w_down",
                "norm_ple", "w_ple", "w_ple_gate", "norm_final")


def _local_step(x, p, loss_target, weights, layer_big, full_small_sharded, on_big_grads):
    n_layers = p.shape[0]
    d_model = x.shape[1]
    layer_w, saved = [], []
    h = x
    for l in range(n_layers):
        w = {n: weights[n][l] for n in SMALL_REP if n != "norm_final"}
        w.update({n: full_small_sharded[n][l] for n in SMALL_SHARDED})
        layer_w.append(w)
        h, s = _layer_fwd(h, p[l], w, layer_big(l))
        saved.append(s)
    dx, d_final, loss = _rowwise("loss", _loss_fn, [(h, d_model, 0), (loss_target, d_model, 0)], [_row(weights["norm_final"])],
                                 [(d_model, F32)], [(1, d_model), (1, PACK_LANES)])
    small_grads = {n: [None] * n_layers for n in SMALL_REP + SMALL_SHARDED if n != "norm_final"}
    for l in reversed(range(n_layers)):
        dx, small = _layer_bwd(dx, p[l], layer_w[l], saved[l], on_big_grads(l))
        for n in small:
            small_grads[n][l] = small[n].reshape(weights[n].shape[1:]) if n in SMALL_REP else small[n]
    small_grads = {n: jnp.stack(v) for n, v in small_grads.items()}
    small_grads["norm_final"] = d_final.reshape(-1)
    return loss, dx, small_grads


def kernel(x, p, norm_mix, w_in, conf_dw, conf_dw_b, conf_ln_g, conf_ln_b, pool_w, pool_scale, sc_conv, gmlp_ln_g, gmlp_ln_b, gmlp_ws, gmlp_bs, w_branch, w_out, norm_mlp, w_up, w_down, norm_ple, w_ple, w_ple_gate, norm_final, loss_target, m_norm_mix, m_w_in, m_conf_dw, m_conf_dw_b, m_conf_ln_g, m_conf_ln_b, m_pool_w, m_pool_scale, m_sc_conv, m_gmlp_ln_g, m_gmlp_ln_b, m_gmlp_ws, m_gmlp_bs, m_w_branch, m_w_out, m_norm_mlp, m_w_up, m_w_down, m_norm_ple, m_w_ple, m_w_ple_gate, m_norm_final, v_norm_mix, v_w_in, v_conf_dw, v_conf_dw_b, v_conf_ln_g, v_conf_ln_b, v_pool_w, v_pool_scale, v_sc_conv, v_gmlp_ln_g, v_gmlp_ln_b, v_gmlp_ws, v_gmlp_bs, v_w_branch, v_w_out, v_norm_mlp, v_w_up, v_w_down, v_norm_ple, v_w_ple, v_w_ple_gate, v_norm_final):
    given = dict(locals())
    weights = {n: given[n] for n in WEIGHT_ORDER}
    mom_m = {n: given["m_" + n] for n in WEIGHT_ORDER}
    mom_v = {n: given["v_" + n] for n in WEIGHT_ORDER}
    my_shard = 2 * lax.axis_index("x") + lax.axis_index("y")

    n_layers = p.shape[0]
    shard_shapes = [weights[n].shape for n in SMALL_SHARDED]
    gathered = _allgather_small("allgather_small_weights", _pack([weights[n] for n in SMALL_SHARDED]))
    after_small = gathered[0, 0] * 0.0
    after_small = jnp.where(after_small == 0.0, after_small, 0.0)

    weight_copies = [[None] * len(WEIGHT_GROUPS) for _ in range(n_layers)]
    started = after_small
    for l in range(n_layers):
        for g, names in enumerate(WEIGHT_GROUPS):
            shards = [(weights[n][l] + started if i == 0 else weights[n][l]).astype(BF16) for i, n in enumerate(names)]
            weight_copies[l][g] = _weights_start(f"{l}_{g}", shards)
            started = weight_copies[l][g][4][0, 0]
    per_dev = gathered.reshape(8, -1, PACK_LANES)
    full_small_sharded = {}
    chip_parts = [_unpack(per_dev[2 * s], shard_shapes) for s in range(4)]
    for i, n in enumerate(SMALL_SHARDED):
        full_small_sharded[n] = jnp.concatenate([chip_parts[s][i] for s in range(4)], axis=-1)

    grad_copies = [[None] * len(GRAD_GROUPS) for _ in range(n_layers)]

    def layer_big(l):
        return lambda g, after: _weights_finish(f"{l}_{g}", WEIGHT_GROUPS[g], weight_copies[l][g], after)

    def on_big_grads(l):
        def send(g, group_grads):
            grad_copies[l][g] = _grads_start(f"{l}_{g}", GRAD_GROUPS[g], group_grads)
            return grad_copies[l][g][0][4][0, 0]
        return send

    loss_row, dx, small_grads = _local_step(x[0] + started, p[:, 0], loss_target[0], weights, layer_big, full_small_sharded,
                                            on_big_grads)
    loss = lax.psum(loss_row[0, 0], MESH_AXES)

    grads, delta, new_m, new_v = {}, {}, {}, {}
    halves = [{} for _ in range(n_layers)]
    last_sent = (0, len(GRAD_GROUPS) - 1)
    for l in reversed(range(n_layers)):
        for g, names in enumerate(GRAD_GROUPS):
            if (l, g) != last_sent:
                halves[l].update(_grads_finish(f"{l}_{g}", names, grad_copies[l][g], dx))
    behind = dx
    for g, names in enumerate(GRAD_GROUPS):
        if g == last_sent[1]:
            halves[0].update(_grads_finish(f"0_{g}", names, grad_copies[0][g], behind))
        for n, joined in zip(names, _join_halves(halves, names)):
            grads[n] = joined
            delta[n], new_m[n], new_v[n] = _adamw("adamw_" + n, weights[n], joined, mom_m[n], mom_v[n])
            behind = delta[n]

    small_names = SMALL_REP + SMALL_SHARDED
    after_big = behind[(0,) * behind.ndim] * 0.0
    after_big = jnp.where(after_big == 0.0, after_big, 0.0)
    packed = _pack([small_grads[n] for n in small_names]) + after_big
    rows = packed.shape[0]
    summed = _sum8("sum_small_grads", _allgather_small("allgather_small_grads", packed.astype(BF16)), rows)
    small_full_shapes = [weights[n].shape for n in SMALL_REP] + [full_small_sharded[n].shape for n in SMALL_SHARDED]
    for n, g in zip(small_names, _unpack(summed, small_full_shapes)):
        if n in SMALL_SHARDED:
            width = weights[n].shape[-1]
            g = lax.dynamic_slice_in_dim(g, my_shard * width, width, axis=g.ndim - 1)
        grads[n] = g

    for group, names in (("rep", SMALL_REP), ("sharded", SMALL_SHARDED)):
        outs = _adamw("adamw_small_" + group, *[_pack([src[n] for n in names]) for src in (weights, grads, mom_m, mom_v)])
        shapes = [weights[n].shape for n in names]
        for dst, buf in zip((delta, new_m, new_v), outs):
            for n, a in zip(names, _unpack(buf, shapes)):
                dst[n] = a

    return (loss, dx[None], *[grads[n] for n in WEIGHT_ORDER], *[delta[n] for n in WEIGHT_ORDER],
            *[new_m[n] for n in WEIGHT_ORDER], *[new_v[n] for n in WEIGHT_ORDER])
```

```python
import jax
import jax.numpy as jnp
from jax import lax
from jax.experimental import pallas as pl
from jax.experimental.pallas import tpu as pltpu

F32, BF16 = jnp.float32, jnp.bfloat16
EPS = 1e-6
ADAM_LR, ADAM_B1, ADAM_B2, ADAM_EPS, ADAM_WD, ADAM_STEP = 0.001, 0.9, 0.999, 1e-08, 0.01, 10
CONF_KERNEL, SC_KERNEL = 31, 3
POOL_WINDOWS = (2, 4, 8, 16)
MIX = 512
GROUP = 128
HALO = 32
SEQ_TILE = 256
BRANCH_ROWS = 512
VMEM_LIMIT_BYTES = 48 * 1024 * 1024
MESH_AXES = ("x", "y", "c")
ANY = pl.BlockSpec(memory_space=pl.ANY)

BIG = (("w_in", 2, 1), ("w_branch", 3, 1), ("w_out", 1, 2), ("w_up", 2, 1), ("w_down", 1, 2),
       ("w_ple", 2, 1), ("w_ple_gate", 1, 2))
SMALL_REP = ("norm_mix", "conf_dw_b", "conf_ln_g", "conf_ln_b", "pool_w", "pool_scale", "gmlp_ln_g",
             "gmlp_ln_b", "gmlp_ws", "gmlp_bs", "norm_mlp", "norm_ple", "norm_final")
SMALL_SHARDED = ("conf_dw", "sc_conv")
BY_CHIP = ("w_in", "w_up")
BIG_AXES = {name: (shard_axis, half_axis) for name, shard_axis, half_axis in BIG}
WEIGHT_GROUPS = (("w_in",), ("w_branch", "w_out", "w_up", "w_down", "w_ple_gate", "w_ple"))
GRAD_GROUPS = (("w_ple", "w_ple_gate", "w_down", "w_up"), ("w_out", "w_branch", "w_in"))
PACK_LANES = 128
PACK_ROW_MULTIPLE = 256


def _params(*semantics):
    return pltpu.CompilerParams(dimension_semantics=semantics, vmem_limit_bytes=VMEM_LIMIT_BYTES)


def _accumulate(ref, value, first):
    @pl.when(first)
    def _():
        ref[...] = value.astype(ref.dtype)

    @pl.when(jnp.logical_not(first))
    def _():
        ref[...] += value.astype(ref.dtype)


def _rowwise(name, fn, rows, params, outs, accs=(), tm=256):
    n_rows = rows[0][0].shape[-2]
    tm = min(tm, n_rows)
    n_in, n_out = len(rows) + len(params), len(outs)
    targets = [(k, o[2]) for k, o in enumerate(outs) if len(o) == 4]

    def body(*refs):
        res = fn(*[r[...] for r in refs[:n_in]])
        res = res if isinstance(res, (tuple, list)) else (res,)
        out_refs = refs[n_in + len(targets):]
        for r, v in zip(out_refs[:n_out], res[:n_out]):
            r[...] = v.astype(r.dtype)
        first = pl.program_id(0) == 0
        for r, v in zip(out_refs[n_out:], res[n_out:]):
            _accumulate(r, v, first)

    in_specs = []
    for row in rows:
        arr, w, cb = row[:3]
        if len(row) == 4:
            in_specs.append(pl.BlockSpec((None, tm, w), lambda i, cb=cb, g=row[3]: (g, i, cb)))
        else:
            in_specs.append(pl.BlockSpec((tm, w), lambda i, cb=cb: (i, cb)))
    for p in params:
        in_specs.append(pl.BlockSpec(p.shape, lambda i, nd=p.ndim: (0,) * nd))
    in_specs += [ANY] * len(targets)
    out_specs = [pl.BlockSpec((tm, o[0]), lambda i, cb=(o[3] if len(o) == 4 else 0): (i, cb)) for o in outs]
    out_specs += [pl.BlockSpec(s, lambda i, nd=len(s): (0,) * nd) for s in accs]
    out_shape = [jax.ShapeDtypeStruct(o[2].shape if len(o) == 4 else (n_rows, o[0]), o[1]) for o in outs]
    out_shape += [jax.ShapeDtypeStruct(s, F32) for s in accs]
    return pl.pallas_call(
        body, name=name, grid=(n_rows // tm,), in_specs=in_specs, out_specs=out_specs, out_shape=out_shape,
        input_output_aliases={n_in + t: k for t, (k, _) in enumerate(targets)},
        compiler_params=_params("arbitrary"),
    )(*[r[0] for r in rows], *params, *[t for _, t in targets])


ROW_OUT_SUBLANES = 8


def _mm(name, a, b, *, ta=False, tb=False, a_cols=0, b_by_chip=False, chips=(0, 4), res=(), rows=(), epi=None, out_dtypes=(F32,), n_row_outs=0,
        tm=1024, tn=1024, tk=1024):
    if a_cols:
        assert ta
        g_, m_, k_ = a.shape[1] // a_cols, a_cols, a.shape[0]
    else:
        g_, m_, k_ = (a.shape[0], a.shape[2], a.shape[1]) if ta else a.shape
    if b_by_chip:
        n_ = b.shape[2] if tb else chips[1] * b.shape[3]
    else:
        n_ = b.shape[1] if tb else b.shape[2]
    if ta:
        tk = 2 * tk
    tm, tn, tk = min(tm, m_), min(tn, n_), min(tk, k_)
    if b_by_chip:
        tn, tk = (tn, min(tk, b.shape[3])) if tb else (min(tn, b.shape[3]), tk)
        per_chip = b.shape[3] // (tk if tb else tn)
    nk, n_res, n_rows, n_out = k_ // tk, len(res), len(rows), len(out_dtypes)
    n_in = 2 + n_res + n_rows
    dims = (((0 if ta else 1,), (1 if tb else 0,)), ((), ()))
    if epi is None:
        epi = lambda acc: (acc,)

    def body(*refs):
        a_ref, b_ref = refs[:2]
        res_refs, out_refs, acc_ref = refs[2:n_in], refs[n_in:n_in + n_out], refs[-1]
        row_out_refs = refs[n_in + n_out:-1]
        k = pl.program_id(3)

        def product():
            return lax.dot_general(a_ref[...].astype(BF16), b_ref[...].astype(BF16), dims, preferred_element_type=F32)

        def finish(acc):
            vals = epi(acc, *[r[...] for r in res_refs])
            for o, v in zip(out_refs, vals):
                o[...] = v.astype(o.dtype)
            for o, v in zip(row_out_refs, vals[n_out:]):
                o[...] = jnp.broadcast_to(v, o.shape)

        if nk == 1:
            finish(product())
            return

        @pl.when(k == 0)
        def _():
            acc_ref[...] = jnp.zeros_like(acc_ref)

        acc_ref[...] += product()

        @pl.when(k == nk - 1)
        def _():
            finish(acc_ref[...])

    if a_cols:
        a_spec = pl.BlockSpec((tk, tm), lambda g, i, j, k: (k, g * (m_ // tm) + i))
    elif ta:
        a_spec = pl.BlockSpec((None, tk, tm), lambda g, i, j, k: (g, k, i))
    else:
        a_spec = pl.BlockSpec((None, tm, tk), lambda g, i, j, k: (g, i, k))
    if b_by_chip and tb:
        b_spec = pl.BlockSpec((None, None, tn, tk), lambda g, i, j, k: (g, k // per_chip, j, k % per_chip))
    elif b_by_chip:
        b_spec = pl.BlockSpec((None, None, tk, tn), lambda g, i, j, k: (g, chips[0] + j // per_chip, k, j % per_chip))
    elif tb:
        b_spec = pl.BlockSpec((None, tn, tk), lambda g, i, j, k: (g, j, k))
    else:
        b_spec = pl.BlockSpec((None, tk, tn), lambda g, i, j, k: (g, k, j))
    res_specs = [pl.BlockSpec((None, tm, tn), lambda g, i, j, k, off=off: (g, i, j + off)) for _, off in res]
    row_specs = [pl.BlockSpec((None, 1, tn), lambda g, i, j, k: (g, 0, j))] * n_rows
    out_spec = pl.BlockSpec((None, tm, tn), lambda g, i, j, k: (g, i, j))
    row_out_spec = pl.BlockSpec((None, ROW_OUT_SUBLANES, tn), lambda g, i, j, k: (g, i, j))
    outs = pl.pallas_call(
        body, name=name, grid=(g_, m_ // tm, n_ // tn, nk),
        in_specs=[a_spec, b_spec, *res_specs, *row_specs], out_specs=[out_spec] * n_out + [row_out_spec] * n_row_outs,
        out_shape=[jax.ShapeDtypeStruct((g_, m_, n_), d) for d in out_dtypes]
        + [jax.ShapeDtypeStruct((g_, ROW_OUT_SUBLANES * (m_ // tm), n_), F32)] * n_row_outs,
        scratch_shapes=[pltpu.VMEM((tm, tn), F32)],
        compiler_params=_params("parallel", "parallel", "parallel", "arbitrary"),
    )(a, b, *[r for r, _ in res], *rows)
    return outs


def _mm2(name, a, b, *, res=(), rows=(), **kw):
    outs = _mm(name, a[None], b[None], res=[(r[None], off) for r, off in res], rows=[r[None] for r in rows], **kw)
    return [o[0] for o in outs]


def _sum_row_out(row_out):
    return jnp.sum(row_out[::ROW_OUT_SUBLANES], axis=0, keepdims=True)


@jax.custom_vjp
def _bdot(a, b):
    return jnp.dot(a.astype(BF16), b.astype(BF16), preferred_element_type=F32)


def _bdot_fwd(a, b):
    return _bdot(a, b), (a, b)


def _bdot_bwd(saved, g):
    a, b = saved
    gb = g.astype(BF16)
    da = lax.dot_general(gb, b.astype(BF16), (((1,), (1,)), ((), ())), preferred_element_type=F32)
    db = lax.dot_general(a.astype(BF16), gb, (((0,), (0,)), ((), ())), preferred_element_type=F32)
    return da.astype(a.dtype), db.astype(b.dtype)


_bdot.defvjp(_bdot_fwd, _bdot_bwd)


def _rms(x, g):
    return x * lax.rsqrt(jnp.mean(x * x, axis=-1, keepdims=True) + EPS) * g


def _layer_norm(x, g, b):
    mu = jnp.mean(x, axis=-1, keepdims=True)
    var = jnp.mean(jnp.square(x - mu), axis=-1, keepdims=True)
    return (x - mu) * lax.rsqrt(var + EPS) * g + b


def _glu_and_gate_fn(a, a_gate, cg, hx):
    return a * jax.nn.sigmoid(a_gate), cg * hx


def _branches_fn(ca, pooled, cc, bg, u, v, conf_b, conf_g, conf_beta, pool_scale, gm_g, gm_b, *mats):
    pool_w, ws, bs = mats[0:4], mats[4:8], mats[8:12]
    ya = jax.nn.silu(_layer_norm(ca + conf_b, conf_g, conf_beta))
    yb = jnp.concatenate([_bdot(pooled[:, g * GROUP:(g + 1) * GROUP], pool_w[g]) for g in range(4)], axis=1) * pool_scale
    yc = bg * cc
    vln = _layer_norm(v, gm_g, gm_b)
    causal = lax.broadcasted_iota(jnp.int32, (GROUP, GROUP), 0) >= lax.broadcasted_iota(jnp.int32, (GROUP, GROUP), 1)
    chunks = v.shape[0] // GROUP

    def spatial(g):
        vg = vln[:, g * GROUP:(g + 1) * GROUP]
        wide = jnp.concatenate([vg[c * GROUP:(c + 1) * GROUP, :] for c in range(chunks)], axis=1)
        out = _bdot(jnp.where(causal, ws[g], 0.0), wide) + bs[g]
        return jnp.concatenate([out[:, c * GROUP:(c + 1) * GROUP] for c in range(chunks)], axis=0)

    sg = jnp.concatenate([spatial(g) for g in range(4)], axis=1)
    yd = u * sg
    return ya, yb, yc, yd


def _branches_joined_fn(*args):
    return jnp.concatenate(_branches_fn(*args), axis=1)


def _branches_bwd_fn(*args):
    ins, dys = args[:6] + args[10:], args[6:10]
    _, vjp = jax.vjp(_branches_fn, *ins)
    grads = vjp(tuple(dys))
    duv = jnp.concatenate([grads[4], grads[5]], axis=1)
    return (*grads[:4], duv, *grads[6:])


def _glu_bwd_fn(a, a_gate, cg, hx, dga, dchx, dpin, dbg):
    _, vjp = jax.vjp(_glu_and_gate_fn, a, a_gate, cg, hx)
    da, dag, dcg, dhx = vjp((dga, dchx))
    return jnp.concatenate([da, dag, dpin.astype(F32), dbg.astype(F32), dcg, dhx], axis=1)


def _rms_bwd_epilogue(dh, x, dres, g):
    _, vjp = jax.vjp(_rms, x, g)
    dx, dg = vjp(dh)
    return dx + dres, dg


def _residual_then_rms(acc, residual, g):
    x = acc + residual
    return x, _rms(x, g)


def _loss_fn(x, target, g):
    def f(x, g):
        err = jnp.square(_rms(x, g) - target)
        return 0.5 * jnp.sum(jnp.mean(err, axis=-1, keepdims=True), axis=0, keepdims=True)

    loss, vjp = jax.vjp(f, x, g)
    dx, dg = vjp(jnp.ones((1, 1), F32))
    return dx, dg, jnp.broadcast_to(loss, (1, PACK_LANES))


def _adamw_fn(w, g, m, v):
    m = ADAM_B1 * m + (1.0 - ADAM_B1) * g
    v = ADAM_B2 * v + (1.0 - ADAM_B2) * jnp.square(g)
    m_hat = m / (1.0 - ADAM_B1 ** ADAM_STEP)
    v_hat = v / (1.0 - ADAM_B2 ** ADAM_STEP)
    delta = -ADAM_LR * (m_hat / (jnp.sqrt(v_hat) + ADAM_EPS) + ADAM_WD * w)
    return delta, m, v


def _group_pick(j):
    return [(j == g).astype(F32) for g in range(len(POOL_WINDOWS))]


SUBLANES = 8


def _causal_taps(x, taps):
    for r in range(min(SUBLANES, taps)):
        behind = x if r == 0 else pltpu.roll(x, r, 0)
        for q in range((taps - 1 - r) // SUBLANES + 1):
            start = HALO - SUBLANES * q
            yield SUBLANES * q + r, behind[start:start + SEQ_TILE, :]


def _rows_iota(n):
    return lax.broadcasted_iota(jnp.int32, (n, 1), 0)


def _seq_specs(n_time, causal):
    per = SEQ_TILE // HALO
    tile = pl.BlockSpec((SEQ_TILE, GROUP), lambda j, i: (i, j))
    if causal:
        halo = pl.BlockSpec((HALO, GROUP), lambda j, i: (jnp.maximum(i * per - 1, 0), j))
    else:
        halo = pl.BlockSpec((HALO, GROUP), lambda j, i: (jnp.minimum((i + 1) * per, n_time * per - 1), j))
    return tile, halo


def _seq_fwd(ga, proj, chx, conf_dw, sc_conv):
    s_len = ga.shape[0]
    n_time = s_len // SEQ_TILE
    ext = HALO + SEQ_TILE

    def body(ga_ref, ga_h, pin_ref, pin_h, chx_ref, chx_h, wc_ref, ws_ref, ca_ref, po_ref, cc_ref):
        j, i = pl.program_id(0), pl.program_id(1)
        keep = (i > 0).astype(F32)

        def extended(cur, halo):
            return jnp.concatenate([halo[...] * keep, cur[...]], axis=0)

        def conv(x, w_ref, taps):
            acc = None
            for d, rows in _causal_taps(x, taps):
                term = rows * w_ref[pl.ds(taps - 1 - d, 1), :]
                acc = term if acc is None else acc + term
            return acc

        ca_ref[...] = conv(extended(ga_ref, ga_h), wc_ref, CONF_KERNEL)
        cc_ref[...] = conv(extended(chx_ref, chx_h), ws_ref, SC_KERNEL)
        u = extended(pin_ref, pin_h)
        s1 = u + pltpu.roll(u, 1, 0)
        s2 = s1 + pltpu.roll(s1, 2, 0)
        s3 = s2 + pltpu.roll(s2, 4, 0)
        s4 = s3 + pltpu.roll(s3, 8, 0)
        pick = _group_pick(j)
        window_sum = s1 * pick[0] + s2 * pick[1] + s3 * pick[2] + s4 * pick[3]
        window = sum(float(w) * f for w, f in zip(POOL_WINDOWS, pick))
        pos = (_rows_iota(ext) + (i * SEQ_TILE - HALO + 1)).astype(F32)
        count = jnp.maximum(jnp.minimum(pos, window), 1.0)
        po_ref[...] = (window_sum / count - u)[HALO:, :]

    tile, halo = _seq_specs(n_time, causal=True)
    pin_tile = pl.BlockSpec((SEQ_TILE, GROUP), lambda j, i: (i, 2 * (MIX // GROUP) + j))
    pin_halo = pl.BlockSpec((HALO, GROUP), lambda j, i: (jnp.maximum(i * (SEQ_TILE // HALO) - 1, 0), 2 * (MIX // GROUP) + j))
    w_spec = lambda taps: pl.BlockSpec((taps, GROUP), lambda j, i: (0, j))
    return pl.pallas_call(
        body, name="seq_fwd", grid=(MIX // GROUP, n_time),
        in_specs=[tile, halo, pin_tile, pin_halo, tile, halo, w_spec(CONF_KERNEL), w_spec(SC_KERNEL)],
        out_specs=[tile] * 3, out_shape=[jax.ShapeDtypeStruct((s_len, MIX), F32)] * 3,
        compiler_params=_params("parallel", "arbitrary"),
    )(ga, ga, proj, proj, chx, chx, conf_dw, sc_conv)


def _seq_bwd(dca, dpooled, dcc, ga, chx, conf_dw, sc_conv):
    s_len = ga.shape[0]
    n_time = s_len // SEQ_TILE
    ext = HALO + SEQ_TILE

    def body(dca_ref, dca_h, dpo_ref, dpo_h, dcc_ref, dcc_h, ga_ref, ga_h, chx_ref, chx_h, wc_ref, ws_ref,
             dga_ref, dpin_ref, dchx_ref, dwc_ref, dws_ref, acc_c, acc_s):
        j, i = pl.program_id(0), pl.program_id(1)
        keep_prev = (i > 0).astype(F32)
        keep_next = (i < n_time - 1).astype(F32)

        def with_next(cur, halo):
            return jnp.concatenate([cur[...], halo[...] * keep_next], axis=0)

        def with_prev(cur, halo):
            return jnp.concatenate([halo[...] * keep_prev, cur[...]], axis=0)

        def conv_t(dy, w_ref, taps):
            acc = None
            for r in range(min(SUBLANES, taps)):
                ahead = dy if r == 0 else pltpu.roll(dy, ext - r, 0)
                for q in range((taps - 1 - r) // SUBLANES + 1):
                    term = ahead[SUBLANES * q:SUBLANES * q + SEQ_TILE, :] * w_ref[pl.ds(taps - 1 - (SUBLANES * q + r), 1), :]
                    acc = term if acc is None else acc + term
            return acc

        def tap_grads(dy_cur, x, acc_ref, dw_ref, taps):
            @pl.when(i == 0)
            def _():
                acc_ref[...] = jnp.zeros_like(acc_ref)

            for d, rows in _causal_taps(x, taps):
                prod = dy_cur * rows
                part = prod[:SUBLANES, :]
                for g in range(1, SEQ_TILE // SUBLANES):
                    part = part + prod[g * SUBLANES:(g + 1) * SUBLANES, :]
                acc_ref[pl.ds((taps - 1 - d) * SUBLANES, SUBLANES), :] += part

            @pl.when(i == n_time - 1)
            def _():
                for k in range(taps):
                    dw_ref[pl.ds(k, 1), :] = jnp.sum(acc_ref[pl.ds(k * SUBLANES, SUBLANES), :], axis=0, keepdims=True)

        dga_ref[...] = conv_t(with_next(dca_ref, dca_h), wc_ref, CONF_KERNEL)
        dchx_ref[...] = conv_t(with_next(dcc_ref, dcc_h), ws_ref, SC_KERNEL)
        tap_grads(dca_ref[...], with_prev(ga_ref, ga_h), acc_c, dwc_ref, CONF_KERNEL)
        tap_grads(dcc_ref[...], with_prev(chx_ref, chx_h), acc_s, dws_ref, SC_KERNEL)

        dpo = with_next(dpo_ref, dpo_h)
        pick = _group_pick(j)
        window = sum(float(w) * f for w, f in zip(POOL_WINDOWS, pick))
        pos = (_rows_iota(ext) + (i * SEQ_TILE + 1)).astype(F32)
        r = dpo / jnp.minimum(pos, window)
        q1 = r + pltpu.roll(r, ext - 1, 0)
        q2 = q1 + pltpu.roll(q1, ext - 2, 0)
        q3 = q2 + pltpu.roll(q2, ext - 4, 0)
        q4 = q3 + pltpu.roll(q3, ext - 8, 0)
        q = q1 * pick[0] + q2 * pick[1] + q3 * pick[2] + q4 * pick[3]
        dpin_ref[...] = (q - dpo)[:SEQ_TILE, :].astype(dpin_ref.dtype)

    tile, nxt = _seq_specs(n_time, causal=False)
    _, prv = _seq_specs(n_time, causal=True)
    w_spec = lambda taps: pl.BlockSpec((taps, GROUP), lambda j, i: (0, j))
    return pl.pallas_call(
        body, name="seq_bwd", grid=(MIX // GROUP, n_time),
        in_specs=[tile, nxt, tile, nxt, tile, nxt, tile, prv, tile, prv, w_spec(CONF_KERNEL), w_spec(SC_KERNEL)],
        out_specs=[tile, tile, tile, w_spec(CONF_KERNEL), w_spec(SC_KERNEL)],
        out_shape=[jax.ShapeDtypeStruct((s_len, MIX), F32), jax.ShapeDtypeStruct((s_len, MIX), BF16),
                   jax.ShapeDtypeStruct((s_len, MIX), F32), jax.ShapeDtypeStruct(conf_dw.shape, F32),
                   jax.ShapeDtypeStruct(sc_conv.shape, F32)],
        scratch_shapes=[pltpu.VMEM((CONF_KERNEL * SUBLANES, GROUP), F32), pltpu.VMEM((SC_KERNEL * SUBLANES, GROUP), F32)],
        compiler_params=_params("parallel", "arbitrary"),
    )(dca, dca, dpooled, dpooled, dcc, dcc, ga, ga, chx, chx, conf_dw, sc_conv)


MERGE_TILE = 512
MERGE_ROWS = 1024


def _merge_fwd(y, w_branch, gates):
    s_len, d_model = y.shape[0], w_branch.shape[2]
    t = min(MERGE_ROWS, s_len)
    per_branch = d_model // MERGE_TILE

    def body(y_ref, w_ref, gate_ref, out_ref, acc_ref):
        k = pl.program_id(2)

        @pl.when(k == 0)
        def _():
            acc_ref[...] = jnp.zeros_like(acc_ref)

        z = jnp.dot(y_ref[...], w_ref[...], preferred_element_type=F32)
        acc_ref[...] += gate_ref[...].astype(F32) * z

        @pl.when(k == 3)
        def _():
            out_ref[...] = acc_ref[...].astype(out_ref.dtype)

    return pl.pallas_call(
        body, name="merge_fwd", grid=(s_len // t, per_branch, 4),
        in_specs=[pl.BlockSpec((t, MIX), lambda i, j, k: (i, k)),
                  pl.BlockSpec((None, MIX, MERGE_TILE), lambda i, j, k: (k, 0, j)),
                  pl.BlockSpec((t, MERGE_TILE), lambda i, j, k: (i, k * per_branch + j))],
        out_specs=pl.BlockSpec((t, MERGE_TILE), lambda i, j, k: (i, j)),
        out_shape=jax.ShapeDtypeStruct((s_len, d_model), BF16),
        scratch_shapes=[pltpu.VMEM((t, MERGE_TILE), F32)],
        compiler_params=_params("parallel", "parallel", "arbitrary"),
    )(y, w_branch, gates)


def _merge_bwd(y, w_branch, gates, dmerged):
    s_len, d_model = y.shape[0], w_branch.shape[2]
    t = min(MERGE_ROWS, s_len)
    per_branch = d_model // MERGE_TILE

    def body(y_ref, w_ref, gate_ref, dm_ref, dz_ref, dgate_ref):
        z = jnp.dot(y_ref[...], w_ref[...], preferred_element_type=F32)
        gate = gate_ref[...].astype(F32)
        dm = dm_ref[...]
        dz_ref[...] = (dm * gate).astype(dz_ref.dtype)
        dgate_ref[...] = (dm * z * gate * (1.0 - gate)).astype(dgate_ref.dtype)

    return pl.pallas_call(
        body, name="merge_bwd", grid=(s_len // t, per_branch, 4),
        in_specs=[pl.BlockSpec((t, MIX), lambda i, j, k: (i, k)),
                  pl.BlockSpec((None, MIX, MERGE_TILE), lambda i, j, k: (k, 0, j)),
                  pl.BlockSpec((t, MERGE_TILE), lambda i, j, k: (i, k * per_branch + j)),
                  pl.BlockSpec((t, MERGE_TILE), lambda i, j, k: (i, j))],
        out_specs=[pl.BlockSpec((None, t, MERGE_TILE), lambda i, j, k: (k, i, j)),
                   pl.BlockSpec((t, MERGE_TILE), lambda i, j, k: (i, (4 + k) * per_branch + j))],
        out_shape=[jax.ShapeDtypeStruct((4, s_len, d_model), BF16), jax.ShapeDtypeStruct((s_len, 8 * d_model), BF16)],
        compiler_params=_params("parallel", "parallel", "arbitrary"),
    )(y, w_branch, gates, dmerged)


def _row(v):
    return v.reshape(1, -1)


def _branch_params(w):
    mats = [w["pool_w"][g] for g in range(4)] + [w["gmlp_ws"][g] for g in range(4)]
    mats += [w["gmlp_bs"][g].reshape(GROUP, 1) for g in range(4)]
    return [_row(w["conf_dw_b"]), _row(w["conf_ln_g"]), _row(w["conf_ln_b"]), _row(w["pool_scale"]),
            _row(w["gmlp_ln_g"]), _row(w["gmlp_ln_b"])] + mats


def _layer_fwd(x0, p_l, w, fetch):
    d_model = x0.shape[1]
    (h1,) = _rowwise("rms_mix", _rms, [(x0, d_model, 0)], [_row(w["norm_mix"])], [(d_model, BF16)])
    w.update(fetch(0, h1))
    (proj,) = _mm2("proj", h1, w["w_in"], b_by_chip=True, chips=(0, 2))
    (gates,) = _mm2("proj_gates", h1, w["w_in"], b_by_chip=True, chips=(2, 2), out_dtypes=(BF16,),
                    epi=lambda acc: (jax.nn.sigmoid(acc),))
    ga, chx = _rowwise("glu", _glu_and_gate_fn, [(proj, MIX, 0), (proj, MIX, 1), (proj, MIX, 4), (proj, MIX, 5)], [],
                       [(MIX, F32), (MIX, F32)])
    ca, pooled, cc = _seq_fwd(ga, proj, chx, w["conf_dw"], w["sc_conv"])
    (y,) = _rowwise("branches", _branches_joined_fn,
                    [(ca, MIX, 0), (pooled, MIX, 0), (cc, MIX, 0), (proj, MIX, 3), (proj, MIX, 6), (proj, MIX, 7)],
                    _branch_params(w), [(4 * MIX, BF16)], tm=BRANCH_ROWS)
    w.update(fetch(1, y))
    merged = _merge_fwd(y, w["w_branch"], gates)
    x1, h2 = _mm2("out_proj", merged, w["w_out"], res=[(x0, 0)], rows=[_row(w["norm_mlp"])], out_dtypes=(F32, BF16),
                  epi=_residual_then_rms)
    (act,) = _mm2("mlp_up", h2, w["w_up"], b_by_chip=True, out_dtypes=(BF16,),
                  epi=lambda acc: (jnp.square(jnp.maximum(acc, 0.0)),))
    x2, h3 = _mm2("mlp_down", act, w["w_down"], res=[(x1, 0)], rows=[_row(w["norm_ple"])], out_dtypes=(F32, BF16),
                  epi=_residual_then_rms)
    (pg,) = _mm2("ple_gate", h3, w["w_ple_gate"])
    (x3,) = _mm2("ple", p_l, w["w_ple"], res=[(pg, 0), (x2, 0)],
                 epi=lambda acc, g, r: (acc * jax.nn.sigmoid(g) + r,))
    saved = dict(x0=x0, h1=h1, proj=proj, gates=gates, ga=ga, chx=chx, ca=ca, pooled=pooled, cc=cc, y=y, merged=merged, x1=x1,
                 h2=h2, act=act, x2=x2, h3=h3, pg=pg)
    return x3, saved


def _layer_bwd(dx3, p_l, w, s, send):
    d_model = dx3.shape[1]
    big, small = {}, {}

    def ple_epi(acc, dx, g):
        sig = jax.nn.sigmoid(g)
        return dx * sig, dx * acc * sig * (1.0 - sig)

    dpp, dpg = _mm2("ple_bwd", p_l, w["w_ple"], res=[(dx3, 0), (s["pg"], 0)], epi=ple_epi, out_dtypes=(BF16, BF16))
    (big["w_ple"],) = _mm2("dw_ple", p_l, dpp, ta=True, out_dtypes=(BF16,))
    (big["w_ple_gate"],) = _mm2("dw_ple_gate", s["h3"], dpg, ta=True, out_dtypes=(BF16,))
    dx2, dg = _mm2("dh_ple", dpg, w["w_ple_gate"], tb=True, res=[(s["x2"], 0), (dx3, 0)], rows=[_row(w["norm_ple"])],
                   epi=_rms_bwd_epilogue, n_row_outs=1)
    small["norm_ple"] = _sum_row_out(dg)

    (dup,) = _mm2("mlp_down_bwd", dx2, w["w_down"], tb=True, res=[(s["act"], 0)], out_dtypes=(BF16,),
                  epi=lambda acc, act: (acc * 2.0 * jnp.sqrt(act.astype(F32)),))
    (big["w_down"],) = _mm2("dw_down", s["act"], dx2, ta=True, out_dtypes=(BF16,))
    (big["w_up"],) = _mm2("dw_up", s["h2"], dup, ta=True, out_dtypes=(BF16,))
    sent = send(0, [big[n] for n in GRAD_GROUPS[0]])
    dx1, dg = _mm2("mlp_up_bwd", dup, w["w_up"], tb=True, b_by_chip=True, res=[(s["x1"], 0), (dx2, 0)],
                   rows=[_row(w["norm_mlp"]) + sent], epi=_rms_bwd_epilogue, n_row_outs=1)
    small["norm_mlp"] = _sum_row_out(dg)

    (dmerged,) = _mm2("out_proj_bwd", dx1, w["w_out"], tb=True)
    (big["w_out"],) = _mm2("dw_out", s["merged"], dx1, ta=True, out_dtypes=(BF16,))
    dz, dproj = _merge_bwd(s["y"], w["w_branch"], s["gates"], dmerged)
    (dy,) = _mm("branch_bwd", dz, w["w_branch"], tb=True)
    (big["w_branch"],) = _mm("dw_branch", s["y"], dz, ta=True, a_cols=MIX, out_dtypes=(BF16,))

    proj = s["proj"]
    params = _branch_params(w)
    grads = _rowwise(
        "branches_bwd", _branches_bwd_fn,
        [(s["ca"], MIX, 0), (s["pooled"], MIX, 0), (s["cc"], MIX, 0), (proj, MIX, 3), (proj, MIX, 6), (proj, MIX, 7)]
        + [(dy, MIX, 0, g) for g in range(4)], params,
        [(MIX, F32), (MIX, F32), (MIX, F32), (MIX, BF16), (2 * MIX, BF16, dproj, 3)], [p.shape for p in params], tm=BRANCH_ROWS)
    dca, dpooled, dcc, dbg, dproj = grads[:5]
    pg_ = grads[5:]
    small["conf_dw_b"], small["conf_ln_g"], small["conf_ln_b"], small["pool_scale"] = pg_[0], pg_[1], pg_[2], pg_[3]
    small["gmlp_ln_g"], small["gmlp_ln_b"] = pg_[4], pg_[5]
    small["pool_w"] = jnp.stack(pg_[6:10])
    small["gmlp_ws"] = jnp.stack(pg_[10:14])
    small["gmlp_bs"] = jnp.stack([b.reshape(GROUP) for b in pg_[14:18]])

    dga, dpin, dchx, small["conf_dw"], small["sc_conv"] = _seq_bwd(dca, dpooled, dcc, s["ga"], s["chx"], w["conf_dw"], w["sc_conv"])
    (dproj,) = _rowwise("glu_bwd", _glu_bwd_fn,
                        [(proj, MIX, 0), (proj, MIX, 1), (proj, MIX, 4), (proj, MIX, 5), (dga, MIX, 0), (dchx, MIX, 0),
                         (dpin, MIX, 0), (dbg, MIX, 0)], [], [(6 * MIX, BF16, dproj, 0)])
    (big["w_in"],) = _mm2("dw_in", s["h1"], dproj, ta=True, out_dtypes=(BF16,))
    sent = send(1, [big[n] for n in GRAD_GROUPS[1]])
    dx0, dg = _mm2("proj_bwd", dproj, w["w_in"], tb=True, b_by_chip=True, res=[(s["x0"], 0), (dx1, 0)],
                   rows=[_row(w["norm_mix"]) + sent], epi=_rms_bwd_epilogue, n_row_outs=1)
    small["norm_mix"] = _sum_row_out(dg)
    return dx0, small


def _position():
    return lax.axis_index("x"), lax.axis_index("y"), lax.axis_index("c")


def _other_chips(x, y):
    return [(1 - x, y), (x, 1 - y), (1 - x, 1 - y)]


def _region(ref, cuts):
    return ref.at[tuple(pl.ds(*cuts[a]) if a in cuts else slice(None) for a in range(len(ref.shape)))]


def _allgather_small(name, block):
    m_per, n = block.shape

    def body(x_ref, out_ref, send_sems, recv_sems, local_sem):
        x, y, c = _position()
        me, sibling = (x, y, c), (x, y, 1 - c)
        chips = _other_chips(x, y)

        def rows(px, py, pc):
            return out_ref.at[pl.ds((4 * px + 2 * py + pc) * m_per, m_per), :]

        def copy(k, blk, to, src=None):
            return pltpu.make_async_remote_copy(
                src_ref=rows(*blk) if src is None else src, dst_ref=rows(*blk), send_sem=send_sems.at[k],
                recv_sem=recv_sems.at[k], device_id=to, device_id_type=pl.DeviceIdType.MESH)

        mine = pltpu.make_async_copy(x_ref, rows(*me), local_sem)
        mine.start()
        first = [copy(0, me, sibling, src=x_ref)]
        first += [copy(1 + j, me, (*chip, c), src=x_ref) for j, chip in enumerate(chips)]
        for cp in first:
            cp.start()
        passed = [copy(4 + j, (*chip, c), sibling) for j, chip in enumerate(chips)]
        for j, chip in enumerate(chips):
            copy(1 + j, (*chip, c), me).wait_recv()
            passed[j].start()
        copy(0, sibling, me).wait_recv()
        for j, chip in enumerate(chips):
            copy(4 + j, (*chip, 1 - c), me).wait_recv()
        for cp in first + passed:
            cp.wait_send()
        mine.wait()

    return pl.pallas_call(
        body, name=name, out_shape=jax.ShapeDtypeStruct((8 * m_per, n), block.dtype), in_specs=[ANY], out_specs=ANY,
        scratch_shapes=[pltpu.SemaphoreType.DMA((7,)), pltpu.SemaphoreType.DMA((7,)), pltpu.SemaphoreType.DMA],
    )(block)


def _cuts(arr_shape, shard_axis, half_axis, shard=None, half=None):
    cuts = {}
    if shard is not None:
        n = arr_shape[shard_axis] // 4
        cuts[shard_axis] = (shard * n, n)
    if half is not None:
        n = arr_shape[half_axis] // 2
        cuts[half_axis] = (half * n, n)
    return cuts


HBM =pl.BlockSpec(memory_space=pltpu.HBM)
SEM = pl.BlockSpec(memory_space=pltpu.SEMAPHORE)
DATAFLOW_EFFECT = pltpu.SideEffectType.DATAFLOW_SIDE_EFFECTING
TOKEN_SHAPE = (8, 128)


def _remote(src, dst, send_sems, recv_sems, k, peer):
    return pltpu.make_async_remote_copy(src_ref=src, dst_ref=dst, send_sem=send_sems.at[k], recv_sem=recv_sems.at[k],
                                        device_id=peer, device_id_type=pl.DeviceIdType.MESH)


def _split_start(name, sources, landings, plan):
    n_s, n_l = len(sources), len(landings)
    n_copies = len(plan([None] * n_s, [None] * n_l, None))

    def body(*refs):
        src, land = refs[:n_s], refs[n_s:n_s + n_l]
        send_sems, recv_sems, token = refs[n_s + n_l], refs[n_s + n_l + 1], refs[-1]
        for k, (s, d, peer) in enumerate(plan(src, land, _position())):
            _remote(s, d, send_sems, recv_sems, k, peer).start()
        token[...] = jnp.zeros_like(token)

    arrays = [pltpu.with_memory_space_constraint(a, pltpu.HBM) for a in (*sources, *landings)]
    outs = pl.pallas_call(
        body, name=name,
        out_shape=(pltpu.SemaphoreType.DMA((n_copies,)), pltpu.SemaphoreType.DMA((n_copies,)),
                   *[pltpu.HBM(a.shape, a.dtype) for a in arrays], jax.ShapeDtypeStruct(TOKEN_SHAPE, F32)),
        in_specs=[HBM] * (n_s + n_l),
        out_specs=(SEM, SEM, *[HBM] * (n_s + n_l), pl.BlockSpec(memory_space=pltpu.VMEM)),
        input_output_aliases={i: 2 + i for i in range(n_s + n_l)},
        compiler_params=pltpu.CompilerParams(has_side_effects=DATAFLOW_EFFECT),
    )(*arrays)
    return outs[0], outs[1], list(outs[2:2 + n_s]), list(outs[2 + n_s:2 + n_s + n_l]), outs[-1]


def _split_wait(name, started, after, plan):
    send_sems, recv_sems, sources, landings, _ = started
    n_s, n_l = len(sources), len(landings)

    def body(*refs):
        src, land = refs[:n_s], refs[n_s:n_s + n_l]
        send_sems_ref, recv_sems_ref = refs[n_s + n_l], refs[n_s + n_l + 1]
        for k, (s, d, peer) in enumerate(plan(src, land, _position())):
            cp = _remote(s, d, send_sems_ref, recv_sems_ref, k, peer)
            cp.wait_send()
            cp.wait_recv()

    outs = pl.pallas_call(
        body, name=name, out_shape=tuple(pltpu.HBM(a.shape, a.dtype) for a in (*sources, *landings)),
        in_specs=[*[HBM] * (n_s + n_l), SEM, SEM, ANY], out_specs=tuple([HBM] * (n_s + n_l)),
        input_output_aliases={i: i for i in range(n_s + n_l)},
        compiler_params=pltpu.CompilerParams(has_side_effects=DATAFLOW_EFFECT),
    )(*sources, *landings, send_sems, recv_sems, after)
    return list(outs[:n_s]), list(outs[n_s:])


def _exchange(name, inputs, out_shapes, plan, in_place=False):
    n_in = len(inputs)
    n_out = n_in if in_place else len(out_shapes)
    n_copies = len(plan([None] * n_in, [None] * n_out, None))

    def body(*refs):
        ins, outs = refs[:n_in], refs[n_in:n_in + n_out]
        send_sems, recv_sems = refs[n_in + n_out:]
        copies = [_remote(s, d, send_sems, recv_sems, k, peer)
                  for k, (s, d, peer) in enumerate(plan(ins, ins if in_place else outs, _position()))]
        for cp in copies:
            cp.start()
        for cp in copies:
            cp.wait()

    if in_place:
        out_shapes = [jax.ShapeDtypeStruct(a.shape, a.dtype) for a in inputs]
    return pl.pallas_call(
        body, name=name, in_specs=[ANY] * n_in, out_specs=[ANY] * n_out, out_shape=out_shapes,
        input_output_aliases={i: i for i in range(n_in)} if in_place else {},
        scratch_shapes=[pltpu.SemaphoreType.DMA((n_copies,)), pltpu.SemaphoreType.DMA((n_copies,))],
    )(*inputs)


def _half_rows(ref, half):
    n = ref.shape[0] // 2
    return ref.at[pl.ds(half * n, n)]


def _gather_plan(sources, landings, pos):
    if pos is None:
        return [None] * (3 * len(sources))
    x, y, c = pos
    copies = []
    for src, land in zip(sources, landings):
        for qx, qy in _other_chips(x, y):
            copies.append((_half_rows(src, c), _half_rows(land.at[2 * x + y], c), (qx, qy, c)))
    return copies


def _sibling_plan(landings, _, pos):
    if pos is None:
        return [None] * (3 * len(landings))
    x, y, c = pos
    copies = []
    for land in landings:
        for qx, qy in _other_chips(x, y):
            region = _half_rows(land.at[2 * qx + qy], c)
            copies.append((region, region, (x, y, 1 - c)))
    return copies


def _weights_start(tag, shards):
    landings = [lax.empty((4, *s.shape), BF16) for s in shards]
    return _split_start(f"weights_start_{tag}", shards, landings, _gather_plan)


def _weights_finish(tag, names, started, after):
    shards, landings = _split_wait(f"weights_wait_{tag}", started, after, _gather_plan)
    landings = _exchange("weights_to_sibling", landings, None, _sibling_plan, in_place=True)
    my_shard = 2 * lax.axis_index("x") + lax.axis_index("y")
    full = {}
    for name, shard, land in zip(names, shards, landings):
        sa = BIG_AXES[name][0]
        land = lax.dynamic_update_index_in_dim(land, shard, my_shard, axis=0)
        if name in BY_CHIP:
            full[name] = land
            continue
        axis = sa - 1
        moved = jnp.moveaxis(land, 0, axis)
        full[name] = moved.reshape(*moved.shape[:axis], 4 * moved.shape[axis + 1], *moved.shape[axis + 2:])
    return full


def _half_shape(shape, half_axis):
    return tuple(n // 2 if a == half_axis else n for a, n in enumerate(shape))


def _shard_shape(shape, shard_axis):
    return tuple(n // 4 if a == shard_axis else n for a, n in enumerate(shape))


def _flat2d(a):
    return a.reshape(-1, a.shape[-1])


ELEMENTWISE_TILE_ELEMENTS = 1 << 18


def _row_tile(rows, cols):
    tile = 1
    while rows % (2 * tile) == 0 and 2 * tile * cols <= ELEMENTWISE_TILE_ELEMENTS:
        tile *= 2
    assert tile >= 16 or tile == rows, (rows, cols)
    return tile


def _window_of(arr, axis, parts, index):
    flat = _flat2d(arr)
    rows, cols = flat.shape
    if axis == arr.ndim - 1:
        return flat, (rows, cols // parts), (0, index)
    assert axis == 0, axis
    return flat, (rows // parts, cols), (index, 0)


def _window_sum(name, source, window, index, others, out_dtype):
    w_rows, w_cols = window
    tm = _row_tile(w_rows, w_cols)
    blocks = w_rows // tm
    where = jnp.stack([jnp.asarray(i, jnp.int32) for i in index])

    def body(where_ref, src_ref, *refs):
        acc = src_ref[...].astype(F32)
        for r in refs[:-1]:
            acc = acc + r[...].astype(F32)
        refs[-1][...] = acc.astype(refs[-1].dtype)

    same = pl.BlockSpec((tm, w_cols), lambda i, where_ref: (i, 0))
    grid_spec = pltpu.PrefetchScalarGridSpec(
        num_scalar_prefetch=1, grid=(blocks,),
        in_specs=[pl.BlockSpec((tm, w_cols), lambda i, where_ref: (where_ref[0] * blocks + i, where_ref[1])), *[same] * len(others)],
        out_specs=same)
    return pl.pallas_call(body, name=name, grid_spec=grid_spec, out_shape=jax.ShapeDtypeStruct(window, out_dtype),
                          compiler_params=_params("arbitrary"))(where, source, *others)


def _grads_start(tag, names, grads):
    c = lax.axis_index("c")
    shapes = [g.shape for g in grads]
    axes = [(BIG_AXES[n][0] - 1, BIG_AXES[n][1] - 1) for n in names]

    def to_sibling(ins, outs, pos):
        if pos is None:
            return [None] * len(names)
        px, py, pc = pos
        return [(_region(ins[wi], _cuts(shapes[wi], sa, ha, half=1 - pc)), outs[wi], (px, py, 1 - pc))
                for wi, (sa, ha) in enumerate(axes)]

    landed = _exchange("grads_to_sibling", grads, [jax.ShapeDtypeStruct(_half_shape(s, ha), BF16) for s, (_, ha) in zip(shapes, axes)],
                       to_sibling)
    chip_sums = []
    for wi, name in enumerate(names):
        source, window, index = _window_of(grads[wi], axes[wi][1], 2, c)
        chip_sum = _window_sum("chip_sum_" + name, source, window, index, [_flat2d(landed[wi])], BF16)
        chip_sums.append(chip_sum.reshape(landed[wi].shape))

    def to_chips(sources, landings, pos):
        if pos is None:
            return [None] * (3 * len(names))
        px, py, pc = pos
        copies = []
        for wi, (sa, ha) in enumerate(axes):
            for j, (qx, qy) in enumerate(_other_chips(px, py)):
                piece = _region(sources[wi], _cuts(shapes[wi], sa, ha, shard=2 * qx + qy))
                copies.append((piece, landings[wi].at[j], (qx, qy, pc)))
        return copies

    landings = [lax.empty((3, *_shard_shape(t.shape, sa)), BF16) for t, (sa, _) in zip(chip_sums, axes)]
    return _split_start(f"grads_start_{tag}", chip_sums, landings, to_chips), to_chips


def _grads_finish(tag, names, started, after):
    chip_sums, pieces = _split_wait(f"grads_wait_{tag}", started[0], after, started[1])
    my_shard = 2 * lax.axis_index("x") + lax.axis_index("y")
    reduced = {}
    for name, chip_sum, piece in zip(names, chip_sums, pieces):
        source, window, index = _window_of(chip_sum, BIG_AXES[name][0] - 1, 4, my_shard)
        total = _window_sum("shard_sum_" + name, source, window, index, [_flat2d(piece[j]) for j in range(3)], F32)
        reduced[name] = total.reshape(piece.shape[1:])
    return reduced


def _join_halves(halves, names):
    c = lax.axis_index("c")
    mine = [halves[l][name] for l in range(len(halves)) for name in names]

    def halves_to_sibling(ins, outs, pos):
        if pos is None:
            return [None] * len(mine)
        px, py, pc = pos
        return [(ins[k], outs[k], (px, py, 1 - pc)) for k in range(len(mine))]

    theirs = _exchange("grad_halves_to_sibling", mine, [jax.ShapeDtypeStruct(r.shape, F32) for r in mine], halves_to_sibling)
    out = []
    for wi, name in enumerate(names):
        ha = BIG_AXES[name][1]
        layers = []
        for l in range(len(halves)):
            a, b = mine[l * len(names) + wi], theirs[l * len(names) + wi]
            layers.append(jnp.concatenate([jnp.where(c == 0, a, b), jnp.where(c == 0, b, a)], axis=ha - 1))
        out.append(jnp.stack(layers))
    return out


def _pack(arrays):
    flat = jnp.concatenate([a.reshape(-1).astype(F32) for a in arrays])
    rows = -(-flat.shape[0] // PACK_LANES)
    rows = -(-rows // PACK_ROW_MULTIPLE) * PACK_ROW_MULTIPLE
    return jnp.pad(flat, (0, rows * PACK_LANES - flat.shape[0])).reshape(rows, PACK_LANES)


def _unpack(buf, shapes):
    flat, out, off = buf.reshape(-1), [], 0
    for s in shapes:
        n = 1
        for d in s:
            n *= d
        out.append(flat[off:off + n].reshape(s))
        off += n
    return out


def _sum8(name, gathered, rows):
    def body(in_ref, out_ref):
        acc = in_ref[0].astype(F32)
        for d in range(1, 8):
            acc = acc + in_ref[d].astype(F32)
        out_ref[...] = acc

    return pl.pallas_call(
        body, name=name, grid=(rows // PACK_ROW_MULTIPLE,),
        in_specs=[pl.BlockSpec((8, PACK_ROW_MULTIPLE, PACK_LANES), lambda i: (0, i, 0))],
        out_specs=pl.BlockSpec((PACK_ROW_MULTIPLE, PACK_LANES), lambda i: (i, 0)),
        out_shape=jax.ShapeDtypeStruct((rows, PACK_LANES), F32), compiler_params=_params("arbitrary"),
    )(gathered.reshape(8, rows, PACK_LANES))


def _adamw(name, w, g, m, v):
    shape = w.shape
    flat = [_flat2d(a) for a in (w, g, m, v)]
    rows, cols = flat[0].shape
    tm = _row_tile(rows, cols)
    outs = _rowwise(name, _adamw_fn, [(a, cols, 0) for a in flat], [], [(cols, F32)] * 3, tm=tm)
    return [o.reshape(shape) for o in outs]


WEIGHT_ORDER = ("norm_mix", "w_in", "conf_dw", "conf_dw_b", "conf_ln_g", "conf_ln_b", "pool_w", "pool_scale", "sc_conv",
                "gmlp_ln_g", "gmlp_ln_b", "gmlp_ws", "gmlp_bs", "w_branch", "w_out", "norm_mlp", "w_up", "w_down",
                "norm_ple", "w_ple", "w_ple_gate", "norm_final")


def _local_step(x, p, loss_target, weights, layer_big, full_small_sharded, on_big_grads):
    n_layers = p.shape[0]
    d_model = x.shape[1]
    layer_w, saved = [], []
    h = x
    for l in range(n_layers):
        w = {n: weights[n][l] for n in SMALL_REP if n != "norm_final"}
        w.update({n: full_small_sharded[n][l] for n in SMALL_SHARDED})
        layer_w.append(w)
        h, s = _layer_fwd(h, p[l], w, layer_big(l))
        saved.append(s)
    dx, d_final, loss = _rowwise("loss", _loss_fn, [(h, d_model, 0), (loss_target, d_model, 0)], [_row(weights["norm_final"])],
                                 [(d_model, F32)], [(1, d_model), (1, PACK_LANES)])
    small_grads = {n: [None] * n_layers for n in SMALL_REP + SMALL_SHARDED if n != "norm_final"}
    for l in reversed(range(n_layers)):
        dx, small = _layer_bwd(dx, p[l], layer_w[l], saved[l], on_big_grads(l))
        for n in small:
            small_grads[n][l] = small[n].reshape(weights[n].shape[1:]) if n in SMALL_REP else small[n]
    small_grads = {n: jnp.stack(v) for n, v in small_grads.items()}
    small_grads["norm_final"] = d_final.reshape(-1)
    return loss, dx, small_grads


def kernel(x, p, norm_mix, w_in, conf_dw, conf_dw_b, conf_ln_g, conf_ln_b, pool_w, pool_scale, sc_conv, gmlp_ln_g, gmlp_ln_b, gmlp_ws, gmlp_bs, w_branch, w_out, norm_mlp, w_up, w_down, norm_ple, w_ple, w_ple_gate, norm_final, loss_target, m_norm_mix, m_w_in, m_conf_dw, m_conf_dw_b, m_conf_ln_g, m_conf_ln_b, m_pool_w, m_pool_scale, m_sc_conv, m_gmlp_ln_g, m_gmlp_ln_b, m_gmlp_ws, m_gmlp_bs, m_w_branch, m_w_out, m_norm_mlp, m_w_up, m_w_down, m_norm_ple, m_w_ple, m_w_ple_gate, m_norm_final, v_norm_mix, v_w_in, v_conf_dw, v_conf_dw_b, v_conf_ln_g, v_conf_ln_b, v_pool_w, v_pool_scale, v_sc_conv, v_gmlp_ln_g, v_gmlp_ln_b, v_gmlp_ws, v_gmlp_bs, v_w_branch, v_w_out, v_norm_mlp, v_w_up, v_w_down, v_norm_ple, v_w_ple, v_w_ple_gate, v_norm_final):
    given = dict(locals())
    weights = {n: given[n] for n in WEIGHT_ORDER}
    mom_m = {n: given["m_" + n] for n in WEIGHT_ORDER}
    mom_v = {n: given["v_" + n] for n in WEIGHT_ORDER}
    my_shard = 2 * lax.axis_index("x") + lax.axis_index("y")

    n_layers = p.shape[0]
    shard_shapes = [weights[n].shape for n in SMALL_SHARDED]
    gathered = _allgather_small("allgather_small_weights", _pack([weights[n] for n in SMALL_SHARDED]))
    after_small = gathered[0, 0] * 0.0
    after_small = jnp.where(after_small == 0.0, after_small, 0.0)

    weight_copies = [[None] * len(WEIGHT_GROUPS) for _ in range(n_layers)]
    started = after_small
    for l in range(n_layers):
        for g, names in enumerate(WEIGHT_GROUPS):
            shards = [(weights[n][l] + started if i == 0 else weights[n][l]).astype(BF16) for i, n in enumerate(names)]
            weight_copies[l][g] = _weights_start(f"{l}_{g}", shards)
            started = weight_copies[l][g][4][0, 0]
    per_dev = gathered.reshape(8, -1, PACK_LANES)
    full_small_sharded = {}
    chip_parts = [_unpack(per_dev[2 * s], shard_shapes) for s in range(4)]
    for i, n in enumerate(SMALL_SHARDED):
        full_small_sharded[n] = jnp.concatenate([chip_parts[s][i] for s in range(4)], axis=-1)

    grad_copies = [[None] * len(GRAD_GROUPS) for _ in range(n_layers)]

    def layer_big(l):
        return lambda g, after: _weights_finish(f"{l}_{g}", WEIGHT_GROUPS[g], weight_copies[l][g], after)

    def on_big_grads(l):
        def send(g, group_grads):
            grad_copies[l][g] = _grads_start(f"{l}_{g}", GRAD_GROUPS[g], group_grads)
            return grad_copies[l][g][0][4][0, 0]
        return send

    loss_row, dx, small_grads = _local_step(x[0] + started, p[:, 0], loss_target[0], weights, layer_big, full_small_sharded,
                                            on_big_grads)
    loss = lax.psum(loss_row[0, 0], MESH_AXES)

    grads, delta, new_m, new_v = {}, {}, {}, {}
    halves = [{} for _ in range(n_layers)]
    last_sent = (0, len(GRAD_GROUPS) - 1)
    for l in reversed(range(n_layers)):
        for g, names in enumerate(GRAD_GROUPS):
            if (l, g) != last_sent:
                halves[l].update(_grads_finish(f"{l}_{g}", names, grad_copies[l][g], dx))
    behind = dx
    for g, names in enumerate(GRAD_GROUPS):
        if g == last_sent[1]:
            halves[0].update(_grads_finish(f"0_{g}", names, grad_copies[0][g], behind))
        for n, joined in zip(names, _join_halves(halves, names)):
            grads[n] = joined
            delta[n], new_m[n], new_v[n] = _adamw("adamw_" + n, weights[n], joined, mom_m[n], mom_v[n])
            behind = delta[n]

    small_names = SMALL_REP + SMALL_SHARDED
    after_big = behind[(0,) * behind.ndim] * 0.0
    after_big = jnp.where(after_big == 0.0, after_big, 0.0)
    packed = _pack([small_grads[n] for n in small_names]) + after_big
    rows = packed.shape[0]
    summed = _sum8("sum_small_grads", _allgather_small("allgather_small_grads", packed.astype(BF16)), rows)
    small_full_shapes = [weights[n].shape for n in SMALL_REP] + [full_small_sharded[n].shape for n in SMALL_SHARDED]
    for n, g in zip(small_names, _unpack(summed, small_full_shapes)):
        if n in SMALL_SHARDED:
            width = weights[n].shape[-1]
            g = lax.dynamic_slice_in_dim(g, my_shard * width, width, axis=g.ndim - 1)
        grads[n] = g

    for group, names in (("rep", SMALL_REP), ("sharded", SMALL_SHARDED)):
        outs = _adamw("adamw_small_" + group, *[_pack([src[n] for n in names]) for src in (weights, grads, mom_m, mom_v)])
        shapes = [weights[n].shape for n in names]
        for dst, buf in zip((delta, new_m, new_v), outs):
            for n, a in zip(names, _unpack(buf, shapes)):
                dst[n] = a

    return (loss, dx[None], *[grads[n] for n in WEIGHT_ORDER], *[delta[n] for n in WEIGHT_ORDER],
            *[new_m[n] for n in WEIGHT_ORDER], *[new_v[n] for n in WEIGHT_ORDER])
```

```python
import jax
import jax.numpy as jnp
from jax import lax
from jax.experimental import pallas as pl
from jax.experimental.pallas import tpu as pltpu

F32, BF16 = jnp.float32, jnp.bfloat16
EPS = 1e-6
ADAM_LR, ADAM_B1, ADAM_B2, ADAM_EPS, ADAM_WD, ADAM_STEP = 0.001, 0.9, 0.999, 1e-08, 0.01, 10
CONF_KERNEL, SC_KERNEL = 31, 3
POOL_WINDOWS = (2, 4, 8, 16)
MIX = 512
GROUP = 128
HALO = 32
SEQ_TILE = 256
BRANCH_ROWS = 512
VMEM_LIMIT_BYTES = 48 * 1024 * 1024
MESH_AXES = ("x", "y", "c")
ANY = pl.BlockSpec(memory_space=pl.ANY)

BIG = (("w_in", 2, 1), ("w_branch", 3, 1), ("w_out", 1, 2), ("w_up", 2, 1), ("w_down", 1, 2),
       ("w_ple", 2, 1), ("w_ple_gate", 1, 2))
SMALL_REP = ("norm_mix", "conf_dw_b", "conf_ln_g", "conf_ln_b", "pool_w", "pool_scale", "gmlp_ln_g",
             "gmlp_ln_b", "gmlp_ws", "gmlp_bs", "norm_mlp", "norm_ple", "norm_final")
SMALL_SHARDED = ("conf_dw", "sc_conv")
BY_CHIP = ("w_in", "w_up")
BIG_AXES = {name: (shard_axis, half_axis) for name, shard_axis, half_axis in BIG}
WEIGHT_GROUPS = (("w_in",), ("w_branch", "w_out", "w_up", "w_down", "w_ple_gate", "w_ple"))
GRAD_GROUPS = (("w_ple", "w_ple_gate", "w_down", "w_up"), ("w_out", "w_branch", "w_in"))
PACK_LANES = 128
PACK_ROW_MULTIPLE = 256


def _params(*semantics):
    return pltpu.CompilerParams(dimension_semantics=semantics, vmem_limit_bytes=VMEM_LIMIT_BYTES)


def _accumulate(ref, value, first):
    @pl.when(first)
    def _():
        ref[...] = value.astype(ref.dtype)

    @pl.when(jnp.logical_not(first))
    def _():
        ref[...] += value.astype(ref.dtype)


def _rowwise(name, fn, rows, params, outs, accs=(), tm=256):
    n_rows = rows[0][0].shape[-2]
    tm = min(tm, n_rows)
    n_in, n_out = len(rows) + len(params), len(outs)
    targets = [(k, o[2]) for k, o in enumerate(outs) if len(o) == 4]

    def body(*refs):
        res = fn(*[r[...] for r in refs[:n_in]])
        res = res if isinstance(res, (tuple, list)) else (res,)
        out_refs = refs[n_in + len(targets):]
        for r, v in zip(out_refs[:n_out], res[:n_out]):
            r[...] = v.astype(r.dtype)
        first = pl.program_id(0) == 0
        for r, v in zip(out_refs[n_out:], res[n_out:]):
            _accumulate(r, v, first)

    in_specs = []
    for row in rows:
        arr, w, cb = row[:3]
        if len(row) == 4:
            in_specs.append(pl.BlockSpec((None, tm, w), lambda i, cb=cb, g=row[3]: (g, i, cb)))
        else:
            in_specs.append(pl.BlockSpec((tm, w), lambda i, cb=cb: (i, cb)))
    for p in params:
        in_specs.append(pl.BlockSpec(p.shape, lambda i, nd=p.ndim: (0,) * nd))
    in_specs += [ANY] * len(targets)
    out_specs = [pl.BlockSpec((tm, o[0]), lambda i, cb=(o[3] if len(o) == 4 else 0): (i, cb)) for o in outs]
    out_specs += [pl.BlockSpec(s, lambda i, nd=len(s): (0,) * nd) for s in accs]
    out_shape = [jax.ShapeDtypeStruct(o[2].shape if len(o) == 4 else (n_rows, o[0]), o[1]) for o in outs]
    out_shape += [jax.ShapeDtypeStruct(s, F32) for s in accs]
    return pl.pallas_call(
        body, name=name, grid=(n_rows // tm,), in_specs=in_specs, out_specs=out_specs, out_shape=out_shape,
        input_output_aliases={n_in + t: k for t, (k, _) in enumerate(targets)},
        compiler_params=_params("arbitrary"),
    )(*[r[0] for r in rows], *params, *[t for _, t in targets])


ROW_OUT_SUBLANES = 8


def _mm(name, a, b, *, ta=False, tb=False, a_cols=0, b_by_chip=False, chips=(0, 4), res=(), rows=(), epi=None, out_dtypes=(F32,), n_row_outs=0,
        tm=1024, tn=1024, tk=1024):
    if a_cols:
        assert ta
        g_, m_, k_ = a.shape[1] // a_cols, a_cols, a.shape[0]
    else:
        g_, m_, k_ = (a.shape[0], a.shape[2], a.shape[1]) if ta else a.shape
    if b_by_chip:
        n_ = b.shape[2] if tb else chips[1] * b.shape[3]
    else:
        n_ = b.shape[1] if tb else b.shape[2]
    if ta:
        tk = 2 * tk
    tm, tn, tk = min(tm, m_), min(tn, n_), min(tk, k_)
    if b_by_chip:
        tn, tk = (tn, min(tk, b.shape[3])) if tb else (min(tn, b.shape[3]), tk)
        per_chip = b.shape[3] // (tk if tb else tn)
    nk, n_res, n_rows, n_out = k_ // tk, len(res), len(rows), len(out_dtypes)
    n_in = 2 + n_res + n_rows
    dims = (((0 if ta else 1,), (1 if tb else 0,)), ((), ()))
    if epi is None:
        epi = lambda acc: (acc,)

    def body(*refs):
        a_ref, b_ref = refs[:2]
        res_refs, out_refs, acc_ref = refs[2:n_in], refs[n_in:n_in + n_out], refs[-1]
        row_out_refs = refs[n_in + n_out:-1]
        k = pl.program_id(3)

        def product():
            return lax.dot_general(a_ref[...].astype(BF16), b_ref[...].astype(BF16), dims, preferred_element_type=F32)

        def finish(acc):
            vals = epi(acc, *[r[...] for r in res_refs])
            for o, v in zip(out_refs, vals):
                o[...] = v.astype(o.dtype)
            for o, v in zip(row_out_refs, vals[n_out:]):
                o[...] = jnp.broadcast_to(v, o.shape)

        if nk == 1:
            finish(product())
            return

        @pl.when(k == 0)
        def _():
            acc_ref[...] = jnp.zeros_like(acc_ref)

        acc_ref[...] += product()

        @pl.when(k == nk - 1)
        def _():
            finish(acc_ref[...])

    if a_cols:
        a_spec = pl.BlockSpec((tk, tm), lambda g, i, j, k: (k, g * (m_ // tm) + i))
    elif ta:
        a_spec = pl.BlockSpec((None, tk, tm), lambda g, i, j, k: (g, k, i))
    else:
        a_spec = pl.BlockSpec((None, tm, tk), lambda g, i, j, k: (g, i, k))
    if b_by_chip and tb:
        b_spec = pl.BlockSpec((None, None, tn, tk), lambda g, i, j, k: (g, k // per_chip, j, k % per_chip))
    elif b_by_chip:
        b_spec = pl.BlockSpec((None, None, tk, tn), lambda g, i, j, k: (g, chips[0] + j // per_chip, k, j % per_chip))
    elif tb:
        b_spec = pl.BlockSpec((None, tn, tk), lambda g, i, j, k: (g, j, k))
    else:
        b_spec = pl.BlockSpec((None, tk, tn), lambda g, i, j, k: (g, k, j))
    res_specs = [pl.BlockSpec((None, tm, tn), lambda g, i, j, k, off=off: (g, i, j + off)) for _, off in res]
    row_specs = [pl.BlockSpec((None, 1, tn), lambda g, i, j, k: (g, 0, j))] * n_rows
    out_spec = pl.BlockSpec((None, tm, tn), lambda g, i, j, k: (g, i, j))
    row_out_spec = pl.BlockSpec((None, ROW_OUT_SUBLANES, tn), lambda g, i, j, k: (g, i, j))
    outs = pl.pallas_call(
        body, name=name, grid=(g_, m_ // tm, n_ // tn, nk),
        in_specs=[a_spec, b_spec, *res_specs, *row_specs], out_specs=[out_spec] * n_out + [row_out_spec] * n_row_outs,
        out_shape=[jax.ShapeDtypeStruct((g_, m_, n_), d) for d in out_dtypes]
        + [jax.ShapeDtypeStruct((g_, ROW_OUT_SUBLANES * (m_ // tm), n_), F32)] * n_row_outs,
        scratch_shapes=[pltpu.VMEM((tm, tn), F32)],
        compiler_params=_params("parallel", "parallel", "parallel", "arbitrary"),
    )(a, b, *[r for r, _ in res], *rows)
    return outs


def _mm2(name, a, b, *, res=(), rows=(), **kw):
    outs = _mm(name, a[None], b[None], res=[(r[None], off) for r, off in res], rows=[r[None] for r in rows], **kw)
    return [o[0] for o in outs]


def _sum_row_out(row_out):
    return jnp.sum(row_out[::ROW_OUT_SUBLANES], axis=0, keepdims=True)


@jax.custom_vjp
def _bdot(a, b):
    return jnp.dot(a.astype(BF16), b.astype(BF16), preferred_element_type=F32)


def _bdot_fwd(a, b):
    return _bdot(a, b), (a, b)


def _bdot_bwd(saved, g):
    a, b = saved
    gb = g.astype(BF16)
    da = lax.dot_general(gb, b.astype(BF16), (((1,), (1,)), ((), ())), preferred_element_type=F32)
    db = lax.dot_general(a.astype(BF16), gb, (((0,), (0,)), ((), ())), preferred_element_type=F32)
    return da.astype(a.dtype), db.astype(b.dtype)


_bdot.defvjp(_bdot_fwd, _bdot_bwd)


def _rms(x, g):
    return x * lax.rsqrt(jnp.mean(x * x, axis=-1, keepdims=True) + EPS) * g


def _layer_norm(x, g, b):
    mu = jnp.mean(x, axis=-1, keepdims=True)
    var = jnp.mean(jnp.square(x - mu), axis=-1, keepdims=True)
    return (x - mu) * lax.rsqrt(var + EPS) * g + b


def _glu_and_gate_fn(a, a_gate, cg, hx):
    return a * jax.nn.sigmoid(a_gate), cg * hx


def _branches_fn(ca, pooled, cc, bg, u, v, conf_b, conf_g, conf_beta, pool_scale, gm_g, gm_b, *mats):
    pool_w, ws, bs = mats[0:4], mats[4:8], mats[8:12]
    ya = jax.nn.silu(_layer_norm(ca + conf_b, conf_g, conf_beta))
    yb = jnp.concatenate([_bdot(pooled[:, g * GROUP:(g + 1) * GROUP], pool_w[g]) for g in range(4)], axis=1) * pool_scale
    yc = bg * cc
    vln = _layer_norm(v, gm_g, gm_b)
    causal = lax.broadcasted_iota(jnp.int32, (GROUP, GROUP), 0) >= lax.broadcasted_iota(jnp.int32, (GROUP, GROUP), 1)
    chunks = v.shape[0] // GROUP

    def spatial(g):
        vg = vln[:, g * GROUP:(g + 1) * GROUP]
        wide = jnp.concatenate([vg[c * GROUP:(c + 1) * GROUP, :] for c in range(chunks)], axis=1)
        out = _bdot(jnp.where(causal, ws[g], 0.0), wide) + bs[g]
        return jnp.concatenate([out[:, c * GROUP:(c + 1) * GROUP] for c in range(chunks)], axis=0)

    sg = jnp.concatenate([spatial(g) for g in range(4)], axis=1)
    yd = u * sg
    return ya, yb, yc, yd


def _branches_joined_fn(*args):
    return jnp.concatenate(_branches_fn(*args), axis=1)


def _branches_bwd_fn(*args):
    ins, dys = args[:6] + args[10:], args[6:10]
    _, vjp = jax.vjp(_branches_fn, *ins)
    grads = vjp(tuple(dys))
    duv = jnp.concatenate([grads[4], grads[5]], axis=1)
    return (*grads[:4], duv, *grads[6:])


def _glu_bwd_fn(a, a_gate, cg, hx, dga, dchx, dpin, dbg):
    _, vjp = jax.vjp(_glu_and_gate_fn, a, a_gate, cg, hx)
    da, dag, dcg, dhx = vjp((dga, dchx))
    return jnp.concatenate([da, dag, dpin.astype(F32), dbg.astype(F32), dcg, dhx], axis=1)


def _rms_bwd_epilogue(dh, x, dres, g):
    _, vjp = jax.vjp(_rms, x, g)
    dx, dg = vjp(dh)
    return dx + dres, dg


def _residual_then_rms(acc, residual, g):
    x = acc + residual
    return x, _rms(x, g)


def _loss_fn(x, target, g):
    def f(x, g):
        err = jnp.square(_rms(x, g) - target)
        return 0.5 * jnp.sum(jnp.mean(err, axis=-1, keepdims=True), axis=0, keepdims=True)

    loss, vjp = jax.vjp(f, x, g)
    dx, dg = vjp(jnp.ones((1, 1), F32))
    return dx, dg, jnp.broadcast_to(loss, (1, PACK_LANES))


def _adamw_fn(w, g, m, v):
    m = ADAM_B1 * m + (1.0 - ADAM_B1) * g
    v = ADAM_B2 * v + (1.0 - ADAM_B2) * jnp.square(g)
    m_hat = m / (1.0 - ADAM_B1 ** ADAM_STEP)
    v_hat = v / (1.0 - ADAM_B2 ** ADAM_STEP)
    delta = -ADAM_LR * (m_hat / (jnp.sqrt(v_hat) + ADAM_EPS) + ADAM_WD * w)
    return delta, m, v


def _group_pick(j):
    return [(j == g).astype(F32) for g in range(len(POOL_WINDOWS))]


SUBLANES = 8


def _causal_taps(x, taps):
    for r in range(min(SUBLANES, taps)):
        behind = x if r == 0 else pltpu.roll(x, r, 0)
        for q in range((taps - 1 - r) // SUBLANES + 1):
            start = HALO - SUBLANES * q
            yield SUBLANES * q + r, behind[start:start + SEQ_TILE, :]


def _rows_iota(n):
    return lax.broadcasted_iota(jnp.int32, (n, 1), 0)


def _seq_specs(n_time, causal):
    per = SEQ_TILE // HALO
    tile = pl.BlockSpec((SEQ_TILE, GROUP), lambda j, i: (i, j))
    if causal:
        halo = pl.BlockSpec((HALO, GROUP), lambda j, i: (jnp.maximum(i * per - 1, 0), j))
    else:
        halo = pl.BlockSpec((HALO, GROUP), lambda j, i: (jnp.minimum((i + 1) * per, n_time * per - 1), j))
    return tile, halo


def _seq_fwd(ga, proj, chx, conf_dw, sc_conv):
    s_len = ga.shape[0]
    n_time = s_len // SEQ_TILE
    ext = HALO + SEQ_TILE

    def body(ga_ref, ga_h, pin_ref, pin_h, chx_ref, chx_h, wc_ref, ws_ref, ca_ref, po_ref, cc_ref):
        j, i = pl.program_id(0), pl.program_id(1)
        keep = (i > 0).astype(F32)

        def extended(cur, halo):
            return jnp.concatenate([halo[...] * keep, cur[...]], axis=0)

        def conv(x, w_ref, taps):
            acc = None
            for d, rows in _causal_taps(x, taps):
                term = rows * w_ref[pl.ds(taps - 1 - d, 1), :]
                acc = term if acc is None else acc + term
            return acc

        ca_ref[...] = conv(extended(ga_ref, ga_h), wc_ref, CONF_KERNEL)
        cc_ref[...] = conv(extended(chx_ref, chx_h), ws_ref, SC_KERNEL)
        u = extended(pin_ref, pin_h)
        s1 = u + pltpu.roll(u, 1, 0)
        s2 = s1 + pltpu.roll(s1, 2, 0)
        s3 = s2 + pltpu.roll(s2, 4, 0)
        s4 = s3 + pltpu.roll(s3, 8, 0)
        pick = _group_pick(j)
        window_sum = s1 * pick[0] + s2 * pick[1] + s3 * pick[2] + s4 * pick[3]
        window = sum(float(w) * f for w, f in zip(POOL_WINDOWS, pick))
        pos = (_rows_iota(ext) + (i * SEQ_TILE - HALO + 1)).astype(F32)
        count = jnp.maximum(jnp.minimum(pos, window), 1.0)
        po_ref[...] = (window_sum / count - u)[HALO:, :]

    tile, halo = _seq_specs(n_time, causal=True)
    pin_tile = pl.BlockSpec((SEQ_TILE, GROUP), lambda j, i: (i, 2 * (MIX // GROUP) + j))
    pin_halo = pl.BlockSpec((HALO, GROUP), lambda j, i: (jnp.maximum(i * (SEQ_TILE // HALO) - 1, 0), 2 * (MIX // GROUP) + j))
    w_spec = lambda taps: pl.BlockSpec((taps, GROUP), lambda j, i: (0, j))
    return pl.pallas_call(
        body, name="seq_fwd", grid=(MIX // GROUP, n_time),
        in_specs=[tile, halo, pin_tile, pin_halo, tile, halo, w_spec(CONF_KERNEL), w_spec(SC_KERNEL)],
        out_specs=[tile] * 3, out_shape=[jax.ShapeDtypeStruct((s_len, MIX), F32)] * 3,
        compiler_params=_params("parallel", "arbitrary"),
    )(ga, ga, proj, proj, chx, chx, conf_dw, sc_conv)


def _seq_bwd(dca, dpooled, dcc, ga, chx, conf_dw, sc_conv):
    s_len = ga.shape[0]
    n_time = s_len // SEQ_TILE
    ext = HALO + SEQ_TILE

    def body(dca_ref, dca_h, dpo_ref, dpo_h, dcc_ref, dcc_h, ga_ref, ga_h, chx_ref, chx_h, wc_ref, ws_ref,
             dga_ref, dpin_ref, dchx_ref, dwc_ref, dws_ref, acc_c, acc_s):
        j, i = pl.program_id(0), pl.program_id(1)
        keep_prev = (i > 0).astype(F32)
        keep_next = (i < n_time - 1).astype(F32)

        def with_next(cur, halo):
            return jnp.concatenate([cur[...], halo[...] * keep_next], axis=0)

        def with_prev(cur, halo):
            return jnp.concatenate([halo[...] * keep_prev, cur[...]], axis=0)

        def conv_t(dy, w_ref, taps):
            acc = None
            for r in range(min(SUBLANES, taps)):
                ahead = dy if r == 0 else pltpu.roll(dy, ext - r, 0)
                for q in range((taps - 1 - r) // SUBLANES + 1):
                    term = ahead[SUBLANES * q:SUBLANES * q + SEQ_TILE, :] * w_ref[pl.ds(taps - 1 - (SUBLANES * q + r), 1), :]
                    acc = term if acc is None else acc + term
            return acc

        def tap_grads(dy_cur, x, acc_ref, dw_ref, taps):
            @pl.when(i == 0)
            def _():
                acc_ref[...] = jnp.zeros_like(acc_ref)

            for d, rows in _causal_taps(x, taps):
                prod = dy_cur * rows
                part = prod[:SUBLANES, :]
                for g in range(1, SEQ_TILE // SUBLANES):
                    part = part + prod[g * SUBLANES:(g + 1) * SUBLANES, :]
                acc_ref[pl.ds((taps - 1 - d) * SUBLANES, SUBLANES), :] += part

            @pl.when(i == n_time - 1)
            def _():
                for k in range(taps):
                    dw_ref[pl.ds(k, 1), :] = jnp.sum(acc_ref[pl.ds(k * SUBLANES, SUBLANES), :], axis=0, keepdims=True)

        dga_ref[...] = conv_t(with_next(dca_ref, dca_h), wc_ref, CONF_KERNEL)
        dchx_ref[...] = conv_t(with_next(dcc_ref, dcc_h), ws_ref, SC_KERNEL)
        tap_grads(dca_ref[...], with_prev(ga_ref, ga_h), acc_c, dwc_ref, CONF_KERNEL)
        tap_grads(dcc_ref[...], with_prev(chx_ref, chx_h), acc_s, dws_ref, SC_KERNEL)

        dpo = with_next(dpo_ref, dpo_h)
        pick = _group_pick(j)
        window = sum(float(w) * f for w, f in zip(POOL_WINDOWS, pick))
        pos = (_rows_iota(ext) + (i * SEQ_TILE + 1)).astype(F32)
        r = dpo / jnp.minimum(pos, window)
        q1 = r + pltpu.roll(r, ext - 1, 0)
        q2 = q1 + pltpu.roll(q1, ext - 2, 0)
        q3 = q2 + pltpu.roll(q2, ext - 4, 0)
        q4 = q3 + pltpu.roll(q3, ext - 8, 0)
        q = q1 * pick[0] + q2 * pick[1] + q3 * pick[2] + q4 * pick[3]
        dpin_ref[...] = (q - dpo)[:SEQ_TILE, :].astype(dpin_ref.dtype)

    tile, nxt = _seq_specs(n_time, causal=False)
    _, prv = _seq_specs(n_time, causal=True)
    w_spec = lambda taps: pl.BlockSpec((taps, GROUP), lambda j, i: (0, j))
    return pl.pallas_call(
        body, name="seq_bwd", grid=(MIX // GROUP, n_time),
        in_specs=[tile, nxt, tile, nxt, tile, nxt, tile, prv, tile, prv, w_spec(CONF_KERNEL), w_spec(SC_KERNEL)],
        out_specs=[tile, tile, tile, w_spec(CONF_KERNEL), w_spec(SC_KERNEL)],
        out_shape=[jax.ShapeDtypeStruct((s_len, MIX), F32), jax.ShapeDtypeStruct((s_len, MIX), BF16),
                   jax.ShapeDtypeStruct((s_len, MIX), F32), jax.ShapeDtypeStruct(conf_dw.shape, F32),
                   jax.ShapeDtypeStruct(sc_conv.shape, F32)],
        scratch_shapes=[pltpu.VMEM((CONF_KERNEL * SUBLANES, GROUP), F32), pltpu.VMEM((SC_KERNEL * SUBLANES, GROUP), F32)],
        compiler_params=_params("parallel", "arbitrary"),
    )(dca, dca, dpooled, dpooled, dcc, dcc, ga, ga, chx, chx, conf_dw, sc_conv)


MERGE_TILE = 1024
MERGE_ROWS = 1024


def _merge_fwd(y, w_branch, gates):
    s_len, d_model = y.shape[0], w_branch.shape[2]
    t = min(MERGE_ROWS, s_len)
    per_branch = d_model // MERGE_TILE

    def body(y_ref, w_ref, gate_ref, out_ref, acc_ref):
        k = pl.program_id(2)

        @pl.when(k == 0)
        def _():
            acc_ref[...] = jnp.zeros_like(acc_ref)

        z = jnp.dot(y_ref[...], w_ref[...], preferred_element_type=F32)
        acc_ref[...] += gate_ref[...].astype(F32) * z

        @pl.when(k == 3)
        def _():
            out_ref[...] = acc_ref[...].astype(out_ref.dtype)

    return pl.pallas_call(
        body, name="merge_fwd", grid=(s_len // t, per_branch, 4),
        in_specs=[pl.BlockSpec((t, MIX), lambda i, j, k: (i, k)),
                  pl.BlockSpec((None, MIX, MERGE_TILE), lambda i, j, k: (k, 0, j)),
                  pl.BlockSpec((t, MERGE_TILE), lambda i, j, k: (i, k * per_branch + j))],
        out_specs=pl.BlockSpec((t, MERGE_TILE), lambda i, j, k: (i, j)),
        out_shape=jax.ShapeDtypeStruct((s_len, d_model), BF16),
        scratch_shapes=[pltpu.VMEM((t, MERGE_TILE), F32)],
        compiler_params=_params("parallel", "parallel", "arbitrary"),
    )(y, w_branch, gates)


def _merge_bwd(y, w_branch, gates, dmerged):
    s_len, d_model = y.shape[0], w_branch.shape[2]
    t = min(MERGE_ROWS, s_len)
    per_branch = d_model // MERGE_TILE

    def body(y_ref, w_ref, gate_ref, dm_ref, dz_ref, dgate_ref):
        z = jnp.dot(y_ref[...], w_ref[...], preferred_element_type=F32)
        gate = gate_ref[...].astype(F32)
        dm = dm_ref[...]
        dz_ref[...] = (dm * gate).astype(dz_ref.dtype)
        dgate_ref[...] = (dm * z * gate * (1.0 - gate)).astype(dgate_ref.dtype)

    return pl.pallas_call(
        body, name="merge_bwd", grid=(s_len // t, per_branch, 4),
        in_specs=[pl.BlockSpec((t, MIX), lambda i, j, k: (i, k)),
                  pl.BlockSpec((None, MIX, MERGE_TILE), lambda i, j, k: (k, 0, j)),
                  pl.BlockSpec((t, MERGE_TILE), lambda i, j, k: (i, k * per_branch + j)),
                  pl.BlockSpec((t, MERGE_TILE), lambda i, j, k: (i, j))],
        out_specs=[pl.BlockSpec((None, t, MERGE_TILE), lambda i, j, k: (k, i, j)),
                   pl.BlockSpec((t, MERGE_TILE), lambda i, j, k: (i, (4 + k) * per_branch + j))],
        out_shape=[jax.ShapeDtypeStruct((4, s_len, d_model), BF16), jax.ShapeDtypeStruct((s_len, 8 * d_model), BF16)],
        compiler_params=_params("parallel", "parallel", "arbitrary"),
    )(y, w_branch, gates, dmerged)


def _row(v):
    return v.reshape(1, -1)


def _branch_params(w):
    mats = [w["pool_w"][g] for g in range(4)] + [w["gmlp_ws"][g] for g in range(4)]
    mats += [w["gmlp_bs"][g].reshape(GROUP, 1) for g in range(4)]
    return [_row(w["conf_dw_b"]), _row(w["conf_ln_g"]), _row(w["conf_ln_b"]), _row(w["pool_scale"]),
            _row(w["gmlp_ln_g"]), _row(w["gmlp_ln_b"])] + mats


def _layer_fwd(x0, p_l, w, fetch):
    d_model = x0.shape[1]
    (h1,) = _rowwise("rms_mix", _rms, [(x0, d_model, 0)], [_row(w["norm_mix"])], [(d_model, BF16)])
    w.update(fetch(0, h1))
    (proj,) = _mm2("proj", h1, w["w_in"], b_by_chip=True, chips=(0, 2))
    (gates,) = _mm2("proj_gates", h1, w["w_in"], b_by_chip=True, chips=(2, 2), out_dtypes=(BF16,),
                    epi=lambda acc: (jax.nn.sigmoid(acc),))
    ga, chx = _rowwise("glu", _glu_and_gate_fn, [(proj, MIX, 0), (proj, MIX, 1), (proj, MIX, 4), (proj, MIX, 5)], [],
                       [(MIX, F32), (MIX, F32)])
    ca, pooled, cc = _seq_fwd(ga, proj, chx, w["conf_dw"], w["sc_conv"])
    (y,) = _rowwise("branches", _branches_joined_fn,
                    [(ca, MIX, 0), (pooled, MIX, 0), (cc, MIX, 0), (proj, MIX, 3), (proj, MIX, 6), (proj, MIX, 7)],
                    _branch_params(w), [(4 * MIX, BF16)], tm=BRANCH_ROWS)
    w.update(fetch(1, y))
    merged = _merge_fwd(y, w["w_branch"], gates)
    x1, h2 = _mm2("out_proj", merged, w["w_out"], res=[(x0, 0)], rows=[_row(w["norm_mlp"])], out_dtypes=(F32, BF16),
                  epi=_residual_then_rms)
    (act,) = _mm2("mlp_up", h2, w["w_up"], b_by_chip=True, out_dtypes=(BF16,),
                  epi=lambda acc: (jnp.square(jnp.maximum(acc, 0.0)),))
    x2, h3 = _mm2("mlp_down", act, w["w_down"], res=[(x1, 0)], rows=[_row(w["norm_ple"])], out_dtypes=(F32, BF16),
                  epi=_residual_then_rms)
    (pg,) = _mm2("ple_gate", h3, w["w_ple_gate"])
    (x3,) = _mm2("ple", p_l, w["w_ple"], res=[(pg, 0), (x2, 0)],
                 epi=lambda acc, g, r: (acc * jax.nn.sigmoid(g) + r,))
    saved = dict(x0=x0, h1=h1, proj=proj, gates=gates, ga=ga, chx=chx, ca=ca, pooled=pooled, cc=cc, y=y, merged=merged, x1=x1,
                 h2=h2, act=act, x2=x2, h3=h3, pg=pg)
    return x3, saved


def _layer_bwd(dx3, p_l, w, s, send):
    d_model = dx3.shape[1]
    big, small = {}, {}

    def ple_epi(acc, dx, g):
        sig = jax.nn.sigmoid(g)
        return dx * sig, dx * acc * sig * (1.0 - sig)

    dpp, dpg = _mm2("ple_bwd", p_l, w["w_ple"], res=[(dx3, 0), (s["pg"], 0)], epi=ple_epi, out_dtypes=(BF16, BF16))
    (big["w_ple"],) = _mm2("dw_ple", p_l, dpp, ta=True, out_dtypes=(BF16,))
    (big["w_ple_gate"],) = _mm2("dw_ple_gate", s["h3"], dpg, ta=True, out_dtypes=(BF16,))
    dx2, dg = _mm2("dh_ple", dpg, w["w_ple_gate"], tb=True, res=[(s["x2"], 0), (dx3, 0)], rows=[_row(w["norm_ple"])],
                   epi=_rms_bwd_epilogue, n_row_outs=1)
    small["norm_ple"] = _sum_row_out(dg)

    (dup,) = _mm2("mlp_down_bwd", dx2, w["w_down"], tb=True, res=[(s["act"], 0)], out_dtypes=(BF16,),
                  epi=lambda acc, act: (acc * 2.0 * jnp.sqrt(act.astype(F32)),))
    (big["w_down"],) = _mm2("dw_down", s["act"], dx2, ta=True, out_dtypes=(BF16,))
    (big["w_up"],) = _mm2("dw_up", s["h2"], dup, ta=True, out_dtypes=(BF16,))
    sent = send(0, [big[n] for n in GRAD_GROUPS[0]])
    dx1, dg = _mm2("mlp_up_bwd", dup, w["w_up"], tb=True, b_by_chip=True, res=[(s["x1"], 0), (dx2, 0)],
                   rows=[_row(w["norm_mlp"]) + sent], epi=_rms_bwd_epilogue, n_row_outs=1)
    small["norm_mlp"] = _sum_row_out(dg)

    (dmerged,) = _mm2("out_proj_bwd", dx1, w["w_out"], tb=True)
    (big["w_out"],) = _mm2("dw_out", s["merged"], dx1, ta=True, out_dtypes=(BF16,))
    dz, dproj = _merge_bwd(s["y"], w["w_branch"], s["gates"], dmerged)
    (dy,) = _mm("branch_bwd", dz, w["w_branch"], tb=True)
    (big["w_branch"],) = _mm("dw_branch", s["y"], dz, ta=True, a_cols=MIX, out_dtypes=(BF16,))

    proj = s["proj"]
    params = _branch_params(w)
    grads = _rowwise(
        "branches_bwd", _branches_bwd_fn,
        [(s["ca"], MIX, 0), (s["pooled"], MIX, 0), (s["cc"], MIX, 0), (proj, MIX, 3), (proj, MIX, 6), (proj, MIX, 7)]
        + [(dy, MIX, 0, g) for g in range(4)], params,
        [(MIX, F32), (MIX, F32), (MIX, F32), (MIX, BF16), (2 * MIX, BF16, dproj, 3)], [p.shape for p in params], tm=BRANCH_ROWS)
    dca, dpooled, dcc, dbg, dproj = grads[:5]
    pg_ = grads[5:]
    small["conf_dw_b"], small["conf_ln_g"], small["conf_ln_b"], small["pool_scale"] = pg_[0], pg_[1], pg_[2], pg_[3]
    small["gmlp_ln_g"], small["gmlp_ln_b"] = pg_[4], pg_[5]
    small["pool_w"] = jnp.stack(pg_[6:10])
    small["gmlp_ws"] = jnp.stack(pg_[10:14])
    small["gmlp_bs"] = jnp.stack([b.reshape(GROUP) for b in pg_[14:18]])

    dga, dpin, dchx, small["conf_dw"], small["sc_conv"] = _seq_bwd(dca, dpooled, dcc, s["ga"], s["chx"], w["conf_dw"], w["sc_conv"])
    (dproj,) = _rowwise("glu_bwd", _glu_bwd_fn,
                        [(proj, MIX, 0), (proj, MIX, 1), (proj, MIX, 4), (proj, MIX, 5), (dga, MIX, 0), (dchx, MIX, 0),
                         (dpin, MIX, 0), (dbg, MIX, 0)], [], [(6 * MIX, BF16, dproj, 0)])
    (big["w_in"],) = _mm2("dw_in", s["h1"], dproj, ta=True, out_dtypes=(BF16,))
    sent = send(1, [big[n] for n in GRAD_GROUPS[1]])
    dx0, dg = _mm2("proj_bwd", dproj, w["w_in"], tb=True, b_by_chip=True, res=[(s["x0"], 0), (dx1, 0)],
                   rows=[_row(w["norm_mix"]) + sent], epi=_rms_bwd_epilogue, n_row_outs=1)
    small["norm_mix"] = _sum_row_out(dg)
    return dx0, small


def _position():
    return lax.axis_index("x"), lax.axis_index("y"), lax.axis_index("c")


def _other_chips(x, y):
    return [(1 - x, y), (x, 1 - y), (1 - x, 1 - y)]


def _region(ref, cuts):
    return ref.at[tuple(pl.ds(*cuts[a]) if a in cuts else slice(None) for a in range(len(ref.shape)))]


def _allgather_small(name, block):
    m_per, n = block.shape

    def body(x_ref, out_ref, send_sems, recv_sems, local_sem):
        x, y, c = _position()
        me, sibling = (x, y, c), (x, y, 1 - c)
        chips = _other_chips(x, y)

        def rows(px, py, pc):
            return out_ref.at[pl.ds((4 * px + 2 * py + pc) * m_per, m_per), :]

        def copy(k, blk, to, src=None):
            return pltpu.make_async_remote_copy(
                src_ref=rows(*blk) if src is None else src, dst_ref=rows(*blk), send_sem=send_sems.at[k],
                recv_sem=recv_sems.at[k], device_id=to, device_id_type=pl.DeviceIdType.MESH)

        mine = pltpu.make_async_copy(x_ref, rows(*me), local_sem)
        mine.start()
        first = [copy(0, me, sibling, src=x_ref)]
        first += [copy(1 + j, me, (*chip, c), src=x_ref) for j, chip in enumerate(chips)]
        for cp in first:
            cp.start()
        passed = [copy(4 + j, (*chip, c), sibling) for j, chip in enumerate(chips)]
        for j, chip in enumerate(chips):
            copy(1 + j, (*chip, c), me).wait_recv()
            passed[j].start()
        copy(0, sibling, me).wait_recv()
        for j, chip in enumerate(chips):
            copy(4 + j, (*chip, 1 - c), me).wait_recv()
        for cp in first + passed:
            cp.wait_send()
        mine.wait()

    return pl.pallas_call(
        body, name=name, out_shape=jax.ShapeDtypeStruct((8 * m_per, n), block.dtype), in_specs=[ANY], out_specs=ANY,
        scratch_shapes=[pltpu.SemaphoreType.DMA((7,)), pltpu.SemaphoreType.DMA((7,)), pltpu.SemaphoreType.DMA],
    )(block)


def _cuts(arr_shape, shard_axis, half_axis, shard=None, half=None):
    cuts = {}
    if shard is not None:
        n = arr_shape[shard_axis] // 4
        cuts[shard_axis] = (shard * n, n)
    if half is not None:
        n = arr_shape[half_axis] // 2
        cuts[half_axis] = (half * n, n)
    return cuts


HBM =pl.BlockSpec(memory_space=pltpu.HBM)
SEM = pl.BlockSpec(memory_space=pltpu.SEMAPHORE)
DATAFLOW_EFFECT = pltpu.SideEffectType.DATAFLOW_SIDE_EFFECTING
TOKEN_SHAPE = (8, 128)


def _remote(src, dst, send_sems, recv_sems, k, peer):
    return pltpu.make_async_remote_copy(src_ref=src, dst_ref=dst, send_sem=send_sems.at[k], recv_sem=recv_sems.at[k],
                                        device_id=peer, device_id_type=pl.DeviceIdType.MESH)


def _split_start(name, sources, landings, plan):
    n_s, n_l = len(sources), len(landings)
    n_copies = len(plan([None] * n_s, [None] * n_l, None))

    def body(*refs):
        src, land = refs[:n_s], refs[n_s:n_s + n_l]
        send_sems, recv_sems, token = refs[n_s + n_l], refs[n_s + n_l + 1], refs[-1]
        for k, (s, d, peer) in enumerate(plan(src, land, _position())):
            _remote(s, d, send_sems, recv_sems, k, peer).start()
        token[...] = jnp.zeros_like(token)

    arrays = [pltpu.with_memory_space_constraint(a, pltpu.HBM) for a in (*sources, *landings)]
    outs = pl.pallas_call(
        body, name=name,
        out_shape=(pltpu.SemaphoreType.DMA((n_copies,)), pltpu.SemaphoreType.DMA((n_copies,)),
                   *[pltpu.HBM(a.shape, a.dtype) for a in arrays], jax.ShapeDtypeStruct(TOKEN_SHAPE, F32)),
        in_specs=[HBM] * (n_s + n_l),
        out_specs=(SEM, SEM, *[HBM] * (n_s + n_l), pl.BlockSpec(memory_space=pltpu.VMEM)),
        input_output_aliases={i: 2 + i for i in range(n_s + n_l)},
        compiler_params=pltpu.CompilerParams(has_side_effects=DATAFLOW_EFFECT),
    )(*arrays)
    return outs[0], outs[1], list(outs[2:2 + n_s]), list(outs[2 + n_s:2 + n_s + n_l]), outs[-1]


def _split_wait(name, started, after, plan):
    send_sems, recv_sems, sources, landings, _ = started
    n_s, n_l = len(sources), len(landings)

    def body(*refs):
        src, land = refs[:n_s], refs[n_s:n_s + n_l]
        send_sems_ref, recv_sems_ref = refs[n_s + n_l], refs[n_s + n_l + 1]
        for k, (s, d, peer) in enumerate(plan(src, land, _position())):
            cp = _remote(s, d, send_sems_ref, recv_sems_ref, k, peer)
            cp.wait_send()
            cp.wait_recv()

    outs = pl.pallas_call(
        body, name=name, out_shape=tuple(pltpu.HBM(a.shape, a.dtype) for a in (*sources, *landings)),
        in_specs=[*[HBM] * (n_s + n_l), SEM, SEM, ANY], out_specs=tuple([HBM] * (n_s + n_l)),
        input_output_aliases={i: i for i in range(n_s + n_l)},
        compiler_params=pltpu.CompilerParams(has_side_effects=DATAFLOW_EFFECT),
    )(*sources, *landings, send_sems, recv_sems, after)
    return list(outs[:n_s]), list(outs[n_s:])


def _exchange(name, inputs, out_shapes, plan, in_place=False):
    n_in = len(inputs)
    n_out = n_in if in_place else len(out_shapes)
    n_copies = len(plan([None] * n_in, [None] * n_out, None))

    def body(*refs):
        ins, outs = refs[:n_in], refs[n_in:n_in + n_out]
        send_sems, recv_sems = refs[n_in + n_out:]
        copies = [_remote(s, d, send_sems, recv_sems, k, peer)
                  for k, (s, d, peer) in enumerate(plan(ins, ins if in_place else outs, _position()))]
        for cp in copies:
            cp.start()
        for cp in copies:
            cp.wait()

    if in_place:
        out_shapes = [jax.ShapeDtypeStruct(a.shape, a.dtype) for a in inputs]
    return pl.pallas_call(
        body, name=name, in_specs=[ANY] * n_in, out_specs=[ANY] * n_out, out_shape=out_shapes,
        input_output_aliases={i: i for i in range(n_in)} if in_place else {},
        scratch_shapes=[pltpu.SemaphoreType.DMA((n_copies,)), pltpu.SemaphoreType.DMA((n_copies,))],
    )(*inputs)


def _half_rows(ref, half):
    n = ref.shape[0] // 2
    return ref.at[pl.ds(half * n, n)]


def _gather_plan(sources, landings, pos):
    if pos is None:
        return [None] * (3 * len(sources))
    x, y, c = pos
    copies = []
    for src, land in zip(sources, landings):
        for qx, qy in _other_chips(x, y):
            copies.append((_half_rows(src, c), _half_rows(land.at[2 * x + y], c), (qx, qy, c)))
    return copies


def _sibling_plan(landings, _, pos):
    if pos is None:
        return [None] * (3 * len(landings))
    x, y, c = pos
    copies = []
    for land in landings:
        for qx, qy in _other_chips(x, y):
            region = _half_rows(land.at[2 * qx + qy], c)
            copies.append((region, region, (x, y, 1 - c)))
    return copies


def _weights_start(tag, shards):
    landings = [lax.empty((4, *s.shape), BF16) for s in shards]
    return _split_start(f"weights_start_{tag}", shards, landings, _gather_plan)


def _weights_finish(tag, names, started, after):
    shards, landings = _split_wait(f"weights_wait_{tag}", started, after, _gather_plan)
    landings = _exchange("weights_to_sibling", landings, None, _sibling_plan, in_place=True)
    my_shard = 2 * lax.axis_index("x") + lax.axis_index("y")
    full = {}
    for name, shard, land in zip(names, shards, landings):
        sa = BIG_AXES[name][0]
        land = lax.dynamic_update_index_in_dim(land, shard, my_shard, axis=0)
        if name in BY_CHIP:
            full[name] = land
            continue
        axis = sa - 1
        moved = jnp.moveaxis(land, 0, axis)
        full[name] = moved.reshape(*moved.shape[:axis], 4 * moved.shape[axis + 1], *moved.shape[axis + 2:])
    return full


def _half_shape(shape, half_axis):
    return tuple(n // 2 if a == half_axis else n for a, n in enumerate(shape))


def _shard_shape(shape, shard_axis):
    return tuple(n // 4 if a == shard_axis else n for a, n in enumerate(shape))


def _flat2d(a):
    return a.reshape(-1, a.shape[-1])


ELEMENTWISE_TILE_ELEMENTS = 1 << 18


def _row_tile(rows, cols):
    tile = 1
    while rows % (2 * tile) == 0 and 2 * tile * cols <= ELEMENTWISE_TILE_ELEMENTS:
        tile *= 2
    assert tile >= 16 or tile == rows, (rows, cols)
    return tile


def _window_of(arr, axis, parts, index):
    flat = _flat2d(arr)
    rows, cols = flat.shape
    if axis == arr.ndim - 1:
        return flat, (rows, cols // parts), (0, index)
    assert axis == 0, axis
    return flat, (rows // parts, cols), (index, 0)


def _window_sum(name, source, window, index, others, out_dtype):
    w_rows, w_cols = window
    tm = _row_tile(w_rows, w_cols)
    blocks = w_rows // tm
    where = jnp.stack([jnp.asarray(i, jnp.int32) for i in index])

    def body(where_ref, src_ref, *refs):
        acc = src_ref[...].astype(F32)
        for r in refs[:-1]:
            acc = acc + r[...].astype(F32)
        refs[-1][...] = acc.astype(refs[-1].dtype)

    same = pl.BlockSpec((tm, w_cols), lambda i, where_ref: (i, 0))
    grid_spec = pltpu.PrefetchScalarGridSpec(
        num_scalar_prefetch=1, grid=(blocks,),
        in_specs=[pl.BlockSpec((tm, w_cols), lambda i, where_ref: (where_ref[0] * blocks + i, where_ref[1])), *[same] * len(others)],
        out_specs=same)
    return pl.pallas_call(body, name=name, grid_spec=grid_spec, out_shape=jax.ShapeDtypeStruct(window, out_dtype),
                          compiler_params=_params("arbitrary"))(where, source, *others)


def _grads_start(tag, names, grads):
    c = lax.axis_index("c")
    shapes = [g.shape for g in grads]
    axes = [(BIG_AXES[n][0] - 1, BIG_AXES[n][1] - 1) for n in names]

    def to_sibling(ins, outs, pos):
        if pos is None:
            return [None] * len(names)
        px, py, pc = pos
        return [(_region(ins[wi], _cuts(shapes[wi], sa, ha, half=1 - pc)), outs[wi], (px, py, 1 - pc))
                for wi, (sa, ha) in enumerate(axes)]

    landed = _exchange("grads_to_sibling", grads, [jax.ShapeDtypeStruct(_half_shape(s, ha), BF16) for s, (_, ha) in zip(shapes, axes)],
                       to_sibling)
    chip_sums = []
    for wi, name in enumerate(names):
        source, window, index = _window_of(grads[wi], axes[wi][1], 2, c)
        chip_sum = _window_sum("chip_sum_" + name, source, window, index, [_flat2d(landed[wi])], BF16)
        chip_sums.append(chip_sum.reshape(landed[wi].shape))

    def to_chips(sources, landings, pos):
        if pos is None:
            return [None] * (3 * len(names))
        px, py, pc = pos
        copies = []
        for wi, (sa, ha) in enumerate(axes):
            for j, (qx, qy) in enumerate(_other_chips(px, py)):
                piece = _region(sources[wi], _cuts(shapes[wi], sa, ha, shard=2 * qx + qy))
                copies.append((piece, landings[wi].at[j], (qx, qy, pc)))
        return copies

    landings = [lax.empty((3, *_shard_shape(t.shape, sa)), BF16) for t, (sa, _) in zip(chip_sums, axes)]
    return _split_start(f"grads_start_{tag}", chip_sums, landings, to_chips), to_chips


def _grads_finish(tag, names, started, after):
    chip_sums, pieces = _split_wait(f"grads_wait_{tag}", started[0], after, started[1])
    my_shard = 2 * lax.axis_index("x") + lax.axis_index("y")
    reduced = {}
    for name, chip_sum, piece in zip(names, chip_sums, pieces):
        source, window, index = _window_of(chip_sum, BIG_AXES[name][0] - 1, 4, my_shard)
        total = _window_sum("shard_sum_" + name, source, window, index, [_flat2d(piece[j]) for j in range(3)], F32)
        reduced[name] = total.reshape(piece.shape[1:])
    return reduced


def _join_halves(halves, names):
    c = lax.axis_index("c")
    mine = [halves[l][name] for l in range(len(halves)) for name in names]

    def halves_to_sibling(ins, outs, pos):
        if pos is None:
            return [None] * len(mine)
        px, py, pc = pos
        return [(ins[k], outs[k], (px, py, 1 - pc)) for k in range(len(mine))]

    theirs = _exchange("grad_halves_to_sibling", mine, [jax.ShapeDtypeStruct(r.shape, F32) for r in mine], halves_to_sibling)
    out = []
    for wi, name in enumerate(names):
        ha = BIG_AXES[name][1]
        layers = []
        for l in range(len(halves)):
            a, b = mine[l * len(names) + wi], theirs[l * len(names) + wi]
            layers.append(jnp.concatenate([jnp.where(c == 0, a, b), jnp.where(c == 0, b, a)], axis=ha - 1))
        out.append(jnp.stack(layers))
    return out


def _pack(arrays):
    flat = jnp.concatenate([a.reshape(-1).astype(F32) for a in arrays])
    rows = -(-flat.shape[0] // PACK_LANES)
    rows = -(-rows // PACK_ROW_MULTIPLE) * PACK_ROW_MULTIPLE
    return jnp.pad(flat, (0, rows * PACK_LANES - flat.shape[0])).reshape(rows, PACK_LANES)


def _unpack(buf, shapes):
    flat, out, off = buf.reshape(-1), [], 0
    for s in shapes:
        n = 1
        for d in s:
            n *= d
        out.append(flat[off:off + n].reshape(s))
        off += n
    return out


def _sum8(name, gathered, rows):
    def body(in_ref, out_ref):
        acc = in_ref[0].astype(F32)
        for d in range(1, 8):
            acc = acc + in_ref[d].astype(F32)
        out_ref[...] = acc

    return pl.pallas_call(
        body, name=name, grid=(rows // PACK_ROW_MULTIPLE,),
        in_specs=[pl.BlockSpec((8, PACK_ROW_MULTIPLE, PACK_LANES), lambda i: (0, i, 0))],
        out_specs=pl.BlockSpec((PACK_ROW_MULTIPLE, PACK_LANES), lambda i: (i, 0)),
        out_shape=jax.ShapeDtypeStruct((rows, PACK_LANES), F32), compiler_params=_params("arbitrary"),
    )(gathered.reshape(8, rows, PACK_LANES))


def _adamw(name, w, g, m, v):
    shape = w.shape
    flat = [_flat2d(a) for a in (w, g, m, v)]
    rows, cols = flat[0].shape
    tm = _row_tile(rows, cols)
    outs = _rowwise(name, _adamw_fn, [(a, cols, 0) for a in flat], [], [(cols, F32)] * 3, tm=tm)
    return [o.reshape(shape) for o in outs]


WEIGHT_ORDER = ("norm_mix", "w_in", "conf_dw", "conf_dw_b", "conf_ln_g", "conf_ln_b", "pool_w", "pool_scale", "sc_conv",
                "gmlp_ln_g", "gmlp_ln_b", "gmlp_ws", "gmlp_bs", "w_branch", "w_out", "norm_mlp", "w_up", "w_down",
                "norm_ple", "w_ple", "w_ple_gate", "norm_final")


def _local_step(x, p, loss_target, weights, layer_big, full_small_sharded, on_big_grads):
    n_layers = p.shape[0]
    d_model = x.shape[1]
    layer_w, saved = [], []
    h = x
    for l in range(n_layers):
        w = {n: weights[n][l] for n in SMALL_REP if n != "norm_final"}
        w.update({n: full_small_sharded[n][l] for n in SMALL_SHARDED})
        layer_w.append(w)
        h, s = _layer_fwd(h, p[l], w, layer_big(l))
        saved.append(s)
    dx, d_final, loss = _rowwise("loss", _loss_fn, [(h, d_model, 0), (loss_target, d_model, 0)], [_row(weights["norm_final"])],
                                 [(d_model, F32)], [(1, d_model), (1, PACK_LANES)])
    small_grads = {n: [None] * n_layers for n in SMALL_REP + SMALL_SHARDED if n != "norm_final"}
    for l in reversed(range(n_layers)):
        dx, small = _layer_bwd(dx, p[l], layer_w[l], saved[l], on_big_grads(l))
        for n in small:
            small_grads[n][l] = small[n].reshape(weights[n].shape[1:]) if n in SMALL_REP else small[n]
    small_grads = {n: jnp.stack(v) for n, v in small_grads.items()}
    small_grads["norm_final"] = d_final.reshape(-1)
    return loss, dx, small_grads


def kernel(x, p, norm_mix, w_in, conf_dw, conf_dw_b, conf_ln_g, conf_ln_b, pool_w, pool_scale, sc_conv, gmlp_ln_g, gmlp_ln_b, gmlp_ws, gmlp_bs, w_branch, w_out, norm_mlp, w_up, w_down, norm_ple, w_ple, w_ple_gate, norm_final, loss_target, m_norm_mix, m_w_in, m_conf_dw, m_conf_dw_b, m_conf_ln_g, m_conf_ln_b, m_pool_w, m_pool_scale, m_sc_conv, m_gmlp_ln_g, m_gmlp_ln_b, m_gmlp_ws, m_gmlp_bs, m_w_branch, m_w_out, m_norm_mlp, m_w_up, m_w_down, m_norm_ple, m_w_ple, m_w_ple_gate, m_norm_final, v_norm_mix, v_w_in, v_conf_dw, v_conf_dw_b, v_conf_ln_g, v_conf_ln_b, v_pool_w, v_pool_scale, v_sc_conv, v_gmlp_ln_g, v_gmlp_ln_b, v_gmlp_ws, v_gmlp_bs, v_w_branch, v_w_out, v_norm_mlp, v_w_up, v_w_down, v_norm_ple, v_w_ple, v_w_ple_gate, v_norm_final):
    given = dict(locals())
    weights = {n: given[n] for n in WEIGHT_ORDER}
    mom_m = {n: given["m_" + n] for n in WEIGHT_ORDER}
    mom_v = {n: given["v_" + n] for n in WEIGHT_ORDER}
    my_shard = 2 * lax.axis_index("x") + lax.axis_index("y")

    n_layers = p.shape[0]
    shard_shapes = [weights[n].shape for n in SMALL_SHARDED]
    gathered = _allgather_small("allgather_small_weights", _pack([weights[n] for n in SMALL_SHARDED]))
    after_small = gathered[0, 0] * 0.0
    after_small = jnp.where(after_small == 0.0, after_small, 0.0)

    weight_copies = [[None] * len(WEIGHT_GROUPS) for _ in range(n_layers)]
    started = after_small
    for l in range(n_layers):
        for g, names in enumerate(WEIGHT_GROUPS):
            shards = [(weights[n][l] + started if i == 0 else weights[n][l]).astype(BF16) for i, n in enumerate(names)]
            weight_copies[l][g] = _weights_start(f"{l}_{g}", shards)
            started = weight_copies[l][g][4][0, 0]
    per_dev = gathered.reshape(8, -1, PACK_LANES)
    full_small_sharded = {}
    chip_parts = [_unpack(per_dev[2 * s], shard_shapes) for s in range(4)]
    for i, n in enumerate(SMALL_SHARDED):
        full_small_sharded[n] = jnp.concatenate([chip_parts[s][i] for s in range(4)], axis=-1)

    grad_copies = [[None] * len(GRAD_GROUPS) for _ in range(n_layers)]

    def layer_big(l):
        return lambda g, after: _weights_finish(f"{l}_{g}", WEIGHT_GROUPS[g], weight_copies[l][g], after)

    def on_big_grads(l):
        def send(g, group_grads):
            grad_copies[l][g] = _grads_start(f"{l}_{g}", GRAD_GROUPS[g], group_grads)
            return grad_copies[l][g][0][4][0, 0]
        return send

    loss_row, dx, small_grads = _local_step(x[0] + started, p[:, 0], loss_target[0], weights, layer_big, full_small_sharded,
                                            on_big_grads)
    loss = lax.psum(loss_row[0, 0], MESH_AXES)

    grads, delta, new_m, new_v = {}, {}, {}, {}
    halves = [{} for _ in range(n_layers)]
    last_sent = (0, len(GRAD_GROUPS) - 1)
    for l in reversed(range(n_layers)):
        for g, names in enumerate(GRAD_GROUPS):
            if (l, g) != last_sent:
                halves[l].update(_grads_finish(f"{l}_{g}", names, grad_copies[l][g], dx))
    behind = dx
    for g, names in enumerate(GRAD_GROUPS):
        if g == last_sent[1]:
            halves[0].update(_grads_finish(f"0_{g}", names, grad_copies[0][g], behind))
        for n, joined in zip(names, _join_halves(halves, names)):
            grads[n] = joined
            delta[n], new_m[n], new_v[n] = _adamw("adamw_" + n, weights[n], joined, mom_m[n], mom_v[n])
            behind = delta[n]

    small_names = SMALL_REP + SMALL_SHARDED
    after_big = behind[(0,) * behind.ndim] * 0.0
    after_big = jnp.where(after_big == 0.0, after_big, 0.0)
    packed = _pack([small_grads[n] for n in small_names]) + after_big
    rows = packed.shape[0]
    summed = _sum8("sum_small_grads", _allgather_small("allgather_small_grads", packed.astype(BF16)), rows)
    small_full_shapes = [weights[n].shape for n in SMALL_REP] + [full_small_sharded[n].shape for n in SMALL_SHARDED]
    for n, g in zip(small_names, _unpack(summed, small_full_shapes)):
        if n in SMALL_SHARDED:
            width = weights[n].shape[-1]
            g = lax.dynamic_slice_in_dim(g, my_shard * width, width, axis=g.ndim - 1)
        grads[n] = g

    for group, names in (("rep", SMALL_REP), ("sharded", SMALL_SHARDED)):
        outs = _adamw("adamw_small_" + group, *[_pack([src[n] for n in names]) for src in (weights, grads, mom_m, mom_v)])
        shapes = [weights[n].shape for n in names]
        for dst, buf in zip((delta, new_m, new_v), outs):
            for n, a in zip(names, _unpack(buf, shapes)):
                dst[n] = a

    return (loss, dx[None], *[grads[n] for n in WEIGHT_ORDER], *[delta[n] for n in WEIGHT_ORDER],
            *[new_m[n] for n in WEIGHT_ORDER], *[new_v[n] for n in WEIGHT_ORDER])
```

```python
import jax
import jax.numpy as jnp
from jax import lax
from jax.experimental import pallas as pl
from jax.experimental.pallas import tpu as pltpu

F32, BF16 = jnp.float32, jnp.bfloat16
EPS = 1e-6
ADAM_LR, ADAM_B1, ADAM_B2, ADAM_EPS, ADAM_WD, ADAM_STEP = 0.001, 0.9, 0.999, 1e-08, 0.01, 10
CONF_KERNEL, SC_KERNEL = 31, 3
POOL_WINDOWS = (2, 4, 8, 16)
MIX = 512
GROUP = 128
HALO = 32
SEQ_TILE = 512
BRANCH_ROWS = 512
VMEM_LIMIT_BYTES = 48 * 1024 * 1024
MESH_AXES = ("x", "y", "c")
ANY = pl.BlockSpec(memory_space=pl.ANY)

BIG = (("w_in", 2, 1), ("w_branch", 3, 1), ("w_out", 1, 2), ("w_up", 2, 1), ("w_down", 1, 2),
       ("w_ple", 2, 1), ("w_ple_gate", 1, 2))
SMALL_REP = ("norm_mix", "conf_dw_b", "conf_ln_g", "conf_ln_b", "pool_w", "pool_scale", "gmlp_ln_g",
             "gmlp_ln_b", "gmlp_ws", "gmlp_bs", "norm_mlp", "norm_ple", "norm_final")
SMALL_SHARDED = ("conf_dw", "sc_conv")
BY_CHIP = ("w_in", "w_up")
BIG_AXES = {name: (shard_axis, half_axis) for name, shard_axis, half_axis in BIG}
WEIGHT_GROUPS = (("w_in",), ("w_branch", "w_out", "w_up", "w_down", "w_ple_gate", "w_ple"))
GRAD_GROUPS = (("w_ple", "w_ple_gate", "w_down", "w_up"), ("w_out", "w_branch", "w_in"))
PACK_LANES = 128
PACK_ROW_MULTIPLE = 256


def _params(*semantics):
    return pltpu.CompilerParams(dimension_semantics=semantics, vmem_limit_bytes=VMEM_LIMIT_BYTES)


def _accumulate(ref, value, first):
    @pl.when(first)
    def _():
        ref[...] = value.astype(ref.dtype)

    @pl.when(jnp.logical_not(first))
    def _():
        ref[...] += value.astype(ref.dtype)


def _rowwise(name, fn, rows, params, outs, accs=(), tm=256):
    n_rows = rows[0][0].shape[-2]
    tm = min(tm, n_rows)
    n_in, n_out = len(rows) + len(params), len(outs)
    targets = [(k, o[2]) for k, o in enumerate(outs) if len(o) == 4]

    def body(*refs):
        res = fn(*[r[...] for r in refs[:n_in]])
        res = res if isinstance(res, (tuple, list)) else (res,)
        out_refs = refs[n_in + len(targets):]
        for r, v in zip(out_refs[:n_out], res[:n_out]):
            r[...] = v.astype(r.dtype)
        first = pl.program_id(0) == 0
        for r, v in zip(out_refs[n_out:], res[n_out:]):
            _accumulate(r, v, first)

    in_specs = []
    for row in rows:
        arr, w, cb = row[:3]
        if len(row) == 4:
            in_specs.append(pl.BlockSpec((None, tm, w), lambda i, cb=cb, g=row[3]: (g, i, cb)))
        else:
            in_specs.append(pl.BlockSpec((tm, w), lambda i, cb=cb: (i, cb)))
    for p in params:
        in_specs.append(pl.BlockSpec(p.shape, lambda i, nd=p.ndim: (0,) * nd))
    in_specs += [ANY] * len(targets)
    out_specs = [pl.BlockSpec((tm, o[0]), lambda i, cb=(o[3] if len(o) == 4 else 0): (i, cb)) for o in outs]
    out_specs += [pl.BlockSpec(s, lambda i, nd=len(s): (0,) * nd) for s in accs]
    out_shape = [jax.ShapeDtypeStruct(o[2].shape if len(o) == 4 else (n_rows, o[0]), o[1]) for o in outs]
    out_shape += [jax.ShapeDtypeStruct(s, F32) for s in accs]
    return pl.pallas_call(
        body, name=name, grid=(n_rows // tm,), in_specs=in_specs, out_specs=out_specs, out_shape=out_shape,
        input_output_aliases={n_in + t: k for t, (k, _) in enumerate(targets)},
        compiler_params=_params("arbitrary"),
    )(*[r[0] for r in rows], *params, *[t for _, t in targets])


ROW_OUT_SUBLANES = 8


def _mm(name, a, b, *, ta=False, tb=False, a_cols=0, b_by_chip=False, chips=(0, 4), res=(), rows=(), epi=None, out_dtypes=(F32,), n_row_outs=0,
        tm=1024, tn=1024, tk=1024):
    if a_cols:
        assert ta
        g_, m_, k_ = a.shape[1] // a_cols, a_cols, a.shape[0]
    else:
        g_, m_, k_ = (a.shape[0], a.shape[2], a.shape[1]) if ta else a.shape
    if b_by_chip:
        n_ = b.shape[2] if tb else chips[1] * b.shape[3]
    else:
        n_ = b.shape[1] if tb else b.shape[2]
    if ta:
        tk = 2 * tk
    tm, tn, tk = min(tm, m_), min(tn, n_), min(tk, k_)
    if b_by_chip:
        tn, tk = (tn, min(tk, b.shape[3])) if tb else (min(tn, b.shape[3]), tk)
        per_chip = b.shape[3] // (tk if tb else tn)
    nk, n_res, n_rows, n_out = k_ // tk, len(res), len(rows), len(out_dtypes)
    n_in = 2 + n_res + n_rows
    dims = (((0 if ta else 1,), (1 if tb else 0,)), ((), ()))
    if epi is None:
        epi = lambda acc: (acc,)

    def body(*refs):
        a_ref, b_ref = refs[:2]
        res_refs, out_refs, acc_ref = refs[2:n_in], refs[n_in:n_in + n_out], refs[-1]
        row_out_refs = refs[n_in + n_out:-1]
        k = pl.program_id(3)

        def product():
            return lax.dot_general(a_ref[...].astype(BF16), b_ref[...].astype(BF16), dims, preferred_element_type=F32)

        def finish(acc):
            vals = epi(acc, *[r[...] for r in res_refs])
            for o, v in zip(out_refs, vals):
                o[...] = v.astype(o.dtype)
            for o, v in zip(row_out_refs, vals[n_out:]):
                o[...] = jnp.broadcast_to(v, o.shape)

        if nk == 1:
            finish(product())
            return

        @pl.when(k == 0)
        def _():
            acc_ref[...] = jnp.zeros_like(acc_ref)

        acc_ref[...] += product()

        @pl.when(k == nk - 1)
        def _():
            finish(acc_ref[...])

    if a_cols:
        a_spec = pl.BlockSpec((tk, tm), lambda g, i, j, k: (k, g * (m_ // tm) + i))
    elif ta:
        a_spec = pl.BlockSpec((None, tk, tm), lambda g, i, j, k: (g, k, i))
    else:
        a_spec = pl.BlockSpec((None, tm, tk), lambda g, i, j, k: (g, i, k))
    if b_by_chip and tb:
        b_spec = pl.BlockSpec((None, None, tn, tk), lambda g, i, j, k: (g, k // per_chip, j, k % per_chip))
    elif b_by_chip:
        b_spec = pl.BlockSpec((None, None, tk, tn), lambda g, i, j, k: (g, chips[0] + j // per_chip, k, j % per_chip))
    elif tb:
        b_spec = pl.BlockSpec((None, tn, tk), lambda g, i, j, k: (g, j, k))
    else:
        b_spec = pl.BlockSpec((None, tk, tn), lambda g, i, j, k: (g, k, j))
    res_specs = [pl.BlockSpec((None, tm, tn), lambda g, i, j, k, off=off: (g, i, j + off)) for _, off in res]
    row_specs = [pl.BlockSpec((None, 1, tn), lambda g, i, j, k: (g, 0, j))] * n_rows
    out_spec = pl.BlockSpec((None, tm, tn), lambda g, i, j, k: (g, i, j))
    row_out_spec = pl.BlockSpec((None, ROW_OUT_SUBLANES, tn), lambda g, i, j, k: (g, i, j))
    outs = pl.pallas_call(
        body, name=name, grid=(g_, m_ // tm, n_ // tn, nk),
        in_specs=[a_spec, b_spec, *res_specs, *row_specs], out_specs=[out_spec] * n_out + [row_out_spec] * n_row_outs,
        out_shape=[jax.ShapeDtypeStruct((g_, m_, n_), d) for d in out_dtypes]
        + [jax.ShapeDtypeStruct((g_, ROW_OUT_SUBLANES * (m_ // tm), n_), F32)] * n_row_outs,
        scratch_shapes=[pltpu.VMEM((tm, tn), F32)],
        compiler_params=_params("parallel", "parallel", "parallel", "arbitrary"),
    )(a, b, *[r for r, _ in res], *rows)
    return outs


def _mm2(name, a, b, *, res=(), rows=(), **kw):
    outs = _mm(name, a[None], b[None], res=[(r[None], off) for r, off in res], rows=[r[None] for r in rows], **kw)
    return [o[0] for o in outs]


def _sum_row_out(row_out):
    return jnp.sum(row_out[::ROW_OUT_SUBLANES], axis=0, keepdims=True)


@jax.custom_vjp
def _bdot(a, b):
    return jnp.dot(a.astype(BF16), b.astype(BF16), preferred_element_type=F32)


def _bdot_fwd(a, b):
    return _bdot(a, b), (a, b)


def _bdot_bwd(saved, g):
    a, b = saved
    gb = g.astype(BF16)
    da = lax.dot_general(gb, b.astype(BF16), (((1,), (1,)), ((), ())), preferred_element_type=F32)
    db = lax.dot_general(a.astype(BF16), gb, (((0,), (0,)), ((), ())), preferred_element_type=F32)
    return da.astype(a.dtype), db.astype(b.dtype)


_bdot.defvjp(_bdot_fwd, _bdot_bwd)


def _rms(x, g):
    return x * lax.rsqrt(jnp.mean(x * x, axis=-1, keepdims=True) + EPS) * g


def _layer_norm(x, g, b):
    mu = jnp.mean(x, axis=-1, keepdims=True)
    var = jnp.mean(jnp.square(x - mu), axis=-1, keepdims=True)
    return (x - mu) * lax.rsqrt(var + EPS) * g + b


def _glu_and_gate_fn(a, a_gate, cg, hx):
    return a * jax.nn.sigmoid(a_gate), cg * hx


def _branches_fn(ca, pooled, cc, bg, u, v, conf_b, conf_g, conf_beta, pool_scale, gm_g, gm_b, *mats):
    pool_w, ws, bs = mats[0:4], mats[4:8], mats[8:12]
    ya = jax.nn.silu(_layer_norm(ca + conf_b, conf_g, conf_beta))
    yb = jnp.concatenate([_bdot(pooled[:, g * GROUP:(g + 1) * GROUP], pool_w[g]) for g in range(4)], axis=1) * pool_scale
    yc = bg * cc
    vln = _layer_norm(v, gm_g, gm_b)
    causal = lax.broadcasted_iota(jnp.int32, (GROUP, GROUP), 0) >= lax.broadcasted_iota(jnp.int32, (GROUP, GROUP), 1)
    chunks = v.shape[0] // GROUP

    def spatial(g):
        vg = vln[:, g * GROUP:(g + 1) * GROUP]
        wide = jnp.concatenate([vg[c * GROUP:(c + 1) * GROUP, :] for c in range(chunks)], axis=1)
        out = _bdot(jnp.where(causal, ws[g], 0.0), wide) + bs[g]
        return jnp.concatenate([out[:, c * GROUP:(c + 1) * GROUP] for c in range(chunks)], axis=0)

    sg = jnp.concatenate([spatial(g) for g in range(4)], axis=1)
    yd = u * sg
    return ya, yb, yc, yd


def _branches_joined_fn(*args):
    return jnp.concatenate(_branches_fn(*args), axis=1)


def _branches_bwd_fn(*args):
    ins, dys = args[:6] + args[10:], args[6:10]
    _, vjp = jax.vjp(_branches_fn, *ins)
    grads = vjp(tuple(dys))
    duv = jnp.concatenate([grads[4], grads[5]], axis=1)
    return (*grads[:4], duv, *grads[6:])


def _glu_bwd_fn(a, a_gate, cg, hx, dga, dchx, dpin, dbg):
    _, vjp = jax.vjp(_glu_and_gate_fn, a, a_gate, cg, hx)
    da, dag, dcg, dhx = vjp((dga, dchx))
    return jnp.concatenate([da, dag, dpin.astype(F32), dbg.astype(F32), dcg, dhx], axis=1)


def _rms_bwd_epilogue(dh, x, dres, g):
    _, vjp = jax.vjp(_rms, x, g)
    dx, dg = vjp(dh)
    return dx + dres, dg


def _residual_then_rms(acc, residual, g):
    x = acc + residual
    return x, _rms(x, g)


def _loss_fn(x, target, g):
    def f(x, g):
        err = jnp.square(_rms(x, g) - target)
        return 0.5 * jnp.sum(jnp.mean(err, axis=-1, keepdims=True), axis=0, keepdims=True)

    loss, vjp = jax.vjp(f, x, g)
    dx, dg = vjp(jnp.ones((1, 1), F32))
    return dx, dg, jnp.broadcast_to(loss, (1, PACK_LANES))


def _adamw_fn(w, g, m, v):
    m = ADAM_B1 * m + (1.0 - ADAM_B1) * g
    v = ADAM_B2 * v + (1.0 - ADAM_B2) * jnp.square(g)
    m_hat = m / (1.0 - ADAM_B1 ** ADAM_STEP)
    v_hat = v / (1.0 - ADAM_B2 ** ADAM_STEP)
    delta = -ADAM_LR * (m_hat / (jnp.sqrt(v_hat) + ADAM_EPS) + ADAM_WD * w)
    return delta, m, v


def _group_pick(j):
    return [(j == g).astype(F32) for g in range(len(POOL_WINDOWS))]


SUBLANES = 8


def _causal_taps(x, taps):
    for r in range(min(SUBLANES, taps)):
        behind = x if r == 0 else pltpu.roll(x, r, 0)
        for q in range((taps - 1 - r) // SUBLANES + 1):
            start = HALO - SUBLANES * q
            yield SUBLANES * q + r, behind[start:start + SEQ_TILE, :]


def _rows_iota(n):
    return lax.broadcasted_iota(jnp.int32, (n, 1), 0)


def _seq_specs(n_time, causal):
    per = SEQ_TILE // HALO
    tile = pl.BlockSpec((SEQ_TILE, GROUP), lambda j, i: (i, j))
    if causal:
        halo = pl.BlockSpec((HALO, GROUP), lambda j, i: (jnp.maximum(i * per - 1, 0), j))
    else:
        halo = pl.BlockSpec((HALO, GROUP), lambda j, i: (jnp.minimum((i + 1) * per, n_time * per - 1), j))
    return tile, halo


def _seq_fwd(ga, proj, chx, conf_dw, sc_conv):
    s_len = ga.shape[0]
    n_time = s_len // SEQ_TILE
    ext = HALO + SEQ_TILE

    def body(ga_ref, ga_h, pin_ref, pin_h, chx_ref, chx_h, wc_ref, ws_ref, ca_ref, po_ref, cc_ref):
        j, i = pl.program_id(0), pl.program_id(1)
        keep = (i > 0).astype(F32)

        def extended(cur, halo):
            return jnp.concatenate([halo[...] * keep, cur[...]], axis=0)

        def conv(x, w_ref, taps):
            acc = None
            for d, rows in _causal_taps(x, taps):
                term = rows * w_ref[pl.ds(taps - 1 - d, 1), :]
                acc = term if acc is None else acc + term
            return acc

        ca_ref[...] = conv(extended(ga_ref, ga_h), wc_ref, CONF_KERNEL)
        cc_ref[...] = conv(extended(chx_ref, chx_h), ws_ref, SC_KERNEL)
        u = extended(pin_ref, pin_h)
        s1 = u + pltpu.roll(u, 1, 0)
        s2 = s1 + pltpu.roll(s1, 2, 0)
        s3 = s2 + pltpu.roll(s2, 4, 0)
        s4 = s3 + pltpu.roll(s3, 8, 0)
        pick = _group_pick(j)
        window_sum = s1 * pick[0] + s2 * pick[1] + s3 * pick[2] + s4 * pick[3]
        window = sum(float(w) * f for w, f in zip(POOL_WINDOWS, pick))
        pos = (_rows_iota(ext) + (i * SEQ_TILE - HALO + 1)).astype(F32)
        count = jnp.maximum(jnp.minimum(pos, window), 1.0)
        po_ref[...] = (window_sum / count - u)[HALO:, :]

    tile, halo = _seq_specs(n_time, causal=True)
    pin_tile = pl.BlockSpec((SEQ_TILE, GROUP), lambda j, i: (i, 2 * (MIX // GROUP) + j))
    pin_halo = pl.BlockSpec((HALO, GROUP), lambda j, i: (jnp.maximum(i * (SEQ_TILE // HALO) - 1, 0), 2 * (MIX // GROUP) + j))
    w_spec = lambda taps: pl.BlockSpec((taps, GROUP), lambda j, i: (0, j))
    return pl.pallas_call(
        body, name="seq_fwd", grid=(MIX // GROUP, n_time),
        in_specs=[tile, halo, pin_tile, pin_halo, tile, halo, w_spec(CONF_KERNEL), w_spec(SC_KERNEL)],
        out_specs=[tile] * 3, out_shape=[jax.ShapeDtypeStruct((s_len, MIX), F32)] * 3,
        compiler_params=_params("parallel", "arbitrary"),
    )(ga, ga, proj, proj, chx, chx, conf_dw, sc_conv)


def _seq_bwd(dca, dpooled, dcc, ga, chx, conf_dw, sc_conv):
    s_len = ga.shape[0]
    n_time = s_len // SEQ_TILE
    ext = HALO + SEQ_TILE

    def body(dca_ref, dca_h, dpo_ref, dpo_h, dcc_ref, dcc_h, ga_ref, ga_h, chx_ref, chx_h, wc_ref, ws_ref,
             dga_ref, dpin_ref, dchx_ref, dwc_ref, dws_ref, acc_c, acc_s):
        j, i = pl.program_id(0), pl.program_id(1)
        keep_prev = (i > 0).astype(F32)
        keep_next = (i < n_time - 1).astype(F32)

        def with_next(cur, halo):
            return jnp.concatenate([cur[...], halo[...] * keep_next], axis=0)

        def with_prev(cur, halo):
            return jnp.concatenate([halo[...] * keep_prev, cur[...]], axis=0)

        def conv_t(dy, w_ref, taps):
            acc = None
            for r in range(min(SUBLANES, taps)):
                ahead = dy if r == 0 else pltpu.roll(dy, ext - r, 0)
                for q in range((taps - 1 - r) // SUBLANES + 1):
                    term = ahead[SUBLANES * q:SUBLANES * q + SEQ_TILE, :] * w_ref[pl.ds(taps - 1 - (SUBLANES * q + r), 1), :]
                    acc = term if acc is None else acc + term
            return acc

        def tap_grads(dy_cur, x, acc_ref, dw_ref, taps):
            @pl.when(i == 0)
            def _():
                acc_ref[...] = jnp.zeros_like(acc_ref)

            for d, rows in _causal_taps(x, taps):
                prod = dy_cur * rows
                part = prod[:SUBLANES, :]
                for g in range(1, SEQ_TILE // SUBLANES):
                    part = part + prod[g * SUBLANES:(g + 1) * SUBLANES, :]
                acc_ref[pl.ds((taps - 1 - d) * SUBLANES, SUBLANES), :] += part

            @pl.when(i == n_time - 1)
            def _():
                for k in range(taps):
                    dw_ref[pl.ds(k, 1), :] = jnp.sum(acc_ref[pl.ds(k * SUBLANES, SUBLANES), :], axis=0, keepdims=True)

        dga_ref[...] = conv_t(with_next(dca_ref, dca_h), wc_ref, CONF_KERNEL)
        dchx_ref[...] = conv_t(with_next(dcc_ref, dcc_h), ws_ref, SC_KERNEL)
        tap_grads(dca_ref[...], with_prev(ga_ref, ga_h), acc_c, dwc_ref, CONF_KERNEL)
        tap_grads(dcc_ref[...], with_prev(chx_ref, chx_h), acc_s, dws_ref, SC_KERNEL)

        dpo = with_next(dpo_ref, dpo_h)
        pick = _group_pick(j)
        window = sum(float(w) * f for w, f in zip(POOL_WINDOWS, pick))
        pos = (_rows_iota(ext) + (i * SEQ_TILE + 1)).astype(F32)
        r = dpo / jnp.minimum(pos, window)
        q1 = r + pltpu.roll(r, ext - 1, 0)
        q2 = q1 + pltpu.roll(q1, ext - 2, 0)
        q3 = q2 + pltpu.roll(q2, ext - 4, 0)
        q4 = q3 + pltpu.roll(q3, ext - 8, 0)
        q = q1 * pick[0] + q2 * pick[1] + q3 * pick[2] + q4 * pick[3]
        dpin_ref[...] = (q - dpo)[:SEQ_TILE, :].astype(dpin_ref.dtype)

    tile, nxt = _seq_specs(n_time, causal=False)
    _, prv = _seq_specs(n_time, causal=True)
    w_spec = lambda taps: pl.BlockSpec((taps, GROUP), lambda j, i: (0, j))
    return pl.pallas_call(
        body, name="seq_bwd", grid=(MIX // GROUP, n_time),
        in_specs=[tile, nxt, tile, nxt, tile, nxt, tile, prv, tile, prv, w_spec(CONF_KERNEL), w_spec(SC_KERNEL)],
        out_specs=[tile, tile, tile, w_spec(CONF_KERNEL), w_spec(SC_KERNEL)],
        out_shape=[jax.ShapeDtypeStruct((s_len, MIX), F32), jax.ShapeDtypeStruct((s_len, MIX), BF16),
                   jax.ShapeDtypeStruct((s_len, MIX), F32), jax.ShapeDtypeStruct(conf_dw.shape, F32),
                   jax.ShapeDtypeStruct(sc_conv.shape, F32)],
        scratch_shapes=[pltpu.VMEM((CONF_KERNEL * SUBLANES, GROUP), F32), pltpu.VMEM((SC_KERNEL * SUBLANES, GROUP), F32)],
        compiler_params=_params("parallel", "arbitrary"),
    )(dca, dca, dpooled, dpooled, dcc, dcc, ga, ga, chx, chx, conf_dw, sc_conv)


MERGE_TILE = 1024
MERGE_ROWS = 1024


def _merge_fwd(y, w_branch, gates):
    s_len, d_model = y.shape[0], w_branch.shape[2]
    t = min(MERGE_ROWS, s_len)
    per_branch = d_model // MERGE_TILE

    def body(y_ref, w_ref, gate_ref, out_ref, acc_ref):
        k = pl.program_id(2)

        @pl.when(k == 0)
        def _():
            acc_ref[...] = jnp.zeros_like(acc_ref)

        z = jnp.dot(y_ref[...], w_ref[...], preferred_element_type=F32)
        acc_ref[...] += gate_ref[...].astype(F32) * z

        @pl.when(k == 3)
        def _():
            out_ref[...] = acc_ref[...].astype(out_ref.dtype)

    return pl.pallas_call(
        body, name="merge_fwd", grid=(s_len // t, per_branch, 4),
        in_specs=[pl.BlockSpec((t, MIX), lambda i, j, k: (i, k)),
                  pl.BlockSpec((None, MIX, MERGE_TILE), lambda i, j, k: (k, 0, j)),
                  pl.BlockSpec((t, MERGE_TILE), lambda i, j, k: (i, k * per_branch + j))],
        out_specs=pl.BlockSpec((t, MERGE_TILE), lambda i, j, k: (i, j)),
        out_shape=jax.ShapeDtypeStruct((s_len, d_model), BF16),
        scratch_shapes=[pltpu.VMEM((t, MERGE_TILE), F32)],
        compiler_params=_params("parallel", "parallel", "arbitrary"),
    )(y, w_branch, gates)


def _merge_bwd(y, w_branch, gates, dmerged):
    s_len, d_model = y.shape[0], w_branch.shape[2]
    t = min(MERGE_ROWS, s_len)
    per_branch = d_model // MERGE_TILE

    def body(y_ref, w_ref, gate_ref, dm_ref, dz_ref, dgate_ref):
        z = jnp.dot(y_ref[...], w_ref[...], preferred_element_type=F32)
        gate = gate_ref[...].astype(F32)
        dm = dm_ref[...]
        dz_ref[...] = (dm * gate).astype(dz_ref.dtype)
        dgate_ref[...] = (dm * z * gate * (1.0 - gate)).astype(dgate_ref.dtype)

    return pl.pallas_call(
        body, name="merge_bwd", grid=(s_len // t, per_branch, 4),
        in_specs=[pl.BlockSpec((t, MIX), lambda i, j, k: (i, k)),
                  pl.BlockSpec((None, MIX, MERGE_TILE), lambda i, j, k: (k, 0, j)),
                  pl.BlockSpec((t, MERGE_TILE), lambda i, j, k: (i, k * per_branch + j)),
                  pl.BlockSpec((t, MERGE_TILE), lambda i, j, k: (i, j))],
        out_specs=[pl.BlockSpec((None, t, MERGE_TILE), lambda i, j, k: (k, i, j)),
                   pl.BlockSpec((t, MERGE_TILE), lambda i, j, k: (i, (4 + k) * per_branch + j))],
        out_shape=[jax.ShapeDtypeStruct((4, s_len, d_model), BF16), jax.ShapeDtypeStruct((s_len, 8 * d_model), BF16)],
        compiler_params=_params("parallel", "parallel", "arbitrary"),
    )(y, w_branch, gates, dmerged)


def _row(v):
    return v.reshape(1, -1)


def _branch_params(w):
    mats = [w["pool_w"][g] for g in range(4)] + [w["gmlp_ws"][g] for g in range(4)]
    mats += [w["gmlp_bs"][g].reshape(GROUP, 1) for g in range(4)]
    return [_row(w["conf_dw_b"]), _row(w["conf_ln_g"]), _row(w["conf_ln_b"]), _row(w["pool_scale"]),
            _row(w["gmlp_ln_g"]), _row(w["gmlp_ln_b"])] + mats


def _layer_fwd(x0, p_l, w, fetch):
    d_model = x0.shape[1]
    (h1,) = _rowwise("rms_mix", _rms, [(x0, d_model, 0)], [_row(w["norm_mix"])], [(d_model, BF16)])
    w.update(fetch(0, h1))
    (proj,) = _mm2("proj", h1, w["w_in"], b_by_chip=True, chips=(0, 2))
    (gates,) = _mm2("proj_gates", h1, w["w_in"], b_by_chip=True, chips=(2, 2), out_dtypes=(BF16,),
                    epi=lambda acc: (jax.nn.sigmoid(acc),))
    ga, chx = _rowwise("glu", _glu_and_gate_fn, [(proj, MIX, 0), (proj, MIX, 1), (proj, MIX, 4), (proj, MIX, 5)], [],
                       [(MIX, F32), (MIX, F32)])
    ca, pooled, cc = _seq_fwd(ga, proj, chx, w["conf_dw"], w["sc_conv"])
    (y,) = _rowwise("branches", _branches_joined_fn,
                    [(ca, MIX, 0), (pooled, MIX, 0), (cc, MIX, 0), (proj, MIX, 3), (proj, MIX, 6), (proj, MIX, 7)],
                    _branch_params(w), [(4 * MIX, BF16)], tm=BRANCH_ROWS)
    w.update(fetch(1, y))
    merged = _merge_fwd(y, w["w_branch"], gates)
    x1, h2 = _mm2("out_proj", merged, w["w_out"], res=[(x0, 0)], rows=[_row(w["norm_mlp"])], out_dtypes=(F32, BF16),
                  epi=_residual_then_rms)
    (act,) = _mm2("mlp_up", h2, w["w_up"], b_by_chip=True, out_dtypes=(BF16,),
                  epi=lambda acc: (jnp.square(jnp.maximum(acc, 0.0)),))
    x2, h3 = _mm2("mlp_down", act, w["w_down"], res=[(x1, 0)], rows=[_row(w["norm_ple"])], out_dtypes=(F32, BF16),
                  epi=_residual_then_rms)
    (pg,) = _mm2("ple_gate", h3, w["w_ple_gate"])
    (x3,) = _mm2("ple", p_l, w["w_ple"], res=[(pg, 0), (x2, 0)],
                 epi=lambda acc, g, r: (acc * jax.nn.sigmoid(g) + r,))
    saved = dict(x0=x0, h1=h1, proj=proj, gates=gates, ga=ga, chx=chx, ca=ca, pooled=pooled, cc=cc, y=y, merged=merged, x1=x1,
                 h2=h2, act=act, x2=x2, h3=h3, pg=pg)
    return x3, saved


def _layer_bwd(dx3, p_l, w, s, send):
    d_model = dx3.shape[1]
    big, small = {}, {}

    def ple_epi(acc, dx, g):
        sig = jax.nn.sigmoid(g)
        return dx * sig, dx * acc * sig * (1.0 - sig)

    dpp, dpg = _mm2("ple_bwd", p_l, w["w_ple"], res=[(dx3, 0), (s["pg"], 0)], epi=ple_epi, out_dtypes=(BF16, BF16))
    (big["w_ple"],) = _mm2("dw_ple", p_l, dpp, ta=True, out_dtypes=(BF16,))
    (big["w_ple_gate"],) = _mm2("dw_ple_gate", s["h3"], dpg, ta=True, out_dtypes=(BF16,))
    dx2, dg = _mm2("dh_ple", dpg, w["w_ple_gate"], tb=True, res=[(s["x2"], 0), (dx3, 0)], rows=[_row(w["norm_ple"])],
                   epi=_rms_bwd_epilogue, n_row_outs=1)
    small["norm_ple"] = _sum_row_out(dg)

    (dup,) = _mm2("mlp_down_bwd", dx2, w["w_down"], tb=True, res=[(s["act"], 0)], out_dtypes=(BF16,),
                  epi=lambda acc, act: (acc * 2.0 * jnp.sqrt(act.astype(F32)),))
    (big["w_down"],) = _mm2("dw_down", s["act"], dx2, ta=True, out_dtypes=(BF16,))
    (big["w_up"],) = _mm2("dw_up", s["h2"], dup, ta=True, out_dtypes=(BF16,))
    sent = send(0, [big[n] for n in GRAD_GROUPS[0]])
    dx1, dg = _mm2("mlp_up_bwd", dup, w["w_up"], tb=True, b_by_chip=True, res=[(s["x1"], 0), (dx2, 0)],
                   rows=[_row(w["norm_mlp"]) + sent], epi=_rms_bwd_epilogue, n_row_outs=1)
    small["norm_mlp"] = _sum_row_out(dg)

    (dmerged,) = _mm2("out_proj_bwd", dx1, w["w_out"], tb=True)
    (big["w_out"],) = _mm2("dw_out", s["merged"], dx1, ta=True, out_dtypes=(BF16,))
    dz, dproj = _merge_bwd(s["y"], w["w_branch"], s["gates"], dmerged)
    (dy,) = _mm("branch_bwd", dz, w["w_branch"], tb=True)
    (big["w_branch"],) = _mm("dw_branch", s["y"], dz, ta=True, a_cols=MIX, out_dtypes=(BF16,))

    proj = s["proj"]
    params = _branch_params(w)
    grads = _rowwise(
        "branches_bwd", _branches_bwd_fn,
        [(s["ca"], MIX, 0), (s["pooled"], MIX, 0), (s["cc"], MIX, 0), (proj, MIX, 3), (proj, MIX, 6), (proj, MIX, 7)]
        + [(dy, MIX, 0, g) for g in range(4)], params,
        [(MIX, F32), (MIX, F32), (MIX, F32), (MIX, BF16), (2 * MIX, BF16, dproj, 3)], [p.shape for p in params], tm=BRANCH_ROWS)
    dca, dpooled, dcc, dbg, dproj = grads[:5]
    pg_ = grads[5:]
    small["conf_dw_b"], small["conf_ln_g"], small["conf_ln_b"], small["pool_scale"] = pg_[0], pg_[1], pg_[2], pg_[3]
    small["gmlp_ln_g"], small["gmlp_ln_b"] = pg_[4], pg_[5]
    small["pool_w"] = jnp.stack(pg_[6:10])
    small["gmlp_ws"] = jnp.stack(pg_[10:14])
    small["gmlp_bs"] = jnp.stack([b.reshape(GROUP) for b in pg_[14:18]])

    dga, dpin, dchx, small["conf_dw"], small["sc_conv"] = _seq_bwd(dca, dpooled, dcc, s["ga"], s["chx"], w["conf_dw"], w["sc_conv"])
    (dproj,) = _rowwise("glu_bwd", _glu_bwd_fn,
                        [(proj, MIX, 0), (proj, MIX, 1), (proj, MIX, 4), (proj, MIX, 5), (dga, MIX, 0), (dchx, MIX, 0),
                         (dpin, MIX, 0), (dbg, MIX, 0)], [], [(6 * MIX, BF16, dproj, 0)])
    (big["w_in"],) = _mm2("dw_in", s["h1"], dproj, ta=True, out_dtypes=(BF16,))
    sent = send(1, [big[n] for n in GRAD_GROUPS[1]])
    dx0, dg = _mm2("proj_bwd", dproj, w["w_in"], tb=True, b_by_chip=True, res=[(s["x0"], 0), (dx1, 0)],
                   rows=[_row(w["norm_mix"]) + sent], epi=_rms_bwd_epilogue, n_row_outs=1)
    small["norm_mix"] = _sum_row_out(dg)
    return dx0, small


def _position():
    return lax.axis_index("x"), lax.axis_index("y"), lax.axis_index("c")


def _other_chips(x, y):
    return [(1 - x, y), (x, 1 - y), (1 - x, 1 - y)]


def _region(ref, cuts):
    return ref.at[tuple(pl.ds(*cuts[a]) if a in cuts else slice(None) for a in range(len(ref.shape)))]


def _allgather_small(name, block):
    m_per, n = block.shape

    def body(x_ref, out_ref, send_sems, recv_sems, local_sem):
        x, y, c = _position()
        me, sibling = (x, y, c), (x, y, 1 - c)
        chips = _other_chips(x, y)

        def rows(px, py, pc):
            return out_ref.at[pl.ds((4 * px + 2 * py + pc) * m_per, m_per), :]

        def copy(k, blk, to, src=None):
            return pltpu.make_async_remote_copy(
                src_ref=rows(*blk) if src is None else src, dst_ref=rows(*blk), send_sem=send_sems.at[k],
                recv_sem=recv_sems.at[k], device_id=to, device_id_type=pl.DeviceIdType.MESH)

        mine = pltpu.make_async_copy(x_ref, rows(*me), local_sem)
        mine.start()
        first = [copy(0, me, sibling, src=x_ref)]
        first += [copy(1 + j, me, (*chip, c), src=x_ref) for j, chip in enumerate(chips)]
        for cp in first:
            cp.start()
        passed = [copy(4 + j, (*chip, c), sibling) for j, chip in enumerate(chips)]
        for j, chip in enumerate(chips):
            copy(1 + j, (*chip, c), me).wait_recv()
            passed[j].start()
        copy(0, sibling, me).wait_recv()
        for j, chip in enumerate(chips):
            copy(4 + j, (*chip, 1 - c), me).wait_recv()
        for cp in first + passed:
            cp.wait_send()
        mine.wait()

    return pl.pallas_call(
        body, name=name, out_shape=jax.ShapeDtypeStruct((8 * m_per, n), block.dtype), in_specs=[ANY], out_specs=ANY,
        scratch_shapes=[pltpu.SemaphoreType.DMA((7,)), pltpu.SemaphoreType.DMA((7,)), pltpu.SemaphoreType.DMA],
    )(block)


def _cuts(arr_shape, shard_axis, half_axis, shard=None, half=None):
    cuts = {}
    if shard is not None:
        n = arr_shape[shard_axis] // 4
        cuts[shard_axis] = (shard * n, n)
    if half is not None:
        n = arr_shape[half_axis] // 2
        cuts[half_axis] = (half * n, n)
    return cuts


HBM =pl.BlockSpec(memory_space=pltpu.HBM)
SEM = pl.BlockSpec(memory_space=pltpu.SEMAPHORE)
DATAFLOW_EFFECT = pltpu.SideEffectType.DATAFLOW_SIDE_EFFECTING
TOKEN_SHAPE = (8, 128)


def _remote(src, dst, send_sems, recv_sems, k, peer):
    return pltpu.make_async_remote_copy(src_ref=src, dst_ref=dst, send_sem=send_sems.at[k], recv_sem=recv_sems.at[k],
                                        device_id=peer, device_id_type=pl.DeviceIdType.MESH)


def _split_start(name, sources, landings, plan):
    n_s, n_l = len(sources), len(landings)
    n_copies = len(plan([None] * n_s, [None] * n_l, None))

    def body(*refs):
        src, land = refs[:n_s], refs[n_s:n_s + n_l]
        send_sems, recv_sems, token = refs[n_s + n_l], refs[n_s + n_l + 1], refs[-1]
        for k, (s, d, peer) in enumerate(plan(src, land, _position())):
            _remote(s, d, send_sems, recv_sems, k, peer).start()
        token[...] = jnp.zeros_like(token)

    arrays = [pltpu.with_memory_space_constraint(a, pltpu.HBM) for a in (*sources, *landings)]
    outs = pl.pallas_call(
        body, name=name,
        out_shape=(pltpu.SemaphoreType.DMA((n_copies,)), pltpu.SemaphoreType.DMA((n_copies,)),
                   *[pltpu.HBM(a.shape, a.dtype) for a in arrays], jax.ShapeDtypeStruct(TOKEN_SHAPE, F32)),
        in_specs=[HBM] * (n_s + n_l),
        out_specs=(SEM, SEM, *[HBM] * (n_s + n_l), pl.BlockSpec(memory_space=pltpu.VMEM)),
        input_output_aliases={i: 2 + i for i in range(n_s + n_l)},
        compiler_params=pltpu.CompilerParams(has_side_effects=DATAFLOW_EFFECT),
    )(*arrays)
    return outs[0], outs[1], list(outs[2:2 + n_s]), list(outs[2 + n_s:2 + n_s + n_l]), outs[-1]


def _split_wait(name, started, after, plan):
    send_sems, recv_sems, sources, landings, _ = started
    n_s, n_l = len(sources), len(landings)

    def body(*refs):
        src, land = refs[:n_s], refs[n_s:n_s + n_l]
        send_sems_ref, recv_sems_ref = refs[n_s + n_l], refs[n_s + n_l + 1]
        for k, (s, d, peer) in enumerate(plan(src, land, _position())):
            cp = _remote(s, d, send_sems_ref, recv_sems_ref, k, peer)
            cp.wait_send()
            cp.wait_recv()

    outs = pl.pallas_call(
        body, name=name, out_shape=tuple(pltpu.HBM(a.shape, a.dtype) for a in (*sources, *landings)),
        in_specs=[*[HBM] * (n_s + n_l), SEM, SEM, ANY], out_specs=tuple([HBM] * (n_s + n_l)),
        input_output_aliases={i: i for i in range(n_s + n_l)},
        compiler_params=pltpu.CompilerParams(has_side_effects=DATAFLOW_EFFECT),
    )(*sources, *landings, send_sems, recv_sems, after)
    return list(outs[:n_s]), list(outs[n_s:])


def _exchange(name, inputs, out_shapes, plan, in_place=False):
    n_in = len(inputs)
    n_out = n_in if in_place else len(out_shapes)
    n_copies = len(plan([None] * n_in, [None] * n_out, None))

    def body(*refs):
        ins, outs = refs[:n_in], refs[n_in:n_in + n_out]
        send_sems, recv_sems = refs[n_in + n_out:]
        copies = [_remote(s, d, send_sems, recv_sems, k, peer)
                  for k, (s, d, peer) in enumerate(plan(ins, ins if in_place else outs, _position()))]
        for cp in copies:
            cp.start()
        for cp in copies:
            cp.wait()

    if in_place:
        out_shapes = [jax.ShapeDtypeStruct(a.shape, a.dtype) for a in inputs]
    return pl.pallas_call(
        body, name=name, in_specs=[ANY] * n_in, out_specs=[ANY] * n_out, out_shape=out_shapes,
        input_output_aliases={i: i for i in range(n_in)} if in_place else {},
        scratch_shapes=[pltpu.SemaphoreType.DMA((n_copies,)), pltpu.SemaphoreType.DMA((n_copies,))],
    )(*inputs)


def _half_rows(ref, half):
    n = ref.shape[0] // 2
    return ref.at[pl.ds(half * n, n)]


def _gather_plan(sources, landings, pos):
    if pos is None:
        return [None] * (3 * len(sources))
    x, y, c = pos
    copies = []
    for src, land in zip(sources, landings):
        for qx, qy in _other_chips(x, y):
            copies.append((_half_rows(src, c), _half_rows(land.at[2 * x + y], c), (qx, qy, c)))
    return copies


def _sibling_plan(landings, _, pos):
    if pos is None:
        return [None] * (3 * len(landings))
    x, y, c = pos
    copies = []
    for land in landings:
        for qx, qy in _other_chips(x, y):
            region = _half_rows(land.at[2 * qx + qy], c)
            copies.append((region, region, (x, y, 1 - c)))
    return copies


def _weights_start(tag, shards):
    landings = [lax.empty((4, *s.shape), BF16) for s in shards]
    return _split_start(f"weights_start_{tag}", shards, landings, _gather_plan)


def _weights_finish(tag, names, started, after):
    shards, landings = _split_wait(f"weights_wait_{tag}", started, after, _gather_plan)
    landings = _exchange("weights_to_sibling", landings, None, _sibling_plan, in_place=True)
    my_shard = 2 * lax.axis_index("x") + lax.axis_index("y")
    full = {}
    for name, shard, land in zip(names, shards, landings):
        sa = BIG_AXES[name][0]
        land = lax.dynamic_update_index_in_dim(land, shard, my_shard, axis=0)
        if name in BY_CHIP:
            full[name] = land
            continue
        axis = sa - 1
        moved = jnp.moveaxis(land, 0, axis)
        full[name] = moved.reshape(*moved.shape[:axis], 4 * moved.shape[axis + 1], *moved.shape[axis + 2:])
    return full


def _half_shape(shape, half_axis):
    return tuple(n // 2 if a == half_axis else n for a, n in enumerate(shape))


def _shard_shape(shape, shard_axis):
    return tuple(n // 4 if a == shard_axis else n for a, n in enumerate(shape))


def _flat2d(a):
    return a.reshape(-1, a.shape[-1])


ELEMENTWISE_TILE_ELEMENTS = 1 << 18


def _row_tile(rows, cols):
    tile = 1
    while rows % (2 * tile) == 0 and 2 * tile * cols <= ELEMENTWISE_TILE_ELEMENTS:
        tile *= 2
    assert tile >= 16 or tile == rows, (rows, cols)
    return tile


def _window_of(arr, axis, parts, index):
    flat = _flat2d(arr)
    rows, cols = flat.shape
    if axis == arr.ndim - 1:
        return flat, (rows, cols // parts), (0, index)
    assert axis == 0, axis
    return flat, (rows // parts, cols), (index, 0)


def _window_sum(name, source, window, index, others, out_dtype):
    w_rows, w_cols = window
    tm = _row_tile(w_rows, w_cols)
    blocks = w_rows // tm
    where = jnp.stack([jnp.asarray(i, jnp.int32) for i in index])

    def body(where_ref, src_ref, *refs):
        acc = src_ref[...].astype(F32)
        for r in refs[:-1]:
            acc = acc + r[...].astype(F32)
        refs[-1][...] = acc.astype(refs[-1].dtype)

    same = pl.BlockSpec((tm, w_cols), lambda i, where_ref: (i, 0))
    grid_spec = pltpu.PrefetchScalarGridSpec(
        num_scalar_prefetch=1, grid=(blocks,),
        in_specs=[pl.BlockSpec((tm, w_cols), lambda i, where_ref: (where_ref[0] * blocks + i, where_ref[1])), *[same] * len(others)],
        out_specs=same)
    return pl.pallas_call(body, name=name, grid_spec=grid_spec, out_shape=jax.ShapeDtypeStruct(window, out_dtype),
                          compiler_params=_params("arbitrary"))(where, source, *others)


def _grads_start(tag, names, grads):
    c = lax.axis_index("c")
    shapes = [g.shape for g in grads]
    axes = [(BIG_AXES[n][0] - 1, BIG_AXES[n][1] - 1) for n in names]

    def to_sibling(ins, outs, pos):
        if pos is None:
            return [None] * len(names)
        px, py, pc = pos
        return [(_region(ins[wi], _cuts(shapes[wi], sa, ha, half=1 - pc)), outs[wi], (px, py, 1 - pc))
                for wi, (sa, ha) in enumerate(axes)]

    landed = _exchange("grads_to_sibling", grads, [jax.ShapeDtypeStruct(_half_shape(s, ha), BF16) for s, (_, ha) in zip(shapes, axes)],
                       to_sibling)
    chip_sums = []
    for wi, name in enumerate(names):
        source, window, index = _window_of(grads[wi], axes[wi][1], 2, c)
        chip_sum = _window_sum("chip_sum_" + name, source, window, index, [_flat2d(landed[wi])], BF16)
        chip_sums.append(chip_sum.reshape(landed[wi].shape))

    def to_chips(sources, landings, pos):
        if pos is None:
            return [None] * (3 * len(names))
        px, py, pc = pos
        copies = []
        for wi, (sa, ha) in enumerate(axes):
            for j, (qx, qy) in enumerate(_other_chips(px, py)):
                piece = _region(sources[wi], _cuts(shapes[wi], sa, ha, shard=2 * qx + qy))
                copies.append((piece, landings[wi].at[j], (qx, qy, pc)))
        return copies

    landings = [lax.empty((3, *_shard_shape(t.shape, sa)), BF16) for t, (sa, _) in zip(chip_sums, axes)]
    return _split_start(f"grads_start_{tag}", chip_sums, landings, to_chips), to_chips


def _grads_finish(tag, names, started, after):
    chip_sums, pieces = _split_wait(f"grads_wait_{tag}", started[0], after, started[1])
    my_shard = 2 * lax.axis_index("x") + lax.axis_index("y")
    reduced = {}
    for name, chip_sum, piece in zip(names, chip_sums, pieces):
        source, window, index = _window_of(chip_sum, BIG_AXES[name][0] - 1, 4, my_shard)
        total = _window_sum("shard_sum_" + name, source, window, index, [_flat2d(piece[j]) for j in range(3)], F32)
        reduced[name] = total.reshape(piece.shape[1:])
    return reduced


def _join_halves(halves, names):
    c = lax.axis_index("c")
    mine = [halves[l][name] for l in range(len(halves)) for name in names]

    def halves_to_sibling(ins, outs, pos):
        if pos is None:
            return [None] * len(mine)
        px, py, pc = pos
        return [(ins[k], outs[k], (px, py, 1 - pc)) for k in range(len(mine))]

    theirs = _exchange("grad_halves_to_sibling", mine, [jax.ShapeDtypeStruct(r.shape, F32) for r in mine], halves_to_sibling)
    out = []
    for wi, name in enumerate(names):
        ha = BIG_AXES[name][1]
        layers = []
        for l in range(len(halves)):
            a, b = mine[l * len(names) + wi], theirs[l * len(names) + wi]
            layers.append(jnp.concatenate([jnp.where(c == 0, a, b), jnp.where(c == 0, b, a)], axis=ha - 1))
        out.append(jnp.stack(layers))
    return out


def _pack(arrays):
    flat = jnp.concatenate([a.reshape(-1).astype(F32) for a in arrays])
    rows = -(-flat.shape[0] // PACK_LANES)
    rows = -(-rows // PACK_ROW_MULTIPLE) * PACK_ROW_MULTIPLE
    return jnp.pad(flat, (0, rows * PACK_LANES - flat.shape[0])).reshape(rows, PACK_LANES)


def _unpack(buf, shapes):
    flat, out, off = buf.reshape(-1), [], 0
    for s in shapes:
        n = 1
        for d in s:
            n *= d
        out.append(flat[off:off + n].reshape(s))
        off += n
    return out


def _sum8(name, gathered, rows):
    def body(in_ref, out_ref):
        acc = in_ref[0].astype(F32)
        for d in range(1, 8):
            acc = acc + in_ref[d].astype(F32)
        out_ref[...] = acc

    return pl.pallas_call(
        body, name=name, grid=(rows // PACK_ROW_MULTIPLE,),
        in_specs=[pl.BlockSpec((8, PACK_ROW_MULTIPLE, PACK_LANES), lambda i: (0, i, 0))],
        out_specs=pl.BlockSpec((PACK_ROW_MULTIPLE, PACK_LANES), lambda i: (i, 0)),
        out_shape=jax.ShapeDtypeStruct((rows, PACK_LANES), F32), compiler_params=_params("arbitrary"),
    )(gathered.reshape(8, rows, PACK_LANES))


def _adamw(name, w, g, m, v):
    shape = w.shape
    flat = [_flat2d(a) for a in (w, g, m, v)]
    rows, cols = flat[0].shape
    tm = _row_tile(rows, cols)
    outs = _rowwise(name, _adamw_fn, [(a, cols, 0) for a in flat], [], [(cols, F32)] * 3, tm=tm)
    return [o.reshape(shape) for o in outs]


WEIGHT_ORDER = ("norm_mix", "w_in", "conf_dw", "conf_dw_b", "conf_ln_g", "conf_ln_b", "pool_w", "pool_scale", "sc_conv",
                "gmlp_ln_g", "gmlp_ln_b", "gmlp_ws", "gmlp_bs", "w_branch", "w_out", "norm_mlp", "w_up", "w_down",
                "norm_ple", "w_ple", "w_ple_gate", "norm_final")


def _local_step(x, p, loss_target, weights, layer_big, full_small_sharded, on_big_grads):
    n_layers = p.shape[0]
    d_model = x.shape[1]
    layer_w, saved = [], []
    h = x
    for l in range(n_layers):
        w = {n: weights[n][l] for n in SMALL_REP if n != "norm_final"}
        w.update({n: full_small_sharded[n][l] for n in SMALL_SHARDED})
        layer_w.append(w)
        h, s = _layer_fwd(h, p[l], w, layer_big(l))
        saved.append(s)
    dx, d_final, loss = _rowwise("loss", _loss_fn, [(h, d_model, 0), (loss_target, d_model, 0)], [_row(weights["norm_final"])],
                                 [(d_model, F32)], [(1, d_model), (1, PACK_LANES)])
    small_grads = {n: [None] * n_layers for n in SMALL_REP + SMALL_SHARDED if n != "norm_final"}
    for l in reversed(range(n_layers)):
        dx, small = _layer_bwd(dx, p[l], layer_w[l], saved[l], on_big_grads(l))
        for n in small:
            small_grads[n][l] = small[n].reshape(weights[n].shape[1:]) if n in SMALL_REP else small[n]
    small_grads = {n: jnp.stack(v) for n, v in small_grads.items()}
    small_grads["norm_final"] = d_final.reshape(-1)
    return loss, dx, small_grads


def kernel(x, p, norm_mix, w_in, conf_dw, conf_dw_b, conf_ln_g, conf_ln_b, pool_w, pool_scale, sc_conv, gmlp_ln_g, gmlp_ln_b, gmlp_ws, gmlp_bs, w_branch, w_out, norm_mlp, w_up, w_down, norm_ple, w_ple, w_ple_gate, norm_final, loss_target, m_norm_mix, m_w_in, m_conf_dw, m_conf_dw_b, m_conf_ln_g, m_conf_ln_b, m_pool_w, m_pool_scale, m_sc_conv, m_gmlp_ln_g, m_gmlp_ln_b, m_gmlp_ws, m_gmlp_bs, m_w_branch, m_w_out, m_norm_mlp, m_w_up, m_w_down, m_norm_ple, m_w_ple, m_w_ple_gate, m_norm_final, v_norm_mix, v_w_in, v_conf_dw, v_conf_dw_b, v_conf_ln_g, v_conf_ln_b, v_pool_w, v_pool_scale, v_sc_conv, v_gmlp_ln_g, v_gmlp_ln_b, v_gmlp_ws, v_gmlp_bs, v_w_branch, v_w_out, v_norm_mlp, v_w_up, v_w_down, v_norm_ple, v_w_ple, v_w_ple_gate, v_norm_final):
    given = dict(locals())
    weights = {n: given[n] for n in WEIGHT_ORDER}
    mom_m = {n: given["m_" + n] for n in WEIGHT_ORDER}
    mom_v = {n: given["v_" + n] for n in WEIGHT_ORDER}
    my_shard = 2 * lax.axis_index("x") + lax.axis_index("y")

    n_layers = p.shape[0]
    shard_shapes = [weights[n].shape for n in SMALL_SHARDED]
    gathered = _allgather_small("allgather_small_weights", _pack([weights[n] for n in SMALL_SHARDED]))
    after_small = gathered[0, 0] * 0.0
    after_small = jnp.where(after_small == 0.0, after_small, 0.0)

    weight_copies = [[None] * len(WEIGHT_GROUPS) for _ in range(n_layers)]
    started = after_small
    for l in range(n_layers):
        for g, names in enumerate(WEIGHT_GROUPS):
            shards = [(weights[n][l] + started if i == 0 else weights[n][l]).astype(BF16) for i, n in enumerate(names)]
            weight_copies[l][g] = _weights_start(f"{l}_{g}", shards)
            started = weight_copies[l][g][4][0, 0]
    per_dev = gathered.reshape(8, -1, PACK_LANES)
    full_small_sharded = {}
    chip_parts = [_unpack(per_dev[2 * s], shard_shapes) for s in range(4)]
    for i, n in enumerate(SMALL_SHARDED):
        full_small_sharded[n] = jnp.concatenate([chip_parts[s][i] for s in range(4)], axis=-1)

    grad_copies = [[None] * len(GRAD_GROUPS) for _ in range(n_layers)]

    def layer_big(l):
        return lambda g, after: _weights_finish(f"{l}_{g}", WEIGHT_GROUPS[g], weight_copies[l][g], after)

    def on_big_grads(l):
        def send(g, group_grads):
            grad_copies[l][g] = _grads_start(f"{l}_{g}", GRAD_GROUPS[g], group_grads)
            return grad_copies[l][g][0][4][0, 0]
        return send

    loss_row, dx, small_grads = _local_step(x[0] + started, p[:, 0], loss_target[0], weights, layer_big, full_small_sharded,
                                            on_big_grads)
    loss = lax.psum(loss_row[0, 0], MESH_AXES)

    grads, delta, new_m, new_v = {}, {}, {}, {}
    halves = [{} for _ in range(n_layers)]
    last_sent = (0, len(GRAD_GROUPS) - 1)
    for l in reversed(range(n_layers)):
        for g, names in enumerate(GRAD_GROUPS):
            if (l, g) != last_sent:
                halves[l].update(_grads_finish(f"{l}_{g}", names, grad_copies[l][g], dx))
    behind = dx
    for g, names in enumerate(GRAD_GROUPS):
        if g == last_sent[1]:
            halves[0].update(_grads_finish(f"0_{g}", names, grad_copies[0][g], behind))
        for n, joined in zip(names, _join_halves(halves, names)):
            grads[n] = joined
            delta[n], new_m[n], new_v[n] = _adamw("adamw_" + n, weights[n], joined, mom_m[n], mom_v[n])
            behind = delta[n]

    small_names = SMALL_REP + SMALL_SHARDED
    after_big = behind[(0,) * behind.ndim] * 0.0
    after_big = jnp.where(after_big == 0.0, after_big, 0.0)
    packed = _pack([small_grads[n] for n in small_names]) + after_big
    rows = packed.shape[0]
    summed = _sum8("sum_small_grads", _allgather_small("allgather_small_grads", packed.astype(BF16)), rows)
    small_full_shapes = [weights[n].shape for n in SMALL_REP] + [full_small_sharded[n].shape for n in SMALL_SHARDED]
    for n, g in zip(small_names, _unpack(summed, small_full_shapes)):
        if n in SMALL_SHARDED:
            width = weights[n].shape[-1]
            g = lax.dynamic_slice_in_dim(g, my_shard * width, width, axis=g.ndim - 1)
        grads[n] = g

    for group, names in (("rep", SMALL_REP), ("sharded", SMALL_SHARDED)):
        outs = _adamw("adamw_small_" + group, *[_pack([src[n] for n in names]) for src in (weights, grads, mom_m, mom_v)])
        shapes = [weights[n].shape for n in names]
        for dst, buf in zip((delta, new_m, new_v), outs):
            for n, a in zip(names, _unpack(buf, shapes)):
                dst[n] = a

    return (loss, dx[None], *[grads[n] for n in WEIGHT_ORDER], *[delta[n] for n in WEIGHT_ORDER],
            *[new_m[n] for n in WEIGHT_ORDER], *[new_v[n] for n in WEIGHT_ORDER])
```
